```python
import math
import jax, jax.numpy as jnp
from jax import lax
import numpy as np

D_MODEL = 1024
BATCH = 8
SEQ = 16384
DEPTH = 4

D_FF = 2816
HEAD_DIM = 64
N_Q_HEADS = 16
N_KV_HEADS = 2
Q_PER_KV = N_Q_HEADS // N_KV_HEADS
ATTN_WIDTH = N_Q_HEADS * HEAD_DIM
KV_WIDTH = N_KV_HEADS * HEAD_DIM
WINDOW = 128
ATTN_BLOCK = 128
SGU_CHUNK = 128
SGU_GROUPS = 8
SGU_GROUP_CH = 128
SGU_WIDTH = SGU_GROUPS * SGU_GROUP_CH
IN_SPLIT_SIZES = (ATTN_WIDTH, KV_WIDTH, KV_WIDTH, SGU_WIDTH, SGU_WIDTH, D_MODEL, D_MODEL)
IN_WIDTH = sum(IN_SPLIT_SIZES)
IN_SPLIT_POINTS = tuple(int(p) for p in np.cumsum(IN_SPLIT_SIZES)[:-1])

RMS_EPS = 1e-6
LN_EPS = 1e-5
MASK_VALUE = -1e30

kernel_name = "hybrid_swa_sink_gmlp_macaron_sandwich"


def rms_norm(x, g):
    x32 = x.astype(jnp.float32)
    y = x32 * lax.rsqrt(jnp.mean(x32 * x32, axis=-1, keepdims=True) + RMS_EPS)
    return (y * g.astype(jnp.float32)).astype(x.dtype)


def layer_norm(x, g, b):
    x32 = x.astype(jnp.float32)
    mu = jnp.mean(x32, axis=-1, keepdims=True)
    xc = x32 - mu
    y = xc * lax.rsqrt(jnp.mean(xc * xc, axis=-1, keepdims=True) + LN_EPS)
    return (y * g.astype(jnp.float32) + b.astype(jnp.float32)).astype(x.dtype)


def swiglu(x, w1, w2):
    g, u = jnp.split(x @ w1, 2, axis=-1)
    return (jax.nn.silu(g) * u) @ w2


def sliding_window_attention(q, k, v, sinks):
    B, S, _ = q.shape
    nb = S // ATTN_BLOCK
    qb = q.reshape(B, nb, ATTN_BLOCK, N_KV_HEADS, Q_PER_KV, HEAD_DIM).astype(jnp.float32)
    kb = k.reshape(B, nb, ATTN_BLOCK, N_KV_HEADS, HEAD_DIM)
    vb = v.reshape(B, nb, ATTN_BLOCK, N_KV_HEADS, HEAD_DIM)
    kpad = jnp.zeros_like(kb[:, :1])
    vpad = jnp.zeros_like(vb[:, :1])
    k2 = jnp.concatenate([jnp.concatenate([kpad, kb[:, :-1]], axis=1), kb], axis=2).astype(jnp.float32)
    v2 = jnp.concatenate([jnp.concatenate([vpad, vb[:, :-1]], axis=1), vb], axis=2).astype(jnp.float32)
    scale = 1.0 / math.sqrt(HEAD_DIM)
    scores = jnp.einsum('bnqgrd,bnkgd->bngrqk', qb, k2) * scale
    qi = jnp.arange(ATTN_BLOCK)[:, None]
    kj = jnp.arange(2 * ATTN_BLOCK)[None, :]
    rel = qi + ATTN_BLOCK - kj
    band = (rel >= 0) & (rel < WINDOW)
    blk = jnp.arange(nb)[:, None, None]
    valid = band[None] & ((blk > 0) | (kj[None] >= ATTN_BLOCK))
    scores = jnp.where(valid[None, :, None, None], scores, MASK_VALUE)
    sink = sinks.astype(jnp.float32).reshape(1, 1, N_KV_HEADS, Q_PER_KV, 1)
    m = jnp.maximum(scores.max(axis=-1), sink)
    p = jnp.exp(scores - m[..., None])
    probs = p / (p.sum(axis=-1) + jnp.exp(sink - m))[..., None]
    out = jnp.einsum('bngrqk,bnkgd->bnqgrd', probs, v2)
    return out.reshape(B, S, ATTN_WIDTH).astype(q.dtype)


def spatial_gating(u, v, ln_g, ln_b, w_s, b_s):
    B, S, _ = u.shape
    nc = S // SGU_CHUNK
    vn = layer_norm(v, ln_g, ln_b).reshape(B, nc, SGU_CHUNK, SGU_GROUPS, SGU_GROUP_CH)
    causal = jnp.tril(jnp.ones((SGU_CHUNK, SGU_CHUNK), dtype=bool))
    w = jnp.where(causal[None], w_s, jnp.zeros_like(w_s))
    s = jnp.einsum('gts,bnsgc->bntgc', w, vn) + b_s.T[None, None, :, :, None]
    return u * s.reshape(B, S, SGU_WIDTH)


def _fwd_setup_inputs(seed: int = 0) -> dict:
    key = jax.random.key(seed)
    ks = iter(jax.random.split(key, 32))
    f32 = jnp.float32

    def nrm(shape, fan_in, scale=1.0):
        return jax.random.normal(next(ks), shape, f32) * (scale * fan_in ** -0.5)

    def gain(shape):
        return 1.0 + 0.05 * jax.random.normal(next(ks), shape, f32)

    L, D = DEPTH, D_MODEL
    return {
        "x": jax.random.normal(next(ks), (BATCH, SEQ, D), f32),
        "ffn1_pre_g": gain((L, D)),
        "ffn1_w1": nrm((L, D, 2 * D_FF), D),
        "ffn1_w2": nrm((L, D_FF, D), D_FF),
        "ffn1_post_g": gain((L, D)),
        "mix_pre_g": gain((L, D)),
        "w_in": nrm((L, D, IN_WIDTH), D),
        "attn_sinks": 0.5 * jax.random.normal(next(ks), (L, N_Q_HEADS), f32),
        "sgu_ln_g": gain((L, SGU_WIDTH)),
        "sgu_ln_b": 0.02 * jax.random.normal(next(ks), (L, SGU_WIDTH), f32),
        "sgu_w": nrm((L, SGU_GROUPS, SGU_CHUNK, SGU_CHUNK), SGU_CHUNK, 0.5),
        "sgu_b": gain((L, SGU_GROUPS, SGU_CHUNK)),
        "w_attn_branch": nrm((L, ATTN_WIDTH, D), ATTN_WIDTH),
        "w_sgu_branch": nrm((L, SGU_WIDTH, D), SGU_WIDTH),
        "w_out": nrm((L, D, D), D),
        "mix_post_g": gain((L, D)),
        "ffn2_pre_g": gain((L, D)),
        "ffn2_w1": nrm((L, D, 2 * D_FF), D),
        "ffn2_w2": nrm((L, D_FF, D), D_FF),
        "ffn2_post_g": gain((L, D)),
    }


def _fwd_reference(x, ffn1_pre_g, ffn1_w1, ffn1_w2, ffn1_post_g, mix_pre_g, w_in, attn_sinks,
              sgu_ln_g, sgu_ln_b, sgu_w, sgu_b, w_attn_branch, w_sgu_branch, w_out, mix_post_g,
              ffn2_pre_g, ffn2_w1, ffn2_w2, ffn2_post_g):
    for l in range(DEPTH):
        h = rms_norm(x, ffn1_pre_g[l])
        x = x + 0.5 * rms_norm(swiglu(h, ffn1_w1[l], ffn1_w2[l]), ffn1_post_g[l])

        h = rms_norm(x, mix_pre_g[l])
        z = h @ w_in[l]
        q, k, v, u_s, v_s, g_a, g_b = jnp.split(z, IN_SPLIT_POINTS, axis=-1)
        y_attn = sliding_window_attention(q, k, v, attn_sinks[l])
        y_sgu = spatial_gating(jax.nn.gelu(u_s, approximate=False), jax.nn.gelu(v_s, approximate=False),
                               sgu_ln_g[l], sgu_ln_b[l], sgu_w[l], sgu_b[l])
        merged = (jax.nn.sigmoid(g_a) * (y_attn @ w_attn_branch[l])
                  + jax.nn.sigmoid(g_b) * (y_sgu @ w_sgu_branch[l]))
        x = x + rms_norm(merged @ w_out[l], mix_post_g[l])

        h = rms_norm(x, ffn2_pre_g[l])
        x = x + 0.5 * rms_norm(swiglu(h, ffn2_w1[l], ffn2_w2[l]), ffn2_post_g[l])
    return x


import jax as _jax
import jax.numpy as _jnp

TWIN_FORMAT = 'train_step'
FWD_PARAMS = ['x', 'ffn1_pre_g', 'ffn1_w1', 'ffn1_w2', 'ffn1_post_g', 'mix_pre_g', 'w_in', 'attn_sinks', 'sgu_ln_g', 'sgu_ln_b', 'sgu_w', 'sgu_b', 'w_attn_branch', 'w_sgu_branch', 'w_out', 'mix_post_g', 'ffn2_pre_g', 'ffn2_w1', 'ffn2_w2', 'ffn2_post_g']
TWIN_WEIGHTS = ['ffn1_pre_g', 'ffn1_w1', 'ffn1_w2', 'ffn1_post_g', 'mix_pre_g', 'w_in', 'attn_sinks', 'sgu_ln_g', 'sgu_ln_b', 'sgu_w', 'sgu_b', 'w_attn_branch', 'w_sgu_branch', 'w_out', 'mix_post_g', 'ffn2_pre_g', 'ffn2_w1', 'ffn2_w2', 'ffn2_post_g']
TWIN_DIFF_INPUT = 'x'
TWIN_INPUTS = ['x', 'ffn1_pre_g', 'ffn1_w1', 'ffn1_w2', 'ffn1_post_g', 'mix_pre_g', 'w_in', 'attn_sinks', 'sgu_ln_g', 'sgu_ln_b', 'sgu_w', 'sgu_b', 'w_attn_branch', 'w_sgu_branch', 'w_out', 'mix_post_g', 'ffn2_pre_g', 'ffn2_w1', 'ffn2_w2', 'ffn2_post_g', 'loss_target', 'm_ffn1_pre_g', 'm_ffn1_w1', 'm_ffn1_w2', 'm_ffn1_post_g', 'm_mix_pre_g', 'm_w_in', 'm_attn_sinks', 'm_sgu_ln_g', 'm_sgu_ln_b', 'm_sgu_w', 'm_sgu_b', 'm_w_attn_branch', 'm_w_sgu_branch', 'm_w_out', 'm_mix_post_g', 'm_ffn2_pre_g', 'm_ffn2_w1', 'm_ffn2_w2', 'm_ffn2_post_g', 'v_ffn1_pre_g', 'v_ffn1_w1', 'v_ffn1_w2', 'v_ffn1_post_g', 'v_mix_pre_g', 'v_w_in', 'v_attn_sinks', 'v_sgu_ln_g', 'v_sgu_ln_b', 'v_sgu_w', 'v_sgu_b', 'v_w_attn_branch', 'v_w_sgu_branch', 'v_w_out', 'v_mix_post_g', 'v_ffn2_pre_g', 'v_ffn2_w1', 'v_ffn2_w2', 'v_ffn2_post_g']
TWIN_OUTPUTS = ['loss', 'grad_x', 'grad_ffn1_pre_g', 'grad_ffn1_w1', 'grad_ffn1_w2', 'grad_ffn1_post_g', 'grad_mix_pre_g', 'grad_w_in', 'grad_attn_sinks', 'grad_sgu_ln_g', 'grad_sgu_ln_b', 'grad_sgu_w', 'grad_sgu_b', 'grad_w_attn_branch', 'grad_w_sgu_branch', 'grad_w_out', 'grad_mix_post_g', 'grad_ffn2_pre_g', 'grad_ffn2_w1', 'grad_ffn2_w2', 'grad_ffn2_post_g', 'delta_ffn1_pre_g', 'delta_ffn1_w1', 'delta_ffn1_w2', 'delta_ffn1_post_g', 'delta_mix_pre_g', 'delta_w_in', 'delta_attn_sinks', 'delta_sgu_ln_g', 'delta_sgu_ln_b', 'delta_sgu_w', 'delta_sgu_b', 'delta_w_attn_branch', 'delta_w_sgu_branch', 'delta_w_out', 'delta_mix_post_g', 'delta_ffn2_pre_g', 'delta_ffn2_w1', 'delta_ffn2_w2', 'delta_ffn2_post_g', 'new_m_ffn1_pre_g', 'new_m_ffn1_w1', 'new_m_ffn1_w2', 'new_m_ffn1_post_g', 'new_m_mix_pre_g', 'new_m_w_in', 'new_m_attn_sinks', 'new_m_sgu_ln_g', 'new_m_sgu_ln_b', 'new_m_sgu_w', 'new_m_sgu_b', 'new_m_w_attn_branch', 'new_m_w_sgu_branch', 'new_m_w_out', 'new_m_mix_post_g', 'new_m_ffn2_pre_g', 'new_m_ffn2_w1', 'new_m_ffn2_w2', 'new_m_ffn2_post_g', 'new_v_ffn1_pre_g', 'new_v_ffn1_w1', 'new_v_ffn1_w2', 'new_v_ffn1_post_g', 'new_v_mix_pre_g', 'new_v_w_in', 'new_v_attn_sinks', 'new_v_sgu_ln_g', 'new_v_sgu_ln_b', 'new_v_sgu_w', 'new_v_sgu_b', 'new_v_w_attn_branch', 'new_v_w_sgu_branch', 'new_v_w_out', 'new_v_mix_post_g', 'new_v_ffn2_pre_g', 'new_v_ffn2_w1', 'new_v_ffn2_w2', 'new_v_ffn2_post_g']
TWIN_LEAF_KINDS = {'loss': 'loss', 'grad_x': 'grad_x', 'grad_ffn1_pre_g': 'grad_w', 'grad_ffn1_w1': 'grad_w', 'grad_ffn1_w2': 'grad_w', 'grad_ffn1_post_g': 'grad_w', 'grad_mix_pre_g': 'grad_w', 'grad_w_in': 'grad_w', 'grad_attn_sinks': 'grad_w', 'grad_sgu_ln_g': 'grad_w', 'grad_sgu_ln_b': 'grad_w', 'grad_sgu_w': 'grad_w', 'grad_sgu_b': 'grad_w', 'grad_w_attn_branch': 'grad_w', 'grad_w_sgu_branch': 'grad_w', 'grad_w_out': 'grad_w', 'grad_mix_post_g': 'grad_w', 'grad_ffn2_pre_g': 'grad_w', 'grad_ffn2_w1': 'grad_w', 'grad_ffn2_w2': 'grad_w', 'grad_ffn2_post_g': 'grad_w', 'delta_ffn1_pre_g': 'delta_w', 'delta_ffn1_w1': 'delta_w', 'delta_ffn1_w2': 'delta_w', 'delta_ffn1_post_g': 'delta_w', 'delta_mix_pre_g': 'delta_w', 'delta_w_in': 'delta_w', 'delta_attn_sinks': 'delta_w', 'delta_sgu_ln_g': 'delta_w', 'delta_sgu_ln_b': 'delta_w', 'delta_sgu_w': 'delta_w', 'delta_sgu_b': 'delta_w', 'delta_w_attn_branch': 'delta_w', 'delta_w_sgu_branch': 'delta_w', 'delta_w_out': 'delta_w', 'delta_mix_post_g': 'delta_w', 'delta_ffn2_pre_g': 'delta_w', 'delta_ffn2_w1': 'delta_w', 'delta_ffn2_w2': 'delta_w', 'delta_ffn2_post_g': 'delta_w', 'new_m_ffn1_pre_g': 'new_m', 'new_m_ffn1_w1': 'new_m', 'new_m_ffn1_w2': 'new_m', 'new_m_ffn1_post_g': 'new_m', 'new_m_mix_pre_g': 'new_m', 'new_m_w_in': 'new_m', 'new_m_attn_sinks': 'new_m', 'new_m_sgu_ln_g': 'new_m', 'new_m_sgu_ln_b': 'new_m', 'new_m_sgu_w': 'new_m', 'new_m_sgu_b': 'new_m', 'new_m_w_attn_branch': 'new_m', 'new_m_w_sgu_branch': 'new_m', 'new_m_w_out': 'new_m', 'new_m_mix_post_g': 'new_m', 'new_m_ffn2_pre_g': 'new_m', 'new_m_ffn2_w1': 'new_m', 'new_m_ffn2_w2': 'new_m', 'new_m_ffn2_post_g': 'new_m', 'new_v_ffn1_pre_g': 'new_v', 'new_v_ffn1_w1': 'new_v', 'new_v_ffn1_w2': 'new_v', 'new_v_ffn1_post_g': 'new_v', 'new_v_mix_pre_g': 'new_v', 'new_v_w_in': 'new_v', 'new_v_attn_sinks': 'new_v', 'new_v_sgu_ln_g': 'new_v', 'new_v_sgu_ln_b': 'new_v', 'new_v_sgu_w': 'new_v', 'new_v_sgu_b': 'new_v', 'new_v_w_attn_branch': 'new_v', 'new_v_w_sgu_branch': 'new_v', 'new_v_w_out': 'new_v', 'new_v_mix_post_g': 'new_v', 'new_v_ffn2_pre_g': 'new_v', 'new_v_ffn2_w1': 'new_v', 'new_v_ffn2_w2': 'new_v', 'new_v_ffn2_post_g': 'new_v'}


def _forward(args):
    return _fwd_reference(*[args[k] for k in FWD_PARAMS])


def _output_shape():
    def fwd():
        inp = _fwd_setup_inputs(0)
        return _fwd_reference(*[inp[k] for k in FWD_PARAMS])
    out = _jax.eval_shape(fwd)
    return out.shape, out.dtype

N_MICROBATCH = 1
ADAM_LR = 0.001
ADAM_B1 = 0.9
ADAM_B2 = 0.999
ADAM_EPS = 1e-08
ADAM_WD = 0.01
ADAM_STEP = 10
PER_EXAMPLE_BATCH_AXIS = {'x': 0, 'loss_target': 0}
SHARED_INPUTS = []
_WEIGHT_DTYPES = {'ffn1_pre_g': _jnp.float32, 'ffn1_w1': _jnp.float32, 'ffn1_w2': _jnp.float32, 'ffn1_post_g': _jnp.float32, 'mix_pre_g': _jnp.float32, 'w_in': _jnp.float32, 'attn_sinks': _jnp.float32, 'sgu_ln_g': _jnp.float32, 'sgu_ln_b': _jnp.float32, 'sgu_w': _jnp.float32, 'sgu_b': _jnp.float32, 'w_attn_branch': _jnp.float32, 'w_sgu_branch': _jnp.float32, 'w_out': _jnp.float32, 'mix_post_g': _jnp.float32, 'ffn2_pre_g': _jnp.float32, 'ffn2_w1': _jnp.float32, 'ffn2_w2': _jnp.float32, 'ffn2_post_g': _jnp.float32}
MOMENT_SCALE = {'ffn1_pre_g': 8.493272e+00, 'ffn1_w1': 3.473554e+00, 'ffn1_w2': 7.508946e+00, 'ffn1_post_g': 3.226286e+01, 'mix_pre_g': 4.109513e+01, 'w_in': 1.811329e+01, 'attn_sinks': 8.272731e-01, 'sgu_ln_g': 5.560813e-01, 'sgu_ln_b': 9.423988e-01, 'sgu_w': 1.064810e+00, 'sgu_b': 2.283076e+00, 'w_attn_branch': 3.473101e+01, 'w_sgu_branch': 5.267112e+01, 'w_out': 6.475811e+01, 'mix_post_g': 1.474388e+02, 'ffn2_pre_g': 1.240844e+01, 'ffn2_w1': 5.239472e+00, 'ffn2_w2': 1.056083e+01, 'ffn2_post_g': 3.417587e+01}


def _to_microbatches(a, axis):
    t = _jnp.moveaxis(a, axis, 0)
    t = t.reshape((N_MICROBATCH, t.shape[0] // N_MICROBATCH) + t.shape[1:])
    return _jnp.moveaxis(t, 1, axis + 1)


def setup_inputs(seed: int = 0) -> dict:
    inp = _fwd_setup_inputs(seed)
    key = _jax.random.fold_in(_jax.random.key(seed), 7919)
    shape, _ = _output_shape()
    out = dict(inp)
    out["loss_target"] = _jax.random.normal(_jax.random.fold_in(key, 0), shape, _jnp.float32)
    for i, name in enumerate(TWIN_WEIGHTS):
        w = inp[name].astype(_jnp.float32)
        if MOMENT_SCALE is None:
            s = _jnp.sqrt(_jnp.mean(_jnp.square(w)) + 1e-30)
        else:
            s = MOMENT_SCALE[name]
        km, kv = _jax.random.split(_jax.random.fold_in(key, i + 1))
        out[name] = w
        out["m_" + name] = s * _jax.random.normal(km, w.shape, _jnp.float32)
        out["v_" + name] = (s * s) * _jax.random.uniform(kv, w.shape, _jnp.float32, 0.5, 1.5)
    if N_MICROBATCH > 1:
        for name, axis in PER_EXAMPLE_BATCH_AXIS.items():
            out[name] = _to_microbatches(out[name], axis)
    return {'x': out['x'], 'ffn1_pre_g': out['ffn1_pre_g'], 'ffn1_w1': out['ffn1_w1'], 'ffn1_w2': out['ffn1_w2'], 'ffn1_post_g': out['ffn1_post_g'], 'mix_pre_g': out['mix_pre_g'], 'w_in': out['w_in'], 'attn_sinks': out['attn_sinks'], 'sgu_ln_g': out['sgu_ln_g'], 'sgu_ln_b': out['sgu_ln_b'], 'sgu_w': out['sgu_w'], 'sgu_b': out['sgu_b'], 'w_attn_branch': out['w_attn_branch'], 'w_sgu_branch': out['w_sgu_branch'], 'w_out': out['w_out'], 'mix_post_g': out['mix_post_g'], 'ffn2_pre_g': out['ffn2_pre_g'], 'ffn2_w1': out['ffn2_w1'], 'ffn2_w2': out['ffn2_w2'], 'ffn2_post_g': out['ffn2_post_g'], 'loss_target': out['loss_target'], 'm_ffn1_pre_g': out['m_ffn1_pre_g'], 'm_ffn1_w1': out['m_ffn1_w1'], 'm_ffn1_w2': out['m_ffn1_w2'], 'm_ffn1_post_g': out['m_ffn1_post_g'], 'm_mix_pre_g': out['m_mix_pre_g'], 'm_w_in': out['m_w_in'], 'm_attn_sinks': out['m_attn_sinks'], 'm_sgu_ln_g': out['m_sgu_ln_g'], 'm_sgu_ln_b': out['m_sgu_ln_b'], 'm_sgu_w': out['m_sgu_w'], 'm_sgu_b': out['m_sgu_b'], 'm_w_attn_branch': out['m_w_attn_branch'], 'm_w_sgu_branch': out['m_w_sgu_branch'], 'm_w_out': out['m_w_out'], 'm_mix_post_g': out['m_mix_post_g'], 'm_ffn2_pre_g': out['m_ffn2_pre_g'], 'm_ffn2_w1': out['m_ffn2_w1'], 'm_ffn2_w2': out['m_ffn2_w2'], 'm_ffn2_post_g': out['m_ffn2_post_g'], 'v_ffn1_pre_g': out['v_ffn1_pre_g'], 'v_ffn1_w1': out['v_ffn1_w1'], 'v_ffn1_w2': out['v_ffn1_w2'], 'v_ffn1_post_g': out['v_ffn1_post_g'], 'v_mix_pre_g': out['v_mix_pre_g'], 'v_w_in': out['v_w_in'], 'v_attn_sinks': out['v_attn_sinks'], 'v_sgu_ln_g': out['v_sgu_ln_g'], 'v_sgu_ln_b': out['v_sgu_ln_b'], 'v_sgu_w': out['v_sgu_w'], 'v_sgu_b': out['v_sgu_b'], 'v_w_attn_branch': out['v_w_attn_branch'], 'v_w_sgu_branch': out['v_w_sgu_branch'], 'v_w_out': out['v_w_out'], 'v_mix_post_g': out['v_mix_post_g'], 'v_ffn2_pre_g': out['v_ffn2_pre_g'], 'v_ffn2_w1': out['v_ffn2_w1'], 'v_ffn2_w2': out['v_ffn2_w2'], 'v_ffn2_post_g': out['v_ffn2_post_g']}


def _loss(weights, diff, rest, loss_target):
    with _jax.named_scope("forward"):
        args = {**rest, TWIN_DIFF_INPUT: diff, **{k: w.astype(_WEIGHT_DTYPES[k]) for k, w in weights.items()}}
        y = _forward(args)
    with _jax.named_scope("loss_head"):
        err = _jnp.square(y.astype(_jnp.float32) - loss_target)
        return 0.5 * _jnp.sum(_jnp.mean(err, axis=-1)) if err.ndim else 0.5 * err


def _adamw(w, g, m, v):
    m = ADAM_B1 * m + (1.0 - ADAM_B1) * g
    v = ADAM_B2 * v + (1.0 - ADAM_B2) * _jnp.square(g)
    m_hat = m / (1.0 - ADAM_B1 ** ADAM_STEP)
    v_hat = v / (1.0 - ADAM_B2 ** ADAM_STEP)
    delta = -ADAM_LR * (m_hat / (_jnp.sqrt(v_hat) + ADAM_EPS) + ADAM_WD * w)
    return delta, m, v


def reference(x, ffn1_pre_g, ffn1_w1, ffn1_w2, ffn1_post_g, mix_pre_g, w_in, attn_sinks, sgu_ln_g, sgu_ln_b, sgu_w, sgu_b, w_attn_branch, w_sgu_branch, w_out, mix_post_g, ffn2_pre_g, ffn2_w1, ffn2_w2, ffn2_post_g, loss_target, m_ffn1_pre_g, m_ffn1_w1, m_ffn1_w2, m_ffn1_post_g, m_mix_pre_g, m_w_in, m_attn_sinks, m_sgu_ln_g, m_sgu_ln_b, m_sgu_w, m_sgu_b, m_w_attn_branch, m_w_sgu_branch, m_w_out, m_mix_post_g, m_ffn2_pre_g, m_ffn2_w1, m_ffn2_w2, m_ffn2_post_g, v_ffn1_pre_g, v_ffn1_w1, v_ffn1_w2, v_ffn1_post_g, v_mix_pre_g, v_w_in, v_attn_sinks, v_sgu_ln_g, v_sgu_ln_b, v_sgu_w, v_sgu_b, v_w_attn_branch, v_w_sgu_branch, v_w_out, v_mix_post_g, v_ffn2_pre_g, v_ffn2_w1, v_ffn2_w2, v_ffn2_post_g):
    given = dict(x=x, ffn1_pre_g=ffn1_pre_g, ffn1_w1=ffn1_w1, ffn1_w2=ffn1_w2, ffn1_post_g=ffn1_post_g, mix_pre_g=mix_pre_g, w_in=w_in, attn_sinks=attn_sinks, sgu_ln_g=sgu_ln_g, sgu_ln_b=sgu_ln_b, sgu_w=sgu_w, sgu_b=sgu_b, w_attn_branch=w_attn_branch, w_sgu_branch=w_sgu_branch, w_out=w_out, mix_post_g=mix_post_g, ffn2_pre_g=ffn2_pre_g, ffn2_w1=ffn2_w1, ffn2_w2=ffn2_w2, ffn2_post_g=ffn2_post_g, loss_target=loss_target, m_ffn1_pre_g=m_ffn1_pre_g, m_ffn1_w1=m_ffn1_w1, m_ffn1_w2=m_ffn1_w2, m_ffn1_post_g=m_ffn1_post_g, m_mix_pre_g=m_mix_pre_g, m_w_in=m_w_in, m_attn_sinks=m_attn_sinks, m_sgu_ln_g=m_sgu_ln_g, m_sgu_ln_b=m_sgu_ln_b, m_sgu_w=m_sgu_w, m_sgu_b=m_sgu_b, m_w_attn_branch=m_w_attn_branch, m_w_sgu_branch=m_w_sgu_branch, m_w_out=m_w_out, m_mix_post_g=m_mix_post_g, m_ffn2_pre_g=m_ffn2_pre_g, m_ffn2_w1=m_ffn2_w1, m_ffn2_w2=m_ffn2_w2, m_ffn2_post_g=m_ffn2_post_g, v_ffn1_pre_g=v_ffn1_pre_g, v_ffn1_w1=v_ffn1_w1, v_ffn1_w2=v_ffn1_w2, v_ffn1_post_g=v_ffn1_post_g, v_mix_pre_g=v_mix_pre_g, v_w_in=v_w_in, v_attn_sinks=v_attn_sinks, v_sgu_ln_g=v_sgu_ln_g, v_sgu_ln_b=v_sgu_ln_b, v_sgu_w=v_sgu_w, v_sgu_b=v_sgu_b, v_w_attn_branch=v_w_attn_branch, v_w_sgu_branch=v_w_sgu_branch, v_w_out=v_w_out, v_mix_post_g=v_mix_post_g, v_ffn2_pre_g=v_ffn2_pre_g, v_ffn2_w1=v_ffn2_w1, v_ffn2_w2=v_ffn2_w2, v_ffn2_post_g=v_ffn2_post_g)
    weights = {n: given[n] for n in TWIN_WEIGHTS}
    shared = {n: given[n] for n in SHARED_INPUTS}
    per_example = {n: given[n] for n in ['x']}
    grad_fn = _jax.value_and_grad(_loss, argnums=(0, 1))

    def one_microbatch(ex, loss_target):
        ex = dict(ex)
        diff = ex.pop(TWIN_DIFF_INPUT)
        return grad_fn(weights, diff, {**shared, **ex}, loss_target)

    if N_MICROBATCH == 1:
        loss, (grad_w, grad_x) = one_microbatch(per_example, given["loss_target"])
    else:
        def body(carry, xs):
            loss_sum, grad_sum = carry
            l_k, (gw_k, gx_k) = one_microbatch(xs[0], xs[1])
            with _jax.named_scope("update"):
                return (loss_sum + l_k, _jax.tree.map(_jnp.add, grad_sum, gw_k)), gx_k

        init = (_jnp.zeros((), _jnp.float32), _jax.tree.map(_jnp.zeros_like, weights))
        (loss, grad_w), grad_x = _jax.lax.scan(body, init, (per_example, given["loss_target"]))
    with _jax.named_scope("update"):
        delta_w, new_m, new_v = {}, {}, {}
        for n in TWIN_WEIGHTS:
            delta_w[n], new_m[n], new_v[n] = _adamw(weights[n], grad_w[n], given["m_" + n], given["v_" + n])
    return (loss, grad_x, *[grad_w[n] for n in TWIN_WEIGHTS], *[delta_w[n] for n in TWIN_WEIGHTS],
            *[new_m[n] for n in TWIN_WEIGHTS], *[new_v[n] for n in TWIN_WEIGHTS])
```

```python
import functools
import math

import jax
import jax.numpy as jnp
from jax import lax
from jax.experimental import pallas as pl
from jax.experimental.pallas import tpu as pltpu

F32, BF16 = jnp.float32, jnp.bfloat16
SDS = jax.ShapeDtypeStruct
MESH = pl.DeviceIdType.MESH
AXES = ("x", "y", "c")

HEAD_DIM = 64
N_Q_HEADS = 16
N_KV_HEADS = 2
Q_PER_KV = N_Q_HEADS // N_KV_HEADS
ATTN_WIDTH = N_Q_HEADS * HEAD_DIM
KV_WIDTH = N_KV_HEADS * HEAD_DIM
ATTN_BLOCK = 128
SGU_CHUNK = 128
SGU_GROUPS = 8
SGU_WIDTH = SGU_GROUPS * 128
QKV_WIDTH = ATTN_WIDTH + 2 * KV_WIDTH
RMS_EPS = 1e-6
LN_EPS = 1e-5
MASK_VALUE = -1e30
ATTN_SCALE = 1.0 / math.sqrt(HEAD_DIM)

ADAM_LR, ADAM_B1, ADAM_B2, ADAM_EPS, ADAM_WD, ADAM_STEP = 0.001, 0.9, 0.999, 1e-08, 0.01, 10

N_CHIPS = 4
N_DEV = 8

VMEM_LIMIT_BYTES = 56 * 1024 * 1024
LANES = 128
SUBLANES_BF16 = 16

TM_NORM_MATMUL = 1024
TM_ROW = 512
TM_FFN_BWD = 256
TT_REDUCE = 1024
TQ_ATTN = 512
TS_SGU = 512


def _tile(n, pref, mult):
    t = (min(pref, n) // mult) * mult
    while t >= mult:
        if n % t == 0:
            return t
        t -= mult
    return n


def _params(*sem):
    return pltpu.CompilerParams(dimension_semantics=sem, vmem_limit_bytes=VMEM_LIMIT_BYTES)


def _dot(a, b):
    return jnp.dot(a, b, preferred_element_type=F32)


def _dot_nt(a, b):
    return lax.dot_general(a, b, (((1,), (1,)), ((), ())), preferred_element_type=F32)


def _dot_tn(a, b):
    return lax.dot_general(a, b, (((0,), (0,)), ((), ())), preferred_element_type=F32)


def _sigmoid(x):
    return 1.0 / (1.0 + jnp.exp(-x))


def _rms_stats(xf):
    r = lax.rsqrt(jnp.mean(xf * xf, axis=-1, keepdims=True) + RMS_EPS)
    return r, xf * r


def _rms_bwd(xf, g, dy):
    r, xh = _rms_stats(xf)
    dyg = dy * g
    dx = r * (dyg - xh * jnp.mean(dyg * xh, axis=-1, keepdims=True))
    return dx, jnp.sum(dy * xh, axis=0, keepdims=True)


def _gelu_parts(x):
    cdf = 0.5 * (1.0 + lax.erf(x * (1.0 / math.sqrt(2.0))))
    return cdf


def _gelu(x):
    return x * _gelu_parts(x)


def _gelu_grad(x):
    return _gelu_parts(x) + x * jnp.exp(-0.5 * x * x) * (1.0 / math.sqrt(2.0 * math.pi))


def _norm_matmul(x, g, w, *, name, with_h):
    T, D = x.shape
    N = w.shape[1]
    tm = _tile(T, TM_NORM_MATMUL, SUBLANES_BF16)
    tn = _tile(N, 1408, LANES)

    def body(x_ref, g_ref, w_ref, a_ref, *rest):
        h_sc = rest[-1]

        @pl.when(pl.program_id(1) == 0)
        def _():
            _, xh = _rms_stats(x_ref[...])
            h = (xh * g_ref[...]).astype(BF16)
            h_sc[...] = h
            if with_h:
                rest[0][...] = h

        a_ref[...] = _dot(h_sc[...], w_ref[...]).astype(BF16)

    out_specs = [pl.BlockSpec((tm, tn), lambda i, j: (i, j))]
    out_shape = [SDS((T, N), BF16)]
    if with_h:
        out_specs.append(pl.BlockSpec((tm, D), lambda i, j: (i, 0)))
        out_shape.append(SDS((T, D), BF16))
    return pl.pallas_call(
        body, name=name, grid=(T // tm, N // tn),
        in_specs=[pl.BlockSpec((tm, D), lambda i, j: (i, 0)),
                  pl.BlockSpec((1, D), lambda i, j: (0, 0)),
                  pl.BlockSpec((D, tn), lambda i, j: (0, j))],
        out_specs=out_specs, out_shape=out_shape,
        scratch_shapes=[pltpu.VMEM((tm, D), BF16)],
        compiler_params=_params("parallel", "arbitrary"),
    )(x, g, w)


def _ff_chunk(F):
    return F if F <= 1408 else F // 2


def _swiglu_out(a, w2, x, g_post, *, name):
    T, F2 = a.shape
    F = F2 // 2
    D = x.shape[1]
    tm = _tile(T, TM_ROW, SUBLANES_BF16)
    fc = _ff_chunk(F)

    def body(a_ref, w_ref, x_ref, g_ref, xn_ref, o_ref):
        acc = None
        for c0 in range(0, F, fc):
            gt = a_ref[:, c0:c0 + fc].astype(F32)
            ut = a_ref[:, F + c0:F + c0 + fc].astype(F32)
            s = (gt * _sigmoid(gt) * ut).astype(BF16)
            part = _dot(s, w_ref[c0:c0 + fc, :])
            acc = part if acc is None else acc + part
        o_ref[...] = acc.astype(BF16)
        _, oh = _rms_stats(acc)
        xn_ref[...] = x_ref[...] + 0.5 * (oh * g_ref[...])

    return pl.pallas_call(
        body, name=name, grid=(T // tm,),
        in_specs=[pl.BlockSpec((tm, F2), lambda i: (i, 0)),
                  pl.BlockSpec((F, D), lambda i: (0, 0)),
                  pl.BlockSpec((tm, D), lambda i: (i, 0)),
                  pl.BlockSpec((1, D), lambda i: (0, 0))],
        out_specs=[pl.BlockSpec((tm, D), lambda i: (i, 0)), pl.BlockSpec((tm, D), lambda i: (i, 0))],
        out_shape=[SDS((T, D), F32), SDS((T, D), BF16)],
        compiler_params=_params("parallel"),
    )(a, w2, x, g_post)


def _ffn_bwd_hidden(dy, o, g_post, a, w2, *, name):
    T, F2 = a.shape
    F = F2 // 2
    D = dy.shape[1]
    tm = _tile(T, TM_FFN_BWD, SUBLANES_BF16)
    fc = _ff_chunk(F)

    def body(dy_ref, o_ref, g_ref, a_ref, w_ref, da_ref, s_ref, do_ref, dg_ref):
        @pl.when(pl.program_id(0) == 0)
        def _():
            dg_ref[...] = jnp.zeros_like(dg_ref)

        do, dg = _rms_bwd(o_ref[...].astype(F32), g_ref[...], 0.5 * dy_ref[...])
        dg_ref[...] += dg
        dob = do.astype(BF16)
        do_ref[...] = dob
        for c0 in range(0, F, fc):
            ds = _dot_nt(dob, w_ref[c0:c0 + fc, :])
            gt = a_ref[:, c0:c0 + fc].astype(F32)
            ut = a_ref[:, F + c0:F + c0 + fc].astype(F32)
            sg = _sigmoid(gt)
            sl = gt * sg
            s_ref[:, c0:c0 + fc] = (sl * ut).astype(BF16)
            da_ref[:, c0:c0 + fc] = (ds * ut * (sg * (1.0 + gt * (1.0 - sg)))).astype(BF16)
            da_ref[:, F + c0:F + c0 + fc] = (ds * sl).astype(BF16)

    row = lambda w: pl.BlockSpec((tm, w), lambda i: (i, 0))
    return pl.pallas_call(
        body, name=name, grid=(T // tm,),
        in_specs=[row(D), row(D), pl.BlockSpec((1, D), lambda i: (0, 0)), row(F2),
                  pl.BlockSpec((F, D), lambda i: (0, 0))],
        out_specs=[row(F2), row(F), row(D), pl.BlockSpec((1, D), lambda i: (0, 0))],
        out_shape=[SDS((T, F2), BF16), SDS((T, F), BF16), SDS((T, D), BF16), SDS((1, D), F32)],
        compiler_params=_params("arbitrary"),
    )(dy, o, g_post, a, w2)


def _matmul_tn(a, b, *, name):
    T, K = a.shape
    N = b.shape[1]
    tk = _tile(K, 1408, LANES)
    tn = _tile(N, 1408, LANES)
    tt = _tile(T, TT_REDUCE, SUBLANES_BF16)

    def body(a_ref, b_ref, o_ref):
        @pl.when(pl.program_id(2) == 0)
        def _():
            o_ref[...] = jnp.zeros_like(o_ref)

        o_ref[...] += _dot_tn(a_ref[...], b_ref[...])

    return pl.pallas_call(
        body, name=name, grid=(K // tk, N // tn, T // tt),
        in_specs=[pl.BlockSpec((tt, tk), lambda k, n, t: (t, k)),
                  pl.BlockSpec((tt, tn), lambda k, n, t: (t, n))],
        out_specs=pl.BlockSpec((tk, tn), lambda k, n, t: (k, n)),
        out_shape=SDS((K, N), F32),
        compiler_params=_params("parallel", "parallel", "arbitrary"),
    )(a, b)


def _matmul_nt(da, w, *, name):
    T, N = da.shape
    D = w.shape[0]
    tm = _tile(T, TM_ROW, SUBLANES_BF16)

    def body(da_ref, w_ref, o_ref):
        o_ref[...] = _dot_nt(da_ref[...], w_ref[...])

    return pl.pallas_call(
        body, name=name, grid=(T // tm,),
        in_specs=[pl.BlockSpec((tm, N), lambda i: (i, 0)), pl.BlockSpec((D, N), lambda i: (0, 0))],
        out_specs=pl.BlockSpec((tm, D), lambda i: (i, 0)),
        out_shape=SDS((T, D), F32),
        compiler_params=_params("parallel"),
    )(da, w)


def _matmul_nt_norm_bwd(da, w, x, g, dy, init, *, name):
    T, N = da.shape
    D = w.shape[0]
    tm = _tile(T, TM_ROW, SUBLANES_BF16)
    tn = _tile(N, 1408, LANES)
    nj = N // tn
    has_init = init is not None

    def body(da_ref, w_ref, x_ref, g_ref, dy_ref, *rest):
        init_ref = rest[0] if has_init else None
        dx_ref, dg_ref, acc = rest[-3:]
        i, j = pl.program_id(0), pl.program_id(1)

        @pl.when(j == 0)
        def _():
            acc[...] = init_ref[...] if has_init else jnp.zeros_like(acc)

        acc[...] += _dot_nt(da_ref[...], w_ref[...])

        @pl.when(j == nj - 1)
        def _():
            @pl.when(i == 0)
            def _():
                dg_ref[...] = jnp.zeros_like(dg_ref)

            dx, dg = _rms_bwd(x_ref[...], g_ref[...], acc[...])
            dx_ref[...] = dy_ref[...] + dx
            dg_ref[...] += dg

    row = pl.BlockSpec((tm, D), lambda i, j: (i, 0))
    vec = pl.BlockSpec((1, D), lambda i, j: (0, 0))
    in_specs = [pl.BlockSpec((tm, tn), lambda i, j: (i, j)), pl.BlockSpec((D, tn), lambda i, j: (0, j)),
                row, vec, row]
    args = [da, w, x, g, dy]
    if has_init:
        in_specs.append(row)
        args.append(init)
    return pl.pallas_call(
        body, name=name, grid=(T // tm, nj), in_specs=in_specs,
        out_specs=[row, vec], out_shape=[SDS((T, D), F32), SDS((1, D), F32)],
        scratch_shapes=[pltpu.VMEM((tm, D), F32)],
        compiler_params=_params("arbitrary", "arbitrary"),
    )(*args)


def _attn_mask(first):
    qi = lax.broadcasted_iota(jnp.int32, (ATTN_BLOCK, 2 * ATTN_BLOCK), 0)
    kj = lax.broadcasted_iota(jnp.int32, (ATTN_BLOCK, 2 * ATTN_BLOCK), 1)
    rel = qi + ATTN_BLOCK - kj
    band = (rel >= 0) & (rel < ATTN_BLOCK)
    if first is False:
        return band
    return band & ((kj >= ATTN_BLOCK) | jnp.logical_not(first))


def _attn_probs(qh, k2g, valid, sink):
    s = _dot_nt(qh, k2g) * ATTN_SCALE
    s = jnp.where(valid, s, MASK_VALUE)
    m = jnp.maximum(jnp.max(s, axis=1, keepdims=True), sink)
    p = jnp.exp(s - m)
    es = jnp.exp(sink - m)
    den = jnp.sum(p, axis=1, keepdims=True) + es
    return p / den, es / den


def _attn_specs(tq, tile_of):
    nb = tq // ATTN_BLOCK
    kcol, vcol = ATTN_WIDTH // KV_WIDTH, ATTN_WIDTH // KV_WIDTH + 1
    halo = lambda col: pl.BlockSpec((ATTN_BLOCK, KV_WIDTH),
                                    lambda t: (jnp.maximum(tile_of(t) * nb - 1, 0), col))
    return [pl.BlockSpec((tq, ATTN_WIDTH), lambda t: (tile_of(t), 0)),
            pl.BlockSpec((tq, KV_WIDTH), lambda t: (tile_of(t), kcol)),
            pl.BlockSpec((tq, KV_WIDTH), lambda t: (tile_of(t), vcol)),
            halo(kcol), halo(vcol)]


def _attn_fwd(zqkv, sinks, *, name):
    T = zqkv.shape[0]
    tq = _tile(T, TQ_ATTN, ATTN_BLOCK)
    nb = tq // ATTN_BLOCK

    def body(q_ref, k_ref, v_ref, kh_ref, vh_ref, s_ref, o_ref, kf, vf):
        kf[0:ATTN_BLOCK, :] = kh_ref[...]
        kf[ATTN_BLOCK:, :] = k_ref[...]
        vf[0:ATTN_BLOCK, :] = vh_ref[...]
        vf[ATTN_BLOCK:, :] = v_ref[...]
        for b in range(nb):
            rows = slice(b * ATTN_BLOCK, (b + 1) * ATTN_BLOCK)
            win = slice(b * ATTN_BLOCK, (b + 2) * ATTN_BLOCK)
            valid = _attn_mask((pl.program_id(0) == 0) if b == 0 else False)
            for g in range(N_KV_HEADS):
                gc = slice(g * HEAD_DIM, (g + 1) * HEAD_DIM)
                k2g = kf[win, gc]
                v2g = vf[win, gc]
                for r in range(Q_PER_KV):
                    h = g * Q_PER_KV + r
                    hc = slice(h * HEAD_DIM, (h + 1) * HEAD_DIM)
                    probs, _ = _attn_probs(q_ref[rows, hc], k2g, valid, s_ref[h])
                    o_ref[rows, hc] = _dot(probs.astype(BF16), v2g).astype(BF16)

    return pl.pallas_call(
        body, name=name, grid=(T // tq,),
        in_specs=_attn_specs(tq, lambda t: t) + [pl.BlockSpec(memory_space=pltpu.SMEM)],
        out_specs=pl.BlockSpec((tq, ATTN_WIDTH), lambda t: (t, 0)),
        out_shape=SDS((T, ATTN_WIDTH), BF16),
        scratch_shapes=[pltpu.VMEM((tq + ATTN_BLOCK, KV_WIDTH), BF16)] * 2,
        compiler_params=_params("parallel"),
    )(zqkv, zqkv, zqkv, zqkv, zqkv, sinks)


def _attn_bwd(zqkv, sinks, do, *, name):
    T = zqkv.shape[0]
    tq = _tile(T, TQ_ATTN, ATTN_BLOCK)
    nb = tq // ATTN_BLOCK
    nt = T // tq
    tile_of = lambda t: nt - 1 - t

    def body(q_ref, k_ref, v_ref, kh_ref, vh_ref, do_ref, s_ref, dz_ref, dsink_ref, kf, vf, dkf, dvf, carry):
        t = pl.program_id(0)

        @pl.when(t == 0)
        def _():
            carry[...] = jnp.zeros_like(carry)
            dsink_ref[...] = jnp.zeros_like(dsink_ref)

        kf[0:ATTN_BLOCK, :] = kh_ref[...]
        kf[ATTN_BLOCK:, :] = k_ref[...]
        vf[0:ATTN_BLOCK, :] = vh_ref[...]
        vf[ATTN_BLOCK:, :] = v_ref[...]
        dkf[...] = jnp.zeros_like(dkf)
        dvf[...] = jnp.zeros_like(dvf)
        dkf[tq:, :] = carry[:, 0:KV_WIDTH]
        dvf[tq:, :] = carry[:, KV_WIDTH:]
        lane = lax.broadcasted_iota(jnp.int32, (1, LANES), 1)
        dsink = jnp.zeros((1, LANES), F32)
        for b in range(nb):
            rows = slice(b * ATTN_BLOCK, (b + 1) * ATTN_BLOCK)
            win = slice(b * ATTN_BLOCK, (b + 2) * ATTN_BLOCK)
            valid = _attn_mask((t == nt - 1) if b == 0 else False)
            for g in range(N_KV_HEADS):
                gc = slice(g * HEAD_DIM, (g + 1) * HEAD_DIM)
                k2g = kf[win, gc]
                v2g = vf[win, gc]
                dk2g = jnp.zeros((2 * ATTN_BLOCK, HEAD_DIM), F32)
                dv2g = jnp.zeros((2 * ATTN_BLOCK, HEAD_DIM), F32)
                for r in range(Q_PER_KV):
                    h = g * Q_PER_KV + r
                    hc = slice(h * HEAD_DIM, (h + 1) * HEAD_DIM)
                    qh = q_ref[rows, hc]
                    doh = do_ref[rows, hc]
                    probs, psink = _attn_probs(qh, k2g, valid, s_ref[h])
                    dp = _dot_nt(doh, v2g)
                    delta = jnp.sum(probs * dp, axis=1, keepdims=True)
                    ds = (probs * (dp - delta)).astype(BF16)
                    dsink = dsink + jnp.where(lane == h, -jnp.sum(psink * delta), 0.0)
                    dz_ref[rows, hc] = (_dot(ds, k2g) * ATTN_SCALE).astype(BF16)
                    dk2g = dk2g + _dot_tn(ds, qh)
                    dv2g = dv2g + _dot_tn(probs.astype(BF16), doh)
                dkf[win, gc] += dk2g * ATTN_SCALE
                dvf[win, gc] += dv2g
        dz_ref[:, ATTN_WIDTH:ATTN_WIDTH + KV_WIDTH] = dkf[ATTN_BLOCK:, :].astype(BF16)
        dz_ref[:, ATTN_WIDTH + KV_WIDTH:] = dvf[ATTN_BLOCK:, :].astype(BF16)
        carry[:, 0:KV_WIDTH] = dkf[0:ATTN_BLOCK, :]
        carry[:, KV_WIDTH:] = dvf[0:ATTN_BLOCK, :]
        dsink_ref[...] += dsink

    return pl.pallas_call(
        body, name=name, grid=(nt,),
        in_specs=_attn_specs(tq, tile_of) + [pl.BlockSpec((tq, ATTN_WIDTH), lambda t: (tile_of(t), 0)),
                                             pl.BlockSpec(memory_space=pltpu.SMEM)],
        out_specs=[pl.BlockSpec((tq, QKV_WIDTH), lambda t: (tile_of(t), 0)),
                   pl.BlockSpec((8, LANES), lambda t: (0, 0))],
        out_shape=[SDS((T, QKV_WIDTH), BF16), SDS((8, LANES), F32)],
        scratch_shapes=[pltpu.VMEM((tq + ATTN_BLOCK, KV_WIDTH), BF16)] * 2
        + [pltpu.VMEM((tq + ATTN_BLOCK, KV_WIDTH), F32)] * 2 + [pltpu.VMEM((ATTN_BLOCK, 2 * KV_WIDTH), F32)],
        compiler_params=_params("arbitrary"),
    )(zqkv, zqkv, zqkv, zqkv, zqkv, do, sinks)


def _layer_norm_stats(v):
    mu = jnp.mean(v, axis=-1, keepdims=True)
    xc = v - mu
    rstd = lax.rsqrt(jnp.mean(xc * xc, axis=-1, keepdims=True) + LN_EPS)
    return rstd, xc * rstd


def _sgu_fwd(zmain, ln_g, ln_b, wm, bias, *, name):
    T = zmain.shape[0]
    ts = _tile(T, TS_SGU, SGU_CHUNK)

    def body(u_ref, v_ref, g_ref, b_ref, w_ref, bias_ref, y_ref):
        u = _gelu(u_ref[...].astype(F32))
        _, vh = _layer_norm_stats(_gelu(v_ref[...].astype(F32)))
        vn = (vh * g_ref[...] + b_ref[...]).astype(BF16)
        for ch in range(ts // SGU_CHUNK):
            rows = slice(ch * SGU_CHUNK, (ch + 1) * SGU_CHUNK)
            for g in range(SGU_GROUPS):
                cols = slice(g * 128, (g + 1) * 128)
                s = _dot(w_ref[g], vn[rows, cols]) + bias_ref[g]
                y_ref[rows, cols] = (u[rows, cols] * s).astype(BF16)

    full = lambda shape: pl.BlockSpec(shape, lambda i: (0,) * len(shape))
    return pl.pallas_call(
        body, name=name, grid=(T // ts,),
        in_specs=[pl.BlockSpec((ts, SGU_WIDTH), lambda i: (i, 0)), pl.BlockSpec((ts, SGU_WIDTH), lambda i: (i, 1)),
                  full((1, SGU_WIDTH)), full((1, SGU_WIDTH)), full(wm.shape), full(bias.shape)],
        out_specs=pl.BlockSpec((ts, SGU_WIDTH), lambda i: (i, 0)),
        out_shape=SDS((T, SGU_WIDTH), BF16),
        compiler_params=_params("parallel"),
    )(zmain, zmain, ln_g, ln_b, wm, bias)


def _sgu_bwd(zmain, dzmain, dy, ln_g, ln_b, wm, wmt, bias, *, name):
    T = zmain.shape[0]
    ts = _tile(T, TS_SGU, SGU_CHUNK)

    def body(u_ref, v_ref, dy_ref, g_ref, b_ref, w_ref, wt_ref, bias_ref, _, dz_ref, dw_ref, db_ref, dlg_ref, dlb_ref,
             dvn):
        @pl.when(pl.program_id(0) == 0)
        def _():
            dw_ref[...] = jnp.zeros_like(dw_ref)
            db_ref[...] = jnp.zeros_like(db_ref)
            dlg_ref[...] = jnp.zeros_like(dlg_ref)
            dlb_ref[...] = jnp.zeros_like(dlb_ref)

        us = u_ref[...].astype(F32)
        vs = v_ref[...].astype(F32)
        u = _gelu(us)
        rstd, vh = _layer_norm_stats(_gelu(vs))
        vn = (vh * g_ref[...] + b_ref[...]).astype(BF16)
        causal = (lax.broadcasted_iota(jnp.int32, (SGU_CHUNK, SGU_CHUNK), 0)
                  >= lax.broadcasted_iota(jnp.int32, (SGU_CHUNK, SGU_CHUNK), 1))
        lane = lax.broadcasted_iota(jnp.int32, (SGU_CHUNK, LANES), 1)
        db = jnp.zeros((SGU_CHUNK, LANES), F32)
        for ch in range(ts // SGU_CHUNK):
            rows = slice(ch * SGU_CHUNK, (ch + 1) * SGU_CHUNK)
            for g in range(SGU_GROUPS):
                cols = slice(g * 128, (g + 1) * 128)
                vng = vn[rows, cols]
                s = _dot(w_ref[g], vng) + bias_ref[g]
                dyf = dy_ref[rows, cols].astype(F32)
                dz_ref[rows, cols] = (dyf * s * _gelu_grad(us[rows, cols])).astype(BF16)
                dsf = dyf * u[rows, cols]
                dsb = dsf.astype(BF16)
                dvn[rows, cols] = _dot(wt_ref[g], dsb)
                dw_ref[g] += jnp.where(causal, _dot_nt(dsb, vng), 0.0)
                db = db + jnp.where(lane == g, jnp.sum(dsf, axis=1, keepdims=True), 0.0)
        db_ref[...] += db
        dvnf = dvn[...]
        dlg_ref[...] += jnp.sum(dvnf * vh, axis=0, keepdims=True)
        dlb_ref[...] += jnp.sum(dvnf, axis=0, keepdims=True)
        dvh = dvnf * g_ref[...]
        dv = rstd * (dvh - jnp.mean(dvh, axis=-1, keepdims=True) - vh * jnp.mean(dvh * vh, axis=-1, keepdims=True))
        dz_ref[:, SGU_WIDTH:] = (dv * _gelu_grad(vs)).astype(BF16)

    full = lambda shape: pl.BlockSpec(shape, lambda i: (0,) * len(shape))
    vec = full((1, SGU_WIDTH))
    return pl.pallas_call(
        body, name=name, grid=(T // ts,),
        in_specs=[pl.BlockSpec((ts, SGU_WIDTH), lambda i: (i, 0)), pl.BlockSpec((ts, SGU_WIDTH), lambda i: (i, 1)),
                  pl.BlockSpec((ts, SGU_WIDTH), lambda i: (i, 0)), vec, vec, full(wm.shape), full(wm.shape),
                  full(bias.shape), pl.BlockSpec(memory_space=pl.ANY)],
        out_specs=[pl.BlockSpec((ts, 2 * SGU_WIDTH), lambda i: (i, 0)), full(wm.shape),
                   full((SGU_CHUNK, LANES)), vec, vec],
        out_shape=[SDS(dzmain.shape, BF16), SDS(wm.shape, F32), SDS((SGU_CHUNK, LANES), F32),
                   SDS((1, SGU_WIDTH), F32), SDS((1, SGU_WIDTH), F32)],
        scratch_shapes=[pltpu.VMEM((ts, SGU_WIDTH), F32)],
        input_output_aliases={8: 0},
        compiler_params=_params("arbitrary"),
    )(zmain, zmain, dy, ln_g, ln_b, wm, wmt, bias, dzmain)


def _merge_fwd(y_attn, y_sgu, zmain, w_a, w_s, w_o, x, g_post, *, name):
    T, D = x.shape
    tm = _tile(T, TM_ROW, SUBLANES_BF16)

    def body(ya_ref, ys_ref, ga_ref, gb_ref, wa_ref, ws_ref, wo_ref, x_ref, g_ref, xn_ref, pa_ref, ps_ref, o_ref):
        pa = _dot(ya_ref[...], wa_ref[...])
        ps = _dot(ys_ref[...], ws_ref[...])
        pa_ref[...] = pa.astype(BF16)
        ps_ref[...] = ps.astype(BF16)
        merged = _sigmoid(ga_ref[...].astype(F32)) * pa + _sigmoid(gb_ref[...].astype(F32)) * ps
        out = _dot(merged.astype(BF16), wo_ref[...])
        o_ref[...] = out.astype(BF16)
        _, oh = _rms_stats(out)
        xn_ref[...] = x_ref[...] + oh * g_ref[...]

    row = lambda col: pl.BlockSpec((tm, D), lambda i: (i, col))
    wfull = pl.BlockSpec((D, D), lambda i: (0, 0))
    return pl.pallas_call(
        body, name=name, grid=(T // tm,),
        in_specs=[row(0), row(0), row(2), row(3), wfull, wfull, wfull, row(0), pl.BlockSpec((1, D), lambda i: (0, 0))],
        out_specs=[row(0)] * 4,
        out_shape=[SDS((T, D), F32), SDS((T, D), BF16), SDS((T, D), BF16), SDS((T, D), BF16)],
        compiler_params=_params("parallel"),
    )(y_attn, y_sgu, zmain, zmain, w_a, w_s, w_o, x, g_post)


def _merge_bwd(dy, out, g_post, pa, ps, zmain, w_a, w_s, w_o, *, name):
    T, D = dy.shape
    tm = _tile(T, TM_ROW, SUBLANES_BF16)

    def body(dy_ref, o_ref, g_ref, pa_ref, ps_ref, ga_ref, gb_ref, wa_ref, ws_ref, wo_ref,
             dz_ref, dout_ref, mg_ref, dpa_ref, dps_ref, dya_ref, dys_ref, dg_ref):
        @pl.when(pl.program_id(0) == 0)
        def _():
            dg_ref[...] = jnp.zeros_like(dg_ref)

        dout, dg = _rms_bwd(o_ref[...].astype(F32), g_ref[...], dy_ref[...])
        dg_ref[...] += dg
        doutb = dout.astype(BF16)
        dout_ref[...] = doutb
        dm = _dot_nt(doutb, wo_ref[...])
        pa = pa_ref[...].astype(F32)
        ps = ps_ref[...].astype(F32)
        sa = _sigmoid(ga_ref[...].astype(F32))
        sb = _sigmoid(gb_ref[...].astype(F32))
        mg_ref[...] = (sa * pa + sb * ps).astype(BF16)
        dpa = (dm * sa).astype(BF16)
        dps = (dm * sb).astype(BF16)
        dpa_ref[...] = dpa
        dps_ref[...] = dps
        dz_ref[:, 0:D] = (dm * pa * sa * (1.0 - sa)).astype(BF16)
        dz_ref[:, D:] = (dm * ps * sb * (1.0 - sb)).astype(BF16)
        dya_ref[...] = _dot_nt(dpa, wa_ref[...]).astype(BF16)
        dys_ref[...] = _dot_nt(dps, ws_ref[...]).astype(BF16)

    row = lambda col: pl.BlockSpec((tm, D), lambda i: (i, col))
    wfull = pl.BlockSpec((D, D), lambda i: (0, 0))
    vec = pl.BlockSpec((1, D), lambda i: (0, 0))
    act = SDS((T, D), BF16)
    return pl.pallas_call(
        body, name=name, grid=(T // tm,),
        in_specs=[row(0), row(0), vec, row(0), row(0), row(2), row(3), wfull, wfull, wfull],
        out_specs=[pl.BlockSpec((tm, 2 * D), lambda i: (i, 1))] + [row(0)] * 6 + [vec],
        out_shape=[SDS(zmain.shape, BF16)] + [act] * 6 + [SDS((1, D), F32)],
        compiler_params=_params("arbitrary"),
    )(dy, out, g_post, pa, ps, zmain, zmain, w_a, w_s, w_o)


def _loss_head(y, target, *, name):
    T, D = y.shape
    tm = _tile(T, TM_ROW, 8)

    def body(y_ref, t_ref, dy_ref, l_ref):
        @pl.when(pl.program_id(0) == 0)
        def _():
            l_ref[...] = jnp.zeros_like(l_ref)

        e = y_ref[...] - t_ref[...]
        dy_ref[...] = e * (1.0 / D)
        l_ref[...] += jnp.sum(jnp.mean(e * e, axis=-1, keepdims=True))

    row = pl.BlockSpec((tm, D), lambda i: (i, 0))
    return pl.pallas_call(
        body, name=name, grid=(T // tm,), in_specs=[row, row],
        out_specs=[row, pl.BlockSpec((8, LANES), lambda i: (0, 0))],
        out_shape=[SDS((T, D), F32), SDS((8, LANES), F32)],
        compiler_params=_params("arbitrary"),
    )(y, target)


def _adamw(w, g, m, v, *, name):
    shape = w.shape
    cols = shape[-1]
    rows = w.size // cols
    w2, g2, m2, v2 = (t.reshape(rows, cols) for t in (w, g, m, v))
    tr = _tile(rows, max(8, (256 * 1024) // cols // 8 * 8), 8)

    def body(w_ref, g_ref, m_ref, v_ref, d_ref, nm_ref, nv_ref):
        gg = g_ref[...]
        nm = ADAM_B1 * m_ref[...] + (1.0 - ADAM_B1) * gg
        nv = ADAM_B2 * v_ref[...] + (1.0 - ADAM_B2) * (gg * gg)
        m_hat = nm / (1.0 - ADAM_B1 ** ADAM_STEP)
        v_hat = nv / (1.0 - ADAM_B2 ** ADAM_STEP)
        d_ref[...] = -ADAM_LR * (m_hat / (jnp.sqrt(v_hat) + ADAM_EPS) + ADAM_WD * w_ref[...])
        nm_ref[...] = nm
        nv_ref[...] = nv

    blk = pl.BlockSpec((tr, cols), lambda i: (i, 0))
    outs = pl.pallas_call(
        body, name=name, grid=(rows // tr,), in_specs=[blk] * 4, out_specs=[blk] * 3,
        out_shape=[SDS((rows, cols), F32)] * 3, compiler_params=_params("parallel"),
    )(w2, g2, m2, v2)
    return tuple(o.reshape(shape) for o in outs)


def _sum_terms(terms, n_rows, n_lead, *, name):
    cols = terms[0][0].shape[-1]
    tr = _tile(n_rows, 704 if len(terms) <= 4 else 256, 8)
    scalars = jnp.stack([jnp.asarray(v, jnp.int32)
                         for _, lead, off in terms for v in (0 if lead is None else lead, off // tr)])

    def body(s_ref, *refs):
        acc = refs[0][...]
        for r in refs[1:-1]:
            acc = acc + r[...]
        refs[-1][...] = acc

    def spec(k, own):
        return pl.BlockSpec((1, tr, cols), lambda a, i, s: (a if own else s[2 * k], s[2 * k + 1] + i, 0))

    return pl.pallas_call(
        body, name=name,
        grid_spec=pltpu.PrefetchScalarGridSpec(
            num_scalar_prefetch=1, grid=(n_lead, n_rows // tr),
            in_specs=[spec(k, lead is None) for k, (_, lead, _) in enumerate(terms)],
            out_specs=pl.BlockSpec((1, tr, cols), lambda a, i, s: (a, i, 0))),
        out_shape=SDS((n_lead, n_rows, cols), F32),
        compiler_params=_params("arbitrary", "arbitrary"),
    )(scalars, *[a for a, _, _ in terms])


def _position():
    x, y, c = (lax.axis_index(a) for a in AXES)
    chips = [(1 - x, y), (x, 1 - y), (1 - x, 1 - y)]
    return x, y, c, chips


ANY = pl.BlockSpec(memory_space=pl.ANY)


def _remote(src, dst, send_sems, recv_sems, k, to):
    return pltpu.make_async_remote_copy(src_ref=src, dst_ref=dst, send_sem=send_sems.at[k], recv_sem=recv_sems.at[k],
                                        device_id=to, device_id_type=MESH)


def _gather_shards(pack, *, name):
    R, C = pack.shape
    Rh = R // 2

    def body(p_ref, o_ref, send_sems, recv_sems, local_sem):
        x, y, c, chips = _position()
        mine = pl.ds(c * Rh, Rh)
        own = pltpu.make_async_copy(p_ref, o_ref.at[2 * x + y], local_sem)
        own.start()
        sent = [_remote(p_ref.at[mine], o_ref.at[2 * x + y, mine], send_sems, recv_sems, j, (*chip, c))
                for j, chip in enumerate(chips)]
        for cp in sent:
            cp.start()
        passed = []
        for j, (cx, cy) in enumerate(chips):
            landed = o_ref.at[2 * cx + cy, mine]
            _remote(landed, landed, send_sems, recv_sems, j, (x, y, c)).wait_recv()
            cp = _remote(landed, landed, send_sems, recv_sems, 3 + j, (x, y, 1 - c))
            cp.start()
            passed.append(cp)
        for j, (cx, cy) in enumerate(chips):
            theirs = o_ref.at[2 * cx + cy, pl.ds((1 - c) * Rh, Rh)]
            _remote(theirs, theirs, send_sems, recv_sems, 3 + j, (x, y, c)).wait_recv()
        for cp in sent + passed:
            cp.wait_send()
        own.wait()

    return pl.pallas_call(
        body, name=name, in_specs=[ANY], out_specs=ANY, out_shape=SDS((N_CHIPS, R, C), pack.dtype),
        scratch_shapes=[pltpu.SemaphoreType.DMA((6,)), pltpu.SemaphoreType.DMA((6,)), pltpu.SemaphoreType.DMA],
    )(pack)


def _sibling_exchange(g, *, name):
    _, R, C = g.shape
    Rh = R // 2

    def body(g_ref, o_ref, send_sems, recv_sems):
        x, y, c, _ = _position()
        cp = _remote(g_ref.at[:, pl.ds((1 - c) * Rh, Rh)], o_ref, send_sems, recv_sems, 0, (x, y, 1 - c))
        cp.start()
        cp.wait()

    return pl.pallas_call(
        body, name=name, in_specs=[ANY], out_specs=ANY, out_shape=SDS((N_CHIPS, Rh, C), g.dtype),
        scratch_shapes=[pltpu.SemaphoreType.DMA((1,)), pltpu.SemaphoreType.DMA((1,))],
    )(g)


def _chip_scatter(p, *, name):
    _, Rh, C = p.shape

    def body(p_ref, o_ref, send_sems, recv_sems):
        x, y, c, chips = _position()
        sent = [_remote(p_ref.at[2 * cx + cy], o_ref.at[j], send_sems, recv_sems, j, (cx, cy, c))
                for j, (cx, cy) in enumerate(chips)]
        for cp in sent:
            cp.start()
        for cp in sent:
            cp.wait()

    return pl.pallas_call(
        body, name=name, in_specs=[ANY], out_specs=ANY, out_shape=SDS((N_CHIPS - 1, Rh, C), p.dtype),
        scratch_shapes=[pltpu.SemaphoreType.DMA((3,)), pltpu.SemaphoreType.DMA((3,))],
    )(p)


def _sibling_join(t, *, name):
    Rh, C = t.shape

    def body(t_ref, o_ref, send_sems, recv_sems, local_sem):
        x, y, c, _ = _position()
        mine = pl.ds(c * Rh, Rh)
        own = pltpu.make_async_copy(t_ref, o_ref.at[mine], local_sem)
        own.start()
        cp = _remote(t_ref, o_ref.at[mine], send_sems, recv_sems, 0, (x, y, 1 - c))
        cp.start()
        theirs = o_ref.at[pl.ds((1 - c) * Rh, Rh)]
        _remote(theirs, theirs, send_sems, recv_sems, 0, (x, y, c)).wait_recv()
        cp.wait_send()
        own.wait()

    return pl.pallas_call(
        body, name=name, in_specs=[ANY], out_specs=ANY, out_shape=SDS((2 * Rh, C), t.dtype),
        scratch_shapes=[pltpu.SemaphoreType.DMA((1,)), pltpu.SemaphoreType.DMA((1,)), pltpu.SemaphoreType.DMA],
    )(t)


def _gather_all(v, *, name):
    M, C = v.shape

    def body(v_ref, o_ref, send_sems, recv_sems, local_sem):
        x, y, c, chips = _position()
        slot = lambda px, py, pc: o_ref.at[4 * px + 2 * py + pc]
        own = pltpu.make_async_copy(v_ref, slot(x, y, c), local_sem)
        own.start()
        first = [_remote(v_ref, slot(x, y, c), send_sems, recv_sems, 0, (x, y, 1 - c))]
        first += [_remote(v_ref, slot(x, y, c), send_sems, recv_sems, 1 + j, (*chip, c)) for j, chip in enumerate(chips)]
        for cp in first:
            cp.start()
        passed = []
        for j, chip in enumerate(chips):
            landed = slot(*chip, c)
            _remote(landed, landed, send_sems, recv_sems, 1 + j, (x, y, c)).wait_recv()
            cp = _remote(landed, landed, send_sems, recv_sems, 4 + j, (x, y, 1 - c))
            cp.start()
            passed.append(cp)
        sib = slot(x, y, 1 - c)
        _remote(sib, sib, send_sems, recv_sems, 0, (x, y, c)).wait_recv()
        for j, chip in enumerate(chips):
            theirs = slot(*chip, 1 - c)
            _remote(theirs, theirs, send_sems, recv_sems, 4 + j, (x, y, c)).wait_recv()
        for cp in first + passed:
            cp.wait_send()
        own.wait()

    return pl.pallas_call(
        body, name=name, in_specs=[ANY], out_specs=ANY, out_shape=SDS((N_DEV, M, C), v.dtype),
        scratch_shapes=[pltpu.SemaphoreType.DMA((7,)), pltpu.SemaphoreType.DMA((7,)), pltpu.SemaphoreType.DMA],
    )(v)


BIG = ("ffn1_w1", "ffn1_w2", "w_in", "w_attn_branch", "w_sgu_branch", "w_out", "ffn2_w1", "ffn2_w2")
COL_SHARDED = ("ffn1_w1", "w_in", "ffn2_w1")
SMALL = ("ffn1_pre_g", "ffn1_post_g", "mix_pre_g", "attn_sinks", "sgu_ln_g", "sgu_ln_b", "sgu_w", "sgu_b",
         "mix_post_g", "ffn2_pre_g", "ffn2_post_g")
WEIGHTS = ("ffn1_pre_g", "ffn1_w1", "ffn1_w2", "ffn1_post_g", "mix_pre_g", "w_in", "attn_sinks", "sgu_ln_g",
           "sgu_ln_b", "sgu_w", "sgu_b", "w_attn_branch", "w_sgu_branch", "w_out", "mix_post_g", "ffn2_pre_g",
           "ffn2_w1", "ffn2_w2", "ffn2_post_g")


def _rows_of(shape, width):
    return (shape[0] * shape[1]) // width


def _pack_layer(shards, width, dtype):
    return jnp.concatenate([shards[n].astype(dtype).reshape(-1, width) for n in BIG], axis=0)


def _unpack_layer(rows, shard_shapes, width):
    out, r0 = {}, 0
    for n in BIG:
        nr = _rows_of(shard_shapes[n], width)
        out[n] = rows[r0:r0 + nr].reshape(shard_shapes[n])
        r0 += nr
    return out


def _ffn_fwd(x, pre_g, w1, w2, post_g, tag):
    a, h = _norm_matmul(x, pre_g, w1, name=f"{tag}_up", with_h=True)
    xn, o = _swiglu_out(a, w2, x, post_g, name=f"{tag}_down")
    return xn, (x, h, a, o)


def _ffn_bwd(dy, saved, pre_g, w1, w2, post_g, tag):
    x, h, a, o = saved
    da, s, do, d_post = _ffn_bwd_hidden(dy, o, post_g, a, w2, name=f"{tag}_bwd_hidden")
    dw2 = _matmul_tn(s, do, name=f"{tag}_dw2")
    dx, d_pre = _matmul_nt_norm_bwd(da, w1, x, pre_g, dy, None, name=f"{tag}_bwd_in")
    dw1 = _matmul_tn(h, da, name=f"{tag}_dw1")
    return dx, dw1, dw2, d_pre, d_post


def kernel(x, ffn1_pre_g, ffn1_w1, ffn1_w2, ffn1_post_g, mix_pre_g, w_in, attn_sinks, sgu_ln_g, sgu_ln_b, sgu_w, sgu_b, w_attn_branch, w_sgu_branch, w_out, mix_post_g, ffn2_pre_g, ffn2_w1, ffn2_w2, ffn2_post_g, loss_target, m_ffn1_pre_g, m_ffn1_w1, m_ffn1_w2, m_ffn1_post_g, m_mix_pre_g, m_w_in, m_attn_sinks, m_sgu_ln_g, m_sgu_ln_b, m_sgu_w, m_sgu_b, m_w_attn_branch, m_w_sgu_branch, m_w_out, m_mix_post_g, m_ffn2_pre_g, m_ffn2_w1, m_ffn2_w2, m_ffn2_post_g, v_ffn1_pre_g, v_ffn1_w1, v_ffn1_w2, v_ffn1_post_g, v_mix_pre_g, v_w_in, v_attn_sinks, v_sgu_ln_g, v_sgu_ln_b, v_sgu_w, v_sgu_b, v_w_attn_branch, v_w_sgu_branch, v_w_out, v_mix_post_g, v_ffn2_pre_g, v_ffn2_w1, v_ffn2_w2, v_ffn2_post_g):
    given = dict(locals())
    W = {n: given[n] for n in WEIGHTS}
    M = {n: given["m_" + n] for n in WEIGHTS}
    V = {n: given["v_" + n] for n in WEIGHTS}
    L = ffn1_w1.shape[0]
    T, D = x.shape[1], x.shape[2]
    xt = x.reshape(T, D)
    target = loss_target.reshape(T, D)
    assert L % 2 == 0 and D == ATTN_WIDTH == SGU_WIDTH and T % ATTN_BLOCK == 0

    shard_shapes = {n: W[n].shape[1:] for n in BIG}
    pack = jnp.concatenate([_pack_layer({n: W[n][l] for n in BIG}, D, BF16) for l in range(L)], axis=0)
    R = pack.shape[0]
    rows_per_layer = R // L
    gathered = _gather_shards(pack, name="gather_weights")
    full = []
    for l in range(L):
        per_chip = [_unpack_layer(gathered[s, l * rows_per_layer:(l + 1) * rows_per_layer], shard_shapes, D)
                    for s in range(N_CHIPS)]
        full.append({n: jnp.concatenate([pc[n] for pc in per_chip], axis=1 if n in COL_SHARDED else 0) for n in BIG})

    row = lambda name, l: W[name][l].reshape(1, -1)
    causal = jnp.tril(jnp.ones((SGU_CHUNK, SGU_CHUNK), dtype=bool))
    saved = []
    h_cur = xt
    for l in range(L):
        fw = full[l]
        sv = {}
        h_cur, sv["ffn1"] = _ffn_fwd(h_cur, row("ffn1_pre_g", l), fw["ffn1_w1"], fw["ffn1_w2"], row("ffn1_post_g", l),
                                     f"l{l}_ffn1")
        w_qkv, w_main = fw["w_in"][:, :QKV_WIDTH], fw["w_in"][:, QKV_WIDTH:]
        zqkv, hm = _norm_matmul(h_cur, row("mix_pre_g", l), w_qkv, name=f"l{l}_mix_in_qkv", with_h=True)
        zmain, = _norm_matmul(h_cur, row("mix_pre_g", l), w_main, name=f"l{l}_mix_in_main", with_h=False)
        wm = jnp.where(causal[None], sgu_w[l], 0.0).astype(BF16)
        wmt = jnp.swapaxes(wm, 1, 2)
        bias = jnp.broadcast_to(sgu_b[l][:, :, None], (SGU_GROUPS, SGU_CHUNK, 128)).astype(F32)
        y_attn = _attn_fwd(zqkv, attn_sinks[l], name=f"l{l}_attn")
        y_sgu = _sgu_fwd(zmain, row("sgu_ln_g", l), row("sgu_ln_b", l), wm, bias, name=f"l{l}_sgu")
        x_mix = h_cur
        h_cur, pa, ps, mo = _merge_fwd(y_attn, y_sgu, zmain, fw["w_attn_branch"], fw["w_sgu_branch"], fw["w_out"],
                                       x_mix, row("mix_post_g", l), name=f"l{l}_merge")
        sv["mix"] = (x_mix, hm, zqkv, zmain, y_attn, y_sgu, pa, ps, mo, wm, wmt, bias)
        h_cur, sv["ffn2"] = _ffn_fwd(h_cur, row("ffn2_pre_g", l), fw["ffn2_w1"], fw["ffn2_w2"], row("ffn2_post_g", l),
                                     f"l{l}_ffn2")
        saved.append(sv)

    dy, lsum = _loss_head(h_cur, target, name="loss_head")
    loss = lax.psum(0.5 * lsum[0, 0], AXES)

    big_grads = [None] * L
    small_grads = [None] * L
    for l in reversed(range(L)):
        fw, sv = full[l], saved[l]
        gb, gs = {}, {}
        dy, gb["ffn2_w1"], gb["ffn2_w2"], gs["ffn2_pre_g"], gs["ffn2_post_g"] = _ffn_bwd(
            dy, sv["ffn2"], row("ffn2_pre_g", l), fw["ffn2_w1"], fw["ffn2_w2"], row("ffn2_post_g", l), f"l{l}_ffn2")

        x_mix, hm, zqkv, zmain, y_attn, y_sgu, pa, ps, mo, wm, wmt, bias = sv["mix"]
        dzmain, dout, merged, dpa, dps, dya, dys, gs["mix_post_g"] = _merge_bwd(
            dy, mo, row("mix_post_g", l), pa, ps, zmain, fw["w_attn_branch"], fw["w_sgu_branch"], fw["w_out"],
            name=f"l{l}_merge_bwd")
        gb["w_out"] = _matmul_tn(merged, dout, name=f"l{l}_dw_out")
        gb["w_attn_branch"] = _matmul_tn(y_attn, dpa, name=f"l{l}_dw_attn")
        gb["w_sgu_branch"] = _matmul_tn(y_sgu, dps, name=f"l{l}_dw_sgu")
        dzqkv, dsink = _attn_bwd(zqkv, attn_sinks[l], dya, name=f"l{l}_attn_bwd")
        gs["attn_sinks"] = dsink[0, :N_Q_HEADS]
        dzmain, dsw, dsb, gs["sgu_ln_g"], gs["sgu_ln_b"] = _sgu_bwd(
            zmain, dzmain, dys, row("sgu_ln_g", l), row("sgu_ln_b", l), wm, wmt, bias, name=f"l{l}_sgu_bwd")
        gs["sgu_w"] = dsw
        gs["sgu_b"] = dsb[:, :SGU_GROUPS].T
        w_qkv, w_main = fw["w_in"][:, :QKV_WIDTH], fw["w_in"][:, QKV_WIDTH:]
        dh_qkv = _matmul_nt(dzqkv, w_qkv, name=f"l{l}_mix_bwd_qkv")
        dy, gs["mix_pre_g"] = _matmul_nt_norm_bwd(dzmain, w_main, x_mix, row("mix_pre_g", l), dy, dh_qkv,
                                                   name=f"l{l}_mix_bwd_in")
        gb["w_in"] = jnp.concatenate([_matmul_tn(hm, dzqkv, name=f"l{l}_dw_in_qkv"),
                                      _matmul_tn(hm, dzmain, name=f"l{l}_dw_in_main")], axis=1)

        dy, gb["ffn1_w1"], gb["ffn1_w2"], gs["ffn1_pre_g"], gs["ffn1_post_g"] = _ffn_bwd(
            dy, sv["ffn1"], row("ffn1_pre_g", l), fw["ffn1_w1"], fw["ffn1_w2"], row("ffn1_post_g", l), f"l{l}_ffn1")
        big_grads[l], small_grads[l] = gb, gs
    grad_x = dy.reshape(x.shape)

    def shard_of(n, g, s):
        k = shard_shapes[n][1 if n in COL_SHARDED else 0]
        return g[:, s * k:(s + 1) * k] if n in COL_SHARDED else g[s * k:(s + 1) * k]

    gr = jnp.stack([jnp.concatenate([_pack_layer({n: shard_of(n, big_grads[l][n], s) for n in BIG}, D, F32)
                                     for l in range(L)], axis=0) for s in range(N_CHIPS)])
    Rh = R // 2
    cx, cy, cc = (lax.axis_index(a) for a in AXES)
    from_sibling = _sibling_exchange(gr, name="grads_sibling_exchange")
    pair = _sum_terms([(gr, None, cc * Rh), (from_sibling, None, 0)], Rh, N_CHIPS, name="grads_pair_sum")
    from_chips = _chip_scatter(pair, name="grads_chip_scatter")
    total_half = _sum_terms([(pair, 2 * cx + cy, 0)] + [(from_chips, j, 0) for j in range(N_CHIPS - 1)], Rh, 1,
                            name="grads_chip_sum")[0]
    reduced = _sibling_join(total_half, name="grads_sibling_join")

    grads = {n: [] for n in WEIGHTS}
    for l in range(L):
        per = _unpack_layer(reduced[l * rows_per_layer:(l + 1) * rows_per_layer], shard_shapes, D)
        for n in BIG:
            grads[n].append(per[n])

    def small_rows(gs):
        parts = []
        for n in SMALL:
            flat = gs[n].reshape(-1)
            pad = (-flat.shape[0]) % D
            parts.append(jnp.pad(flat, (0, pad)).reshape(-1, D))
        return jnp.concatenate(parts, axis=0)

    spack = jnp.concatenate([small_rows(small_grads[l]) for l in range(L)], axis=0)
    n_small = spack.shape[0]
    pad_rows = (-n_small) % 8
    spack = jnp.pad(spack, ((0, pad_rows), (0, 0)))
    everyone = _gather_all(spack, name="small_grads_gather")
    ssum = _sum_terms([(everyone, d, 0) for d in range(N_DEV)], spack.shape[0], 1, name="small_grads_sum")[0]
    per_layer = n_small // L
    for l in range(L):
        r0 = l * per_layer
        for n in SMALL:
            shp = W[n].shape[1:]
            size = math.prod(shp)
            nr = -(-size // D)
            grads[n].append(ssum[r0:r0 + nr].reshape(-1)[:size].reshape(shp))
            r0 += nr
    grads = {n: jnp.stack(grads[n]) for n in WEIGHTS}

    delta, new_m, new_v = {}, {}, {}
    for n in WEIGHTS:
        delta[n], new_m[n], new_v[n] = _adamw(W[n], grads[n], M[n], V[n], name=f"adamw_{n}")

    return (loss, grad_x, *[grads[n] for n in WEIGHTS], *[delta[n] for n in WEIGHTS],
            *[new_m[n] for n in WEIGHTS], *[new_v[n] for n in WEIGHTS])
```

```python
import functools
import math

import jax
import jax.numpy as jnp
from jax import lax
from jax.experimental import pallas as pl
from jax.experimental.pallas import tpu as pltpu

F32, BF16 = jnp.float32, jnp.bfloat16
SDS = jax.ShapeDtypeStruct
MESH = pl.DeviceIdType.MESH
AXES = ("x", "y", "c")

HEAD_DIM = 64
N_Q_HEADS = 16
N_KV_HEADS = 2
Q_PER_KV = N_Q_HEADS // N_KV_HEADS
ATTN_WIDTH = N_Q_HEADS * HEAD_DIM
KV_WIDTH = N_KV_HEADS * HEAD_DIM
ATTN_BLOCK = 128
SGU_CHUNK = 128
SGU_GROUPS = 8
SGU_WIDTH = SGU_GROUPS * 128
QKV_WIDTH = ATTN_WIDTH + 2 * KV_WIDTH
RMS_EPS = 1e-6
LN_EPS = 1e-5
MASK_VALUE = -1e30
ATTN_SCALE = 1.0 / math.sqrt(HEAD_DIM)

ADAM_LR, ADAM_B1, ADAM_B2, ADAM_EPS, ADAM_WD, ADAM_STEP = 0.001, 0.9, 0.999, 1e-08, 0.01, 10

N_CHIPS = 4
N_DEV = 8

VMEM_LIMIT_BYTES = 56 * 1024 * 1024
LANES = 128
SUBLANES_BF16 = 16

TM_NORM_MATMUL = 1024
TM_ROW = 512
TM_FFN_BWD = 256
TT_REDUCE = 1024
TQ_ATTN = 512
TS_SGU = 512


def _tile(n, pref, mult):
    t = (min(pref, n) // mult) * mult
    while t >= mult:
        if n % t == 0:
            return t
        t -= mult
    return n


def _params(*sem):
    return pltpu.CompilerParams(dimension_semantics=sem, vmem_limit_bytes=VMEM_LIMIT_BYTES)


def _dot(a, b):
    return jnp.dot(a, b, preferred_element_type=F32)


def _dot_nt(a, b):
    return lax.dot_general(a, b, (((1,), (1,)), ((), ())), preferred_element_type=F32)


def _dot_tn(a, b):
    return lax.dot_general(a, b, (((0,), (0,)), ((), ())), preferred_element_type=F32)


def _sigmoid(x):
    return 1.0 / (1.0 + jnp.exp(-x))


def _rms_stats(xf):
    r = lax.rsqrt(jnp.mean(xf * xf, axis=-1, keepdims=True) + RMS_EPS)
    return r, xf * r


def _rms_bwd(xf, g, dy):
    r, xh = _rms_stats(xf)
    dyg = dy * g
    dx = r * (dyg - xh * jnp.mean(dyg * xh, axis=-1, keepdims=True))
    return dx, jnp.sum(dy * xh, axis=0, keepdims=True)


def _gelu_parts(x):
    cdf = 0.5 * (1.0 + lax.erf(x * (1.0 / math.sqrt(2.0))))
    return cdf


def _gelu(x):
    return x * _gelu_parts(x)


def _gelu_grad(x):
    return _gelu_parts(x) + x * jnp.exp(-0.5 * x * x) * (1.0 / math.sqrt(2.0 * math.pi))


def _norm_matmul(x, g, w, *, name, with_h):
    T, D = x.shape
    N = w.shape[1]
    tm = _tile(T, TM_NORM_MATMUL, SUBLANES_BF16)
    tn = _tile(N, 1408, LANES)

    def body(x_ref, g_ref, w_ref, a_ref, *rest):
        h_sc = rest[-1]

        @pl.when(pl.program_id(1) == 0)
        def _():
            _, xh = _rms_stats(x_ref[...])
            h = (xh * g_ref[...]).astype(BF16)
            h_sc[...] = h
            if with_h:
                rest[0][...] = h

        a_ref[...] = _dot(h_sc[...], w_ref[...]).astype(BF16)

    out_specs = [pl.BlockSpec((tm, tn), lambda i, j: (i, j))]
    out_shape = [SDS((T, N), BF16)]
    if with_h:
        out_specs.append(pl.BlockSpec((tm, D), lambda i, j: (i, 0)))
        out_shape.append(SDS((T, D), BF16))
    return pl.pallas_call(
        body, name=name, grid=(T // tm, N // tn),
        in_specs=[pl.BlockSpec((tm, D), lambda i, j: (i, 0)),
                  pl.BlockSpec((1, D), lambda i, j: (0, 0)),
                  pl.BlockSpec((D, tn), lambda i, j: (0, j))],
        out_specs=out_specs, out_shape=out_shape,
        scratch_shapes=[pltpu.VMEM((tm, D), BF16)],
        compiler_params=_params("parallel", "arbitrary"),
    )(x, g, w)


def _norm_matmul_t(x, g, wt, *, name):
    T, D = x.shape
    N = wt.shape[0]
    tm = _tile(T, TM_ROW, LANES)

    def body(x_ref, g_ref, w_ref, a_ref, h_ref):
        _, xh = _rms_stats(x_ref[...])
        h = (xh * g_ref[...]).astype(BF16)
        h_ref[...] = h
        a_ref[...] = _dot_nt(w_ref[...], h).astype(BF16)

    return pl.pallas_call(
        body, name=name, grid=(T // tm,),
        in_specs=[pl.BlockSpec((tm, D), lambda i: (i, 0)), pl.BlockSpec((1, D), lambda i: (0, 0)),
                  pl.BlockSpec((N, D), lambda i: (0, 0))],
        out_specs=[pl.BlockSpec((N, tm), lambda i: (0, i)), pl.BlockSpec((tm, D), lambda i: (i, 0))],
        out_shape=[SDS((N, T), BF16), SDS((T, D), BF16)],
        compiler_params=_params("parallel"),
    )(x, g, wt)


def _matmul_tokens(at, b, *, name):
    K, T = at.shape
    N = b.shape[1]
    tt = _tile(T, TT_REDUCE, LANES)

    def body(a_ref, b_ref, o_ref):
        @pl.when(pl.program_id(0) == 0)
        def _():
            o_ref[...] = jnp.zeros_like(o_ref)

        o_ref[...] += _dot(a_ref[...], b_ref[...])

    return pl.pallas_call(
        body, name=name, grid=(T // tt,),
        in_specs=[pl.BlockSpec((K, tt), lambda t: (0, t)), pl.BlockSpec((tt, N), lambda t: (t, 0))],
        out_specs=pl.BlockSpec((K, N), lambda t: (0, 0)),
        out_shape=SDS((K, N), F32),
        compiler_params=_params("arbitrary"),
    )(at, b)


def _matmul_tn_rows(dat, wt, *, name):
    N, T = dat.shape
    D = wt.shape[1]
    tm = _tile(T, TM_ROW, LANES)

    def body(da_ref, w_ref, o_ref):
        o_ref[...] = _dot_tn(da_ref[...], w_ref[...])

    return pl.pallas_call(
        body, name=name, grid=(T // tm,),
        in_specs=[pl.BlockSpec((N, tm), lambda i: (0, i)), pl.BlockSpec((N, D), lambda i: (0, 0))],
        out_specs=pl.BlockSpec((tm, D), lambda i: (i, 0)),
        out_shape=SDS((T, D), F32),
        compiler_params=_params("parallel"),
    )(dat, wt)


def _ff_chunk(F):
    return F if F <= 1408 else F // 2


def _swiglu_out(a, w2, x, g_post, *, name):
    T, F2 = a.shape
    F = F2 // 2
    D = x.shape[1]
    tm = _tile(T, TM_ROW, SUBLANES_BF16)
    fc = _ff_chunk(F)

    def body(a_ref, w_ref, x_ref, g_ref, xn_ref, o_ref):
        acc = None
        for c0 in range(0, F, fc):
            gt = a_ref[:, c0:c0 + fc].astype(F32)
            ut = a_ref[:, F + c0:F + c0 + fc].astype(F32)
            s = (gt * _sigmoid(gt) * ut).astype(BF16)
            part = _dot(s, w_ref[c0:c0 + fc, :])
            acc = part if acc is None else acc + part
        o_ref[...] = acc.astype(BF16)
        _, oh = _rms_stats(acc)
        xn_ref[...] = x_ref[...] + 0.5 * (oh * g_ref[...])

    return pl.pallas_call(
        body, name=name, grid=(T // tm,),
        in_specs=[pl.BlockSpec((tm, F2), lambda i: (i, 0)),
                  pl.BlockSpec((F, D), lambda i: (0, 0)),
                  pl.BlockSpec((tm, D), lambda i: (i, 0)),
                  pl.BlockSpec((1, D), lambda i: (0, 0))],
        out_specs=[pl.BlockSpec((tm, D), lambda i: (i, 0)), pl.BlockSpec((tm, D), lambda i: (i, 0))],
        out_shape=[SDS((T, D), F32), SDS((T, D), BF16)],
        compiler_params=_params("parallel"),
    )(a, w2, x, g_post)


def _ffn_bwd_hidden(dy, o, g_post, a, w2, *, name):
    T, F2 = a.shape
    F = F2 // 2
    D = dy.shape[1]
    tm = _tile(T, TM_FFN_BWD, SUBLANES_BF16)
    fc = _ff_chunk(F)

    def body(dy_ref, o_ref, g_ref, a_ref, w_ref, da_ref, s_ref, do_ref, dg_ref):
        @pl.when(pl.program_id(0) == 0)
        def _():
            dg_ref[...] = jnp.zeros_like(dg_ref)

        do, dg = _rms_bwd(o_ref[...].astype(F32), g_ref[...], 0.5 * dy_ref[...])
        dg_ref[...] += dg
        dob = do.astype(BF16)
        do_ref[...] = dob
        for c0 in range(0, F, fc):
            ds = _dot_nt(dob, w_ref[c0:c0 + fc, :])
            gt = a_ref[:, c0:c0 + fc].astype(F32)
            ut = a_ref[:, F + c0:F + c0 + fc].astype(F32)
            sg = _sigmoid(gt)
            sl = gt * sg
            s_ref[:, c0:c0 + fc] = (sl * ut).astype(BF16)
            da_ref[:, c0:c0 + fc] = (ds * ut * (sg * (1.0 + gt * (1.0 - sg)))).astype(BF16)
            da_ref[:, F + c0:F + c0 + fc] = (ds * sl).astype(BF16)

    row = lambda w: pl.BlockSpec((tm, w), lambda i: (i, 0))
    return pl.pallas_call(
        body, name=name, grid=(T // tm,),
        in_specs=[row(D), row(D), pl.BlockSpec((1, D), lambda i: (0, 0)), row(F2),
                  pl.BlockSpec((F, D), lambda i: (0, 0))],
        out_specs=[row(F2), row(F), row(D), pl.BlockSpec((1, D), lambda i: (0, 0))],
        out_shape=[SDS((T, F2), BF16), SDS((T, F), BF16), SDS((T, D), BF16), SDS((1, D), F32)],
        compiler_params=_params("arbitrary"),
    )(dy, o, g_post, a, w2)


def _matmul_tn(a, b, *, name):
    T, K = a.shape
    N = b.shape[1]
    tk = _tile(K, 1408, LANES)
    tn = _tile(N, 1408, LANES)
    tt = _tile(T, TT_REDUCE, SUBLANES_BF16)

    def body(a_ref, b_ref, o_ref):
        @pl.when(pl.program_id(2) == 0)
        def _():
            o_ref[...] = jnp.zeros_like(o_ref)

        o_ref[...] += _dot_tn(a_ref[...], b_ref[...])

    return pl.pallas_call(
        body, name=name, grid=(K // tk, N // tn, T // tt),
        in_specs=[pl.BlockSpec((tt, tk), lambda k, n, t: (t, k)),
                  pl.BlockSpec((tt, tn), lambda k, n, t: (t, n))],
        out_specs=pl.BlockSpec((tk, tn), lambda k, n, t: (k, n)),
        out_shape=SDS((K, N), F32),
        compiler_params=_params("parallel", "parallel", "arbitrary"),
    )(a, b)


def _matmul_nt_norm_bwd(da, w, x, g, dy, init, *, name):
    T, N = da.shape
    D = w.shape[0]
    tm = _tile(T, TM_ROW, SUBLANES_BF16)
    tn = _tile(N, 1408, LANES)
    nj = N // tn
    has_init = init is not None

    def body(da_ref, w_ref, x_ref, g_ref, dy_ref, *rest):
        init_ref = rest[0] if has_init else None
        dx_ref, dg_ref, acc = rest[-3:]
        i, j = pl.program_id(0), pl.program_id(1)

        @pl.when(j == 0)
        def _():
            acc[...] = init_ref[...] if has_init else jnp.zeros_like(acc)

        acc[...] += _dot_nt(da_ref[...], w_ref[...])

        @pl.when(j == nj - 1)
        def _():
            @pl.when(i == 0)
            def _():
                dg_ref[...] = jnp.zeros_like(dg_ref)

            dx, dg = _rms_bwd(x_ref[...], g_ref[...], acc[...])
            dx_ref[...] = dy_ref[...] + dx
            dg_ref[...] += dg

    row = pl.BlockSpec((tm, D), lambda i, j: (i, 0))
    vec = pl.BlockSpec((1, D), lambda i, j: (0, 0))
    in_specs = [pl.BlockSpec((tm, tn), lambda i, j: (i, j)), pl.BlockSpec((D, tn), lambda i, j: (0, j)),
                row, vec, row]
    args = [da, w, x, g, dy]
    if has_init:
        in_specs.append(row)
        args.append(init)
    return pl.pallas_call(
        body, name=name, grid=(T // tm, nj), in_specs=in_specs,
        out_specs=[row, vec], out_shape=[SDS((T, D), F32), SDS((1, D), F32)],
        scratch_shapes=[pltpu.VMEM((tm, D), F32)],
        compiler_params=_params("arbitrary", "arbitrary"),
    )(*args)


GROUP_LANES = Q_PER_KV * ATTN_BLOCK


def _attn_mask_t(first):
    kj = lax.broadcasted_iota(jnp.int32, (2 * ATTN_BLOCK, ATTN_BLOCK), 0)
    qi = lax.broadcasted_iota(jnp.int32, (2 * ATTN_BLOCK, ATTN_BLOCK), 1)
    rel = qi + ATTN_BLOCK - kj
    band = (rel >= 0) & (rel < ATTN_BLOCK)
    if first is False:
        return band
    return band & ((kj >= ATTN_BLOCK) | jnp.logical_not(first))


def _attn_probs_t(st, valid, sink):
    s = jnp.where(valid, st, MASK_VALUE)
    m = jnp.maximum(jnp.max(s, axis=0, keepdims=True), sink)
    p = jnp.exp(s - m)
    es = jnp.exp(sink - m)
    inv = 1.0 / (jnp.sum(p, axis=0, keepdims=True) + es)
    return p * inv, es * inv


def _attn_specs(tq, tile_of):
    nb = tq // ATTN_BLOCK
    krow, vrow = ATTN_WIDTH // KV_WIDTH, ATTN_WIDTH // KV_WIDTH + 1
    halo = lambda r: pl.BlockSpec((KV_WIDTH, ATTN_BLOCK), lambda t: (r, jnp.maximum(tile_of(t) * nb - 1, 0)))
    return [pl.BlockSpec((ATTN_WIDTH, tq), lambda t: (0, tile_of(t))),
            pl.BlockSpec((KV_WIDTH, tq), lambda t: (krow, tile_of(t))),
            pl.BlockSpec((KV_WIDTH, tq), lambda t: (vrow, tile_of(t))),
            halo(krow), halo(vrow)]


def _head_rows(g, r):
    h = g * Q_PER_KV + r
    return h, slice(h * HEAD_DIM, (h + 1) * HEAD_DIM)


def _group_stack(ref, g, cols):
    return jnp.concatenate([ref[_head_rows(g, r)[1], cols] for r in range(Q_PER_KV)], axis=1)


def _attn_fwd(zt, sinks, *, name):
    T = zt.shape[1]
    tq = _tile(T, TQ_ATTN, ATTN_BLOCK)
    nb = tq // ATTN_BLOCK

    def body(q_ref, k_ref, v_ref, kh_ref, vh_ref, s_ref, o_ref, kf, vf, pt):
        kf[:, 0:ATTN_BLOCK] = kh_ref[...]
        kf[:, ATTN_BLOCK:] = k_ref[...]
        vf[:, 0:ATTN_BLOCK] = vh_ref[...]
        vf[:, ATTN_BLOCK:] = v_ref[...]
        for b in range(nb):
            cols = slice(b * ATTN_BLOCK, (b + 1) * ATTN_BLOCK)
            win = slice(b * ATTN_BLOCK, (b + 2) * ATTN_BLOCK)
            valid = _attn_mask_t((pl.program_id(0) == 0) if b == 0 else False)
            for g in range(N_KV_HEADS):
                gr = slice(g * HEAD_DIM, (g + 1) * HEAD_DIM)
                st = _dot_tn(kf[gr, win], _group_stack(q_ref, g, cols)) * ATTN_SCALE
                for r in range(Q_PER_KV):
                    h, _ = _head_rows(g, r)
                    sl = slice(r * ATTN_BLOCK, (r + 1) * ATTN_BLOCK)
                    probs, _ = _attn_probs_t(st[:, sl], valid, s_ref[h])
                    pt[:, sl] = probs.astype(BF16)
                ot = _dot(vf[gr, win], pt[...])
                for r in range(Q_PER_KV):
                    o_ref[_head_rows(g, r)[1], cols] = ot[:, r * ATTN_BLOCK:(r + 1) * ATTN_BLOCK].astype(BF16)

    return pl.pallas_call(
        body, name=name, grid=(T // tq,),
        in_specs=_attn_specs(tq, lambda t: t) + [pl.BlockSpec(memory_space=pltpu.SMEM)],
        out_specs=pl.BlockSpec((ATTN_WIDTH, tq), lambda t: (0, t)),
        out_shape=SDS((ATTN_WIDTH, T), BF16),
        scratch_shapes=[pltpu.VMEM((KV_WIDTH, tq + ATTN_BLOCK), BF16)] * 2
        + [pltpu.VMEM((2 * ATTN_BLOCK, GROUP_LANES), BF16)],
        compiler_params=_params("parallel"),
    )(zt, zt, zt, zt, zt, sinks)


def _attn_bwd(zt, sinks, dot_, *, name):
    T = zt.shape[1]
    tq = _tile(T, TQ_ATTN, ATTN_BLOCK)
    nb = tq // ATTN_BLOCK
    nt = T // tq
    tile_of = lambda t: nt - 1 - t

    def body(q_ref, k_ref, v_ref, kh_ref, vh_ref, do_ref, s_ref, dz_ref, dsink_ref, kf, vf, dkf, dvf, carry, pt, dst):
        t = pl.program_id(0)

        @pl.when(t == 0)
        def _():
            carry[...] = jnp.zeros_like(carry)
            dsink_ref[...] = jnp.zeros_like(dsink_ref)

        kf[:, 0:ATTN_BLOCK] = kh_ref[...]
        kf[:, ATTN_BLOCK:] = k_ref[...]
        vf[:, 0:ATTN_BLOCK] = vh_ref[...]
        vf[:, ATTN_BLOCK:] = v_ref[...]
        dkf[...] = jnp.zeros_like(dkf)
        dvf[...] = jnp.zeros_like(dvf)
        dkf[:, tq:] = carry[0:KV_WIDTH, :]
        dvf[:, tq:] = carry[KV_WIDTH:, :]
        lane = lax.broadcasted_iota(jnp.int32, (1, LANES), 1)
        dsink = jnp.zeros((1, LANES), F32)
        for b in range(nb):
            cols = slice(b * ATTN_BLOCK, (b + 1) * ATTN_BLOCK)
            win = slice(b * ATTN_BLOCK, (b + 2) * ATTN_BLOCK)
            valid = _attn_mask_t((t == nt - 1) if b == 0 else False)
            for g in range(N_KV_HEADS):
                gr = slice(g * HEAD_DIM, (g + 1) * HEAD_DIM)
                kt2, vt2 = kf[gr, win], vf[gr, win]
                qst = _group_stack(q_ref, g, cols)
                dost = _group_stack(do_ref, g, cols)
                st = _dot_tn(kt2, qst) * ATTN_SCALE
                dpt = _dot_tn(vt2, dost)
                for r in range(Q_PER_KV):
                    h, _ = _head_rows(g, r)
                    sl = slice(r * ATTN_BLOCK, (r + 1) * ATTN_BLOCK)
                    probs, psink = _attn_probs_t(st[:, sl], valid, s_ref[h])
                    dp = dpt[:, sl]
                    delta = jnp.sum(probs * dp, axis=0, keepdims=True)
                    pt[:, sl] = probs.astype(BF16)
                    dst[:, sl] = (probs * (dp - delta)).astype(BF16)
                    dsink = dsink + jnp.where(lane == h, -jnp.sum(psink * delta), 0.0)
                dqt = _dot(kt2, dst[...]) * ATTN_SCALE
                for r in range(Q_PER_KV):
                    dz_ref[_head_rows(g, r)[1], cols] = dqt[:, r * ATTN_BLOCK:(r + 1) * ATTN_BLOCK].astype(BF16)
                dkf[gr, win] += _dot_nt(qst, dst[...]) * ATTN_SCALE
                dvf[gr, win] += _dot_nt(dost, pt[...])
        dz_ref[ATTN_WIDTH:ATTN_WIDTH + KV_WIDTH, :] = dkf[:, ATTN_BLOCK:].astype(BF16)
        dz_ref[ATTN_WIDTH + KV_WIDTH:, :] = dvf[:, ATTN_BLOCK:].astype(BF16)
        carry[0:KV_WIDTH, :] = dkf[:, 0:ATTN_BLOCK]
        carry[KV_WIDTH:, :] = dvf[:, 0:ATTN_BLOCK]
        dsink_ref[...] += dsink

    return pl.pallas_call(
        body, name=name, grid=(nt,),
        in_specs=_attn_specs(tq, tile_of) + [pl.BlockSpec((ATTN_WIDTH, tq), lambda t: (0, tile_of(t))),
                                             pl.BlockSpec(memory_space=pltpu.SMEM)],
        out_specs=[pl.BlockSpec((QKV_WIDTH, tq), lambda t: (0, tile_of(t))),
                   pl.BlockSpec((8, LANES), lambda t: (0, 0))],
        out_shape=[SDS((QKV_WIDTH, T), BF16), SDS((8, LANES), F32)],
        scratch_shapes=[pltpu.VMEM((KV_WIDTH, tq + ATTN_BLOCK), BF16)] * 2
        + [pltpu.VMEM((KV_WIDTH, tq + ATTN_BLOCK), F32)] * 2 + [pltpu.VMEM((2 * KV_WIDTH, ATTN_BLOCK), F32)]
        + [pltpu.VMEM((2 * ATTN_BLOCK, GROUP_LANES), BF16)] * 2,
        compiler_params=_params("arbitrary"),
    )(zt, zt, zt, zt, zt, dot_, sinks)


def _layer_norm_stats(v):
    mu = jnp.mean(v, axis=-1, keepdims=True)
    xc = v - mu
    rstd = lax.rsqrt(jnp.mean(xc * xc, axis=-1, keepdims=True) + LN_EPS)
    return rstd, xc * rstd


def _sgu_fwd(zmain, ln_g, ln_b, wm, bias, *, name):
    T = zmain.shape[0]
    ts = _tile(T, TS_SGU, SGU_CHUNK)

    def body(u_ref, v_ref, g_ref, b_ref, w_ref, bias_ref, y_ref):
        u = _gelu(u_ref[...].astype(F32))
        _, vh = _layer_norm_stats(_gelu(v_ref[...].astype(F32)))
        vn = (vh * g_ref[...] + b_ref[...]).astype(BF16)
        for ch in range(ts // SGU_CHUNK):
            rows = slice(ch * SGU_CHUNK, (ch + 1) * SGU_CHUNK)
            for g in range(SGU_GROUPS):
                cols = slice(g * 128, (g + 1) * 128)
                s = _dot(w_ref[g], vn[rows, cols]) + bias_ref[g]
                y_ref[rows, cols] = (u[rows, cols] * s).astype(BF16)

    full = lambda shape: pl.BlockSpec(shape, lambda i: (0,) * len(shape))
    return pl.pallas_call(
        body, name=name, grid=(T // ts,),
        in_specs=[pl.BlockSpec((ts, SGU_WIDTH), lambda i: (i, 0)), pl.BlockSpec((ts, SGU_WIDTH), lambda i: (i, 1)),
                  full((1, SGU_WIDTH)), full((1, SGU_WIDTH)), full(wm.shape), full(bias.shape)],
        out_specs=pl.BlockSpec((ts, SGU_WIDTH), lambda i: (i, 0)),
        out_shape=SDS((T, SGU_WIDTH), BF16),
        compiler_params=_params("parallel"),
    )(zmain, zmain, ln_g, ln_b, wm, bias)


def _sgu_bwd(zmain, dzmain, dy, ln_g, ln_b, wm, wmt, bias, *, name):
    T = zmain.shape[0]
    ts = _tile(T, TS_SGU, SGU_CHUNK)

    def body(u_ref, v_ref, dy_ref, g_ref, b_ref, w_ref, wt_ref, bias_ref, _, dz_ref, dw_ref, db_ref, dlg_ref, dlb_ref,
             dvn):
        @pl.when(pl.program_id(0) == 0)
        def _():
            dw_ref[...] = jnp.zeros_like(dw_ref)
            db_ref[...] = jnp.zeros_like(db_ref)
            dlg_ref[...] = jnp.zeros_like(dlg_ref)
            dlb_ref[...] = jnp.zeros_like(dlb_ref)

        us = u_ref[...].astype(F32)
        vs = v_ref[...].astype(F32)
        u = _gelu(us)
        rstd, vh = _layer_norm_stats(_gelu(vs))
        vn = (vh * g_ref[...] + b_ref[...]).astype(BF16)
        causal = (lax.broadcasted_iota(jnp.int32, (SGU_CHUNK, SGU_CHUNK), 0)
                  >= lax.broadcasted_iota(jnp.int32, (SGU_CHUNK, SGU_CHUNK), 1))
        lane = lax.broadcasted_iota(jnp.int32, (SGU_CHUNK, LANES), 1)
        db = jnp.zeros((SGU_CHUNK, LANES), F32)
        for ch in range(ts // SGU_CHUNK):
            rows = slice(ch * SGU_CHUNK, (ch + 1) * SGU_CHUNK)
            for g in range(SGU_GROUPS):
                cols = slice(g * 128, (g + 1) * 128)
                vng = vn[rows, cols]
                s = _dot(w_ref[g], vng) + bias_ref[g]
                dyf = dy_ref[rows, cols].astype(F32)
                dz_ref[rows, cols] = (dyf * s * _gelu_grad(us[rows, cols])).astype(BF16)
                dsf = dyf * u[rows, cols]
                dsb = dsf.astype(BF16)
                dvn[rows, cols] = _dot(wt_ref[g], dsb)
                dw_ref[g] += jnp.where(causal, _dot_nt(dsb, vng), 0.0)
                db = db + jnp.where(lane == g, jnp.sum(dsf, axis=1, keepdims=True), 0.0)
        db_ref[...] += db
        dvnf = dvn[...]
        dlg_ref[...] += jnp.sum(dvnf * vh, axis=0, keepdims=True)
        dlb_ref[...] += jnp.sum(dvnf, axis=0, keepdims=True)
        dvh = dvnf * g_ref[...]
        dv = rstd * (dvh - jnp.mean(dvh, axis=-1, keepdims=True) - vh * jnp.mean(dvh * vh, axis=-1, keepdims=True))
        dz_ref[:, SGU_WIDTH:] = (dv * _gelu_grad(vs)).astype(BF16)

    full = lambda shape: pl.BlockSpec(shape, lambda i: (0,) * len(shape))
    vec = full((1, SGU_WIDTH))
    return pl.pallas_call(
        body, name=name, grid=(T // ts,),
        in_specs=[pl.BlockSpec((ts, SGU_WIDTH), lambda i: (i, 0)), pl.BlockSpec((ts, SGU_WIDTH), lambda i: (i, 1)),
                  pl.BlockSpec((ts, SGU_WIDTH), lambda i: (i, 0)), vec, vec, full(wm.shape), full(wm.shape),
                  full(bias.shape), pl.BlockSpec(memory_space=pl.ANY)],
        out_specs=[pl.BlockSpec((ts, 2 * SGU_WIDTH), lambda i: (i, 0)), full(wm.shape),
                   full((SGU_CHUNK, LANES)), vec, vec],
        out_shape=[SDS(dzmain.shape, BF16), SDS(wm.shape, F32), SDS((SGU_CHUNK, LANES), F32),
                   SDS((1, SGU_WIDTH), F32), SDS((1, SGU_WIDTH), F32)],
        scratch_shapes=[pltpu.VMEM((ts, SGU_WIDTH), F32)],
        input_output_aliases={8: 0},
        compiler_params=_params("arbitrary"),
    )(zmain, zmain, dy, ln_g, ln_b, wm, wmt, bias, dzmain)


def _merge_fwd(y_attn_t, y_sgu, zmain, w_a, w_s, w_o, x, g_post, *, name):
    T, D = x.shape
    tm = _tile(T, TM_ROW, LANES)

    def body(ya_ref, ys_ref, ga_ref, gb_ref, wa_ref, ws_ref, wo_ref, x_ref, g_ref, xn_ref, pa_ref, ps_ref, o_ref):
        pa = _dot_tn(ya_ref[...], wa_ref[...])
        ps = _dot(ys_ref[...], ws_ref[...])
        pa_ref[...] = pa.astype(BF16)
        ps_ref[...] = ps.astype(BF16)
        merged = _sigmoid(ga_ref[...].astype(F32)) * pa + _sigmoid(gb_ref[...].astype(F32)) * ps
        out = _dot(merged.astype(BF16), wo_ref[...])
        o_ref[...] = out.astype(BF16)
        _, oh = _rms_stats(out)
        xn_ref[...] = x_ref[...] + oh * g_ref[...]

    row = lambda col: pl.BlockSpec((tm, D), lambda i: (i, col))
    wfull = pl.BlockSpec((D, D), lambda i: (0, 0))
    return pl.pallas_call(
        body, name=name, grid=(T // tm,),
        in_specs=[pl.BlockSpec((D, tm), lambda i: (0, i)), row(0), row(2), row(3), wfull, wfull, wfull, row(0),
                  pl.BlockSpec((1, D), lambda i: (0, 0))],
        out_specs=[row(0)] * 4,
        out_shape=[SDS((T, D), F32), SDS((T, D), BF16), SDS((T, D), BF16), SDS((T, D), BF16)],
        compiler_params=_params("parallel"),
    )(y_attn_t, y_sgu, zmain, zmain, w_a, w_s, w_o, x, g_post)


def _merge_bwd(dy, out, g_post, pa, ps, zmain, w_a, w_s, w_o, *, name):
    T, D = dy.shape
    tm = _tile(T, TM_ROW, LANES)

    def body(dy_ref, o_ref, g_ref, pa_ref, ps_ref, ga_ref, gb_ref, wa_ref, ws_ref, wo_ref,
             dz_ref, dout_ref, mg_ref, dpa_ref, dps_ref, dya_ref, dys_ref, dg_ref):
        @pl.when(pl.program_id(0) == 0)
        def _():
            dg_ref[...] = jnp.zeros_like(dg_ref)

        dout, dg = _rms_bwd(o_ref[...].astype(F32), g_ref[...], dy_ref[...])
        dg_ref[...] += dg
        doutb = dout.astype(BF16)
        dout_ref[...] = doutb
        dm = _dot_nt(doutb, wo_ref[...])
        pa = pa_ref[...].astype(F32)
        ps = ps_ref[...].astype(F32)
        sa = _sigmoid(ga_ref[...].astype(F32))
        sb = _sigmoid(gb_ref[...].astype(F32))
        mg_ref[...] = (sa * pa + sb * ps).astype(BF16)
        dpa = (dm * sa).astype(BF16)
        dps = (dm * sb).astype(BF16)
        dpa_ref[...] = dpa
        dps_ref[...] = dps
        dz_ref[:, 0:D] = (dm * pa * sa * (1.0 - sa)).astype(BF16)
        dz_ref[:, D:] = (dm * ps * sb * (1.0 - sb)).astype(BF16)
        dya_ref[...] = _dot_nt(wa_ref[...], dpa).astype(BF16)
        dys_ref[...] = _dot_nt(dps, ws_ref[...]).astype(BF16)

    row = lambda col: pl.BlockSpec((tm, D), lambda i: (i, col))
    wfull = pl.BlockSpec((D, D), lambda i: (0, 0))
    vec = pl.BlockSpec((1, D), lambda i: (0, 0))
    act = SDS((T, D), BF16)
    return pl.pallas_call(
        body, name=name, grid=(T // tm,),
        in_specs=[row(0), row(0), vec, row(0), row(0), row(2), row(3), wfull, wfull, wfull],
        out_specs=[pl.BlockSpec((tm, 2 * D), lambda i: (i, 1))] + [row(0)] * 4
        + [pl.BlockSpec((D, tm), lambda i: (0, i)), row(0), vec],
        out_shape=[SDS(zmain.shape, BF16)] + [act] * 4 + [SDS((D, T), BF16), act, SDS((1, D), F32)],
        compiler_params=_params("arbitrary"),
    )(dy, out, g_post, pa, ps, zmain, zmain, w_a, w_s, w_o)


def _loss_head(y, target, *, name):
    T, D = y.shape
    tm = _tile(T, TM_ROW, 8)

    def body(y_ref, t_ref, dy_ref, l_ref):
        @pl.when(pl.program_id(0) == 0)
        def _():
            l_ref[...] = jnp.zeros_like(l_ref)

        e = y_ref[...] - t_ref[...]
        dy_ref[...] = e * (1.0 / D)
        l_ref[...] += jnp.sum(jnp.mean(e * e, axis=-1, keepdims=True))

    row = pl.BlockSpec((tm, D), lambda i: (i, 0))
    return pl.pallas_call(
        body, name=name, grid=(T // tm,), in_specs=[row, row],
        out_specs=[row, pl.BlockSpec((8, LANES), lambda i: (0, 0))],
        out_shape=[SDS((T, D), F32), SDS((8, LANES), F32)],
        compiler_params=_params("arbitrary"),
    )(y, target)


def _adamw(w, g, m, v, *, name):
    shape = w.shape
    cols = shape[-1]
    rows = w.size // cols
    w2, g2, m2, v2 = (t.reshape(rows, cols) for t in (w, g, m, v))
    tr = _tile(rows, max(8, (256 * 1024) // cols // 8 * 8), 8)

    def body(w_ref, g_ref, m_ref, v_ref, d_ref, nm_ref, nv_ref):
        gg = g_ref[...]
        nm = ADAM_B1 * m_ref[...] + (1.0 - ADAM_B1) * gg
        nv = ADAM_B2 * v_ref[...] + (1.0 - ADAM_B2) * (gg * gg)
        m_hat = nm / (1.0 - ADAM_B1 ** ADAM_STEP)
        v_hat = nv / (1.0 - ADAM_B2 ** ADAM_STEP)
        d_ref[...] = -ADAM_LR * (m_hat / (jnp.sqrt(v_hat) + ADAM_EPS) + ADAM_WD * w_ref[...])
        nm_ref[...] = nm
        nv_ref[...] = nv

    blk = pl.BlockSpec((tr, cols), lambda i: (i, 0))
    outs = pl.pallas_call(
        body, name=name, grid=(rows // tr,), in_specs=[blk] * 4, out_specs=[blk] * 3,
        out_shape=[SDS((rows, cols), F32)] * 3, compiler_params=_params("parallel"),
    )(w2, g2, m2, v2)
    return tuple(o.reshape(shape) for o in outs)


def _sum_terms(terms, n_rows, n_lead, dtypes, *, name):
    cols = terms[0][0].shape[-1]
    tr = _tile(n_rows, 704 if len(terms) <= 4 else 256, SUBLANES_BF16)
    nblk = n_rows // tr
    n_out = len(dtypes)

    def body(*refs):
        acc = refs[0][...].astype(F32)
        for r in refs[1:-n_out]:
            acc = acc + r[...].astype(F32)
        for o_ref in refs[-n_out:]:
            o_ref[...] = acc.astype(o_ref.dtype)

    def spec(lead, first):
        return pl.BlockSpec((1, tr, cols), lambda a, i: (a if lead is None else lead(), first() * nblk + i, 0))

    out = pl.BlockSpec((1, tr, cols), lambda a, i: (a, i, 0))
    return pl.pallas_call(
        body, name=name, grid=(n_lead, nblk), in_specs=[spec(lead, first) for _, lead, first in terms],
        out_specs=[out] * n_out, out_shape=[SDS((n_lead, n_rows, cols), d) for d in dtypes],
        compiler_params=_params("arbitrary", "arbitrary"),
    )(*[a for a, _, _ in terms])


def _position():
    x, y, c = (lax.axis_index(a) for a in AXES)
    chips = [(1 - x, y), (x, 1 - y), (1 - x, 1 - y)]
    return x, y, c, chips


ANY = pl.BlockSpec(memory_space=pl.ANY)


def _remote(src, dst, send_sems, recv_sems, k, to):
    return pltpu.make_async_remote_copy(src_ref=src, dst_ref=dst, send_sem=send_sems.at[k], recv_sem=recv_sems.at[k],
                                        device_id=to, device_id_type=MESH)


def _gather_shards(pack, *, name):
    R, C = pack.shape
    Rh = R // 2

    def body(p_ref, o_ref, send_sems, recv_sems, local_sem):
        x, y, c, chips = _position()
        mine = pl.ds(c * Rh, Rh)
        own = pltpu.make_async_copy(p_ref, o_ref.at[2 * x + y], local_sem)
        own.start()
        sent = [_remote(p_ref.at[mine], o_ref.at[2 * x + y, mine], send_sems, recv_sems, j, (*chip, c))
                for j, chip in enumerate(chips)]
        for cp in sent:
            cp.start()
        passed = []
        for j, (cx, cy) in enumerate(chips):
            landed = o_ref.at[2 * cx + cy, mine]
            _remote(landed, landed, send_sems, recv_sems, j, (x, y, c)).wait_recv()
            cp = _remote(landed, landed, send_sems, recv_sems, 3 + j, (x, y, 1 - c))
            cp.start()
            passed.append(cp)
        for j, (cx, cy) in enumerate(chips):
            theirs = o_ref.at[2 * cx + cy, pl.ds((1 - c) * Rh, Rh)]
            _remote(theirs, theirs, send_sems, recv_sems, 3 + j, (x, y, c)).wait_recv()
        for cp in sent + passed:
            cp.wait_send()
        own.wait()

    return pl.pallas_call(
        body, name=name, in_specs=[ANY], out_specs=ANY, out_shape=SDS((N_CHIPS, R, C), pack.dtype),
        scratch_shapes=[pltpu.SemaphoreType.DMA((6,)), pltpu.SemaphoreType.DMA((6,)), pltpu.SemaphoreType.DMA],
    )(pack)


def _sibling_exchange(g, *, name):
    _, R, C = g.shape
    Rh = R // 2

    def body(g_ref, o_ref, send_sems, recv_sems):
        x, y, c, _ = _position()
        cp = _remote(g_ref.at[:, pl.ds((1 - c) * Rh, Rh)], o_ref, send_sems, recv_sems, 0, (x, y, 1 - c))
        cp.start()
        cp.wait()

    return pl.pallas_call(
        body, name=name, in_specs=[ANY], out_specs=ANY, out_shape=SDS((N_CHIPS, Rh, C), g.dtype),
        scratch_shapes=[pltpu.SemaphoreType.DMA((1,)), pltpu.SemaphoreType.DMA((1,))],
    )(g)


def _chip_scatter(p, *, name):
    _, Rh, C = p.shape

    def body(p_ref, o_ref, send_sems, recv_sems):
        x, y, c, chips = _position()
        sent = [_remote(p_ref.at[2 * cx + cy], o_ref.at[j], send_sems, recv_sems, j, (cx, cy, c))
                for j, (cx, cy) in enumerate(chips)]
        for cp in sent:
            cp.start()
        for cp in sent:
            cp.wait()

    return pl.pallas_call(
        body, name=name, in_specs=[ANY], out_specs=ANY, out_shape=SDS((N_CHIPS - 1, Rh, C), p.dtype),
        scratch_shapes=[pltpu.SemaphoreType.DMA((3,)), pltpu.SemaphoreType.DMA((3,))],
    )(p)


def _sibling_join(t, *, name):
    Rh, C = t.shape

    def body(t_ref, o_ref, send_sems, recv_sems, local_sem):
        x, y, c, _ = _position()
        mine = pl.ds(c * Rh, Rh)
        own = pltpu.make_async_copy(t_ref, o_ref.at[mine], local_sem)
        own.start()
        cp = _remote(t_ref, o_ref.at[mine], send_sems, recv_sems, 0, (x, y, 1 - c))
        cp.start()
        theirs = o_ref.at[pl.ds((1 - c) * Rh, Rh)]
        _remote(theirs, theirs, send_sems, recv_sems, 0, (x, y, c)).wait_recv()
        cp.wait_send()
        own.wait()

    return pl.pallas_call(
        body, name=name, in_specs=[ANY], out_specs=ANY, out_shape=SDS((2 * Rh, C), t.dtype),
        scratch_shapes=[pltpu.SemaphoreType.DMA((1,)), pltpu.SemaphoreType.DMA((1,)), pltpu.SemaphoreType.DMA],
    )(t)


def _gather_all(v, *, name):
    M, C = v.shape

    def body(v_ref, o_ref, send_sems, recv_sems, local_sem):
        x, y, c, chips = _position()
        slot = lambda px, py, pc: o_ref.at[4 * px + 2 * py + pc]
        own = pltpu.make_async_copy(v_ref, slot(x, y, c), local_sem)
        own.start()
        first = [_remote(v_ref, slot(x, y, c), send_sems, recv_sems, 0, (x, y, 1 - c))]
        first += [_remote(v_ref, slot(x, y, c), send_sems, recv_sems, 1 + j, (*chip, c)) for j, chip in enumerate(chips)]
        for cp in first:
            cp.start()
        passed = []
        for j, chip in enumerate(chips):
            landed = slot(*chip, c)
            _remote(landed, landed, send_sems, recv_sems, 1 + j, (x, y, c)).wait_recv()
            cp = _remote(landed, landed, send_sems, recv_sems, 4 + j, (x, y, 1 - c))
            cp.start()
            passed.append(cp)
        sib = slot(x, y, 1 - c)
        _remote(sib, sib, send_sems, recv_sems, 0, (x, y, c)).wait_recv()
        for j, chip in enumerate(chips):
            theirs = slot(*chip, 1 - c)
            _remote(theirs, theirs, send_sems, recv_sems, 4 + j, (x, y, c)).wait_recv()
        for cp in first + passed:
            cp.wait_send()
        own.wait()

    return pl.pallas_call(
        body, name=name, in_specs=[ANY], out_specs=ANY, out_shape=SDS((N_DEV, M, C), v.dtype),
        scratch_shapes=[pltpu.SemaphoreType.DMA((7,)), pltpu.SemaphoreType.DMA((7,)), pltpu.SemaphoreType.DMA],
    )(v)


BIG = ("ffn1_w1", "ffn1_w2", "w_in", "w_attn_branch", "w_sgu_branch", "w_out", "ffn2_w1", "ffn2_w2")
COL_SHARDED = ("ffn1_w1", "w_in", "ffn2_w1")
SMALL = ("ffn1_pre_g", "ffn1_post_g", "mix_pre_g", "attn_sinks", "sgu_ln_g", "sgu_ln_b", "sgu_w", "sgu_b",
         "mix_post_g", "ffn2_pre_g", "ffn2_post_g")
WEIGHTS = ("ffn1_pre_g", "ffn1_w1", "ffn1_w2", "ffn1_post_g", "mix_pre_g", "w_in", "attn_sinks", "sgu_ln_g",
           "sgu_ln_b", "sgu_w", "sgu_b", "w_attn_branch", "w_sgu_branch", "w_out", "mix_post_g", "ffn2_pre_g",
           "ffn2_w1", "ffn2_w2", "ffn2_post_g")


def _rows_of(shape, width):
    return (shape[0] * shape[1]) // width


def _pack_layer(shards, width, dtype):
    return jnp.concatenate([shards[n].astype(dtype).reshape(-1, width) for n in BIG], axis=0)


def _unpack_layer(rows, shard_shapes, width):
    out, r0 = {}, 0
    for n in BIG:
        nr = _rows_of(shard_shapes[n], width)
        out[n] = rows[r0:r0 + nr].reshape(shard_shapes[n])
        r0 += nr
    return out


def _ffn_fwd(x, pre_g, w1, w2, post_g, tag):
    a, h = _norm_matmul(x, pre_g, w1, name=f"{tag}_up", with_h=True)
    xn, o = _swiglu_out(a, w2, x, post_g, name=f"{tag}_down")
    return xn, (x, h, a, o)


def _ffn_bwd(dy, saved, pre_g, w1, w2, post_g, tag):
    x, h, a, o = saved
    da, s, do, d_post = _ffn_bwd_hidden(dy, o, post_g, a, w2, name=f"{tag}_bwd_hidden")
    dw2 = _matmul_tn(s, do, name=f"{tag}_dw2")
    dx, d_pre = _matmul_nt_norm_bwd(da, w1, x, pre_g, dy, None, name=f"{tag}_bwd_in")
    dw1 = _matmul_tn(h, da, name=f"{tag}_dw1")
    return dx, dw1, dw2, d_pre, d_post


def kernel(x, ffn1_pre_g, ffn1_w1, ffn1_w2, ffn1_post_g, mix_pre_g, w_in, attn_sinks, sgu_ln_g, sgu_ln_b, sgu_w, sgu_b, w_attn_branch, w_sgu_branch, w_out, mix_post_g, ffn2_pre_g, ffn2_w1, ffn2_w2, ffn2_post_g, loss_target, m_ffn1_pre_g, m_ffn1_w1, m_ffn1_w2, m_ffn1_post_g, m_mix_pre_g, m_w_in, m_attn_sinks, m_sgu_ln_g, m_sgu_ln_b, m_sgu_w, m_sgu_b, m_w_attn_branch, m_w_sgu_branch, m_w_out, m_mix_post_g, m_ffn2_pre_g, m_ffn2_w1, m_ffn2_w2, m_ffn2_post_g, v_ffn1_pre_g, v_ffn1_w1, v_ffn1_w2, v_ffn1_post_g, v_mix_pre_g, v_w_in, v_attn_sinks, v_sgu_ln_g, v_sgu_ln_b, v_sgu_w, v_sgu_b, v_w_attn_branch, v_w_sgu_branch, v_w_out, v_mix_post_g, v_ffn2_pre_g, v_ffn2_w1, v_ffn2_w2, v_ffn2_post_g):
    given = dict(locals())
    W = {n: given[n] for n in WEIGHTS}
    M = {n: given["m_" + n] for n in WEIGHTS}
    V = {n: given["v_" + n] for n in WEIGHTS}
    L = ffn1_w1.shape[0]
    T, D = x.shape[1], x.shape[2]
    xt = x.reshape(T, D)
    target = loss_target.reshape(T, D)
    assert L % 2 == 0 and D == ATTN_WIDTH == SGU_WIDTH and T % ATTN_BLOCK == 0

    shard_shapes = {n: W[n].shape[1:] for n in BIG}
    pack = jnp.concatenate([_pack_layer({n: W[n][l] for n in BIG}, D, BF16) for l in range(L)], axis=0)
    R = pack.shape[0]
    rows_per_layer = R // L
    gathered = _gather_shards(pack, name="gather_weights")
    full = []
    for l in range(L):
        per_chip = [_unpack_layer(gathered[s, l * rows_per_layer:(l + 1) * rows_per_layer], shard_shapes, D)
                    for s in range(N_CHIPS)]
        full.append({n: jnp.concatenate([pc[n] for pc in per_chip], axis=1 if n in COL_SHARDED else 0) for n in BIG})

    row = lambda name, l: W[name][l].reshape(1, -1)
    causal = jnp.tril(jnp.ones((SGU_CHUNK, SGU_CHUNK), dtype=bool))
    saved = []
    h_cur = xt
    for l in range(L):
        fw = full[l]
        sv = {}
        h_cur, sv["ffn1"] = _ffn_fwd(h_cur, row("ffn1_pre_g", l), fw["ffn1_w1"], fw["ffn1_w2"], row("ffn1_post_g", l),
                                     f"l{l}_ffn1")
        w_qkv, w_main = fw["w_in"][:, :QKV_WIDTH], fw["w_in"][:, QKV_WIDTH:]
        zqkv, hm = _norm_matmul_t(h_cur, row("mix_pre_g", l), w_qkv.T, name=f"l{l}_mix_in_qkv")
        zmain, = _norm_matmul(h_cur, row("mix_pre_g", l), w_main, name=f"l{l}_mix_in_main", with_h=False)
        wm = jnp.where(causal[None], sgu_w[l], 0.0).astype(BF16)
        wmt = jnp.swapaxes(wm, 1, 2)
        bias = jnp.broadcast_to(sgu_b[l][:, :, None], (SGU_GROUPS, SGU_CHUNK, 128)).astype(F32)
        y_attn = _attn_fwd(zqkv, attn_sinks[l], name=f"l{l}_attn")
        y_sgu = _sgu_fwd(zmain, row("sgu_ln_g", l), row("sgu_ln_b", l), wm, bias, name=f"l{l}_sgu")
        x_mix = h_cur
        h_cur, pa, ps, mo = _merge_fwd(y_attn, y_sgu, zmain, fw["w_attn_branch"], fw["w_sgu_branch"], fw["w_out"],
                                       x_mix, row("mix_post_g", l), name=f"l{l}_merge")
        sv["mix"] = (x_mix, hm, zqkv, zmain, y_attn, y_sgu, pa, ps, mo, wm, wmt, bias)
        h_cur, sv["ffn2"] = _ffn_fwd(h_cur, row("ffn2_pre_g", l), fw["ffn2_w1"], fw["ffn2_w2"], row("ffn2_post_g", l),
                                     f"l{l}_ffn2")
        saved.append(sv)

    dy, lsum = _loss_head(h_cur, target, name="loss_head")
    loss = lax.psum(0.5 * lsum[0, 0], AXES)

    big_grads = [None] * L
    small_grads = [None] * L
    for l in reversed(range(L)):
        fw, sv = full[l], saved[l]
        gb, gs = {}, {}
        dy, gb["ffn2_w1"], gb["ffn2_w2"], gs["ffn2_pre_g"], gs["ffn2_post_g"] = _ffn_bwd(
            dy, sv["ffn2"], row("ffn2_pre_g", l), fw["ffn2_w1"], fw["ffn2_w2"], row("ffn2_post_g", l), f"l{l}_ffn2")

        x_mix, hm, zqkv, zmain, y_attn, y_sgu, pa, ps, mo, wm, wmt, bias = sv["mix"]
        dzmain, dout, merged, dpa, dps, dya, dys, gs["mix_post_g"] = _merge_bwd(
            dy, mo, row("mix_post_g", l), pa, ps, zmain, fw["w_attn_branch"], fw["w_sgu_branch"], fw["w_out"],
            name=f"l{l}_merge_bwd")
        gb["w_out"] = _matmul_tn(merged, dout, name=f"l{l}_dw_out")
        gb["w_attn_branch"] = _matmul_tokens(y_attn, dpa, name=f"l{l}_dw_attn")
        gb["w_sgu_branch"] = _matmul_tn(y_sgu, dps, name=f"l{l}_dw_sgu")
        dzqkv, dsink = _attn_bwd(zqkv, attn_sinks[l], dya, name=f"l{l}_attn_bwd")
        gs["attn_sinks"] = dsink[0, :N_Q_HEADS]
        dzmain, dsw, dsb, gs["sgu_ln_g"], gs["sgu_ln_b"] = _sgu_bwd(
            zmain, dzmain, dys, row("sgu_ln_g", l), row("sgu_ln_b", l), wm, wmt, bias, name=f"l{l}_sgu_bwd")
        gs["sgu_w"] = dsw
        gs["sgu_b"] = dsb[:, :SGU_GROUPS].T
        w_qkv, w_main = fw["w_in"][:, :QKV_WIDTH], fw["w_in"][:, QKV_WIDTH:]
        dh_qkv = _matmul_tn_rows(dzqkv, w_qkv.T, name=f"l{l}_mix_bwd_qkv")
        dy, gs["mix_pre_g"] = _matmul_nt_norm_bwd(dzmain, w_main, x_mix, row("mix_pre_g", l), dy, dh_qkv,
                                                   name=f"l{l}_mix_bwd_in")
        gb["w_in"] = jnp.concatenate([_matmul_tokens(dzqkv, hm, name=f"l{l}_dw_in_qkv").T,
                                      _matmul_tn(hm, dzmain, name=f"l{l}_dw_in_main")], axis=1)

        dy, gb["ffn1_w1"], gb["ffn1_w2"], gs["ffn1_pre_g"], gs["ffn1_post_g"] = _ffn_bwd(
            dy, sv["ffn1"], row("ffn1_pre_g", l), fw["ffn1_w1"], fw["ffn1_w2"], row("ffn1_post_g", l), f"l{l}_ffn1")
        big_grads[l], small_grads[l] = gb, gs
    grad_x = dy.reshape(x.shape)

    def shard_of(n, g, s):
        k = shard_shapes[n][1 if n in COL_SHARDED else 0]
        return g[:, s * k:(s + 1) * k] if n in COL_SHARDED else g[s * k:(s + 1) * k]

    gr = jnp.stack([jnp.concatenate([_pack_layer({n: shard_of(n, big_grads[l][n], s) for n in BIG}, D, F32)
                                     for l in range(L)], axis=0) for s in range(N_CHIPS)])
    Rh = R // 2
    from_sibling = _sibling_exchange(gr, name="grads_sibling_exchange")
    zero = lambda: 0
    my_core = lambda: lax.axis_index("c")
    my_chip = lambda: 2 * lax.axis_index("x") + lax.axis_index("y")
    pair, pair_bf16 = _sum_terms([(gr, None, my_core), (from_sibling, None, zero)], Rh, N_CHIPS, (F32, BF16),
                                 name="grads_pair_sum")
    from_chips = _chip_scatter(pair_bf16, name="grads_chip_scatter")
    total_half = _sum_terms([(pair, my_chip, zero)] + [(from_chips, (lambda j=j: j), zero) for j in range(N_CHIPS - 1)],
                            Rh, 1, (F32,), name="grads_chip_sum")[0][0]
    reduced = _sibling_join(total_half, name="grads_sibling_join")

    grads = {n: [] for n in WEIGHTS}
    for l in range(L):
        per = _unpack_layer(reduced[l * rows_per_layer:(l + 1) * rows_per_layer], shard_shapes, D)
        for n in BIG:
            grads[n].append(per[n])

    def small_rows(gs):
        parts = []
        for n in SMALL:
            flat = gs[n].reshape(-1)
            pad = (-flat.shape[0]) % D
            parts.append(jnp.pad(flat, (0, pad)).reshape(-1, D))
        return jnp.concatenate(parts, axis=0)

    spack = jnp.concatenate([small_rows(small_grads[l]) for l in range(L)], axis=0)
    n_small = spack.shape[0]
    pad_rows = (-n_small) % SUBLANES_BF16
    spack = jnp.pad(spack, ((0, pad_rows), (0, 0)))
    everyone = _gather_all(spack, name="small_grads_gather")
    ssum = _sum_terms([(everyone, (lambda d=d: d), zero) for d in range(N_DEV)], spack.shape[0], 1, (F32,),
                      name="small_grads_sum")[0][0]
    per_layer = n_small // L
    for l in range(L):
        r0 = l * per_layer
        for n in SMALL:
            shp = W[n].shape[1:]
            size = math.prod(shp)
            nr = -(-size // D)
            grads[n].append(ssum[r0:r0 + nr].reshape(-1)[:size].reshape(shp))
            r0 += nr
    grads = {n: jnp.stack(grads[n]) for n in WEIGHTS}

    delta, new_m, new_v = {}, {}, {}
    for n in WEIGHTS:
        delta[n], new_m[n], new_v[n] = _adamw(W[n], grads[n], M[n], V[n], name=f"adamw_{n}")

    return (loss, grad_x, *[grads[n] for n in WEIGHTS], *[delta[n] for n in WEIGHTS],
            *[new_m[n] for n in WEIGHTS], *[new_v[n] for n in WEIGHTS])
```

```python
import functools
import math

import jax
import jax.numpy as jnp
from jax import lax
from jax.experimental import pallas as pl
from jax.experimental.pallas import tpu as pltpu

F32, BF16 = jnp.float32, jnp.bfloat16
SDS = jax.ShapeDtypeStruct
MESH = pl.DeviceIdType.MESH
AXES = ("x", "y", "c")

HEAD_DIM = 64
N_Q_HEADS = 16
N_KV_HEADS = 2
Q_PER_KV = N_Q_HEADS // N_KV_HEADS
ATTN_WIDTH = N_Q_HEADS * HEAD_DIM
KV_WIDTH = N_KV_HEADS * HEAD_DIM
ATTN_BLOCK = 128
SGU_CHUNK = 128
SGU_GROUPS = 8
SGU_WIDTH = SGU_GROUPS * 128
QKV_WIDTH = ATTN_WIDTH + 2 * KV_WIDTH
RMS_EPS = 1e-6
LN_EPS = 1e-5
MASK_VALUE = -1e30
ATTN_SCALE = 1.0 / math.sqrt(HEAD_DIM)

ADAM_LR, ADAM_B1, ADAM_B2, ADAM_EPS, ADAM_WD, ADAM_STEP = 0.001, 0.9, 0.999, 1e-08, 0.01, 10

N_CHIPS = 4
N_DEV = 8

VMEM_LIMIT_BYTES = 56 * 1024 * 1024
LANES = 128
SUBLANES_BF16 = 16

TM_NORM_MATMUL = 1024
TM_ROW = 512
TM_FFN_BWD = 256
TT_REDUCE = 1024
TQ_ATTN = 512
TS_SGU = 512


def _tile(n, pref, mult):
    t = (min(pref, n) // mult) * mult
    while t >= mult:
        if n % t == 0:
            return t
        t -= mult
    return n


def _params(*sem):
    return pltpu.CompilerParams(dimension_semantics=sem, vmem_limit_bytes=VMEM_LIMIT_BYTES)


def _dot(a, b):
    return jnp.dot(a, b, preferred_element_type=F32)


def _dot_nt(a, b):
    return lax.dot_general(a, b, (((1,), (1,)), ((), ())), preferred_element_type=F32)


def _dot_tn(a, b):
    return lax.dot_general(a, b, (((0,), (0,)), ((), ())), preferred_element_type=F32)


def _sigmoid(x):
    return 1.0 / (1.0 + jnp.exp(-x))


def _rms_stats(xf):
    r = lax.rsqrt(jnp.mean(xf * xf, axis=-1, keepdims=True) + RMS_EPS)
    return r, xf * r


def _rms_bwd(xf, g, dy):
    r, xh = _rms_stats(xf)
    dyg = dy * g
    dx = r * (dyg - xh * jnp.mean(dyg * xh, axis=-1, keepdims=True))
    return dx, jnp.sum(dy * xh, axis=0, keepdims=True)


def _gelu_parts(x):
    cdf = 0.5 * (1.0 + lax.erf(x * (1.0 / math.sqrt(2.0))))
    return cdf


def _gelu(x):
    return x * _gelu_parts(x)


def _gelu_grad(x):
    return _gelu_parts(x) + x * jnp.exp(-0.5 * x * x) * (1.0 / math.sqrt(2.0 * math.pi))


def _norm_matmul(x, g, w, *, name, with_h):
    T, D = x.shape
    N = w.shape[1]
    tm = _tile(T, TM_NORM_MATMUL, SUBLANES_BF16)
    tn = _tile(N, 1408, LANES)

    def body(x_ref, g_ref, w_ref, a_ref, *rest):
        h_sc = rest[-1]

        @pl.when(pl.program_id(1) == 0)
        def _():
            _, xh = _rms_stats(x_ref[...])
            h = (xh * g_ref[...]).astype(BF16)
            h_sc[...] = h
            if with_h:
                rest[0][...] = h

        a_ref[...] = _dot(h_sc[...], w_ref[...]).astype(BF16)

    out_specs = [pl.BlockSpec((tm, tn), lambda i, j: (i, j))]
    out_shape = [SDS((T, N), BF16)]
    if with_h:
        out_specs.append(pl.BlockSpec((tm, D), lambda i, j: (i, 0)))
        out_shape.append(SDS((T, D), BF16))
    return pl.pallas_call(
        body, name=name, grid=(T // tm, N // tn),
        in_specs=[pl.BlockSpec((tm, D), lambda i, j: (i, 0)),
                  pl.BlockSpec((1, D), lambda i, j: (0, 0)),
                  pl.BlockSpec((D, tn), lambda i, j: (0, j))],
        out_specs=out_specs, out_shape=out_shape,
        scratch_shapes=[pltpu.VMEM((tm, D), BF16)],
        compiler_params=_params("parallel", "arbitrary"),
    )(x, g, w)


def _norm_matmul_t(x, g, wt, *, name):
    T, D = x.shape
    N = wt.shape[0]
    tm = _tile(T, TM_ROW, LANES)

    def body(x_ref, g_ref, w_ref, a_ref, h_ref):
        _, xh = _rms_stats(x_ref[...])
        h = (xh * g_ref[...]).astype(BF16)
        h_ref[...] = h
        a_ref[...] = _dot_nt(w_ref[...], h).astype(BF16)

    return pl.pallas_call(
        body, name=name, grid=(T // tm,),
        in_specs=[pl.BlockSpec((tm, D), lambda i: (i, 0)), pl.BlockSpec((1, D), lambda i: (0, 0)),
                  pl.BlockSpec((N, D), lambda i: (0, 0))],
        out_specs=[pl.BlockSpec((N, tm), lambda i: (0, i)), pl.BlockSpec((tm, D), lambda i: (i, 0))],
        out_shape=[SDS((N, T), BF16), SDS((T, D), BF16)],
        compiler_params=_params("parallel"),
    )(x, g, wt)


def _matmul_tokens(at, b, *, name):
    K, T = at.shape
    N = b.shape[1]
    tt = _tile(T, TT_REDUCE, LANES)

    def body(a_ref, b_ref, o_ref):
        @pl.when(pl.program_id(0) == 0)
        def _():
            o_ref[...] = jnp.zeros_like(o_ref)

        o_ref[...] += _dot(a_ref[...], b_ref[...])

    return pl.pallas_call(
        body, name=name, grid=(T // tt,),
        in_specs=[pl.BlockSpec((K, tt), lambda t: (0, t)), pl.BlockSpec((tt, N), lambda t: (t, 0))],
        out_specs=pl.BlockSpec((K, N), lambda t: (0, 0)),
        out_shape=SDS((K, N), F32),
        compiler_params=_params("arbitrary"),
    )(at, b)


def _matmul_tn_rows(dat, wt, *, name):
    N, T = dat.shape
    D = wt.shape[1]
    tm = _tile(T, TM_ROW, LANES)

    def body(da_ref, w_ref, o_ref):
        o_ref[...] = _dot_tn(da_ref[...], w_ref[...])

    return pl.pallas_call(
        body, name=name, grid=(T // tm,),
        in_specs=[pl.BlockSpec((N, tm), lambda i: (0, i)), pl.BlockSpec((N, D), lambda i: (0, 0))],
        out_specs=pl.BlockSpec((tm, D), lambda i: (i, 0)),
        out_shape=SDS((T, D), F32),
        compiler_params=_params("parallel"),
    )(dat, wt)


def _ff_chunk(F):
    return F if F <= 1408 else F // 2


def _swiglu_out(a, w2, x, g_post, *, name):
    T, F2 = a.shape
    F = F2 // 2
    D = x.shape[1]
    tm = _tile(T, TM_ROW, SUBLANES_BF16)
    fc = _ff_chunk(F)

    def body(a_ref, w_ref, x_ref, g_ref, xn_ref, o_ref):
        acc = None
        for c0 in range(0, F, fc):
            gt = a_ref[:, c0:c0 + fc].astype(F32)
            ut = a_ref[:, F + c0:F + c0 + fc].astype(F32)
            s = (gt * _sigmoid(gt) * ut).astype(BF16)
            part = _dot(s, w_ref[c0:c0 + fc, :])
            acc = part if acc is None else acc + part
        o_ref[...] = acc.astype(BF16)
        _, oh = _rms_stats(acc)
        xn_ref[...] = x_ref[...] + 0.5 * (oh * g_ref[...])

    return pl.pallas_call(
        body, name=name, grid=(T // tm,),
        in_specs=[pl.BlockSpec((tm, F2), lambda i: (i, 0)),
                  pl.BlockSpec((F, D), lambda i: (0, 0)),
                  pl.BlockSpec((tm, D), lambda i: (i, 0)),
                  pl.BlockSpec((1, D), lambda i: (0, 0))],
        out_specs=[pl.BlockSpec((tm, D), lambda i: (i, 0)), pl.BlockSpec((tm, D), lambda i: (i, 0))],
        out_shape=[SDS((T, D), F32), SDS((T, D), BF16)],
        compiler_params=_params("parallel"),
    )(a, w2, x, g_post)


def _ffn_bwd_hidden(dy, o, g_post, a, w2, *, name):
    T, F2 = a.shape
    F = F2 // 2
    D = dy.shape[1]
    tm = _tile(T, TM_FFN_BWD, SUBLANES_BF16)
    fc = _ff_chunk(F)

    def body(dy_ref, o_ref, g_ref, a_ref, w_ref, da_ref, s_ref, do_ref, dg_ref):
        @pl.when(pl.program_id(0) == 0)
        def _():
            dg_ref[...] = jnp.zeros_like(dg_ref)

        do, dg = _rms_bwd(o_ref[...].astype(F32), g_ref[...], 0.5 * dy_ref[...])
        dg_ref[...] += dg
        dob = do.astype(BF16)
        do_ref[...] = dob
        for c0 in range(0, F, fc):
            ds = _dot_nt(dob, w_ref[c0:c0 + fc, :])
            gt = a_ref[:, c0:c0 + fc].astype(F32)
            ut = a_ref[:, F + c0:F + c0 + fc].astype(F32)
            sg = _sigmoid(gt)
            sl = gt * sg
            s_ref[:, c0:c0 + fc] = (sl * ut).astype(BF16)
            da_ref[:, c0:c0 + fc] = (ds * ut * (sg * (1.0 + gt * (1.0 - sg)))).astype(BF16)
            da_ref[:, F + c0:F + c0 + fc] = (ds * sl).astype(BF16)

    row = lambda w: pl.BlockSpec((tm, w), lambda i: (i, 0))
    return pl.pallas_call(
        body, name=name, grid=(T // tm,),
        in_specs=[row(D), row(D), pl.BlockSpec((1, D), lambda i: (0, 0)), row(F2),
                  pl.BlockSpec((F, D), lambda i: (0, 0))],
        out_specs=[row(F2), row(F), row(D), pl.BlockSpec((1, D), lambda i: (0, 0))],
        out_shape=[SDS((T, F2), BF16), SDS((T, F), BF16), SDS((T, D), BF16), SDS((1, D), F32)],
        compiler_params=_params("arbitrary"),
    )(dy, o, g_post, a, w2)


def _matmul_tn(a, b, *, name):
    T, K = a.shape
    N = b.shape[1]
    tk = _tile(K, 1408, LANES)
    tn = _tile(N, 1408, LANES)
    tt = _tile(T, TT_REDUCE, SUBLANES_BF16)

    def body(a_ref, b_ref, o_ref):
        @pl.when(pl.program_id(2) == 0)
        def _():
            o_ref[...] = jnp.zeros_like(o_ref)

        o_ref[...] += _dot_tn(a_ref[...], b_ref[...])

    return pl.pallas_call(
        body, name=name, grid=(K // tk, N // tn, T // tt),
        in_specs=[pl.BlockSpec((tt, tk), lambda k, n, t: (t, k)),
                  pl.BlockSpec((tt, tn), lambda k, n, t: (t, n))],
        out_specs=pl.BlockSpec((tk, tn), lambda k, n, t: (k, n)),
        out_shape=SDS((K, N), F32),
        compiler_params=_params("parallel", "parallel", "arbitrary"),
    )(a, b)


def _matmul_nt_norm_bwd(da, w, x, g, dy, init, *, name):
    T, N = da.shape
    D = w.shape[0]
    tm = _tile(T, TM_ROW, SUBLANES_BF16)
    tn = _tile(N, 1408, LANES)
    nj = N // tn
    has_init = init is not None

    def body(da_ref, w_ref, x_ref, g_ref, dy_ref, *rest):
        init_ref = rest[0] if has_init else None
        dx_ref, dg_ref, acc = rest[-3:]
        i, j = pl.program_id(0), pl.program_id(1)

        @pl.when(j == 0)
        def _():
            acc[...] = init_ref[...] if has_init else jnp.zeros_like(acc)

        acc[...] += _dot_nt(da_ref[...], w_ref[...])

        @pl.when(j == nj - 1)
        def _():
            @pl.when(i == 0)
            def _():
                dg_ref[...] = jnp.zeros_like(dg_ref)

            dx, dg = _rms_bwd(x_ref[...], g_ref[...], acc[...])
            dx_ref[...] = dy_ref[...] + dx
            dg_ref[...] += dg

    row = pl.BlockSpec((tm, D), lambda i, j: (i, 0))
    vec = pl.BlockSpec((1, D), lambda i, j: (0, 0))
    in_specs = [pl.BlockSpec((tm, tn), lambda i, j: (i, j)), pl.BlockSpec((D, tn), lambda i, j: (0, j)),
                row, vec, row]
    args = [da, w, x, g, dy]
    if has_init:
        in_specs.append(row)
        args.append(init)
    return pl.pallas_call(
        body, name=name, grid=(T // tm, nj), in_specs=in_specs,
        out_specs=[row, vec], out_shape=[SDS((T, D), F32), SDS((1, D), F32)],
        scratch_shapes=[pltpu.VMEM((tm, D), F32)],
        compiler_params=_params("arbitrary", "arbitrary"),
    )(*args)


GROUP_LANES = Q_PER_KV * ATTN_BLOCK


def _attn_mask_t(first):
    kj = lax.broadcasted_iota(jnp.int32, (2 * ATTN_BLOCK, ATTN_BLOCK), 0)
    qi = lax.broadcasted_iota(jnp.int32, (2 * ATTN_BLOCK, ATTN_BLOCK), 1)
    rel = qi + ATTN_BLOCK - kj
    band = (rel >= 0) & (rel < ATTN_BLOCK)
    if first is False:
        return band
    return band & ((kj >= ATTN_BLOCK) | jnp.logical_not(first))


def _attn_probs_t(st, valid, sink):
    s = jnp.where(valid, st, MASK_VALUE)
    m = jnp.maximum(jnp.max(s, axis=0, keepdims=True), sink)
    p = jnp.exp(s - m)
    es = jnp.exp(sink - m)
    inv = 1.0 / (jnp.sum(p, axis=0, keepdims=True) + es)
    return p * inv, es * inv


def _attn_specs(tq, tile_of):
    nb = tq // ATTN_BLOCK
    krow, vrow = ATTN_WIDTH // KV_WIDTH, ATTN_WIDTH // KV_WIDTH + 1
    halo = lambda r: pl.BlockSpec((KV_WIDTH, ATTN_BLOCK), lambda t: (r, jnp.maximum(tile_of(t) * nb - 1, 0)))
    return [pl.BlockSpec((ATTN_WIDTH, tq), lambda t: (0, tile_of(t))),
            pl.BlockSpec((KV_WIDTH, tq), lambda t: (krow, tile_of(t))),
            pl.BlockSpec((KV_WIDTH, tq), lambda t: (vrow, tile_of(t))),
            halo(krow), halo(vrow)]


def _head_rows(g, r):
    h = g * Q_PER_KV + r
    return h, slice(h * HEAD_DIM, (h + 1) * HEAD_DIM)


def _group_stack(ref, g, cols):
    return jnp.concatenate([ref[_head_rows(g, r)[1], cols] for r in range(Q_PER_KV)], axis=1)


def _attn_fwd(zt, sinks, *, name):
    T = zt.shape[1]
    tq = _tile(T, TQ_ATTN, ATTN_BLOCK)
    nb = tq // ATTN_BLOCK

    def body(q_ref, k_ref, v_ref, kh_ref, vh_ref, s_ref, o_ref, kf, vf, pt):
        kf[:, 0:ATTN_BLOCK] = kh_ref[...]
        kf[:, ATTN_BLOCK:] = k_ref[...]
        vf[:, 0:ATTN_BLOCK] = vh_ref[...]
        vf[:, ATTN_BLOCK:] = v_ref[...]
        for b in range(nb):
            cols = slice(b * ATTN_BLOCK, (b + 1) * ATTN_BLOCK)
            win = slice(b * ATTN_BLOCK, (b + 2) * ATTN_BLOCK)
            valid = _attn_mask_t((pl.program_id(0) == 0) if b == 0 else False)
            for g in range(N_KV_HEADS):
                gr = slice(g * HEAD_DIM, (g + 1) * HEAD_DIM)
                st = _dot_tn(kf[gr, win], _group_stack(q_ref, g, cols)) * ATTN_SCALE
                for r in range(Q_PER_KV):
                    h, _ = _head_rows(g, r)
                    sl = slice(r * ATTN_BLOCK, (r + 1) * ATTN_BLOCK)
                    probs, _ = _attn_probs_t(st[:, sl], valid, s_ref[h])
                    pt[:, sl] = probs.astype(BF16)
                ot = _dot(vf[gr, win], pt[...])
                for r in range(Q_PER_KV):
                    o_ref[_head_rows(g, r)[1], cols] = ot[:, r * ATTN_BLOCK:(r + 1) * ATTN_BLOCK].astype(BF16)

    return pl.pallas_call(
        body, name=name, grid=(T // tq,),
        in_specs=_attn_specs(tq, lambda t: t) + [pl.BlockSpec(memory_space=pltpu.SMEM)],
        out_specs=pl.BlockSpec((ATTN_WIDTH, tq), lambda t: (0, t)),
        out_shape=SDS((ATTN_WIDTH, T), BF16),
        scratch_shapes=[pltpu.VMEM((KV_WIDTH, tq + ATTN_BLOCK), BF16)] * 2
        + [pltpu.VMEM((2 * ATTN_BLOCK, GROUP_LANES), BF16)],
        compiler_params=_params("parallel"),
    )(zt, zt, zt, zt, zt, sinks)


def _attn_bwd(zt, sinks, dot_, *, name):
    T = zt.shape[1]
    tq = _tile(T, TQ_ATTN, ATTN_BLOCK)
    nb = tq // ATTN_BLOCK
    nt = T // tq
    tile_of = lambda t: nt - 1 - t

    def body(q_ref, k_ref, v_ref, kh_ref, vh_ref, do_ref, s_ref, dz_ref, dsink_ref, kf, vf, dkf, dvf, carry, pt, dst):
        t = pl.program_id(0)

        @pl.when(t == 0)
        def _():
            carry[...] = jnp.zeros_like(carry)
            dsink_ref[...] = jnp.zeros_like(dsink_ref)

        kf[:, 0:ATTN_BLOCK] = kh_ref[...]
        kf[:, ATTN_BLOCK:] = k_ref[...]
        vf[:, 0:ATTN_BLOCK] = vh_ref[...]
        vf[:, ATTN_BLOCK:] = v_ref[...]
        dkf[...] = jnp.zeros_like(dkf)
        dvf[...] = jnp.zeros_like(dvf)
        dkf[:, tq:] = carry[0:KV_WIDTH, :]
        dvf[:, tq:] = carry[KV_WIDTH:, :]
        lane = lax.broadcasted_iota(jnp.int32, (1, LANES), 1)
        dsink = jnp.zeros((1, LANES), F32)
        for b in range(nb):
            cols = slice(b * ATTN_BLOCK, (b + 1) * ATTN_BLOCK)
            win = slice(b * ATTN_BLOCK, (b + 2) * ATTN_BLOCK)
            valid = _attn_mask_t((t == nt - 1) if b == 0 else False)
            for g in range(N_KV_HEADS):
                gr = slice(g * HEAD_DIM, (g + 1) * HEAD_DIM)
                kt2, vt2 = kf[gr, win], vf[gr, win]
                qst = _group_stack(q_ref, g, cols)
                dost = _group_stack(do_ref, g, cols)
                st = _dot_tn(kt2, qst) * ATTN_SCALE
                dpt = _dot_tn(vt2, dost)
                for r in range(Q_PER_KV):
                    h, _ = _head_rows(g, r)
                    sl = slice(r * ATTN_BLOCK, (r + 1) * ATTN_BLOCK)
                    probs, psink = _attn_probs_t(st[:, sl], valid, s_ref[h])
                    dp = dpt[:, sl]
                    delta = jnp.sum(probs * dp, axis=0, keepdims=True)
                    pt[:, sl] = probs.astype(BF16)
                    dst[:, sl] = (probs * (dp - delta)).astype(BF16)
                    dsink = dsink + jnp.where(lane == h, -jnp.sum(psink * delta), 0.0)
                dqt = _dot(kt2, dst[...]) * ATTN_SCALE
                for r in range(Q_PER_KV):
                    dz_ref[_head_rows(g, r)[1], cols] = dqt[:, r * ATTN_BLOCK:(r + 1) * ATTN_BLOCK].astype(BF16)
                dkf[gr, win] += _dot_nt(qst, dst[...]) * ATTN_SCALE
                dvf[gr, win] += _dot_nt(dost, pt[...])
        dz_ref[ATTN_WIDTH:ATTN_WIDTH + KV_WIDTH, :] = dkf[:, ATTN_BLOCK:].astype(BF16)
        dz_ref[ATTN_WIDTH + KV_WIDTH:, :] = dvf[:, ATTN_BLOCK:].astype(BF16)
        carry[0:KV_WIDTH, :] = dkf[:, 0:ATTN_BLOCK]
        carry[KV_WIDTH:, :] = dvf[:, 0:ATTN_BLOCK]
        dsink_ref[...] += dsink

    return pl.pallas_call(
        body, name=name, grid=(nt,),
        in_specs=_attn_specs(tq, tile_of) + [pl.BlockSpec((ATTN_WIDTH, tq), lambda t: (0, tile_of(t))),
                                             pl.BlockSpec(memory_space=pltpu.SMEM)],
        out_specs=[pl.BlockSpec((QKV_WIDTH, tq), lambda t: (0, tile_of(t))),
                   pl.BlockSpec((8, LANES), lambda t: (0, 0))],
        out_shape=[SDS((QKV_WIDTH, T), BF16), SDS((8, LANES), F32)],
        scratch_shapes=[pltpu.VMEM((KV_WIDTH, tq + ATTN_BLOCK), BF16)] * 2
        + [pltpu.VMEM((KV_WIDTH, tq + ATTN_BLOCK), F32)] * 2 + [pltpu.VMEM((2 * KV_WIDTH, ATTN_BLOCK), F32)]
        + [pltpu.VMEM((2 * ATTN_BLOCK, GROUP_LANES), BF16)] * 2,
        compiler_params=_params("arbitrary"),
    )(zt, zt, zt, zt, zt, dot_, sinks)


def _layer_norm_stats(v):
    mu = jnp.mean(v, axis=-1, keepdims=True)
    xc = v - mu
    rstd = lax.rsqrt(jnp.mean(xc * xc, axis=-1, keepdims=True) + LN_EPS)
    return rstd, xc * rstd


def _sgu_fwd(zmain, ln_g, ln_b, wm, bias, *, name):
    T = zmain.shape[0]
    ts = _tile(T, TS_SGU, SGU_CHUNK)

    def body(u_ref, v_ref, g_ref, b_ref, w_ref, bias_ref, y_ref):
        u = _gelu(u_ref[...].astype(F32))
        _, vh = _layer_norm_stats(_gelu(v_ref[...].astype(F32)))
        vn = (vh * g_ref[...] + b_ref[...]).astype(BF16)
        for ch in range(ts // SGU_CHUNK):
            rows = slice(ch * SGU_CHUNK, (ch + 1) * SGU_CHUNK)
            for g in range(SGU_GROUPS):
                cols = slice(g * 128, (g + 1) * 128)
                s = _dot(w_ref[g], vn[rows, cols]) + bias_ref[g]
                y_ref[rows, cols] = (u[rows, cols] * s).astype(BF16)

    full = lambda shape: pl.BlockSpec(shape, lambda i: (0,) * len(shape))
    return pl.pallas_call(
        body, name=name, grid=(T // ts,),
        in_specs=[pl.BlockSpec((ts, SGU_WIDTH), lambda i: (i, 0)), pl.BlockSpec((ts, SGU_WIDTH), lambda i: (i, 1)),
                  full((1, SGU_WIDTH)), full((1, SGU_WIDTH)), full(wm.shape), full(bias.shape)],
        out_specs=pl.BlockSpec((ts, SGU_WIDTH), lambda i: (i, 0)),
        out_shape=SDS((T, SGU_WIDTH), BF16),
        compiler_params=_params("parallel"),
    )(zmain, zmain, ln_g, ln_b, wm, bias)


def _sgu_bwd(zmain, dzmain, dy, ln_g, ln_b, wm, wmt, bias, *, name):
    T = zmain.shape[0]
    ts = _tile(T, TS_SGU, SGU_CHUNK)

    def body(u_ref, v_ref, dy_ref, g_ref, b_ref, w_ref, wt_ref, bias_ref, _, dz_ref, dw_ref, db_ref, dlg_ref, dlb_ref,
             dvn):
        @pl.when(pl.program_id(0) == 0)
        def _():
            dw_ref[...] = jnp.zeros_like(dw_ref)
            db_ref[...] = jnp.zeros_like(db_ref)
            dlg_ref[...] = jnp.zeros_like(dlg_ref)
            dlb_ref[...] = jnp.zeros_like(dlb_ref)

        us = u_ref[...].astype(F32)
        vs = v_ref[...].astype(F32)
        u = _gelu(us)
        rstd, vh = _layer_norm_stats(_gelu(vs))
        vn = (vh * g_ref[...] + b_ref[...]).astype(BF16)
        causal = (lax.broadcasted_iota(jnp.int32, (SGU_CHUNK, SGU_CHUNK), 0)
                  >= lax.broadcasted_iota(jnp.int32, (SGU_CHUNK, SGU_CHUNK), 1))
        lane = lax.broadcasted_iota(jnp.int32, (SGU_CHUNK, LANES), 1)
        db = jnp.zeros((SGU_CHUNK, LANES), F32)
        for ch in range(ts // SGU_CHUNK):
            rows = slice(ch * SGU_CHUNK, (ch + 1) * SGU_CHUNK)
            for g in range(SGU_GROUPS):
                cols = slice(g * 128, (g + 1) * 128)
                vng = vn[rows, cols]
                s = _dot(w_ref[g], vng) + bias_ref[g]
                dyf = dy_ref[rows, cols].astype(F32)
                dz_ref[rows, cols] = (dyf * s * _gelu_grad(us[rows, cols])).astype(BF16)
                dsf = dyf * u[rows, cols]
                dsb = dsf.astype(BF16)
                dvn[rows, cols] = _dot(wt_ref[g], dsb)
                dw_ref[g] += jnp.where(causal, _dot_nt(dsb, vng), 0.0)
                db = db + jnp.where(lane == g, jnp.sum(dsf, axis=1, keepdims=True), 0.0)
        db_ref[...] += db
        dvnf = dvn[...]
        dlg_ref[...] += jnp.sum(dvnf * vh, axis=0, keepdims=True)
        dlb_ref[...] += jnp.sum(dvnf, axis=0, keepdims=True)
        dvh = dvnf * g_ref[...]
        dv = rstd * (dvh - jnp.mean(dvh, axis=-1, keepdims=True) - vh * jnp.mean(dvh * vh, axis=-1, keepdims=True))
        dz_ref[:, SGU_WIDTH:] = (dv * _gelu_grad(vs)).astype(BF16)

    full = lambda shape: pl.BlockSpec(shape, lambda i: (0,) * len(shape))
    vec = full((1, SGU_WIDTH))
    return pl.pallas_call(
        body, name=name, grid=(T // ts,),
        in_specs=[pl.BlockSpec((ts, SGU_WIDTH), lambda i: (i, 0)), pl.BlockSpec((ts, SGU_WIDTH), lambda i: (i, 1)),
                  pl.BlockSpec((ts, SGU_WIDTH), lambda i: (i, 0)), vec, vec, full(wm.shape), full(wm.shape),
                  full(bias.shape), pl.BlockSpec(memory_space=pl.ANY)],
        out_specs=[pl.BlockSpec((ts, 2 * SGU_WIDTH), lambda i: (i, 0)), full(wm.shape),
                   full((SGU_CHUNK, LANES)), vec, vec],
        out_shape=[SDS(dzmain.shape, BF16), SDS(wm.shape, F32), SDS((SGU_CHUNK, LANES), F32),
                   SDS((1, SGU_WIDTH), F32), SDS((1, SGU_WIDTH), F32)],
        scratch_shapes=[pltpu.VMEM((ts, SGU_WIDTH), F32)],
        input_output_aliases={8: 0},
        compiler_params=_params("arbitrary"),
    )(zmain, zmain, dy, ln_g, ln_b, wm, wmt, bias, dzmain)


def _merge_fwd(y_attn_t, y_sgu, zmain, w_a, w_s, w_o, x, g_post, *, name):
    T, D = x.shape
    tm = _tile(T, TM_ROW, LANES)

    def body(ya_ref, ys_ref, ga_ref, gb_ref, wa_ref, ws_ref, wo_ref, x_ref, g_ref, xn_ref, pa_ref, ps_ref, o_ref):
        pa = _dot_tn(ya_ref[...], wa_ref[...])
        ps = _dot(ys_ref[...], ws_ref[...])
        pa_ref[...] = pa.astype(BF16)
        ps_ref[...] = ps.astype(BF16)
        merged = _sigmoid(ga_ref[...].astype(F32)) * pa + _sigmoid(gb_ref[...].astype(F32)) * ps
        out = _dot(merged.astype(BF16), wo_ref[...])
        o_ref[...] = out.astype(BF16)
        _, oh = _rms_stats(out)
        xn_ref[...] = x_ref[...] + oh * g_ref[...]

    row = lambda col: pl.BlockSpec((tm, D), lambda i: (i, col))
    wfull = pl.BlockSpec((D, D), lambda i: (0, 0))
    return pl.pallas_call(
        body, name=name, grid=(T // tm,),
        in_specs=[pl.BlockSpec((D, tm), lambda i: (0, i)), row(0), row(2), row(3), wfull, wfull, wfull, row(0),
                  pl.BlockSpec((1, D), lambda i: (0, 0))],
        out_specs=[row(0)] * 4,
        out_shape=[SDS((T, D), F32), SDS((T, D), BF16), SDS((T, D), BF16), SDS((T, D), BF16)],
        compiler_params=_params("parallel"),
    )(y_attn_t, y_sgu, zmain, zmain, w_a, w_s, w_o, x, g_post)


def _merge_bwd(dy, out, g_post, pa, ps, zmain, w_a, w_s, w_o, *, name):
    T, D = dy.shape
    tm = _tile(T, TM_ROW, LANES)

    def body(dy_ref, o_ref, g_ref, pa_ref, ps_ref, ga_ref, gb_ref, wa_ref, ws_ref, wo_ref,
             dz_ref, dout_ref, mg_ref, dpa_ref, dps_ref, dya_ref, dys_ref, dg_ref):
        @pl.when(pl.program_id(0) == 0)
        def _():
            dg_ref[...] = jnp.zeros_like(dg_ref)

        dout, dg = _rms_bwd(o_ref[...].astype(F32), g_ref[...], dy_ref[...])
        dg_ref[...] += dg
        doutb = dout.astype(BF16)
        dout_ref[...] = doutb
        dm = _dot_nt(doutb, wo_ref[...])
        pa = pa_ref[...].astype(F32)
        ps = ps_ref[...].astype(F32)
        sa = _sigmoid(ga_ref[...].astype(F32))
        sb = _sigmoid(gb_ref[...].astype(F32))
        mg_ref[...] = (sa * pa + sb * ps).astype(BF16)
        dpa = (dm * sa).astype(BF16)
        dps = (dm * sb).astype(BF16)
        dpa_ref[...] = dpa
        dps_ref[...] = dps
        dz_ref[:, 0:D] = (dm * pa * sa * (1.0 - sa)).astype(BF16)
        dz_ref[:, D:] = (dm * ps * sb * (1.0 - sb)).astype(BF16)
        dya_ref[...] = _dot_nt(wa_ref[...], dpa).astype(BF16)
        dys_ref[...] = _dot_nt(dps, ws_ref[...]).astype(BF16)

    row = lambda col: pl.BlockSpec((tm, D), lambda i: (i, col))
    wfull = pl.BlockSpec((D, D), lambda i: (0, 0))
    vec = pl.BlockSpec((1, D), lambda i: (0, 0))
    act = SDS((T, D), BF16)
    return pl.pallas_call(
        body, name=name, grid=(T // tm,),
        in_specs=[row(0), row(0), vec, row(0), row(0), row(2), row(3), wfull, wfull, wfull],
        out_specs=[pl.BlockSpec((tm, 2 * D), lambda i: (i, 1))] + [row(0)] * 4
        + [pl.BlockSpec((D, tm), lambda i: (0, i)), row(0), vec],
        out_shape=[SDS(zmain.shape, BF16)] + [act] * 4 + [SDS((D, T), BF16), act, SDS((1, D), F32)],
        compiler_params=_params("arbitrary"),
    )(dy, out, g_post, pa, ps, zmain, zmain, w_a, w_s, w_o)


def _loss_head(y, target, *, name):
    T, D = y.shape
    tm = _tile(T, TM_ROW, 8)

    def body(y_ref, t_ref, dy_ref, l_ref):
        @pl.when(pl.program_id(0) == 0)
        def _():
            l_ref[...] = jnp.zeros_like(l_ref)

        e = y_ref[...] - t_ref[...]
        dy_ref[...] = e * (1.0 / D)
        l_ref[...] += jnp.sum(jnp.mean(e * e, axis=-1, keepdims=True))

    row = pl.BlockSpec((tm, D), lambda i: (i, 0))
    return pl.pallas_call(
        body, name=name, grid=(T // tm,), in_specs=[row, row],
        out_specs=[row, pl.BlockSpec((8, LANES), lambda i: (0, 0))],
        out_shape=[SDS((T, D), F32), SDS((8, LANES), F32)],
        compiler_params=_params("arbitrary"),
    )(y, target)


def _adamw(w, g, m, v, *, name):
    shape = w.shape
    cols = shape[-1]
    rows = w.size // cols
    w2, g2, m2, v2 = (t.reshape(rows, cols) for t in (w, g, m, v))
    tr = _tile(rows, max(8, (256 * 1024) // cols // 8 * 8), 8)

    def body(w_ref, g_ref, m_ref, v_ref, d_ref, nm_ref, nv_ref):
        gg = g_ref[...]
        nm = ADAM_B1 * m_ref[...] + (1.0 - ADAM_B1) * gg
        nv = ADAM_B2 * v_ref[...] + (1.0 - ADAM_B2) * (gg * gg)
        m_hat = nm / (1.0 - ADAM_B1 ** ADAM_STEP)
        v_hat = nv / (1.0 - ADAM_B2 ** ADAM_STEP)
        d_ref[...] = -ADAM_LR * (m_hat / (jnp.sqrt(v_hat) + ADAM_EPS) + ADAM_WD * w_ref[...])
        nm_ref[...] = nm
        nv_ref[...] = nv

    blk = pl.BlockSpec((tr, cols), lambda i: (i, 0))
    outs = pl.pallas_call(
        body, name=name, grid=(rows // tr,), in_specs=[blk] * 4, out_specs=[blk] * 3,
        out_shape=[SDS((rows, cols), F32)] * 3, compiler_params=_params("parallel"),
    )(w2, g2, m2, v2)
    return tuple(o.reshape(shape) for o in outs)


def _sum_terms(terms, n_rows, n_lead, dtypes, *, name):
    cols = terms[0][0].shape[-1]
    tr = _tile(n_rows, 704 if len(terms) <= 4 else 256, SUBLANES_BF16)
    nblk = n_rows // tr
    n_out = len(dtypes)

    def body(*refs):
        acc = refs[0][...].astype(F32)
        for r in refs[1:-n_out]:
            acc = acc + r[...].astype(F32)
        for o_ref in refs[-n_out:]:
            o_ref[...] = acc.astype(o_ref.dtype)

    def spec(lead, first):
        return pl.BlockSpec((1, tr, cols), lambda a, i: (a if lead is None else lead(), first() * nblk + i, 0))

    out = pl.BlockSpec((1, tr, cols), lambda a, i: (a, i, 0))
    return pl.pallas_call(
        body, name=name, grid=(n_lead, nblk), in_specs=[spec(lead, first) for _, lead, first in terms],
        out_specs=[out] * n_out, out_shape=[SDS((n_lead, n_rows, cols), d) for d in dtypes],
        compiler_params=_params("arbitrary", "arbitrary"),
    )(*[a for a, _, _ in terms])


def _position():
    x, y, c = (lax.axis_index(a) for a in AXES)
    chips = [(1 - x, y), (x, 1 - y), (1 - x, 1 - y)]
    return x, y, c, chips


ANY = pl.BlockSpec(memory_space=pl.ANY)


def _remote(src, dst, send_sems, recv_sems, k, to):
    return pltpu.make_async_remote_copy(src_ref=src, dst_ref=dst, send_sem=send_sems.at[k], recv_sem=recv_sems.at[k],
                                        device_id=to, device_id_type=MESH)


def _comm_call(body, arrays, out_shapes, n_sems, *, name):
    n = len(arrays)

    def wrapped(*refs):
        body(refs[:n], refs[n:n + len(out_shapes)], refs[-2], refs[-1])

    return pl.pallas_call(
        wrapped, name=name, in_specs=[ANY] * n, out_specs=[ANY] * len(out_shapes), out_shape=out_shapes,
        scratch_shapes=[pltpu.SemaphoreType.DMA((n_sems,)), pltpu.SemaphoreType.DMA((n_sems,))],
    )(*arrays)


def _gather_shards(packs, *, name):
    def body(p_refs, o_refs, send_sems, recv_sems):
        x, y, c, chips = _position()
        sent, passed = [], []
        for a, (p_ref, o_ref) in enumerate(zip(p_refs, o_refs)):
            mine = pl.ds(c * (p_ref.shape[0] // 2), p_ref.shape[0] // 2)
            sent += [_remote(p_ref.at[mine], o_ref.at[2 * x + y, mine], send_sems, recv_sems, 6 * a + j, (*chip, c))
                     for j, chip in enumerate(chips)]
        for cp in sent:
            cp.start()
        for a, o_ref in enumerate(o_refs):
            mine = pl.ds(c * (o_ref.shape[1] // 2), o_ref.shape[1] // 2)
            for j, (cx, cy) in enumerate(chips):
                landed = o_ref.at[2 * cx + cy, mine]
                _remote(landed, landed, send_sems, recv_sems, 6 * a + j, (x, y, c)).wait_recv()
                cp = _remote(landed, landed, send_sems, recv_sems, 6 * a + 3 + j, (x, y, 1 - c))
                cp.start()
                passed.append(cp)
        for a, o_ref in enumerate(o_refs):
            other = pl.ds((1 - c) * (o_ref.shape[1] // 2), o_ref.shape[1] // 2)
            for j, (cx, cy) in enumerate(chips):
                theirs = o_ref.at[2 * cx + cy, other]
                _remote(theirs, theirs, send_sems, recv_sems, 6 * a + 3 + j, (x, y, c)).wait_recv()
        for cp in sent + passed:
            cp.wait_send()

    return _comm_call(body, packs, [SDS((N_CHIPS,) + p.shape, p.dtype) for p in packs], 6 * len(packs), name=name)


def _sibling_exchange(gs, *, name):
    def body(g_refs, o_refs, send_sems, recv_sems):
        x, y, c, _ = _position()
        sent = [_remote(g_ref.at[:, pl.ds((1 - c) * o_ref.shape[1], o_ref.shape[1])], o_ref, send_sems, recv_sems, a,
                        (x, y, 1 - c)) for a, (g_ref, o_ref) in enumerate(zip(g_refs, o_refs))]
        for cp in sent:
            cp.start()
        for cp in sent:
            cp.wait()

    return _comm_call(body, gs, [SDS((N_CHIPS, g.shape[1] // 2, g.shape[2]), g.dtype) for g in gs], len(gs), name=name)


def _chip_scatter(ps, *, name):
    def body(p_refs, o_refs, send_sems, recv_sems):
        x, y, c, chips = _position()
        sent = [_remote(p_ref.at[2 * cx + cy], o_ref.at[j], send_sems, recv_sems, 3 * a + j, (cx, cy, c))
                for a, (p_ref, o_ref) in enumerate(zip(p_refs, o_refs)) for j, (cx, cy) in enumerate(chips)]
        for cp in sent:
            cp.start()
        for cp in sent:
            cp.wait()

    return _comm_call(body, ps, [SDS((N_CHIPS - 1,) + p.shape[1:], p.dtype) for p in ps], 3 * len(ps), name=name)


def _sibling_swap(ts, *, name):
    def body(t_refs, o_refs, send_sems, recv_sems):
        x, y, c, _ = _position()
        sent = [_remote(t_ref, o_ref, send_sems, recv_sems, a, (x, y, 1 - c))
                for a, (t_ref, o_ref) in enumerate(zip(t_refs, o_refs))]
        for cp in sent:
            cp.start()
        for cp in sent:
            cp.wait()

    return _comm_call(body, ts, [SDS(t.shape, t.dtype) for t in ts], len(ts), name=name)


def _gather_all(v, *, name):
    M, C = v.shape

    def body(v_ref, o_ref, send_sems, recv_sems):
        x, y, c, chips = _position()
        slot = lambda px, py, pc: o_ref.at[4 * px + 2 * py + pc]
        first = [_remote(v_ref, slot(x, y, c), send_sems, recv_sems, 0, (x, y, 1 - c))]
        first += [_remote(v_ref, slot(x, y, c), send_sems, recv_sems, 1 + j, (*chip, c)) for j, chip in enumerate(chips)]
        for cp in first:
            cp.start()
        passed = []
        for j, chip in enumerate(chips):
            landed = slot(*chip, c)
            _remote(landed, landed, send_sems, recv_sems, 1 + j, (x, y, c)).wait_recv()
            cp = _remote(landed, landed, send_sems, recv_sems, 4 + j, (x, y, 1 - c))
            cp.start()
            passed.append(cp)
        sib = slot(x, y, 1 - c)
        _remote(sib, sib, send_sems, recv_sems, 0, (x, y, c)).wait_recv()
        for j, chip in enumerate(chips):
            theirs = slot(*chip, 1 - c)
            _remote(theirs, theirs, send_sems, recv_sems, 4 + j, (x, y, c)).wait_recv()
        for cp in first + passed:
            cp.wait_send()

    return pl.pallas_call(
        body, name=name, in_specs=[ANY], out_specs=ANY, out_shape=SDS((N_DEV, M, C), v.dtype),
        scratch_shapes=[pltpu.SemaphoreType.DMA((7,)), pltpu.SemaphoreType.DMA((7,))],
    )(v)


BIG = ("ffn1_w1", "ffn1_w2", "w_in", "w_attn_branch", "w_sgu_branch", "w_out", "ffn2_w1", "ffn2_w2")
COL_SHARDED = ("ffn1_w1", "w_in", "ffn2_w1")
SMALL = ("ffn1_pre_g", "ffn1_post_g", "mix_pre_g", "attn_sinks", "sgu_ln_g", "sgu_ln_b", "sgu_w", "sgu_b",
         "mix_post_g", "ffn2_pre_g", "ffn2_post_g")
WEIGHTS = ("ffn1_pre_g", "ffn1_w1", "ffn1_w2", "ffn1_post_g", "mix_pre_g", "w_in", "attn_sinks", "sgu_ln_g",
           "sgu_ln_b", "sgu_w", "sgu_b", "w_attn_branch", "w_sgu_branch", "w_out", "mix_post_g", "ffn2_pre_g",
           "ffn2_w1", "ffn2_w2", "ffn2_post_g")


def _width_classes(shard_shapes):
    widths = sorted({shard_shapes[n][-1] for n in BIG}, reverse=True)
    return [[n for n in BIG if shard_shapes[n][-1] == w] for w in widths]


def _pack_class(names, n_layers, shard_of, dtype):
    return jnp.concatenate([shard_of(n, l).astype(dtype) for l in range(n_layers) for n in names], axis=0)


def _unpack_class(rows, names, n_layers, shard_shapes):
    out, r0 = {}, 0
    for l in range(n_layers):
        for n in names:
            out[n, l] = rows[r0:r0 + shard_shapes[n][0]]
            r0 += shard_shapes[n][0]
    return out


def _ffn_fwd(x, pre_g, w1, w2, post_g, tag):
    a, h = _norm_matmul(x, pre_g, w1, name=f"{tag}_up", with_h=True)
    xn, o = _swiglu_out(a, w2, x, post_g, name=f"{tag}_down")
    return xn, (x, h, a, o)


def _ffn_bwd(dy, saved, pre_g, w1, w2, post_g, tag):
    x, h, a, o = saved
    da, s, do, d_post = _ffn_bwd_hidden(dy, o, post_g, a, w2, name=f"{tag}_bwd_hidden")
    dw2 = _matmul_tn(s, do, name=f"{tag}_dw2")
    dx, d_pre = _matmul_nt_norm_bwd(da, w1, x, pre_g, dy, None, name=f"{tag}_bwd_in")
    dw1 = _matmul_tn(h, da, name=f"{tag}_dw1")
    return dx, dw1, dw2, d_pre, d_post


def kernel(x, ffn1_pre_g, ffn1_w1, ffn1_w2, ffn1_post_g, mix_pre_g, w_in, attn_sinks, sgu_ln_g, sgu_ln_b, sgu_w, sgu_b, w_attn_branch, w_sgu_branch, w_out, mix_post_g, ffn2_pre_g, ffn2_w1, ffn2_w2, ffn2_post_g, loss_target, m_ffn1_pre_g, m_ffn1_w1, m_ffn1_w2, m_ffn1_post_g, m_mix_pre_g, m_w_in, m_attn_sinks, m_sgu_ln_g, m_sgu_ln_b, m_sgu_w, m_sgu_b, m_w_attn_branch, m_w_sgu_branch, m_w_out, m_mix_post_g, m_ffn2_pre_g, m_ffn2_w1, m_ffn2_w2, m_ffn2_post_g, v_ffn1_pre_g, v_ffn1_w1, v_ffn1_w2, v_ffn1_post_g, v_mix_pre_g, v_w_in, v_attn_sinks, v_sgu_ln_g, v_sgu_ln_b, v_sgu_w, v_sgu_b, v_w_attn_branch, v_w_sgu_branch, v_w_out, v_mix_post_g, v_ffn2_pre_g, v_ffn2_w1, v_ffn2_w2, v_ffn2_post_g):
    given = dict(locals())
    W = {n: given[n] for n in WEIGHTS}
    M = {n: given["m_" + n] for n in WEIGHTS}
    V = {n: given["v_" + n] for n in WEIGHTS}
    L = ffn1_w1.shape[0]
    T, D = x.shape[1], x.shape[2]
    xt = x.reshape(T, D)
    target = loss_target.reshape(T, D)
    assert L % 2 == 0 and D == ATTN_WIDTH == SGU_WIDTH and T % ATTN_BLOCK == 0

    shard_shapes = {n: W[n].shape[1:] for n in BIG}
    classes = _width_classes(shard_shapes)
    my_chip = 2 * lax.axis_index("x") + lax.axis_index("y")
    my_core = lax.axis_index("c")
    packs = [_pack_class(names, L, lambda n, l: W[n][l], BF16) for names in classes]
    gathered = _gather_shards(packs, name="gather_weights")
    full = [{} for _ in range(L)]
    for names, pack, got in zip(classes, packs, gathered):
        per_chip = [_unpack_class(jnp.where(my_chip == s, pack, got[s]), names, L, shard_shapes)
                    for s in range(N_CHIPS)]
        for l in range(L):
            for n in names:
                full[l][n] = jnp.concatenate([pc[n, l] for pc in per_chip], axis=1 if n in COL_SHARDED else 0)

    row = lambda name, l: W[name][l].reshape(1, -1)
    causal = jnp.tril(jnp.ones((SGU_CHUNK, SGU_CHUNK), dtype=bool))
    saved = []
    h_cur = xt
    for l in range(L):
        fw = full[l]
        sv = {}
        h_cur, sv["ffn1"] = _ffn_fwd(h_cur, row("ffn1_pre_g", l), fw["ffn1_w1"], fw["ffn1_w2"], row("ffn1_post_g", l),
                                     f"l{l}_ffn1")
        w_qkv, w_main = fw["w_in"][:, :QKV_WIDTH], fw["w_in"][:, QKV_WIDTH:]
        zqkv, hm = _norm_matmul_t(h_cur, row("mix_pre_g", l), w_qkv.T, name=f"l{l}_mix_in_qkv")
        zmain, = _norm_matmul(h_cur, row("mix_pre_g", l), w_main, name=f"l{l}_mix_in_main", with_h=False)
        wm = jnp.where(causal[None], sgu_w[l], 0.0).astype(BF16)
        wmt = jnp.swapaxes(wm, 1, 2)
        bias = jnp.broadcast_to(sgu_b[l][:, :, None], (SGU_GROUPS, SGU_CHUNK, 128)).astype(F32)
        y_attn = _attn_fwd(zqkv, attn_sinks[l], name=f"l{l}_attn")
        y_sgu = _sgu_fwd(zmain, row("sgu_ln_g", l), row("sgu_ln_b", l), wm, bias, name=f"l{l}_sgu")
        x_mix = h_cur
        h_cur, pa, ps, mo = _merge_fwd(y_attn, y_sgu, zmain, fw["w_attn_branch"], fw["w_sgu_branch"], fw["w_out"],
                                       x_mix, row("mix_post_g", l), name=f"l{l}_merge")
        sv["mix"] = (x_mix, hm, zqkv, zmain, y_attn, y_sgu, pa, ps, mo, wm, wmt, bias)
        h_cur, sv["ffn2"] = _ffn_fwd(h_cur, row("ffn2_pre_g", l), fw["ffn2_w1"], fw["ffn2_w2"], row("ffn2_post_g", l),
                                     f"l{l}_ffn2")
        saved.append(sv)

    dy, lsum = _loss_head(h_cur, target, name="loss_head")
    loss = lax.psum(0.5 * lsum[0, 0], AXES)

    big_grads = [None] * L
    small_grads = [None] * L
    for l in reversed(range(L)):
        fw, sv = full[l], saved[l]
        gb, gs = {}, {}
        dy, gb["ffn2_w1"], gb["ffn2_w2"], gs["ffn2_pre_g"], gs["ffn2_post_g"] = _ffn_bwd(
            dy, sv["ffn2"], row("ffn2_pre_g", l), fw["ffn2_w1"], fw["ffn2_w2"], row("ffn2_post_g", l), f"l{l}_ffn2")

        x_mix, hm, zqkv, zmain, y_attn, y_sgu, pa, ps, mo, wm, wmt, bias = sv["mix"]
        dzmain, dout, merged, dpa, dps, dya, dys, gs["mix_post_g"] = _merge_bwd(
            dy, mo, row("mix_post_g", l), pa, ps, zmain, fw["w_attn_branch"], fw["w_sgu_branch"], fw["w_out"],
            name=f"l{l}_merge_bwd")
        gb["w_out"] = _matmul_tn(merged, dout, name=f"l{l}_dw_out")
        gb["w_attn_branch"] = _matmul_tokens(y_attn, dpa, name=f"l{l}_dw_attn")
        gb["w_sgu_branch"] = _matmul_tn(y_sgu, dps, name=f"l{l}_dw_sgu")
        dzqkv, dsink = _attn_bwd(zqkv, attn_sinks[l], dya, name=f"l{l}_attn_bwd")
        gs["attn_sinks"] = dsink[0, :N_Q_HEADS]
        dzmain, dsw, dsb, gs["sgu_ln_g"], gs["sgu_ln_b"] = _sgu_bwd(
            zmain, dzmain, dys, row("sgu_ln_g", l), row("sgu_ln_b", l), wm, wmt, bias, name=f"l{l}_sgu_bwd")
        gs["sgu_w"] = dsw
        gs["sgu_b"] = dsb[:, :SGU_GROUPS].T
        w_qkv, w_main = fw["w_in"][:, :QKV_WIDTH], fw["w_in"][:, QKV_WIDTH:]
        dh_qkv = _matmul_tn_rows(dzqkv, w_qkv.T, name=f"l{l}_mix_bwd_qkv")
        dy, gs["mix_pre_g"] = _matmul_nt_norm_bwd(dzmain, w_main, x_mix, row("mix_pre_g", l), dy, dh_qkv,
                                                   name=f"l{l}_mix_bwd_in")
        gb["w_in"] = jnp.concatenate([_matmul_tokens(dzqkv, hm, name=f"l{l}_dw_in_qkv").T,
                                      _matmul_tn(hm, dzmain, name=f"l{l}_dw_in_main")], axis=1)

        dy, gb["ffn1_w1"], gb["ffn1_w2"], gs["ffn1_pre_g"], gs["ffn1_post_g"] = _ffn_bwd(
            dy, sv["ffn1"], row("ffn1_pre_g", l), fw["ffn1_w1"], fw["ffn1_w2"], row("ffn1_post_g", l), f"l{l}_ffn1")
        big_grads[l], small_grads[l] = gb, gs
    grad_x = dy.reshape(x.shape)

    def shard_of(n, g, s):
        k = shard_shapes[n][1 if n in COL_SHARDED else 0]
        return g[:, s * k:(s + 1) * k] if n in COL_SHARDED else g[s * k:(s + 1) * k]

    grs = [jnp.stack([_pack_class(names, L, lambda n, l: shard_of(n, big_grads[l][n], s), F32)
                      for s in range(N_CHIPS)]) for names in classes]
    from_sibling = _sibling_exchange(grs, name="grads_sibling_exchange")
    zero = lambda: 0
    core_in_map = lambda: lax.axis_index("c")
    chip_in_map = lambda: 2 * lax.axis_index("x") + lax.axis_index("y")
    pairs = [_sum_terms([(g, None, core_in_map), (fs, None, zero)], fs.shape[1], N_CHIPS, (F32, BF16),
                        name=f"grads_pair_sum{k}") for k, (g, fs) in enumerate(zip(grs, from_sibling))]
    from_chips = _chip_scatter([p[1] for p in pairs], name="grads_chip_scatter")
    halves = [_sum_terms([(p[0], chip_in_map, zero)] + [(fc, (lambda j=j: j), zero) for j in range(N_CHIPS - 1)],
                         fc.shape[1], 1, (F32,), name=f"grads_chip_sum{k}")[0][0]
              for k, (p, fc) in enumerate(zip(pairs, from_chips))]
    others = _sibling_swap(halves, name="grads_sibling_swap")

    grads = {n: [None] * L for n in WEIGHTS}
    for names, half, other in zip(classes, halves, others):
        reduced = jnp.concatenate([jnp.where(my_core == 0, half, other), jnp.where(my_core == 0, other, half)], axis=0)
        for (n, l), g in _unpack_class(reduced, names, L, shard_shapes).items():
            grads[n][l] = g

    def small_rows(gs):
        parts = []
        for n in SMALL:
            flat = gs[n].reshape(-1)
            pad = (-flat.shape[0]) % D
            parts.append(jnp.pad(flat, (0, pad)).reshape(-1, D))
        return jnp.concatenate(parts, axis=0)

    spack = jnp.concatenate([small_rows(small_grads[l]) for l in range(L)], axis=0)
    n_small = spack.shape[0]
    pad_rows = (-n_small) % SUBLANES_BF16
    spack = jnp.pad(spack, ((0, pad_rows), (0, 0)))
    everyone = _gather_all(spack, name="small_grads_gather")
    is_me = (jnp.arange(N_DEV) == 2 * my_chip + my_core)[:, None, None]
    everyone = jnp.where(is_me, spack[None], everyone)
    ssum = _sum_terms([(everyone, (lambda d=d: d), zero) for d in range(N_DEV)], spack.shape[0], 1, (F32,),
                      name="small_grads_sum")[0][0]
    per_layer = n_small // L
    for l in range(L):
        r0 = l * per_layer
        for n in SMALL:
            shp = W[n].shape[1:]
            size = math.prod(shp)
            nr = -(-size // D)
            grads[n][l] = ssum[r0:r0 + nr].reshape(-1)[:size].reshape(shp)
            r0 += nr
    grads = {n: jnp.stack(grads[n]) for n in WEIGHTS}

    delta, new_m, new_v = {}, {}, {}
    for n in WEIGHTS:
        delta[n], new_m[n], new_v[n] = _adamw(W[n], grads[n], M[n], V[n], name=f"adamw_{n}")

    return (loss, grad_x, *[grads[n] for n in WEIGHTS], *[delta[n] for n in WEIGHTS],
            *[new_m[n] for n in WEIGHTS], *[new_v[n] for n in WEIGHTS])
```

```python
import functools
import math

import jax
import jax.numpy as jnp
from jax import lax
from jax.experimental import pallas as pl
from jax.experimental.pallas import tpu as pltpu

F32, BF16 = jnp.float32, jnp.bfloat16
SDS = jax.ShapeDtypeStruct
MESH = pl.DeviceIdType.MESH
AXES = ("x", "y", "c")

HEAD_DIM = 64
N_Q_HEADS = 16
N_KV_HEADS = 2
Q_PER_KV = N_Q_HEADS // N_KV_HEADS
ATTN_WIDTH = N_Q_HEADS * HEAD_DIM
KV_WIDTH = N_KV_HEADS * HEAD_DIM
ATTN_BLOCK = 128
SGU_CHUNK = 128
SGU_GROUPS = 8
SGU_WIDTH = SGU_GROUPS * 128
QKV_WIDTH = ATTN_WIDTH + 2 * KV_WIDTH
RMS_EPS = 1e-6
LN_EPS = 1e-5
MASK_VALUE = -1e30
ATTN_SCALE = 1.0 / math.sqrt(HEAD_DIM)

ADAM_LR, ADAM_B1, ADAM_B2, ADAM_EPS, ADAM_WD, ADAM_STEP = 0.001, 0.9, 0.999, 1e-08, 0.01, 10

N_CHIPS = 4
N_DEV = 8

VMEM_LIMIT_BYTES = 56 * 1024 * 1024
LANES = 128
SUBLANES_BF16 = 16

TM_NORM_MATMUL = 1024
TM_ROW = 512
TM_FFN_BWD = 512
TT_REDUCE = 1024
TQ_ATTN = 512
TS_SGU = 512


def _tile(n, pref, mult):
    t = (min(pref, n) // mult) * mult
    while t >= mult:
        if n % t == 0:
            return t
        t -= mult
    return n


def _params(*sem):
    return pltpu.CompilerParams(dimension_semantics=sem, vmem_limit_bytes=VMEM_LIMIT_BYTES)


def _dot(a, b):
    return jnp.dot(a, b, preferred_element_type=F32)


def _dot_nt(a, b):
    return lax.dot_general(a, b, (((1,), (1,)), ((), ())), preferred_element_type=F32)


def _dot_tn(a, b):
    return lax.dot_general(a, b, (((0,), (0,)), ((), ())), preferred_element_type=F32)


def _sigmoid(x):
    return 0.5 * (1.0 + jnp.tanh(0.5 * x))


def _rms_stats(xf):
    r = lax.rsqrt(jnp.mean(xf * xf, axis=-1, keepdims=True) + RMS_EPS)
    return r, xf * r


def _rms_bwd(xf, g, dy):
    r, xh = _rms_stats(xf)
    dyg = dy * g
    dx = r * (dyg - xh * jnp.mean(dyg * xh, axis=-1, keepdims=True))
    return dx, jnp.sum(dy * xh, axis=0, keepdims=True)


def _gelu_parts(x):
    cdf = 0.5 * (1.0 + lax.erf(x * (1.0 / math.sqrt(2.0))))
    return cdf


def _gelu(x):
    return x * _gelu_parts(x)


def _gelu_grad(x):
    return _gelu_parts(x) + x * jnp.exp(-0.5 * x * x) * (1.0 / math.sqrt(2.0 * math.pi))


def _resident(shape):
    return pl.BlockSpec(shape, lambda *_: (0,) * len(shape), pipeline_mode=pl.Buffered(1))


def _norm_matmul(x, g, w3, *, name, with_h):
    T, D = x.shape
    nj, _, tn = w3.shape
    tm = _tile(T, TM_NORM_MATMUL, SUBLANES_BF16)

    def body(x_ref, g_ref, w_ref, a_ref, *rest):
        h_sc = rest[-1]

        @pl.when(pl.program_id(1) == 0)
        def _():
            _, xh = _rms_stats(x_ref[...])
            h = (xh * g_ref[...]).astype(BF16)
            h_sc[...] = h
            if with_h:
                rest[0][...] = h

        a_ref[...] = _dot(h_sc[...], w_ref[pl.program_id(1)]).astype(BF16)

    out_specs = [pl.BlockSpec((tm, tn), lambda i, j: (i, j))]
    out_shape = [SDS((T, nj * tn), BF16)]
    if with_h:
        out_specs.append(pl.BlockSpec((tm, D), lambda i, j: (i, 0)))
        out_shape.append(SDS((T, D), BF16))
    return pl.pallas_call(
        body, name=name, grid=(T // tm, nj),
        in_specs=[pl.BlockSpec((tm, D), lambda i, j: (i, 0)),
                  pl.BlockSpec((1, D), lambda i, j: (0, 0)),
                  _resident(w3.shape)],
        out_specs=out_specs, out_shape=out_shape,
        scratch_shapes=[pltpu.VMEM((tm, D), BF16)],
        compiler_params=_params("parallel", "arbitrary"),
    )(x, g, w3)


def _norm_matmul_t(x, g, wt, *, name):
    T, D = x.shape
    N = wt.shape[0]
    tm = _tile(T, TM_ROW, LANES)

    def body(x_ref, g_ref, w_ref, a_ref, h_ref):
        _, xh = _rms_stats(x_ref[...])
        h = (xh * g_ref[...]).astype(BF16)
        h_ref[...] = h
        a_ref[...] = _dot_nt(w_ref[...], h).astype(BF16)

    return pl.pallas_call(
        body, name=name, grid=(T // tm,),
        in_specs=[pl.BlockSpec((tm, D), lambda i: (i, 0)), pl.BlockSpec((1, D), lambda i: (0, 0)),
                  _resident((N, D))],
        out_specs=[pl.BlockSpec((N, tm), lambda i: (0, i)), pl.BlockSpec((tm, D), lambda i: (i, 0))],
        out_shape=[SDS((N, T), BF16), SDS((T, D), BF16)],
        compiler_params=_params("parallel"),
    )(x, g, wt)


def _matmul_tokens(at, b, *, name):
    K, T = at.shape
    N = b.shape[1]
    tt = _tile(T, TT_REDUCE, LANES)

    def body(a_ref, b_ref, o_ref):
        @pl.when(pl.program_id(0) == 0)
        def _():
            o_ref[...] = jnp.zeros_like(o_ref)

        o_ref[...] += _dot(a_ref[...], b_ref[...])

    return pl.pallas_call(
        body, name=name, grid=(T // tt,),
        in_specs=[pl.BlockSpec((K, tt), lambda t: (0, t)), pl.BlockSpec((tt, N), lambda t: (t, 0))],
        out_specs=pl.BlockSpec((K, N), lambda t: (0, 0)),
        out_shape=SDS((K, N), F32),
        compiler_params=_params("arbitrary"),
    )(at, b)


def _matmul_tn_rows(dat, wt, *, name):
    N, T = dat.shape
    D = wt.shape[1]
    tm = _tile(T, TM_ROW, LANES)

    def body(da_ref, w_ref, o_ref):
        o_ref[...] = _dot_tn(da_ref[...], w_ref[...])

    return pl.pallas_call(
        body, name=name, grid=(T // tm,),
        in_specs=[pl.BlockSpec((N, tm), lambda i: (0, i)), _resident((N, D))],
        out_specs=pl.BlockSpec((tm, D), lambda i: (i, 0)),
        out_shape=SDS((T, D), F32),
        compiler_params=_params("parallel"),
    )(dat, wt)


def _ff_chunk(F):
    return F if F <= 1408 else F // 2


def _swiglu_out(a, w2, x, g_post, *, name):
    T, F2 = a.shape
    F = F2 // 2
    D = x.shape[1]
    tm = _tile(T, TM_ROW, SUBLANES_BF16)
    fc = _ff_chunk(F)

    def body(a_ref, w_ref, x_ref, g_ref, xn_ref, o_ref):
        acc = None
        for c0 in range(0, F, fc):
            gt = a_ref[:, c0:c0 + fc].astype(F32)
            ut = a_ref[:, F + c0:F + c0 + fc].astype(F32)
            s = (gt * _sigmoid(gt) * ut).astype(BF16)
            part = _dot(s, w_ref[c0:c0 + fc, :])
            acc = part if acc is None else acc + part
        o_ref[...] = acc.astype(BF16)
        _, oh = _rms_stats(acc)
        xn_ref[...] = x_ref[...] + 0.5 * (oh * g_ref[...])

    return pl.pallas_call(
        body, name=name, grid=(T // tm,),
        in_specs=[pl.BlockSpec((tm, F2), lambda i: (i, 0)),
                  _resident((F, D)),
                  pl.BlockSpec((tm, D), lambda i: (i, 0)),
                  pl.BlockSpec((1, D), lambda i: (0, 0))],
        out_specs=[pl.BlockSpec((tm, D), lambda i: (i, 0)), pl.BlockSpec((tm, D), lambda i: (i, 0))],
        out_shape=[SDS((T, D), F32), SDS((T, D), BF16)],
        compiler_params=_params("parallel"),
    )(a, w2, x, g_post)


def _ffn_bwd_hidden(dy, o, g_post, a, w2, *, name):
    T, F2 = a.shape
    F = F2 // 2
    D = dy.shape[1]
    tm = _tile(T, TM_FFN_BWD, SUBLANES_BF16)
    fc = _tile(F, 256, LANES)

    def body(dy_ref, o_ref, g_ref, a_ref, w_ref, da_ref, s_ref, do_ref, dg_ref):
        @pl.when(pl.program_id(0) == 0)
        def _():
            dg_ref[...] = jnp.zeros_like(dg_ref)

        do, dg = _rms_bwd(o_ref[...].astype(F32), g_ref[...], 0.5 * dy_ref[...])
        dg_ref[...] += dg
        dob = do.astype(BF16)
        do_ref[...] = dob
        for c0 in range(0, F, fc):
            ds = _dot_nt(dob, w_ref[c0:c0 + fc, :])
            gt = a_ref[:, c0:c0 + fc].astype(F32)
            ut = a_ref[:, F + c0:F + c0 + fc].astype(F32)
            sg = _sigmoid(gt)
            sl = gt * sg
            s_ref[:, c0:c0 + fc] = (sl * ut).astype(BF16)
            da_ref[:, c0:c0 + fc] = (ds * ut * (sg * (1.0 + gt * (1.0 - sg)))).astype(BF16)
            da_ref[:, F + c0:F + c0 + fc] = (ds * sl).astype(BF16)

    row = lambda w: pl.BlockSpec((tm, w), lambda i: (i, 0))
    return pl.pallas_call(
        body, name=name, grid=(T // tm,),
        in_specs=[row(D), row(D), pl.BlockSpec((1, D), lambda i: (0, 0)), row(F2),
                  _resident((F, D))],
        out_specs=[row(F2), row(F), row(D), pl.BlockSpec((1, D), lambda i: (0, 0))],
        out_shape=[SDS((T, F2), BF16), SDS((T, F), BF16), SDS((T, D), BF16), SDS((1, D), F32)],
        compiler_params=_params("arbitrary"),
    )(dy, o, g_post, a, w2)


def _matmul_tn(a, b, *, name):
    T, K = a.shape
    N = b.shape[1]
    tk = _tile(K, 1408, LANES)
    tn = _tile(N, 1408, LANES)
    tt = _tile(T, TT_REDUCE, SUBLANES_BF16)

    def body(a_ref, b_ref, o_ref):
        @pl.when(pl.program_id(2) == 0)
        def _():
            o_ref[...] = jnp.zeros_like(o_ref)

        o_ref[...] += _dot_tn(a_ref[...], b_ref[...])

    return pl.pallas_call(
        body, name=name, grid=(K // tk, N // tn, T // tt),
        in_specs=[pl.BlockSpec((tt, tk), lambda k, n, t: (t, k)),
                  pl.BlockSpec((tt, tn), lambda k, n, t: (t, n))],
        out_specs=pl.BlockSpec((tk, tn), lambda k, n, t: (k, n)),
        out_shape=SDS((K, N), F32),
        compiler_params=_params("parallel", "parallel", "arbitrary"),
    )(a, b)


def _matmul_nt_norm_bwd(da, w, x, g, dy, init, *, name):
    T, N = da.shape
    nj, D, tn = w.shape
    tm = _tile(T, TM_ROW, SUBLANES_BF16)
    has_init = init is not None

    def body(da_ref, w_ref, x_ref, g_ref, dy_ref, *rest):
        dx_ref, dg_ref = rest[-2:]

        @pl.when(pl.program_id(0) == 0)
        def _():
            dg_ref[...] = jnp.zeros_like(dg_ref)

        dh = rest[0][...] if has_init else None
        for j in range(nj):
            part = _dot_nt(da_ref[:, j * tn:(j + 1) * tn], w_ref[j])
            dh = part if dh is None else dh + part
        dx, dg = _rms_bwd(x_ref[...], g_ref[...], dh)
        dx_ref[...] = dy_ref[...] + dx
        dg_ref[...] += dg

    row = pl.BlockSpec((tm, D), lambda i: (i, 0))
    vec = pl.BlockSpec((1, D), lambda i: (0, 0))
    in_specs = [pl.BlockSpec((tm, N), lambda i: (i, 0)), _resident(w.shape), row, vec, row]
    args = [da, w, x, g, dy]
    if has_init:
        in_specs.append(row)
        args.append(init)
    return pl.pallas_call(
        body, name=name, grid=(T // tm,), in_specs=in_specs,
        out_specs=[row, vec], out_shape=[SDS((T, D), F32), SDS((1, D), F32)],
        compiler_params=_params("arbitrary"),
    )(*args)


GROUP_LANES = Q_PER_KV * ATTN_BLOCK


def _attn_mask_t(first):
    kj = lax.broadcasted_iota(jnp.int32, (2 * ATTN_BLOCK, ATTN_BLOCK), 0)
    qi = lax.broadcasted_iota(jnp.int32, (2 * ATTN_BLOCK, ATTN_BLOCK), 1)
    rel = qi + ATTN_BLOCK - kj
    band = (rel >= 0) & (rel < ATTN_BLOCK)
    if first is False:
        return band
    return band & ((kj >= ATTN_BLOCK) | jnp.logical_not(first))


def _attn_probs_t(st, valid, sink):
    s = jnp.where(valid, st, MASK_VALUE)
    m = jnp.maximum(jnp.max(s, axis=0, keepdims=True), sink)
    p = jnp.exp(s - m)
    es = jnp.exp(sink - m)
    inv = 1.0 / (jnp.sum(p, axis=0, keepdims=True) + es)
    return p * inv, es * inv


def _attn_specs(tq, tile_of):
    nb = tq // ATTN_BLOCK
    krow, vrow = ATTN_WIDTH // KV_WIDTH, ATTN_WIDTH // KV_WIDTH + 1
    halo = lambda r: pl.BlockSpec((KV_WIDTH, ATTN_BLOCK), lambda t: (r, jnp.maximum(tile_of(t) * nb - 1, 0)))
    return [pl.BlockSpec((ATTN_WIDTH, tq), lambda t: (0, tile_of(t))),
            pl.BlockSpec((KV_WIDTH, tq), lambda t: (krow, tile_of(t))),
            pl.BlockSpec((KV_WIDTH, tq), lambda t: (vrow, tile_of(t))),
            halo(krow), halo(vrow)]


def _head_rows(g, r):
    h = g * Q_PER_KV + r
    return h, slice(h * HEAD_DIM, (h + 1) * HEAD_DIM)


def _group_stack(ref, g, cols):
    return jnp.concatenate([ref[_head_rows(g, r)[1], cols] for r in range(Q_PER_KV)], axis=1)


def _attn_fwd(zt, sinks, *, name):
    T = zt.shape[1]
    tq = _tile(T, TQ_ATTN, ATTN_BLOCK)
    nb = tq // ATTN_BLOCK

    def body(q_ref, k_ref, v_ref, kh_ref, vh_ref, s_ref, o_ref, kf, vf, pt):
        kf[:, 0:ATTN_BLOCK] = kh_ref[...]
        kf[:, ATTN_BLOCK:] = k_ref[...]
        vf[:, 0:ATTN_BLOCK] = vh_ref[...]
        vf[:, ATTN_BLOCK:] = v_ref[...]
        for b in range(nb):
            cols = slice(b * ATTN_BLOCK, (b + 1) * ATTN_BLOCK)
            win = slice(b * ATTN_BLOCK, (b + 2) * ATTN_BLOCK)
            valid = _attn_mask_t((pl.program_id(0) == 0) if b == 0 else False)
            for g in range(N_KV_HEADS):
                gr = slice(g * HEAD_DIM, (g + 1) * HEAD_DIM)
                st = _dot_tn(kf[gr, win], _group_stack(q_ref, g, cols)) * ATTN_SCALE
                for r in range(Q_PER_KV):
                    h, _ = _head_rows(g, r)
                    sl = slice(r * ATTN_BLOCK, (r + 1) * ATTN_BLOCK)
                    probs, _ = _attn_probs_t(st[:, sl], valid, s_ref[h])
                    pt[:, sl] = probs.astype(BF16)
                ot = _dot(vf[gr, win], pt[...])
                for r in range(Q_PER_KV):
                    o_ref[_head_rows(g, r)[1], cols] = ot[:, r * ATTN_BLOCK:(r + 1) * ATTN_BLOCK].astype(BF16)

    return pl.pallas_call(
        body, name=name, grid=(T // tq,),
        in_specs=_attn_specs(tq, lambda t: t) + [pl.BlockSpec(memory_space=pltpu.SMEM)],
        out_specs=pl.BlockSpec((ATTN_WIDTH, tq), lambda t: (0, t)),
        out_shape=SDS((ATTN_WIDTH, T), BF16),
        scratch_shapes=[pltpu.VMEM((KV_WIDTH, tq + ATTN_BLOCK), BF16)] * 2
        + [pltpu.VMEM((2 * ATTN_BLOCK, GROUP_LANES), BF16)],
        compiler_params=_params("parallel"),
    )(zt, zt, zt, zt, zt, sinks)


def _attn_bwd(zt, sinks, dot_, *, name):
    T = zt.shape[1]
    tq = _tile(T, TQ_ATTN, ATTN_BLOCK)
    nb = tq // ATTN_BLOCK
    nt = T // tq
    tile_of = lambda t: nt - 1 - t

    def body(q_ref, k_ref, v_ref, kh_ref, vh_ref, do_ref, s_ref, dz_ref, dsink_ref, kf, vf, dkf, dvf, carry, pt, dst):
        t = pl.program_id(0)

        @pl.when(t == 0)
        def _():
            carry[...] = jnp.zeros_like(carry)
            dsink_ref[...] = jnp.zeros_like(dsink_ref)

        kf[:, 0:ATTN_BLOCK] = kh_ref[...]
        kf[:, ATTN_BLOCK:] = k_ref[...]
        vf[:, 0:ATTN_BLOCK] = vh_ref[...]
        vf[:, ATTN_BLOCK:] = v_ref[...]
        dkf[...] = jnp.zeros_like(dkf)
        dvf[...] = jnp.zeros_like(dvf)
        dkf[:, tq:] = carry[0:KV_WIDTH, :]
        dvf[:, tq:] = carry[KV_WIDTH:, :]
        lane = lax.broadcasted_iota(jnp.int32, (1, LANES), 1)
        dsink = jnp.zeros((1, LANES), F32)
        for b in range(nb):
            cols = slice(b * ATTN_BLOCK, (b + 1) * ATTN_BLOCK)
            win = slice(b * ATTN_BLOCK, (b + 2) * ATTN_BLOCK)
            valid = _attn_mask_t((t == nt - 1) if b == 0 else False)
            for g in range(N_KV_HEADS):
                gr = slice(g * HEAD_DIM, (g + 1) * HEAD_DIM)
                kt2, vt2 = kf[gr, win], vf[gr, win]
                qst = _group_stack(q_ref, g, cols)
                dost = _group_stack(do_ref, g, cols)
                st = _dot_tn(kt2, qst) * ATTN_SCALE
                dpt = _dot_tn(vt2, dost)
                for r in range(Q_PER_KV):
                    h, _ = _head_rows(g, r)
                    sl = slice(r * ATTN_BLOCK, (r + 1) * ATTN_BLOCK)
                    probs, psink = _attn_probs_t(st[:, sl], valid, s_ref[h])
                    dp = dpt[:, sl]
                    delta = jnp.sum(probs * dp, axis=0, keepdims=True)
                    pt[:, sl] = probs.astype(BF16)
                    dst[:, sl] = (probs * (dp - delta)).astype(BF16)
                    dsink = dsink + jnp.where(lane == h, -jnp.sum(psink * delta), 0.0)
                dqt = _dot(kt2, dst[...]) * ATTN_SCALE
                for r in range(Q_PER_KV):
                    dz_ref[_head_rows(g, r)[1], cols] = dqt[:, r * ATTN_BLOCK:(r + 1) * ATTN_BLOCK].astype(BF16)
                dkf[gr, win] += _dot_nt(qst, dst[...]) * ATTN_SCALE
                dvf[gr, win] += _dot_nt(dost, pt[...])
        dz_ref[ATTN_WIDTH:ATTN_WIDTH + KV_WIDTH, :] = dkf[:, ATTN_BLOCK:].astype(BF16)
        dz_ref[ATTN_WIDTH + KV_WIDTH:, :] = dvf[:, ATTN_BLOCK:].astype(BF16)
        carry[0:KV_WIDTH, :] = dkf[:, 0:ATTN_BLOCK]
        carry[KV_WIDTH:, :] = dvf[:, 0:ATTN_BLOCK]
        dsink_ref[...] += dsink

    return pl.pallas_call(
        body, name=name, grid=(nt,),
        in_specs=_attn_specs(tq, tile_of) + [pl.BlockSpec((ATTN_WIDTH, tq), lambda t: (0, tile_of(t))),
                                             pl.BlockSpec(memory_space=pltpu.SMEM)],
        out_specs=[pl.BlockSpec((QKV_WIDTH, tq), lambda t: (0, tile_of(t))),
                   pl.BlockSpec((8, LANES), lambda t: (0, 0))],
        out_shape=[SDS((QKV_WIDTH, T), BF16), SDS((8, LANES), F32)],
        scratch_shapes=[pltpu.VMEM((KV_WIDTH, tq + ATTN_BLOCK), BF16)] * 2
        + [pltpu.VMEM((KV_WIDTH, tq + ATTN_BLOCK), F32)] * 2 + [pltpu.VMEM((2 * KV_WIDTH, ATTN_BLOCK), F32)]
        + [pltpu.VMEM((2 * ATTN_BLOCK, GROUP_LANES), BF16)] * 2,
        compiler_params=_params("arbitrary"),
    )(zt, zt, zt, zt, zt, dot_, sinks)


def _layer_norm_stats(v):
    mu = jnp.mean(v, axis=-1, keepdims=True)
    xc = v - mu
    rstd = lax.rsqrt(jnp.mean(xc * xc, axis=-1, keepdims=True) + LN_EPS)
    return rstd, xc * rstd


def _sgu_fwd(zmain, ln_g, ln_b, wm, bias, *, name):
    T = zmain.shape[0]
    ts = _tile(T, TS_SGU, SGU_CHUNK)

    def body(u_ref, v_ref, g_ref, b_ref, w_ref, bias_ref, y_ref):
        u = _gelu(u_ref[...].astype(F32))
        _, vh = _layer_norm_stats(_gelu(v_ref[...].astype(F32)))
        vn = (vh * g_ref[...] + b_ref[...]).astype(BF16)
        for ch in range(ts // SGU_CHUNK):
            rows = slice(ch * SGU_CHUNK, (ch + 1) * SGU_CHUNK)
            for g in range(SGU_GROUPS):
                cols = slice(g * 128, (g + 1) * 128)
                s = _dot(w_ref[g], vn[rows, cols]) + bias_ref[g]
                y_ref[rows, cols] = (u[rows, cols] * s).astype(BF16)

    full = _resident
    return pl.pallas_call(
        body, name=name, grid=(T // ts,),
        in_specs=[pl.BlockSpec((ts, SGU_WIDTH), lambda i: (i, 0)), pl.BlockSpec((ts, SGU_WIDTH), lambda i: (i, 1)),
                  full((1, SGU_WIDTH)), full((1, SGU_WIDTH)), full(wm.shape), full(bias.shape)],
        out_specs=pl.BlockSpec((ts, SGU_WIDTH), lambda i: (i, 0)),
        out_shape=SDS((T, SGU_WIDTH), BF16),
        compiler_params=_params("parallel"),
    )(zmain, zmain, ln_g, ln_b, wm, bias)


def _sgu_bwd(zmain, dzmain, dy, ln_g, ln_b, wm, wmt, bias, *, name):
    T = zmain.shape[0]
    ts = _tile(T, TS_SGU, SGU_CHUNK)

    def body(u_ref, v_ref, dy_ref, g_ref, b_ref, w_ref, wt_ref, bias_ref, _, dz_ref, dw_ref, db_ref, dlg_ref, dlb_ref,
             dvn):
        @pl.when(pl.program_id(0) == 0)
        def _():
            dw_ref[...] = jnp.zeros_like(dw_ref)
            db_ref[...] = jnp.zeros_like(db_ref)
            dlg_ref[...] = jnp.zeros_like(dlg_ref)
            dlb_ref[...] = jnp.zeros_like(dlb_ref)

        us = u_ref[...].astype(F32)
        vs = v_ref[...].astype(F32)
        u = _gelu(us)
        rstd, vh = _layer_norm_stats(_gelu(vs))
        vn = (vh * g_ref[...] + b_ref[...]).astype(BF16)
        causal = (lax.broadcasted_iota(jnp.int32, (SGU_CHUNK, SGU_CHUNK), 0)
                  >= lax.broadcasted_iota(jnp.int32, (SGU_CHUNK, SGU_CHUNK), 1))
        lane = lax.broadcasted_iota(jnp.int32, (SGU_CHUNK, LANES), 1)
        db = jnp.zeros((SGU_CHUNK, LANES), F32)
        for ch in range(ts // SGU_CHUNK):
            rows = slice(ch * SGU_CHUNK, (ch + 1) * SGU_CHUNK)
            for g in range(SGU_GROUPS):
                cols = slice(g * 128, (g + 1) * 128)
                vng = vn[rows, cols]
                s = _dot(w_ref[g], vng) + bias_ref[g]
                dyf = dy_ref[rows, cols].astype(F32)
                dz_ref[rows, cols] = (dyf * s * _gelu_grad(us[rows, cols])).astype(BF16)
                dsf = dyf * u[rows, cols]
                dsb = dsf.astype(BF16)
                dvn[rows, cols] = _dot(wt_ref[g], dsb)
                dw_ref[g] += jnp.where(causal, _dot_nt(dsb, vng), 0.0)
                db = db + jnp.where(lane == g, jnp.sum(dsf, axis=1, keepdims=True), 0.0)
        db_ref[...] += db
        dvnf = dvn[...]
        dlg_ref[...] += jnp.sum(dvnf * vh, axis=0, keepdims=True)
        dlb_ref[...] += jnp.sum(dvnf, axis=0, keepdims=True)
        dvh = dvnf * g_ref[...]
        dv = rstd * (dvh - jnp.mean(dvh, axis=-1, keepdims=True) - vh * jnp.mean(dvh * vh, axis=-1, keepdims=True))
        dz_ref[:, SGU_WIDTH:] = (dv * _gelu_grad(vs)).astype(BF16)

    full = _resident
    vec = full((1, SGU_WIDTH))
    acc = lambda shape: pl.BlockSpec(shape, lambda i: (0,) * len(shape))
    return pl.pallas_call(
        body, name=name, grid=(T // ts,),
        in_specs=[pl.BlockSpec((ts, SGU_WIDTH), lambda i: (i, 0)), pl.BlockSpec((ts, SGU_WIDTH), lambda i: (i, 1)),
                  pl.BlockSpec((ts, SGU_WIDTH), lambda i: (i, 0)), vec, vec, full(wm.shape), full(wm.shape),
                  full(bias.shape), pl.BlockSpec(memory_space=pl.ANY)],
        out_specs=[pl.BlockSpec((ts, 2 * SGU_WIDTH), lambda i: (i, 0)), acc(wm.shape),
                   acc((SGU_CHUNK, LANES)), acc((1, SGU_WIDTH)), acc((1, SGU_WIDTH))],
        out_shape=[SDS(dzmain.shape, BF16), SDS(wm.shape, F32), SDS((SGU_CHUNK, LANES), F32),
                   SDS((1, SGU_WIDTH), F32), SDS((1, SGU_WIDTH), F32)],
        scratch_shapes=[pltpu.VMEM((ts, SGU_WIDTH), F32)],
        input_output_aliases={8: 0},
        compiler_params=_params("arbitrary"),
    )(zmain, zmain, dy, ln_g, ln_b, wm, wmt, bias, dzmain)


def _merge_fwd(y_attn_t, y_sgu, zmain, w_a, w_s, w_o, x, g_post, *, name):
    T, D = x.shape
    tm = _tile(T, TM_ROW, LANES)

    def body(ya_ref, ys_ref, ga_ref, gb_ref, wa_ref, ws_ref, wo_ref, x_ref, g_ref, xn_ref, pa_ref, ps_ref, o_ref):
        pa = _dot_tn(ya_ref[...], wa_ref[...])
        ps = _dot(ys_ref[...], ws_ref[...])
        pa_ref[...] = pa.astype(BF16)
        ps_ref[...] = ps.astype(BF16)
        merged = _sigmoid(ga_ref[...].astype(F32)) * pa + _sigmoid(gb_ref[...].astype(F32)) * ps
        out = _dot(merged.astype(BF16), wo_ref[...])
        o_ref[...] = out.astype(BF16)
        _, oh = _rms_stats(out)
        xn_ref[...] = x_ref[...] + oh * g_ref[...]

    row = lambda col: pl.BlockSpec((tm, D), lambda i: (i, col))
    wfull = _resident((D, D))
    return pl.pallas_call(
        body, name=name, grid=(T // tm,),
        in_specs=[pl.BlockSpec((D, tm), lambda i: (0, i)), row(0), row(2), row(3), wfull, wfull, wfull, row(0),
                  pl.BlockSpec((1, D), lambda i: (0, 0))],
        out_specs=[row(0)] * 4,
        out_shape=[SDS((T, D), F32), SDS((T, D), BF16), SDS((T, D), BF16), SDS((T, D), BF16)],
        compiler_params=_params("parallel"),
    )(y_attn_t, y_sgu, zmain, zmain, w_a, w_s, w_o, x, g_post)


def _merge_bwd(dy, out, g_post, pa, ps, zmain, w_a, w_s, w_o, *, name):
    T, D = dy.shape
    tm = _tile(T, TM_ROW, LANES)

    def body(dy_ref, o_ref, g_ref, pa_ref, ps_ref, ga_ref, gb_ref, wa_ref, ws_ref, wo_ref,
             dz_ref, dout_ref, mg_ref, dpa_ref, dps_ref, dya_ref, dys_ref, dg_ref):
        @pl.when(pl.program_id(0) == 0)
        def _():
            dg_ref[...] = jnp.zeros_like(dg_ref)

        dout, dg = _rms_bwd(o_ref[...].astype(F32), g_ref[...], dy_ref[...])
        dg_ref[...] += dg
        doutb = dout.astype(BF16)
        dout_ref[...] = doutb
        dm = _dot_nt(doutb, wo_ref[...])
        pa = pa_ref[...].astype(F32)
        ps = ps_ref[...].astype(F32)
        sa = _sigmoid(ga_ref[...].astype(F32))
        sb = _sigmoid(gb_ref[...].astype(F32))
        mg_ref[...] = (sa * pa + sb * ps).astype(BF16)
        dpa = (dm * sa).astype(BF16)
        dps = (dm * sb).astype(BF16)
        dpa_ref[...] = dpa
        dps_ref[...] = dps
        dz_ref[:, 0:D] = (dm * pa * sa * (1.0 - sa)).astype(BF16)
        dz_ref[:, D:] = (dm * ps * sb * (1.0 - sb)).astype(BF16)
        dya_ref[...] = _dot_nt(wa_ref[...], dpa).astype(BF16)
        dys_ref[...] = _dot_nt(dps, ws_ref[...]).astype(BF16)

    row = lambda col: pl.BlockSpec((tm, D), lambda i: (i, col))
    wfull = _resident((D, D))
    vec = pl.BlockSpec((1, D), lambda i: (0, 0))
    act = SDS((T, D), BF16)
    return pl.pallas_call(
        body, name=name, grid=(T // tm,),
        in_specs=[row(0), row(0), vec, row(0), row(0), row(2), row(3), wfull, wfull, wfull],
        out_specs=[pl.BlockSpec((tm, 2 * D), lambda i: (i, 1))] + [row(0)] * 4
        + [pl.BlockSpec((D, tm), lambda i: (0, i)), row(0), vec],
        out_shape=[SDS(zmain.shape, BF16)] + [act] * 4 + [SDS((D, T), BF16), act, SDS((1, D), F32)],
        compiler_params=_params("arbitrary"),
    )(dy, out, g_post, pa, ps, zmain, zmain, w_a, w_s, w_o)


def _loss_head(y, target, *, name):
    T, D = y.shape
    tm = _tile(T, TM_ROW, 8)

    def body(y_ref, t_ref, dy_ref, l_ref):
        @pl.when(pl.program_id(0) == 0)
        def _():
            l_ref[...] = jnp.zeros_like(l_ref)

        e = y_ref[...] - t_ref[...]
        dy_ref[...] = e * (1.0 / D)
        l_ref[...] += jnp.sum(jnp.mean(e * e, axis=-1, keepdims=True))

    row = pl.BlockSpec((tm, D), lambda i: (i, 0))
    return pl.pallas_call(
        body, name=name, grid=(T // tm,), in_specs=[row, row],
        out_specs=[row, pl.BlockSpec((8, LANES), lambda i: (0, 0))],
        out_shape=[SDS((T, D), F32), SDS((8, LANES), F32)],
        compiler_params=_params("arbitrary"),
    )(y, target)


def _adamw(w, g, m, v, *, name):
    shape = w.shape
    cols = shape[-1]
    rows = w.size // cols
    w2, g2, m2, v2 = (t.reshape(rows, cols) for t in (w, g, m, v))
    tr = _tile(rows, max(8, (256 * 1024) // cols // 8 * 8), 8)

    def body(w_ref, g_ref, m_ref, v_ref, d_ref, nm_ref, nv_ref):
        gg = g_ref[...]
        nm = ADAM_B1 * m_ref[...] + (1.0 - ADAM_B1) * gg
        nv = ADAM_B2 * v_ref[...] + (1.0 - ADAM_B2) * (gg * gg)
        m_hat = nm / (1.0 - ADAM_B1 ** ADAM_STEP)
        v_hat = nv / (1.0 - ADAM_B2 ** ADAM_STEP)
        d_ref[...] = -ADAM_LR * (m_hat / (jnp.sqrt(v_hat) + ADAM_EPS) + ADAM_WD * w_ref[...])
        nm_ref[...] = nm
        nv_ref[...] = nv

    blk = pl.BlockSpec((tr, cols), lambda i: (i, 0))
    outs = pl.pallas_call(
        body, name=name, grid=(rows // tr,), in_specs=[blk] * 4, out_specs=[blk] * 3,
        out_shape=[SDS((rows, cols), F32)] * 3, compiler_params=_params("parallel"),
    )(w2, g2, m2, v2)
    return tuple(o.reshape(shape) for o in outs)


def _sum_terms(terms, n_rows, n_lead, dtypes, *, name):
    cols = terms[0][0].shape[-1]
    tr = _tile(n_rows, 704 if len(terms) <= 4 else 256, SUBLANES_BF16)
    nblk = n_rows // tr
    n_out = len(dtypes)

    def body(*refs):
        acc = refs[0][...].astype(F32)
        for r in refs[1:-n_out]:
            acc = acc + r[...].astype(F32)
        for o_ref in refs[-n_out:]:
            o_ref[...] = acc.astype(o_ref.dtype)

    def spec(lead, first):
        return pl.BlockSpec((1, tr, cols), lambda a, i: (a if lead is None else lead(), first() * nblk + i, 0))

    out = pl.BlockSpec((1, tr, cols), lambda a, i: (a, i, 0))
    return pl.pallas_call(
        body, name=name, grid=(n_lead, nblk), in_specs=[spec(lead, first) for _, lead, first in terms],
        out_specs=[out] * n_out, out_shape=[SDS((n_lead, n_rows, cols), d) for d in dtypes],
        compiler_params=_params("arbitrary", "arbitrary"),
    )(*[a for a, _, _ in terms])


def _position():
    x, y, c = (lax.axis_index(a) for a in AXES)
    chips = [(1 - x, y), (x, 1 - y), (1 - x, 1 - y)]
    return x, y, c, chips


ANY = pl.BlockSpec(memory_space=pl.ANY)


def _remote(src, dst, send_sems, recv_sems, k, to):
    return pltpu.make_async_remote_copy(src_ref=src, dst_ref=dst, send_sem=send_sems.at[k], recv_sem=recv_sems.at[k],
                                        device_id=to, device_id_type=MESH)


def _comm_call(body, arrays, out_shapes, n_sems, *, name):
    n = len(arrays)

    def wrapped(*refs):
        body(refs[:n], refs[n:n + len(out_shapes)], refs[-2], refs[-1])

    return pl.pallas_call(
        wrapped, name=name, in_specs=[ANY] * n, out_specs=[ANY] * len(out_shapes), out_shape=out_shapes,
        scratch_shapes=[pltpu.SemaphoreType.DMA((n_sems,)), pltpu.SemaphoreType.DMA((n_sems,))],
    )(*arrays)


def _gather_shards(packs, *, name):
    def body(p_refs, o_refs, send_sems, recv_sems):
        x, y, c, chips = _position()
        sent, passed = [], []
        for a, (p_ref, o_ref) in enumerate(zip(p_refs, o_refs)):
            mine = pl.ds(c * (p_ref.shape[0] // 2), p_ref.shape[0] // 2)
            sent += [_remote(p_ref.at[mine], o_ref.at[2 * x + y, mine], send_sems, recv_sems, 6 * a + j, (*chip, c))
                     for j, chip in enumerate(chips)]
        for cp in sent:
            cp.start()
        for a, o_ref in enumerate(o_refs):
            mine = pl.ds(c * (o_ref.shape[1] // 2), o_ref.shape[1] // 2)
            for j, (cx, cy) in enumerate(chips):
                landed = o_ref.at[2 * cx + cy, mine]
                _remote(landed, landed, send_sems, recv_sems, 6 * a + j, (x, y, c)).wait_recv()
                cp = _remote(landed, landed, send_sems, recv_sems, 6 * a + 3 + j, (x, y, 1 - c))
                cp.start()
                passed.append(cp)
        for a, o_ref in enumerate(o_refs):
            other = pl.ds((1 - c) * (o_ref.shape[1] // 2), o_ref.shape[1] // 2)
            for j, (cx, cy) in enumerate(chips):
                theirs = o_ref.at[2 * cx + cy, other]
                _remote(theirs, theirs, send_sems, recv_sems, 6 * a + 3 + j, (x, y, c)).wait_recv()
        for cp in sent + passed:
            cp.wait_send()

    return _comm_call(body, packs, [SDS((N_CHIPS,) + p.shape, p.dtype) for p in packs], 6 * len(packs), name=name)


def _sibling_exchange(gs, *, name):
    def body(g_refs, o_refs, send_sems, recv_sems):
        x, y, c, _ = _position()
        sent = [_remote(g_ref.at[:, pl.ds((1 - c) * o_ref.shape[1], o_ref.shape[1])], o_ref, send_sems, recv_sems, a,
                        (x, y, 1 - c)) for a, (g_ref, o_ref) in enumerate(zip(g_refs, o_refs))]
        for cp in sent:
            cp.start()
        for cp in sent:
            cp.wait()

    return _comm_call(body, gs, [SDS((N_CHIPS, g.shape[1] // 2, g.shape[2]), g.dtype) for g in gs], len(gs), name=name)


def _chip_scatter(ps, *, name):
    def body(p_refs, o_refs, send_sems, recv_sems):
        x, y, c, chips = _position()
        sent = [_remote(p_ref.at[2 * cx + cy], o_ref.at[j], send_sems, recv_sems, 3 * a + j, (cx, cy, c))
                for a, (p_ref, o_ref) in enumerate(zip(p_refs, o_refs)) for j, (cx, cy) in enumerate(chips)]
        for cp in sent:
            cp.start()
        for cp in sent:
            cp.wait()

    return _comm_call(body, ps, [SDS((N_CHIPS - 1,) + p.shape[1:], p.dtype) for p in ps], 3 * len(ps), name=name)


def _sibling_swap(ts, *, name):
    def body(t_refs, o_refs, send_sems, recv_sems):
        x, y, c, _ = _position()
        sent = [_remote(t_ref, o_ref, send_sems, recv_sems, a, (x, y, 1 - c))
                for a, (t_ref, o_ref) in enumerate(zip(t_refs, o_refs))]
        for cp in sent:
            cp.start()
        for cp in sent:
            cp.wait()

    return _comm_call(body, ts, [SDS(t.shape, t.dtype) for t in ts], len(ts), name=name)


def _gather_all(v, *, name):
    M, C = v.shape

    def body(v_ref, o_ref, send_sems, recv_sems):
        x, y, c, chips = _position()
        slot = lambda px, py, pc: o_ref.at[4 * px + 2 * py + pc]
        first = [_remote(v_ref, slot(x, y, c), send_sems, recv_sems, 0, (x, y, 1 - c))]
        first += [_remote(v_ref, slot(x, y, c), send_sems, recv_sems, 1 + j, (*chip, c)) for j, chip in enumerate(chips)]
        for cp in first:
            cp.start()
        passed = []
        for j, chip in enumerate(chips):
            landed = slot(*chip, c)
            _remote(landed, landed, send_sems, recv_sems, 1 + j, (x, y, c)).wait_recv()
            cp = _remote(landed, landed, send_sems, recv_sems, 4 + j, (x, y, 1 - c))
            cp.start()
            passed.append(cp)
        sib = slot(x, y, 1 - c)
        _remote(sib, sib, send_sems, recv_sems, 0, (x, y, c)).wait_recv()
        for j, chip in enumerate(chips):
            theirs = slot(*chip, 1 - c)
            _remote(theirs, theirs, send_sems, recv_sems, 4 + j, (x, y, c)).wait_recv()
        for cp in first + passed:
            cp.wait_send()

    return pl.pallas_call(
        body, name=name, in_specs=[ANY], out_specs=ANY, out_shape=SDS((N_DEV, M, C), v.dtype),
        scratch_shapes=[pltpu.SemaphoreType.DMA((7,)), pltpu.SemaphoreType.DMA((7,))],
    )(v)


BIG = ("ffn1_w1", "ffn1_w2", "w_in", "w_attn_branch", "w_sgu_branch", "w_out", "ffn2_w1", "ffn2_w2")
COL_SHARDED = ("ffn1_w1", "w_in", "ffn2_w1")
FFN_IN = ("ffn1_w1", "ffn2_w1")
SMALL = ("ffn1_pre_g", "ffn1_post_g", "mix_pre_g", "attn_sinks", "sgu_ln_g", "sgu_ln_b", "sgu_w", "sgu_b",
         "mix_post_g", "ffn2_pre_g", "ffn2_post_g")
WEIGHTS = ("ffn1_pre_g", "ffn1_w1", "ffn1_w2", "ffn1_post_g", "mix_pre_g", "w_in", "attn_sinks", "sgu_ln_g",
           "sgu_ln_b", "sgu_w", "sgu_b", "w_attn_branch", "w_sgu_branch", "w_out", "mix_post_g", "ffn2_pre_g",
           "ffn2_w1", "ffn2_w2", "ffn2_post_g")


def _column_chunks(w, tn):
    return jnp.swapaxes(w.reshape(w.shape[0], w.shape[1] // tn, tn), 0, 1)


def _width_classes(shard_shapes):
    widths = sorted({shard_shapes[n][-1] for n in BIG}, reverse=True)
    return [[n for n in BIG if shard_shapes[n][-1] == w] for w in widths]


def _pack_class(names, n_layers, shard_of, dtype):
    return jnp.concatenate([shard_of(n, l).astype(dtype) for l in range(n_layers) for n in names], axis=0)


def _unpack_class(rows, names, n_layers, shard_shapes):
    out, r0 = {}, 0
    for l in range(n_layers):
        for n in names:
            out[n, l] = rows[r0:r0 + shard_shapes[n][0]]
            r0 += shard_shapes[n][0]
    return out


def _ffn_fwd(x, pre_g, w1, w2, post_g, tag):
    a, h = _norm_matmul(x, pre_g, w1, name=f"{tag}_up", with_h=True)
    xn, o = _swiglu_out(a, w2, x, post_g, name=f"{tag}_down")
    return xn, (x, h, a, o)


def _ffn_bwd(dy, saved, pre_g, w1, w2, post_g, tag):
    x, h, a, o = saved
    da, s, do, d_post = _ffn_bwd_hidden(dy, o, post_g, a, w2, name=f"{tag}_bwd_hidden")
    dw2 = _matmul_tn(s, do, name=f"{tag}_dw2")
    dx, d_pre = _matmul_nt_norm_bwd(da, w1, x, pre_g, dy, None, name=f"{tag}_bwd_in")
    dw1 = _matmul_tn(h, da, name=f"{tag}_dw1")
    return dx, dw1, dw2, d_pre, d_post


def kernel(x, ffn1_pre_g, ffn1_w1, ffn1_w2, ffn1_post_g, mix_pre_g, w_in, attn_sinks, sgu_ln_g, sgu_ln_b, sgu_w, sgu_b, w_attn_branch, w_sgu_branch, w_out, mix_post_g, ffn2_pre_g, ffn2_w1, ffn2_w2, ffn2_post_g, loss_target, m_ffn1_pre_g, m_ffn1_w1, m_ffn1_w2, m_ffn1_post_g, m_mix_pre_g, m_w_in, m_attn_sinks, m_sgu_ln_g, m_sgu_ln_b, m_sgu_w, m_sgu_b, m_w_attn_branch, m_w_sgu_branch, m_w_out, m_mix_post_g, m_ffn2_pre_g, m_ffn2_w1, m_ffn2_w2, m_ffn2_post_g, v_ffn1_pre_g, v_ffn1_w1, v_ffn1_w2, v_ffn1_post_g, v_mix_pre_g, v_w_in, v_attn_sinks, v_sgu_ln_g, v_sgu_ln_b, v_sgu_w, v_sgu_b, v_w_attn_branch, v_w_sgu_branch, v_w_out, v_mix_post_g, v_ffn2_pre_g, v_ffn2_w1, v_ffn2_w2, v_ffn2_post_g):
    given = dict(locals())
    W = {n: given[n] for n in WEIGHTS}
    M = {n: given["m_" + n] for n in WEIGHTS}
    V = {n: given["v_" + n] for n in WEIGHTS}
    L = ffn1_w1.shape[0]
    T, D = x.shape[1], x.shape[2]
    xt = x.reshape(T, D)
    target = loss_target.reshape(T, D)
    assert L % 2 == 0 and D == ATTN_WIDTH == SGU_WIDTH and T % ATTN_BLOCK == 0

    shard_shapes = {n: W[n].shape[1:] for n in BIG}
    classes = _width_classes(shard_shapes)
    my_chip = 2 * lax.axis_index("x") + lax.axis_index("y")
    my_core = lax.axis_index("c")
    packs = [_pack_class(names, L, lambda n, l: W[n][l], BF16) for names in classes]
    gathered = _gather_shards(packs, name="gather_weights")
    full = [{} for _ in range(L)]
    for names, pack, got in zip(classes, packs, gathered):
        per_chip = [_unpack_class(jnp.where(my_chip == s, pack, got[s]), names, L, shard_shapes)
                    for s in range(N_CHIPS)]
        for l in range(L):
            for n in names:
                if n in FFN_IN:
                    full[l][n] = jnp.stack([pc[n, l] for pc in per_chip])
                else:
                    full[l][n] = jnp.concatenate([pc[n, l] for pc in per_chip], axis=1 if n in COL_SHARDED else 0)

    row = lambda name, l: W[name][l].reshape(1, -1)
    causal = jnp.tril(jnp.ones((SGU_CHUNK, SGU_CHUNK), dtype=bool))
    saved = []
    h_cur = xt
    for l in range(L):
        fw = full[l]
        sv = {}
        h_cur, sv["ffn1"] = _ffn_fwd(h_cur, row("ffn1_pre_g", l), fw["ffn1_w1"], fw["ffn1_w2"], row("ffn1_post_g", l),
                                     f"l{l}_ffn1")
        w_qkv, w_main = fw["w_in"][:, :QKV_WIDTH], _column_chunks(fw["w_in"][:, QKV_WIDTH:], D)
        zqkv, hm = _norm_matmul_t(h_cur, row("mix_pre_g", l), w_qkv.T, name=f"l{l}_mix_in_qkv")
        zmain, = _norm_matmul(h_cur, row("mix_pre_g", l), w_main, name=f"l{l}_mix_in_main", with_h=False)
        wm = jnp.where(causal[None], sgu_w[l], 0.0).astype(BF16)
        wmt = jnp.swapaxes(wm, 1, 2)
        bias = jnp.broadcast_to(sgu_b[l][:, :, None], (SGU_GROUPS, SGU_CHUNK, 128)).astype(F32)
        y_attn = _attn_fwd(zqkv, attn_sinks[l], name=f"l{l}_attn")
        y_sgu = _sgu_fwd(zmain, row("sgu_ln_g", l), row("sgu_ln_b", l), wm, bias, name=f"l{l}_sgu")
        x_mix = h_cur
        h_cur, pa, ps, mo = _merge_fwd(y_attn, y_sgu, zmain, fw["w_attn_branch"], fw["w_sgu_branch"], fw["w_out"],
                                       x_mix, row("mix_post_g", l), name=f"l{l}_merge")
        sv["mix"] = (x_mix, hm, zqkv, zmain, y_attn, y_sgu, pa, ps, mo, wm, wmt, bias)
        h_cur, sv["ffn2"] = _ffn_fwd(h_cur, row("ffn2_pre_g", l), fw["ffn2_w1"], fw["ffn2_w2"], row("ffn2_post_g", l),
                                     f"l{l}_ffn2")
        saved.append(sv)

    dy, lsum = _loss_head(h_cur, target, name="loss_head")
    loss = lax.psum(0.5 * lsum[0, 0], AXES)

    big_grads = [None] * L
    small_grads = [None] * L
    for l in reversed(range(L)):
        fw, sv = full[l], saved[l]
        gb, gs = {}, {}
        dy, gb["ffn2_w1"], gb["ffn2_w2"], gs["ffn2_pre_g"], gs["ffn2_post_g"] = _ffn_bwd(
            dy, sv["ffn2"], row("ffn2_pre_g", l), fw["ffn2_w1"], fw["ffn2_w2"], row("ffn2_post_g", l), f"l{l}_ffn2")

        x_mix, hm, zqkv, zmain, y_attn, y_sgu, pa, ps, mo, wm, wmt, bias = sv["mix"]
        dzmain, dout, merged, dpa, dps, dya, dys, gs["mix_post_g"] = _merge_bwd(
            dy, mo, row("mix_post_g", l), pa, ps, zmain, fw["w_attn_branch"], fw["w_sgu_branch"], fw["w_out"],
            name=f"l{l}_merge_bwd")
        gb["w_out"] = _matmul_tn(merged, dout, name=f"l{l}_dw_out")
        gb["w_attn_branch"] = _matmul_tokens(y_attn, dpa, name=f"l{l}_dw_attn")
        gb["w_sgu_branch"] = _matmul_tn(y_sgu, dps, name=f"l{l}_dw_sgu")
        dzqkv, dsink = _attn_bwd(zqkv, attn_sinks[l], dya, name=f"l{l}_attn_bwd")
        gs["attn_sinks"] = dsink[0, :N_Q_HEADS]
        dzmain, dsw, dsb, gs["sgu_ln_g"], gs["sgu_ln_b"] = _sgu_bwd(
            zmain, dzmain, dys, row("sgu_ln_g", l), row("sgu_ln_b", l), wm, wmt, bias, name=f"l{l}_sgu_bwd")
        gs["sgu_w"] = dsw
        gs["sgu_b"] = dsb[:, :SGU_GROUPS].T
        w_qkv, w_main = fw["w_in"][:, :QKV_WIDTH], _column_chunks(fw["w_in"][:, QKV_WIDTH:], D)
        dh_qkv =_matmul_tn_rows(dzqkv, w_qkv.T, name=f"l{l}_mix_bwd_qkv")
        dy, gs["mix_pre_g"] = _matmul_nt_norm_bwd(dzmain, w_main, x_mix, row("mix_pre_g", l), dy, dh_qkv,
                                                   name=f"l{l}_mix_bwd_in")
        gb["w_in"] = jnp.concatenate([_matmul_tokens(dzqkv, hm, name=f"l{l}_dw_in_qkv").T,
                                      _matmul_tn(hm, dzmain, name=f"l{l}_dw_in_main")], axis=1)

        dy, gb["ffn1_w1"], gb["ffn1_w2"], gs["ffn1_pre_g"], gs["ffn1_post_g"] = _ffn_bwd(
            dy, sv["ffn1"], row("ffn1_pre_g", l), fw["ffn1_w1"], fw["ffn1_w2"], row("ffn1_post_g", l), f"l{l}_ffn1")
        big_grads[l], small_grads[l] = gb, gs
    grad_x = dy.reshape(x.shape)

    def shard_of(n, g, s):
        k = shard_shapes[n][1 if n in COL_SHARDED else 0]
        return g[:, s * k:(s + 1) * k] if n in COL_SHARDED else g[s * k:(s + 1) * k]

    grs = [jnp.stack([_pack_class(names, L, lambda n, l: shard_of(n, big_grads[l][n], s), F32)
                      for s in range(N_CHIPS)]) for names in classes]
    from_sibling = _sibling_exchange(grs, name="grads_sibling_exchange")
    zero = lambda: 0
    core_in_map = lambda: lax.axis_index("c")
    chip_in_map = lambda: 2 * lax.axis_index("x") + lax.axis_index("y")
    pairs = [_sum_terms([(g, None, core_in_map), (fs, None, zero)], fs.shape[1], N_CHIPS, (F32, BF16),
                        name=f"grads_pair_sum{k}") for k, (g, fs) in enumerate(zip(grs, from_sibling))]
    from_chips = _chip_scatter([p[1] for p in pairs], name="grads_chip_scatter")
    halves = [_sum_terms([(p[0], chip_in_map, zero)] + [(fc, (lambda j=j: j), zero) for j in range(N_CHIPS - 1)],
                         fc.shape[1], 1, (F32,), name=f"grads_chip_sum{k}")[0][0]
              for k, (p, fc) in enumerate(zip(pairs, from_chips))]
    others = _sibling_swap(halves, name="grads_sibling_swap")

    grads = {n: [None] * L for n in WEIGHTS}
    for names, half, other in zip(classes, halves, others):
        reduced = jnp.concatenate([jnp.where(my_core == 0, half, other), jnp.where(my_core == 0, other, half)], axis=0)
        for (n, l), g in _unpack_class(reduced, names, L, shard_shapes).items():
            grads[n][l] = g

    def small_rows(gs):
        parts = []
        for n in SMALL:
            flat = gs[n].reshape(-1)
            pad = (-flat.shape[0]) % D
            parts.append(jnp.pad(flat, (0, pad)).reshape(-1, D))
        return jnp.concatenate(parts, axis=0)

    spack = jnp.concatenate([small_rows(small_grads[l]) for l in range(L)], axis=0)
    n_small = spack.shape[0]
    pad_rows = (-n_small) % SUBLANES_BF16
    spack = jnp.pad(spack, ((0, pad_rows), (0, 0)))
    everyone = _gather_all(spack, name="small_grads_gather")
    is_me = (jnp.arange(N_DEV) == 2 * my_chip + my_core)[:, None, None]
    everyone = jnp.where(is_me, spack[None], everyone)
    ssum = _sum_terms([(everyone, (lambda d=d: d), zero) for d in range(N_DEV)], spack.shape[0], 1, (F32,),
                      name="small_grads_sum")[0][0]
    per_layer = n_small // L
    for l in range(L):
        r0 = l * per_layer
        for n in SMALL:
            shp = W[n].shape[1:]
            size = math.prod(shp)
            nr = -(-size // D)
            grads[n][l] = ssum[r0:r0 + nr].reshape(-1)[:size].reshape(shp)
            r0 += nr
    grads = {n: jnp.stack(grads[n]) for n in WEIGHTS}

    delta, new_m, new_v = {}, {}, {}
    for n in WEIGHTS:
        delta[n], new_m[n], new_v[n] = _adamw(W[n], grads[n], M[n], V[n], name=f"adamw_{n}")

    return (loss, grad_x, *[grads[n] for n in WEIGHTS], *[delta[n] for n in WEIGHTS],
            *[new_m[n] for n in WEIGHTS], *[new_v[n] for n in WEIGHTS])
```

```python
import functools
import math

import jax
import jax.numpy as jnp
from jax import lax
from jax.experimental import pallas as pl
from jax.experimental.pallas import tpu as pltpu

F32, BF16 = jnp.float32, jnp.bfloat16
SDS = jax.ShapeDtypeStruct
MESH = pl.DeviceIdType.MESH
AXES = ("x", "y", "c")

HEAD_DIM = 64
N_Q_HEADS = 16
N_KV_HEADS = 2
Q_PER_KV = N_Q_HEADS // N_KV_HEADS
ATTN_WIDTH = N_Q_HEADS * HEAD_DIM
KV_WIDTH = N_KV_HEADS * HEAD_DIM
ATTN_BLOCK = 128
SGU_CHUNK = 128
SGU_GROUPS = 8
SGU_WIDTH = SGU_GROUPS * 128
QKV_WIDTH = ATTN_WIDTH + 2 * KV_WIDTH
RMS_EPS = 1e-6
LN_EPS = 1e-5
MASK_VALUE = -1e30
ATTN_SCALE = 1.0 / math.sqrt(HEAD_DIM)

ADAM_LR, ADAM_B1, ADAM_B2, ADAM_EPS, ADAM_WD, ADAM_STEP = 0.001, 0.9, 0.999, 1e-08, 0.01, 10

N_CHIPS = 4
N_DEV = 8

VMEM_LIMIT_BYTES = 56 * 1024 * 1024
LANES = 128
SUBLANES_BF16 = 16

TM_NORM_MATMUL = 1024
TM_ROW = 512
TM_FFN_BWD = 512
TT_REDUCE = 1024
TQ_ATTN = 512
TS_SGU = 512


def _tile(n, pref, mult):
    t = (min(pref, n) // mult) * mult
    while t >= mult:
        if n % t == 0:
            return t
        t -= mult
    return n


def _params(*sem):
    return pltpu.CompilerParams(dimension_semantics=sem, vmem_limit_bytes=VMEM_LIMIT_BYTES)


def _dot(a, b):
    return jnp.dot(a, b, preferred_element_type=F32)


def _dot_nt(a, b):
    return lax.dot_general(a, b, (((1,), (1,)), ((), ())), preferred_element_type=F32)


def _dot_tn(a, b):
    return lax.dot_general(a, b, (((0,), (0,)), ((), ())), preferred_element_type=F32)


def _sigmoid(x):
    return 0.5 * (1.0 + jnp.tanh(0.5 * x))


def _rms_stats(xf):
    r = lax.rsqrt(jnp.mean(xf * xf, axis=-1, keepdims=True) + RMS_EPS)
    return r, xf * r


def _rms_bwd(xf, g, dy):
    r, xh = _rms_stats(xf)
    dyg = dy * g
    dx = r * (dyg - xh * jnp.mean(dyg * xh, axis=-1, keepdims=True))
    return dx, jnp.sum(dy * xh, axis=0, keepdims=True)


def _gelu_parts(x):
    cdf = 0.5 * (1.0 + lax.erf(x * (1.0 / math.sqrt(2.0))))
    return cdf


def _gelu(x):
    return x * _gelu_parts(x)


def _gelu_grad(x):
    return _gelu_parts(x) + x * jnp.exp(-0.5 * x * x) * (1.0 / math.sqrt(2.0 * math.pi))


def _resident(shape, index=None):
    index = (0,) * len(shape) if index is None else index
    return pl.BlockSpec(shape, lambda *_: index, pipeline_mode=pl.Buffered(1))


def _norm_matmul(x, g, w3, w_block, *, name, with_h):
    T, D = x.shape
    nj, _, tn = w3.shape
    tm = _tile(T, TM_NORM_MATMUL, SUBLANES_BF16)

    def body(x_ref, g_ref, w_ref, a_ref, *rest):
        h_sc = rest[-1]

        @pl.when(pl.program_id(1) == 0)
        def _():
            _, xh = _rms_stats(x_ref[...])
            h = (xh * g_ref[...]).astype(BF16)
            h_sc[...] = h
            if with_h:
                rest[0][...] = h

        a_ref[...] = _dot(h_sc[...], w_ref[pl.program_id(1)]).astype(BF16)

    out_specs = [pl.BlockSpec((tm, tn), lambda i, j: (i, j))]
    out_shape = [SDS((T, nj * tn), BF16)]
    if with_h:
        out_specs.append(pl.BlockSpec((tm, D), lambda i, j: (i, 0)))
        out_shape.append(SDS((T, D), BF16))
    return pl.pallas_call(
        body, name=name, grid=(T // tm, nj),
        in_specs=[pl.BlockSpec((tm, D), lambda i, j: (i, 0)),
                  pl.BlockSpec((1, D), lambda i, j: (0, 0)),
                  _resident((nj, D, tn), (0, w_block, 0))],
        out_specs=out_specs, out_shape=out_shape,
        scratch_shapes=[pltpu.VMEM((tm, D), BF16)],
        compiler_params=_params("parallel", "arbitrary"),
    )(x, g, w3)


def _norm_matmul_t(x, g, wt, *, name):
    T, D = x.shape
    N = wt.shape[0]
    tm = _tile(T, TM_ROW, LANES)

    def body(x_ref, g_ref, w_ref, a_ref, h_ref):
        _, xh = _rms_stats(x_ref[...])
        h = (xh * g_ref[...]).astype(BF16)
        h_ref[...] = h
        a_ref[...] = _dot_nt(w_ref[...], h).astype(BF16)

    return pl.pallas_call(
        body, name=name, grid=(T // tm,),
        in_specs=[pl.BlockSpec((tm, D), lambda i: (i, 0)), pl.BlockSpec((1, D), lambda i: (0, 0)),
                  _resident((N, D))],
        out_specs=[pl.BlockSpec((N, tm), lambda i: (0, i)), pl.BlockSpec((tm, D), lambda i: (i, 0))],
        out_shape=[SDS((N, T), BF16), SDS((T, D), BF16)],
        compiler_params=_params("parallel"),
    )(x, g, wt)


def _matmul_tokens(at, b, *, name):
    K, T = at.shape
    N = b.shape[1]
    tt = _tile(T, TT_REDUCE, LANES)

    def body(a_ref, b_ref, o_ref):
        @pl.when(pl.program_id(0) == 0)
        def _():
            o_ref[...] = jnp.zeros_like(o_ref)

        o_ref[...] += _dot(a_ref[...], b_ref[...])

    return pl.pallas_call(
        body, name=name, grid=(T // tt,),
        in_specs=[pl.BlockSpec((K, tt), lambda t: (0, t)), pl.BlockSpec((tt, N), lambda t: (t, 0))],
        out_specs=pl.BlockSpec((K, N), lambda t: (0, 0)),
        out_shape=SDS((K, N), F32),
        compiler_params=_params("arbitrary"),
    )(at, b)


def _matmul_tn_rows(dat, wt, *, name):
    N, T = dat.shape
    D = wt.shape[1]
    tm = _tile(T, TM_ROW, LANES)

    def body(da_ref, w_ref, o_ref):
        o_ref[...] = _dot_tn(da_ref[...], w_ref[...])

    return pl.pallas_call(
        body, name=name, grid=(T // tm,),
        in_specs=[pl.BlockSpec((N, tm), lambda i: (0, i)), _resident((N, D))],
        out_specs=pl.BlockSpec((tm, D), lambda i: (i, 0)),
        out_shape=SDS((T, D), F32),
        compiler_params=_params("parallel"),
    )(dat, wt)


def _ff_chunk(F):
    return F if F <= 1408 else F // 2


def _swiglu_out(a, w2, x, g_post, *, name):
    T, F2 = a.shape
    F = F2 // 2
    D = x.shape[1]
    tm = _tile(T, TM_ROW, SUBLANES_BF16)
    fc = _ff_chunk(F)

    def body(a_ref, w_ref, x_ref, g_ref, xn_ref, o_ref):
        acc = None
        for c0 in range(0, F, fc):
            gt = a_ref[:, c0:c0 + fc].astype(F32)
            ut = a_ref[:, F + c0:F + c0 + fc].astype(F32)
            s = (gt * _sigmoid(gt) * ut).astype(BF16)
            part = _dot(s, w_ref[c0:c0 + fc, :])
            acc = part if acc is None else acc + part
        o_ref[...] = acc.astype(BF16)
        _, oh = _rms_stats(acc)
        xn_ref[...] = x_ref[...] + 0.5 * (oh * g_ref[...])

    return pl.pallas_call(
        body, name=name, grid=(T // tm,),
        in_specs=[pl.BlockSpec((tm, F2), lambda i: (i, 0)),
                  _resident((F, D)),
                  pl.BlockSpec((tm, D), lambda i: (i, 0)),
                  pl.BlockSpec((1, D), lambda i: (0, 0))],
        out_specs=[pl.BlockSpec((tm, D), lambda i: (i, 0)), pl.BlockSpec((tm, D), lambda i: (i, 0))],
        out_shape=[SDS((T, D), F32), SDS((T, D), BF16)],
        compiler_params=_params("parallel"),
    )(a, w2, x, g_post)


def _ffn_bwd_hidden(dy, o, g_post, a, w2, *, name):
    T, F2 = a.shape
    F = F2 // 2
    D = dy.shape[1]
    tm = _tile(T, TM_FFN_BWD, SUBLANES_BF16)
    fc = _tile(F, 256, LANES)

    def body(dy_ref, o_ref, g_ref, a_ref, w_ref, da_ref, s_ref, do_ref, dg_ref):
        @pl.when(pl.program_id(0) == 0)
        def _():
            dg_ref[...] = jnp.zeros_like(dg_ref)

        do, dg = _rms_bwd(o_ref[...].astype(F32), g_ref[...], 0.5 * dy_ref[...])
        dg_ref[...] += dg
        dob = do.astype(BF16)
        do_ref[...] = dob
        for c0 in range(0, F, fc):
            ds = _dot_nt(dob, w_ref[c0:c0 + fc, :])
            gt = a_ref[:, c0:c0 + fc].astype(F32)
            ut = a_ref[:, F + c0:F + c0 + fc].astype(F32)
            sg = _sigmoid(gt)
            sl = gt * sg
            s_ref[:, c0:c0 + fc] = (sl * ut).astype(BF16)
            da_ref[:, c0:c0 + fc] = (ds * ut * (sg * (1.0 + gt * (1.0 - sg)))).astype(BF16)
            da_ref[:, F + c0:F + c0 + fc] = (ds * sl).astype(BF16)

    row = lambda w: pl.BlockSpec((tm, w), lambda i: (i, 0))
    return pl.pallas_call(
        body, name=name, grid=(T // tm,),
        in_specs=[row(D), row(D), pl.BlockSpec((1, D), lambda i: (0, 0)), row(F2),
                  _resident((F, D))],
        out_specs=[row(F2), row(F), row(D), pl.BlockSpec((1, D), lambda i: (0, 0))],
        out_shape=[SDS((T, F2), BF16), SDS((T, F), BF16), SDS((T, D), BF16), SDS((1, D), F32)],
        compiler_params=_params("arbitrary"),
    )(dy, o, g_post, a, w2)


def _dw_call(body, name, grid, in_specs, args, block, pack, row_block, sem):
    if pack is None:
        out_spec = pl.BlockSpec(block, lambda *_: (0, 0, 0), pipeline_mode=pl.Buffered(1))
        return pl.pallas_call(body, name=name, grid=grid, in_specs=in_specs, out_specs=out_spec,
                              out_shape=SDS(block, F32), compiler_params=_params(*sem))(*args)
    assert pack.shape[0] == block[0] and pack.shape[2] == block[2]
    out_spec = pl.BlockSpec(block, lambda *_: (0, row_block, 0), pipeline_mode=pl.Buffered(1))
    return pl.pallas_call(body, name=name, grid=grid, in_specs=in_specs + [ANY], out_specs=out_spec,
                          out_shape=SDS(pack.shape, F32), input_output_aliases={len(args): 0},
                          compiler_params=_params(*sem))(*args, pack)


def _dw_cols(a, b, n_chunks, pack, row_block, *, name):
    T, K = a.shape
    tn = b.shape[1] // n_chunks
    tt = _tile(T, TT_REDUCE, SUBLANES_BF16)

    def body(a_ref, b_ref, *rest):
        o_ref = rest[-1]
        t, j = pl.program_id(0), pl.program_id(1)
        part = _dot_tn(a_ref[...], b_ref[...])

        @pl.when(t == 0)
        def _():
            o_ref[j] = part

        @pl.when(t > 0)
        def _():
            o_ref[j] += part

    return _dw_call(body, name, (T // tt, n_chunks),
                    [pl.BlockSpec((tt, K), lambda t, j: (t, 0)), pl.BlockSpec((tt, tn), lambda t, j: (t, j))],
                    [a, b], (n_chunks, K, tn), pack, row_block, ("arbitrary", "arbitrary"))


def _dw_rows(a, b, pack, row_block, *, name, a_feature_major=False):
    K, T = a.shape if a_feature_major else a.shape[::-1]
    N = b.shape[1]
    r = K // N_CHIPS
    cw = r if r % LANES == 0 else 2 * r
    assert cw % LANES == 0 and K % cw == 0 and r % 8 == 0
    tt = _tile(T, TT_REDUCE, LANES)

    def body(a_ref, b_ref, *rest):
        o_ref = rest[-1]

        @pl.when(pl.program_id(0) == 0)
        def _():
            o_ref[...] = jnp.zeros_like(o_ref)

        for c in range(K // cw):
            if a_feature_major:
                part = _dot(a_ref[c * cw:(c + 1) * cw, :], b_ref[...])
            else:
                part = _dot_tn(a_ref[:, c * cw:(c + 1) * cw], b_ref[...])
            for p in range(cw // r):
                o_ref[c * (cw // r) + p] += part[p * r:(p + 1) * r]

    a_spec = pl.BlockSpec((K, tt), lambda t: (0, t)) if a_feature_major else pl.BlockSpec((tt, K), lambda t: (t, 0))
    return _dw_call(body, name, (T // tt,), [a_spec, pl.BlockSpec((tt, N), lambda t: (t, 0))],
                    [a, b], (N_CHIPS, r, N), pack, row_block, ("arbitrary",))


def _matmul_nt_norm_bwd(da, w, w_block, x, g, dy, init, *, name):
    T, N = da.shape
    D = x.shape[1]
    nj, _, tn = w.shape
    tm = _tile(T, TM_ROW, SUBLANES_BF16)
    has_init = init is not None

    def body(da_ref, w_ref, x_ref, g_ref, dy_ref, *rest):
        dx_ref, dg_ref = rest[-2:]

        @pl.when(pl.program_id(0) == 0)
        def _():
            dg_ref[...] = jnp.zeros_like(dg_ref)

        dh = rest[0][...] if has_init else None
        for j in range(nj):
            part = _dot_nt(da_ref[:, j * tn:(j + 1) * tn], w_ref[j])
            dh = part if dh is None else dh + part
        dx, dg = _rms_bwd(x_ref[...], g_ref[...], dh)
        dx_ref[...] = dy_ref[...] + dx
        dg_ref[...] += dg

    row = pl.BlockSpec((tm, D), lambda i: (i, 0))
    vec = pl.BlockSpec((1, D), lambda i: (0, 0))
    in_specs = [pl.BlockSpec((tm, N), lambda i: (i, 0)), _resident((nj, D, tn), (0, w_block, 0)), row, vec, row]
    args = [da, w, x, g, dy]
    if has_init:
        in_specs.append(row)
        args.append(init)
    return pl.pallas_call(
        body, name=name, grid=(T // tm,), in_specs=in_specs,
        out_specs=[row, vec], out_shape=[SDS((T, D), F32), SDS((1, D), F32)],
        compiler_params=_params("arbitrary"),
    )(*args)


GROUP_LANES = Q_PER_KV * ATTN_BLOCK


def _attn_mask_t(first):
    kj = lax.broadcasted_iota(jnp.int32, (2 * ATTN_BLOCK, ATTN_BLOCK), 0)
    qi = lax.broadcasted_iota(jnp.int32, (2 * ATTN_BLOCK, ATTN_BLOCK), 1)
    rel = qi + ATTN_BLOCK - kj
    band = (rel >= 0) & (rel < ATTN_BLOCK)
    if first is False:
        return band
    return band & ((kj >= ATTN_BLOCK) | jnp.logical_not(first))


def _attn_probs_t(st, valid, sink):
    s = jnp.where(valid, st, MASK_VALUE)
    m = jnp.maximum(jnp.max(s, axis=0, keepdims=True), sink)
    p = jnp.exp(s - m)
    es = jnp.exp(sink - m)
    inv = 1.0 / (jnp.sum(p, axis=0, keepdims=True) + es)
    return p * inv, es * inv


def _attn_specs(tq, tile_of):
    nb = tq // ATTN_BLOCK
    krow, vrow = ATTN_WIDTH // KV_WIDTH, ATTN_WIDTH // KV_WIDTH + 1
    halo = lambda r: pl.BlockSpec((KV_WIDTH, ATTN_BLOCK), lambda t: (r, jnp.maximum(tile_of(t) * nb - 1, 0)))
    return [pl.BlockSpec((ATTN_WIDTH, tq), lambda t: (0, tile_of(t))),
            pl.BlockSpec((KV_WIDTH, tq), lambda t: (krow, tile_of(t))),
            pl.BlockSpec((KV_WIDTH, tq), lambda t: (vrow, tile_of(t))),
            halo(krow), halo(vrow)]


def _head_rows(g, r):
    h = g * Q_PER_KV + r
    return h, slice(h * HEAD_DIM, (h + 1) * HEAD_DIM)


def _group_stack(ref, g, cols):
    return jnp.concatenate([ref[_head_rows(g, r)[1], cols] for r in range(Q_PER_KV)], axis=1)


def _attn_fwd(zt, sinks, *, name):
    T = zt.shape[1]
    tq = _tile(T, TQ_ATTN, ATTN_BLOCK)
    nb = tq // ATTN_BLOCK

    def body(q_ref, k_ref, v_ref, kh_ref, vh_ref, s_ref, o_ref, kf, vf, pt):
        kf[:, 0:ATTN_BLOCK] = kh_ref[...]
        kf[:, ATTN_BLOCK:] = k_ref[...]
        vf[:, 0:ATTN_BLOCK] = vh_ref[...]
        vf[:, ATTN_BLOCK:] = v_ref[...]
        for b in range(nb):
            cols = slice(b * ATTN_BLOCK, (b + 1) * ATTN_BLOCK)
            win = slice(b * ATTN_BLOCK, (b + 2) * ATTN_BLOCK)
            valid = _attn_mask_t((pl.program_id(0) == 0) if b == 0 else False)
            for g in range(N_KV_HEADS):
                gr = slice(g * HEAD_DIM, (g + 1) * HEAD_DIM)
                st = _dot_tn(kf[gr, win], _group_stack(q_ref, g, cols)) * ATTN_SCALE
                for r in range(Q_PER_KV):
                    h, _ = _head_rows(g, r)
                    sl = slice(r * ATTN_BLOCK, (r + 1) * ATTN_BLOCK)
                    probs, _ = _attn_probs_t(st[:, sl], valid, s_ref[h])
                    pt[:, sl] = probs.astype(BF16)
                ot = _dot(vf[gr, win], pt[...])
                for r in range(Q_PER_KV):
                    o_ref[_head_rows(g, r)[1], cols] = ot[:, r * ATTN_BLOCK:(r + 1) * ATTN_BLOCK].astype(BF16)

    return pl.pallas_call(
        body, name=name, grid=(T // tq,),
        in_specs=_attn_specs(tq, lambda t: t) + [pl.BlockSpec(memory_space=pltpu.SMEM)],
        out_specs=pl.BlockSpec((ATTN_WIDTH, tq), lambda t: (0, t)),
        out_shape=SDS((ATTN_WIDTH, T), BF16),
        scratch_shapes=[pltpu.VMEM((KV_WIDTH, tq + ATTN_BLOCK), BF16)] * 2
        + [pltpu.VMEM((2 * ATTN_BLOCK, GROUP_LANES), BF16)],
        compiler_params=_params("parallel"),
    )(zt, zt, zt, zt, zt, sinks)


def _attn_bwd(zt, sinks, dot_, *, name):
    T = zt.shape[1]
    tq = _tile(T, TQ_ATTN, ATTN_BLOCK)
    nb = tq // ATTN_BLOCK
    nt = T // tq
    tile_of = lambda t: nt - 1 - t

    def body(q_ref, k_ref, v_ref, kh_ref, vh_ref, do_ref, s_ref, dz_ref, dsink_ref, kf, vf, dkf, dvf, carry, pt, dst):
        t = pl.program_id(0)

        @pl.when(t == 0)
        def _():
            carry[...] = jnp.zeros_like(carry)
            dsink_ref[...] = jnp.zeros_like(dsink_ref)

        kf[:, 0:ATTN_BLOCK] = kh_ref[...]
        kf[:, ATTN_BLOCK:] = k_ref[...]
        vf[:, 0:ATTN_BLOCK] = vh_ref[...]
        vf[:, ATTN_BLOCK:] = v_ref[...]
        dkf[...] = jnp.zeros_like(dkf)
        dvf[...] = jnp.zeros_like(dvf)
        dkf[:, tq:] = carry[0:KV_WIDTH, :]
        dvf[:, tq:] = carry[KV_WIDTH:, :]
        lane = lax.broadcasted_iota(jnp.int32, (1, LANES), 1)
        dsink = jnp.zeros((1, LANES), F32)
        for b in range(nb):
            cols = slice(b * ATTN_BLOCK, (b + 1) * ATTN_BLOCK)
            win = slice(b * ATTN_BLOCK, (b + 2) * ATTN_BLOCK)
            valid = _attn_mask_t((t == nt - 1) if b == 0 else False)
            for g in range(N_KV_HEADS):
                gr = slice(g * HEAD_DIM, (g + 1) * HEAD_DIM)
                kt2, vt2 = kf[gr, win], vf[gr, win]
                qst = _group_stack(q_ref, g, cols)
                dost = _group_stack(do_ref, g, cols)
                st = _dot_tn(kt2, qst) * ATTN_SCALE
                dpt = _dot_tn(vt2, dost)
                for r in range(Q_PER_KV):
                    h, _ = _head_rows(g, r)
                    sl = slice(r * ATTN_BLOCK, (r + 1) * ATTN_BLOCK)
                    probs, psink = _attn_probs_t(st[:, sl], valid, s_ref[h])
                    dp = dpt[:, sl]
                    delta = jnp.sum(probs * dp, axis=0, keepdims=True)
                    pt[:, sl] = probs.astype(BF16)
                    dst[:, sl] = (probs * (dp - delta)).astype(BF16)
                    dsink = dsink + jnp.where(lane == h, -jnp.sum(psink * delta), 0.0)
                dqt = _dot(kt2, dst[...]) * ATTN_SCALE
                for r in range(Q_PER_KV):
                    dz_ref[_head_rows(g, r)[1], cols] = dqt[:, r * ATTN_BLOCK:(r + 1) * ATTN_BLOCK].astype(BF16)
                dkf[gr, win] += _dot_nt(qst, dst[...]) * ATTN_SCALE
                dvf[gr, win] += _dot_nt(dost, pt[...])
        dz_ref[ATTN_WIDTH:ATTN_WIDTH + KV_WIDTH, :] = dkf[:, ATTN_BLOCK:].astype(BF16)
        dz_ref[ATTN_WIDTH + KV_WIDTH:, :] = dvf[:, ATTN_BLOCK:].astype(BF16)
        carry[0:KV_WIDTH, :] = dkf[:, 0:ATTN_BLOCK]
        carry[KV_WIDTH:, :] = dvf[:, 0:ATTN_BLOCK]
        dsink_ref[...] += dsink

    return pl.pallas_call(
        body, name=name, grid=(nt,),
        in_specs=_attn_specs(tq, tile_of) + [pl.BlockSpec((ATTN_WIDTH, tq), lambda t: (0, tile_of(t))),
                                             pl.BlockSpec(memory_space=pltpu.SMEM)],
        out_specs=[pl.BlockSpec((QKV_WIDTH, tq), lambda t: (0, tile_of(t))),
                   pl.BlockSpec((8, LANES), lambda t: (0, 0))],
        out_shape=[SDS((QKV_WIDTH, T), BF16), SDS((8, LANES), F32)],
        scratch_shapes=[pltpu.VMEM((KV_WIDTH, tq + ATTN_BLOCK), BF16)] * 2
        + [pltpu.VMEM((KV_WIDTH, tq + ATTN_BLOCK), F32)] * 2 + [pltpu.VMEM((2 * KV_WIDTH, ATTN_BLOCK), F32)]
        + [pltpu.VMEM((2 * ATTN_BLOCK, GROUP_LANES), BF16)] * 2,
        compiler_params=_params("arbitrary"),
    )(zt, zt, zt, zt, zt, dot_, sinks)


def _layer_norm_stats(v):
    mu = jnp.mean(v, axis=-1, keepdims=True)
    xc = v - mu
    rstd = lax.rsqrt(jnp.mean(xc * xc, axis=-1, keepdims=True) + LN_EPS)
    return rstd, xc * rstd


def _sgu_fwd(zmain, ln_g, ln_b, wm, bias, *, name):
    T = zmain.shape[0]
    ts = _tile(T, TS_SGU, SGU_CHUNK)

    def body(u_ref, v_ref, g_ref, b_ref, w_ref, bias_ref, y_ref):
        u = _gelu(u_ref[...].astype(F32))
        _, vh = _layer_norm_stats(_gelu(v_ref[...].astype(F32)))
        vn = (vh * g_ref[...] + b_ref[...]).astype(BF16)
        for ch in range(ts // SGU_CHUNK):
            rows = slice(ch * SGU_CHUNK, (ch + 1) * SGU_CHUNK)
            for g in range(SGU_GROUPS):
                cols = slice(g * 128, (g + 1) * 128)
                s = _dot(w_ref[g], vn[rows, cols]) + bias_ref[g]
                y_ref[rows, cols] = (u[rows, cols] * s).astype(BF16)

    full = _resident
    return pl.pallas_call(
        body, name=name, grid=(T // ts,),
        in_specs=[pl.BlockSpec((ts, SGU_WIDTH), lambda i: (i, 0)), pl.BlockSpec((ts, SGU_WIDTH), lambda i: (i, 1)),
                  full((1, SGU_WIDTH)), full((1, SGU_WIDTH)), full(wm.shape), full(bias.shape)],
        out_specs=pl.BlockSpec((ts, SGU_WIDTH), lambda i: (i, 0)),
        out_shape=SDS((T, SGU_WIDTH), BF16),
        compiler_params=_params("parallel"),
    )(zmain, zmain, ln_g, ln_b, wm, bias)


def _sgu_bwd(zmain, dzmain, dy, ln_g, ln_b, wm, wmt, bias, *, name):
    T = zmain.shape[0]
    ts = _tile(T, TS_SGU, SGU_CHUNK)

    def body(u_ref, v_ref, dy_ref, g_ref, b_ref, w_ref, wt_ref, bias_ref, _, dz_ref, dw_ref, db_ref, dlg_ref, dlb_ref,
             dvn):
        @pl.when(pl.program_id(0) == 0)
        def _():
            dw_ref[...] = jnp.zeros_like(dw_ref)
            db_ref[...] = jnp.zeros_like(db_ref)
            dlg_ref[...] = jnp.zeros_like(dlg_ref)
            dlb_ref[...] = jnp.zeros_like(dlb_ref)

        us = u_ref[...].astype(F32)
        vs = v_ref[...].astype(F32)
        u = _gelu(us)
        rstd, vh = _layer_norm_stats(_gelu(vs))
        vn = (vh * g_ref[...] + b_ref[...]).astype(BF16)
        causal = (lax.broadcasted_iota(jnp.int32, (SGU_CHUNK, SGU_CHUNK), 0)
                  >= lax.broadcasted_iota(jnp.int32, (SGU_CHUNK, SGU_CHUNK), 1))
        lane = lax.broadcasted_iota(jnp.int32, (SGU_CHUNK, LANES), 1)
        db = jnp.zeros((SGU_CHUNK, LANES), F32)
        for ch in range(ts // SGU_CHUNK):
            rows = slice(ch * SGU_CHUNK, (ch + 1) * SGU_CHUNK)
            for g in range(SGU_GROUPS):
                cols = slice(g * 128, (g + 1) * 128)
                vng = vn[rows, cols]
                s = _dot(w_ref[g], vng) + bias_ref[g]
                dyf = dy_ref[rows, cols].astype(F32)
                dz_ref[rows, cols] = (dyf * s * _gelu_grad(us[rows, cols])).astype(BF16)
                dsf = dyf * u[rows, cols]
                dsb = dsf.astype(BF16)
                dvn[rows, cols] = _dot(wt_ref[g], dsb)
                dw_ref[g] += jnp.where(causal, _dot_nt(dsb, vng), 0.0)
                db = db + jnp.where(lane == g, jnp.sum(dsf, axis=1, keepdims=True), 0.0)
        db_ref[...] += db
        dvnf = dvn[...]
        dlg_ref[...] += jnp.sum(dvnf * vh, axis=0, keepdims=True)
        dlb_ref[...] += jnp.sum(dvnf, axis=0, keepdims=True)
        dvh = dvnf * g_ref[...]
        dv = rstd * (dvh - jnp.mean(dvh, axis=-1, keepdims=True) - vh * jnp.mean(dvh * vh, axis=-1, keepdims=True))
        dz_ref[:, SGU_WIDTH:] = (dv * _gelu_grad(vs)).astype(BF16)

    full = _resident
    vec = full((1, SGU_WIDTH))
    acc = lambda shape: pl.BlockSpec(shape, lambda i: (0,) * len(shape))
    return pl.pallas_call(
        body, name=name, grid=(T // ts,),
        in_specs=[pl.BlockSpec((ts, SGU_WIDTH), lambda i: (i, 0)), pl.BlockSpec((ts, SGU_WIDTH), lambda i: (i, 1)),
                  pl.BlockSpec((ts, SGU_WIDTH), lambda i: (i, 0)), vec, vec, full(wm.shape), full(wm.shape),
                  full(bias.shape), pl.BlockSpec(memory_space=pl.ANY)],
        out_specs=[pl.BlockSpec((ts, 2 * SGU_WIDTH), lambda i: (i, 0)), acc(wm.shape),
                   acc((SGU_CHUNK, LANES)), acc((1, SGU_WIDTH)), acc((1, SGU_WIDTH))],
        out_shape=[SDS(dzmain.shape, BF16), SDS(wm.shape, F32), SDS((SGU_CHUNK, LANES), F32),
                   SDS((1, SGU_WIDTH), F32), SDS((1, SGU_WIDTH), F32)],
        scratch_shapes=[pltpu.VMEM((ts, SGU_WIDTH), F32)],
        input_output_aliases={8: 0},
        compiler_params=_params("arbitrary"),
    )(zmain, zmain, dy, ln_g, ln_b, wm, wmt, bias, dzmain)


def _merge_fwd(y_attn_t, y_sgu, zmain, w_a, w_s, w_o, x, g_post, *, name):
    T, D = x.shape
    tm = _tile(T, TM_ROW, LANES)

    def body(ya_ref, ys_ref, ga_ref, gb_ref, wa_ref, ws_ref, wo_ref, x_ref, g_ref, xn_ref, pa_ref, ps_ref, o_ref):
        pa = _dot_tn(ya_ref[...], wa_ref[...])
        ps = _dot(ys_ref[...], ws_ref[...])
        pa_ref[...] = pa.astype(BF16)
        ps_ref[...] = ps.astype(BF16)
        merged = _sigmoid(ga_ref[...].astype(F32)) * pa + _sigmoid(gb_ref[...].astype(F32)) * ps
        out = _dot(merged.astype(BF16), wo_ref[...])
        o_ref[...] = out.astype(BF16)
        _, oh = _rms_stats(out)
        xn_ref[...] = x_ref[...] + oh * g_ref[...]

    row = lambda col: pl.BlockSpec((tm, D), lambda i: (i, col))
    wfull = _resident((D, D))
    return pl.pallas_call(
        body, name=name, grid=(T // tm,),
        in_specs=[pl.BlockSpec((D, tm), lambda i: (0, i)), row(0), row(2), row(3), wfull, wfull, wfull, row(0),
                  pl.BlockSpec((1, D), lambda i: (0, 0))],
        out_specs=[row(0)] * 4,
        out_shape=[SDS((T, D), F32), SDS((T, D), BF16), SDS((T, D), BF16), SDS((T, D), BF16)],
        compiler_params=_params("parallel"),
    )(y_attn_t, y_sgu, zmain, zmain, w_a, w_s, w_o, x, g_post)


def _merge_bwd(dy, out, g_post, pa, ps, zmain, w_a, w_s, w_o, *, name):
    T, D = dy.shape
    tm = _tile(T, TM_ROW, LANES)

    def body(dy_ref, o_ref, g_ref, pa_ref, ps_ref, ga_ref, gb_ref, wa_ref, ws_ref, wo_ref,
             dz_ref, dout_ref, mg_ref, dpa_ref, dps_ref, dya_ref, dys_ref, dg_ref):
        @pl.when(pl.program_id(0) == 0)
        def _():
            dg_ref[...] = jnp.zeros_like(dg_ref)

        dout, dg = _rms_bwd(o_ref[...].astype(F32), g_ref[...], dy_ref[...])
        dg_ref[...] += dg
        doutb = dout.astype(BF16)
        dout_ref[...] = doutb
        dm = _dot_nt(doutb, wo_ref[...])
        pa = pa_ref[...].astype(F32)
        ps = ps_ref[...].astype(F32)
        sa = _sigmoid(ga_ref[...].astype(F32))
        sb = _sigmoid(gb_ref[...].astype(F32))
        mg_ref[...] = (sa * pa + sb * ps).astype(BF16)
        dpa = (dm * sa).astype(BF16)
        dps = (dm * sb).astype(BF16)
        dpa_ref[...] = dpa
        dps_ref[...] = dps
        dz_ref[:, 0:D] = (dm * pa * sa * (1.0 - sa)).astype(BF16)
        dz_ref[:, D:] = (dm * ps * sb * (1.0 - sb)).astype(BF16)
        dya_ref[...] = _dot_nt(wa_ref[...], dpa).astype(BF16)
        dys_ref[...] = _dot_nt(dps, ws_ref[...]).astype(BF16)

    row = lambda col: pl.BlockSpec((tm, D), lambda i: (i, col))
    wfull = _resident((D, D))
    vec = pl.BlockSpec((1, D), lambda i: (0, 0))
    act = SDS((T, D), BF16)
    return pl.pallas_call(
        body, name=name, grid=(T // tm,),
        in_specs=[row(0), row(0), vec, row(0), row(0), row(2), row(3), wfull, wfull, wfull],
        out_specs=[pl.BlockSpec((tm, 2 * D), lambda i: (i, 1))] + [row(0)] * 4
        + [pl.BlockSpec((D, tm), lambda i: (0, i)), row(0), vec],
        out_shape=[SDS(zmain.shape, BF16)] + [act] * 4 + [SDS((D, T), BF16), act, SDS((1, D), F32)],
        compiler_params=_params("arbitrary"),
    )(dy, out, g_post, pa, ps, zmain, zmain, w_a, w_s, w_o)


def _loss_head(y, target, *, name):
    T, D = y.shape
    tm = _tile(T, TM_ROW, 8)

    def body(y_ref, t_ref, dy_ref, l_ref):
        @pl.when(pl.program_id(0) == 0)
        def _():
            l_ref[...] = jnp.zeros_like(l_ref)

        e = y_ref[...] - t_ref[...]
        dy_ref[...] = e * (1.0 / D)
        l_ref[...] += jnp.sum(jnp.mean(e * e, axis=-1, keepdims=True))

    row = pl.BlockSpec((tm, D), lambda i: (i, 0))
    return pl.pallas_call(
        body, name=name, grid=(T // tm,), in_specs=[row, row],
        out_specs=[row, pl.BlockSpec((8, LANES), lambda i: (0, 0))],
        out_shape=[SDS((T, D), F32), SDS((8, LANES), F32)],
        compiler_params=_params("arbitrary"),
    )(y, target)


def _adamw(w, g, m, v, *, name):
    shape = w.shape
    cols = shape[-1]
    rows = w.size // cols
    w2, g2, m2, v2 = (t.reshape(rows, cols) for t in (w, g, m, v))
    tr = _tile(rows, max(8, (256 * 1024) // cols // 8 * 8), 8)

    def body(w_ref, g_ref, m_ref, v_ref, d_ref, nm_ref, nv_ref):
        gg = g_ref[...]
        nm = ADAM_B1 * m_ref[...] + (1.0 - ADAM_B1) * gg
        nv = ADAM_B2 * v_ref[...] + (1.0 - ADAM_B2) * (gg * gg)
        m_hat = nm / (1.0 - ADAM_B1 ** ADAM_STEP)
        v_hat = nv / (1.0 - ADAM_B2 ** ADAM_STEP)
        d_ref[...] = -ADAM_LR * (m_hat / (jnp.sqrt(v_hat) + ADAM_EPS) + ADAM_WD * w_ref[...])
        nm_ref[...] = nm
        nv_ref[...] = nv

    blk = pl.BlockSpec((tr, cols), lambda i: (i, 0))
    outs = pl.pallas_call(
        body, name=name, grid=(rows // tr,), in_specs=[blk] * 4, out_specs=[blk] * 3,
        out_shape=[SDS((rows, cols), F32)] * 3, compiler_params=_params("parallel"),
    )(w2, g2, m2, v2)
    return tuple(o.reshape(shape) for o in outs)


def _sum_terms(terms, n_rows, n_lead, dtypes, *, name):
    cols = terms[0][0].shape[-1]
    tr = _tile(n_rows, 704 if len(terms) <= 4 else 256, SUBLANES_BF16)
    nblk = n_rows // tr
    n_out = len(dtypes)

    def body(*refs):
        acc = refs[0][...].astype(F32)
        for r in refs[1:-n_out]:
            acc = acc + r[...].astype(F32)
        for o_ref in refs[-n_out:]:
            o_ref[...] = acc.astype(o_ref.dtype)

    def spec(lead, first):
        return pl.BlockSpec((1, tr, cols), lambda a, i: (a if lead is None else lead(), first() * nblk + i, 0))

    out = pl.BlockSpec((1, tr, cols), lambda a, i: (a, i, 0))
    return pl.pallas_call(
        body, name=name, grid=(n_lead, nblk), in_specs=[spec(lead, first) for _, lead, first in terms],
        out_specs=[out] * n_out, out_shape=[SDS((n_lead, n_rows, cols), d) for d in dtypes],
        compiler_params=_params("arbitrary", "arbitrary"),
    )(*[a for a, _, _ in terms])


def _position():
    x, y, c = (lax.axis_index(a) for a in AXES)
    chips = [(1 - x, y), (x, 1 - y), (1 - x, 1 - y)]
    return x, y, c, chips


ANY = pl.BlockSpec(memory_space=pl.ANY)


def _remote(src, dst, send_sems, recv_sems, k, to):
    return pltpu.make_async_remote_copy(src_ref=src, dst_ref=dst, send_sem=send_sems.at[k], recv_sem=recv_sems.at[k],
                                        device_id=to, device_id_type=MESH)


def _comm_call(body, arrays, out_shapes, n_sems, *, name):
    n = len(arrays)

    def wrapped(*refs):
        body(refs[:n], refs[n:n + len(out_shapes)], refs[-2], refs[-1])

    return pl.pallas_call(
        wrapped, name=name, in_specs=[ANY] * n, out_specs=[ANY] * len(out_shapes), out_shape=out_shapes,
        scratch_shapes=[pltpu.SemaphoreType.DMA((n_sems,)), pltpu.SemaphoreType.DMA((n_sems,))],
    )(*arrays)


def _gather_shards(packs, *, name):
    def body(p_refs, o_refs, send_sems, recv_sems):
        x, y, c, chips = _position()
        sent, passed = [], []
        for a, (p_ref, o_ref) in enumerate(zip(p_refs, o_refs)):
            mine = pl.ds(c * (p_ref.shape[0] // 2), p_ref.shape[0] // 2)
            sent += [_remote(p_ref.at[mine], o_ref.at[2 * x + y, mine], send_sems, recv_sems, 6 * a + j, (*chip, c))
                     for j, chip in enumerate(chips)]
        for cp in sent:
            cp.start()
        for a, o_ref in enumerate(o_refs):
            mine = pl.ds(c * (o_ref.shape[1] // 2), o_ref.shape[1] // 2)
            for j, (cx, cy) in enumerate(chips):
                landed = o_ref.at[2 * cx + cy, mine]
                _remote(landed, landed, send_sems, recv_sems, 6 * a + j, (x, y, c)).wait_recv()
                cp = _remote(landed, landed, send_sems, recv_sems, 6 * a + 3 + j, (x, y, 1 - c))
                cp.start()
                passed.append(cp)
        for a, o_ref in enumerate(o_refs):
            other = pl.ds((1 - c) * (o_ref.shape[1] // 2), o_ref.shape[1] // 2)
            for j, (cx, cy) in enumerate(chips):
                theirs = o_ref.at[2 * cx + cy, other]
                _remote(theirs, theirs, send_sems, recv_sems, 6 * a + 3 + j, (x, y, c)).wait_recv()
        for cp in sent + passed:
            cp.wait_send()

    return _comm_call(body, packs, [SDS((N_CHIPS,) + p.shape, p.dtype) for p in packs], 6 * len(packs), name=name)


def _sibling_exchange(gs, *, name):
    def body(g_refs, o_refs, send_sems, recv_sems):
        x, y, c, _ = _position()
        sent = [_remote(g_ref.at[:, pl.ds((1 - c) * o_ref.shape[1], o_ref.shape[1])], o_ref, send_sems, recv_sems, a,
                        (x, y, 1 - c)) for a, (g_ref, o_ref) in enumerate(zip(g_refs, o_refs))]
        for cp in sent:
            cp.start()
        for cp in sent:
            cp.wait()

    return _comm_call(body, gs, [SDS((N_CHIPS, g.shape[1] // 2, g.shape[2]), g.dtype) for g in gs], len(gs), name=name)


def _chip_scatter(ps, *, name):
    def body(p_refs, o_refs, send_sems, recv_sems):
        x, y, c, chips = _position()
        sent = [_remote(p_ref.at[2 * cx + cy], o_ref.at[j], send_sems, recv_sems, 3 * a + j, (cx, cy, c))
                for a, (p_ref, o_ref) in enumerate(zip(p_refs, o_refs)) for j, (cx, cy) in enumerate(chips)]
        for cp in sent:
            cp.start()
        for cp in sent:
            cp.wait()

    return _comm_call(body, ps, [SDS((N_CHIPS - 1,) + p.shape[1:], p.dtype) for p in ps], 3 * len(ps), name=name)


def _sibling_swap(ts, *, name):
    def body(t_refs, o_refs, send_sems, recv_sems):
        x, y, c, _ = _position()
        sent = [_remote(t_ref, o_ref, send_sems, recv_sems, a, (x, y, 1 - c))
                for a, (t_ref, o_ref) in enumerate(zip(t_refs, o_refs))]
        for cp in sent:
            cp.start()
        for cp in sent:
            cp.wait()

    return _comm_call(body, ts, [SDS(t.shape, t.dtype) for t in ts], len(ts), name=name)


def _gather_all(v, *, name):
    M, C = v.shape

    def body(v_ref, o_ref, send_sems, recv_sems):
        x, y, c, chips = _position()
        slot = lambda px, py, pc: o_ref.at[4 * px + 2 * py + pc]
        first = [_remote(v_ref, slot(x, y, c), send_sems, recv_sems, 0, (x, y, 1 - c))]
        first += [_remote(v_ref, slot(x, y, c), send_sems, recv_sems, 1 + j, (*chip, c)) for j, chip in enumerate(chips)]
        for cp in first:
            cp.start()
        passed = []
        for j, chip in enumerate(chips):
            landed = slot(*chip, c)
            _remote(landed, landed, send_sems, recv_sems, 1 + j, (x, y, c)).wait_recv()
            cp = _remote(landed, landed, send_sems, recv_sems, 4 + j, (x, y, 1 - c))
            cp.start()
            passed.append(cp)
        sib = slot(x, y, 1 - c)
        _remote(sib, sib, send_sems, recv_sems, 0, (x, y, c)).wait_recv()
        for j, chip in enumerate(chips):
            theirs = slot(*chip, 1 - c)
            _remote(theirs, theirs, send_sems, recv_sems, 4 + j, (x, y, c)).wait_recv()
        for cp in first + passed:
            cp.wait_send()

    return pl.pallas_call(
        body, name=name, in_specs=[ANY], out_specs=ANY, out_shape=SDS((N_DEV, M, C), v.dtype),
        scratch_shapes=[pltpu.SemaphoreType.DMA((7,)), pltpu.SemaphoreType.DMA((7,))],
    )(v)


BIG = ("ffn1_w1", "ffn2_w1", "w_in", "ffn1_w2", "ffn2_w2", "w_attn_branch", "w_sgu_branch", "w_out")
COL_SHARDED = ("ffn1_w1", "w_in", "ffn2_w1")
FFN_IN = ("ffn1_w1", "ffn2_w1")
SMALL = ("ffn1_pre_g", "ffn1_post_g", "mix_pre_g", "attn_sinks", "sgu_ln_g", "sgu_ln_b", "sgu_w", "sgu_b",
         "mix_post_g", "ffn2_pre_g", "ffn2_post_g")
WEIGHTS = ("ffn1_pre_g", "ffn1_w1", "ffn1_w2", "ffn1_post_g", "mix_pre_g", "w_in", "attn_sinks", "sgu_ln_g",
           "sgu_ln_b", "sgu_w", "sgu_b", "w_attn_branch", "w_sgu_branch", "w_out", "mix_post_g", "ffn2_pre_g",
           "ffn2_w1", "ffn2_w2", "ffn2_post_g")


def _column_chunks(w, tn):
    return jnp.swapaxes(w.reshape(w.shape[0], w.shape[1] // tn, tn), 0, 1)


def _width_classes(shard_shapes):
    widths = sorted({shard_shapes[n][-1] for n in BIG}, reverse=True)
    return [[n for n in BIG if shard_shapes[n][-1] == w] for w in widths]


def _class_rows(classes, shard_shapes, n_layers):
    where = {}
    for k, names in enumerate(classes):
        off = 0
        for n in names:
            r = shard_shapes[n][0]
            assert off % r == 0
            where[n] = (k, off, r)
            off += n_layers * r
    return where


def _ffn_fwd(x, pre_g, w1, w1_block, w2, post_g, tag):
    a, h = _norm_matmul(x, pre_g, w1, w1_block, name=f"{tag}_up", with_h=True)
    xn, o = _swiglu_out(a, w2, x, post_g, name=f"{tag}_down")
    return xn, (x, h, a, o)


def _ffn_bwd(dy, saved, pre_g, w1, w1_block, w2, post_g, dw1_into, dw2_into, tag):
    x, h, a, o = saved
    da, s, do, d_post = _ffn_bwd_hidden(dy, o, post_g, a, w2, name=f"{tag}_bwd_hidden")
    g2 = _dw_rows(s, do, *dw2_into, name=f"{tag}_dw2")
    dx, d_pre = _matmul_nt_norm_bwd(da, w1, w1_block, x, pre_g, dy, None, name=f"{tag}_bwd_in")
    g1 = _dw_cols(h, da, N_CHIPS, *dw1_into, name=f"{tag}_dw1")
    return dx, g1, g2, d_pre, d_post


def kernel(x, ffn1_pre_g, ffn1_w1, ffn1_w2, ffn1_post_g, mix_pre_g, w_in, attn_sinks, sgu_ln_g, sgu_ln_b, sgu_w, sgu_b, w_attn_branch, w_sgu_branch, w_out, mix_post_g, ffn2_pre_g, ffn2_w1, ffn2_w2, ffn2_post_g, loss_target, m_ffn1_pre_g, m_ffn1_w1, m_ffn1_w2, m_ffn1_post_g, m_mix_pre_g, m_w_in, m_attn_sinks, m_sgu_ln_g, m_sgu_ln_b, m_sgu_w, m_sgu_b, m_w_attn_branch, m_w_sgu_branch, m_w_out, m_mix_post_g, m_ffn2_pre_g, m_ffn2_w1, m_ffn2_w2, m_ffn2_post_g, v_ffn1_pre_g, v_ffn1_w1, v_ffn1_w2, v_ffn1_post_g, v_mix_pre_g, v_w_in, v_attn_sinks, v_sgu_ln_g, v_sgu_ln_b, v_sgu_w, v_sgu_b, v_w_attn_branch, v_w_sgu_branch, v_w_out, v_mix_post_g, v_ffn2_pre_g, v_ffn2_w1, v_ffn2_w2, v_ffn2_post_g):
    given = dict(locals())
    W = {n: given[n] for n in WEIGHTS}
    M = {n: given["m_" + n] for n in WEIGHTS}
    V = {n: given["v_" + n] for n in WEIGHTS}
    L = ffn1_w1.shape[0]
    T, D = x.shape[1], x.shape[2]
    xt = x.reshape(T, D)
    target = loss_target.reshape(T, D)
    assert L % 2 == 0 and D == ATTN_WIDTH == SGU_WIDTH and T % ATTN_BLOCK == 0

    shard_shapes = {n: W[n].shape[1:] for n in BIG}
    classes = _width_classes(shard_shapes)
    my_chip = 2 * lax.axis_index("x") + lax.axis_index("y")
    my_core = lax.axis_index("c")
    where = _class_rows(classes, shard_shapes, L)
    packs = [jnp.concatenate([W[n].reshape(-1, shard_shapes[n][1]).astype(BF16) for n in names], axis=0)
             for names in classes]
    gathered = _gather_shards(packs, name="gather_weights")
    wcls = [lax.dynamic_update_slice(got, pack[None], (my_chip, 0, 0)) for pack, got in zip(packs, gathered)]

    def block_of(n, l):
        k, off, r = where[n]
        return k, off // r + l

    def chip_shards(n, l):
        k, off, r = where[n]
        return wcls[k][:, off + l * r:off + (l + 1) * r, :]

    full = []
    for l in range(L):
        fw = {n: chip_shards(n, l).reshape(-1, D) for n in BIG if n not in COL_SHARDED}
        w_in_l = jnp.swapaxes(chip_shards("w_in", l), 0, 1).reshape(D, -1)
        fw["w_qkv_t"] = w_in_l[:, :QKV_WIDTH].T
        fw["w_main"] = _column_chunks(w_in_l[:, QKV_WIDTH:], D)
        for n in FFN_IN:
            fw[n] = (wcls[block_of(n, l)[0]], block_of(n, l)[1])
        full.append(fw)

    row = lambda name, l: W[name][l].reshape(1, -1)
    causal = jnp.tril(jnp.ones((SGU_CHUNK, SGU_CHUNK), dtype=bool))
    saved = []
    h_cur = xt
    for l in range(L):
        fw = full[l]
        sv = {}
        h_cur, sv["ffn1"] = _ffn_fwd(h_cur, row("ffn1_pre_g", l), *fw["ffn1_w1"], fw["ffn1_w2"], row("ffn1_post_g", l),
                                     f"l{l}_ffn1")
        zqkv, hm = _norm_matmul_t(h_cur, row("mix_pre_g", l), fw["w_qkv_t"], name=f"l{l}_mix_in_qkv")
        zmain, = _norm_matmul(h_cur, row("mix_pre_g", l), fw["w_main"], 0, name=f"l{l}_mix_in_main", with_h=False)
        wm = jnp.where(causal[None], sgu_w[l], 0.0).astype(BF16)
        wmt = jnp.swapaxes(wm, 1, 2)
        bias = jnp.broadcast_to(sgu_b[l][:, :, None], (SGU_GROUPS, SGU_CHUNK, 128)).astype(F32)
        y_attn = _attn_fwd(zqkv, attn_sinks[l], name=f"l{l}_attn")
        y_sgu = _sgu_fwd(zmain, row("sgu_ln_g", l), row("sgu_ln_b", l), wm, bias, name=f"l{l}_sgu")
        x_mix = h_cur
        h_cur, pa, ps, mo = _merge_fwd(y_attn, y_sgu, zmain, fw["w_attn_branch"], fw["w_sgu_branch"], fw["w_out"],
                                       x_mix, row("mix_post_g", l), name=f"l{l}_merge")
        sv["mix"] = (x_mix, hm, zqkv, zmain, y_attn, y_sgu, pa, ps, mo, wm, wmt, bias)
        h_cur, sv["ffn2"] = _ffn_fwd(h_cur, row("ffn2_pre_g", l), *fw["ffn2_w1"], fw["ffn2_w2"], row("ffn2_post_g", l),
                                     f"l{l}_ffn2")
        saved.append(sv)

    dy, lsum = _loss_head(h_cur, target, name="loss_head")
    loss = lax.psum(0.5 * lsum[0, 0], AXES)

    k_in = where["w_in"][0]
    assert classes[k_in] == ["w_in"]
    gcls = [None if k == k_in else lax.empty((N_CHIPS,) + p.shape, F32) for k, p in enumerate(packs)]
    dw_in = [None] * L
    small_grads = [None] * L

    def into(n, l):
        return gcls[block_of(n, l)[0]], block_of(n, l)[1]

    def ffn_bwd(dy, which, l):
        n1, n2 = f"{which}_w1", f"{which}_w2"
        dy, g1, g2, d_pre, d_post = _ffn_bwd(
            dy, saved[l][which], row(f"{which}_pre_g", l), *full[l][n1], full[l][n2], row(f"{which}_post_g", l),
            into(n1, l), into(n2, l), f"l{l}_{which}")
        gcls[where[n1][0]], gcls[where[n2][0]] = g1, g2
        return dy, d_pre, d_post

    for l in reversed(range(L)):
        fw, sv = full[l], saved[l]
        gs = {}
        dy, gs["ffn2_pre_g"], gs["ffn2_post_g"] = ffn_bwd(dy, "ffn2", l)

        x_mix, hm, zqkv, zmain, y_attn, y_sgu, pa, ps, mo, wm, wmt, bias = sv["mix"]
        dzmain, dout, merged, dpa, dps, dya, dys, gs["mix_post_g"] = _merge_bwd(
            dy, mo, row("mix_post_g", l), pa, ps, zmain, fw["w_attn_branch"], fw["w_sgu_branch"], fw["w_out"],
            name=f"l{l}_merge_bwd")
        k_sq = where["w_out"][0]
        gcls[k_sq] = _dw_rows(merged, dout, *into("w_out", l), name=f"l{l}_dw_out")
        gcls[k_sq] = _dw_rows(y_attn, dpa, *into("w_attn_branch", l), name=f"l{l}_dw_attn", a_feature_major=True)
        gcls[k_sq] = _dw_rows(y_sgu, dps, *into("w_sgu_branch", l), name=f"l{l}_dw_sgu")
        dzqkv, dsink = _attn_bwd(zqkv, attn_sinks[l], dya, name=f"l{l}_attn_bwd")
        gs["attn_sinks"] = dsink[0, :N_Q_HEADS]
        dzmain, dsw, dsb, gs["sgu_ln_g"], gs["sgu_ln_b"] = _sgu_bwd(
            zmain, dzmain, dys, row("sgu_ln_g", l), row("sgu_ln_b", l), wm, wmt, bias, name=f"l{l}_sgu_bwd")
        gs["sgu_w"] = dsw
        gs["sgu_b"] = dsb[:, :SGU_GROUPS].T
        dh_qkv = _matmul_tn_rows(dzqkv, fw["w_qkv_t"], name=f"l{l}_mix_bwd_qkv")
        dy, gs["mix_pre_g"] = _matmul_nt_norm_bwd(dzmain, fw["w_main"], 0, x_mix, row("mix_pre_g", l), dy, dh_qkv,
                                                   name=f"l{l}_mix_bwd_in")
        dw_main = _dw_cols(hm, dzmain, zmain.shape[1] // D, None, 0, name=f"l{l}_dw_in_main")
        dw_in[l] = jnp.concatenate([_matmul_tokens(dzqkv, hm, name=f"l{l}_dw_in_qkv").T,
                                    jnp.swapaxes(dw_main, 0, 1).reshape(D, -1)], axis=1)

        dy, gs["ffn1_pre_g"], gs["ffn1_post_g"] = ffn_bwd(dy, "ffn1", l)
        small_grads[l] = gs
    grad_x = dy.reshape(x.shape)

    w_in_width = shard_shapes["w_in"][1]
    gcls[k_in] = jnp.stack([jnp.concatenate([g[:, s * w_in_width:(s + 1) * w_in_width] for g in dw_in], axis=0)
                            for s in range(N_CHIPS)])
    grs = gcls
    from_sibling = _sibling_exchange(grs, name="grads_sibling_exchange")
    zero = lambda: 0
    core_in_map = lambda: lax.axis_index("c")
    chip_in_map = lambda: 2 * lax.axis_index("x") + lax.axis_index("y")
    pairs = [_sum_terms([(g, None, core_in_map), (fs, None, zero)], fs.shape[1], N_CHIPS, (F32, BF16),
                        name=f"grads_pair_sum{k}") for k, (g, fs) in enumerate(zip(grs, from_sibling))]
    from_chips = _chip_scatter([p[1] for p in pairs], name="grads_chip_scatter")
    halves = [_sum_terms([(p[0], chip_in_map, zero)] + [(fc, (lambda j=j: j), zero) for j in range(N_CHIPS - 1)],
                         fc.shape[1], 1, (F32,), name=f"grads_chip_sum{k}")[0][0]
              for k, (p, fc) in enumerate(zip(pairs, from_chips))]
    others = _sibling_swap(halves, name="grads_sibling_swap")

    grads = {n: [None] * L for n in SMALL}
    for names, half, other in zip(classes, halves, others):
        reduced = jnp.concatenate([jnp.where(my_core == 0, half, other), jnp.where(my_core == 0, other, half)], axis=0)
        for n in names:
            _, off, r = where[n]
            grads[n] = reduced[off:off + L * r].reshape((L,) + shard_shapes[n])

    def small_rows(gs):
        parts = []
        for n in SMALL:
            flat = gs[n].reshape(-1)
            pad = (-flat.shape[0]) % D
            parts.append(jnp.pad(flat, (0, pad)).reshape(-1, D))
        return jnp.concatenate(parts, axis=0)

    spack = jnp.concatenate([small_rows(small_grads[l]) for l in range(L)], axis=0)
    n_small = spack.shape[0]
    pad_rows = (-n_small) % SUBLANES_BF16
    spack = jnp.pad(spack, ((0, pad_rows), (0, 0)))
    everyone = _gather_all(spack, name="small_grads_gather")
    is_me = (jnp.arange(N_DEV) == 2 * my_chip + my_core)[:, None, None]
    everyone = jnp.where(is_me, spack[None], everyone)
    ssum = _sum_terms([(everyone, (lambda d=d: d), zero) for d in range(N_DEV)], spack.shape[0], 1, (F32,),
                      name="small_grads_sum")[0][0]
    per_layer = n_small // L
    for l in range(L):
        r0 = l * per_layer
        for n in SMALL:
            shp = W[n].shape[1:]
            size = math.prod(shp)
            nr = -(-size // D)
            grads[n][l] = ssum[r0:r0 + nr].reshape(-1)[:size].reshape(shp)
            r0 += nr
    grads.update({n: jnp.stack(grads[n]) for n in SMALL})

    delta, new_m, new_v = {}, {}, {}
    for n in WEIGHTS:
        delta[n], new_m[n], new_v[n] = _adamw(W[n], grads[n], M[n], V[n], name=f"adamw_{n}")

    return (loss, grad_x, *[grads[n] for n in WEIGHTS], *[delta[n] for n in WEIGHTS],
            *[new_m[n] for n in WEIGHTS], *[new_v[n] for n in WEIGHTS])
```

```python
import functools
import math

import jax
import jax.numpy as jnp
from jax import lax
from jax.experimental import pallas as pl
from jax.experimental.pallas import tpu as pltpu

F32, BF16 = jnp.float32, jnp.bfloat16
SDS = jax.ShapeDtypeStruct
MESH = pl.DeviceIdType.MESH
AXES = ("x", "y", "c")

HEAD_DIM = 64
N_Q_HEADS = 16
N_KV_HEADS = 2
Q_PER_KV = N_Q_HEADS // N_KV_HEADS
ATTN_WIDTH = N_Q_HEADS * HEAD_DIM
KV_WIDTH = N_KV_HEADS * HEAD_DIM
ATTN_BLOCK = 128
SGU_CHUNK = 128
SGU_GROUPS = 8
SGU_WIDTH = SGU_GROUPS * 128
QKV_WIDTH = ATTN_WIDTH + 2 * KV_WIDTH
RMS_EPS = 1e-6
LN_EPS = 1e-5
MASK_VALUE = -1e30
ATTN_SCALE = 1.0 / math.sqrt(HEAD_DIM)

ADAM_LR, ADAM_B1, ADAM_B2, ADAM_EPS, ADAM_WD, ADAM_STEP = 0.001, 0.9, 0.999, 1e-08, 0.01, 10

N_CHIPS = 4
N_DEV = 8

VMEM_LIMIT_BYTES = 56 * 1024 * 1024
LANES = 128
SUBLANES_BF16 = 16

TM_NORM_MATMUL = 1024
TM_ROW = 512
TM_FFN_BWD = 512
TT_REDUCE = 1024
TQ_ATTN = 512
TS_SGU = 512


def _tile(n, pref, mult):
    t = (min(pref, n) // mult) * mult
    while t >= mult:
        if n % t == 0:
            return t
        t -= mult
    return n


def _params(*sem):
    return pltpu.CompilerParams(dimension_semantics=sem, vmem_limit_bytes=VMEM_LIMIT_BYTES)


def _dot(a, b):
    return jnp.dot(a, b, preferred_element_type=F32)


def _dot_nt(a, b):
    return lax.dot_general(a, b, (((1,), (1,)), ((), ())), preferred_element_type=F32)


def _dot_tn(a, b):
    return lax.dot_general(a, b, (((0,), (0,)), ((), ())), preferred_element_type=F32)


def _sigmoid(x):
    return 0.5 * (1.0 + jnp.tanh(0.5 * x))


def _rms_stats(xf):
    r = lax.rsqrt(jnp.mean(xf * xf, axis=-1, keepdims=True) + RMS_EPS)
    return r, xf * r


def _rms_bwd(xf, g, dy):
    r, xh = _rms_stats(xf)
    dyg = dy * g
    dx = r * (dyg - xh * jnp.mean(dyg * xh, axis=-1, keepdims=True))
    return dx, jnp.sum(dy * xh, axis=0, keepdims=True)


def _gelu_parts(x):
    cdf = 0.5 * (1.0 + lax.erf(x * (1.0 / math.sqrt(2.0))))
    return cdf


def _gelu(x):
    return x * _gelu_parts(x)


def _gelu_grad(x):
    return _gelu_parts(x) + x * jnp.exp(-0.5 * x * x) * (1.0 / math.sqrt(2.0 * math.pi))


def _resident(shape, index=None):
    index = (0,) * len(shape) if index is None else index
    return pl.BlockSpec(shape, lambda *_: index, pipeline_mode=pl.Buffered(1))


def _norm_matmul(x, g, w3, w_block, *, name, with_h):
    T, D = x.shape
    nj, _, tn = w3.shape
    tm = _tile(T, TM_NORM_MATMUL, SUBLANES_BF16)

    def body(x_ref, g_ref, w_ref, a_ref, *rest):
        h_sc = rest[-1]

        @pl.when(pl.program_id(1) == 0)
        def _():
            _, xh = _rms_stats(x_ref[...])
            h = (xh * g_ref[...]).astype(BF16)
            h_sc[...] = h
            if with_h:
                rest[0][...] = h

        a_ref[...] = _dot(h_sc[...], w_ref[pl.program_id(1)]).astype(BF16)

    out_specs = [pl.BlockSpec((tm, tn), lambda i, j: (i, j))]
    out_shape = [SDS((T, nj * tn), BF16)]
    if with_h:
        out_specs.append(pl.BlockSpec((tm, D), lambda i, j: (i, 0)))
        out_shape.append(SDS((T, D), BF16))
    return pl.pallas_call(
        body, name=name, grid=(T // tm, nj),
        in_specs=[pl.BlockSpec((tm, D), lambda i, j: (i, 0)),
                  pl.BlockSpec((1, D), lambda i, j: (0, 0)),
                  _resident((nj, D, tn), (0, w_block, 0))],
        out_specs=out_specs, out_shape=out_shape,
        scratch_shapes=[pltpu.VMEM((tm, D), BF16)],
        compiler_params=_params("parallel", "arbitrary"),
    )(x, g, w3)


def _norm_matmul_t(x, g, wt, *, name):
    T, D = x.shape
    N = wt.shape[0]
    tm = _tile(T, TM_ROW, LANES)

    def body(x_ref, g_ref, w_ref, a_ref, h_ref):
        _, xh = _rms_stats(x_ref[...])
        h = (xh * g_ref[...]).astype(BF16)
        h_ref[...] = h
        a_ref[...] = _dot_nt(w_ref[...], h).astype(BF16)

    return pl.pallas_call(
        body, name=name, grid=(T // tm,),
        in_specs=[pl.BlockSpec((tm, D), lambda i: (i, 0)), pl.BlockSpec((1, D), lambda i: (0, 0)),
                  _resident((N, D))],
        out_specs=[pl.BlockSpec((N, tm), lambda i: (0, i)), pl.BlockSpec((tm, D), lambda i: (i, 0))],
        out_shape=[SDS((N, T), BF16), SDS((T, D), BF16)],
        compiler_params=_params("parallel"),
    )(x, g, wt)


def _matmul_tokens(at, b, *, name):
    K, T = at.shape
    N = b.shape[1]
    tt = _tile(T, TT_REDUCE, LANES)

    def body(a_ref, b_ref, o_ref):
        @pl.when(pl.program_id(0) == 0)
        def _():
            o_ref[...] = jnp.zeros_like(o_ref)

        o_ref[...] += _dot(a_ref[...], b_ref[...])

    return pl.pallas_call(
        body, name=name, grid=(T // tt,),
        in_specs=[pl.BlockSpec((K, tt), lambda t: (0, t)), pl.BlockSpec((tt, N), lambda t: (t, 0))],
        out_specs=pl.BlockSpec((K, N), lambda t: (0, 0)),
        out_shape=SDS((K, N), F32),
        compiler_params=_params("arbitrary"),
    )(at, b)


def _matmul_tn_rows(dat, wt, *, name):
    N, T = dat.shape
    D = wt.shape[1]
    tm = _tile(T, TM_ROW, LANES)

    def body(da_ref, w_ref, o_ref):
        o_ref[...] = _dot_tn(da_ref[...], w_ref[...])

    return pl.pallas_call(
        body, name=name, grid=(T // tm,),
        in_specs=[pl.BlockSpec((N, tm), lambda i: (0, i)), _resident((N, D))],
        out_specs=pl.BlockSpec((tm, D), lambda i: (i, 0)),
        out_shape=SDS((T, D), F32),
        compiler_params=_params("parallel"),
    )(dat, wt)


def _ff_chunk(F):
    return F if F <= 1408 else F // 2


def _swiglu_out(a, w2, x, g_post, *, name):
    T, F2 = a.shape
    F = F2 // 2
    D = x.shape[1]
    tm = _tile(T, TM_ROW, SUBLANES_BF16)
    fc = _ff_chunk(F)

    def body(a_ref, w_ref, x_ref, g_ref, xn_ref, o_ref):
        acc = None
        for c0 in range(0, F, fc):
            gt = a_ref[:, c0:c0 + fc].astype(F32)
            s = (gt * _sigmoid(gt)).astype(BF16) * a_ref[:, F + c0:F + c0 + fc]
            part = _dot(s, w_ref[c0:c0 + fc, :])
            acc = part if acc is None else acc + part
        o_ref[...] = acc.astype(BF16)
        _, oh = _rms_stats(acc)
        xn_ref[...] = x_ref[...] + 0.5 * (oh * g_ref[...])

    return pl.pallas_call(
        body, name=name, grid=(T // tm,),
        in_specs=[pl.BlockSpec((tm, F2), lambda i: (i, 0)),
                  _resident((F, D)),
                  pl.BlockSpec((tm, D), lambda i: (i, 0)),
                  pl.BlockSpec((1, D), lambda i: (0, 0))],
        out_specs=[pl.BlockSpec((tm, D), lambda i: (i, 0)), pl.BlockSpec((tm, D), lambda i: (i, 0))],
        out_shape=[SDS((T, D), F32), SDS((T, D), BF16)],
        compiler_params=_params("parallel"),
    )(a, w2, x, g_post)


def _ffn_bwd_hidden(dy, o, g_post, a, w2, *, name):
    T, F2 = a.shape
    F = F2 // 2
    D = dy.shape[1]
    tm = _tile(T, TM_FFN_BWD, SUBLANES_BF16)
    fc = _tile(F, 256, LANES)

    def body(dy_ref, o_ref, g_ref, a_ref, w_ref, da_ref, s_ref, do_ref, dg_ref):
        @pl.when(pl.program_id(0) == 0)
        def _():
            dg_ref[...] = jnp.zeros_like(dg_ref)

        do, dg = _rms_bwd(o_ref[...].astype(F32), g_ref[...], 0.5 * dy_ref[...])
        dg_ref[...] += dg
        dob = do.astype(BF16)
        do_ref[...] = dob
        for c0 in range(0, F, fc):
            ds = _dot_nt(dob, w_ref[c0:c0 + fc, :]).astype(BF16)
            gt = a_ref[:, c0:c0 + fc].astype(F32)
            ub = a_ref[:, F + c0:F + c0 + fc]
            sg = _sigmoid(gt)
            sl = gt * sg
            dsl = (sg + sl * (1.0 - sg)).astype(BF16)
            sl = sl.astype(BF16)
            s_ref[:, c0:c0 + fc] = sl * ub
            da_ref[:, c0:c0 + fc] = ds * ub * dsl
            da_ref[:, F + c0:F + c0 + fc] = ds * sl

    row = lambda w: pl.BlockSpec((tm, w), lambda i: (i, 0))
    return pl.pallas_call(
        body, name=name, grid=(T // tm,),
        in_specs=[row(D), row(D), pl.BlockSpec((1, D), lambda i: (0, 0)), row(F2),
                  _resident((F, D))],
        out_specs=[row(F2), row(F), row(D), pl.BlockSpec((1, D), lambda i: (0, 0))],
        out_shape=[SDS((T, F2), BF16), SDS((T, F), BF16), SDS((T, D), BF16), SDS((1, D), F32)],
        compiler_params=_params("arbitrary"),
    )(dy, o, g_post, a, w2)


def _dw_call(body, name, grid, in_specs, args, block, pack, row_block, sem):
    if pack is None:
        out_spec = pl.BlockSpec(block, lambda *_: (0, 0, 0), pipeline_mode=pl.Buffered(1))
        return pl.pallas_call(body, name=name, grid=grid, in_specs=in_specs, out_specs=out_spec,
                              out_shape=SDS(block, F32), compiler_params=_params(*sem))(*args)
    assert pack.shape[0] == block[0] and pack.shape[2] == block[2]
    out_spec = pl.BlockSpec(block, lambda *_: (0, row_block, 0), pipeline_mode=pl.Buffered(1))
    return pl.pallas_call(body, name=name, grid=grid, in_specs=in_specs + [ANY], out_specs=out_spec,
                          out_shape=SDS(pack.shape, F32), input_output_aliases={len(args): 0},
                          compiler_params=_params(*sem))(*args, pack)


def _dw_cols(a, b, n_chunks, pack, row_block, *, name):
    T, K = a.shape
    tn = b.shape[1] // n_chunks
    tt = _tile(T, TT_REDUCE, SUBLANES_BF16)

    def body(a_ref, b_ref, *rest):
        o_ref = rest[-1]

        @pl.when(pl.program_id(1) == 0)
        def _():
            o_ref[...] = jnp.zeros_like(o_ref)

        o_ref[0] += _dot_tn(a_ref[...], b_ref[...])

    in_specs = [pl.BlockSpec((tt, K), lambda j, t: (t, 0)), pl.BlockSpec((tt, tn), lambda j, t: (t, j))]
    sem = ("parallel", "arbitrary")
    if pack is None:
        return pl.pallas_call(body, name=name, grid=(n_chunks, T // tt), in_specs=in_specs,
                              out_specs=pl.BlockSpec((1, K, tn), lambda j, t: (j, 0, 0)),
                              out_shape=SDS((n_chunks, K, tn), F32), compiler_params=_params(*sem))(a, b)
    assert pack.shape[0] == n_chunks and pack.shape[2] == tn
    return pl.pallas_call(body, name=name, grid=(n_chunks, T // tt), in_specs=in_specs + [ANY],
                          out_specs=pl.BlockSpec((1, K, tn), lambda j, t: (j, row_block, 0)),
                          out_shape=SDS(pack.shape, F32), input_output_aliases={2: 0},
                          compiler_params=_params(*sem))(a, b, pack)


def _dw_rows(a, b, pack, row_block, *, name, a_feature_major=False):
    K, T = a.shape if a_feature_major else a.shape[::-1]
    N = b.shape[1]
    r = K // N_CHIPS
    cw = r if r % LANES == 0 else 2 * r
    assert cw % LANES == 0 and K % cw == 0 and r % 8 == 0
    tt = _tile(T, TT_REDUCE, LANES)

    def body(a_ref, b_ref, *rest):
        o_ref = rest[-1]

        @pl.when(pl.program_id(0) == 0)
        def _():
            o_ref[...] = jnp.zeros_like(o_ref)

        for c in range(K // cw):
            if a_feature_major:
                part = _dot(a_ref[c * cw:(c + 1) * cw, :], b_ref[...])
            else:
                part = _dot_tn(a_ref[:, c * cw:(c + 1) * cw], b_ref[...])
            for p in range(cw // r):
                o_ref[c * (cw // r) + p] += part[p * r:(p + 1) * r]

    a_spec = pl.BlockSpec((K, tt), lambda t: (0, t)) if a_feature_major else pl.BlockSpec((tt, K), lambda t: (t, 0))
    return _dw_call(body, name, (T // tt,), [a_spec, pl.BlockSpec((tt, N), lambda t: (t, 0))],
                    [a, b], (N_CHIPS, r, N), pack, row_block, ("arbitrary",))


def _matmul_nt_norm_bwd(da, w, w_block, x, g, dy, init, *, name):
    T, N = da.shape
    D = x.shape[1]
    nj, _, tn = w.shape
    tm = _tile(T, TM_ROW, SUBLANES_BF16)
    has_init = init is not None

    def body(da_ref, w_ref, x_ref, g_ref, dy_ref, *rest):
        dx_ref, dg_ref = rest[-2:]

        @pl.when(pl.program_id(0) == 0)
        def _():
            dg_ref[...] = jnp.zeros_like(dg_ref)

        dh = rest[0][...] if has_init else None
        for j in range(nj):
            part = _dot_nt(da_ref[:, j * tn:(j + 1) * tn], w_ref[j])
            dh = part if dh is None else dh + part
        dx, dg = _rms_bwd(x_ref[...], g_ref[...], dh)
        dx_ref[...] = dy_ref[...] + dx
        dg_ref[...] += dg

    row = pl.BlockSpec((tm, D), lambda i: (i, 0))
    vec = pl.BlockSpec((1, D), lambda i: (0, 0))
    in_specs = [pl.BlockSpec((tm, N), lambda i: (i, 0)), _resident((nj, D, tn), (0, w_block, 0)), row, vec, row]
    args = [da, w, x, g, dy]
    if has_init:
        in_specs.append(row)
        args.append(init)
    return pl.pallas_call(
        body, name=name, grid=(T // tm,), in_specs=in_specs,
        out_specs=[row, vec], out_shape=[SDS((T, D), F32), SDS((1, D), F32)],
        compiler_params=_params("arbitrary"),
    )(*args)


GROUP_LANES = Q_PER_KV * ATTN_BLOCK


def _attn_mask_t(first):
    kj = lax.broadcasted_iota(jnp.int32, (2 * ATTN_BLOCK, ATTN_BLOCK), 0)
    qi = lax.broadcasted_iota(jnp.int32, (2 * ATTN_BLOCK, ATTN_BLOCK), 1)
    rel = qi + ATTN_BLOCK - kj
    band = (rel >= 0) & (rel < ATTN_BLOCK)
    if first is False:
        return band
    return band & ((kj >= ATTN_BLOCK) | jnp.logical_not(first))


def _attn_probs_t(st, valid, sink):
    s = jnp.where(valid, st, MASK_VALUE)
    m = jnp.maximum(jnp.max(s, axis=0, keepdims=True), sink)
    p = jnp.exp(s - m)
    es = jnp.exp(sink - m)
    inv = 1.0 / (jnp.sum(p, axis=0, keepdims=True) + es)
    return p * inv, es * inv


def _attn_specs(tq, tile_of):
    nb = tq // ATTN_BLOCK
    krow, vrow = ATTN_WIDTH // KV_WIDTH, ATTN_WIDTH // KV_WIDTH + 1
    halo = lambda r: pl.BlockSpec((KV_WIDTH, ATTN_BLOCK), lambda t: (r, jnp.maximum(tile_of(t) * nb - 1, 0)))
    return [pl.BlockSpec((ATTN_WIDTH, tq), lambda t: (0, tile_of(t))),
            pl.BlockSpec((KV_WIDTH, tq), lambda t: (krow, tile_of(t))),
            pl.BlockSpec((KV_WIDTH, tq), lambda t: (vrow, tile_of(t))),
            halo(krow), halo(vrow)]


def _head_rows(g, r):
    h = g * Q_PER_KV + r
    return h, slice(h * HEAD_DIM, (h + 1) * HEAD_DIM)


def _group_stack(ref, g, cols):
    return jnp.concatenate([ref[_head_rows(g, r)[1], cols] for r in range(Q_PER_KV)], axis=1)


def _attn_fwd(zt, sinks, *, name):
    T = zt.shape[1]
    tq = _tile(T, TQ_ATTN, ATTN_BLOCK)
    nb = tq // ATTN_BLOCK

    def body(q_ref, k_ref, v_ref, kh_ref, vh_ref, s_ref, o_ref, kf, vf, pt):
        kf[:, 0:ATTN_BLOCK] = kh_ref[...]
        kf[:, ATTN_BLOCK:] = k_ref[...]
        vf[:, 0:ATTN_BLOCK] = vh_ref[...]
        vf[:, ATTN_BLOCK:] = v_ref[...]
        for b in range(nb):
            cols = slice(b * ATTN_BLOCK, (b + 1) * ATTN_BLOCK)
            win = slice(b * ATTN_BLOCK, (b + 2) * ATTN_BLOCK)
            valid = _attn_mask_t((pl.program_id(0) == 0) if b == 0 else False)
            for g in range(N_KV_HEADS):
                gr = slice(g * HEAD_DIM, (g + 1) * HEAD_DIM)
                st = _dot_tn(kf[gr, win], _group_stack(q_ref, g, cols)) * ATTN_SCALE
                for r in range(Q_PER_KV):
                    h, _ = _head_rows(g, r)
                    sl = slice(r * ATTN_BLOCK, (r + 1) * ATTN_BLOCK)
                    probs, _ = _attn_probs_t(st[:, sl], valid, s_ref[h])
                    pt[:, sl] = probs.astype(BF16)
                ot = _dot(vf[gr, win], pt[...])
                for r in range(Q_PER_KV):
                    o_ref[_head_rows(g, r)[1], cols] = ot[:, r * ATTN_BLOCK:(r + 1) * ATTN_BLOCK].astype(BF16)

    return pl.pallas_call(
        body, name=name, grid=(T // tq,),
        in_specs=_attn_specs(tq, lambda t: t) + [pl.BlockSpec(memory_space=pltpu.SMEM)],
        out_specs=pl.BlockSpec((ATTN_WIDTH, tq), lambda t: (0, t)),
        out_shape=SDS((ATTN_WIDTH, T), BF16),
        scratch_shapes=[pltpu.VMEM((KV_WIDTH, tq + ATTN_BLOCK), BF16)] * 2
        + [pltpu.VMEM((2 * ATTN_BLOCK, GROUP_LANES), BF16)],
        compiler_params=_params("parallel"),
    )(zt, zt, zt, zt, zt, sinks)


def _attn_bwd(zt, sinks, dot_, *, name):
    T = zt.shape[1]
    tq = _tile(T, TQ_ATTN, ATTN_BLOCK)
    nb = tq // ATTN_BLOCK
    nt = T // tq
    tile_of = lambda t: nt - 1 - t

    def body(q_ref, k_ref, v_ref, kh_ref, vh_ref, do_ref, s_ref, dz_ref, dsink_ref, kf, vf, dkf, dvf, carry, pt, dst):
        t = pl.program_id(0)

        @pl.when(t == 0)
        def _():
            carry[...] = jnp.zeros_like(carry)
            dsink_ref[...] = jnp.zeros_like(dsink_ref)

        kf[:, 0:ATTN_BLOCK] = kh_ref[...]
        kf[:, ATTN_BLOCK:] = k_ref[...]
        vf[:, 0:ATTN_BLOCK] = vh_ref[...]
        vf[:, ATTN_BLOCK:] = v_ref[...]
        dkf[...] = jnp.zeros_like(dkf)
        dvf[...] = jnp.zeros_like(dvf)
        dkf[:, tq:] = carry[0:KV_WIDTH, :]
        dvf[:, tq:] = carry[KV_WIDTH:, :]
        lane = lax.broadcasted_iota(jnp.int32, (1, LANES), 1)
        dsink = jnp.zeros((1, LANES), F32)
        for b in range(nb):
            cols = slice(b * ATTN_BLOCK, (b + 1) * ATTN_BLOCK)
            win = slice(b * ATTN_BLOCK, (b + 2) * ATTN_BLOCK)
            valid = _attn_mask_t((t == nt - 1) if b == 0 else False)
            for g in range(N_KV_HEADS):
                gr = slice(g * HEAD_DIM, (g + 1) * HEAD_DIM)
                kt2, vt2 = kf[gr, win], vf[gr, win]
                qst = _group_stack(q_ref, g, cols)
                dost = _group_stack(do_ref, g, cols)
                st = _dot_tn(kt2, qst) * ATTN_SCALE
                dpt = _dot_tn(vt2, dost)
                for r in range(Q_PER_KV):
                    h, _ = _head_rows(g, r)
                    sl = slice(r * ATTN_BLOCK, (r + 1) * ATTN_BLOCK)
                    probs, psink = _attn_probs_t(st[:, sl], valid, s_ref[h])
                    dp = dpt[:, sl]
                    delta = jnp.sum(probs * dp, axis=0, keepdims=True)
                    pt[:, sl] = probs.astype(BF16)
                    dst[:, sl] = (probs * (dp - delta)).astype(BF16)
                    dsink = dsink + jnp.where(lane == h, -jnp.sum(psink * delta), 0.0)
                dqt = _dot(kt2, dst[...]) * ATTN_SCALE
                for r in range(Q_PER_KV):
                    dz_ref[_head_rows(g, r)[1], cols] = dqt[:, r * ATTN_BLOCK:(r + 1) * ATTN_BLOCK].astype(BF16)
                dkf[gr, win] += _dot_nt(qst, dst[...]) * ATTN_SCALE
                dvf[gr, win] += _dot_nt(dost, pt[...])
        dz_ref[ATTN_WIDTH:ATTN_WIDTH + KV_WIDTH, :] = dkf[:, ATTN_BLOCK:].astype(BF16)
        dz_ref[ATTN_WIDTH + KV_WIDTH:, :] = dvf[:, ATTN_BLOCK:].astype(BF16)
        carry[0:KV_WIDTH, :] = dkf[:, 0:ATTN_BLOCK]
        carry[KV_WIDTH:, :] = dvf[:, 0:ATTN_BLOCK]
        dsink_ref[...] += dsink

    return pl.pallas_call(
        body, name=name, grid=(nt,),
        in_specs=_attn_specs(tq, tile_of) + [pl.BlockSpec((ATTN_WIDTH, tq), lambda t: (0, tile_of(t))),
                                             pl.BlockSpec(memory_space=pltpu.SMEM)],
        out_specs=[pl.BlockSpec((QKV_WIDTH, tq), lambda t: (0, tile_of(t))),
                   pl.BlockSpec((8, LANES), lambda t: (0, 0))],
        out_shape=[SDS((QKV_WIDTH, T), BF16), SDS((8, LANES), F32)],
        scratch_shapes=[pltpu.VMEM((KV_WIDTH, tq + ATTN_BLOCK), BF16)] * 2
        + [pltpu.VMEM((KV_WIDTH, tq + ATTN_BLOCK), F32)] * 2 + [pltpu.VMEM((2 * KV_WIDTH, ATTN_BLOCK), F32)]
        + [pltpu.VMEM((2 * ATTN_BLOCK, GROUP_LANES), BF16)] * 2,
        compiler_params=_params("arbitrary"),
    )(zt, zt, zt, zt, zt, dot_, sinks)


def _layer_norm_stats(v):
    mu = jnp.mean(v, axis=-1, keepdims=True)
    xc = v - mu
    rstd = lax.rsqrt(jnp.mean(xc * xc, axis=-1, keepdims=True) + LN_EPS)
    return rstd, xc * rstd


def _sgu_fwd(zmain, ln_g, ln_b, wm, bias, *, name):
    T = zmain.shape[0]
    ts = _tile(T, TS_SGU, SGU_CHUNK)

    def body(u_ref, v_ref, g_ref, b_ref, w_ref, bias_ref, y_ref):
        u = _gelu(u_ref[...].astype(F32))
        _, vh = _layer_norm_stats(_gelu(v_ref[...].astype(F32)))
        vn = (vh * g_ref[...] + b_ref[...]).astype(BF16)
        for ch in range(ts // SGU_CHUNK):
            rows = slice(ch * SGU_CHUNK, (ch + 1) * SGU_CHUNK)
            for g in range(SGU_GROUPS):
                cols = slice(g * 128, (g + 1) * 128)
                s = _dot(w_ref[g], vn[rows, cols]) + bias_ref[g]
                y_ref[rows, cols] = (u[rows, cols] * s).astype(BF16)

    full = _resident
    return pl.pallas_call(
        body, name=name, grid=(T // ts,),
        in_specs=[pl.BlockSpec((ts, SGU_WIDTH), lambda i: (i, 0)), pl.BlockSpec((ts, SGU_WIDTH), lambda i: (i, 1)),
                  full((1, SGU_WIDTH)), full((1, SGU_WIDTH)), full(wm.shape), full(bias.shape)],
        out_specs=pl.BlockSpec((ts, SGU_WIDTH), lambda i: (i, 0)),
        out_shape=SDS((T, SGU_WIDTH), BF16),
        compiler_params=_params("parallel"),
    )(zmain, zmain, ln_g, ln_b, wm, bias)


def _sgu_bwd(zmain, dzmain, dy, ln_g, ln_b, wm, wmt, bias, *, name):
    T = zmain.shape[0]
    ts = _tile(T, TS_SGU, SGU_CHUNK)

    def body(u_ref, v_ref, dy_ref, g_ref, b_ref, w_ref, wt_ref, bias_ref, _, dz_ref, dw_ref, db_ref, dlg_ref, dlb_ref,
             dvn):
        @pl.when(pl.program_id(0) == 0)
        def _():
            dw_ref[...] = jnp.zeros_like(dw_ref)
            db_ref[...] = jnp.zeros_like(db_ref)
            dlg_ref[...] = jnp.zeros_like(dlg_ref)
            dlb_ref[...] = jnp.zeros_like(dlb_ref)

        us = u_ref[...].astype(F32)
        vs = v_ref[...].astype(F32)
        u = _gelu(us)
        rstd, vh = _layer_norm_stats(_gelu(vs))
        vn = (vh * g_ref[...] + b_ref[...]).astype(BF16)
        causal = (lax.broadcasted_iota(jnp.int32, (SGU_CHUNK, SGU_CHUNK), 0)
                  >= lax.broadcasted_iota(jnp.int32, (SGU_CHUNK, SGU_CHUNK), 1))
        lane = lax.broadcasted_iota(jnp.int32, (SGU_CHUNK, LANES), 1)
        db = jnp.zeros((SGU_CHUNK, LANES), F32)
        for ch in range(ts // SGU_CHUNK):
            rows = slice(ch * SGU_CHUNK, (ch + 1) * SGU_CHUNK)
            for g in range(SGU_GROUPS):
                cols = slice(g * 128, (g + 1) * 128)
                vng = vn[rows, cols]
                s = _dot(w_ref[g], vng) + bias_ref[g]
                dyf = dy_ref[rows, cols].astype(F32)
                dz_ref[rows, cols] = (dyf * s * _gelu_grad(us[rows, cols])).astype(BF16)
                dsf = dyf * u[rows, cols]
                dsb = dsf.astype(BF16)
                dvn[rows, cols] = _dot(wt_ref[g], dsb)
                dw_ref[g] += jnp.where(causal, _dot_nt(dsb, vng), 0.0)
                db = db + jnp.where(lane == g, jnp.sum(dsf, axis=1, keepdims=True), 0.0)
        db_ref[...] += db
        dvnf = dvn[...]
        dlg_ref[...] += jnp.sum(dvnf * vh, axis=0, keepdims=True)
        dlb_ref[...] += jnp.sum(dvnf, axis=0, keepdims=True)
        dvh = dvnf * g_ref[...]
        dv = rstd * (dvh - jnp.mean(dvh, axis=-1, keepdims=True) - vh * jnp.mean(dvh * vh, axis=-1, keepdims=True))
        dz_ref[:, SGU_WIDTH:] = (dv * _gelu_grad(vs)).astype(BF16)

    full = _resident
    vec = full((1, SGU_WIDTH))
    acc = lambda shape: pl.BlockSpec(shape, lambda i: (0,) * len(shape))
    return pl.pallas_call(
        body, name=name, grid=(T // ts,),
        in_specs=[pl.BlockSpec((ts, SGU_WIDTH), lambda i: (i, 0)), pl.BlockSpec((ts, SGU_WIDTH), lambda i: (i, 1)),
                  pl.BlockSpec((ts, SGU_WIDTH), lambda i: (i, 0)), vec, vec, full(wm.shape), full(wm.shape),
                  full(bias.shape), pl.BlockSpec(memory_space=pl.ANY)],
        out_specs=[pl.BlockSpec((ts, 2 * SGU_WIDTH), lambda i: (i, 0)), acc(wm.shape),
                   acc((SGU_CHUNK, LANES)), acc((1, SGU_WIDTH)), acc((1, SGU_WIDTH))],
        out_shape=[SDS(dzmain.shape, BF16), SDS(wm.shape, F32), SDS((SGU_CHUNK, LANES), F32),
                   SDS((1, SGU_WIDTH), F32), SDS((1, SGU_WIDTH), F32)],
        scratch_shapes=[pltpu.VMEM((ts, SGU_WIDTH), F32)],
        input_output_aliases={8: 0},
        compiler_params=_params("arbitrary"),
    )(zmain, zmain, dy, ln_g, ln_b, wm, wmt, bias, dzmain)


def _merge_fwd(y_attn_t, y_sgu, zmain, w_a, w_s, w_o, x, g_post, *, name):
    T, D = x.shape
    tm = _tile(T, TM_ROW, LANES)

    def body(ya_ref, ys_ref, ga_ref, gb_ref, wa_ref, ws_ref, wo_ref, x_ref, g_ref, xn_ref, pa_ref, ps_ref, o_ref):
        pa = _dot_tn(ya_ref[...], wa_ref[...])
        ps = _dot(ys_ref[...], ws_ref[...])
        pa_ref[...] = pa.astype(BF16)
        ps_ref[...] = ps.astype(BF16)
        merged = _sigmoid(ga_ref[...].astype(F32)) * pa + _sigmoid(gb_ref[...].astype(F32)) * ps
        out = _dot(merged.astype(BF16), wo_ref[...])
        o_ref[...] = out.astype(BF16)
        _, oh = _rms_stats(out)
        xn_ref[...] = x_ref[...] + oh * g_ref[...]

    row = lambda col: pl.BlockSpec((tm, D), lambda i: (i, col))
    wfull = _resident((D, D))
    return pl.pallas_call(
        body, name=name, grid=(T // tm,),
        in_specs=[pl.BlockSpec((D, tm), lambda i: (0, i)), row(0), row(2), row(3), wfull, wfull, wfull, row(0),
                  pl.BlockSpec((1, D), lambda i: (0, 0))],
        out_specs=[row(0)] * 4,
        out_shape=[SDS((T, D), F32), SDS((T, D), BF16), SDS((T, D), BF16), SDS((T, D), BF16)],
        compiler_params=_params("parallel"),
    )(y_attn_t, y_sgu, zmain, zmain, w_a, w_s, w_o, x, g_post)


def _merge_bwd(dy, out, g_post, pa, ps, zmain, w_a, w_s, w_o, *, name):
    T, D = dy.shape
    tm = _tile(T, TM_ROW, LANES)

    def body(dy_ref, o_ref, g_ref, pa_ref, ps_ref, ga_ref, gb_ref, wa_ref, ws_ref, wo_ref,
             dz_ref, dout_ref, mg_ref, dpa_ref, dps_ref, dya_ref, dys_ref, dg_ref):
        @pl.when(pl.program_id(0) == 0)
        def _():
            dg_ref[...] = jnp.zeros_like(dg_ref)

        dout, dg = _rms_bwd(o_ref[...].astype(F32), g_ref[...], dy_ref[...])
        dg_ref[...] += dg
        doutb = dout.astype(BF16)
        dout_ref[...] = doutb
        dm = _dot_nt(doutb, wo_ref[...])
        pa = pa_ref[...].astype(F32)
        ps = ps_ref[...].astype(F32)
        sa = _sigmoid(ga_ref[...].astype(F32))
        sb = _sigmoid(gb_ref[...].astype(F32))
        mg_ref[...] = (sa * pa + sb * ps).astype(BF16)
        dpa = (dm * sa).astype(BF16)
        dps = (dm * sb).astype(BF16)
        dpa_ref[...] = dpa
        dps_ref[...] = dps
        dz_ref[:, 0:D] = (dm * pa * sa * (1.0 - sa)).astype(BF16)
        dz_ref[:, D:] = (dm * ps * sb * (1.0 - sb)).astype(BF16)
        dya_ref[...] = _dot_nt(wa_ref[...], dpa).astype(BF16)
        dys_ref[...] = _dot_nt(dps, ws_ref[...]).astype(BF16)

    row = lambda col: pl.BlockSpec((tm, D), lambda i: (i, col))
    wfull = _resident((D, D))
    vec = pl.BlockSpec((1, D), lambda i: (0, 0))
    act = SDS((T, D), BF16)
    return pl.pallas_call(
        body, name=name, grid=(T // tm,),
        in_specs=[row(0), row(0), vec, row(0), row(0), row(2), row(3), wfull, wfull, wfull],
        out_specs=[pl.BlockSpec((tm, 2 * D), lambda i: (i, 1))] + [row(0)] * 4
        + [pl.BlockSpec((D, tm), lambda i: (0, i)), row(0), vec],
        out_shape=[SDS(zmain.shape, BF16)] + [act] * 4 + [SDS((D, T), BF16), act, SDS((1, D), F32)],
        compiler_params=_params("arbitrary"),
    )(dy, out, g_post, pa, ps, zmain, zmain, w_a, w_s, w_o)


def _loss_head(y, target, *, name):
    T, D = y.shape
    tm = _tile(T, TM_ROW, 8)

    def body(y_ref, t_ref, dy_ref, l_ref):
        @pl.when(pl.program_id(0) == 0)
        def _():
            l_ref[...] = jnp.zeros_like(l_ref)

        e = y_ref[...] - t_ref[...]
        dy_ref[...] = e * (1.0 / D)
        l_ref[...] += jnp.sum(jnp.mean(e * e, axis=-1, keepdims=True))

    row = pl.BlockSpec((tm, D), lambda i: (i, 0))
    return pl.pallas_call(
        body, name=name, grid=(T // tm,), in_specs=[row, row],
        out_specs=[row, pl.BlockSpec((8, LANES), lambda i: (0, 0))],
        out_shape=[SDS((T, D), F32), SDS((8, LANES), F32)],
        compiler_params=_params("arbitrary"),
    )(y, target)


def _adamw(w, g, m, v, *, name):
    shape = w.shape
    cols = shape[-1]
    rows = w.size // cols
    w2, g2, m2, v2 = (t.reshape(rows, cols) for t in (w, g, m, v))
    tr = _tile(rows, max(8, (256 * 1024) // cols // 8 * 8), 8)

    def body(w_ref, g_ref, m_ref, v_ref, d_ref, nm_ref, nv_ref):
        gg = g_ref[...]
        nm = ADAM_B1 * m_ref[...] + (1.0 - ADAM_B1) * gg
        nv = ADAM_B2 * v_ref[...] + (1.0 - ADAM_B2) * (gg * gg)
        m_hat = nm / (1.0 - ADAM_B1 ** ADAM_STEP)
        v_hat = nv / (1.0 - ADAM_B2 ** ADAM_STEP)
        d_ref[...] = -ADAM_LR * (m_hat / (jnp.sqrt(v_hat) + ADAM_EPS) + ADAM_WD * w_ref[...])
        nm_ref[...] = nm
        nv_ref[...] = nv

    blk = pl.BlockSpec((tr, cols), lambda i: (i, 0))
    outs = pl.pallas_call(
        body, name=name, grid=(rows // tr,), in_specs=[blk] * 4, out_specs=[blk] * 3,
        out_shape=[SDS((rows, cols), F32)] * 3, compiler_params=_params("parallel"),
    )(w2, g2, m2, v2)
    return tuple(o.reshape(shape) for o in outs)


def _sum_terms(terms, n_rows, n_lead, dtypes, *, name):
    cols = terms[0][0].shape[-1]
    tr = _tile(n_rows, 704 if len(terms) <= 4 else 256, SUBLANES_BF16)
    nblk = n_rows // tr
    n_out = len(dtypes)

    def body(*refs):
        acc = refs[0][...].astype(F32)
        for r in refs[1:-n_out]:
            acc = acc + r[...].astype(F32)
        for o_ref in refs[-n_out:]:
            o_ref[...] = acc.astype(o_ref.dtype)

    def spec(lead, first):
        return pl.BlockSpec((1, tr, cols), lambda a, i: (a if lead is None else lead(), first() * nblk + i, 0))

    out = pl.BlockSpec((1, tr, cols), lambda a, i: (a, i, 0))
    return pl.pallas_call(
        body, name=name, grid=(n_lead, nblk), in_specs=[spec(lead, first) for _, lead, first in terms],
        out_specs=[out] * n_out, out_shape=[SDS((n_lead, n_rows, cols), d) for d in dtypes],
        compiler_params=_params("arbitrary", "arbitrary"),
    )(*[a for a, _, _ in terms])


def _position():
    x, y, c = (lax.axis_index(a) for a in AXES)
    chips = [(1 - x, y), (x, 1 - y), (1 - x, 1 - y)]
    return x, y, c, chips


ANY = pl.BlockSpec(memory_space=pl.ANY)


def _remote(src, dst, send_sems, recv_sems, k, to):
    return pltpu.make_async_remote_copy(src_ref=src, dst_ref=dst, send_sem=send_sems.at[k], recv_sem=recv_sems.at[k],
                                        device_id=to, device_id_type=MESH)


def _comm_call(body, arrays, out_shapes, n_sems, *, name):
    n = len(arrays)

    def wrapped(*refs):
        body(refs[:n], refs[n:n + len(out_shapes)], refs[-2], refs[-1])

    return pl.pallas_call(
        wrapped, name=name, in_specs=[ANY] * n, out_specs=[ANY] * len(out_shapes), out_shape=out_shapes,
        scratch_shapes=[pltpu.SemaphoreType.DMA((n_sems,)), pltpu.SemaphoreType.DMA((n_sems,))],
    )(*arrays)


def _gather_shards(packs, *, name):
    def body(p_refs, o_refs, send_sems, recv_sems):
        x, y, c, chips = _position()
        sent, passed = [], []
        for a, (p_ref, o_ref) in enumerate(zip(p_refs, o_refs)):
            mine = pl.ds(c * (p_ref.shape[0] // 2), p_ref.shape[0] // 2)
            sent += [_remote(p_ref.at[mine], o_ref.at[2 * x + y, mine], send_sems, recv_sems, 6 * a + j, (*chip, c))
                     for j, chip in enumerate(chips)]
        for cp in sent:
            cp.start()
        for a, o_ref in enumerate(o_refs):
            mine = pl.ds(c * (o_ref.shape[1] // 2), o_ref.shape[1] // 2)
            for j, (cx, cy) in enumerate(chips):
                landed = o_ref.at[2 * cx + cy, mine]
                _remote(landed, landed, send_sems, recv_sems, 6 * a + j, (x, y, c)).wait_recv()
                cp = _remote(landed, landed, send_sems, recv_sems, 6 * a + 3 + j, (x, y, 1 - c))
                cp.start()
                passed.append(cp)
        for a, o_ref in enumerate(o_refs):
            other = pl.ds((1 - c) * (o_ref.shape[1] // 2), o_ref.shape[1] // 2)
            for j, (cx, cy) in enumerate(chips):
                theirs = o_ref.at[2 * cx + cy, other]
                _remote(theirs, theirs, send_sems, recv_sems, 6 * a + 3 + j, (x, y, c)).wait_recv()
        for cp in sent + passed:
            cp.wait_send()

    return _comm_call(body, packs, [SDS((N_CHIPS,) + p.shape, p.dtype) for p in packs], 6 * len(packs), name=name)


def _sibling_exchange(gs, *, name):
    def body(g_refs, o_refs, send_sems, recv_sems):
        x, y, c, _ = _position()
        sent = [_remote(g_ref.at[:, pl.ds((1 - c) * o_ref.shape[1], o_ref.shape[1])], o_ref, send_sems, recv_sems, a,
                        (x, y, 1 - c)) for a, (g_ref, o_ref) in enumerate(zip(g_refs, o_refs))]
        for cp in sent:
            cp.start()
        for cp in sent:
            cp.wait()

    return _comm_call(body, gs, [SDS((N_CHIPS, g.shape[1] // 2, g.shape[2]), g.dtype) for g in gs], len(gs), name=name)


def _chip_scatter(ps, *, name):
    def body(p_refs, o_refs, send_sems, recv_sems):
        x, y, c, chips = _position()
        sent = [_remote(p_ref.at[2 * cx + cy], o_ref.at[j], send_sems, recv_sems, 3 * a + j, (cx, cy, c))
                for a, (p_ref, o_ref) in enumerate(zip(p_refs, o_refs)) for j, (cx, cy) in enumerate(chips)]
        for cp in sent:
            cp.start()
        for cp in sent:
            cp.wait()

    return _comm_call(body, ps, [SDS((N_CHIPS - 1,) + p.shape[1:], p.dtype) for p in ps], 3 * len(ps), name=name)


def _sibling_swap(ts, *, name):
    def body(t_refs, o_refs, send_sems, recv_sems):
        x, y, c, _ = _position()
        sent = [_remote(t_ref, o_ref, send_sems, recv_sems, a, (x, y, 1 - c))
                for a, (t_ref, o_ref) in enumerate(zip(t_refs, o_refs))]
        for cp in sent:
            cp.start()
        for cp in sent:
            cp.wait()

    return _comm_call(body, ts, [SDS(t.shape, t.dtype) for t in ts], len(ts), name=name)


def _gather_all(v, *, name):
    M, C = v.shape

    def body(v_ref, o_ref, send_sems, recv_sems):
        x, y, c, chips = _position()
        slot = lambda px, py, pc: o_ref.at[4 * px + 2 * py + pc]
        first = [_remote(v_ref, slot(x, y, c), send_sems, recv_sems, 0, (x, y, 1 - c))]
        first += [_remote(v_ref, slot(x, y, c), send_sems, recv_sems, 1 + j, (*chip, c)) for j, chip in enumerate(chips)]
        for cp in first:
            cp.start()
        passed = []
        for j, chip in enumerate(chips):
            landed = slot(*chip, c)
            _remote(landed, landed, send_sems, recv_sems, 1 + j, (x, y, c)).wait_recv()
            cp = _remote(landed, landed, send_sems, recv_sems, 4 + j, (x, y, 1 - c))
            cp.start()
            passed.append(cp)
        sib = slot(x, y, 1 - c)
        _remote(sib, sib, send_sems, recv_sems, 0, (x, y, c)).wait_recv()
        for j, chip in enumerate(chips):
            theirs = slot(*chip, 1 - c)
            _remote(theirs, theirs, send_sems, recv_sems, 4 + j, (x, y, c)).wait_recv()
        for cp in first + passed:
            cp.wait_send()

    return pl.pallas_call(
        body, name=name, in_specs=[ANY], out_specs=ANY, out_shape=SDS((N_DEV, M, C), v.dtype),
        scratch_shapes=[pltpu.SemaphoreType.DMA((7,)), pltpu.SemaphoreType.DMA((7,))],
    )(v)


BIG = ("ffn1_w1", "ffn2_w1", "w_in", "ffn1_w2", "ffn2_w2", "w_attn_branch", "w_sgu_branch", "w_out")
COL_SHARDED = ("ffn1_w1", "w_in", "ffn2_w1")
FFN_IN = ("ffn1_w1", "ffn2_w1")
SMALL = ("ffn1_pre_g", "ffn1_post_g", "mix_pre_g", "attn_sinks", "sgu_ln_g", "sgu_ln_b", "sgu_w", "sgu_b",
         "mix_post_g", "ffn2_pre_g", "ffn2_post_g")
WEIGHTS = ("ffn1_pre_g", "ffn1_w1", "ffn1_w2", "ffn1_post_g", "mix_pre_g", "w_in", "attn_sinks", "sgu_ln_g",
           "sgu_ln_b", "sgu_w", "sgu_b", "w_attn_branch", "w_sgu_branch", "w_out", "mix_post_g", "ffn2_pre_g",
           "ffn2_w1", "ffn2_w2", "ffn2_post_g")


def _column_chunks(w, tn):
    return jnp.swapaxes(w.reshape(w.shape[0], w.shape[1] // tn, tn), 0, 1)


def _width_classes(shard_shapes):
    widths = sorted({shard_shapes[n][-1] for n in BIG}, reverse=True)
    return [[n for n in BIG if shard_shapes[n][-1] == w] for w in widths]


def _class_rows(classes, shard_shapes, n_layers):
    where = {}
    for k, names in enumerate(classes):
        off = 0
        for n in names:
            r = shard_shapes[n][0]
            assert off % r == 0
            where[n] = (k, off, r)
            off += n_layers * r
    return where


def _ffn_fwd(x, pre_g, w1, w1_block, w2, post_g, tag):
    a, h = _norm_matmul(x, pre_g, w1, w1_block, name=f"{tag}_up", with_h=True)
    xn, o = _swiglu_out(a, w2, x, post_g, name=f"{tag}_down")
    return xn, (x, h, a, o)


def _ffn_bwd(dy, saved, pre_g, w1, w1_block, w2, post_g, dw1_into, dw2_into, tag):
    x, h, a, o = saved
    da, s, do, d_post = _ffn_bwd_hidden(dy, o, post_g, a, w2, name=f"{tag}_bwd_hidden")
    g2 = _dw_rows(s, do, *dw2_into, name=f"{tag}_dw2")
    dx, d_pre = _matmul_nt_norm_bwd(da, w1, w1_block, x, pre_g, dy, None, name=f"{tag}_bwd_in")
    g1 = _dw_cols(h, da, N_CHIPS, *dw1_into, name=f"{tag}_dw1")
    return dx, g1, g2, d_pre, d_post


def kernel(x, ffn1_pre_g, ffn1_w1, ffn1_w2, ffn1_post_g, mix_pre_g, w_in, attn_sinks, sgu_ln_g, sgu_ln_b, sgu_w, sgu_b, w_attn_branch, w_sgu_branch, w_out, mix_post_g, ffn2_pre_g, ffn2_w1, ffn2_w2, ffn2_post_g, loss_target, m_ffn1_pre_g, m_ffn1_w1, m_ffn1_w2, m_ffn1_post_g, m_mix_pre_g, m_w_in, m_attn_sinks, m_sgu_ln_g, m_sgu_ln_b, m_sgu_w, m_sgu_b, m_w_attn_branch, m_w_sgu_branch, m_w_out, m_mix_post_g, m_ffn2_pre_g, m_ffn2_w1, m_ffn2_w2, m_ffn2_post_g, v_ffn1_pre_g, v_ffn1_w1, v_ffn1_w2, v_ffn1_post_g, v_mix_pre_g, v_w_in, v_attn_sinks, v_sgu_ln_g, v_sgu_ln_b, v_sgu_w, v_sgu_b, v_w_attn_branch, v_w_sgu_branch, v_w_out, v_mix_post_g, v_ffn2_pre_g, v_ffn2_w1, v_ffn2_w2, v_ffn2_post_g):
    given = dict(locals())
    W = {n: given[n] for n in WEIGHTS}
    M = {n: given["m_" + n] for n in WEIGHTS}
    V = {n: given["v_" + n] for n in WEIGHTS}
    L = ffn1_w1.shape[0]
    T, D = x.shape[1], x.shape[2]
    xt = x.reshape(T, D)
    target = loss_target.reshape(T, D)
    assert L % 2 == 0 and D == ATTN_WIDTH == SGU_WIDTH and T % ATTN_BLOCK == 0

    shard_shapes = {n: W[n].shape[1:] for n in BIG}
    classes = _width_classes(shard_shapes)
    my_chip = 2 * lax.axis_index("x") + lax.axis_index("y")
    my_core = lax.axis_index("c")
    where = _class_rows(classes, shard_shapes, L)
    packs = [jnp.concatenate([W[n].reshape(-1, shard_shapes[n][1]).astype(BF16) for n in names], axis=0)
             for names in classes]
    gathered = _gather_shards(packs, name="gather_weights")
    wcls = [lax.dynamic_update_slice(got, pack[None], (my_chip, 0, 0)) for pack, got in zip(packs, gathered)]

    def block_of(n, l):
        k, off, r = where[n]
        return k, off // r + l

    def chip_shards(n, l):
        k, off, r = where[n]
        return wcls[k][:, off + l * r:off + (l + 1) * r, :]

    full = []
    for l in range(L):
        fw = {n: chip_shards(n, l).reshape(-1, D) for n in BIG if n not in COL_SHARDED}
        w_in_l = jnp.swapaxes(chip_shards("w_in", l), 0, 1).reshape(D, -1)
        fw["w_qkv_t"] = w_in_l[:, :QKV_WIDTH].T
        fw["w_main"] = _column_chunks(w_in_l[:, QKV_WIDTH:], D)
        for n in FFN_IN:
            fw[n] = (wcls[block_of(n, l)[0]], block_of(n, l)[1])
        full.append(fw)

    row = lambda name, l: W[name][l].reshape(1, -1)
    causal = jnp.tril(jnp.ones((SGU_CHUNK, SGU_CHUNK), dtype=bool))
    saved = []
    h_cur = xt
    for l in range(L):
        fw = full[l]
        sv = {}
        h_cur, sv["ffn1"] = _ffn_fwd(h_cur, row("ffn1_pre_g", l), *fw["ffn1_w1"], fw["ffn1_w2"], row("ffn1_post_g", l),
                                     f"l{l}_ffn1")
        zqkv, hm = _norm_matmul_t(h_cur, row("mix_pre_g", l), fw["w_qkv_t"], name=f"l{l}_mix_in_qkv")
        zmain, = _norm_matmul(h_cur, row("mix_pre_g", l), fw["w_main"], 0, name=f"l{l}_mix_in_main", with_h=False)
        wm = jnp.where(causal[None], sgu_w[l], 0.0).astype(BF16)
        wmt = jnp.swapaxes(wm, 1, 2)
        bias = jnp.broadcast_to(sgu_b[l][:, :, None], (SGU_GROUPS, SGU_CHUNK, 128)).astype(F32)
        y_attn = _attn_fwd(zqkv, attn_sinks[l], name=f"l{l}_attn")
        y_sgu = _sgu_fwd(zmain, row("sgu_ln_g", l), row("sgu_ln_b", l), wm, bias, name=f"l{l}_sgu")
        x_mix = h_cur
        h_cur, pa, ps, mo = _merge_fwd(y_attn, y_sgu, zmain, fw["w_attn_branch"], fw["w_sgu_branch"], fw["w_out"],
                                       x_mix, row("mix_post_g", l), name=f"l{l}_merge")
        sv["mix"] = (x_mix, hm, zqkv, zmain, y_attn, y_sgu, pa, ps, mo, wm, wmt, bias)
        h_cur, sv["ffn2"] = _ffn_fwd(h_cur, row("ffn2_pre_g", l), *fw["ffn2_w1"], fw["ffn2_w2"], row("ffn2_post_g", l),
                                     f"l{l}_ffn2")
        saved.append(sv)

    dy, lsum = _loss_head(h_cur, target, name="loss_head")
    loss = lax.psum(0.5 * lsum[0, 0], AXES)

    k_in = where["w_in"][0]
    assert classes[k_in] == ["w_in"]
    gcls = [None if k == k_in else lax.empty((N_CHIPS,) + p.shape, F32) for k, p in enumerate(packs)]
    dw_in = [None] * L
    small_grads = [None] * L

    def into(n, l):
        return gcls[block_of(n, l)[0]], block_of(n, l)[1]

    def ffn_bwd(dy, which, l):
        n1, n2 = f"{which}_w1", f"{which}_w2"
        dy, g1, g2, d_pre, d_post = _ffn_bwd(
            dy, saved[l][which], row(f"{which}_pre_g", l), *full[l][n1], full[l][n2], row(f"{which}_post_g", l),
            into(n1, l), into(n2, l), f"l{l}_{which}")
        gcls[where[n1][0]], gcls[where[n2][0]] = g1, g2
        return dy, d_pre, d_post

    for l in reversed(range(L)):
        fw, sv = full[l], saved[l]
        gs = {}
        dy, gs["ffn2_pre_g"], gs["ffn2_post_g"] = ffn_bwd(dy, "ffn2", l)

        x_mix, hm, zqkv, zmain, y_attn, y_sgu, pa, ps, mo, wm, wmt, bias = sv["mix"]
        dzmain, dout, merged, dpa, dps, dya, dys, gs["mix_post_g"] = _merge_bwd(
            dy, mo, row("mix_post_g", l), pa, ps, zmain, fw["w_attn_branch"], fw["w_sgu_branch"], fw["w_out"],
            name=f"l{l}_merge_bwd")
        k_sq = where["w_out"][0]
        gcls[k_sq] = _dw_rows(merged, dout, *into("w_out", l), name=f"l{l}_dw_out")
        gcls[k_sq] = _dw_rows(y_attn, dpa, *into("w_attn_branch", l), name=f"l{l}_dw_attn", a_feature_major=True)
        gcls[k_sq] = _dw_rows(y_sgu, dps, *into("w_sgu_branch", l), name=f"l{l}_dw_sgu")
        dzqkv, dsink = _attn_bwd(zqkv, attn_sinks[l], dya, name=f"l{l}_attn_bwd")
        gs["attn_sinks"] = dsink[0, :N_Q_HEADS]
        dzmain, dsw, dsb, gs["sgu_ln_g"], gs["sgu_ln_b"] = _sgu_bwd(
            zmain, dzmain, dys, row("sgu_ln_g", l), row("sgu_ln_b", l), wm, wmt, bias, name=f"l{l}_sgu_bwd")
        gs["sgu_w"] = dsw
        gs["sgu_b"] = dsb[:, :SGU_GROUPS].T
        dh_qkv = _matmul_tn_rows(dzqkv, fw["w_qkv_t"], name=f"l{l}_mix_bwd_qkv")
        dy, gs["mix_pre_g"] = _matmul_nt_norm_bwd(dzmain, fw["w_main"], 0, x_mix, row("mix_pre_g", l), dy, dh_qkv,
                                                   name=f"l{l}_mix_bwd_in")
        dw_main = _dw_cols(hm, dzmain, zmain.shape[1] // D, None, 0, name=f"l{l}_dw_in_main")
        dw_in[l] = jnp.concatenate([_matmul_tokens(dzqkv, hm, name=f"l{l}_dw_in_qkv").T,
                                    jnp.swapaxes(dw_main, 0, 1).reshape(D, -1)], axis=1)

        dy, gs["ffn1_pre_g"], gs["ffn1_post_g"] = ffn_bwd(dy, "ffn1", l)
        small_grads[l] = gs
    grad_x = dy.reshape(x.shape)

    w_in_width = shard_shapes["w_in"][1]
    gcls[k_in] = jnp.stack([jnp.concatenate([g[:, s * w_in_width:(s + 1) * w_in_width] for g in dw_in], axis=0)
                            for s in range(N_CHIPS)])
    grs = gcls
    from_sibling = _sibling_exchange(grs, name="grads_sibling_exchange")
    zero = lambda: 0
    core_in_map = lambda: lax.axis_index("c")
    chip_in_map = lambda: 2 * lax.axis_index("x") + lax.axis_index("y")
    pairs = [_sum_terms([(g, None, core_in_map), (fs, None, zero)], fs.shape[1], N_CHIPS, (F32, BF16),
                        name=f"grads_pair_sum{k}") for k, (g, fs) in enumerate(zip(grs, from_sibling))]
    from_chips = _chip_scatter([p[1] for p in pairs], name="grads_chip_scatter")
    halves = [_sum_terms([(p[0], chip_in_map, zero)] + [(fc, (lambda j=j: j), zero) for j in range(N_CHIPS - 1)],
                         fc.shape[1], 1, (F32,), name=f"grads_chip_sum{k}")[0][0]
              for k, (p, fc) in enumerate(zip(pairs, from_chips))]
    others = _sibling_swap(halves, name="grads_sibling_swap")

    grads = {n: [None] * L for n in SMALL}
    for names, half, other in zip(classes, halves, others):
        reduced = jnp.concatenate([jnp.where(my_core == 0, half, other), jnp.where(my_core == 0, other, half)], axis=0)
        for n in names:
            _, off, r = where[n]
            grads[n] = reduced[off:off + L * r].reshape((L,) + shard_shapes[n])

    def small_rows(gs):
        parts = []
        for n in SMALL:
            flat = gs[n].reshape(-1)
            pad = (-flat.shape[0]) % D
            parts.append(jnp.pad(flat, (0, pad)).reshape(-1, D))
        return jnp.concatenate(parts, axis=0)

    spack = jnp.concatenate([small_rows(small_grads[l]) for l in range(L)], axis=0)
    n_small = spack.shape[0]
    pad_rows = (-n_small) % SUBLANES_BF16
    spack = jnp.pad(spack, ((0, pad_rows), (0, 0)))
    everyone = _gather_all(spack, name="small_grads_gather")
    is_me = (jnp.arange(N_DEV) == 2 * my_chip + my_core)[:, None, None]
    everyone = jnp.where(is_me, spack[None], everyone)
    ssum = _sum_terms([(everyone, (lambda d=d: d), zero) for d in range(N_DEV)], spack.shape[0], 1, (F32,),
                      name="small_grads_sum")[0][0]
    per_layer = n_small // L
    for l in range(L):
        r0 = l * per_layer
        for n in SMALL:
            shp = W[n].shape[1:]
            size = math.prod(shp)
            nr = -(-size // D)
            grads[n][l] = ssum[r0:r0 + nr].reshape(-1)[:size].reshape(shp)
            r0 += nr
    grads.update({n: jnp.stack(grads[n]) for n in SMALL})

    delta, new_m, new_v = {}, {}, {}
    for n in WEIGHTS:
        delta[n], new_m[n], new_v[n] = _adamw(W[n], grads[n], M[n], V[n], name=f"adamw_{n}")

    return (loss, grad_x, *[grads[n] for n in WEIGHTS], *[delta[n] for n in WEIGHTS],
            *[new_m[n] for n in WEIGHTS], *[new_v[n] for n in WEIGHTS])
```

```python
import functools
import math

import jax
import jax.numpy as jnp
from jax import lax
from jax.experimental import pallas as pl
from jax.experimental.pallas import tpu as pltpu

F32, BF16 = jnp.float32, jnp.bfloat16
SDS = jax.ShapeDtypeStruct
MESH = pl.DeviceIdType.MESH
AXES = ("x", "y", "c")

HEAD_DIM = 64
N_Q_HEADS = 16
N_KV_HEADS = 2
Q_PER_KV = N_Q_HEADS // N_KV_HEADS
ATTN_WIDTH = N_Q_HEADS * HEAD_DIM
KV_WIDTH = N_KV_HEADS * HEAD_DIM
ATTN_BLOCK = 128
SGU_CHUNK = 128
SGU_GROUPS = 8
SGU_WIDTH = SGU_GROUPS * 128
QKV_WIDTH = ATTN_WIDTH + 2 * KV_WIDTH
RMS_EPS = 1e-6
LN_EPS = 1e-5
MASK_VALUE = -1e30
ATTN_SCALE = 1.0 / math.sqrt(HEAD_DIM)

ADAM_LR, ADAM_B1, ADAM_B2, ADAM_EPS, ADAM_WD, ADAM_STEP = 0.001, 0.9, 0.999, 1e-08, 0.01, 10

N_CHIPS = 4
N_DEV = 8

VMEM_LIMIT_BYTES = 56 * 1024 * 1024
LANES = 128
SUBLANES_BF16 = 16

TM_NORM_MATMUL = 1024
TM_ROW = 512
TM_FFN_BWD = 512
TT_REDUCE = 1024
TQ_ATTN = 512
TS_SGU = 512


def _tile(n, pref, mult):
    t = (min(pref, n) // mult) * mult
    while t >= mult:
        if n % t == 0:
            return t
        t -= mult
    return n


def _params(*sem):
    return pltpu.CompilerParams(dimension_semantics=sem, vmem_limit_bytes=VMEM_LIMIT_BYTES)


def _dot(a, b):
    return jnp.dot(a, b, preferred_element_type=F32)


def _dot_nt(a, b):
    return lax.dot_general(a, b, (((1,), (1,)), ((), ())), preferred_element_type=F32)


def _dot_tn(a, b):
    return lax.dot_general(a, b, (((0,), (0,)), ((), ())), preferred_element_type=F32)


def _sigmoid(x):
    return 0.5 * (1.0 + jnp.tanh(0.5 * x))


def _rms_stats(xf):
    r = lax.rsqrt(jnp.mean(xf * xf, axis=-1, keepdims=True) + RMS_EPS)
    return r, xf * r


def _rms_bwd(xf, g, dy):
    r, xh = _rms_stats(xf)
    dyg = dy * g
    dx = r * (dyg - xh * jnp.mean(dyg * xh, axis=-1, keepdims=True))
    return dx, jnp.sum(dy * xh, axis=0, keepdims=True)


def _gelu_parts(x):
    cdf = 0.5 * (1.0 + lax.erf(x * (1.0 / math.sqrt(2.0))))
    return cdf


def _gelu(x):
    return x * _gelu_parts(x)


def _gelu_grad(x):
    return _gelu_parts(x) + x * jnp.exp(-0.5 * x * x) * (1.0 / math.sqrt(2.0 * math.pi))


def _resident(shape, index=None):
    index = (0,) * len(shape) if index is None else index
    return pl.BlockSpec(shape, lambda *_: index, pipeline_mode=pl.Buffered(1))


def _norm_matmul(x, g, w3, w_block, *, name, with_h):
    T, D = x.shape
    nj, _, tn = w3.shape
    tm = _tile(T, TM_NORM_MATMUL, SUBLANES_BF16)

    def body(x_ref, g_ref, w_ref, a_ref, *rest):
        h_sc = rest[-1]

        @pl.when(pl.program_id(1) == 0)
        def _():
            _, xh = _rms_stats(x_ref[...])
            h = (xh * g_ref[...]).astype(BF16)
            h_sc[...] = h
            if with_h:
                rest[0][...] = h

        a_ref[...] = _dot(h_sc[...], w_ref[pl.program_id(1)]).astype(BF16)

    out_specs = [pl.BlockSpec((tm, tn), lambda i, j: (i, j))]
    out_shape = [SDS((T, nj * tn), BF16)]
    if with_h:
        out_specs.append(pl.BlockSpec((tm, D), lambda i, j: (i, 0)))
        out_shape.append(SDS((T, D), BF16))
    return pl.pallas_call(
        body, name=name, grid=(T // tm, nj),
        in_specs=[pl.BlockSpec((tm, D), lambda i, j: (i, 0)),
                  pl.BlockSpec((1, D), lambda i, j: (0, 0)),
                  _resident((nj, D, tn), (0, w_block, 0))],
        out_specs=out_specs, out_shape=out_shape,
        scratch_shapes=[pltpu.VMEM((tm, D), BF16)],
        compiler_params=_params("parallel", "arbitrary"),
    )(x, g, w3)


def _norm_matmul_t(x, g, wt, *, name):
    T, D = x.shape
    N = wt.shape[0]
    tm = _tile(T, TM_ROW, LANES)

    def body(x_ref, g_ref, w_ref, a_ref, h_ref):
        _, xh = _rms_stats(x_ref[...])
        h = (xh * g_ref[...]).astype(BF16)
        h_ref[...] = h
        a_ref[...] = _dot_nt(w_ref[...], h).astype(BF16)

    return pl.pallas_call(
        body, name=name, grid=(T // tm,),
        in_specs=[pl.BlockSpec((tm, D), lambda i: (i, 0)), pl.BlockSpec((1, D), lambda i: (0, 0)),
                  _resident((N, D))],
        out_specs=[pl.BlockSpec((N, tm), lambda i: (0, i)), pl.BlockSpec((tm, D), lambda i: (i, 0))],
        out_shape=[SDS((N, T), BF16), SDS((T, D), BF16)],
        compiler_params=_params("parallel"),
    )(x, g, wt)


def _matmul_tokens(at, b, *, name):
    K, T = at.shape
    N = b.shape[1]
    tt = _tile(T, TT_REDUCE, LANES)

    def body(a_ref, b_ref, o_ref):
        @pl.when(pl.program_id(0) == 0)
        def _():
            o_ref[...] = jnp.zeros_like(o_ref)

        o_ref[...] += _dot(a_ref[...], b_ref[...])

    return pl.pallas_call(
        body, name=name, grid=(T // tt,),
        in_specs=[pl.BlockSpec((K, tt), lambda t: (0, t)), pl.BlockSpec((tt, N), lambda t: (t, 0))],
        out_specs=pl.BlockSpec((K, N), lambda t: (0, 0)),
        out_shape=SDS((K, N), F32),
        compiler_params=_params("arbitrary"),
    )(at, b)


def _matmul_tn_rows(dat, wt, *, name):
    N, T = dat.shape
    D = wt.shape[1]
    tm = _tile(T, TM_ROW, LANES)

    def body(da_ref, w_ref, o_ref):
        o_ref[...] = _dot_tn(da_ref[...], w_ref[...])

    return pl.pallas_call(
        body, name=name, grid=(T // tm,),
        in_specs=[pl.BlockSpec((N, tm), lambda i: (0, i)), _resident((N, D))],
        out_specs=pl.BlockSpec((tm, D), lambda i: (i, 0)),
        out_shape=SDS((T, D), F32),
        compiler_params=_params("parallel"),
    )(dat, wt)


def _ff_chunk(F):
    return F if F <= 1408 else F // 2


def _swiglu_out(a, w2, x, g_post, *, name):
    T, F2 = a.shape
    F = F2 // 2
    D = x.shape[1]
    tm = _tile(T, TM_ROW, SUBLANES_BF16)
    fc = _ff_chunk(F)

    def body(a_ref, w_ref, x_ref, g_ref, xn_ref, o_ref):
        acc = None
        for c0 in range(0, F, fc):
            gt = a_ref[:, c0:c0 + fc].astype(F32)
            s = (gt * _sigmoid(gt)).astype(BF16) * a_ref[:, F + c0:F + c0 + fc]
            part = _dot(s, w_ref[c0:c0 + fc, :])
            acc = part if acc is None else acc + part
        o_ref[...] = acc.astype(BF16)
        _, oh = _rms_stats(acc)
        xn_ref[...] = x_ref[...] + 0.5 * (oh * g_ref[...])

    return pl.pallas_call(
        body, name=name, grid=(T // tm,),
        in_specs=[pl.BlockSpec((tm, F2), lambda i: (i, 0)),
                  _resident((F, D)),
                  pl.BlockSpec((tm, D), lambda i: (i, 0)),
                  pl.BlockSpec((1, D), lambda i: (0, 0))],
        out_specs=[pl.BlockSpec((tm, D), lambda i: (i, 0)), pl.BlockSpec((tm, D), lambda i: (i, 0))],
        out_shape=[SDS((T, D), F32), SDS((T, D), BF16)],
        compiler_params=_params("parallel"),
    )(a, w2, x, g_post)


def _ffn_bwd_hidden(dy, o, g_post, a, w2, *, name):
    T, F2 = a.shape
    F = F2 // 2
    D = dy.shape[1]
    tm = _tile(T, TM_FFN_BWD, SUBLANES_BF16)
    fc = _tile(F, 256, LANES)

    def body(dy_ref, o_ref, g_ref, a_ref, w_ref, da_ref, s_ref, do_ref, dg_ref):
        @pl.when(pl.program_id(0) == 0)
        def _():
            dg_ref[...] = jnp.zeros_like(dg_ref)

        do, dg = _rms_bwd(o_ref[...].astype(F32), g_ref[...], 0.5 * dy_ref[...])
        dg_ref[...] += dg
        dob = do.astype(BF16)
        do_ref[...] = dob
        for c0 in range(0, F, fc):
            ds = _dot_nt(dob, w_ref[c0:c0 + fc, :]).astype(BF16)
            gt = a_ref[:, c0:c0 + fc].astype(F32)
            ub = a_ref[:, F + c0:F + c0 + fc]
            sg = _sigmoid(gt)
            sl = gt * sg
            dsl = (sg + sl * (1.0 - sg)).astype(BF16)
            sl = sl.astype(BF16)
            s_ref[:, c0:c0 + fc] = sl * ub
            da_ref[:, c0:c0 + fc] = ds * ub * dsl
            da_ref[:, F + c0:F + c0 + fc] = ds * sl

    row = lambda w: pl.BlockSpec((tm, w), lambda i: (i, 0))
    return pl.pallas_call(
        body, name=name, grid=(T // tm,),
        in_specs=[row(D), row(D), pl.BlockSpec((1, D), lambda i: (0, 0)), row(F2),
                  _resident((F, D))],
        out_specs=[row(F2), row(F), row(D), pl.BlockSpec((1, D), lambda i: (0, 0))],
        out_shape=[SDS((T, F2), BF16), SDS((T, F), BF16), SDS((T, D), BF16), SDS((1, D), F32)],
        compiler_params=_params("arbitrary"),
    )(dy, o, g_post, a, w2)


def _dw_call(body, name, grid, in_specs, args, block, pack, row_block, sem):
    if pack is None:
        out_spec = pl.BlockSpec(block, lambda *_: (0, 0, 0), pipeline_mode=pl.Buffered(1))
        return pl.pallas_call(body, name=name, grid=grid, in_specs=in_specs, out_specs=out_spec,
                              out_shape=SDS(block, F32), compiler_params=_params(*sem))(*args)
    assert pack.shape[0] == block[0] and pack.shape[2] == block[2]
    out_spec = pl.BlockSpec(block, lambda *_: (0, row_block, 0), pipeline_mode=pl.Buffered(1))
    return pl.pallas_call(body, name=name, grid=grid, in_specs=in_specs + [ANY], out_specs=out_spec,
                          out_shape=SDS(pack.shape, F32), input_output_aliases={len(args): 0},
                          compiler_params=_params(*sem))(*args, pack)


def _dw_cols(a, b, n_chunks, pack, row_block, *, name):
    T, K = a.shape
    tn = b.shape[1] // n_chunks
    tt = _tile(T, TT_REDUCE, SUBLANES_BF16)

    def body(a_ref, b_ref, *rest):
        o_ref = rest[-1]

        @pl.when(pl.program_id(1) == 0)
        def _():
            o_ref[...] = jnp.zeros_like(o_ref)

        o_ref[0] += _dot_tn(a_ref[...], b_ref[...])

    in_specs = [pl.BlockSpec((tt, K), lambda j, t: (t, 0)), pl.BlockSpec((tt, tn), lambda j, t: (t, j))]
    sem = ("parallel", "arbitrary")
    if pack is None:
        return pl.pallas_call(body, name=name, grid=(n_chunks, T // tt), in_specs=in_specs,
                              out_specs=pl.BlockSpec((1, K, tn), lambda j, t: (j, 0, 0)),
                              out_shape=SDS((n_chunks, K, tn), F32), compiler_params=_params(*sem))(a, b)
    assert pack.shape[0] == n_chunks and pack.shape[2] == tn
    return pl.pallas_call(body, name=name, grid=(n_chunks, T // tt), in_specs=in_specs + [ANY],
                          out_specs=pl.BlockSpec((1, K, tn), lambda j, t: (j, row_block, 0)),
                          out_shape=SDS(pack.shape, F32), input_output_aliases={2: 0},
                          compiler_params=_params(*sem))(a, b, pack)


def _dw_rows(a, b, pack, row_block, *, name, a_feature_major=False):
    K, T = a.shape if a_feature_major else a.shape[::-1]
    N = b.shape[1]
    r = K // N_CHIPS
    cw = r if r % LANES == 0 else 2 * r
    assert cw % LANES == 0 and K % cw == 0 and r % 8 == 0
    tt = _tile(T, TT_REDUCE, LANES)

    def body(a_ref, b_ref, *rest):
        o_ref = rest[-1]

        @pl.when(pl.program_id(0) == 0)
        def _():
            o_ref[...] = jnp.zeros_like(o_ref)

        for c in range(K // cw):
            if a_feature_major:
                part = _dot(a_ref[c * cw:(c + 1) * cw, :], b_ref[...])
            else:
                part = _dot_tn(a_ref[:, c * cw:(c + 1) * cw], b_ref[...])
            for p in range(cw // r):
                o_ref[c * (cw // r) + p] += part[p * r:(p + 1) * r]

    a_spec = pl.BlockSpec((K, tt), lambda t: (0, t)) if a_feature_major else pl.BlockSpec((tt, K), lambda t: (t, 0))
    return _dw_call(body, name, (T // tt,), [a_spec, pl.BlockSpec((tt, N), lambda t: (t, 0))],
                    [a, b], (N_CHIPS, r, N), pack, row_block, ("arbitrary",))


def _matmul_nt_norm_bwd(da, w, w_block, x, g, dy, init, *, name):
    T, N = da.shape
    D = x.shape[1]
    nj, _, tn = w.shape
    tm = _tile(T, TM_ROW, SUBLANES_BF16)
    has_init = init is not None

    def body(da_ref, w_ref, x_ref, g_ref, dy_ref, *rest):
        dx_ref, dg_ref = rest[-2:]

        @pl.when(pl.program_id(0) == 0)
        def _():
            dg_ref[...] = jnp.zeros_like(dg_ref)

        dh = rest[0][...] if has_init else None
        for j in range(nj):
            part = _dot_nt(da_ref[:, j * tn:(j + 1) * tn], w_ref[j])
            dh = part if dh is None else dh + part
        dx, dg = _rms_bwd(x_ref[...], g_ref[...], dh)
        dx_ref[...] = dy_ref[...] + dx
        dg_ref[...] += dg

    row = pl.BlockSpec((tm, D), lambda i: (i, 0))
    vec = pl.BlockSpec((1, D), lambda i: (0, 0))
    in_specs = [pl.BlockSpec((tm, N), lambda i: (i, 0)), _resident((nj, D, tn), (0, w_block, 0)), row, vec, row]
    args = [da, w, x, g, dy]
    if has_init:
        in_specs.append(row)
        args.append(init)
    return pl.pallas_call(
        body, name=name, grid=(T // tm,), in_specs=in_specs,
        out_specs=[row, vec], out_shape=[SDS((T, D), F32), SDS((1, D), F32)],
        compiler_params=_params("arbitrary"),
    )(*args)


GROUP_LANES = Q_PER_KV * ATTN_BLOCK


def _attn_mask_t(first):
    kj = lax.broadcasted_iota(jnp.int32, (2 * ATTN_BLOCK, ATTN_BLOCK), 0)
    qi = lax.broadcasted_iota(jnp.int32, (2 * ATTN_BLOCK, ATTN_BLOCK), 1)
    rel = qi + ATTN_BLOCK - kj
    band = (rel >= 0) & (rel < ATTN_BLOCK)
    if first is False:
        return band
    return band & ((kj >= ATTN_BLOCK) | jnp.logical_not(first))


def _attn_probs_t(st, valid, sink):
    s = jnp.where(valid, st, MASK_VALUE)
    m = jnp.maximum(jnp.max(s, axis=0, keepdims=True), sink)
    p = jnp.exp(s - m)
    es = jnp.exp(sink - m)
    inv = 1.0 / (jnp.sum(p, axis=0, keepdims=True) + es)
    return p * inv, es * inv


def _attn_specs(tq, tile_of):
    nb = tq // ATTN_BLOCK
    krow, vrow = ATTN_WIDTH // KV_WIDTH, ATTN_WIDTH // KV_WIDTH + 1
    halo = lambda r: pl.BlockSpec((KV_WIDTH, ATTN_BLOCK), lambda t: (r, jnp.maximum(tile_of(t) * nb - 1, 0)))
    return [pl.BlockSpec((ATTN_WIDTH, tq), lambda t: (0, tile_of(t))),
            pl.BlockSpec((KV_WIDTH, tq), lambda t: (krow, tile_of(t))),
            pl.BlockSpec((KV_WIDTH, tq), lambda t: (vrow, tile_of(t))),
            halo(krow), halo(vrow)]


def _head_rows(g, r):
    h = g * Q_PER_KV + r
    return h, slice(h * HEAD_DIM, (h + 1) * HEAD_DIM)


def _group_stack(ref, g, cols):
    return jnp.concatenate([ref[_head_rows(g, r)[1], cols] for r in range(Q_PER_KV)], axis=1)


def _attn_fwd(zt, sinks, *, name):
    T = zt.shape[1]
    tq = _tile(T, TQ_ATTN, ATTN_BLOCK)
    nb = tq // ATTN_BLOCK

    def body(q_ref, k_ref, v_ref, kh_ref, vh_ref, s_ref, o_ref, kf, vf, pt):
        kf[:, 0:ATTN_BLOCK] = kh_ref[...]
        kf[:, ATTN_BLOCK:] = k_ref[...]
        vf[:, 0:ATTN_BLOCK] = vh_ref[...]
        vf[:, ATTN_BLOCK:] = v_ref[...]
        for b in range(nb):
            cols = slice(b * ATTN_BLOCK, (b + 1) * ATTN_BLOCK)
            win = slice(b * ATTN_BLOCK, (b + 2) * ATTN_BLOCK)
            valid = _attn_mask_t((pl.program_id(0) == 0) if b == 0 else False)
            for g in range(N_KV_HEADS):
                gr = slice(g * HEAD_DIM, (g + 1) * HEAD_DIM)
                st = _dot_tn(kf[gr, win], _group_stack(q_ref, g, cols)) * ATTN_SCALE
                for r in range(Q_PER_KV):
                    h, _ = _head_rows(g, r)
                    sl = slice(r * ATTN_BLOCK, (r + 1) * ATTN_BLOCK)
                    probs, _ = _attn_probs_t(st[:, sl], valid, s_ref[h])
                    pt[:, sl] = probs.astype(BF16)
                ot = _dot(vf[gr, win], pt[...])
                for r in range(Q_PER_KV):
                    o_ref[_head_rows(g, r)[1], cols] = ot[:, r * ATTN_BLOCK:(r + 1) * ATTN_BLOCK].astype(BF16)

    return pl.pallas_call(
        body, name=name, grid=(T // tq,),
        in_specs=_attn_specs(tq, lambda t: t) + [pl.BlockSpec(memory_space=pltpu.SMEM)],
        out_specs=pl.BlockSpec((ATTN_WIDTH, tq), lambda t: (0, t)),
        out_shape=SDS((ATTN_WIDTH, T), BF16),
        scratch_shapes=[pltpu.VMEM((KV_WIDTH, tq + ATTN_BLOCK), BF16)] * 2
        + [pltpu.VMEM((2 * ATTN_BLOCK, GROUP_LANES), BF16)],
        compiler_params=_params("parallel"),
    )(zt, zt, zt, zt, zt, sinks)


def _attn_bwd(zt, sinks, dot_, *, name):
    T = zt.shape[1]
    tq = _tile(T, TQ_ATTN, ATTN_BLOCK)
    nb = tq // ATTN_BLOCK
    nt = T // tq
    tile_of = lambda t: nt - 1 - t

    def body(q_ref, k_ref, v_ref, kh_ref, vh_ref, do_ref, s_ref, dz_ref, dsink_ref, kf, vf, dkf, dvf, carry, pt, dst):
        t = pl.program_id(0)

        @pl.when(t == 0)
        def _():
            carry[...] = jnp.zeros_like(carry)
            dsink_ref[...] = jnp.zeros_like(dsink_ref)

        kf[:, 0:ATTN_BLOCK] = kh_ref[...]
        kf[:, ATTN_BLOCK:] = k_ref[...]
        vf[:, 0:ATTN_BLOCK] = vh_ref[...]
        vf[:, ATTN_BLOCK:] = v_ref[...]
        dkf[...] = jnp.zeros_like(dkf)
        dvf[...] = jnp.zeros_like(dvf)
        dkf[:, tq:] = carry[0:KV_WIDTH, :]
        dvf[:, tq:] = carry[KV_WIDTH:, :]
        lane = lax.broadcasted_iota(jnp.int32, (1, LANES), 1)
        dsink = jnp.zeros((1, LANES), F32)
        for b in range(nb):
            cols = slice(b * ATTN_BLOCK, (b + 1) * ATTN_BLOCK)
            win = slice(b * ATTN_BLOCK, (b + 2) * ATTN_BLOCK)
            valid = _attn_mask_t((t == nt - 1) if b == 0 else False)
            for g in range(N_KV_HEADS):
                gr = slice(g * HEAD_DIM, (g + 1) * HEAD_DIM)
                kt2, vt2 = kf[gr, win], vf[gr, win]
                qst = _group_stack(q_ref, g, cols)
                dost = _group_stack(do_ref, g, cols)
                st = _dot_tn(kt2, qst) * ATTN_SCALE
                dpt = _dot_tn(vt2, dost)
                for r in range(Q_PER_KV):
                    h, _ = _head_rows(g, r)
                    sl = slice(r * ATTN_BLOCK, (r + 1) * ATTN_BLOCK)
                    probs, psink = _attn_probs_t(st[:, sl], valid, s_ref[h])
                    dp = dpt[:, sl]
                    delta = jnp.sum(probs * dp, axis=0, keepdims=True)
                    pt[:, sl] = probs.astype(BF16)
                    dst[:, sl] = (probs * (dp - delta)).astype(BF16)
                    dsink = dsink + jnp.where(lane == h, -jnp.sum(psink * delta), 0.0)
                dqt = _dot(kt2, dst[...]) * ATTN_SCALE
                for r in range(Q_PER_KV):
                    dz_ref[_head_rows(g, r)[1], cols] = dqt[:, r * ATTN_BLOCK:(r + 1) * ATTN_BLOCK].astype(BF16)
                dkf[gr, win] += _dot_nt(qst, dst[...]) * ATTN_SCALE
                dvf[gr, win] += _dot_nt(dost, pt[...])
        dz_ref[ATTN_WIDTH:ATTN_WIDTH + KV_WIDTH, :] = dkf[:, ATTN_BLOCK:].astype(BF16)
        dz_ref[ATTN_WIDTH + KV_WIDTH:, :] = dvf[:, ATTN_BLOCK:].astype(BF16)
        carry[0:KV_WIDTH, :] = dkf[:, 0:ATTN_BLOCK]
        carry[KV_WIDTH:, :] = dvf[:, 0:ATTN_BLOCK]
        dsink_ref[...] += dsink

    return pl.pallas_call(
        body, name=name, grid=(nt,),
        in_specs=_attn_specs(tq, tile_of) + [pl.BlockSpec((ATTN_WIDTH, tq), lambda t: (0, tile_of(t))),
                                             pl.BlockSpec(memory_space=pltpu.SMEM)],
        out_specs=[pl.BlockSpec((QKV_WIDTH, tq), lambda t: (0, tile_of(t))),
                   pl.BlockSpec((8, LANES), lambda t: (0, 0))],
        out_shape=[SDS((QKV_WIDTH, T), BF16), SDS((8, LANES), F32)],
        scratch_shapes=[pltpu.VMEM((KV_WIDTH, tq + ATTN_BLOCK), BF16)] * 2
        + [pltpu.VMEM((KV_WIDTH, tq + ATTN_BLOCK), F32)] * 2 + [pltpu.VMEM((2 * KV_WIDTH, ATTN_BLOCK), F32)]
        + [pltpu.VMEM((2 * ATTN_BLOCK, GROUP_LANES), BF16)] * 2,
        compiler_params=_params("arbitrary"),
    )(zt, zt, zt, zt, zt, dot_, sinks)


def _layer_norm_stats(v):
    mu = jnp.mean(v, axis=-1, keepdims=True)
    xc = v - mu
    rstd = lax.rsqrt(jnp.mean(xc * xc, axis=-1, keepdims=True) + LN_EPS)
    return rstd, xc * rstd


def _sgu_fwd(zmain, ln_g, ln_b, wm, bias, *, name):
    T = zmain.shape[0]
    ts = _tile(T, TS_SGU, SGU_CHUNK)

    def body(u_ref, v_ref, g_ref, b_ref, w_ref, bias_ref, y_ref):
        u = _gelu(u_ref[...].astype(F32))
        _, vh = _layer_norm_stats(_gelu(v_ref[...].astype(F32)))
        vn = (vh * g_ref[...] + b_ref[...]).astype(BF16)
        for ch in range(ts // SGU_CHUNK):
            rows = slice(ch * SGU_CHUNK, (ch + 1) * SGU_CHUNK)
            for g in range(SGU_GROUPS):
                cols = slice(g * 128, (g + 1) * 128)
                s = _dot(w_ref[g], vn[rows, cols]) + bias_ref[g]
                y_ref[rows, cols] = (u[rows, cols] * s).astype(BF16)

    full = _resident
    return pl.pallas_call(
        body, name=name, grid=(T // ts,),
        in_specs=[pl.BlockSpec((ts, SGU_WIDTH), lambda i: (i, 0)), pl.BlockSpec((ts, SGU_WIDTH), lambda i: (i, 1)),
                  full((1, SGU_WIDTH)), full((1, SGU_WIDTH)), full(wm.shape), full(bias.shape)],
        out_specs=pl.BlockSpec((ts, SGU_WIDTH), lambda i: (i, 0)),
        out_shape=SDS((T, SGU_WIDTH), BF16),
        compiler_params=_params("parallel"),
    )(zmain, zmain, ln_g, ln_b, wm, bias)


def _sgu_bwd(zmain, dzmain, dy, ln_g, ln_b, wm, wmt, bias, *, name):
    T = zmain.shape[0]
    ts = _tile(T, TS_SGU, SGU_CHUNK)

    def body(u_ref, v_ref, dy_ref, g_ref, b_ref, w_ref, wt_ref, bias_ref, _, dz_ref, dw_ref, db_ref, dlg_ref, dlb_ref,
             dvn):
        @pl.when(pl.program_id(0) == 0)
        def _():
            dw_ref[...] = jnp.zeros_like(dw_ref)
            db_ref[...] = jnp.zeros_like(db_ref)
            dlg_ref[...] = jnp.zeros_like(dlg_ref)
            dlb_ref[...] = jnp.zeros_like(dlb_ref)

        us = u_ref[...].astype(F32)
        vs = v_ref[...].astype(F32)
        u = _gelu(us)
        rstd, vh = _layer_norm_stats(_gelu(vs))
        vn = (vh * g_ref[...] + b_ref[...]).astype(BF16)
        causal = (lax.broadcasted_iota(jnp.int32, (SGU_CHUNK, SGU_CHUNK), 0)
                  >= lax.broadcasted_iota(jnp.int32, (SGU_CHUNK, SGU_CHUNK), 1))
        lane = lax.broadcasted_iota(jnp.int32, (SGU_CHUNK, LANES), 1)
        db = jnp.zeros((SGU_CHUNK, LANES), F32)
        for ch in range(ts // SGU_CHUNK):
            rows = slice(ch * SGU_CHUNK, (ch + 1) * SGU_CHUNK)
            for g in range(SGU_GROUPS):
                cols = slice(g * 128, (g + 1) * 128)
                vng = vn[rows, cols]
                s = _dot(w_ref[g], vng) + bias_ref[g]
                dyf = dy_ref[rows, cols].astype(F32)
                dz_ref[rows, cols] = (dyf * s * _gelu_grad(us[rows, cols])).astype(BF16)
                dsf = dyf * u[rows, cols]
                dsb = dsf.astype(BF16)
                dvn[rows, cols] = _dot(wt_ref[g], dsb)
                dw_ref[g] += jnp.where(causal, _dot_nt(dsb, vng), 0.0)
                db = db + jnp.where(lane == g, jnp.sum(dsf, axis=1, keepdims=True), 0.0)
        db_ref[...] += db
        dvnf = dvn[...]
        dlg_ref[...] += jnp.sum(dvnf * vh, axis=0, keepdims=True)
        dlb_ref[...] += jnp.sum(dvnf, axis=0, keepdims=True)
        dvh = dvnf * g_ref[...]
        dv = rstd * (dvh - jnp.mean(dvh, axis=-1, keepdims=True) - vh * jnp.mean(dvh * vh, axis=-1, keepdims=True))
        dz_ref[:, SGU_WIDTH:] = (dv * _gelu_grad(vs)).astype(BF16)

    full = _resident
    vec = full((1, SGU_WIDTH))
    acc = lambda shape: pl.BlockSpec(shape, lambda i: (0,) * len(shape))
    return pl.pallas_call(
        body, name=name, grid=(T // ts,),
        in_specs=[pl.BlockSpec((ts, SGU_WIDTH), lambda i: (i, 0)), pl.BlockSpec((ts, SGU_WIDTH), lambda i: (i, 1)),
                  pl.BlockSpec((ts, SGU_WIDTH), lambda i: (i, 0)), vec, vec, full(wm.shape), full(wm.shape),
                  full(bias.shape), pl.BlockSpec(memory_space=pl.ANY)],
        out_specs=[pl.BlockSpec((ts, 2 * SGU_WIDTH), lambda i: (i, 0)), acc(wm.shape),
                   acc((SGU_CHUNK, LANES)), acc((1, SGU_WIDTH)), acc((1, SGU_WIDTH))],
        out_shape=[SDS(dzmain.shape, BF16), SDS(wm.shape, F32), SDS((SGU_CHUNK, LANES), F32),
                   SDS((1, SGU_WIDTH), F32), SDS((1, SGU_WIDTH), F32)],
        scratch_shapes=[pltpu.VMEM((ts, SGU_WIDTH), F32)],
        input_output_aliases={8: 0},
        compiler_params=_params("arbitrary"),
    )(zmain, zmain, dy, ln_g, ln_b, wm, wmt, bias, dzmain)


def _merge_fwd(y_attn_t, y_sgu, zmain, w_a, w_s, w_o, x, g_post, *, name):
    T, D = x.shape
    tm = _tile(T, TM_ROW, LANES)

    def body(ya_ref, ys_ref, ga_ref, gb_ref, wa_ref, ws_ref, wo_ref, x_ref, g_ref, xn_ref, pa_ref, ps_ref, o_ref):
        pa = _dot_tn(ya_ref[...], wa_ref[...])
        ps = _dot(ys_ref[...], ws_ref[...])
        pa_ref[...] = pa.astype(BF16)
        ps_ref[...] = ps.astype(BF16)
        merged = _sigmoid(ga_ref[...].astype(F32)) * pa + _sigmoid(gb_ref[...].astype(F32)) * ps
        out = _dot(merged.astype(BF16), wo_ref[...])
        o_ref[...] = out.astype(BF16)
        _, oh = _rms_stats(out)
        xn_ref[...] = x_ref[...] + oh * g_ref[...]

    row = lambda col: pl.BlockSpec((tm, D), lambda i: (i, col))
    wfull = _resident((D, D))
    return pl.pallas_call(
        body, name=name, grid=(T // tm,),
        in_specs=[pl.BlockSpec((D, tm), lambda i: (0, i)), row(0), row(2), row(3), wfull, wfull, wfull, row(0),
                  pl.BlockSpec((1, D), lambda i: (0, 0))],
        out_specs=[row(0)] * 4,
        out_shape=[SDS((T, D), F32), SDS((T, D), BF16), SDS((T, D), BF16), SDS((T, D), BF16)],
        compiler_params=_params("parallel"),
    )(y_attn_t, y_sgu, zmain, zmain, w_a, w_s, w_o, x, g_post)


def _merge_bwd(dy, out, g_post, pa, ps, zmain, w_a, w_s, w_o, *, name):
    T, D = dy.shape
    tm = _tile(T, TM_ROW, LANES)

    def body(dy_ref, o_ref, g_ref, pa_ref, ps_ref, ga_ref, gb_ref, wa_ref, ws_ref, wo_ref,
             dz_ref, dout_ref, mg_ref, dpa_ref, dps_ref, dya_ref, dys_ref, dg_ref):
        @pl.when(pl.program_id(0) == 0)
        def _():
            dg_ref[...] = jnp.zeros_like(dg_ref)

        dout, dg = _rms_bwd(o_ref[...].astype(F32), g_ref[...], dy_ref[...])
        dg_ref[...] += dg
        doutb = dout.astype(BF16)
        dout_ref[...] = doutb
        dm = _dot_nt(doutb, wo_ref[...])
        pa = pa_ref[...].astype(F32)
        ps = ps_ref[...].astype(F32)
        sa = _sigmoid(ga_ref[...].astype(F32))
        sb = _sigmoid(gb_ref[...].astype(F32))
        mg_ref[...] = (sa * pa + sb * ps).astype(BF16)
        dpa = (dm * sa).astype(BF16)
        dps = (dm * sb).astype(BF16)
        dpa_ref[...] = dpa
        dps_ref[...] = dps
        dz_ref[:, 0:D] = (dm * pa * sa * (1.0 - sa)).astype(BF16)
        dz_ref[:, D:] = (dm * ps * sb * (1.0 - sb)).astype(BF16)
        dya_ref[...] = _dot_nt(wa_ref[...], dpa).astype(BF16)
        dys_ref[...] = _dot_nt(dps, ws_ref[...]).astype(BF16)

    row = lambda col: pl.BlockSpec((tm, D), lambda i: (i, col))
    wfull = _resident((D, D))
    vec = pl.BlockSpec((1, D), lambda i: (0, 0))
    act = SDS((T, D), BF16)
    return pl.pallas_call(
        body, name=name, grid=(T // tm,),
        in_specs=[row(0), row(0), vec, row(0), row(0), row(2), row(3), wfull, wfull, wfull],
        out_specs=[pl.BlockSpec((tm, 2 * D), lambda i: (i, 1))] + [row(0)] * 4
        + [pl.BlockSpec((D, tm), lambda i: (0, i)), row(0), vec],
        out_shape=[SDS(zmain.shape, BF16)] + [act] * 4 + [SDS((D, T), BF16), act, SDS((1, D), F32)],
        compiler_params=_params("arbitrary"),
    )(dy, out, g_post, pa, ps, zmain, zmain, w_a, w_s, w_o)


def _loss_head(y, target, *, name):
    T, D = y.shape
    tm = _tile(T, TM_ROW, 8)

    def body(y_ref, t_ref, dy_ref, l_ref):
        @pl.when(pl.program_id(0) == 0)
        def _():
            l_ref[...] = jnp.zeros_like(l_ref)

        e = y_ref[...] - t_ref[...]
        dy_ref[...] = e * (1.0 / D)
        l_ref[...] += jnp.sum(jnp.mean(e * e, axis=-1, keepdims=True))

    row = pl.BlockSpec((tm, D), lambda i: (i, 0))
    return pl.pallas_call(
        body, name=name, grid=(T // tm,), in_specs=[row, row],
        out_specs=[row, pl.BlockSpec((8, LANES), lambda i: (0, 0))],
        out_shape=[SDS((T, D), F32), SDS((8, LANES), F32)],
        compiler_params=_params("arbitrary"),
    )(y, target)


def _adamw(w, g, m, v, *, name):
    shape = w.shape
    cols = shape[-1]
    rows = w.size // cols
    w2, g2, m2, v2 = (t.reshape(rows, cols) for t in (w, g, m, v))
    tr = _tile(rows, max(8, (256 * 1024) // cols // 8 * 8), 8)

    def body(w_ref, g_ref, m_ref, v_ref, d_ref, nm_ref, nv_ref):
        gg = g_ref[...]
        nm = ADAM_B1 * m_ref[...] + (1.0 - ADAM_B1) * gg
        nv = ADAM_B2 * v_ref[...] + (1.0 - ADAM_B2) * (gg * gg)
        m_hat = nm / (1.0 - ADAM_B1 ** ADAM_STEP)
        v_hat = nv / (1.0 - ADAM_B2 ** ADAM_STEP)
        d_ref[...] = -ADAM_LR * (m_hat / (jnp.sqrt(v_hat) + ADAM_EPS) + ADAM_WD * w_ref[...])
        nm_ref[...] = nm
        nv_ref[...] = nv

    blk = pl.BlockSpec((tr, cols), lambda i: (i, 0))
    outs = pl.pallas_call(
        body, name=name, grid=(rows // tr,), in_specs=[blk] * 4, out_specs=[blk] * 3,
        out_shape=[SDS((rows, cols), F32)] * 3, compiler_params=_params("parallel"),
    )(w2, g2, m2, v2)
    return tuple(o.reshape(shape) for o in outs)


def _sum_terms(terms, n_rows, n_lead, dtypes, *, name):
    cols = terms[0][0].shape[-1]
    tr = _tile(n_rows, 704 if len(terms) <= 4 else 256, SUBLANES_BF16)
    nblk = n_rows // tr
    n_out = len(dtypes)

    def body(*refs):
        acc = refs[0][...].astype(F32)
        for r in refs[1:-n_out]:
            acc = acc + r[...].astype(F32)
        for o_ref in refs[-n_out:]:
            o_ref[...] = acc.astype(o_ref.dtype)

    def spec(lead, first):
        return pl.BlockSpec((1, tr, cols), lambda a, i: (lead(a), first(a) * nblk + i, 0))

    out = pl.BlockSpec((1, tr, cols), lambda a, i: (a, i, 0))
    return pl.pallas_call(
        body, name=name, grid=(n_lead, nblk), in_specs=[spec(lead, first) for _, lead, first in terms],
        out_specs=[out] * n_out, out_shape=[SDS((n_lead, n_rows, cols), d) for d in dtypes],
        compiler_params=_params("arbitrary", "arbitrary"),
    )(*[a for a, _, _ in terms])


def _position():
    x, y, c = (lax.axis_index(a) for a in AXES)
    chips = [(1 - x, y), (x, 1 - y), (1 - x, 1 - y)]
    return x, y, c, chips


ANY = pl.BlockSpec(memory_space=pl.ANY)


def _remote(src, dst, send_sems, recv_sems, k, to):
    return pltpu.make_async_remote_copy(src_ref=src, dst_ref=dst, send_sem=send_sems.at[k], recv_sem=recv_sems.at[k],
                                        device_id=to, device_id_type=MESH)


def _comm_call(body, arrays, out_shapes, n_sems, *, name):
    n = len(arrays)

    def wrapped(*refs):
        body(refs[:n], refs[n:n + len(out_shapes)], refs[-2], refs[-1])

    return pl.pallas_call(
        wrapped, name=name, in_specs=[ANY] * n, out_specs=[ANY] * len(out_shapes), out_shape=out_shapes,
        scratch_shapes=[pltpu.SemaphoreType.DMA((n_sems,)), pltpu.SemaphoreType.DMA((n_sems,))],
    )(*arrays)


def _gather_shards(packs, *, name):
    NS = 8

    def body(p_refs, o_refs, send_sems, recv_sems):
        x, y, c, _ = _position()
        me, sib = (x, y, c), (x, y, 1 - c)
        xn, yn = (1 - x, y, c), (x, 1 - y, c)
        s_me, s_xn, s_yn, s_dg = 2 * x + y, 2 * (1 - x) + y, 2 * x + 1 - y, 2 * (1 - x) + 1 - y
        copies = []

        def send(src, dst, k, to):
            cp = _remote(src, dst, send_sems, recv_sems, k, to)
            cp.start()
            copies.append(cp)

        def landed(ref, k):
            _remote(ref, ref, send_sems, recv_sems, k, me).wait_recv()
            return ref

        for a, (p_ref, o_ref) in enumerate(zip(p_refs, o_refs)):
            rh = p_ref.shape[0] // 2
            send(p_ref.at[pl.ds(c * rh, rh)], o_ref.at[s_me, pl.ds(c * rh, rh)], NS * a, xn)
            send(p_ref.at[pl.ds(c * rh, rh)], o_ref.at[s_me, pl.ds(c * rh, rh)], NS * a + 1, yn)
        for a, o_ref in enumerate(o_refs):
            rh = o_ref.shape[1] // 2
            rq = rh // 2
            q0, q1 = pl.ds(c * rh, rq), pl.ds(c * rh + rq, rq)
            from_x = landed(o_ref.at[s_xn, pl.ds(c * rh, rh)], NS * a)
            send(o_ref.at[s_xn, q0], o_ref.at[s_xn, q0], NS * a + 2, yn)
            send(from_x, from_x, NS * a + 4, sib)
            from_y = landed(o_ref.at[s_yn, pl.ds(c * rh, rh)], NS * a + 1)
            send(o_ref.at[s_yn, q1], o_ref.at[s_yn, q1], NS * a + 3, xn)
            send(from_y, from_y, NS * a + 5, sib)
        for a, o_ref in enumerate(o_refs):
            rh = o_ref.shape[1] // 2
            rq = rh // 2
            d0 = landed(o_ref.at[s_dg, pl.ds(c * rh, rq)], NS * a + 2)
            send(d0, d0, NS * a + 6, sib)
            d1 = landed(o_ref.at[s_dg, pl.ds(c * rh + rq, rq)], NS * a + 3)
            send(d1, d1, NS * a + 7, sib)
        for a, o_ref in enumerate(o_refs):
            rh = o_ref.shape[1] // 2
            rq = rh // 2
            o = (1 - c) * rh
            landed(o_ref.at[s_xn, pl.ds(o, rh)], NS * a + 4)
            landed(o_ref.at[s_yn, pl.ds(o, rh)], NS * a + 5)
            landed(o_ref.at[s_dg, pl.ds(o, rq)], NS * a + 6)
            landed(o_ref.at[s_dg, pl.ds(o + rq, rq)], NS * a + 7)
        for cp in copies:
            cp.wait_send()

    for p in packs:
        assert p.shape[0] % (4 * SUBLANES_BF16) == 0
    return _comm_call(body, packs, [SDS((N_CHIPS,) + p.shape, p.dtype) for p in packs], NS * len(packs), name=name)


def _sibling_exchange(gs, *, name):
    def body(g_refs, o_refs, send_sems, recv_sems):
        x, y, c, _ = _position()
        sent = [_remote(g_ref.at[:, pl.ds((1 - c) * o_ref.shape[1], o_ref.shape[1])], o_ref, send_sems, recv_sems, a,
                        (x, y, 1 - c)) for a, (g_ref, o_ref) in enumerate(zip(g_refs, o_refs))]
        for cp in sent:
            cp.start()
        for cp in sent:
            cp.wait()

    return _comm_call(body, gs, [SDS((N_CHIPS, g.shape[1] // 2, g.shape[2]), g.dtype) for g in gs], len(gs), name=name)


def _scatter_hop1(ps, *, name):
    def body(p_refs, o_refs, send_sems, recv_sems):
        x, y, c, _ = _position()
        xn, yn = (1 - x, y, c), (x, 1 - y, c)
        s_xn, s_yn, s_dg = 2 * (1 - x) + y, 2 * x + 1 - y, 2 * (1 - x) + 1 - y
        sent = []
        for a, (p_ref, o_ref) in enumerate(zip(p_refs, o_refs)):
            rq = o_ref.shape[1]
            first, second = pl.ds(0, rq), pl.ds(rq, rq)
            sent += [_remote(p_ref.at[s_xn, second], o_ref.at[0], send_sems, recv_sems, 4 * a, xn),
                     _remote(p_ref.at[s_dg, second], o_ref.at[1], send_sems, recv_sems, 4 * a + 1, xn),
                     _remote(p_ref.at[s_yn, first], o_ref.at[2], send_sems, recv_sems, 4 * a + 2, yn),
                     _remote(p_ref.at[s_dg, first], o_ref.at[3], send_sems, recv_sems, 4 * a + 3, yn)]
        for cp in sent:
            cp.start()
        for cp in sent:
            cp.wait()

    return _comm_call(body, ps, [SDS((4, p.shape[1] // 2, p.shape[2]), p.dtype) for p in ps], 4 * len(ps), name=name)


def _scatter_hop2(fs, *, name):
    def body(f_refs, o_refs, send_sems, recv_sems):
        x, y, c, _ = _position()
        sent = []
        for a, (f_ref, o_ref) in enumerate(zip(f_refs, o_refs)):
            sent += [_remote(f_ref.at[0], o_ref.at[0], send_sems, recv_sems, 2 * a, (1 - x, y, c)),
                     _remote(f_ref.at[1], o_ref.at[1], send_sems, recv_sems, 2 * a + 1, (x, 1 - y, c))]
        for cp in sent:
            cp.start()
        for cp in sent:
            cp.wait()

    return _comm_call(body, fs, [SDS(f.shape, f.dtype) for f in fs], 2 * len(fs), name=name)


def _sibling_swap(ts, *, name):
    def body(t_refs, o_refs, send_sems, recv_sems):
        x, y, c, _ = _position()
        sent = [_remote(t_ref, o_ref, send_sems, recv_sems, a, (x, y, 1 - c))
                for a, (t_ref, o_ref) in enumerate(zip(t_refs, o_refs))]
        for cp in sent:
            cp.start()
        for cp in sent:
            cp.wait()

    return _comm_call(body, ts, [SDS(t.shape, t.dtype) for t in ts], len(ts), name=name)


def _gather_all(v, *, name):
    M, C = v.shape

    def body(v_ref, o_ref, send_sems, recv_sems):
        x, y, c, chips = _position()
        slot = lambda px, py, pc: o_ref.at[4 * px + 2 * py + pc]
        first = [_remote(v_ref, slot(x, y, c), send_sems, recv_sems, 0, (x, y, 1 - c))]
        first += [_remote(v_ref, slot(x, y, c), send_sems, recv_sems, 1 + j, (*chip, c)) for j, chip in enumerate(chips)]
        for cp in first:
            cp.start()
        passed = []
        for j, chip in enumerate(chips):
            landed = slot(*chip, c)
            _remote(landed, landed, send_sems, recv_sems, 1 + j, (x, y, c)).wait_recv()
            cp = _remote(landed, landed, send_sems, recv_sems, 4 + j, (x, y, 1 - c))
            cp.start()
            passed.append(cp)
        sib = slot(x, y, 1 - c)
        _remote(sib, sib, send_sems, recv_sems, 0, (x, y, c)).wait_recv()
        for j, chip in enumerate(chips):
            theirs = slot(*chip, 1 - c)
            _remote(theirs, theirs, send_sems, recv_sems, 4 + j, (x, y, c)).wait_recv()
        for cp in first + passed:
            cp.wait_send()

    return pl.pallas_call(
        body, name=name, in_specs=[ANY], out_specs=ANY, out_shape=SDS((N_DEV, M, C), v.dtype),
        scratch_shapes=[pltpu.SemaphoreType.DMA((7,)), pltpu.SemaphoreType.DMA((7,))],
    )(v)


BIG = ("ffn1_w1", "ffn2_w1", "w_in", "ffn1_w2", "ffn2_w2", "w_attn_branch", "w_sgu_branch", "w_out")
COL_SHARDED = ("ffn1_w1", "w_in", "ffn2_w1")
FFN_IN = ("ffn1_w1", "ffn2_w1")
SMALL = ("ffn1_pre_g", "ffn1_post_g", "mix_pre_g", "attn_sinks", "sgu_ln_g", "sgu_ln_b", "sgu_w", "sgu_b",
         "mix_post_g", "ffn2_pre_g", "ffn2_post_g")
WEIGHTS = ("ffn1_pre_g", "ffn1_w1", "ffn1_w2", "ffn1_post_g", "mix_pre_g", "w_in", "attn_sinks", "sgu_ln_g",
           "sgu_ln_b", "sgu_w", "sgu_b", "w_attn_branch", "w_sgu_branch", "w_out", "mix_post_g", "ffn2_pre_g",
           "ffn2_w1", "ffn2_w2", "ffn2_post_g")


def _column_chunks(w, tn):
    return jnp.swapaxes(w.reshape(w.shape[0], w.shape[1] // tn, tn), 0, 1)


def _width_classes(shard_shapes):
    widths = sorted({shard_shapes[n][-1] for n in BIG}, reverse=True)
    return [[n for n in BIG if shard_shapes[n][-1] == w] for w in widths]


def _class_rows(classes, shard_shapes, n_layers):
    where = {}
    for k, names in enumerate(classes):
        off = 0
        for n in names:
            r = shard_shapes[n][0]
            assert off % r == 0
            where[n] = (k, off, r)
            off += n_layers * r
    return where


def _ffn_fwd(x, pre_g, w1, w1_block, w2, post_g, tag):
    a, h = _norm_matmul(x, pre_g, w1, w1_block, name=f"{tag}_up", with_h=True)
    xn, o = _swiglu_out(a, w2, x, post_g, name=f"{tag}_down")
    return xn, (x, h, a, o)


def _ffn_bwd(dy, saved, pre_g, w1, w1_block, w2, post_g, dw1_into, dw2_into, tag):
    x, h, a, o = saved
    da, s, do, d_post = _ffn_bwd_hidden(dy, o, post_g, a, w2, name=f"{tag}_bwd_hidden")
    g2 = _dw_rows(s, do, *dw2_into, name=f"{tag}_dw2")
    dx, d_pre = _matmul_nt_norm_bwd(da, w1, w1_block, x, pre_g, dy, None, name=f"{tag}_bwd_in")
    g1 = _dw_cols(h, da, N_CHIPS, *dw1_into, name=f"{tag}_dw1")
    return dx, g1, g2, d_pre, d_post


def kernel(x, ffn1_pre_g, ffn1_w1, ffn1_w2, ffn1_post_g, mix_pre_g, w_in, attn_sinks, sgu_ln_g, sgu_ln_b, sgu_w, sgu_b, w_attn_branch, w_sgu_branch, w_out, mix_post_g, ffn2_pre_g, ffn2_w1, ffn2_w2, ffn2_post_g, loss_target, m_ffn1_pre_g, m_ffn1_w1, m_ffn1_w2, m_ffn1_post_g, m_mix_pre_g, m_w_in, m_attn_sinks, m_sgu_ln_g, m_sgu_ln_b, m_sgu_w, m_sgu_b, m_w_attn_branch, m_w_sgu_branch, m_w_out, m_mix_post_g, m_ffn2_pre_g, m_ffn2_w1, m_ffn2_w2, m_ffn2_post_g, v_ffn1_pre_g, v_ffn1_w1, v_ffn1_w2, v_ffn1_post_g, v_mix_pre_g, v_w_in, v_attn_sinks, v_sgu_ln_g, v_sgu_ln_b, v_sgu_w, v_sgu_b, v_w_attn_branch, v_w_sgu_branch, v_w_out, v_mix_post_g, v_ffn2_pre_g, v_ffn2_w1, v_ffn2_w2, v_ffn2_post_g):
    given = dict(locals())
    W = {n: given[n] for n in WEIGHTS}
    M = {n: given["m_" + n] for n in WEIGHTS}
    V = {n: given["v_" + n] for n in WEIGHTS}
    L = ffn1_w1.shape[0]
    T, D = x.shape[1], x.shape[2]
    xt = x.reshape(T, D)
    target = loss_target.reshape(T, D)
    assert L % 2 == 0 and D == ATTN_WIDTH == SGU_WIDTH and T % ATTN_BLOCK == 0

    shard_shapes = {n: W[n].shape[1:] for n in BIG}
    classes = _width_classes(shard_shapes)
    my_chip = 2 * lax.axis_index("x") + lax.axis_index("y")
    my_core = lax.axis_index("c")
    where = _class_rows(classes, shard_shapes, L)
    packs = [jnp.concatenate([W[n].reshape(-1, shard_shapes[n][1]).astype(BF16) for n in names], axis=0)
             for names in classes]
    gathered = _gather_shards(packs, name="gather_weights")
    wcls = [lax.dynamic_update_slice(got, pack[None], (my_chip, 0, 0)) for pack, got in zip(packs, gathered)]

    def block_of(n, l):
        k, off, r = where[n]
        return k, off // r + l

    def chip_shards(n, l):
        k, off, r = where[n]
        return wcls[k][:, off + l * r:off + (l + 1) * r, :]

    full = []
    for l in range(L):
        fw = {n: chip_shards(n, l).reshape(-1, D) for n in BIG if n not in COL_SHARDED}
        w_in_l = jnp.swapaxes(chip_shards("w_in", l), 0, 1).reshape(D, -1)
        fw["w_qkv_t"] = w_in_l[:, :QKV_WIDTH].T
        fw["w_main"] = _column_chunks(w_in_l[:, QKV_WIDTH:], D)
        for n in FFN_IN:
            fw[n] = (wcls[block_of(n, l)[0]], block_of(n, l)[1])
        full.append(fw)

    row = lambda name, l: W[name][l].reshape(1, -1)
    causal = jnp.tril(jnp.ones((SGU_CHUNK, SGU_CHUNK), dtype=bool))
    saved = []
    h_cur = xt
    for l in range(L):
        fw = full[l]
        sv = {}
        h_cur, sv["ffn1"] = _ffn_fwd(h_cur, row("ffn1_pre_g", l), *fw["ffn1_w1"], fw["ffn1_w2"], row("ffn1_post_g", l),
                                     f"l{l}_ffn1")
        zqkv, hm = _norm_matmul_t(h_cur, row("mix_pre_g", l), fw["w_qkv_t"], name=f"l{l}_mix_in_qkv")
        zmain, = _norm_matmul(h_cur, row("mix_pre_g", l), fw["w_main"], 0, name=f"l{l}_mix_in_main", with_h=False)
        wm = jnp.where(causal[None], sgu_w[l], 0.0).astype(BF16)
        wmt = jnp.swapaxes(wm, 1, 2)
        bias = jnp.broadcast_to(sgu_b[l][:, :, None], (SGU_GROUPS, SGU_CHUNK, 128)).astype(F32)
        y_attn = _attn_fwd(zqkv, attn_sinks[l], name=f"l{l}_attn")
        y_sgu = _sgu_fwd(zmain, row("sgu_ln_g", l), row("sgu_ln_b", l), wm, bias, name=f"l{l}_sgu")
        x_mix = h_cur
        h_cur, pa, ps, mo = _merge_fwd(y_attn, y_sgu, zmain, fw["w_attn_branch"], fw["w_sgu_branch"], fw["w_out"],
                                       x_mix, row("mix_post_g", l), name=f"l{l}_merge")
        sv["mix"] = (x_mix, hm, zqkv, zmain, y_attn, y_sgu, pa, ps, mo, wm, wmt, bias)
        h_cur, sv["ffn2"] = _ffn_fwd(h_cur, row("ffn2_pre_g", l), *fw["ffn2_w1"], fw["ffn2_w2"], row("ffn2_post_g", l),
                                     f"l{l}_ffn2")
        saved.append(sv)

    dy, lsum = _loss_head(h_cur, target, name="loss_head")
    loss = lax.psum(0.5 * lsum[0, 0], AXES)

    k_in = where["w_in"][0]
    assert classes[k_in] == ["w_in"]
    gcls = [None if k == k_in else lax.empty((N_CHIPS,) + p.shape, F32) for k, p in enumerate(packs)]
    dw_in = [None] * L
    small_grads = [None] * L

    def into(n, l):
        return gcls[block_of(n, l)[0]], block_of(n, l)[1]

    def ffn_bwd(dy, which, l):
        n1, n2 = f"{which}_w1", f"{which}_w2"
        dy, g1, g2, d_pre, d_post = _ffn_bwd(
            dy, saved[l][which], row(f"{which}_pre_g", l), *full[l][n1], full[l][n2], row(f"{which}_post_g", l),
            into(n1, l), into(n2, l), f"l{l}_{which}")
        gcls[where[n1][0]], gcls[where[n2][0]] = g1, g2
        return dy, d_pre, d_post

    for l in reversed(range(L)):
        fw, sv = full[l], saved[l]
        gs = {}
        dy, gs["ffn2_pre_g"], gs["ffn2_post_g"] = ffn_bwd(dy, "ffn2", l)

        x_mix, hm, zqkv, zmain, y_attn, y_sgu, pa, ps, mo, wm, wmt, bias = sv["mix"]
        dzmain, dout, merged, dpa, dps, dya, dys, gs["mix_post_g"] = _merge_bwd(
            dy, mo, row("mix_post_g", l), pa, ps, zmain, fw["w_attn_branch"], fw["w_sgu_branch"], fw["w_out"],
            name=f"l{l}_merge_bwd")
        k_sq = where["w_out"][0]
        gcls[k_sq] = _dw_rows(merged, dout, *into("w_out", l), name=f"l{l}_dw_out")
        gcls[k_sq] = _dw_rows(y_attn, dpa, *into("w_attn_branch", l), name=f"l{l}_dw_attn", a_feature_major=True)
        gcls[k_sq] = _dw_rows(y_sgu, dps, *into("w_sgu_branch", l), name=f"l{l}_dw_sgu")
        dzqkv, dsink = _attn_bwd(zqkv, attn_sinks[l], dya, name=f"l{l}_attn_bwd")
        gs["attn_sinks"] = dsink[0, :N_Q_HEADS]
        dzmain, dsw, dsb, gs["sgu_ln_g"], gs["sgu_ln_b"] = _sgu_bwd(
            zmain, dzmain, dys, row("sgu_ln_g", l), row("sgu_ln_b", l), wm, wmt, bias, name=f"l{l}_sgu_bwd")
        gs["sgu_w"] = dsw
        gs["sgu_b"] = dsb[:, :SGU_GROUPS].T
        dh_qkv = _matmul_tn_rows(dzqkv, fw["w_qkv_t"], name=f"l{l}_mix_bwd_qkv")
        dy, gs["mix_pre_g"] = _matmul_nt_norm_bwd(dzmain, fw["w_main"], 0, x_mix, row("mix_pre_g", l), dy, dh_qkv,
                                                   name=f"l{l}_mix_bwd_in")
        dw_main = _dw_cols(hm, dzmain, zmain.shape[1] // D, None, 0, name=f"l{l}_dw_in_main")
        dw_in[l] = jnp.concatenate([_matmul_tokens(dzqkv, hm, name=f"l{l}_dw_in_qkv").T,
                                    jnp.swapaxes(dw_main, 0, 1).reshape(D, -1)], axis=1)

        dy, gs["ffn1_pre_g"], gs["ffn1_post_g"] = ffn_bwd(dy, "ffn1", l)
        small_grads[l] = gs
    grad_x = dy.reshape(x.shape)

    w_in_width = shard_shapes["w_in"][1]
    gcls[k_in] = jnp.stack([jnp.concatenate([g[:, s * w_in_width:(s + 1) * w_in_width] for g in dw_in], axis=0)
                            for s in range(N_CHIPS)])
    grs = gcls
    from_sibling = _sibling_exchange(grs, name="grads_sibling_exchange")
    zero = lambda a: 0
    own = lambda a: a
    core = lambda a: lax.axis_index("c")
    chip = lambda a: 2 * lax.axis_index("x") + lax.axis_index("y")
    chip_xn = lambda a: 2 * (1 - lax.axis_index("x")) + lax.axis_index("y")
    chip_yn = lambda a: 2 * lax.axis_index("x") + 1 - lax.axis_index("y")
    pairs = [_sum_terms([(g, own, core), (fs, own, zero)], fs.shape[1], N_CHIPS, (F32, BF16),
                        name=f"grads_pair_sum{k}") for k, (g, fs) in enumerate(zip(grs, from_sibling))]
    hop1 = _scatter_hop1([p[1] for p in pairs], name="grads_scatter_hop1")
    relay = [_sum_terms([(p[1], lambda a: chip_xn(a) + a * (chip_yn(a) - chip_xn(a)), own), (h, lambda a: 3 - 2 * a, zero)],
                        h.shape[1], 2, (BF16,), name=f"grads_relay_sum{k}")[0]
             for k, (p, h) in enumerate(zip(pairs, hop1))]
    hop2 = _scatter_hop2(relay, name="grads_scatter_hop2")
    halves = [_sum_terms([(p[0], chip, own), (h1, lambda a: 2 - 2 * a, zero), (h2, own, zero)],
                         h1.shape[1], 2, (F32,), name=f"grads_chip_sum{k}")[0].reshape(p[0].shape[1:])
              for k, (p, h1, h2) in enumerate(zip(pairs, hop1, hop2))]
    others = _sibling_swap(halves, name="grads_sibling_swap")

    grads = {n: [None] * L for n in SMALL}
    for names, half, other in zip(classes, halves, others):
        reduced = jnp.concatenate([jnp.where(my_core == 0, half, other), jnp.where(my_core == 0, other, half)], axis=0)
        for n in names:
            _, off, r = where[n]
            grads[n] = reduced[off:off + L * r].reshape((L,) + shard_shapes[n])

    def small_rows(gs):
        parts = []
        for n in SMALL:
            flat = gs[n].reshape(-1)
            pad = (-flat.shape[0]) % D
            parts.append(jnp.pad(flat, (0, pad)).reshape(-1, D))
        return jnp.concatenate(parts, axis=0)

    spack = jnp.concatenate([small_rows(small_grads[l]) for l in range(L)], axis=0)
    n_small = spack.shape[0]
    pad_rows = (-n_small) % SUBLANES_BF16
    spack = jnp.pad(spack, ((0, pad_rows), (0, 0)))
    everyone = _gather_all(spack, name="small_grads_gather")
    is_me = (jnp.arange(N_DEV) == 2 * my_chip + my_core)[:, None, None]
    everyone = jnp.where(is_me, spack[None], everyone)
    ssum = _sum_terms([(everyone, (lambda a, d=d: d), zero) for d in range(N_DEV)], spack.shape[0], 1, (F32,),
                      name="small_grads_sum")[0][0]
    per_layer = n_small // L
    for l in range(L):
        r0 = l * per_layer
        for n in SMALL:
            shp = W[n].shape[1:]
            size = math.prod(shp)
            nr = -(-size // D)
            grads[n][l] = ssum[r0:r0 + nr].reshape(-1)[:size].reshape(shp)
            r0 += nr
    grads.update({n: jnp.stack(grads[n]) for n in SMALL})

    delta, new_m, new_v = {}, {}, {}
    for n in WEIGHTS:
        delta[n], new_m[n], new_v[n] = _adamw(W[n], grads[n], M[n], V[n], name=f"adamw_{n}")

    return (loss, grad_x, *[grads[n] for n in WEIGHTS], *[delta[n] for n in WEIGHTS],
            *[new_m[n] for n in WEIGHTS], *[new_v[n] for n in WEIGHTS])
```

```python
import functools
import math

import jax
import jax.numpy as jnp
from jax import lax
from jax.experimental import pallas as pl
from jax.experimental.pallas import tpu as pltpu

F32, BF16 = jnp.float32, jnp.bfloat16
SDS = jax.ShapeDtypeStruct
MESH = pl.DeviceIdType.MESH
AXES = ("x", "y", "c")

HEAD_DIM = 64
N_Q_HEADS = 16
N_KV_HEADS = 2
Q_PER_KV = N_Q_HEADS // N_KV_HEADS
ATTN_WIDTH = N_Q_HEADS * HEAD_DIM
KV_WIDTH = N_KV_HEADS * HEAD_DIM
ATTN_BLOCK = 128
SGU_CHUNK = 128
SGU_GROUPS = 8
SGU_WIDTH = SGU_GROUPS * 128
QKV_WIDTH = ATTN_WIDTH + 2 * KV_WIDTH
RMS_EPS = 1e-6
LN_EPS = 1e-5
MASK_VALUE = -1e30
ATTN_SCALE = 1.0 / math.sqrt(HEAD_DIM)

ADAM_LR, ADAM_B1, ADAM_B2, ADAM_EPS, ADAM_WD, ADAM_STEP = 0.001, 0.9, 0.999, 1e-08, 0.01, 10

N_CHIPS = 4
N_DEV = 8

VMEM_LIMIT_BYTES = 56 * 1024 * 1024
LANES = 128
SUBLANES_BF16 = 16

TM_NORM_MATMUL = 1024
TM_ROW = 512
TM_FFN_BWD = 512
TT_REDUCE = 1024
TQ_ATTN = 1024
TM_FEATURE_MAJOR = 1024
TS_SGU = 512


def _tile(n, pref, mult):
    t = (min(pref, n) // mult) * mult
    while t >= mult:
        if n % t == 0:
            return t
        t -= mult
    return n


def _params(*sem):
    return pltpu.CompilerParams(dimension_semantics=sem, vmem_limit_bytes=VMEM_LIMIT_BYTES)


def _dot(a, b):
    return jnp.dot(a, b, preferred_element_type=F32)


def _dot_nt(a, b):
    return lax.dot_general(a, b, (((1,), (1,)), ((), ())), preferred_element_type=F32)


def _dot_tn(a, b):
    return lax.dot_general(a, b, (((0,), (0,)), ((), ())), preferred_element_type=F32)


def _sigmoid(x):
    return 0.5 * (1.0 + jnp.tanh(0.5 * x))


def _rms_stats(xf):
    r = lax.rsqrt(jnp.mean(xf * xf, axis=-1, keepdims=True) + RMS_EPS)
    return r, xf * r


def _rms_bwd(xf, g, dy):
    r, xh = _rms_stats(xf)
    dyg = dy * g
    dx = r * (dyg - xh * jnp.mean(dyg * xh, axis=-1, keepdims=True))
    return dx, jnp.sum(dy * xh, axis=0, keepdims=True)


def _gelu_parts(x):
    cdf = 0.5 * (1.0 + lax.erf(x * (1.0 / math.sqrt(2.0))))
    return cdf


def _gelu(x):
    return x * _gelu_parts(x)


def _gelu_grad(x):
    return _gelu_parts(x) + x * jnp.exp(-0.5 * x * x) * (1.0 / math.sqrt(2.0 * math.pi))


def _resident(shape, index=None):
    index = (0,) * len(shape) if index is None else index
    return pl.BlockSpec(shape, lambda *_: index, pipeline_mode=pl.Buffered(1))


def _norm_matmul(x, g, w3, w_block, *, name, with_h):
    T, D = x.shape
    nj, _, tn = w3.shape
    tm = _tile(T, TM_NORM_MATMUL, SUBLANES_BF16)

    def body(x_ref, g_ref, w_ref, a_ref, *rest):
        h_sc = rest[-1]

        @pl.when(pl.program_id(1) == 0)
        def _():
            _, xh = _rms_stats(x_ref[...])
            h = (xh * g_ref[...]).astype(BF16)
            h_sc[...] = h
            if with_h:
                rest[0][...] = h

        a_ref[...] = _dot(h_sc[...], w_ref[pl.program_id(1)]).astype(BF16)

    out_specs = [pl.BlockSpec((tm, tn), lambda i, j: (i, j))]
    out_shape = [SDS((T, nj * tn), BF16)]
    if with_h:
        out_specs.append(pl.BlockSpec((tm, D), lambda i, j: (i, 0)))
        out_shape.append(SDS((T, D), BF16))
    return pl.pallas_call(
        body, name=name, grid=(T // tm, nj),
        in_specs=[pl.BlockSpec((tm, D), lambda i, j: (i, 0)),
                  pl.BlockSpec((1, D), lambda i, j: (0, 0)),
                  _resident((nj, D, tn), (0, w_block, 0))],
        out_specs=out_specs, out_shape=out_shape,
        scratch_shapes=[pltpu.VMEM((tm, D), BF16)],
        compiler_params=_params("parallel", "arbitrary"),
    )(x, g, w3)


def _norm_matmul_t(x, g, wt, *, name):
    T, D = x.shape
    N = wt.shape[0]
    tm = _tile(T, TM_FEATURE_MAJOR, LANES)

    def body(x_ref, g_ref, w_ref, a_ref, h_ref):
        _, xh = _rms_stats(x_ref[...])
        h = (xh * g_ref[...]).astype(BF16)
        h_ref[...] = h
        a_ref[...] = _dot_nt(w_ref[...], h).astype(BF16)

    return pl.pallas_call(
        body, name=name, grid=(T // tm,),
        in_specs=[pl.BlockSpec((tm, D), lambda i: (i, 0)), pl.BlockSpec((1, D), lambda i: (0, 0)),
                  _resident((N, D))],
        out_specs=[pl.BlockSpec((N, tm), lambda i: (0, i)), pl.BlockSpec((tm, D), lambda i: (i, 0))],
        out_shape=[SDS((N, T), BF16), SDS((T, D), BF16)],
        compiler_params=_params("parallel"),
    )(x, g, wt)


def _matmul_tokens(at, b, *, name):
    K, T = at.shape
    N = b.shape[1]
    tt = _tile(T, TT_REDUCE, LANES)

    def body(a_ref, b_ref, o_ref):
        @pl.when(pl.program_id(0) == 0)
        def _():
            o_ref[...] = jnp.zeros_like(o_ref)

        o_ref[...] += _dot(a_ref[...], b_ref[...])

    return pl.pallas_call(
        body, name=name, grid=(T // tt,),
        in_specs=[pl.BlockSpec((K, tt), lambda t: (0, t)), pl.BlockSpec((tt, N), lambda t: (t, 0))],
        out_specs=pl.BlockSpec((K, N), lambda t: (0, 0)),
        out_shape=SDS((K, N), F32),
        compiler_params=_params("arbitrary"),
    )(at, b)


def _matmul_tn_rows(dat, wt, *, name):
    N, T = dat.shape
    D = wt.shape[1]
    tm = _tile(T, TM_FEATURE_MAJOR, LANES)

    def body(da_ref, w_ref, o_ref):
        o_ref[...] = _dot_tn(da_ref[...], w_ref[...])

    return pl.pallas_call(
        body, name=name, grid=(T // tm,),
        in_specs=[pl.BlockSpec((N, tm), lambda i: (0, i)), _resident((N, D))],
        out_specs=pl.BlockSpec((tm, D), lambda i: (i, 0)),
        out_shape=SDS((T, D), F32),
        compiler_params=_params("parallel"),
    )(dat, wt)


def _ff_chunk(F):
    return F if F <= 1408 else F // 2


def _swiglu_out(a, w2, x, g_post, *, name):
    T, F2 = a.shape
    F = F2 // 2
    D = x.shape[1]
    tm = _tile(T, TM_ROW, SUBLANES_BF16)
    fc = _ff_chunk(F)

    def body(a_ref, w_ref, x_ref, g_ref, xn_ref, o_ref):
        acc = None
        for c0 in range(0, F, fc):
            gt = a_ref[:, c0:c0 + fc].astype(F32)
            s = (gt * _sigmoid(gt)).astype(BF16) * a_ref[:, F + c0:F + c0 + fc]
            part = _dot(s, w_ref[c0:c0 + fc, :])
            acc = part if acc is None else acc + part
        o_ref[...] = acc.astype(BF16)
        _, oh = _rms_stats(acc)
        xn_ref[...] = x_ref[...] + 0.5 * (oh * g_ref[...])

    return pl.pallas_call(
        body, name=name, grid=(T // tm,),
        in_specs=[pl.BlockSpec((tm, F2), lambda i: (i, 0)),
                  _resident((F, D)),
                  pl.BlockSpec((tm, D), lambda i: (i, 0)),
                  pl.BlockSpec((1, D), lambda i: (0, 0))],
        out_specs=[pl.BlockSpec((tm, D), lambda i: (i, 0)), pl.BlockSpec((tm, D), lambda i: (i, 0))],
        out_shape=[SDS((T, D), F32), SDS((T, D), BF16)],
        compiler_params=_params("parallel"),
    )(a, w2, x, g_post)


def _ffn_bwd_hidden(dy, o, g_post, a, w2, *, name):
    T, F2 = a.shape
    F = F2 // 2
    D = dy.shape[1]
    tm = _tile(T, TM_FFN_BWD, SUBLANES_BF16)
    fc = _tile(F, 256, LANES)

    def body(dy_ref, o_ref, g_ref, a_ref, w_ref, da_ref, s_ref, do_ref, dg_ref):
        @pl.when(pl.program_id(0) == 0)
        def _():
            dg_ref[...] = jnp.zeros_like(dg_ref)

        do, dg = _rms_bwd(o_ref[...].astype(F32), g_ref[...], 0.5 * dy_ref[...])
        dg_ref[...] += dg
        dob = do.astype(BF16)
        do_ref[...] = dob
        for c0 in range(0, F, fc):
            ds = _dot_nt(dob, w_ref[c0:c0 + fc, :]).astype(BF16)
            gt = a_ref[:, c0:c0 + fc].astype(F32)
            ub = a_ref[:, F + c0:F + c0 + fc]
            sg = _sigmoid(gt)
            sl = gt * sg
            dsl = (sg + sl * (1.0 - sg)).astype(BF16)
            sl = sl.astype(BF16)
            s_ref[:, c0:c0 + fc] = sl * ub
            da_ref[:, c0:c0 + fc] = ds * ub * dsl
            da_ref[:, F + c0:F + c0 + fc] = ds * sl

    row = lambda w: pl.BlockSpec((tm, w), lambda i: (i, 0))
    return pl.pallas_call(
        body, name=name, grid=(T // tm,),
        in_specs=[row(D), row(D), pl.BlockSpec((1, D), lambda i: (0, 0)), row(F2),
                  _resident((F, D))],
        out_specs=[row(F2), row(F), row(D), pl.BlockSpec((1, D), lambda i: (0, 0))],
        out_shape=[SDS((T, F2), BF16), SDS((T, F), BF16), SDS((T, D), BF16), SDS((1, D), F32)],
        compiler_params=_params("arbitrary"),
    )(dy, o, g_post, a, w2)


def _dw_call(body, name, grid, in_specs, args, block, pack, row_block, sem):
    if pack is None:
        out_spec = pl.BlockSpec(block, lambda *_: (0, 0, 0), pipeline_mode=pl.Buffered(1))
        return pl.pallas_call(body, name=name, grid=grid, in_specs=in_specs, out_specs=out_spec,
                              out_shape=SDS(block, F32), compiler_params=_params(*sem))(*args)
    assert pack.shape[0] == block[0] and pack.shape[2] == block[2]
    out_spec = pl.BlockSpec(block, lambda *_: (0, row_block, 0), pipeline_mode=pl.Buffered(1))
    return pl.pallas_call(body, name=name, grid=grid, in_specs=in_specs + [ANY], out_specs=out_spec,
                          out_shape=SDS(pack.shape, F32), input_output_aliases={len(args): 0},
                          compiler_params=_params(*sem))(*args, pack)


def _dw_cols(a, b, n_chunks, pack, row_block, *, name):
    T, K = a.shape
    tn = b.shape[1] // n_chunks
    tt = _tile(T, TT_REDUCE, SUBLANES_BF16)
    per = 2 if n_chunks % 2 == 0 else 1

    def body(a_ref, b_ref, *rest):
        o_ref = rest[-1]

        @pl.when(pl.program_id(1) == 0)
        def _():
            o_ref[...] = jnp.zeros_like(o_ref)

        for p in range(per):
            o_ref[p] += _dot_tn(a_ref[...], b_ref[:, p * tn:(p + 1) * tn])

    grid = (n_chunks // per, T // tt)
    in_specs = [pl.BlockSpec((tt, K), lambda j, t: (t, 0)), pl.BlockSpec((tt, per * tn), lambda j, t: (t, j))]
    sem = ("parallel", "arbitrary")
    if pack is None:
        return pl.pallas_call(body, name=name, grid=grid, in_specs=in_specs,
                              out_specs=pl.BlockSpec((per, K, tn), lambda j, t: (j, 0, 0)),
                              out_shape=SDS((n_chunks, K, tn), F32), compiler_params=_params(*sem))(a, b)
    assert pack.shape[0] == n_chunks and pack.shape[2] == tn
    return pl.pallas_call(body, name=name, grid=grid, in_specs=in_specs + [ANY],
                          out_specs=pl.BlockSpec((per, K, tn), lambda j, t: (j, row_block, 0)),
                          out_shape=SDS(pack.shape, F32), input_output_aliases={2: 0},
                          compiler_params=_params(*sem))(a, b, pack)


def _dw_rows(a, b, pack, row_block, *, name, a_feature_major=False):
    K, T = a.shape if a_feature_major else a.shape[::-1]
    N = b.shape[1]
    r = K // N_CHIPS
    cw = r if r % LANES == 0 else 2 * r
    assert cw % LANES == 0 and K % cw == 0 and r % 8 == 0
    tt = _tile(T, TT_REDUCE, LANES)

    def body(a_ref, b_ref, *rest):
        o_ref = rest[-1]

        @pl.when(pl.program_id(0) == 0)
        def _():
            o_ref[...] = jnp.zeros_like(o_ref)

        for c in range(K // cw):
            if a_feature_major:
                part = _dot(a_ref[c * cw:(c + 1) * cw, :], b_ref[...])
            else:
                part = _dot_tn(a_ref[:, c * cw:(c + 1) * cw], b_ref[...])
            for p in range(cw // r):
                o_ref[c * (cw // r) + p] += part[p * r:(p + 1) * r]

    a_spec = pl.BlockSpec((K, tt), lambda t: (0, t)) if a_feature_major else pl.BlockSpec((tt, K), lambda t: (t, 0))
    return _dw_call(body, name, (T // tt,), [a_spec, pl.BlockSpec((tt, N), lambda t: (t, 0))],
                    [a, b], (N_CHIPS, r, N), pack, row_block, ("arbitrary",))


def _matmul_nt_norm_bwd(da, w, w_block, x, g, dy, init, *, name):
    T, N = da.shape
    D = x.shape[1]
    nj, _, tn = w.shape
    tm = _tile(T, TM_ROW, SUBLANES_BF16)
    has_init = init is not None

    def body(da_ref, w_ref, x_ref, g_ref, dy_ref, *rest):
        dx_ref, dg_ref = rest[-2:]

        @pl.when(pl.program_id(0) == 0)
        def _():
            dg_ref[...] = jnp.zeros_like(dg_ref)

        dh = rest[0][...] if has_init else None
        for j in range(nj):
            part = _dot_nt(da_ref[:, j * tn:(j + 1) * tn], w_ref[j])
            dh = part if dh is None else dh + part
        dx, dg = _rms_bwd(x_ref[...], g_ref[...], dh)
        dx_ref[...] = dy_ref[...] + dx
        dg_ref[...] += dg

    row = pl.BlockSpec((tm, D), lambda i: (i, 0))
    vec = pl.BlockSpec((1, D), lambda i: (0, 0))
    in_specs = [pl.BlockSpec((tm, N), lambda i: (i, 0)), _resident((nj, D, tn), (0, w_block, 0)), row, vec, row]
    args = [da, w, x, g, dy]
    if has_init:
        in_specs.append(row)
        args.append(init)
    return pl.pallas_call(
        body, name=name, grid=(T // tm,), in_specs=in_specs,
        out_specs=[row, vec], out_shape=[SDS((T, D), F32), SDS((1, D), F32)],
        compiler_params=_params("arbitrary"),
    )(*args)


GROUP_LANES = Q_PER_KV * ATTN_BLOCK


def _attn_mask_t(first):
    kj = lax.broadcasted_iota(jnp.int32, (2 * ATTN_BLOCK, ATTN_BLOCK), 0)
    qi = lax.broadcasted_iota(jnp.int32, (2 * ATTN_BLOCK, ATTN_BLOCK), 1)
    rel = qi + ATTN_BLOCK - kj
    band = (rel >= 0) & (rel < ATTN_BLOCK)
    if first is False:
        return band
    return band & ((kj >= ATTN_BLOCK) | jnp.logical_not(first))


def _attn_probs_t(st, valid, sink):
    s = jnp.where(valid, st, MASK_VALUE)
    m = jnp.maximum(jnp.max(s, axis=0, keepdims=True), sink)
    p = jnp.exp(s - m)
    es = jnp.exp(sink - m)
    inv = 1.0 / (jnp.sum(p, axis=0, keepdims=True) + es)
    return p * inv, es * inv


def _attn_specs(tq, tile_of):
    nb = tq // ATTN_BLOCK
    krow, vrow = ATTN_WIDTH // KV_WIDTH, ATTN_WIDTH // KV_WIDTH + 1
    halo = lambda r: pl.BlockSpec((KV_WIDTH, ATTN_BLOCK), lambda t: (r, jnp.maximum(tile_of(t) * nb - 1, 0)))
    return [pl.BlockSpec((ATTN_WIDTH, tq), lambda t: (0, tile_of(t))),
            pl.BlockSpec((KV_WIDTH, tq), lambda t: (krow, tile_of(t))),
            pl.BlockSpec((KV_WIDTH, tq), lambda t: (vrow, tile_of(t))),
            halo(krow), halo(vrow)]


def _head_rows(g, r):
    h = g * Q_PER_KV + r
    return h, slice(h * HEAD_DIM, (h + 1) * HEAD_DIM)


def _group_stack(ref, g, cols):
    return jnp.concatenate([ref[_head_rows(g, r)[1], cols] for r in range(Q_PER_KV)], axis=1)


def _attn_fwd(zt, sinks, *, name):
    T = zt.shape[1]
    tq = _tile(T, TQ_ATTN, ATTN_BLOCK)
    nb = tq // ATTN_BLOCK

    def body(q_ref, k_ref, v_ref, kh_ref, vh_ref, s_ref, o_ref, kf, vf, pt):
        kf[:, 0:ATTN_BLOCK] = kh_ref[...]
        kf[:, ATTN_BLOCK:] = k_ref[...]
        vf[:, 0:ATTN_BLOCK] = vh_ref[...]
        vf[:, ATTN_BLOCK:] = v_ref[...]
        for b in range(nb):
            cols = slice(b * ATTN_BLOCK, (b + 1) * ATTN_BLOCK)
            win = slice(b * ATTN_BLOCK, (b + 2) * ATTN_BLOCK)
            valid = _attn_mask_t((pl.program_id(0) == 0) if b == 0 else False)
            for g in range(N_KV_HEADS):
                gr = slice(g * HEAD_DIM, (g + 1) * HEAD_DIM)
                st = _dot_tn(kf[gr, win], _group_stack(q_ref, g, cols)) * ATTN_SCALE
                for r in range(Q_PER_KV):
                    h, _ = _head_rows(g, r)
                    sl = slice(r * ATTN_BLOCK, (r + 1) * ATTN_BLOCK)
                    probs, _ = _attn_probs_t(st[:, sl], valid, s_ref[h])
                    pt[:, sl] = probs.astype(BF16)
                ot = _dot(vf[gr, win], pt[...])
                for r in range(Q_PER_KV):
                    o_ref[_head_rows(g, r)[1], cols] = ot[:, r * ATTN_BLOCK:(r + 1) * ATTN_BLOCK].astype(BF16)

    return pl.pallas_call(
        body, name=name, grid=(T // tq,),
        in_specs=_attn_specs(tq, lambda t: t) + [pl.BlockSpec(memory_space=pltpu.SMEM)],
        out_specs=pl.BlockSpec((ATTN_WIDTH, tq), lambda t: (0, t)),
        out_shape=SDS((ATTN_WIDTH, T), BF16),
        scratch_shapes=[pltpu.VMEM((KV_WIDTH, tq + ATTN_BLOCK), BF16)] * 2
        + [pltpu.VMEM((2 * ATTN_BLOCK, GROUP_LANES), BF16)],
        compiler_params=_params("parallel"),
    )(zt, zt, zt, zt, zt, sinks)


def _attn_bwd(zt, sinks, dot_, *, name):
    T = zt.shape[1]
    tq = _tile(T, TQ_ATTN, ATTN_BLOCK)
    nb = tq // ATTN_BLOCK
    nt = T // tq
    tile_of = lambda t: nt - 1 - t

    def body(q_ref, k_ref, v_ref, kh_ref, vh_ref, do_ref, s_ref, dz_ref, dsink_ref, kf, vf, dkf, dvf, carry, pt, dst):
        t = pl.program_id(0)

        @pl.when(t == 0)
        def _():
            carry[...] = jnp.zeros_like(carry)
            dsink_ref[...] = jnp.zeros_like(dsink_ref)

        kf[:, 0:ATTN_BLOCK] = kh_ref[...]
        kf[:, ATTN_BLOCK:] = k_ref[...]
        vf[:, 0:ATTN_BLOCK] = vh_ref[...]
        vf[:, ATTN_BLOCK:] = v_ref[...]
        dkf[...] = jnp.zeros_like(dkf)
        dvf[...] = jnp.zeros_like(dvf)
        dkf[:, tq:] = carry[0:KV_WIDTH, :]
        dvf[:, tq:] = carry[KV_WIDTH:, :]
        lane = lax.broadcasted_iota(jnp.int32, (1, LANES), 1)
        dsink = jnp.zeros((1, LANES), F32)
        for b in range(nb):
            cols = slice(b * ATTN_BLOCK, (b + 1) * ATTN_BLOCK)
            win = slice(b * ATTN_BLOCK, (b + 2) * ATTN_BLOCK)
            valid = _attn_mask_t((t == nt - 1) if b == 0 else False)
            for g in range(N_KV_HEADS):
                gr = slice(g * HEAD_DIM, (g + 1) * HEAD_DIM)
                kt2, vt2 = kf[gr, win], vf[gr, win]
                qst = _group_stack(q_ref, g, cols)
                dost = _group_stack(do_ref, g, cols)
                st = _dot_tn(kt2, qst) * ATTN_SCALE
                dpt = _dot_tn(vt2, dost)
                for r in range(Q_PER_KV):
                    h, _ = _head_rows(g, r)
                    sl = slice(r * ATTN_BLOCK, (r + 1) * ATTN_BLOCK)
                    probs, psink = _attn_probs_t(st[:, sl], valid, s_ref[h])
                    dp = dpt[:, sl]
                    delta = jnp.sum(probs * dp, axis=0, keepdims=True)
                    pt[:, sl] = probs.astype(BF16)
                    dst[:, sl] = (probs * (dp - delta)).astype(BF16)
                    dsink = dsink + jnp.where(lane == h, -jnp.sum(psink * delta), 0.0)
                dqt = _dot(kt2, dst[...]) * ATTN_SCALE
                for r in range(Q_PER_KV):
                    dz_ref[_head_rows(g, r)[1], cols] = dqt[:, r * ATTN_BLOCK:(r + 1) * ATTN_BLOCK].astype(BF16)
                dkf[gr, win] += _dot_nt(qst, dst[...]) * ATTN_SCALE
                dvf[gr, win] += _dot_nt(dost, pt[...])
        dz_ref[ATTN_WIDTH:ATTN_WIDTH + KV_WIDTH, :] = dkf[:, ATTN_BLOCK:].astype(BF16)
        dz_ref[ATTN_WIDTH + KV_WIDTH:, :] = dvf[:, ATTN_BLOCK:].astype(BF16)
        carry[0:KV_WIDTH, :] = dkf[:, 0:ATTN_BLOCK]
        carry[KV_WIDTH:, :] = dvf[:, 0:ATTN_BLOCK]
        dsink_ref[...] += dsink

    return pl.pallas_call(
        body, name=name, grid=(nt,),
        in_specs=_attn_specs(tq, tile_of) + [pl.BlockSpec((ATTN_WIDTH, tq), lambda t: (0, tile_of(t))),
                                             pl.BlockSpec(memory_space=pltpu.SMEM)],
        out_specs=[pl.BlockSpec((QKV_WIDTH, tq), lambda t: (0, tile_of(t))),
                   pl.BlockSpec((8, LANES), lambda t: (0, 0))],
        out_shape=[SDS((QKV_WIDTH, T), BF16), SDS((8, LANES), F32)],
        scratch_shapes=[pltpu.VMEM((KV_WIDTH, tq + ATTN_BLOCK), BF16)] * 2
        + [pltpu.VMEM((KV_WIDTH, tq + ATTN_BLOCK), F32)] * 2 + [pltpu.VMEM((2 * KV_WIDTH, ATTN_BLOCK), F32)]
        + [pltpu.VMEM((2 * ATTN_BLOCK, GROUP_LANES), BF16)] * 2,
        compiler_params=_params("arbitrary"),
    )(zt, zt, zt, zt, zt, dot_, sinks)


def _layer_norm_stats(v):
    mu = jnp.mean(v, axis=-1, keepdims=True)
    xc = v - mu
    rstd = lax.rsqrt(jnp.mean(xc * xc, axis=-1, keepdims=True) + LN_EPS)
    return rstd, xc * rstd


def _sgu_fwd(zmain, ln_g, ln_b, wm, bias, *, name):
    T = zmain.shape[0]
    ts = _tile(T, TS_SGU, SGU_CHUNK)

    def body(u_ref, v_ref, g_ref, b_ref, w_ref, bias_ref, y_ref):
        u = _gelu(u_ref[...].astype(F32))
        _, vh = _layer_norm_stats(_gelu(v_ref[...].astype(F32)))
        vn = (vh * g_ref[...] + b_ref[...]).astype(BF16)
        for ch in range(ts // SGU_CHUNK):
            rows = slice(ch * SGU_CHUNK, (ch + 1) * SGU_CHUNK)
            for g in range(SGU_GROUPS):
                cols = slice(g * 128, (g + 1) * 128)
                s = _dot(w_ref[g], vn[rows, cols]) + bias_ref[g]
                y_ref[rows, cols] = (u[rows, cols] * s).astype(BF16)

    full = _resident
    return pl.pallas_call(
        body, name=name, grid=(T // ts,),
        in_specs=[pl.BlockSpec((ts, SGU_WIDTH), lambda i: (i, 0)), pl.BlockSpec((ts, SGU_WIDTH), lambda i: (i, 1)),
                  full((1, SGU_WIDTH)), full((1, SGU_WIDTH)), full(wm.shape), full(bias.shape)],
        out_specs=pl.BlockSpec((ts, SGU_WIDTH), lambda i: (i, 0)),
        out_shape=SDS((T, SGU_WIDTH), BF16),
        compiler_params=_params("parallel"),
    )(zmain, zmain, ln_g, ln_b, wm, bias)


def _sgu_bwd(zmain, dzmain, dy, ln_g, ln_b, wm, wmt, bias, *, name):
    T = zmain.shape[0]
    ts = _tile(T, TS_SGU, SGU_CHUNK)

    def body(u_ref, v_ref, dy_ref, g_ref, b_ref, w_ref, wt_ref, bias_ref, _, dz_ref, dw_ref, db_ref, dlg_ref, dlb_ref,
             dvn):
        @pl.when(pl.program_id(0) == 0)
        def _():
            dw_ref[...] = jnp.zeros_like(dw_ref)
            db_ref[...] = jnp.zeros_like(db_ref)
            dlg_ref[...] = jnp.zeros_like(dlg_ref)
            dlb_ref[...] = jnp.zeros_like(dlb_ref)

        us = u_ref[...].astype(F32)
        vs = v_ref[...].astype(F32)
        u = _gelu(us)
        rstd, vh = _layer_norm_stats(_gelu(vs))
        vn = (vh * g_ref[...] + b_ref[...]).astype(BF16)
        causal = (lax.broadcasted_iota(jnp.int32, (SGU_CHUNK, SGU_CHUNK), 0)
                  >= lax.broadcasted_iota(jnp.int32, (SGU_CHUNK, SGU_CHUNK), 1))
        lane = lax.broadcasted_iota(jnp.int32, (SGU_CHUNK, LANES), 1)
        db = jnp.zeros((SGU_CHUNK, LANES), F32)
        for ch in range(ts // SGU_CHUNK):
            rows = slice(ch * SGU_CHUNK, (ch + 1) * SGU_CHUNK)
            for g in range(SGU_GROUPS):
                cols = slice(g * 128, (g + 1) * 128)
                vng = vn[rows, cols]
                s = _dot(w_ref[g], vng) + bias_ref[g]
                dyf = dy_ref[rows, cols].astype(F32)
                dz_ref[rows, cols] = (dyf * s * _gelu_grad(us[rows, cols])).astype(BF16)
                dsf = dyf * u[rows, cols]
                dsb = dsf.astype(BF16)
                dvn[rows, cols] = _dot(wt_ref[g], dsb)
                dw_ref[g] += jnp.where(causal, _dot_nt(dsb, vng), 0.0)
                db = db + jnp.where(lane == g, jnp.sum(dsf, axis=1, keepdims=True), 0.0)
        db_ref[...] += db
        dvnf = dvn[...]
        dlg_ref[...] += jnp.sum(dvnf * vh, axis=0, keepdims=True)
        dlb_ref[...] += jnp.sum(dvnf, axis=0, keepdims=True)
        dvh = dvnf * g_ref[...]
        dv = rstd * (dvh - jnp.mean(dvh, axis=-1, keepdims=True) - vh * jnp.mean(dvh * vh, axis=-1, keepdims=True))
        dz_ref[:, SGU_WIDTH:] = (dv * _gelu_grad(vs)).astype(BF16)

    full = _resident
    vec = full((1, SGU_WIDTH))
    acc = lambda shape: pl.BlockSpec(shape, lambda i: (0,) * len(shape))
    return pl.pallas_call(
        body, name=name, grid=(T // ts,),
        in_specs=[pl.BlockSpec((ts, SGU_WIDTH), lambda i: (i, 0)), pl.BlockSpec((ts, SGU_WIDTH), lambda i: (i, 1)),
                  pl.BlockSpec((ts, SGU_WIDTH), lambda i: (i, 0)), vec, vec, full(wm.shape), full(wm.shape),
                  full(bias.shape), pl.BlockSpec(memory_space=pl.ANY)],
        out_specs=[pl.BlockSpec((ts, 2 * SGU_WIDTH), lambda i: (i, 0)), acc(wm.shape),
                   acc((SGU_CHUNK, LANES)), acc((1, SGU_WIDTH)), acc((1, SGU_WIDTH))],
        out_shape=[SDS(dzmain.shape, BF16), SDS(wm.shape, F32), SDS((SGU_CHUNK, LANES), F32),
                   SDS((1, SGU_WIDTH), F32), SDS((1, SGU_WIDTH), F32)],
        scratch_shapes=[pltpu.VMEM((ts, SGU_WIDTH), F32)],
        input_output_aliases={8: 0},
        compiler_params=_params("arbitrary"),
    )(zmain, zmain, dy, ln_g, ln_b, wm, wmt, bias, dzmain)


def _merge_fwd(y_attn_t, y_sgu, zmain, w_a, w_s, w_o, x, g_post, *, name):
    T, D = x.shape
    tm = _tile(T, TM_ROW, LANES)

    def body(ya_ref, ys_ref, ga_ref, gb_ref, wa_ref, ws_ref, wo_ref, x_ref, g_ref, xn_ref, pa_ref, ps_ref, o_ref):
        pa = _dot_tn(ya_ref[...], wa_ref[...])
        ps = _dot(ys_ref[...], ws_ref[...])
        pa_ref[...] = pa.astype(BF16)
        ps_ref[...] = ps.astype(BF16)
        merged = _sigmoid(ga_ref[...].astype(F32)) * pa + _sigmoid(gb_ref[...].astype(F32)) * ps
        out = _dot(merged.astype(BF16), wo_ref[...])
        o_ref[...] = out.astype(BF16)
        _, oh = _rms_stats(out)
        xn_ref[...] = x_ref[...] + oh * g_ref[...]

    row = lambda col: pl.BlockSpec((tm, D), lambda i: (i, col))
    wfull = _resident((D, D))
    return pl.pallas_call(
        body, name=name, grid=(T // tm,),
        in_specs=[pl.BlockSpec((D, tm), lambda i: (0, i)), row(0), row(2), row(3), wfull, wfull, wfull, row(0),
                  pl.BlockSpec((1, D), lambda i: (0, 0))],
        out_specs=[row(0)] * 4,
        out_shape=[SDS((T, D), F32), SDS((T, D), BF16), SDS((T, D), BF16), SDS((T, D), BF16)],
        compiler_params=_params("parallel"),
    )(y_attn_t, y_sgu, zmain, zmain, w_a, w_s, w_o, x, g_post)


def _merge_bwd(dy, out, g_post, pa, ps, zmain, w_a, w_s, w_o, *, name):
    T, D = dy.shape
    tm = _tile(T, TM_ROW, LANES)

    def body(dy_ref, o_ref, g_ref, pa_ref, ps_ref, ga_ref, gb_ref, wa_ref, ws_ref, wo_ref,
             dz_ref, dout_ref, mg_ref, dpa_ref, dps_ref, dya_ref, dys_ref, dg_ref):
        @pl.when(pl.program_id(0) == 0)
        def _():
            dg_ref[...] = jnp.zeros_like(dg_ref)

        dout, dg = _rms_bwd(o_ref[...].astype(F32), g_ref[...], dy_ref[...])
        dg_ref[...] += dg
        doutb = dout.astype(BF16)
        dout_ref[...] = doutb
        dm = _dot_nt(doutb, wo_ref[...]).astype(BF16)
        pa, ps = pa_ref[...], ps_ref[...]
        sa = _sigmoid(ga_ref[...].astype(F32))
        sb = _sigmoid(gb_ref[...].astype(F32))
        one_minus_sa, one_minus_sb = (1.0 - sa).astype(BF16), (1.0 - sb).astype(BF16)
        sa, sb = sa.astype(BF16), sb.astype(BF16)
        mg_ref[...] = sa * pa + sb * ps
        dpa = dm * sa
        dps = dm * sb
        dpa_ref[...] = dpa
        dps_ref[...] = dps
        dz_ref[:, 0:D] = dpa * pa * one_minus_sa
        dz_ref[:, D:] = dps * ps * one_minus_sb
        dya_ref[...] = _dot_nt(wa_ref[...], dpa).astype(BF16)
        dys_ref[...] = _dot_nt(dps, ws_ref[...]).astype(BF16)

    row = lambda col: pl.BlockSpec((tm, D), lambda i: (i, col))
    wfull = _resident((D, D))
    vec = pl.BlockSpec((1, D), lambda i: (0, 0))
    act = SDS((T, D), BF16)
    return pl.pallas_call(
        body, name=name, grid=(T // tm,),
        in_specs=[row(0), row(0), vec, row(0), row(0), row(2), row(3), wfull, wfull, wfull],
        out_specs=[pl.BlockSpec((tm, 2 * D), lambda i: (i, 1))] + [row(0)] * 4
        + [pl.BlockSpec((D, tm), lambda i: (0, i)), row(0), vec],
        out_shape=[SDS(zmain.shape, BF16)] + [act] * 4 + [SDS((D, T), BF16), act, SDS((1, D), F32)],
        compiler_params=_params("arbitrary"),
    )(dy, out, g_post, pa, ps, zmain, zmain, w_a, w_s, w_o)


def _loss_head(y, target, *, name):
    T, D = y.shape
    tm = _tile(T, TM_ROW, 8)

    def body(y_ref, t_ref, dy_ref, l_ref):
        @pl.when(pl.program_id(0) == 0)
        def _():
            l_ref[...] = jnp.zeros_like(l_ref)

        e = y_ref[...] - t_ref[...]
        dy_ref[...] = e * (1.0 / D)
        l_ref[...] += jnp.sum(jnp.mean(e * e, axis=-1, keepdims=True))

    row = pl.BlockSpec((tm, D), lambda i: (i, 0))
    return pl.pallas_call(
        body, name=name, grid=(T // tm,), in_specs=[row, row],
        out_specs=[row, pl.BlockSpec((8, LANES), lambda i: (0, 0))],
        out_shape=[SDS((T, D), F32), SDS((8, LANES), F32)],
        compiler_params=_params("arbitrary"),
    )(y, target)


def _adamw(w, g, m, v, *, name):
    shape = w.shape
    cols = shape[-1]
    rows = w.size // cols
    w2, g2, m2, v2 = (t.reshape(rows, cols) for t in (w, g, m, v))
    tr = _tile(rows, max(8, (256 * 1024) // cols // 8 * 8), 8)

    def body(w_ref, g_ref, m_ref, v_ref, d_ref, nm_ref, nv_ref):
        gg = g_ref[...]
        nm = ADAM_B1 * m_ref[...] + (1.0 - ADAM_B1) * gg
        nv = ADAM_B2 * v_ref[...] + (1.0 - ADAM_B2) * (gg * gg)
        m_hat = nm / (1.0 - ADAM_B1 ** ADAM_STEP)
        v_hat = nv / (1.0 - ADAM_B2 ** ADAM_STEP)
        d_ref[...] = -ADAM_LR * (m_hat / (jnp.sqrt(v_hat) + ADAM_EPS) + ADAM_WD * w_ref[...])
        nm_ref[...] = nm
        nv_ref[...] = nv

    blk = pl.BlockSpec((tr, cols), lambda i: (i, 0))
    outs = pl.pallas_call(
        body, name=name, grid=(rows // tr,), in_specs=[blk] * 4, out_specs=[blk] * 3,
        out_shape=[SDS((rows, cols), F32)] * 3, compiler_params=_params("parallel"),
    )(w2, g2, m2, v2)
    return tuple(o.reshape(shape) for o in outs)


def _sum_terms(terms, n_rows, n_lead, dtypes, *, name):
    cols = terms[0][0].shape[-1]
    tr = _tile(n_rows, 704 if len(terms) <= 4 else 256, SUBLANES_BF16)
    nblk = n_rows // tr
    n_out = len(dtypes)

    def body(*refs):
        acc = refs[0][...].astype(F32)
        for r in refs[1:-n_out]:
            acc = acc + r[...].astype(F32)
        for o_ref in refs[-n_out:]:
            o_ref[...] = acc.astype(o_ref.dtype)

    def spec(lead, first):
        return pl.BlockSpec((1, tr, cols), lambda a, i: (lead(a), first(a) * nblk + i, 0))

    out = pl.BlockSpec((1, tr, cols), lambda a, i: (a, i, 0))
    return pl.pallas_call(
        body, name=name, grid=(n_lead, nblk), in_specs=[spec(lead, first) for _, lead, first in terms],
        out_specs=[out] * n_out, out_shape=[SDS((n_lead, n_rows, cols), d) for d in dtypes],
        compiler_params=_params("arbitrary", "arbitrary"),
    )(*[a for a, _, _ in terms])


def _position():
    x, y, c = (lax.axis_index(a) for a in AXES)
    chips = [(1 - x, y), (x, 1 - y), (1 - x, 1 - y)]
    return x, y, c, chips


ANY = pl.BlockSpec(memory_space=pl.ANY)


def _remote(src, dst, send_sems, recv_sems, k, to):
    return pltpu.make_async_remote_copy(src_ref=src, dst_ref=dst, send_sem=send_sems.at[k], recv_sem=recv_sems.at[k],
                                        device_id=to, device_id_type=MESH)


def _comm_call(body, arrays, out_shapes, n_sems, *, name):
    n = len(arrays)

    def wrapped(*refs):
        body(refs[:n], refs[n:n + len(out_shapes)], refs[-2], refs[-1])

    return pl.pallas_call(
        wrapped, name=name, in_specs=[ANY] * n, out_specs=[ANY] * len(out_shapes), out_shape=out_shapes,
        scratch_shapes=[pltpu.SemaphoreType.DMA((n_sems,)), pltpu.SemaphoreType.DMA((n_sems,))],
    )(*arrays)


def _gather_shards(packs, *, name):
    NS = 8

    def body(p_refs, o_refs, send_sems, recv_sems):
        x, y, c, _ = _position()
        me, sib = (x, y, c), (x, y, 1 - c)
        xn, yn = (1 - x, y, c), (x, 1 - y, c)
        s_me, s_xn, s_yn, s_dg = 2 * x + y, 2 * (1 - x) + y, 2 * x + 1 - y, 2 * (1 - x) + 1 - y
        copies = []

        def send(src, dst, k, to):
            cp = _remote(src, dst, send_sems, recv_sems, k, to)
            cp.start()
            copies.append(cp)

        def landed(ref, k):
            _remote(ref, ref, send_sems, recv_sems, k, me).wait_recv()
            return ref

        for a, (p_ref, o_ref) in enumerate(zip(p_refs, o_refs)):
            rh = p_ref.shape[0] // 2
            send(p_ref.at[pl.ds(c * rh, rh)], o_ref.at[s_me, pl.ds(c * rh, rh)], NS * a, xn)
            send(p_ref.at[pl.ds(c * rh, rh)], o_ref.at[s_me, pl.ds(c * rh, rh)], NS * a + 1, yn)
        for a, o_ref in enumerate(o_refs):
            rh = o_ref.shape[1] // 2
            rq = rh // 2
            q0, q1 = pl.ds(c * rh, rq), pl.ds(c * rh + rq, rq)
            from_x = landed(o_ref.at[s_xn, pl.ds(c * rh, rh)], NS * a)
            send(o_ref.at[s_xn, q0], o_ref.at[s_xn, q0], NS * a + 2, yn)
            send(from_x, from_x, NS * a + 4, sib)
            from_y = landed(o_ref.at[s_yn, pl.ds(c * rh, rh)], NS * a + 1)
            send(o_ref.at[s_yn, q1], o_ref.at[s_yn, q1], NS * a + 3, xn)
            send(from_y, from_y, NS * a + 5, sib)
        for a, o_ref in enumerate(o_refs):
            rh = o_ref.shape[1] // 2
            rq = rh // 2
            d0 = landed(o_ref.at[s_dg, pl.ds(c * rh, rq)], NS * a + 2)
            send(d0, d0, NS * a + 6, sib)
            d1 = landed(o_ref.at[s_dg, pl.ds(c * rh + rq, rq)], NS * a + 3)
            send(d1, d1, NS * a + 7, sib)
        for a, o_ref in enumerate(o_refs):
            rh = o_ref.shape[1] // 2
            rq = rh // 2
            o = (1 - c) * rh
            landed(o_ref.at[s_xn, pl.ds(o, rh)], NS * a + 4)
            landed(o_ref.at[s_yn, pl.ds(o, rh)], NS * a + 5)
            landed(o_ref.at[s_dg, pl.ds(o, rq)], NS * a + 6)
            landed(o_ref.at[s_dg, pl.ds(o + rq, rq)], NS * a + 7)
        for cp in copies:
            cp.wait_send()

    for p in packs:
        assert p.shape[0] % (4 * SUBLANES_BF16) == 0
    return _comm_call(body, packs, [SDS((N_CHIPS,) + p.shape, p.dtype) for p in packs], NS * len(packs), name=name)


def _sibling_exchange(gs, *, name):
    def body(g_refs, o_refs, send_sems, recv_sems):
        x, y, c, _ = _position()
        sent = [_remote(g_ref.at[:, pl.ds((1 - c) * o_ref.shape[1], o_ref.shape[1])], o_ref, send_sems, recv_sems, a,
                        (x, y, 1 - c)) for a, (g_ref, o_ref) in enumerate(zip(g_refs, o_refs))]
        for cp in sent:
            cp.start()
        for cp in sent:
            cp.wait()

    return _comm_call(body, gs, [SDS((N_CHIPS, g.shape[1] // 2, g.shape[2]), g.dtype) for g in gs], len(gs), name=name)


def _scatter_hop1(ps, *, name):
    def body(p_refs, o_refs, send_sems, recv_sems):
        x, y, c, _ = _position()
        xn, yn = (1 - x, y, c), (x, 1 - y, c)
        s_xn, s_yn, s_dg = 2 * (1 - x) + y, 2 * x + 1 - y, 2 * (1 - x) + 1 - y
        sent = []
        for a, (p_ref, o_ref) in enumerate(zip(p_refs, o_refs)):
            rq = o_ref.shape[1]
            first, second = pl.ds(0, rq), pl.ds(rq, rq)
            sent += [_remote(p_ref.at[s_xn, second], o_ref.at[0], send_sems, recv_sems, 4 * a, xn),
                     _remote(p_ref.at[s_dg, second], o_ref.at[1], send_sems, recv_sems, 4 * a + 1, xn),
                     _remote(p_ref.at[s_yn, first], o_ref.at[2], send_sems, recv_sems, 4 * a + 2, yn),
                     _remote(p_ref.at[s_dg, first], o_ref.at[3], send_sems, recv_sems, 4 * a + 3, yn)]
        for cp in sent:
            cp.start()
        for cp in sent:
            cp.wait()

    return _comm_call(body, ps, [SDS((4, p.shape[1] // 2, p.shape[2]), p.dtype) for p in ps], 4 * len(ps), name=name)


def _scatter_hop2(fs, *, name):
    def body(f_refs, o_refs, send_sems, recv_sems):
        x, y, c, _ = _position()
        sent = []
        for a, (f_ref, o_ref) in enumerate(zip(f_refs, o_refs)):
            sent += [_remote(f_ref.at[0], o_ref.at[0], send_sems, recv_sems, 2 * a, (1 - x, y, c)),
                     _remote(f_ref.at[1], o_ref.at[1], send_sems, recv_sems, 2 * a + 1, (x, 1 - y, c))]
        for cp in sent:
            cp.start()
        for cp in sent:
            cp.wait()

    return _comm_call(body, fs, [SDS(f.shape, f.dtype) for f in fs], 2 * len(fs), name=name)


def _sibling_swap(ts, *, name):
    def body(t_refs, o_refs, send_sems, recv_sems):
        x, y, c, _ = _position()
        sent = [_remote(t_ref, o_ref, send_sems, recv_sems, a, (x, y, 1 - c))
                for a, (t_ref, o_ref) in enumerate(zip(t_refs, o_refs))]
        for cp in sent:
            cp.start()
        for cp in sent:
            cp.wait()

    return _comm_call(body, ts, [SDS(t.shape, t.dtype) for t in ts], len(ts), name=name)


def _gather_all(v, *, name):
    M, C = v.shape

    def body(v_ref, o_ref, send_sems, recv_sems):
        x, y, c, chips = _position()
        slot = lambda px, py, pc: o_ref.at[4 * px + 2 * py + pc]
        first = [_remote(v_ref, slot(x, y, c), send_sems, recv_sems, 0, (x, y, 1 - c))]
        first += [_remote(v_ref, slot(x, y, c), send_sems, recv_sems, 1 + j, (*chip, c)) for j, chip in enumerate(chips)]
        for cp in first:
            cp.start()
        passed = []
        for j, chip in enumerate(chips):
            landed = slot(*chip, c)
            _remote(landed, landed, send_sems, recv_sems, 1 + j, (x, y, c)).wait_recv()
            cp = _remote(landed, landed, send_sems, recv_sems, 4 + j, (x, y, 1 - c))
            cp.start()
            passed.append(cp)
        sib = slot(x, y, 1 - c)
        _remote(sib, sib, send_sems, recv_sems, 0, (x, y, c)).wait_recv()
        for j, chip in enumerate(chips):
            theirs = slot(*chip, 1 - c)
            _remote(theirs, theirs, send_sems, recv_sems, 4 + j, (x, y, c)).wait_recv()
        for cp in first + passed:
            cp.wait_send()

    return pl.pallas_call(
        body, name=name, in_specs=[ANY], out_specs=ANY, out_shape=SDS((N_DEV, M, C), v.dtype),
        scratch_shapes=[pltpu.SemaphoreType.DMA((7,)), pltpu.SemaphoreType.DMA((7,))],
    )(v)


BIG = ("ffn1_w1", "ffn2_w1", "w_in", "ffn1_w2", "ffn2_w2", "w_attn_branch", "w_sgu_branch", "w_out")
COL_SHARDED = ("ffn1_w1", "w_in", "ffn2_w1")
FFN_IN = ("ffn1_w1", "ffn2_w1")
SMALL = ("ffn1_pre_g", "ffn1_post_g", "mix_pre_g", "attn_sinks", "sgu_ln_g", "sgu_ln_b", "sgu_w", "sgu_b",
         "mix_post_g", "ffn2_pre_g", "ffn2_post_g")
WEIGHTS = ("ffn1_pre_g", "ffn1_w1", "ffn1_w2", "ffn1_post_g", "mix_pre_g", "w_in", "attn_sinks", "sgu_ln_g",
           "sgu_ln_b", "sgu_w", "sgu_b", "w_attn_branch", "w_sgu_branch", "w_out", "mix_post_g", "ffn2_pre_g",
           "ffn2_w1", "ffn2_w2", "ffn2_post_g")


def _column_chunks(w, tn):
    return jnp.swapaxes(w.reshape(w.shape[0], w.shape[1] // tn, tn), 0, 1)


def _width_classes(shard_shapes):
    widths = sorted({shard_shapes[n][-1] for n in BIG}, reverse=True)
    return [[n for n in BIG if shard_shapes[n][-1] == w] for w in widths]


def _class_rows(classes, shard_shapes, n_layers):
    where = {}
    for k, names in enumerate(classes):
        off = 0
        for n in names:
            r = shard_shapes[n][0]
            assert off % r == 0
            where[n] = (k, off, r)
            off += n_layers * r
    return where


def _ffn_fwd(x, pre_g, w1, w1_block, w2, post_g, tag):
    a, h = _norm_matmul(x, pre_g, w1, w1_block, name=f"{tag}_up", with_h=True)
    xn, o = _swiglu_out(a, w2, x, post_g, name=f"{tag}_down")
    return xn, (x, h, a, o)


def _ffn_bwd(dy, saved, pre_g, w1, w1_block, w2, post_g, dw1_into, dw2_into, tag):
    x, h, a, o = saved
    da, s, do, d_post = _ffn_bwd_hidden(dy, o, post_g, a, w2, name=f"{tag}_bwd_hidden")
    g2 = _dw_rows(s, do, *dw2_into, name=f"{tag}_dw2")
    dx, d_pre = _matmul_nt_norm_bwd(da, w1, w1_block, x, pre_g, dy, None, name=f"{tag}_bwd_in")
    g1 = _dw_cols(h, da, N_CHIPS, *dw1_into, name=f"{tag}_dw1")
    return dx, g1, g2, d_pre, d_post


def kernel(x, ffn1_pre_g, ffn1_w1, ffn1_w2, ffn1_post_g, mix_pre_g, w_in, attn_sinks, sgu_ln_g, sgu_ln_b, sgu_w, sgu_b, w_attn_branch, w_sgu_branch, w_out, mix_post_g, ffn2_pre_g, ffn2_w1, ffn2_w2, ffn2_post_g, loss_target, m_ffn1_pre_g, m_ffn1_w1, m_ffn1_w2, m_ffn1_post_g, m_mix_pre_g, m_w_in, m_attn_sinks, m_sgu_ln_g, m_sgu_ln_b, m_sgu_w, m_sgu_b, m_w_attn_branch, m_w_sgu_branch, m_w_out, m_mix_post_g, m_ffn2_pre_g, m_ffn2_w1, m_ffn2_w2, m_ffn2_post_g, v_ffn1_pre_g, v_ffn1_w1, v_ffn1_w2, v_ffn1_post_g, v_mix_pre_g, v_w_in, v_attn_sinks, v_sgu_ln_g, v_sgu_ln_b, v_sgu_w, v_sgu_b, v_w_attn_branch, v_w_sgu_branch, v_w_out, v_mix_post_g, v_ffn2_pre_g, v_ffn2_w1, v_ffn2_w2, v_ffn2_post_g):
    given = dict(locals())
    W = {n: given[n] for n in WEIGHTS}
    M = {n: given["m_" + n] for n in WEIGHTS}
    V = {n: given["v_" + n] for n in WEIGHTS}
    L = ffn1_w1.shape[0]
    T, D = x.shape[1], x.shape[2]
    xt = x.reshape(T, D)
    target = loss_target.reshape(T, D)
    assert L % 2 == 0 and D == ATTN_WIDTH == SGU_WIDTH and T % ATTN_BLOCK == 0

    shard_shapes = {n: W[n].shape[1:] for n in BIG}
    classes = _width_classes(shard_shapes)
    my_chip = 2 * lax.axis_index("x") + lax.axis_index("y")
    my_core = lax.axis_index("c")
    where = _class_rows(classes, shard_shapes, L)
    packs = [jnp.concatenate([W[n].reshape(-1, shard_shapes[n][1]).astype(BF16) for n in names], axis=0)
             for names in classes]
    gathered = _gather_shards(packs, name="gather_weights")
    wcls = [lax.dynamic_update_slice(got, pack[None], (my_chip, 0, 0)) for pack, got in zip(packs, gathered)]

    def block_of(n, l):
        k, off, r = where[n]
        return k, off // r + l

    def chip_shards(n, l):
        k, off, r = where[n]
        return wcls[k][:, off + l * r:off + (l + 1) * r, :]

    full = []
    for l in range(L):
        fw = {n: chip_shards(n, l).reshape(-1, D) for n in BIG if n not in COL_SHARDED}
        w_in_l = jnp.swapaxes(chip_shards("w_in", l), 0, 1).reshape(D, -1)
        fw["w_qkv_t"] = w_in_l[:, :QKV_WIDTH].T
        fw["w_main"] = _column_chunks(w_in_l[:, QKV_WIDTH:], D)
        for n in FFN_IN:
            fw[n] = (wcls[block_of(n, l)[0]], block_of(n, l)[1])
        full.append(fw)

    row = lambda name, l: W[name][l].reshape(1, -1)
    causal = jnp.tril(jnp.ones((SGU_CHUNK, SGU_CHUNK), dtype=bool))
    saved = []
    h_cur = xt
    for l in range(L):
        fw = full[l]
        sv = {}
        h_cur, sv["ffn1"] = _ffn_fwd(h_cur, row("ffn1_pre_g", l), *fw["ffn1_w1"], fw["ffn1_w2"], row("ffn1_post_g", l),
                                     f"l{l}_ffn1")
        zqkv, hm = _norm_matmul_t(h_cur, row("mix_pre_g", l), fw["w_qkv_t"], name=f"l{l}_mix_in_qkv")
        zmain, = _norm_matmul(h_cur, row("mix_pre_g", l), fw["w_main"], 0, name=f"l{l}_mix_in_main", with_h=False)
        wm = jnp.where(causal[None], sgu_w[l], 0.0).astype(BF16)
        wmt = jnp.swapaxes(wm, 1, 2)
        bias = jnp.broadcast_to(sgu_b[l][:, :, None], (SGU_GROUPS, SGU_CHUNK, 128)).astype(F32)
        y_attn = _attn_fwd(zqkv, attn_sinks[l], name=f"l{l}_attn")
        y_sgu = _sgu_fwd(zmain, row("sgu_ln_g", l), row("sgu_ln_b", l), wm, bias, name=f"l{l}_sgu")
        x_mix = h_cur
        h_cur, pa, ps, mo = _merge_fwd(y_attn, y_sgu, zmain, fw["w_attn_branch"], fw["w_sgu_branch"], fw["w_out"],
                                       x_mix, row("mix_post_g", l), name=f"l{l}_merge")
        sv["mix"] = (x_mix, hm, zqkv, zmain, y_attn, y_sgu, pa, ps, mo, wm, wmt, bias)
        h_cur, sv["ffn2"] = _ffn_fwd(h_cur, row("ffn2_pre_g", l), *fw["ffn2_w1"], fw["ffn2_w2"], row("ffn2_post_g", l),
                                     f"l{l}_ffn2")
        saved.append(sv)

    dy, lsum = _loss_head(h_cur, target, name="loss_head")
    loss = lax.psum(0.5 * lsum[0, 0], AXES)

    k_in = where["w_in"][0]
    assert classes[k_in] == ["w_in"]
    gcls = [None if k == k_in else lax.empty((N_CHIPS,) + p.shape, F32) for k, p in enumerate(packs)]
    dw_in = [None] * L
    small_grads = [None] * L

    def into(n, l):
        return gcls[block_of(n, l)[0]], block_of(n, l)[1]

    def ffn_bwd(dy, which, l):
        n1, n2 = f"{which}_w1", f"{which}_w2"
        dy, g1, g2, d_pre, d_post = _ffn_bwd(
            dy, saved[l][which], row(f"{which}_pre_g", l), *full[l][n1], full[l][n2], row(f"{which}_post_g", l),
            into(n1, l), into(n2, l), f"l{l}_{which}")
        gcls[where[n1][0]], gcls[where[n2][0]] = g1, g2
        return dy, d_pre, d_post

    for l in reversed(range(L)):
        fw, sv = full[l], saved[l]
        gs = {}
        dy, gs["ffn2_pre_g"], gs["ffn2_post_g"] = ffn_bwd(dy, "ffn2", l)

        x_mix, hm, zqkv, zmain, y_attn, y_sgu, pa, ps, mo, wm, wmt, bias = sv["mix"]
        dzmain, dout, merged, dpa, dps, dya, dys, gs["mix_post_g"] = _merge_bwd(
            dy, mo, row("mix_post_g", l), pa, ps, zmain, fw["w_attn_branch"], fw["w_sgu_branch"], fw["w_out"],
            name=f"l{l}_merge_bwd")
        k_sq = where["w_out"][0]
        gcls[k_sq] = _dw_rows(merged, dout, *into("w_out", l), name=f"l{l}_dw_out")
        gcls[k_sq] = _dw_rows(y_attn, dpa, *into("w_attn_branch", l), name=f"l{l}_dw_attn", a_feature_major=True)
        gcls[k_sq] = _dw_rows(y_sgu, dps, *into("w_sgu_branch", l), name=f"l{l}_dw_sgu")
        dzqkv, dsink = _attn_bwd(zqkv, attn_sinks[l], dya, name=f"l{l}_attn_bwd")
        gs["attn_sinks"] = dsink[0, :N_Q_HEADS]
        dzmain, dsw, dsb, gs["sgu_ln_g"], gs["sgu_ln_b"] = _sgu_bwd(
            zmain, dzmain, dys, row("sgu_ln_g", l), row("sgu_ln_b", l), wm, wmt, bias, name=f"l{l}_sgu_bwd")
        gs["sgu_w"] = dsw
        gs["sgu_b"] = dsb[:, :SGU_GROUPS].T
        dh_qkv = _matmul_tn_rows(dzqkv, fw["w_qkv_t"], name=f"l{l}_mix_bwd_qkv")
        dy, gs["mix_pre_g"] = _matmul_nt_norm_bwd(dzmain, fw["w_main"], 0, x_mix, row("mix_pre_g", l), dy, dh_qkv,
                                                   name=f"l{l}_mix_bwd_in")
        dw_main = _dw_cols(hm, dzmain, zmain.shape[1] // D, None, 0, name=f"l{l}_dw_in_main")
        dw_in[l] = jnp.concatenate([_matmul_tokens(dzqkv, hm, name=f"l{l}_dw_in_qkv").T,
                                    jnp.swapaxes(dw_main, 0, 1).reshape(D, -1)], axis=1)

        dy, gs["ffn1_pre_g"], gs["ffn1_post_g"] = ffn_bwd(dy, "ffn1", l)
        small_grads[l] = gs
    grad_x = dy.reshape(x.shape)

    w_in_width = shard_shapes["w_in"][1]
    gcls[k_in] = jnp.stack([jnp.concatenate([g[:, s * w_in_width:(s + 1) * w_in_width] for g in dw_in], axis=0)
                            for s in range(N_CHIPS)])
    grs = gcls
    from_sibling = _sibling_exchange(grs, name="grads_sibling_exchange")
    zero = lambda a: 0
    own = lambda a: a
    core = lambda a: lax.axis_index("c")
    chip = lambda a: 2 * lax.axis_index("x") + lax.axis_index("y")
    chip_xn = lambda a: 2 * (1 - lax.axis_index("x")) + lax.axis_index("y")
    chip_yn = lambda a: 2 * lax.axis_index("x") + 1 - lax.axis_index("y")
    pairs = [_sum_terms([(g, own, core), (fs, own, zero)], fs.shape[1], N_CHIPS, (BF16,),
                        name=f"grads_pair_sum{k}")[0] for k, (g, fs) in enumerate(zip(grs, from_sibling))]
    hop1 = _scatter_hop1(pairs, name="grads_scatter_hop1")
    relay = [_sum_terms([(p, lambda a: chip_xn(a) + a * (chip_yn(a) - chip_xn(a)), own), (h, lambda a: 3 - 2 * a, zero)],
                        h.shape[1], 2, (BF16,), name=f"grads_relay_sum{k}")[0]
             for k, (p, h) in enumerate(zip(pairs, hop1))]
    hop2 = _scatter_hop2(relay, name="grads_scatter_hop2")
    halves = [_sum_terms([(g, chip, lambda a: 2 * core(a) + a), (fs, chip, own), (h1, lambda a: 2 - 2 * a, zero), (h2, own, zero)],
                         h1.shape[1], 2, (F32,), name=f"grads_chip_sum{k}")[0].reshape(fs.shape[1:])
              for k, (g, fs, h1, h2) in enumerate(zip(grs, from_sibling, hop1, hop2))]
    others = _sibling_swap(halves, name="grads_sibling_swap")

    grads = {n: [None] * L for n in SMALL}
    for names, half, other in zip(classes, halves, others):
        reduced = jnp.concatenate([jnp.where(my_core == 0, half, other), jnp.where(my_core == 0, other, half)], axis=0)
        for n in names:
            _, off, r = where[n]
            grads[n] = reduced[off:off + L * r].reshape((L,) + shard_shapes[n])

    def small_rows(gs):
        parts = []
        for n in SMALL:
            flat = gs[n].reshape(-1)
            pad = (-flat.shape[0]) % D
            parts.append(jnp.pad(flat, (0, pad)).reshape(-1, D))
        return jnp.concatenate(parts, axis=0)

    spack = jnp.concatenate([small_rows(small_grads[l]) for l in range(L)], axis=0)
    n_small = spack.shape[0]
    pad_rows = (-n_small) % SUBLANES_BF16
    spack = jnp.pad(spack, ((0, pad_rows), (0, 0)))
    everyone = _gather_all(spack, name="small_grads_gather")
    is_me = (jnp.arange(N_DEV) == 2 * my_chip + my_core)[:, None, None]
    everyone = jnp.where(is_me, spack[None], everyone)
    ssum = _sum_terms([(everyone, (lambda a, d=d: d), zero) for d in range(N_DEV)], spack.shape[0], 1, (F32,),
                      name="small_grads_sum")[0][0]
    per_layer = n_small // L
    for l in range(L):
        r0 = l * per_layer
        for n in SMALL:
            shp = W[n].shape[1:]
            size = math.prod(shp)
            nr = -(-size // D)
            grads[n][l] = ssum[r0:r0 + nr].reshape(-1)[:size].reshape(shp)
            r0 += nr
    grads.update({n: jnp.stack(grads[n]) for n in SMALL})

    delta, new_m, new_v = {}, {}, {}
    for n in WEIGHTS:
        delta[n], new_m[n], new_v[n] = _adamw(W[n], grads[n], M[n], V[n], name=f"adamw_{n}")

    return (loss, grad_x, *[grads[n] for n in WEIGHTS], *[delta[n] for n in WEIGHTS],
            *[new_m[n] for n in WEIGHTS], *[new_v[n] for n in WEIGHTS])
```

```python
import functools
import math

import jax
import jax.numpy as jnp
from jax import lax
from jax.experimental import pallas as pl
from jax.experimental.pallas import tpu as pltpu

F32, BF16 = jnp.float32, jnp.bfloat16
SDS = jax.ShapeDtypeStruct
MESH = pl.DeviceIdType.MESH
AXES = ("x", "y", "c")

HEAD_DIM = 64
N_Q_HEADS = 16
N_KV_HEADS = 2
Q_PER_KV = N_Q_HEADS // N_KV_HEADS
ATTN_WIDTH = N_Q_HEADS * HEAD_DIM
KV_WIDTH = N_KV_HEADS * HEAD_DIM
ATTN_BLOCK = 128
SGU_CHUNK = 128
SGU_GROUPS = 8
SGU_WIDTH = SGU_GROUPS * 128
QKV_WIDTH = ATTN_WIDTH + 2 * KV_WIDTH
RMS_EPS = 1e-6
LN_EPS = 1e-5
MASK_VALUE = -1e30
ATTN_SCALE = 1.0 / math.sqrt(HEAD_DIM)

ADAM_LR, ADAM_B1, ADAM_B2, ADAM_EPS, ADAM_WD, ADAM_STEP = 0.001, 0.9, 0.999, 1e-08, 0.01, 10

N_CHIPS = 4
N_DEV = 8

VMEM_LIMIT_BYTES = 56 * 1024 * 1024
LANES = 128
SUBLANES_BF16 = 16

TM_NORM_MATMUL = 1024
TM_ROW = 512
TM_FFN_BWD = 512
TT_REDUCE = 1024
TQ_ATTN = 1024
TM_FEATURE_MAJOR = 1024
TS_SGU = 512


def _tile(n, pref, mult):
    t = (min(pref, n) // mult) * mult
    while t >= mult:
        if n % t == 0:
            return t
        t -= mult
    return n


def _params(*sem):
    return pltpu.CompilerParams(dimension_semantics=sem, vmem_limit_bytes=VMEM_LIMIT_BYTES)


def _dot(a, b):
    return jnp.dot(a, b, preferred_element_type=F32)


def _dot_nt(a, b):
    return lax.dot_general(a, b, (((1,), (1,)), ((), ())), preferred_element_type=F32)


def _dot_tn(a, b):
    return lax.dot_general(a, b, (((0,), (0,)), ((), ())), preferred_element_type=F32)


def _sigmoid(x):
    return 0.5 * (1.0 + jnp.tanh(0.5 * x))


def _rms_stats(xf):
    r = lax.rsqrt(jnp.mean(xf * xf, axis=-1, keepdims=True) + RMS_EPS)
    return r, xf * r


def _rms_bwd(xf, g, dy):
    r, xh = _rms_stats(xf)
    dyg = dy * g
    dx = r * (dyg - xh * jnp.mean(dyg * xh, axis=-1, keepdims=True))
    return dx, jnp.sum(dy * xh, axis=0, keepdims=True)


def _gelu_parts(x):
    cdf = 0.5 * (1.0 + lax.erf(x * (1.0 / math.sqrt(2.0))))
    return cdf


def _gelu(x):
    return x * _gelu_parts(x)


def _gelu_grad(x):
    return _gelu_parts(x) + x * jnp.exp(-0.5 * x * x) * (1.0 / math.sqrt(2.0 * math.pi))


def _resident(shape, index=None):
    index = (0,) * len(shape) if index is None else index
    return pl.BlockSpec(shape, lambda *_: index, pipeline_mode=pl.Buffered(1))


def _norm_matmul(x, g, w3, w_block, *, name, with_h):
    T, D = x.shape
    nj, _, tn = w3.shape
    tm = _tile(T, TM_NORM_MATMUL, SUBLANES_BF16)

    def body(x_ref, g_ref, w_ref, a_ref, *rest):
        h_sc = rest[-1]

        @pl.when(pl.program_id(1) == 0)
        def _():
            _, xh = _rms_stats(x_ref[...])
            h = (xh * g_ref[...]).astype(BF16)
            h_sc[...] = h
            if with_h:
                rest[0][...] = h

        a_ref[...] = _dot(h_sc[...], w_ref[pl.program_id(1)]).astype(BF16)

    out_specs = [pl.BlockSpec((tm, tn), lambda i, j: (i, j))]
    out_shape = [SDS((T, nj * tn), BF16)]
    if with_h:
        out_specs.append(pl.BlockSpec((tm, D), lambda i, j: (i, 0)))
        out_shape.append(SDS((T, D), BF16))
    return pl.pallas_call(
        body, name=name, grid=(T // tm, nj),
        in_specs=[pl.BlockSpec((tm, D), lambda i, j: (i, 0)),
                  pl.BlockSpec((1, D), lambda i, j: (0, 0)),
                  _resident((nj, D, tn), (0, w_block, 0))],
        out_specs=out_specs, out_shape=out_shape,
        scratch_shapes=[pltpu.VMEM((tm, D), BF16)],
        compiler_params=_params("parallel", "arbitrary"),
    )(x, g, w3)


def _norm_matmul_t(x, g, wt, *, name):
    T, D = x.shape
    N = wt.shape[0]
    tm = _tile(T, TM_FEATURE_MAJOR, LANES)

    def body(x_ref, g_ref, w_ref, a_ref, h_ref):
        _, xh = _rms_stats(x_ref[...])
        h = (xh * g_ref[...]).astype(BF16)
        h_ref[...] = h
        a_ref[...] = _dot_nt(w_ref[...], h).astype(BF16)

    return pl.pallas_call(
        body, name=name, grid=(T // tm,),
        in_specs=[pl.BlockSpec((tm, D), lambda i: (i, 0)), pl.BlockSpec((1, D), lambda i: (0, 0)),
                  _resident((N, D))],
        out_specs=[pl.BlockSpec((N, tm), lambda i: (0, i)), pl.BlockSpec((tm, D), lambda i: (i, 0))],
        out_shape=[SDS((N, T), BF16), SDS((T, D), BF16)],
        compiler_params=_params("parallel"),
    )(x, g, wt)


def _matmul_tokens(at, b, *, name):
    K, T = at.shape
    N = b.shape[1]
    tt = _tile(T, TT_REDUCE, LANES)

    def body(a_ref, b_ref, o_ref):
        @pl.when(pl.program_id(0) == 0)
        def _():
            o_ref[...] = jnp.zeros_like(o_ref)

        o_ref[...] += _dot(a_ref[...], b_ref[...])

    return pl.pallas_call(
        body, name=name, grid=(T // tt,),
        in_specs=[pl.BlockSpec((K, tt), lambda t: (0, t)), pl.BlockSpec((tt, N), lambda t: (t, 0))],
        out_specs=pl.BlockSpec((K, N), lambda t: (0, 0)),
        out_shape=SDS((K, N), F32),
        compiler_params=_params("arbitrary"),
    )(at, b)


def _matmul_tn_rows(dat, wt, *, name):
    N, T = dat.shape
    D = wt.shape[1]
    tm = _tile(T, TM_FEATURE_MAJOR, LANES)

    def body(da_ref, w_ref, o_ref):
        o_ref[...] = _dot_tn(da_ref[...], w_ref[...])

    return pl.pallas_call(
        body, name=name, grid=(T // tm,),
        in_specs=[pl.BlockSpec((N, tm), lambda i: (0, i)), _resident((N, D))],
        out_specs=pl.BlockSpec((tm, D), lambda i: (i, 0)),
        out_shape=SDS((T, D), F32),
        compiler_params=_params("parallel"),
    )(dat, wt)


def _ff_chunk(F):
    return F if F <= 1408 else F // 2


def _swiglu_out(a, w2, x, g_post, *, name):
    T, F2 = a.shape
    F = F2 // 2
    D = x.shape[1]
    tm = _tile(T, TM_ROW, SUBLANES_BF16)
    fc = _ff_chunk(F)

    def body(a_ref, w_ref, x_ref, g_ref, xn_ref, o_ref):
        acc = None
        for c0 in range(0, F, fc):
            gt = a_ref[:, c0:c0 + fc].astype(F32)
            s = (gt * _sigmoid(gt)).astype(BF16) * a_ref[:, F + c0:F + c0 + fc]
            part = _dot(s, w_ref[c0:c0 + fc, :])
            acc = part if acc is None else acc + part
        o_ref[...] = acc.astype(BF16)
        _, oh = _rms_stats(acc)
        xn_ref[...] = x_ref[...] + 0.5 * (oh * g_ref[...])

    return pl.pallas_call(
        body, name=name, grid=(T // tm,),
        in_specs=[pl.BlockSpec((tm, F2), lambda i: (i, 0)),
                  _resident((F, D)),
                  pl.BlockSpec((tm, D), lambda i: (i, 0)),
                  pl.BlockSpec((1, D), lambda i: (0, 0))],
        out_specs=[pl.BlockSpec((tm, D), lambda i: (i, 0)), pl.BlockSpec((tm, D), lambda i: (i, 0))],
        out_shape=[SDS((T, D), F32), SDS((T, D), BF16)],
        compiler_params=_params("parallel"),
    )(a, w2, x, g_post)


def _ffn_bwd_hidden(dy, o, g_post, a, w2, *, name):
    T, F2 = a.shape
    F = F2 // 2
    D = dy.shape[1]
    tm = _tile(T, TM_FFN_BWD, SUBLANES_BF16)
    fc = _tile(F, 256, LANES)

    def body(dy_ref, o_ref, g_ref, a_ref, w_ref, da_ref, do_ref, dg_ref):
        @pl.when(pl.program_id(0) == 0)
        def _():
            dg_ref[...] = jnp.zeros_like(dg_ref)

        do, dg = _rms_bwd(o_ref[...].astype(F32), g_ref[...], 0.5 * dy_ref[...])
        dg_ref[...] += dg
        dob = do.astype(BF16)
        do_ref[...] = dob
        for c0 in range(0, F, fc):
            ds = _dot_nt(dob, w_ref[c0:c0 + fc, :]).astype(BF16)
            gt = a_ref[:, c0:c0 + fc].astype(F32)
            ub = a_ref[:, F + c0:F + c0 + fc]
            sg = _sigmoid(gt)
            sl = gt * sg
            dsl = (sg + sl * (1.0 - sg)).astype(BF16)
            da_ref[:, c0:c0 + fc] = ds * ub * dsl
            da_ref[:, F + c0:F + c0 + fc] = ds * sl.astype(BF16)

    row = lambda w: pl.BlockSpec((tm, w), lambda i: (i, 0))
    return pl.pallas_call(
        body, name=name, grid=(T // tm,),
        in_specs=[row(D), row(D), pl.BlockSpec((1, D), lambda i: (0, 0)), row(F2),
                  _resident((F, D))],
        out_specs=[row(F2), row(D), pl.BlockSpec((1, D), lambda i: (0, 0))],
        out_shape=[SDS((T, F2), BF16), SDS((T, D), BF16), SDS((1, D), F32)],
        compiler_params=_params("arbitrary"),
    )(dy, o, g_post, a, w2)


def _dw_call(body, name, grid, in_specs, args, block, pack, row_block, sem):
    if pack is None:
        out_spec = pl.BlockSpec(block, lambda *_: (0, 0, 0), pipeline_mode=pl.Buffered(1))
        return pl.pallas_call(body, name=name, grid=grid, in_specs=in_specs, out_specs=out_spec,
                              out_shape=SDS(block, F32), compiler_params=_params(*sem))(*args)
    assert pack.shape[0] == block[0] and pack.shape[2] == block[2]
    out_spec = pl.BlockSpec(block, lambda *_: (0, row_block, 0), pipeline_mode=pl.Buffered(1))
    return pl.pallas_call(body, name=name, grid=grid, in_specs=in_specs + [ANY], out_specs=out_spec,
                          out_shape=SDS(pack.shape, F32), input_output_aliases={len(args): 0},
                          compiler_params=_params(*sem))(*args, pack)


def _dw_cols(a, b, n_chunks, pack, row_block, *, name):
    T, K = a.shape
    tn = b.shape[1] // n_chunks
    tt = _tile(T, TT_REDUCE, SUBLANES_BF16)
    per = 2 if n_chunks % 2 == 0 else 1

    def body(a_ref, b_ref, *rest):
        o_ref = rest[-1]

        @pl.when(pl.program_id(1) == 0)
        def _():
            o_ref[...] = jnp.zeros_like(o_ref)

        for p in range(per):
            o_ref[p] += _dot_tn(a_ref[...], b_ref[:, p * tn:(p + 1) * tn])

    grid = (n_chunks // per, T // tt)
    in_specs = [pl.BlockSpec((tt, K), lambda j, t: (t, 0)), pl.BlockSpec((tt, per * tn), lambda j, t: (t, j))]
    sem = ("parallel", "arbitrary")
    if pack is None:
        return pl.pallas_call(body, name=name, grid=grid, in_specs=in_specs,
                              out_specs=pl.BlockSpec((per, K, tn), lambda j, t: (j, 0, 0)),
                              out_shape=SDS((n_chunks, K, tn), F32), compiler_params=_params(*sem))(a, b)
    assert pack.shape[0] == n_chunks and pack.shape[2] == tn
    return pl.pallas_call(body, name=name, grid=grid, in_specs=in_specs + [ANY],
                          out_specs=pl.BlockSpec((per, K, tn), lambda j, t: (j, row_block, 0)),
                          out_shape=SDS(pack.shape, F32), input_output_aliases={2: 0},
                          compiler_params=_params(*sem))(a, b, pack)


def _dw_rows(a, b, pack, row_block, *, name, a_feature_major=False, swiglu=False):
    K, T = a.shape if a_feature_major else a.shape[::-1]
    K = K // 2 if swiglu else K
    N = b.shape[1]
    r = K // N_CHIPS
    cw = r if r % LANES == 0 else 2 * r
    assert cw % LANES == 0 and K % cw == 0 and r % 8 == 0
    tt = _tile(T, TT_REDUCE // 2 if swiglu else TT_REDUCE, LANES)

    def body(a_ref, b_ref, *rest):
        o_ref = rest[-1]

        @pl.when(pl.program_id(0) == 0)
        def _():
            o_ref[...] = jnp.zeros_like(o_ref)

        for c in range(K // cw):
            if a_feature_major:
                part = _dot(a_ref[c * cw:(c + 1) * cw, :], b_ref[...])
            elif swiglu:
                gt = a_ref[:, c * cw:(c + 1) * cw].astype(F32)
                part = _dot_tn((gt * _sigmoid(gt)).astype(BF16) * a_ref[:, K + c * cw:K + (c + 1) * cw], b_ref[...])
            else:
                part = _dot_tn(a_ref[:, c * cw:(c + 1) * cw], b_ref[...])
            for p in range(cw // r):
                o_ref[c * (cw // r) + p] += part[p * r:(p + 1) * r]

    a_spec = (pl.BlockSpec((K, tt), lambda t: (0, t)) if a_feature_major
              else pl.BlockSpec((tt, a.shape[1]), lambda t: (t, 0)))
    return _dw_call(body, name, (T // tt,), [a_spec, pl.BlockSpec((tt, N), lambda t: (t, 0))],
                    [a, b], (N_CHIPS, r, N), pack, row_block, ("arbitrary",))


def _matmul_nt_norm_bwd(da, w, w_block, x, g, dy, init, *, name):
    T, N = da.shape
    D = x.shape[1]
    nj, _, tn = w.shape
    tm = _tile(T, TM_ROW, SUBLANES_BF16)
    has_init = init is not None

    def body(da_ref, w_ref, x_ref, g_ref, dy_ref, *rest):
        dx_ref, dg_ref = rest[-2:]

        @pl.when(pl.program_id(0) == 0)
        def _():
            dg_ref[...] = jnp.zeros_like(dg_ref)

        dh = rest[0][...] if has_init else None
        for j in range(nj):
            part = _dot_nt(da_ref[:, j * tn:(j + 1) * tn], w_ref[j])
            dh = part if dh is None else dh + part
        dx, dg = _rms_bwd(x_ref[...], g_ref[...], dh)
        dx_ref[...] = dy_ref[...] + dx
        dg_ref[...] += dg

    row = pl.BlockSpec((tm, D), lambda i: (i, 0))
    vec = pl.BlockSpec((1, D), lambda i: (0, 0))
    in_specs = [pl.BlockSpec((tm, N), lambda i: (i, 0)), _resident((nj, D, tn), (0, w_block, 0)), row, vec, row]
    args = [da, w, x, g, dy]
    if has_init:
        in_specs.append(row)
        args.append(init)
    return pl.pallas_call(
        body, name=name, grid=(T // tm,), in_specs=in_specs,
        out_specs=[row, vec], out_shape=[SDS((T, D), F32), SDS((1, D), F32)],
        compiler_params=_params("arbitrary"),
    )(*args)


GROUP_LANES = Q_PER_KV * ATTN_BLOCK


def _attn_mask_t(first):
    kj = lax.broadcasted_iota(jnp.int32, (2 * ATTN_BLOCK, ATTN_BLOCK), 0)
    qi = lax.broadcasted_iota(jnp.int32, (2 * ATTN_BLOCK, ATTN_BLOCK), 1)
    rel = qi + ATTN_BLOCK - kj
    band = (rel >= 0) & (rel < ATTN_BLOCK)
    if first is False:
        return band
    return band & ((kj >= ATTN_BLOCK) | jnp.logical_not(first))


def _attn_probs_t(st, valid, sink):
    s = jnp.where(valid, st, MASK_VALUE)
    m = jnp.maximum(jnp.max(s, axis=0, keepdims=True), sink)
    p = jnp.exp(s - m)
    es = jnp.exp(sink - m)
    inv = 1.0 / (jnp.sum(p, axis=0, keepdims=True) + es)
    return p * inv, es * inv


def _attn_specs(tq, tile_of):
    nb = tq // ATTN_BLOCK
    krow, vrow = ATTN_WIDTH // KV_WIDTH, ATTN_WIDTH // KV_WIDTH + 1
    halo = lambda r: pl.BlockSpec((KV_WIDTH, ATTN_BLOCK), lambda t: (r, jnp.maximum(tile_of(t) * nb - 1, 0)))
    return [pl.BlockSpec((ATTN_WIDTH, tq), lambda t: (0, tile_of(t))),
            pl.BlockSpec((KV_WIDTH, tq), lambda t: (krow, tile_of(t))),
            pl.BlockSpec((KV_WIDTH, tq), lambda t: (vrow, tile_of(t))),
            halo(krow), halo(vrow)]


def _head_rows(g, r):
    h = g * Q_PER_KV + r
    return h, slice(h * HEAD_DIM, (h + 1) * HEAD_DIM)


def _group_stack(ref, g, cols):
    return jnp.concatenate([ref[_head_rows(g, r)[1], cols] for r in range(Q_PER_KV)], axis=1)


def _attn_fwd(zt, sinks, *, name):
    T = zt.shape[1]
    tq = _tile(T, TQ_ATTN, ATTN_BLOCK)
    nb = tq // ATTN_BLOCK

    def body(q_ref, k_ref, v_ref, kh_ref, vh_ref, s_ref, o_ref, kf, vf, pt):
        kf[:, 0:ATTN_BLOCK] = kh_ref[...]
        kf[:, ATTN_BLOCK:] = k_ref[...]
        vf[:, 0:ATTN_BLOCK] = vh_ref[...]
        vf[:, ATTN_BLOCK:] = v_ref[...]
        for b in range(nb):
            cols = slice(b * ATTN_BLOCK, (b + 1) * ATTN_BLOCK)
            win = slice(b * ATTN_BLOCK, (b + 2) * ATTN_BLOCK)
            valid = _attn_mask_t((pl.program_id(0) == 0) if b == 0 else False)
            for g in range(N_KV_HEADS):
                gr = slice(g * HEAD_DIM, (g + 1) * HEAD_DIM)
                st = _dot_tn(kf[gr, win], _group_stack(q_ref, g, cols)) * ATTN_SCALE
                for r in range(Q_PER_KV):
                    h, _ = _head_rows(g, r)
                    sl = slice(r * ATTN_BLOCK, (r + 1) * ATTN_BLOCK)
                    probs, _ = _attn_probs_t(st[:, sl], valid, s_ref[h])
                    pt[:, sl] = probs.astype(BF16)
                ot = _dot(vf[gr, win], pt[...])
                for r in range(Q_PER_KV):
                    o_ref[_head_rows(g, r)[1], cols] = ot[:, r * ATTN_BLOCK:(r + 1) * ATTN_BLOCK].astype(BF16)

    return pl.pallas_call(
        body, name=name, grid=(T // tq,),
        in_specs=_attn_specs(tq, lambda t: t) + [pl.BlockSpec(memory_space=pltpu.SMEM)],
        out_specs=pl.BlockSpec((ATTN_WIDTH, tq), lambda t: (0, t)),
        out_shape=SDS((ATTN_WIDTH, T), BF16),
        scratch_shapes=[pltpu.VMEM((KV_WIDTH, tq + ATTN_BLOCK), BF16)] * 2
        + [pltpu.VMEM((2 * ATTN_BLOCK, GROUP_LANES), BF16)],
        compiler_params=_params("parallel"),
    )(zt, zt, zt, zt, zt, sinks)


def _attn_bwd(zt, sinks, dot_, *, name):
    T = zt.shape[1]
    tq = _tile(T, TQ_ATTN, ATTN_BLOCK)
    nb = tq // ATTN_BLOCK
    nt = T // tq
    tile_of = lambda t: nt - 1 - t

    def body(q_ref, k_ref, v_ref, kh_ref, vh_ref, do_ref, s_ref, dz_ref, dsink_ref, kf, vf, dkf, dvf, carry, pt, dst):
        t = pl.program_id(0)

        @pl.when(t == 0)
        def _():
            carry[...] = jnp.zeros_like(carry)
            dsink_ref[...] = jnp.zeros_like(dsink_ref)

        kf[:, 0:ATTN_BLOCK] = kh_ref[...]
        kf[:, ATTN_BLOCK:] = k_ref[...]
        vf[:, 0:ATTN_BLOCK] = vh_ref[...]
        vf[:, ATTN_BLOCK:] = v_ref[...]
        dkf[...] = jnp.zeros_like(dkf)
        dvf[...] = jnp.zeros_like(dvf)
        dkf[:, tq:] = carry[0:KV_WIDTH, :]
        dvf[:, tq:] = carry[KV_WIDTH:, :]
        lane = lax.broadcasted_iota(jnp.int32, (1, LANES), 1)
        dsink = jnp.zeros((1, LANES), F32)
        for b in range(nb):
            cols = slice(b * ATTN_BLOCK, (b + 1) * ATTN_BLOCK)
            win = slice(b * ATTN_BLOCK, (b + 2) * ATTN_BLOCK)
            valid = _attn_mask_t((t == nt - 1) if b == 0 else False)
            for g in range(N_KV_HEADS):
                gr = slice(g * HEAD_DIM, (g + 1) * HEAD_DIM)
                kt2, vt2 = kf[gr, win], vf[gr, win]
                qst = _group_stack(q_ref, g, cols)
                dost = _group_stack(do_ref, g, cols)
                st = _dot_tn(kt2, qst) * ATTN_SCALE
                dpt = _dot_tn(vt2, dost)
                for r in range(Q_PER_KV):
                    h, _ = _head_rows(g, r)
                    sl = slice(r * ATTN_BLOCK, (r + 1) * ATTN_BLOCK)
                    probs, psink = _attn_probs_t(st[:, sl], valid, s_ref[h])
                    dp = dpt[:, sl]
                    delta = jnp.sum(probs * dp, axis=0, keepdims=True)
                    pt[:, sl] = probs.astype(BF16)
                    dst[:, sl] = (probs * (dp - delta)).astype(BF16)
                    dsink = dsink + jnp.where(lane == h, -jnp.sum(psink * delta), 0.0)
                dqt = _dot(kt2, dst[...]) * ATTN_SCALE
                for r in range(Q_PER_KV):
                    dz_ref[_head_rows(g, r)[1], cols] = dqt[:, r * ATTN_BLOCK:(r + 1) * ATTN_BLOCK].astype(BF16)
                dkf[gr, win] += _dot_nt(qst, dst[...]) * ATTN_SCALE
                dvf[gr, win] += _dot_nt(dost, pt[...])
        dz_ref[ATTN_WIDTH:ATTN_WIDTH + KV_WIDTH, :] = dkf[:, ATTN_BLOCK:].astype(BF16)
        dz_ref[ATTN_WIDTH + KV_WIDTH:, :] = dvf[:, ATTN_BLOCK:].astype(BF16)
        carry[0:KV_WIDTH, :] = dkf[:, 0:ATTN_BLOCK]
        carry[KV_WIDTH:, :] = dvf[:, 0:ATTN_BLOCK]
        dsink_ref[...] += dsink

    return pl.pallas_call(
        body, name=name, grid=(nt,),
        in_specs=_attn_specs(tq, tile_of) + [pl.BlockSpec((ATTN_WIDTH, tq), lambda t: (0, tile_of(t))),
                                             pl.BlockSpec(memory_space=pltpu.SMEM)],
        out_specs=[pl.BlockSpec((QKV_WIDTH, tq), lambda t: (0, tile_of(t))),
                   pl.BlockSpec((8, LANES), lambda t: (0, 0))],
        out_shape=[SDS((QKV_WIDTH, T), BF16), SDS((8, LANES), F32)],
        scratch_shapes=[pltpu.VMEM((KV_WIDTH, tq + ATTN_BLOCK), BF16)] * 2
        + [pltpu.VMEM((KV_WIDTH, tq + ATTN_BLOCK), F32)] * 2 + [pltpu.VMEM((2 * KV_WIDTH, ATTN_BLOCK), F32)]
        + [pltpu.VMEM((2 * ATTN_BLOCK, GROUP_LANES), BF16)] * 2,
        compiler_params=_params("arbitrary"),
    )(zt, zt, zt, zt, zt, dot_, sinks)


def _layer_norm_stats(v):
    mu = jnp.mean(v, axis=-1, keepdims=True)
    xc = v - mu
    rstd = lax.rsqrt(jnp.mean(xc * xc, axis=-1, keepdims=True) + LN_EPS)
    return rstd, xc * rstd


def _sgu_fwd(zmain, ln_g, ln_b, wm, bias, *, name):
    T = zmain.shape[0]
    ts = _tile(T, TS_SGU, SGU_CHUNK)

    def body(u_ref, v_ref, g_ref, b_ref, w_ref, bias_ref, y_ref):
        u = _gelu(u_ref[...].astype(F32))
        _, vh = _layer_norm_stats(_gelu(v_ref[...].astype(F32)))
        vn = (vh * g_ref[...] + b_ref[...]).astype(BF16)
        for ch in range(ts // SGU_CHUNK):
            rows = slice(ch * SGU_CHUNK, (ch + 1) * SGU_CHUNK)
            for g in range(SGU_GROUPS):
                cols = slice(g * 128, (g + 1) * 128)
                s = _dot(w_ref[g], vn[rows, cols]) + bias_ref[g]
                y_ref[rows, cols] = (u[rows, cols] * s).astype(BF16)

    full = _resident
    return pl.pallas_call(
        body, name=name, grid=(T // ts,),
        in_specs=[pl.BlockSpec((ts, SGU_WIDTH), lambda i: (i, 0)), pl.BlockSpec((ts, SGU_WIDTH), lambda i: (i, 1)),
                  full((1, SGU_WIDTH)), full((1, SGU_WIDTH)), full(wm.shape), full(bias.shape)],
        out_specs=pl.BlockSpec((ts, SGU_WIDTH), lambda i: (i, 0)),
        out_shape=SDS((T, SGU_WIDTH), BF16),
        compiler_params=_params("parallel"),
    )(zmain, zmain, ln_g, ln_b, wm, bias)


def _sgu_bwd(zmain, dzmain, dy, ln_g, ln_b, wm, wmt, bias, *, name):
    T = zmain.shape[0]
    ts = _tile(T, TS_SGU, SGU_CHUNK)

    def body(u_ref, v_ref, dy_ref, g_ref, b_ref, w_ref, wt_ref, bias_ref, _, dz_ref, dw_ref, db_ref, dlg_ref, dlb_ref,
             dvn):
        @pl.when(pl.program_id(0) == 0)
        def _():
            dw_ref[...] = jnp.zeros_like(dw_ref)
            db_ref[...] = jnp.zeros_like(db_ref)
            dlg_ref[...] = jnp.zeros_like(dlg_ref)
            dlb_ref[...] = jnp.zeros_like(dlb_ref)

        us = u_ref[...].astype(F32)
        vs = v_ref[...].astype(F32)
        u = _gelu(us)
        rstd, vh = _layer_norm_stats(_gelu(vs))
        vn = (vh * g_ref[...] + b_ref[...]).astype(BF16)
        causal = (lax.broadcasted_iota(jnp.int32, (SGU_CHUNK, SGU_CHUNK), 0)
                  >= lax.broadcasted_iota(jnp.int32, (SGU_CHUNK, SGU_CHUNK), 1))
        lane = lax.broadcasted_iota(jnp.int32, (SGU_CHUNK, LANES), 1)
        db = jnp.zeros((SGU_CHUNK, LANES), F32)
        for ch in range(ts // SGU_CHUNK):
            rows = slice(ch * SGU_CHUNK, (ch + 1) * SGU_CHUNK)
            for g in range(SGU_GROUPS):
                cols = slice(g * 128, (g + 1) * 128)
                vng = vn[rows, cols]
                s = _dot(w_ref[g], vng) + bias_ref[g]
                dyf = dy_ref[rows, cols].astype(F32)
                dz_ref[rows, cols] = (dyf * s * _gelu_grad(us[rows, cols])).astype(BF16)
                dsf = dyf * u[rows, cols]
                dsb = dsf.astype(BF16)
                dvn[rows, cols] = _dot(wt_ref[g], dsb)
                dw_ref[g] += jnp.where(causal, _dot_nt(dsb, vng), 0.0)
                db = db + jnp.where(lane == g, jnp.sum(dsf, axis=1, keepdims=True), 0.0)
        db_ref[...] += db
        dvnf = dvn[...]
        dlg_ref[...] += jnp.sum(dvnf * vh, axis=0, keepdims=True)
        dlb_ref[...] += jnp.sum(dvnf, axis=0, keepdims=True)
        dvh = dvnf * g_ref[...]
        dv = rstd * (dvh - jnp.mean(dvh, axis=-1, keepdims=True) - vh * jnp.mean(dvh * vh, axis=-1, keepdims=True))
        dz_ref[:, SGU_WIDTH:] = (dv * _gelu_grad(vs)).astype(BF16)

    full = _resident
    vec = full((1, SGU_WIDTH))
    acc = lambda shape: pl.BlockSpec(shape, lambda i: (0,) * len(shape))
    return pl.pallas_call(
        body, name=name, grid=(T // ts,),
        in_specs=[pl.BlockSpec((ts, SGU_WIDTH), lambda i: (i, 0)), pl.BlockSpec((ts, SGU_WIDTH), lambda i: (i, 1)),
                  pl.BlockSpec((ts, SGU_WIDTH), lambda i: (i, 0)), vec, vec, full(wm.shape), full(wm.shape),
                  full(bias.shape), pl.BlockSpec(memory_space=pl.ANY)],
        out_specs=[pl.BlockSpec((ts, 2 * SGU_WIDTH), lambda i: (i, 0)), acc(wm.shape),
                   acc((SGU_CHUNK, LANES)), acc((1, SGU_WIDTH)), acc((1, SGU_WIDTH))],
        out_shape=[SDS(dzmain.shape, BF16), SDS(wm.shape, F32), SDS((SGU_CHUNK, LANES), F32),
                   SDS((1, SGU_WIDTH), F32), SDS((1, SGU_WIDTH), F32)],
        scratch_shapes=[pltpu.VMEM((ts, SGU_WIDTH), F32)],
        input_output_aliases={8: 0},
        compiler_params=_params("arbitrary"),
    )(zmain, zmain, dy, ln_g, ln_b, wm, wmt, bias, dzmain)


def _merge_fwd(y_attn_t, y_sgu, zmain, w_a, w_s, w_o, x, g_post, *, name):
    T, D = x.shape
    tm = _tile(T, TM_ROW, LANES)

    def body(ya_ref, ys_ref, ga_ref, gb_ref, wa_ref, ws_ref, wo_ref, x_ref, g_ref, xn_ref, pa_ref, ps_ref, o_ref):
        pa = _dot_tn(ya_ref[...], wa_ref[...])
        ps = _dot(ys_ref[...], ws_ref[...])
        pa_ref[...] = pa.astype(BF16)
        ps_ref[...] = ps.astype(BF16)
        merged = _sigmoid(ga_ref[...].astype(F32)) * pa + _sigmoid(gb_ref[...].astype(F32)) * ps
        out = _dot(merged.astype(BF16), wo_ref[...])
        o_ref[...] = out.astype(BF16)
        _, oh = _rms_stats(out)
        xn_ref[...] = x_ref[...] + oh * g_ref[...]

    row = lambda col: pl.BlockSpec((tm, D), lambda i: (i, col))
    wfull = _resident((D, D))
    return pl.pallas_call(
        body, name=name, grid=(T // tm,),
        in_specs=[pl.BlockSpec((D, tm), lambda i: (0, i)), row(0), row(2), row(3), wfull, wfull, wfull, row(0),
                  pl.BlockSpec((1, D), lambda i: (0, 0))],
        out_specs=[row(0)] * 4,
        out_shape=[SDS((T, D), F32), SDS((T, D), BF16), SDS((T, D), BF16), SDS((T, D), BF16)],
        compiler_params=_params("parallel"),
    )(y_attn_t, y_sgu, zmain, zmain, w_a, w_s, w_o, x, g_post)


def _merge_bwd(dy, out, g_post, pa, ps, zmain, w_a, w_s, w_o, *, name):
    T, D = dy.shape
    tm = _tile(T, TM_ROW, LANES)

    def body(dy_ref, o_ref, g_ref, pa_ref, ps_ref, ga_ref, gb_ref, wa_ref, ws_ref, wo_ref,
             dz_ref, dout_ref, mg_ref, dpa_ref, dps_ref, dya_ref, dys_ref, dg_ref):
        @pl.when(pl.program_id(0) == 0)
        def _():
            dg_ref[...] = jnp.zeros_like(dg_ref)

        dout, dg = _rms_bwd(o_ref[...].astype(F32), g_ref[...], dy_ref[...])
        dg_ref[...] += dg
        doutb = dout.astype(BF16)
        dout_ref[...] = doutb
        dm = _dot_nt(doutb, wo_ref[...]).astype(BF16)
        pa, ps = pa_ref[...], ps_ref[...]
        sa = _sigmoid(ga_ref[...].astype(F32))
        sb = _sigmoid(gb_ref[...].astype(F32))
        one_minus_sa, one_minus_sb = (1.0 - sa).astype(BF16), (1.0 - sb).astype(BF16)
        sa, sb = sa.astype(BF16), sb.astype(BF16)
        mg_ref[...] = sa * pa + sb * ps
        dpa = dm * sa
        dps = dm * sb
        dpa_ref[...] = dpa
        dps_ref[...] = dps
        dz_ref[:, 0:D] = dpa * pa * one_minus_sa
        dz_ref[:, D:] = dps * ps * one_minus_sb
        dya_ref[...] = _dot_nt(wa_ref[...], dpa).astype(BF16)
        dys_ref[...] = _dot_nt(dps, ws_ref[...]).astype(BF16)

    row = lambda col: pl.BlockSpec((tm, D), lambda i: (i, col))
    wfull = _resident((D, D))
    vec = pl.BlockSpec((1, D), lambda i: (0, 0))
    act = SDS((T, D), BF16)
    return pl.pallas_call(
        body, name=name, grid=(T // tm,),
        in_specs=[row(0), row(0), vec, row(0), row(0), row(2), row(3), wfull, wfull, wfull],
        out_specs=[pl.BlockSpec((tm, 2 * D), lambda i: (i, 1))] + [row(0)] * 4
        + [pl.BlockSpec((D, tm), lambda i: (0, i)), row(0), vec],
        out_shape=[SDS(zmain.shape, BF16)] + [act] * 4 + [SDS((D, T), BF16), act, SDS((1, D), F32)],
        compiler_params=_params("arbitrary"),
    )(dy, out, g_post, pa, ps, zmain, zmain, w_a, w_s, w_o)


def _loss_head(y, target, *, name):
    T, D = y.shape
    tm = _tile(T, TM_ROW, 8)

    def body(y_ref, t_ref, dy_ref, l_ref):
        @pl.when(pl.program_id(0) == 0)
        def _():
            l_ref[...] = jnp.zeros_like(l_ref)

        e = y_ref[...] - t_ref[...]
        dy_ref[...] = e * (1.0 / D)
        l_ref[...] += jnp.sum(jnp.mean(e * e, axis=-1, keepdims=True))

    row = pl.BlockSpec((tm, D), lambda i: (i, 0))
    return pl.pallas_call(
        body, name=name, grid=(T // tm,), in_specs=[row, row],
        out_specs=[row, pl.BlockSpec((8, LANES), lambda i: (0, 0))],
        out_shape=[SDS((T, D), F32), SDS((8, LANES), F32)],
        compiler_params=_params("arbitrary"),
    )(y, target)


def _adamw(w, g, m, v, *, name):
    shape = w.shape
    cols = shape[-1]
    rows = w.size // cols
    w2, g2, m2, v2 = (t.reshape(rows, cols) for t in (w, g, m, v))
    tr = _tile(rows, max(8, (256 * 1024) // cols // 8 * 8), 8)

    def body(w_ref, g_ref, m_ref, v_ref, d_ref, nm_ref, nv_ref):
        gg = g_ref[...]
        nm = ADAM_B1 * m_ref[...] + (1.0 - ADAM_B1) * gg
        nv = ADAM_B2 * v_ref[...] + (1.0 - ADAM_B2) * (gg * gg)
        m_hat = nm / (1.0 - ADAM_B1 ** ADAM_STEP)
        v_hat = nv / (1.0 - ADAM_B2 ** ADAM_STEP)
        d_ref[...] = -ADAM_LR * (m_hat / (jnp.sqrt(v_hat) + ADAM_EPS) + ADAM_WD * w_ref[...])
        nm_ref[...] = nm
        nv_ref[...] = nv

    blk = pl.BlockSpec((tr, cols), lambda i: (i, 0))
    outs = pl.pallas_call(
        body, name=name, grid=(rows // tr,), in_specs=[blk] * 4, out_specs=[blk] * 3,
        out_shape=[SDS((rows, cols), F32)] * 3, compiler_params=_params("parallel"),
    )(w2, g2, m2, v2)
    return tuple(o.reshape(shape) for o in outs)


def _sum_terms(terms, n_rows, n_lead, dtypes, *, name, out_lead=None, out_n_lead=None):
    cols = terms[0][0].shape[-1]
    tr = _tile(n_rows, 704 if len(terms) <= 4 else 256, SUBLANES_BF16)
    nblk = n_rows // tr
    n_out = len(dtypes)

    def body(*refs):
        acc = refs[0][...].astype(F32)
        for r in refs[1:-n_out]:
            acc = acc + r[...].astype(F32)
        for o_ref in refs[-n_out:]:
            o_ref[...] = acc.astype(o_ref.dtype)

    def spec(lead, first):
        return pl.BlockSpec((1, tr, cols), lambda a, i: (lead(a), first(a) * nblk + i, 0))

    out = pl.BlockSpec((1, tr, cols), lambda a, i: (a if out_lead is None else out_lead(a), i, 0))
    return pl.pallas_call(
        body, name=name, grid=(n_lead, nblk), in_specs=[spec(lead, first) for _, lead, first in terms],
        out_specs=[out] * n_out, out_shape=[SDS((out_n_lead or n_lead, n_rows, cols), d) for d in dtypes],
        compiler_params=_params("arbitrary", "arbitrary"),
    )(*[a for a, _, _ in terms])


def _position():
    x, y, c = (lax.axis_index(a) for a in AXES)
    chips = [(1 - x, y), (x, 1 - y), (1 - x, 1 - y)]
    return x, y, c, chips


ANY = pl.BlockSpec(memory_space=pl.ANY)


def _remote(src, dst, send_sems, recv_sems, k, to):
    return pltpu.make_async_remote_copy(src_ref=src, dst_ref=dst, send_sem=send_sems.at[k], recv_sem=recv_sems.at[k],
                                        device_id=to, device_id_type=MESH)


def _comm_call(body, arrays, out_shapes, n_sems, *, name):
    n = len(arrays)

    def wrapped(*refs):
        body(refs[:n], refs[n:n + len(out_shapes)], refs[-2], refs[-1])

    return pl.pallas_call(
        wrapped, name=name, in_specs=[ANY] * n, out_specs=[ANY] * len(out_shapes), out_shape=out_shapes,
        scratch_shapes=[pltpu.SemaphoreType.DMA((n_sems,)), pltpu.SemaphoreType.DMA((n_sems,))],
    )(*arrays)


def _gather_shards(packs, *, name):
    NS = 8

    def body(p_refs, o_refs, send_sems, recv_sems):
        x, y, c, _ = _position()
        me, sib = (x, y, c), (x, y, 1 - c)
        xn, yn = (1 - x, y, c), (x, 1 - y, c)
        s_me, s_xn, s_yn, s_dg = 2 * x + y, 2 * (1 - x) + y, 2 * x + 1 - y, 2 * (1 - x) + 1 - y
        copies = []

        def send(src, dst, k, to):
            cp = _remote(src, dst, send_sems, recv_sems, k, to)
            cp.start()
            copies.append(cp)

        def landed(ref, k):
            _remote(ref, ref, send_sems, recv_sems, k, me).wait_recv()
            return ref

        for a, (p_ref, o_ref) in enumerate(zip(p_refs, o_refs)):
            rh = p_ref.shape[0] // 2
            send(p_ref.at[pl.ds(c * rh, rh)], o_ref.at[s_me, pl.ds(c * rh, rh)], NS * a, xn)
            send(p_ref.at[pl.ds(c * rh, rh)], o_ref.at[s_me, pl.ds(c * rh, rh)], NS * a + 1, yn)
        for a, o_ref in enumerate(o_refs):
            rh = o_ref.shape[1] // 2
            rq = rh // 2
            q0, q1 = pl.ds(c * rh, rq), pl.ds(c * rh + rq, rq)
            from_x = landed(o_ref.at[s_xn, pl.ds(c * rh, rh)], NS * a)
            send(o_ref.at[s_xn, q0], o_ref.at[s_xn, q0], NS * a + 2, yn)
            send(from_x, from_x, NS * a + 4, sib)
            from_y = landed(o_ref.at[s_yn, pl.ds(c * rh, rh)], NS * a + 1)
            send(o_ref.at[s_yn, q1], o_ref.at[s_yn, q1], NS * a + 3, xn)
            send(from_y, from_y, NS * a + 5, sib)
        for a, o_ref in enumerate(o_refs):
            rh = o_ref.shape[1] // 2
            rq = rh // 2
            d0 = landed(o_ref.at[s_dg, pl.ds(c * rh, rq)], NS * a + 2)
            send(d0, d0, NS * a + 6, sib)
            d1 = landed(o_ref.at[s_dg, pl.ds(c * rh + rq, rq)], NS * a + 3)
            send(d1, d1, NS * a + 7, sib)
        for a, o_ref in enumerate(o_refs):
            rh = o_ref.shape[1] // 2
            rq = rh // 2
            o = (1 - c) * rh
            landed(o_ref.at[s_xn, pl.ds(o, rh)], NS * a + 4)
            landed(o_ref.at[s_yn, pl.ds(o, rh)], NS * a + 5)
            landed(o_ref.at[s_dg, pl.ds(o, rq)], NS * a + 6)
            landed(o_ref.at[s_dg, pl.ds(o + rq, rq)], NS * a + 7)
        for cp in copies:
            cp.wait_send()

    for p in packs:
        assert p.shape[0] % (4 * SUBLANES_BF16) == 0
    return _comm_call(body, packs, [SDS((N_CHIPS,) + p.shape, p.dtype) for p in packs], NS * len(packs), name=name)


def _sibling_exchange(gs, *, name):
    def body(g_refs, o_refs, send_sems, recv_sems):
        x, y, c, _ = _position()
        sent = [_remote(g_ref.at[:, pl.ds((1 - c) * o_ref.shape[1], o_ref.shape[1])], o_ref, send_sems, recv_sems, a,
                        (x, y, 1 - c)) for a, (g_ref, o_ref) in enumerate(zip(g_refs, o_refs))]
        for cp in sent:
            cp.start()
        for cp in sent:
            cp.wait()

    return _comm_call(body, gs, [SDS((N_CHIPS, g.shape[1] // 2, g.shape[2]), g.dtype) for g in gs], len(gs), name=name)


def _scatter_hop1(ps, *, name):
    def body(p_refs, o_refs, send_sems, recv_sems):
        x, y, c, _ = _position()
        xn, yn = (1 - x, y, c), (x, 1 - y, c)
        s_xn, s_yn, s_dg = 2 * (1 - x) + y, 2 * x + 1 - y, 2 * (1 - x) + 1 - y
        sent = []
        for a, (p_ref, o_ref) in enumerate(zip(p_refs, o_refs)):
            rq = o_ref.shape[1]
            first, second = pl.ds(0, rq), pl.ds(rq, rq)
            sent += [_remote(p_ref.at[s_xn, second], o_ref.at[0], send_sems, recv_sems, 4 * a, xn),
                     _remote(p_ref.at[s_dg, second], o_ref.at[1], send_sems, recv_sems, 4 * a + 1, xn),
                     _remote(p_ref.at[s_yn, first], o_ref.at[2], send_sems, recv_sems, 4 * a + 2, yn),
                     _remote(p_ref.at[s_dg, first], o_ref.at[3], send_sems, recv_sems, 4 * a + 3, yn)]
        for cp in sent:
            cp.start()
        for cp in sent:
            cp.wait()

    return _comm_call(body, ps, [SDS((4, p.shape[1] // 2, p.shape[2]), p.dtype) for p in ps], 4 * len(ps), name=name)


def _scatter_hop2(fs, *, name):
    def body(f_refs, o_refs, send_sems, recv_sems):
        x, y, c, _ = _position()
        sent = []
        for a, (f_ref, o_ref) in enumerate(zip(f_refs, o_refs)):
            sent += [_remote(f_ref.at[0], o_ref.at[0], send_sems, recv_sems, 2 * a, (1 - x, y, c)),
                     _remote(f_ref.at[1], o_ref.at[1], send_sems, recv_sems, 2 * a + 1, (x, 1 - y, c))]
        for cp in sent:
            cp.start()
        for cp in sent:
            cp.wait()

    return _comm_call(body, fs, [SDS(f.shape, f.dtype) for f in fs], 2 * len(fs), name=name)


def _sibling_fill(rs, *, name):
    n = len(rs)

    def body(*refs):
        r_refs, send_sems, recv_sems = refs[n:2 * n], refs[-2], refs[-1]
        x, y, c, _ = _position()
        sent = []
        for a, r_ref in enumerate(r_refs):
            mine = r_ref.at[pl.ds(c * (r_ref.shape[0] // 2), r_ref.shape[0] // 2)]
            sent.append(_remote(mine, mine, send_sems, recv_sems, a, (x, y, 1 - c)))
            sent[-1].start()
        for a, r_ref in enumerate(r_refs):
            theirs = r_ref.at[pl.ds((1 - c) * (r_ref.shape[0] // 2), r_ref.shape[0] // 2)]
            _remote(theirs, theirs, send_sems, recv_sems, a, (x, y, c)).wait_recv()
        for cp in sent:
            cp.wait_send()

    return pl.pallas_call(
        body, name=name, in_specs=[ANY] * n, out_specs=[ANY] * n, out_shape=[SDS(r.shape, r.dtype) for r in rs],
        input_output_aliases={i: i for i in range(n)},
        scratch_shapes=[pltpu.SemaphoreType.DMA((n,)), pltpu.SemaphoreType.DMA((n,))],
    )(*rs)


def _gather_all(v, *, name):
    M, C = v.shape

    def body(v_ref, o_ref, send_sems, recv_sems):
        x, y, c, chips = _position()
        slot = lambda px, py, pc: o_ref.at[4 * px + 2 * py + pc]
        first = [_remote(v_ref, slot(x, y, c), send_sems, recv_sems, 0, (x, y, 1 - c))]
        first += [_remote(v_ref, slot(x, y, c), send_sems, recv_sems, 1 + j, (*chip, c)) for j, chip in enumerate(chips)]
        for cp in first:
            cp.start()
        passed = []
        for j, chip in enumerate(chips):
            landed = slot(*chip, c)
            _remote(landed, landed, send_sems, recv_sems, 1 + j, (x, y, c)).wait_recv()
            cp = _remote(landed, landed, send_sems, recv_sems, 4 + j, (x, y, 1 - c))
            cp.start()
            passed.append(cp)
        sib = slot(x, y, 1 - c)
        _remote(sib, sib, send_sems, recv_sems, 0, (x, y, c)).wait_recv()
        for j, chip in enumerate(chips):
            theirs = slot(*chip, 1 - c)
            _remote(theirs, theirs, send_sems, recv_sems, 4 + j, (x, y, c)).wait_recv()
        for cp in first + passed:
            cp.wait_send()

    return pl.pallas_call(
        body, name=name, in_specs=[ANY], out_specs=ANY, out_shape=SDS((N_DEV, M, C), v.dtype),
        scratch_shapes=[pltpu.SemaphoreType.DMA((7,)), pltpu.SemaphoreType.DMA((7,))],
    )(v)


BIG = ("ffn1_w1", "ffn2_w1", "w_in", "ffn1_w2", "ffn2_w2", "w_attn_branch", "w_sgu_branch", "w_out")
COL_SHARDED = ("ffn1_w1", "w_in", "ffn2_w1")
FFN_IN = ("ffn1_w1", "ffn2_w1")
SMALL = ("ffn1_pre_g", "ffn1_post_g", "mix_pre_g", "attn_sinks", "sgu_ln_g", "sgu_ln_b", "sgu_w", "sgu_b",
         "mix_post_g", "ffn2_pre_g", "ffn2_post_g")
WEIGHTS = ("ffn1_pre_g", "ffn1_w1", "ffn1_w2", "ffn1_post_g", "mix_pre_g", "w_in", "attn_sinks", "sgu_ln_g",
           "sgu_ln_b", "sgu_w", "sgu_b", "w_attn_branch", "w_sgu_branch", "w_out", "mix_post_g", "ffn2_pre_g",
           "ffn2_w1", "ffn2_w2", "ffn2_post_g")


def _column_chunks(w, tn):
    return jnp.swapaxes(w.reshape(w.shape[0], w.shape[1] // tn, tn), 0, 1)


def _width_classes(shard_shapes):
    widths = sorted({shard_shapes[n][-1] for n in BIG}, reverse=True)
    return [[n for n in BIG if shard_shapes[n][-1] == w] for w in widths]


def _class_rows(classes, shard_shapes, n_layers):
    where = {}
    for k, names in enumerate(classes):
        off = 0
        for n in names:
            r = shard_shapes[n][0]
            assert off % r == 0
            where[n] = (k, off, r)
            off += n_layers * r
    return where


def _ffn_fwd(x, pre_g, w1, w1_block, w2, post_g, tag):
    a, h = _norm_matmul(x, pre_g, w1, w1_block, name=f"{tag}_up", with_h=True)
    xn, o = _swiglu_out(a, w2, x, post_g, name=f"{tag}_down")
    return xn, (x, h, a, o)


def _ffn_bwd(dy, saved, pre_g, w1, w1_block, w2, post_g, dw1_into, dw2_into, tag):
    x, h, a, o = saved
    da, do, d_post = _ffn_bwd_hidden(dy, o, post_g, a, w2, name=f"{tag}_bwd_hidden")
    g2 = _dw_rows(a, do, *dw2_into, name=f"{tag}_dw2", swiglu=True)
    dx, d_pre = _matmul_nt_norm_bwd(da, w1, w1_block, x, pre_g, dy, None, name=f"{tag}_bwd_in")
    g1 = _dw_cols(h, da, N_CHIPS, *dw1_into, name=f"{tag}_dw1")
    return dx, g1, g2, d_pre, d_post


def kernel(x, ffn1_pre_g, ffn1_w1, ffn1_w2, ffn1_post_g, mix_pre_g, w_in, attn_sinks, sgu_ln_g, sgu_ln_b, sgu_w, sgu_b, w_attn_branch, w_sgu_branch, w_out, mix_post_g, ffn2_pre_g, ffn2_w1, ffn2_w2, ffn2_post_g, loss_target, m_ffn1_pre_g, m_ffn1_w1, m_ffn1_w2, m_ffn1_post_g, m_mix_pre_g, m_w_in, m_attn_sinks, m_sgu_ln_g, m_sgu_ln_b, m_sgu_w, m_sgu_b, m_w_attn_branch, m_w_sgu_branch, m_w_out, m_mix_post_g, m_ffn2_pre_g, m_ffn2_w1, m_ffn2_w2, m_ffn2_post_g, v_ffn1_pre_g, v_ffn1_w1, v_ffn1_w2, v_ffn1_post_g, v_mix_pre_g, v_w_in, v_attn_sinks, v_sgu_ln_g, v_sgu_ln_b, v_sgu_w, v_sgu_b, v_w_attn_branch, v_w_sgu_branch, v_w_out, v_mix_post_g, v_ffn2_pre_g, v_ffn2_w1, v_ffn2_w2, v_ffn2_post_g):
    given = dict(locals())
    W = {n: given[n] for n in WEIGHTS}
    M = {n: given["m_" + n] for n in WEIGHTS}
    V = {n: given["v_" + n] for n in WEIGHTS}
    L = ffn1_w1.shape[0]
    T, D = x.shape[1], x.shape[2]
    xt = x.reshape(T, D)
    target = loss_target.reshape(T, D)
    assert L % 2 == 0 and D == ATTN_WIDTH == SGU_WIDTH and T % ATTN_BLOCK == 0

    shard_shapes = {n: W[n].shape[1:] for n in BIG}
    classes = _width_classes(shard_shapes)
    my_chip = 2 * lax.axis_index("x") + lax.axis_index("y")
    my_core = lax.axis_index("c")
    where = _class_rows(classes, shard_shapes, L)
    packs = [jnp.concatenate([W[n].reshape(-1, shard_shapes[n][1]).astype(BF16) for n in names], axis=0)
             for names in classes]
    gathered = _gather_shards(packs, name="gather_weights")
    wcls = [lax.dynamic_update_slice(got, pack[None], (my_chip, 0, 0)) for pack, got in zip(packs, gathered)]

    def block_of(n, l):
        k, off, r = where[n]
        return k, off // r + l

    def chip_shards(n, l):
        k, off, r = where[n]
        return wcls[k][:, off + l * r:off + (l + 1) * r, :]

    full = []
    for l in range(L):
        fw = {n: chip_shards(n, l).reshape(-1, D) for n in BIG if n not in COL_SHARDED}
        w_in_l = jnp.swapaxes(chip_shards("w_in", l), 0, 1).reshape(D, -1)
        fw["w_qkv_t"] = w_in_l[:, :QKV_WIDTH].T
        fw["w_main"] = _column_chunks(w_in_l[:, QKV_WIDTH:], D)
        for n in FFN_IN:
            fw[n] = (wcls[block_of(n, l)[0]], block_of(n, l)[1])
        full.append(fw)

    row = lambda name, l: W[name][l].reshape(1, -1)
    causal = jnp.tril(jnp.ones((SGU_CHUNK, SGU_CHUNK), dtype=bool))
    saved = []
    h_cur = xt
    for l in range(L):
        fw = full[l]
        sv = {}
        h_cur, sv["ffn1"] = _ffn_fwd(h_cur, row("ffn1_pre_g", l), *fw["ffn1_w1"], fw["ffn1_w2"], row("ffn1_post_g", l),
                                     f"l{l}_ffn1")
        zqkv, hm = _norm_matmul_t(h_cur, row("mix_pre_g", l), fw["w_qkv_t"], name=f"l{l}_mix_in_qkv")
        zmain, = _norm_matmul(h_cur, row("mix_pre_g", l), fw["w_main"], 0, name=f"l{l}_mix_in_main", with_h=False)
        wm = jnp.where(causal[None], sgu_w[l], 0.0).astype(BF16)
        wmt = jnp.swapaxes(wm, 1, 2)
        bias = jnp.broadcast_to(sgu_b[l][:, :, None], (SGU_GROUPS, SGU_CHUNK, 128)).astype(F32)
        y_attn = _attn_fwd(zqkv, attn_sinks[l], name=f"l{l}_attn")
        y_sgu = _sgu_fwd(zmain, row("sgu_ln_g", l), row("sgu_ln_b", l), wm, bias, name=f"l{l}_sgu")
        x_mix = h_cur
        h_cur, pa, ps, mo = _merge_fwd(y_attn, y_sgu, zmain, fw["w_attn_branch"], fw["w_sgu_branch"], fw["w_out"],
                                       x_mix, row("mix_post_g", l), name=f"l{l}_merge")
        sv["mix"] = (x_mix, hm, zqkv, zmain, y_attn, y_sgu, pa, ps, mo, wm, wmt, bias)
        h_cur, sv["ffn2"] = _ffn_fwd(h_cur, row("ffn2_pre_g", l), *fw["ffn2_w1"], fw["ffn2_w2"], row("ffn2_post_g", l),
                                     f"l{l}_ffn2")
        saved.append(sv)

    dy, lsum = _loss_head(h_cur, target, name="loss_head")
    loss = lax.psum(0.5 * lsum[0, 0], AXES)

    k_in = where["w_in"][0]
    assert classes[k_in] == ["w_in"]
    gcls = [None if k == k_in else lax.empty((N_CHIPS,) + p.shape, F32) for k, p in enumerate(packs)]
    dw_in = [None] * L
    small_grads = [None] * L

    def into(n, l):
        return gcls[block_of(n, l)[0]], block_of(n, l)[1]

    def ffn_bwd(dy, which, l):
        n1, n2 = f"{which}_w1", f"{which}_w2"
        dy, g1, g2, d_pre, d_post = _ffn_bwd(
            dy, saved[l][which], row(f"{which}_pre_g", l), *full[l][n1], full[l][n2], row(f"{which}_post_g", l),
            into(n1, l), into(n2, l), f"l{l}_{which}")
        gcls[where[n1][0]], gcls[where[n2][0]] = g1, g2
        return dy, d_pre, d_post

    for l in reversed(range(L)):
        fw, sv = full[l], saved[l]
        gs = {}
        dy, gs["ffn2_pre_g"], gs["ffn2_post_g"] = ffn_bwd(dy, "ffn2", l)

        x_mix, hm, zqkv, zmain, y_attn, y_sgu, pa, ps, mo, wm, wmt, bias = sv["mix"]
        dzmain, dout, merged, dpa, dps, dya, dys, gs["mix_post_g"] = _merge_bwd(
            dy, mo, row("mix_post_g", l), pa, ps, zmain, fw["w_attn_branch"], fw["w_sgu_branch"], fw["w_out"],
            name=f"l{l}_merge_bwd")
        k_sq = where["w_out"][0]
        gcls[k_sq] = _dw_rows(merged, dout, *into("w_out", l), name=f"l{l}_dw_out")
        gcls[k_sq] = _dw_rows(y_attn, dpa, *into("w_attn_branch", l), name=f"l{l}_dw_attn", a_feature_major=True)
        gcls[k_sq] = _dw_rows(y_sgu, dps, *into("w_sgu_branch", l), name=f"l{l}_dw_sgu")
        dzqkv, dsink = _attn_bwd(zqkv, attn_sinks[l], dya, name=f"l{l}_attn_bwd")
        gs["attn_sinks"] = dsink[0, :N_Q_HEADS]
        dzmain, dsw, dsb, gs["sgu_ln_g"], gs["sgu_ln_b"] = _sgu_bwd(
            zmain, dzmain, dys, row("sgu_ln_g", l), row("sgu_ln_b", l), wm, wmt, bias, name=f"l{l}_sgu_bwd")
        gs["sgu_w"] = dsw
        gs["sgu_b"] = dsb[:, :SGU_GROUPS].T
        dh_qkv = _matmul_tn_rows(dzqkv, fw["w_qkv_t"], name=f"l{l}_mix_bwd_qkv")
        dy, gs["mix_pre_g"] = _matmul_nt_norm_bwd(dzmain, fw["w_main"], 0, x_mix, row("mix_pre_g", l), dy, dh_qkv,
                                                   name=f"l{l}_mix_bwd_in")
        dw_main = _dw_cols(hm, dzmain, zmain.shape[1] // D, None, 0, name=f"l{l}_dw_in_main")
        dw_in[l] = jnp.concatenate([_matmul_tokens(dzqkv, hm, name=f"l{l}_dw_in_qkv").T,
                                    jnp.swapaxes(dw_main, 0, 1).reshape(D, -1)], axis=1)

        dy, gs["ffn1_pre_g"], gs["ffn1_post_g"] = ffn_bwd(dy, "ffn1", l)
        small_grads[l] = gs
    grad_x = dy.reshape(x.shape)

    w_in_width = shard_shapes["w_in"][1]
    gcls[k_in] = jnp.stack([jnp.concatenate([g[:, s * w_in_width:(s + 1) * w_in_width] for g in dw_in], axis=0)
                            for s in range(N_CHIPS)])
    grs = gcls
    from_sibling = _sibling_exchange(grs, name="grads_sibling_exchange")
    zero = lambda a: 0
    own = lambda a: a
    core = lambda a: lax.axis_index("c")
    chip = lambda a: 2 * lax.axis_index("x") + lax.axis_index("y")
    chip_xn = lambda a: 2 * (1 - lax.axis_index("x")) + lax.axis_index("y")
    chip_yn = lambda a: 2 * lax.axis_index("x") + 1 - lax.axis_index("y")
    pairs = [_sum_terms([(g, own, core), (fs, own, zero)], fs.shape[1], N_CHIPS, (BF16,),
                        name=f"grads_pair_sum{k}")[0] for k, (g, fs) in enumerate(zip(grs, from_sibling))]
    hop1 = _scatter_hop1(pairs, name="grads_scatter_hop1")
    relay = [_sum_terms([(p, lambda a: chip_xn(a) + a * (chip_yn(a) - chip_xn(a)), own), (h, lambda a: 3 - 2 * a, zero)],
                        h.shape[1], 2, (BF16,), name=f"grads_relay_sum{k}")[0]
             for k, (p, h) in enumerate(zip(pairs, hop1))]
    hop2 = _scatter_hop2(relay, name="grads_scatter_hop2")
    quarter = lambda a: 2 * core(a) + a
    halves = [_sum_terms([(g, chip, quarter), (fs, chip, own), (h1, lambda a: 2 - 2 * a, zero), (h2, own, zero)],
                         h1.shape[1], 2, (F32,), name=f"grads_chip_sum{k}", out_lead=quarter, out_n_lead=4)[0]
              .reshape(g.shape[1:]) for k, (g, fs, h1, h2) in enumerate(zip(grs, from_sibling, hop1, hop2))]
    reduced_all = _sibling_fill(halves, name="grads_sibling_fill")

    grads = {n: [None] * L for n in SMALL}
    for names, reduced in zip(classes, reduced_all):
        for n in names:
            _, off, r = where[n]
            grads[n] = reduced[off:off + L * r].reshape((L,) + shard_shapes[n])

    def small_rows(gs):
        parts = []
        for n in SMALL:
            flat = gs[n].reshape(-1)
            pad = (-flat.shape[0]) % D
            parts.append(jnp.pad(flat, (0, pad)).reshape(-1, D))
        return jnp.concatenate(parts, axis=0)

    spack = jnp.concatenate([small_rows(small_grads[l]) for l in range(L)], axis=0)
    n_small = spack.shape[0]
    pad_rows = (-n_small) % SUBLANES_BF16
    spack = jnp.pad(spack, ((0, pad_rows), (0, 0)))
    everyone = _gather_all(spack, name="small_grads_gather")
    is_me = (jnp.arange(N_DEV) == 2 * my_chip + my_core)[:, None, None]
    everyone = jnp.where(is_me, spack[None], everyone)
    ssum = _sum_terms([(everyone, (lambda a, d=d: d), zero) for d in range(N_DEV)], spack.shape[0], 1, (F32,),
                      name="small_grads_sum")[0][0]
    per_layer = n_small // L
    for l in range(L):
        r0 = l * per_layer
        for n in SMALL:
            shp = W[n].shape[1:]
            size = math.prod(shp)
            nr = -(-size // D)
            grads[n][l] = ssum[r0:r0 + nr].reshape(-1)[:size].reshape(shp)
            r0 += nr
    grads.update({n: jnp.stack(grads[n]) for n in SMALL})

    delta, new_m, new_v = {}, {}, {}
    for n in WEIGHTS:
        delta[n], new_m[n], new_v[n] = _adamw(W[n], grads[n], M[n], V[n], name=f"adamw_{n}")

    return (loss, grad_x, *[grads[n] for n in WEIGHTS], *[delta[n] for n in WEIGHTS],
            *[new_m[n] for n in WEIGHTS], *[new_v[n] for n in WEIGHTS])
```

```python
import functools
import math

import jax
import jax.numpy as jnp
from jax import lax
from jax.experimental import pallas as pl
from jax.experimental.pallas import tpu as pltpu

F32, BF16 = jnp.float32, jnp.bfloat16
SDS = jax.ShapeDtypeStruct
MESH = pl.DeviceIdType.MESH
AXES = ("x", "y", "c")

HEAD_DIM = 64
N_Q_HEADS = 16
N_KV_HEADS = 2
Q_PER_KV = N_Q_HEADS // N_KV_HEADS
ATTN_WIDTH = N_Q_HEADS * HEAD_DIM
KV_WIDTH = N_KV_HEADS * HEAD_DIM
ATTN_BLOCK = 128
SGU_CHUNK = 128
SGU_GROUPS = 8
SGU_WIDTH = SGU_GROUPS * 128
QKV_WIDTH = ATTN_WIDTH + 2 * KV_WIDTH
RMS_EPS = 1e-6
LN_EPS = 1e-5
MASK_VALUE = -1e30
ATTN_SCALE = 1.0 / math.sqrt(HEAD_DIM)

ADAM_LR, ADAM_B1, ADAM_B2, ADAM_EPS, ADAM_WD, ADAM_STEP = 0.001, 0.9, 0.999, 1e-08, 0.01, 10

N_CHIPS = 4
N_DEV = 8

VMEM_LIMIT_BYTES = 56 * 1024 * 1024
LANES = 128
SUBLANES_BF16 = 16

TM_NORM_MATMUL = 1024
TM_ROW = 512
TM_FFN_BWD = 512
TT_REDUCE = 1024
TQ_ATTN = 1024
TM_FEATURE_MAJOR = 1024
TS_SGU = 512


def _tile(n, pref, mult):
    t = (min(pref, n) // mult) * mult
    while t >= mult:
        if n % t == 0:
            return t
        t -= mult
    return n


def _params(*sem):
    return pltpu.CompilerParams(dimension_semantics=sem, vmem_limit_bytes=VMEM_LIMIT_BYTES)


def _dot(a, b):
    return jnp.dot(a, b, preferred_element_type=F32)


def _dot_nt(a, b):
    return lax.dot_general(a, b, (((1,), (1,)), ((), ())), preferred_element_type=F32)


def _dot_tn(a, b):
    return lax.dot_general(a, b, (((0,), (0,)), ((), ())), preferred_element_type=F32)


def _sigmoid(x):
    return 0.5 * (1.0 + jnp.tanh(0.5 * x))


def _rms_stats(xf):
    r = lax.rsqrt(jnp.mean(xf * xf, axis=-1, keepdims=True) + RMS_EPS)
    return r, xf * r


def _rms_bwd(xf, g, dy):
    r, xh = _rms_stats(xf)
    dyg = dy * g
    dx = r * (dyg - xh * jnp.mean(dyg * xh, axis=-1, keepdims=True))
    return dx, jnp.sum(dy * xh, axis=0, keepdims=True)


def _gelu_parts(x):
    cdf = 0.5 * (1.0 + lax.erf(x * (1.0 / math.sqrt(2.0))))
    return cdf


def _gelu(x):
    return x * _gelu_parts(x)


def _gelu_grad(x):
    return _gelu_parts(x) + x * jnp.exp(-0.5 * x * x) * (1.0 / math.sqrt(2.0 * math.pi))


def _resident(shape, index=None):
    index = (0,) * len(shape) if index is None else index
    return pl.BlockSpec(shape, lambda *_: index, pipeline_mode=pl.Buffered(1))


def _norm_matmul(x, g, w3, w_block, *, name, with_h, gather=()):
    T, D = x.shape
    nj, _, tn = w3.shape
    tm = _tile(T, TM_NORM_MATMUL, SUBLANES_BF16)
    ni, ng = T // tm, len(gather)
    n_out = 1 + with_h

    def body(x_ref, g_ref, w_ref, *rest):
        a_ref, h_sc = rest[ng], rest[ng + n_out + ng]
        i, j = pl.program_id(0), pl.program_id(1)
        if ng:
            stages = _gather_stages(rest[:ng], rest[ng + n_out:ng + n_out + ng], *rest[-2:])
            for stage, (si, sj) in zip(stages[:2], ((0, 0), (ni // 2, 0))):
                pl.when((i == si) & (j == sj))(stage)

        @pl.when(j == 0)
        def _():
            _, xh = _rms_stats(x_ref[...])
            h = (xh * g_ref[...]).astype(BF16)
            h_sc[...] = h
            if with_h:
                rest[ng + 1][...] = h

        a_ref[...] = _dot(h_sc[...], w_ref[j]).astype(BF16)
        if ng:
            pl.when((i == ni - 1) & (j == nj - 1))(stages[2])

    out_specs = [pl.BlockSpec((tm, tn), lambda i, j: (i, j))]
    out_shape = [SDS((T, nj * tn), BF16)]
    if with_h:
        out_specs.append(pl.BlockSpec((tm, D), lambda i, j: (i, 0)))
        out_shape.append(SDS((T, D), BF16))
    scratch = [pltpu.VMEM((tm, D), BF16)]
    if ng:
        scratch += [pltpu.SemaphoreType.DMA((GATHER_SEMS * ng,))] * 2
    return pl.pallas_call(
        body, name=name, grid=(ni, nj),
        in_specs=[pl.BlockSpec((tm, D), lambda i, j: (i, 0)),
                  pl.BlockSpec((1, D), lambda i, j: (0, 0)),
                  _resident((nj, D, tn), (0, w_block, 0))] + [ANY] * ng,
        out_specs=out_specs + [ANY] * ng, out_shape=out_shape + _gathered_shapes(gather),
        scratch_shapes=scratch,
        compiler_params=_params(*(("arbitrary", "arbitrary") if ng else ("parallel", "arbitrary"))),
    )(x, g, w3, *gather)


def _norm_matmul_t(x, g, wt, *, name):
    T, D = x.shape
    N = wt.shape[0]
    tm = _tile(T, TM_FEATURE_MAJOR, LANES)

    def body(x_ref, g_ref, w_ref, a_ref, h_ref):
        _, xh = _rms_stats(x_ref[...])
        h = (xh * g_ref[...]).astype(BF16)
        h_ref[...] = h
        a_ref[...] = _dot_nt(w_ref[...], h).astype(BF16)

    return pl.pallas_call(
        body, name=name, grid=(T // tm,),
        in_specs=[pl.BlockSpec((tm, D), lambda i: (i, 0)), pl.BlockSpec((1, D), lambda i: (0, 0)),
                  _resident((N, D))],
        out_specs=[pl.BlockSpec((N, tm), lambda i: (0, i)), pl.BlockSpec((tm, D), lambda i: (i, 0))],
        out_shape=[SDS((N, T), BF16), SDS((T, D), BF16)],
        compiler_params=_params("parallel"),
    )(x, g, wt)


def _matmul_tokens(at, b, *, name):
    K, T = at.shape
    N = b.shape[1]
    tt = _tile(T, TT_REDUCE, LANES)

    def body(a_ref, b_ref, o_ref):
        @pl.when(pl.program_id(0) == 0)
        def _():
            o_ref[...] = jnp.zeros_like(o_ref)

        o_ref[...] += _dot(a_ref[...], b_ref[...])

    return pl.pallas_call(
        body, name=name, grid=(T // tt,),
        in_specs=[pl.BlockSpec((K, tt), lambda t: (0, t)), pl.BlockSpec((tt, N), lambda t: (t, 0))],
        out_specs=pl.BlockSpec((K, N), lambda t: (0, 0)),
        out_shape=SDS((K, N), F32),
        compiler_params=_params("arbitrary"),
    )(at, b)


def _matmul_tn_rows(dat, wt, *, name):
    N, T = dat.shape
    D = wt.shape[1]
    tm = _tile(T, TM_FEATURE_MAJOR, LANES)

    def body(da_ref, w_ref, o_ref):
        o_ref[...] = _dot_tn(da_ref[...], w_ref[...])

    return pl.pallas_call(
        body, name=name, grid=(T // tm,),
        in_specs=[pl.BlockSpec((N, tm), lambda i: (0, i)), _resident((N, D))],
        out_specs=pl.BlockSpec((tm, D), lambda i: (i, 0)),
        out_shape=SDS((T, D), F32),
        compiler_params=_params("parallel"),
    )(dat, wt)


def _ff_chunk(F):
    return F if F <= 1408 else F // 2


def _swiglu_out(a, w2, x, g_post, *, name):
    T, F2 = a.shape
    F = F2 // 2
    D = x.shape[1]
    tm = _tile(T, TM_ROW, SUBLANES_BF16)
    fc = _ff_chunk(F)

    def body(a_ref, w_ref, x_ref, g_ref, xn_ref, o_ref):
        acc = None
        for c0 in range(0, F, fc):
            gt = a_ref[:, c0:c0 + fc].astype(F32)
            s = (gt * _sigmoid(gt)).astype(BF16) * a_ref[:, F + c0:F + c0 + fc]
            part = _dot(s, w_ref[c0:c0 + fc, :])
            acc = part if acc is None else acc + part
        o_ref[...] = acc.astype(BF16)
        _, oh = _rms_stats(acc)
        xn_ref[...] = x_ref[...] + 0.5 * (oh * g_ref[...])

    return pl.pallas_call(
        body, name=name, grid=(T // tm,),
        in_specs=[pl.BlockSpec((tm, F2), lambda i: (i, 0)),
                  _resident((F, D)),
                  pl.BlockSpec((tm, D), lambda i: (i, 0)),
                  pl.BlockSpec((1, D), lambda i: (0, 0))],
        out_specs=[pl.BlockSpec((tm, D), lambda i: (i, 0)), pl.BlockSpec((tm, D), lambda i: (i, 0))],
        out_shape=[SDS((T, D), F32), SDS((T, D), BF16)],
        compiler_params=_params("parallel"),
    )(a, w2, x, g_post)


def _ffn_bwd_hidden(dy, o, g_post, a, w2, *, name):
    T, F2 = a.shape
    F = F2 // 2
    D = dy.shape[1]
    tm = _tile(T, TM_FFN_BWD, SUBLANES_BF16)
    fc = _tile(F, 256, LANES)

    def body(dy_ref, o_ref, g_ref, a_ref, w_ref, da_ref, do_ref, dg_ref):
        @pl.when(pl.program_id(0) == 0)
        def _():
            dg_ref[...] = jnp.zeros_like(dg_ref)

        do, dg = _rms_bwd(o_ref[...].astype(F32), g_ref[...], 0.5 * dy_ref[...])
        dg_ref[...] += dg
        dob = do.astype(BF16)
        do_ref[...] = dob
        for c0 in range(0, F, fc):
            ds = _dot_nt(dob, w_ref[c0:c0 + fc, :]).astype(BF16)
            gt = a_ref[:, c0:c0 + fc].astype(F32)
            ub = a_ref[:, F + c0:F + c0 + fc]
            sg = _sigmoid(gt)
            sl = gt * sg
            dsl = (sg + sl * (1.0 - sg)).astype(BF16)
            da_ref[:, c0:c0 + fc] = ds * ub * dsl
            da_ref[:, F + c0:F + c0 + fc] = ds * sl.astype(BF16)

    row = lambda w: pl.BlockSpec((tm, w), lambda i: (i, 0))
    return pl.pallas_call(
        body, name=name, grid=(T // tm,),
        in_specs=[row(D), row(D), pl.BlockSpec((1, D), lambda i: (0, 0)), row(F2),
                  _resident((F, D))],
        out_specs=[row(F2), row(D), pl.BlockSpec((1, D), lambda i: (0, 0))],
        out_shape=[SDS((T, F2), BF16), SDS((T, D), BF16), SDS((1, D), F32)],
        compiler_params=_params("arbitrary"),
    )(dy, o, g_post, a, w2)


def _dw_call(body, name, grid, in_specs, args, block, pack, row_block, sem):
    if pack is None:
        out_spec = pl.BlockSpec(block, lambda *_: (0, 0, 0), pipeline_mode=pl.Buffered(1))
        return pl.pallas_call(body, name=name, grid=grid, in_specs=in_specs, out_specs=out_spec,
                              out_shape=SDS(block, F32), compiler_params=_params(*sem))(*args)
    assert pack.shape[0] == block[0] and pack.shape[2] == block[2]
    out_spec = pl.BlockSpec(block, lambda *_: (0, row_block, 0), pipeline_mode=pl.Buffered(1))
    return pl.pallas_call(body, name=name, grid=grid, in_specs=in_specs + [ANY], out_specs=out_spec,
                          out_shape=SDS(pack.shape, F32), input_output_aliases={len(args): 0},
                          compiler_params=_params(*sem))(*args, pack)


def _dw_cols(a, b, n_chunks, pack, row_block, *, name):
    T, K = a.shape
    tn = b.shape[1] // n_chunks
    tt = _tile(T, TT_REDUCE, SUBLANES_BF16)
    per = 2 if n_chunks % 2 == 0 else 1

    def body(a_ref, b_ref, *rest):
        o_ref = rest[-1]

        @pl.when(pl.program_id(1) == 0)
        def _():
            o_ref[...] = jnp.zeros_like(o_ref)

        for p in range(per):
            o_ref[p] += _dot_tn(a_ref[...], b_ref[:, p * tn:(p + 1) * tn])

    grid = (n_chunks // per, T // tt)
    in_specs = [pl.BlockSpec((tt, K), lambda j, t: (t, 0)), pl.BlockSpec((tt, per * tn), lambda j, t: (t, j))]
    sem = ("parallel", "arbitrary")
    if pack is None:
        return pl.pallas_call(body, name=name, grid=grid, in_specs=in_specs,
                              out_specs=pl.BlockSpec((per, K, tn), lambda j, t: (j, 0, 0)),
                              out_shape=SDS((n_chunks, K, tn), F32), compiler_params=_params(*sem))(a, b)
    assert pack.shape[0] == n_chunks and pack.shape[2] == tn
    return pl.pallas_call(body, name=name, grid=grid, in_specs=in_specs + [ANY],
                          out_specs=pl.BlockSpec((per, K, tn), lambda j, t: (j, row_block, 0)),
                          out_shape=SDS(pack.shape, F32), input_output_aliases={2: 0},
                          compiler_params=_params(*sem))(a, b, pack)


def _dw_rows(a, b, pack, row_block, *, name, a_feature_major=False, swiglu=False):
    K, T = a.shape if a_feature_major else a.shape[::-1]
    K = K // 2 if swiglu else K
    N = b.shape[1]
    r = K // N_CHIPS
    cw = r if r % LANES == 0 else 2 * r
    assert cw % LANES == 0 and K % cw == 0 and r % 8 == 0
    tt = _tile(T, TT_REDUCE // 2 if swiglu else TT_REDUCE, LANES)

    def body(a_ref, b_ref, *rest):
        o_ref = rest[-1]

        @pl.when(pl.program_id(0) == 0)
        def _():
            o_ref[...] = jnp.zeros_like(o_ref)

        for c in range(K // cw):
            if a_feature_major:
                part = _dot(a_ref[c * cw:(c + 1) * cw, :], b_ref[...])
            elif swiglu:
                gt = a_ref[:, c * cw:(c + 1) * cw].astype(F32)
                part = _dot_tn((gt * _sigmoid(gt)).astype(BF16) * a_ref[:, K + c * cw:K + (c + 1) * cw], b_ref[...])
            else:
                part = _dot_tn(a_ref[:, c * cw:(c + 1) * cw], b_ref[...])
            for p in range(cw // r):
                o_ref[c * (cw // r) + p] += part[p * r:(p + 1) * r]

    a_spec = (pl.BlockSpec((K, tt), lambda t: (0, t)) if a_feature_major
              else pl.BlockSpec((tt, a.shape[1]), lambda t: (t, 0)))
    return _dw_call(body, name, (T // tt,), [a_spec, pl.BlockSpec((tt, N), lambda t: (t, 0))],
                    [a, b], (N_CHIPS, r, N), pack, row_block, ("arbitrary",))


def _matmul_nt_norm_bwd(da, w, w_block, x, g, dy, init, *, name):
    T, N = da.shape
    D = x.shape[1]
    nj, _, tn = w.shape
    tm = _tile(T, TM_ROW, SUBLANES_BF16)
    has_init = init is not None

    def body(da_ref, w_ref, x_ref, g_ref, dy_ref, *rest):
        dx_ref, dg_ref = rest[-2:]

        @pl.when(pl.program_id(0) == 0)
        def _():
            dg_ref[...] = jnp.zeros_like(dg_ref)

        dh = rest[0][...] if has_init else None
        for j in range(nj):
            part = _dot_nt(da_ref[:, j * tn:(j + 1) * tn], w_ref[j])
            dh = part if dh is None else dh + part
        dx, dg = _rms_bwd(x_ref[...], g_ref[...], dh)
        dx_ref[...] = dy_ref[...] + dx
        dg_ref[...] += dg

    row = pl.BlockSpec((tm, D), lambda i: (i, 0))
    vec = pl.BlockSpec((1, D), lambda i: (0, 0))
    in_specs = [pl.BlockSpec((tm, N), lambda i: (i, 0)), _resident((nj, D, tn), (0, w_block, 0)), row, vec, row]
    args = [da, w, x, g, dy]
    if has_init:
        in_specs.append(row)
        args.append(init)
    return pl.pallas_call(
        body, name=name, grid=(T // tm,), in_specs=in_specs,
        out_specs=[row, vec], out_shape=[SDS((T, D), F32), SDS((1, D), F32)],
        compiler_params=_params("arbitrary"),
    )(*args)


GROUP_LANES = Q_PER_KV * ATTN_BLOCK


def _attn_mask_t(first):
    kj = lax.broadcasted_iota(jnp.int32, (2 * ATTN_BLOCK, ATTN_BLOCK), 0)
    qi = lax.broadcasted_iota(jnp.int32, (2 * ATTN_BLOCK, ATTN_BLOCK), 1)
    rel = qi + ATTN_BLOCK - kj
    band = (rel >= 0) & (rel < ATTN_BLOCK)
    if first is False:
        return band
    return band & ((kj >= ATTN_BLOCK) | jnp.logical_not(first))


def _attn_probs_t(st, valid, sink):
    s = jnp.where(valid, st, MASK_VALUE)
    m = jnp.maximum(jnp.max(s, axis=0, keepdims=True), sink)
    p = jnp.exp(s - m)
    es = jnp.exp(sink - m)
    inv = 1.0 / (jnp.sum(p, axis=0, keepdims=True) + es)
    return p * inv, es * inv


def _attn_specs(tq, tile_of):
    nb = tq // ATTN_BLOCK
    krow, vrow = ATTN_WIDTH // KV_WIDTH, ATTN_WIDTH // KV_WIDTH + 1
    halo = lambda r: pl.BlockSpec((KV_WIDTH, ATTN_BLOCK), lambda t: (r, jnp.maximum(tile_of(t) * nb - 1, 0)))
    return [pl.BlockSpec((ATTN_WIDTH, tq), lambda t: (0, tile_of(t))),
            pl.BlockSpec((KV_WIDTH, tq), lambda t: (krow, tile_of(t))),
            pl.BlockSpec((KV_WIDTH, tq), lambda t: (vrow, tile_of(t))),
            halo(krow), halo(vrow)]


def _head_rows(g, r):
    h = g * Q_PER_KV + r
    return h, slice(h * HEAD_DIM, (h + 1) * HEAD_DIM)


def _group_stack(ref, g, cols):
    return jnp.concatenate([ref[_head_rows(g, r)[1], cols] for r in range(Q_PER_KV)], axis=1)


def _attn_fwd(zt, sinks, *, name):
    T = zt.shape[1]
    tq = _tile(T, TQ_ATTN, ATTN_BLOCK)
    nb = tq // ATTN_BLOCK

    def body(q_ref, k_ref, v_ref, kh_ref, vh_ref, s_ref, o_ref, kf, vf, pt):
        kf[:, 0:ATTN_BLOCK] = kh_ref[...]
        kf[:, ATTN_BLOCK:] = k_ref[...]
        vf[:, 0:ATTN_BLOCK] = vh_ref[...]
        vf[:, ATTN_BLOCK:] = v_ref[...]
        for b in range(nb):
            cols = slice(b * ATTN_BLOCK, (b + 1) * ATTN_BLOCK)
            win = slice(b * ATTN_BLOCK, (b + 2) * ATTN_BLOCK)
            valid = _attn_mask_t((pl.program_id(0) == 0) if b == 0 else False)
            for g in range(N_KV_HEADS):
                gr = slice(g * HEAD_DIM, (g + 1) * HEAD_DIM)
                st = _dot_tn(kf[gr, win], _group_stack(q_ref, g, cols)) * ATTN_SCALE
                for r in range(Q_PER_KV):
                    h, _ = _head_rows(g, r)
                    sl = slice(r * ATTN_BLOCK, (r + 1) * ATTN_BLOCK)
                    probs, _ = _attn_probs_t(st[:, sl], valid, s_ref[h])
                    pt[:, sl] = probs.astype(BF16)
                ot = _dot(vf[gr, win], pt[...])
                for r in range(Q_PER_KV):
                    o_ref[_head_rows(g, r)[1], cols] = ot[:, r * ATTN_BLOCK:(r + 1) * ATTN_BLOCK].astype(BF16)

    return pl.pallas_call(
        body, name=name, grid=(T // tq,),
        in_specs=_attn_specs(tq, lambda t: t) + [pl.BlockSpec(memory_space=pltpu.SMEM)],
        out_specs=pl.BlockSpec((ATTN_WIDTH, tq), lambda t: (0, t)),
        out_shape=SDS((ATTN_WIDTH, T), BF16),
        scratch_shapes=[pltpu.VMEM((KV_WIDTH, tq + ATTN_BLOCK), BF16)] * 2
        + [pltpu.VMEM((2 * ATTN_BLOCK, GROUP_LANES), BF16)],
        compiler_params=_params("parallel"),
    )(zt, zt, zt, zt, zt, sinks)


def _attn_bwd(zt, sinks, dot_, *, name):
    T = zt.shape[1]
    tq = _tile(T, TQ_ATTN, ATTN_BLOCK)
    nb = tq // ATTN_BLOCK
    nt = T // tq
    tile_of = lambda t: nt - 1 - t

    def body(q_ref, k_ref, v_ref, kh_ref, vh_ref, do_ref, s_ref, dz_ref, dsink_ref, kf, vf, dkf, dvf, carry, pt, dst):
        t = pl.program_id(0)

        @pl.when(t == 0)
        def _():
            carry[...] = jnp.zeros_like(carry)
            dsink_ref[...] = jnp.zeros_like(dsink_ref)

        kf[:, 0:ATTN_BLOCK] = kh_ref[...]
        kf[:, ATTN_BLOCK:] = k_ref[...]
        vf[:, 0:ATTN_BLOCK] = vh_ref[...]
        vf[:, ATTN_BLOCK:] = v_ref[...]
        dkf[...] = jnp.zeros_like(dkf)
        dvf[...] = jnp.zeros_like(dvf)
        dkf[:, tq:] = carry[0:KV_WIDTH, :]
        dvf[:, tq:] = carry[KV_WIDTH:, :]
        lane = lax.broadcasted_iota(jnp.int32, (1, LANES), 1)
        dsink = jnp.zeros((1, LANES), F32)
        for b in range(nb):
            cols = slice(b * ATTN_BLOCK, (b + 1) * ATTN_BLOCK)
            win = slice(b * ATTN_BLOCK, (b + 2) * ATTN_BLOCK)
            valid = _attn_mask_t((t == nt - 1) if b == 0 else False)
            for g in range(N_KV_HEADS):
                gr = slice(g * HEAD_DIM, (g + 1) * HEAD_DIM)
                kt2, vt2 = kf[gr, win], vf[gr, win]
                qst = _group_stack(q_ref, g, cols)
                dost = _group_stack(do_ref, g, cols)
                st = _dot_tn(kt2, qst) * ATTN_SCALE
                dpt = _dot_tn(vt2, dost)
                for r in range(Q_PER_KV):
                    h, _ = _head_rows(g, r)
                    sl = slice(r * ATTN_BLOCK, (r + 1) * ATTN_BLOCK)
                    probs, psink = _attn_probs_t(st[:, sl], valid, s_ref[h])
                    dp = dpt[:, sl]
                    delta = jnp.sum(probs * dp, axis=0, keepdims=True)
                    pt[:, sl] = probs.astype(BF16)
                    dst[:, sl] = (probs * (dp - delta)).astype(BF16)
                    dsink = dsink + jnp.where(lane == h, -jnp.sum(psink * delta), 0.0)
                dqt = _dot(kt2, dst[...]) * ATTN_SCALE
                for r in range(Q_PER_KV):
                    dz_ref[_head_rows(g, r)[1], cols] = dqt[:, r * ATTN_BLOCK:(r + 1) * ATTN_BLOCK].astype(BF16)
                dkf[gr, win] += _dot_nt(qst, dst[...]) * ATTN_SCALE
                dvf[gr, win] += _dot_nt(dost, pt[...])
        dz_ref[ATTN_WIDTH:ATTN_WIDTH + KV_WIDTH, :] = dkf[:, ATTN_BLOCK:].astype(BF16)
        dz_ref[ATTN_WIDTH + KV_WIDTH:, :] = dvf[:, ATTN_BLOCK:].astype(BF16)
        carry[0:KV_WIDTH, :] = dkf[:, 0:ATTN_BLOCK]
        carry[KV_WIDTH:, :] = dvf[:, 0:ATTN_BLOCK]
        dsink_ref[...] += dsink

    return pl.pallas_call(
        body, name=name, grid=(nt,),
        in_specs=_attn_specs(tq, tile_of) + [pl.BlockSpec((ATTN_WIDTH, tq), lambda t: (0, tile_of(t))),
                                             pl.BlockSpec(memory_space=pltpu.SMEM)],
        out_specs=[pl.BlockSpec((QKV_WIDTH, tq), lambda t: (0, tile_of(t))),
                   pl.BlockSpec((8, LANES), lambda t: (0, 0))],
        out_shape=[SDS((QKV_WIDTH, T), BF16), SDS((8, LANES), F32)],
        scratch_shapes=[pltpu.VMEM((KV_WIDTH, tq + ATTN_BLOCK), BF16)] * 2
        + [pltpu.VMEM((KV_WIDTH, tq + ATTN_BLOCK), F32)] * 2 + [pltpu.VMEM((2 * KV_WIDTH, ATTN_BLOCK), F32)]
        + [pltpu.VMEM((2 * ATTN_BLOCK, GROUP_LANES), BF16)] * 2,
        compiler_params=_params("arbitrary"),
    )(zt, zt, zt, zt, zt, dot_, sinks)


def _layer_norm_stats(v):
    mu = jnp.mean(v, axis=-1, keepdims=True)
    xc = v - mu
    rstd = lax.rsqrt(jnp.mean(xc * xc, axis=-1, keepdims=True) + LN_EPS)
    return rstd, xc * rstd


def _sgu_fwd(zmain, ln_g, ln_b, wm, bias, *, name):
    T = zmain.shape[0]
    ts = _tile(T, TS_SGU, SGU_CHUNK)

    def body(u_ref, v_ref, g_ref, b_ref, w_ref, bias_ref, y_ref):
        u = _gelu(u_ref[...].astype(F32))
        _, vh = _layer_norm_stats(_gelu(v_ref[...].astype(F32)))
        vn = (vh * g_ref[...] + b_ref[...]).astype(BF16)
        for ch in range(ts // SGU_CHUNK):
            rows = slice(ch * SGU_CHUNK, (ch + 1) * SGU_CHUNK)
            for g in range(SGU_GROUPS):
                cols = slice(g * 128, (g + 1) * 128)
                s = _dot(w_ref[g], vn[rows, cols]) + bias_ref[g]
                y_ref[rows, cols] = (u[rows, cols] * s).astype(BF16)

    full = _resident
    return pl.pallas_call(
        body, name=name, grid=(T // ts,),
        in_specs=[pl.BlockSpec((ts, SGU_WIDTH), lambda i: (i, 0)), pl.BlockSpec((ts, SGU_WIDTH), lambda i: (i, 1)),
                  full((1, SGU_WIDTH)), full((1, SGU_WIDTH)), full(wm.shape), full(bias.shape)],
        out_specs=pl.BlockSpec((ts, SGU_WIDTH), lambda i: (i, 0)),
        out_shape=SDS((T, SGU_WIDTH), BF16),
        compiler_params=_params("parallel"),
    )(zmain, zmain, ln_g, ln_b, wm, bias)


def _sgu_bwd(zmain, dzmain, dy, ln_g, ln_b, wm, wmt, bias, *, name):
    T = zmain.shape[0]
    ts = _tile(T, TS_SGU, SGU_CHUNK)

    def body(u_ref, v_ref, dy_ref, g_ref, b_ref, w_ref, wt_ref, bias_ref, _, dz_ref, dw_ref, db_ref, dlg_ref, dlb_ref,
             dvn):
        @pl.when(pl.program_id(0) == 0)
        def _():
            dw_ref[...] = jnp.zeros_like(dw_ref)
            db_ref[...] = jnp.zeros_like(db_ref)
            dlg_ref[...] = jnp.zeros_like(dlg_ref)
            dlb_ref[...] = jnp.zeros_like(dlb_ref)

        us = u_ref[...].astype(F32)
        vs = v_ref[...].astype(F32)
        u = _gelu(us)
        rstd, vh = _layer_norm_stats(_gelu(vs))
        vn = (vh * g_ref[...] + b_ref[...]).astype(BF16)
        causal = (lax.broadcasted_iota(jnp.int32, (SGU_CHUNK, SGU_CHUNK), 0)
                  >= lax.broadcasted_iota(jnp.int32, (SGU_CHUNK, SGU_CHUNK), 1))
        lane = lax.broadcasted_iota(jnp.int32, (SGU_CHUNK, LANES), 1)
        db = jnp.zeros((SGU_CHUNK, LANES), F32)
        for ch in range(ts // SGU_CHUNK):
            rows = slice(ch * SGU_CHUNK, (ch + 1) * SGU_CHUNK)
            for g in range(SGU_GROUPS):
                cols = slice(g * 128, (g + 1) * 128)
                vng = vn[rows, cols]
                s = _dot(w_ref[g], vng) + bias_ref[g]
                dyf = dy_ref[rows, cols].astype(F32)
                dz_ref[rows, cols] = (dyf * s * _gelu_grad(us[rows, cols])).astype(BF16)
                dsf = dyf * u[rows, cols]
                dsb = dsf.astype(BF16)
                dvn[rows, cols] = _dot(wt_ref[g], dsb)
                dw_ref[g] += jnp.where(causal, _dot_nt(dsb, vng), 0.0)
                db = db + jnp.where(lane == g, jnp.sum(dsf, axis=1, keepdims=True), 0.0)
        db_ref[...] += db
        dvnf = dvn[...]
        dlg_ref[...] += jnp.sum(dvnf * vh, axis=0, keepdims=True)
        dlb_ref[...] += jnp.sum(dvnf, axis=0, keepdims=True)
        dvh = dvnf * g_ref[...]
        dv = rstd * (dvh - jnp.mean(dvh, axis=-1, keepdims=True) - vh * jnp.mean(dvh * vh, axis=-1, keepdims=True))
        dz_ref[:, SGU_WIDTH:] = (dv * _gelu_grad(vs)).astype(BF16)

    full = _resident
    vec = full((1, SGU_WIDTH))
    acc = lambda shape: pl.BlockSpec(shape, lambda i: (0,) * len(shape))
    return pl.pallas_call(
        body, name=name, grid=(T // ts,),
        in_specs=[pl.BlockSpec((ts, SGU_WIDTH), lambda i: (i, 0)), pl.BlockSpec((ts, SGU_WIDTH), lambda i: (i, 1)),
                  pl.BlockSpec((ts, SGU_WIDTH), lambda i: (i, 0)), vec, vec, full(wm.shape), full(wm.shape),
                  full(bias.shape), pl.BlockSpec(memory_space=pl.ANY)],
        out_specs=[pl.BlockSpec((ts, 2 * SGU_WIDTH), lambda i: (i, 0)), acc(wm.shape),
                   acc((SGU_CHUNK, LANES)), acc((1, SGU_WIDTH)), acc((1, SGU_WIDTH))],
        out_shape=[SDS(dzmain.shape, BF16), SDS(wm.shape, F32), SDS((SGU_CHUNK, LANES), F32),
                   SDS((1, SGU_WIDTH), F32), SDS((1, SGU_WIDTH), F32)],
        scratch_shapes=[pltpu.VMEM((ts, SGU_WIDTH), F32)],
        input_output_aliases={8: 0},
        compiler_params=_params("arbitrary"),
    )(zmain, zmain, dy, ln_g, ln_b, wm, wmt, bias, dzmain)


def _merge_fwd(y_attn_t, y_sgu, zmain, w_a, w_s, w_o, x, g_post, *, name):
    T, D = x.shape
    tm = _tile(T, TM_ROW, LANES)

    def body(ya_ref, ys_ref, ga_ref, gb_ref, wa_ref, ws_ref, wo_ref, x_ref, g_ref, xn_ref, pa_ref, ps_ref, o_ref):
        pa = _dot_tn(ya_ref[...], wa_ref[...])
        ps = _dot(ys_ref[...], ws_ref[...])
        pa_ref[...] = pa.astype(BF16)
        ps_ref[...] = ps.astype(BF16)
        merged = _sigmoid(ga_ref[...].astype(F32)) * pa + _sigmoid(gb_ref[...].astype(F32)) * ps
        out = _dot(merged.astype(BF16), wo_ref[...])
        o_ref[...] = out.astype(BF16)
        _, oh = _rms_stats(out)
        xn_ref[...] = x_ref[...] + oh * g_ref[...]

    row = lambda col: pl.BlockSpec((tm, D), lambda i: (i, col))
    wfull = _resident((D, D))
    return pl.pallas_call(
        body, name=name, grid=(T // tm,),
        in_specs=[pl.BlockSpec((D, tm), lambda i: (0, i)), row(0), row(2), row(3), wfull, wfull, wfull, row(0),
                  pl.BlockSpec((1, D), lambda i: (0, 0))],
        out_specs=[row(0)] * 4,
        out_shape=[SDS((T, D), F32), SDS((T, D), BF16), SDS((T, D), BF16), SDS((T, D), BF16)],
        compiler_params=_params("parallel"),
    )(y_attn_t, y_sgu, zmain, zmain, w_a, w_s, w_o, x, g_post)


def _merge_bwd(dy, out, g_post, pa, ps, zmain, w_a, w_s, w_o, *, name):
    T, D = dy.shape
    tm = _tile(T, TM_ROW, LANES)

    def body(dy_ref, o_ref, g_ref, pa_ref, ps_ref, ga_ref, gb_ref, wa_ref, ws_ref, wo_ref,
             dz_ref, dout_ref, mg_ref, dpa_ref, dps_ref, dya_ref, dys_ref, dg_ref):
        @pl.when(pl.program_id(0) == 0)
        def _():
            dg_ref[...] = jnp.zeros_like(dg_ref)

        dout, dg = _rms_bwd(o_ref[...].astype(F32), g_ref[...], dy_ref[...])
        dg_ref[...] += dg
        doutb = dout.astype(BF16)
        dout_ref[...] = doutb
        dm = _dot_nt(doutb, wo_ref[...]).astype(BF16)
        pa, ps = pa_ref[...], ps_ref[...]
        sa = _sigmoid(ga_ref[...].astype(F32))
        sb = _sigmoid(gb_ref[...].astype(F32))
        one_minus_sa, one_minus_sb = (1.0 - sa).astype(BF16), (1.0 - sb).astype(BF16)
        sa, sb = sa.astype(BF16), sb.astype(BF16)
        mg_ref[...] = sa * pa + sb * ps
        dpa = dm * sa
        dps = dm * sb
        dpa_ref[...] = dpa
        dps_ref[...] = dps
        dz_ref[:, 0:D] = dpa * pa * one_minus_sa
        dz_ref[:, D:] = dps * ps * one_minus_sb
        dya_ref[...] = _dot_nt(wa_ref[...], dpa).astype(BF16)
        dys_ref[...] = _dot_nt(dps, ws_ref[...]).astype(BF16)

    row = lambda col: pl.BlockSpec((tm, D), lambda i: (i, col))
    wfull = _resident((D, D))
    vec = pl.BlockSpec((1, D), lambda i: (0, 0))
    act = SDS((T, D), BF16)
    return pl.pallas_call(
        body, name=name, grid=(T // tm,),
        in_specs=[row(0), row(0), vec, row(0), row(0), row(2), row(3), wfull, wfull, wfull],
        out_specs=[pl.BlockSpec((tm, 2 * D), lambda i: (i, 1))] + [row(0)] * 4
        + [pl.BlockSpec((D, tm), lambda i: (0, i)), row(0), vec],
        out_shape=[SDS(zmain.shape, BF16)] + [act] * 4 + [SDS((D, T), BF16), act, SDS((1, D), F32)],
        compiler_params=_params("arbitrary"),
    )(dy, out, g_post, pa, ps, zmain, zmain, w_a, w_s, w_o)


def _loss_head(y, target, *, name):
    T, D = y.shape
    tm = _tile(T, TM_ROW, 8)

    def body(y_ref, t_ref, dy_ref, l_ref):
        @pl.when(pl.program_id(0) == 0)
        def _():
            l_ref[...] = jnp.zeros_like(l_ref)

        e = y_ref[...] - t_ref[...]
        dy_ref[...] = e * (1.0 / D)
        l_ref[...] += jnp.sum(jnp.mean(e * e, axis=-1, keepdims=True))

    row = pl.BlockSpec((tm, D), lambda i: (i, 0))
    return pl.pallas_call(
        body, name=name, grid=(T // tm,), in_specs=[row, row],
        out_specs=[row, pl.BlockSpec((8, LANES), lambda i: (0, 0))],
        out_shape=[SDS((T, D), F32), SDS((8, LANES), F32)],
        compiler_params=_params("arbitrary"),
    )(y, target)


def _adamw(w, g, m, v, *, name):
    shape = w.shape
    cols = shape[-1]
    rows = w.size // cols
    w2, g2, m2, v2 = (t.reshape(rows, cols) for t in (w, g, m, v))
    tr = _tile(rows, max(8, (256 * 1024) // cols // 8 * 8), 8)

    def body(w_ref, g_ref, m_ref, v_ref, d_ref, nm_ref, nv_ref):
        gg = g_ref[...]
        nm = ADAM_B1 * m_ref[...] + (1.0 - ADAM_B1) * gg
        nv = ADAM_B2 * v_ref[...] + (1.0 - ADAM_B2) * (gg * gg)
        m_hat = nm / (1.0 - ADAM_B1 ** ADAM_STEP)
        v_hat = nv / (1.0 - ADAM_B2 ** ADAM_STEP)
        d_ref[...] = -ADAM_LR * (m_hat / (jnp.sqrt(v_hat) + ADAM_EPS) + ADAM_WD * w_ref[...])
        nm_ref[...] = nm
        nv_ref[...] = nv

    blk = pl.BlockSpec((tr, cols), lambda i: (i, 0))
    outs = pl.pallas_call(
        body, name=name, grid=(rows // tr,), in_specs=[blk] * 4, out_specs=[blk] * 3,
        out_shape=[SDS((rows, cols), F32)] * 3, compiler_params=_params("parallel"),
    )(w2, g2, m2, v2)
    return tuple(o.reshape(shape) for o in outs)


def _sum_terms(terms, n_rows, n_lead, dtypes, *, name, out_lead=None, out_n_lead=None):
    cols = terms[0][0].shape[-1]
    tr = _tile(n_rows, 704 if len(terms) <= 4 else 256, SUBLANES_BF16)
    nblk = n_rows // tr
    n_out = len(dtypes)

    def body(*refs):
        acc = refs[0][...].astype(F32)
        for r in refs[1:-n_out]:
            acc = acc + r[...].astype(F32)
        for o_ref in refs[-n_out:]:
            o_ref[...] = acc.astype(o_ref.dtype)

    def spec(lead, first):
        return pl.BlockSpec((1, tr, cols), lambda a, i: (lead(a), first(a) * nblk + i, 0))

    out = pl.BlockSpec((1, tr, cols), lambda a, i: (a if out_lead is None else out_lead(a), i, 0))
    return pl.pallas_call(
        body, name=name, grid=(n_lead, nblk), in_specs=[spec(lead, first) for _, lead, first in terms],
        out_specs=[out] * n_out, out_shape=[SDS((out_n_lead or n_lead, n_rows, cols), d) for d in dtypes],
        compiler_params=_params("arbitrary", "arbitrary"),
    )(*[a for a, _, _ in terms])


def _position():
    x, y, c = (lax.axis_index(a) for a in AXES)
    chips = [(1 - x, y), (x, 1 - y), (1 - x, 1 - y)]
    return x, y, c, chips


ANY = pl.BlockSpec(memory_space=pl.ANY)


def _remote(src, dst, send_sems, recv_sems, k, to):
    return pltpu.make_async_remote_copy(src_ref=src, dst_ref=dst, send_sem=send_sems.at[k], recv_sem=recv_sems.at[k],
                                        device_id=to, device_id_type=MESH)


def _comm_call(body, arrays, out_shapes, n_sems, *, name):
    n = len(arrays)

    def wrapped(*refs):
        body(refs[:n], refs[n:n + len(out_shapes)], refs[-2], refs[-1])

    return pl.pallas_call(
        wrapped, name=name, in_specs=[ANY] * n, out_specs=[ANY] * len(out_shapes), out_shape=out_shapes,
        scratch_shapes=[pltpu.SemaphoreType.DMA((n_sems,)), pltpu.SemaphoreType.DMA((n_sems,))],
    )(*arrays)


def _gather_shards(packs, *, name):
    def body(p_refs, o_refs, send_sems, recv_sems):
        for stage in _gather_stages(p_refs, o_refs, send_sems, recv_sems):
            stage()

    for p in packs:
        assert p.shape[0] % (4 * SUBLANES_BF16) == 0
    return _comm_call(body, packs, _gathered_shapes(packs), GATHER_SEMS * len(packs), name=name)


GATHER_SEMS = 8


def _gathered_shapes(packs):
    return [SDS((N_CHIPS,) + p.shape, p.dtype) for p in packs]


def _gather_stages(p_refs, o_refs, send_sems, recv_sems):
    NS = GATHER_SEMS
    x, y, c, _ = _position()
    me, sib = (x, y, c), (x, y, 1 - c)
    xn, yn = (1 - x, y, c), (x, 1 - y, c)
    s_me, s_xn, s_yn, s_dg = 2 * x + y, 2 * (1 - x) + y, 2 * x + 1 - y, 2 * (1 - x) + 1 - y

    def copy(a, k):
        p_ref, o_ref = p_refs[a], o_refs[a]
        rh = p_ref.shape[0] // 2
        rq = rh // 2
        half, q0, q1 = pl.ds(c * rh, rh), pl.ds(c * rh, rq), pl.ds(c * rh + rq, rq)
        src, dst, to = [(p_ref.at[half], o_ref.at[s_me, half], xn), (p_ref.at[half], o_ref.at[s_me, half], yn),
                        (o_ref.at[s_xn, q0],) * 2 + (yn,), (o_ref.at[s_yn, q1],) * 2 + (xn,),
                        (o_ref.at[s_xn, half],) * 2 + (sib,), (o_ref.at[s_yn, half],) * 2 + (sib,),
                        (o_ref.at[s_dg, q0],) * 2 + (sib,), (o_ref.at[s_dg, q1],) * 2 + (sib,)][k]
        return _remote(src, dst, send_sems, recv_sems, NS * a + k, to)

    def landed(a, k):
        o_ref = o_refs[a]
        rh = o_ref.shape[1] // 2
        rq = rh // 2
        o = (1 - c) * rh
        dst = [o_ref.at[s_xn, pl.ds(c * rh, rh)], o_ref.at[s_yn, pl.ds(c * rh, rh)],
               o_ref.at[s_dg, pl.ds(c * rh, rq)], o_ref.at[s_dg, pl.ds(c * rh + rq, rq)],
               o_ref.at[s_xn, pl.ds(o, rh)], o_ref.at[s_yn, pl.ds(o, rh)],
               o_ref.at[s_dg, pl.ds(o, rq)], o_ref.at[s_dg, pl.ds(o + rq, rq)]][k]
        _remote(dst, dst, send_sems, recv_sems, NS * a + k, me).wait_recv()

    n = len(p_refs)

    def stage_a():
        for a in range(n):
            for k in (0, 1):
                copy(a, k).start()

    def stage_b():
        for a in range(n):
            landed(a, 0)
            copy(a, 2).start()
            copy(a, 4).start()
            landed(a, 1)
            copy(a, 3).start()
            copy(a, 5).start()

    def stage_c():
        for a in range(n):
            landed(a, 2)
            copy(a, 6).start()
            landed(a, 3)
            copy(a, 7).start()
        for a in range(n):
            for k in (4, 5, 6, 7):
                landed(a, k)
        for a in range(n):
            for k in range(NS):
                copy(a, k).wait_send()

    return stage_a, stage_b, stage_c


def _sibling_exchange(gs, *, name):
    def body(g_refs, o_refs, send_sems, recv_sems):
        x, y, c, _ = _position()
        sent = [_remote(g_ref.at[:, pl.ds((1 - c) * o_ref.shape[1], o_ref.shape[1])], o_ref, send_sems, recv_sems, a,
                        (x, y, 1 - c)) for a, (g_ref, o_ref) in enumerate(zip(g_refs, o_refs))]
        for cp in sent:
            cp.start()
        for cp in sent:
            cp.wait()

    return _comm_call(body, gs, [SDS((N_CHIPS, g.shape[1] // 2, g.shape[2]), g.dtype) for g in gs], len(gs), name=name)


def _scatter_hop1(ps, *, name):
    def body(p_refs, o_refs, send_sems, recv_sems):
        x, y, c, _ = _position()
        xn, yn = (1 - x, y, c), (x, 1 - y, c)
        s_xn, s_yn, s_dg = 2 * (1 - x) + y, 2 * x + 1 - y, 2 * (1 - x) + 1 - y
        sent = []
        for a, (p_ref, o_ref) in enumerate(zip(p_refs, o_refs)):
            rq = o_ref.shape[1]
            first, second = pl.ds(0, rq), pl.ds(rq, rq)
            sent += [_remote(p_ref.at[s_xn, second], o_ref.at[0], send_sems, recv_sems, 4 * a, xn),
                     _remote(p_ref.at[s_dg, second], o_ref.at[1], send_sems, recv_sems, 4 * a + 1, xn),
                     _remote(p_ref.at[s_yn, first], o_ref.at[2], send_sems, recv_sems, 4 * a + 2, yn),
                     _remote(p_ref.at[s_dg, first], o_ref.at[3], send_sems, recv_sems, 4 * a + 3, yn)]
        for cp in sent:
            cp.start()
        for cp in sent:
            cp.wait()

    return _comm_call(body, ps, [SDS((4, p.shape[1] // 2, p.shape[2]), p.dtype) for p in ps], 4 * len(ps), name=name)


def _scatter_hop2(fs, *, name):
    def body(f_refs, o_refs, send_sems, recv_sems):
        x, y, c, _ = _position()
        sent = []
        for a, (f_ref, o_ref) in enumerate(zip(f_refs, o_refs)):
            sent += [_remote(f_ref.at[0], o_ref.at[0], send_sems, recv_sems, 2 * a, (1 - x, y, c)),
                     _remote(f_ref.at[1], o_ref.at[1], send_sems, recv_sems, 2 * a + 1, (x, 1 - y, c))]
        for cp in sent:
            cp.start()
        for cp in sent:
            cp.wait()

    return _comm_call(body, fs, [SDS(f.shape, f.dtype) for f in fs], 2 * len(fs), name=name)


def _sibling_fill(rs, *, name):
    n = len(rs)

    def body(*refs):
        r_refs, send_sems, recv_sems = refs[n:2 * n], refs[-2], refs[-1]
        x, y, c, _ = _position()
        sent = []
        for a, r_ref in enumerate(r_refs):
            mine = r_ref.at[pl.ds(c * (r_ref.shape[0] // 2), r_ref.shape[0] // 2)]
            sent.append(_remote(mine, mine, send_sems, recv_sems, a, (x, y, 1 - c)))
            sent[-1].start()
        for a, r_ref in enumerate(r_refs):
            theirs = r_ref.at[pl.ds((1 - c) * (r_ref.shape[0] // 2), r_ref.shape[0] // 2)]
            _remote(theirs, theirs, send_sems, recv_sems, a, (x, y, c)).wait_recv()
        for cp in sent:
            cp.wait_send()

    return pl.pallas_call(
        body, name=name, in_specs=[ANY] * n, out_specs=[ANY] * n, out_shape=[SDS(r.shape, r.dtype) for r in rs],
        input_output_aliases={i: i for i in range(n)},
        scratch_shapes=[pltpu.SemaphoreType.DMA((n,)), pltpu.SemaphoreType.DMA((n,))],
    )(*rs)


def _gather_all(v, *, name):
    M, C = v.shape

    def body(v_ref, o_ref, send_sems, recv_sems):
        x, y, c, chips = _position()
        slot = lambda px, py, pc: o_ref.at[4 * px + 2 * py + pc]
        first = [_remote(v_ref, slot(x, y, c), send_sems, recv_sems, 0, (x, y, 1 - c))]
        first += [_remote(v_ref, slot(x, y, c), send_sems, recv_sems, 1 + j, (*chip, c)) for j, chip in enumerate(chips)]
        for cp in first:
            cp.start()
        passed = []
        for j, chip in enumerate(chips):
            landed = slot(*chip, c)
            _remote(landed, landed, send_sems, recv_sems, 1 + j, (x, y, c)).wait_recv()
            cp = _remote(landed, landed, send_sems, recv_sems, 4 + j, (x, y, 1 - c))
            cp.start()
            passed.append(cp)
        sib = slot(x, y, 1 - c)
        _remote(sib, sib, send_sems, recv_sems, 0, (x, y, c)).wait_recv()
        for j, chip in enumerate(chips):
            theirs = slot(*chip, 1 - c)
            _remote(theirs, theirs, send_sems, recv_sems, 4 + j, (x, y, c)).wait_recv()
        for cp in first + passed:
            cp.wait_send()

    return pl.pallas_call(
        body, name=name, in_specs=[ANY], out_specs=ANY, out_shape=SDS((N_DEV, M, C), v.dtype),
        scratch_shapes=[pltpu.SemaphoreType.DMA((7,)), pltpu.SemaphoreType.DMA((7,))],
    )(v)


BIG = ("ffn1_w1", "ffn2_w1", "w_in", "ffn1_w2", "ffn2_w2", "w_attn_branch", "w_sgu_branch", "w_out")
COL_SHARDED = ("ffn1_w1", "w_in", "ffn2_w1")
FFN_IN = ("ffn1_w1", "ffn2_w1")
SMALL = ("ffn1_pre_g", "ffn1_post_g", "mix_pre_g", "attn_sinks", "sgu_ln_g", "sgu_ln_b", "sgu_w", "sgu_b",
         "mix_post_g", "ffn2_pre_g", "ffn2_post_g")
WEIGHTS = ("ffn1_pre_g", "ffn1_w1", "ffn1_w2", "ffn1_post_g", "mix_pre_g", "w_in", "attn_sinks", "sgu_ln_g",
           "sgu_ln_b", "sgu_w", "sgu_b", "w_attn_branch", "w_sgu_branch", "w_out", "mix_post_g", "ffn2_pre_g",
           "ffn2_w1", "ffn2_w2", "ffn2_post_g")


def _column_chunks(w, tn):
    return jnp.swapaxes(w.reshape(w.shape[0], w.shape[1] // tn, tn), 0, 1)


def _width_classes(shard_shapes):
    widths = sorted({shard_shapes[n][-1] for n in BIG}, reverse=True)
    return [[n for n in BIG if shard_shapes[n][-1] == w] for w in widths]


def _class_rows(classes, shard_shapes, n_layers, aligned=BIG):
    where = {}
    for k, names in enumerate(classes):
        off = 0
        for n in names:
            r = shard_shapes[n][0]
            assert off % r == 0 or n not in aligned
            where[n] = (k, off, r)
            off += n_layers * r
    return where


def _ffn_fwd(x, pre_g, w1, w1_block, w2, post_g, tag, gather=()):
    a, h, *gathered = _norm_matmul(x, pre_g, w1, w1_block, name=f"{tag}_up", with_h=True, gather=gather)
    xn, o = _swiglu_out(a, w2, x, post_g, name=f"{tag}_down")
    return xn, (x, h, a, o), gathered


def _ffn_bwd(dy, saved, pre_g, w1, w1_block, w2, post_g, dw1_into, dw2_into, tag):
    x, h, a, o = saved
    da, do, d_post = _ffn_bwd_hidden(dy, o, post_g, a, w2, name=f"{tag}_bwd_hidden")
    g2 = _dw_rows(a, do, *dw2_into, name=f"{tag}_dw2", swiglu=True)
    dx, d_pre = _matmul_nt_norm_bwd(da, w1, w1_block, x, pre_g, dy, None, name=f"{tag}_bwd_in")
    g1 = _dw_cols(h, da, N_CHIPS, *dw1_into, name=f"{tag}_dw1")
    return dx, g1, g2, d_pre, d_post


def kernel(x, ffn1_pre_g, ffn1_w1, ffn1_w2, ffn1_post_g, mix_pre_g, w_in, attn_sinks, sgu_ln_g, sgu_ln_b, sgu_w, sgu_b, w_attn_branch, w_sgu_branch, w_out, mix_post_g, ffn2_pre_g, ffn2_w1, ffn2_w2, ffn2_post_g, loss_target, m_ffn1_pre_g, m_ffn1_w1, m_ffn1_w2, m_ffn1_post_g, m_mix_pre_g, m_w_in, m_attn_sinks, m_sgu_ln_g, m_sgu_ln_b, m_sgu_w, m_sgu_b, m_w_attn_branch, m_w_sgu_branch, m_w_out, m_mix_post_g, m_ffn2_pre_g, m_ffn2_w1, m_ffn2_w2, m_ffn2_post_g, v_ffn1_pre_g, v_ffn1_w1, v_ffn1_w2, v_ffn1_post_g, v_mix_pre_g, v_w_in, v_attn_sinks, v_sgu_ln_g, v_sgu_ln_b, v_sgu_w, v_sgu_b, v_w_attn_branch, v_w_sgu_branch, v_w_out, v_mix_post_g, v_ffn2_pre_g, v_ffn2_w1, v_ffn2_w2, v_ffn2_post_g):
    given = dict(locals())
    W = {n: given[n] for n in WEIGHTS}
    M = {n: given["m_" + n] for n in WEIGHTS}
    V = {n: given["v_" + n] for n in WEIGHTS}
    L = ffn1_w1.shape[0]
    T, D = x.shape[1], x.shape[2]
    xt = x.reshape(T, D)
    target = loss_target.reshape(T, D)
    assert L % 2 == 0 and D == ATTN_WIDTH == SGU_WIDTH and T % ATTN_BLOCK == 0

    shard_shapes = {n: W[n].shape[1:] for n in BIG}
    classes = _width_classes(shard_shapes)
    my_chip = 2 * lax.axis_index("x") + lax.axis_index("y")
    my_core = lax.axis_index("c")
    where = _class_rows(classes, shard_shapes, L)
    where_w = _class_rows(classes, shard_shapes, 1, aligned=FFN_IN)
    packs = [[jnp.concatenate([W[n][l].astype(BF16) for n in names], axis=0) for names in classes] for l in range(L)]

    def block_of(n, l):
        k, off, r = where[n]
        return k, off // r + l

    def layer_weights(l, got):
        wc = [lax.dynamic_update_slice(g, p[None], (my_chip, 0, 0)) for g, p in zip(got, packs[l])]

        def chip_shards(n):
            k, off, r = where_w[n]
            return wc[k][:, off:off + r, :]

        fw = {n: chip_shards(n).reshape(-1, D) for n in BIG if n not in COL_SHARDED}
        w_in_l = jnp.swapaxes(chip_shards("w_in"), 0, 1).reshape(D, -1)
        fw["w_qkv_t"] = w_in_l[:, :QKV_WIDTH].T
        fw["w_main"] = _column_chunks(w_in_l[:, QKV_WIDTH:], D)
        for n in FFN_IN:
            fw[n] = (wc[where_w[n][0]], where_w[n][1] // where_w[n][2])
        return fw

    full = [None] * L
    full[0] = layer_weights(0, _gather_shards(packs[0], name="gather_weights_l0"))

    row = lambda name, l: W[name][l].reshape(1, -1)
    causal = jnp.tril(jnp.ones((SGU_CHUNK, SGU_CHUNK), dtype=bool))
    saved = []
    h_cur = xt
    for l in range(L):
        fw = full[l]
        sv = {}
        h_cur, sv["ffn1"], got = _ffn_fwd(h_cur, row("ffn1_pre_g", l), *fw["ffn1_w1"], fw["ffn1_w2"],
                                          row("ffn1_post_g", l), f"l{l}_ffn1", gather=packs[l + 1] if l + 1 < L else ())
        if l + 1 < L:
            full[l + 1] = layer_weights(l + 1, got)
        zqkv, hm = _norm_matmul_t(h_cur, row("mix_pre_g", l), fw["w_qkv_t"], name=f"l{l}_mix_in_qkv")
        zmain, = _norm_matmul(h_cur, row("mix_pre_g", l), fw["w_main"], 0, name=f"l{l}_mix_in_main", with_h=False)
        wm = jnp.where(causal[None], sgu_w[l], 0.0).astype(BF16)
        wmt = jnp.swapaxes(wm, 1, 2)
        bias = jnp.broadcast_to(sgu_b[l][:, :, None], (SGU_GROUPS, SGU_CHUNK, 128)).astype(F32)
        y_attn = _attn_fwd(zqkv, attn_sinks[l], name=f"l{l}_attn")
        y_sgu = _sgu_fwd(zmain, row("sgu_ln_g", l), row("sgu_ln_b", l), wm, bias, name=f"l{l}_sgu")
        x_mix = h_cur
        h_cur, pa, ps, mo = _merge_fwd(y_attn, y_sgu, zmain, fw["w_attn_branch"], fw["w_sgu_branch"], fw["w_out"],
                                       x_mix, row("mix_post_g", l), name=f"l{l}_merge")
        sv["mix"] = (x_mix, hm, zqkv, zmain, y_attn, y_sgu, pa, ps, mo, wm, wmt, bias)
        h_cur, sv["ffn2"], _ = _ffn_fwd(h_cur, row("ffn2_pre_g", l), *fw["ffn2_w1"], fw["ffn2_w2"],
                                        row("ffn2_post_g", l), f"l{l}_ffn2")
        saved.append(sv)

    dy, lsum = _loss_head(h_cur, target, name="loss_head")
    loss = lax.psum(0.5 * lsum[0, 0], AXES)

    k_in = where["w_in"][0]
    assert classes[k_in] == ["w_in"]
    gcls = [None if k == k_in else lax.empty((N_CHIPS, L * p.shape[0], p.shape[1]), F32)
            for k, p in enumerate(packs[0])]
    dw_in = [None] * L
    small_grads = [None] * L

    def into(n, l):
        return gcls[block_of(n, l)[0]], block_of(n, l)[1]

    def ffn_bwd(dy, which, l):
        n1, n2 = f"{which}_w1", f"{which}_w2"
        dy, g1, g2, d_pre, d_post = _ffn_bwd(
            dy, saved[l][which], row(f"{which}_pre_g", l), *full[l][n1], full[l][n2], row(f"{which}_post_g", l),
            into(n1, l), into(n2, l), f"l{l}_{which}")
        gcls[where[n1][0]], gcls[where[n2][0]] = g1, g2
        return dy, d_pre, d_post

    for l in reversed(range(L)):
        fw, sv = full[l], saved[l]
        gs = {}
        dy, gs["ffn2_pre_g"], gs["ffn2_post_g"] = ffn_bwd(dy, "ffn2", l)

        x_mix, hm, zqkv, zmain, y_attn, y_sgu, pa, ps, mo, wm, wmt, bias = sv["mix"]
        dzmain, dout, merged, dpa, dps, dya, dys, gs["mix_post_g"] = _merge_bwd(
            dy, mo, row("mix_post_g", l), pa, ps, zmain, fw["w_attn_branch"], fw["w_sgu_branch"], fw["w_out"],
            name=f"l{l}_merge_bwd")
        k_sq = where["w_out"][0]
        gcls[k_sq] = _dw_rows(merged, dout, *into("w_out", l), name=f"l{l}_dw_out")
        gcls[k_sq] = _dw_rows(y_attn, dpa, *into("w_attn_branch", l), name=f"l{l}_dw_attn", a_feature_major=True)
        gcls[k_sq] = _dw_rows(y_sgu, dps, *into("w_sgu_branch", l), name=f"l{l}_dw_sgu")
        dzqkv, dsink = _attn_bwd(zqkv, attn_sinks[l], dya, name=f"l{l}_attn_bwd")
        gs["attn_sinks"] = dsink[0, :N_Q_HEADS]
        dzmain, dsw, dsb, gs["sgu_ln_g"], gs["sgu_ln_b"] = _sgu_bwd(
            zmain, dzmain, dys, row("sgu_ln_g", l), row("sgu_ln_b", l), wm, wmt, bias, name=f"l{l}_sgu_bwd")
        gs["sgu_w"] = dsw
        gs["sgu_b"] = dsb[:, :SGU_GROUPS].T
        dh_qkv = _matmul_tn_rows(dzqkv, fw["w_qkv_t"], name=f"l{l}_mix_bwd_qkv")
        dy, gs["mix_pre_g"] = _matmul_nt_norm_bwd(dzmain, fw["w_main"], 0, x_mix, row("mix_pre_g", l), dy, dh_qkv,
                                                   name=f"l{l}_mix_bwd_in")
        dw_main = _dw_cols(hm, dzmain, zmain.shape[1] // D, None, 0, name=f"l{l}_dw_in_main")
        dw_in[l] = jnp.concatenate([_matmul_tokens(dzqkv, hm, name=f"l{l}_dw_in_qkv").T,
                                    jnp.swapaxes(dw_main, 0, 1).reshape(D, -1)], axis=1)

        dy, gs["ffn1_pre_g"], gs["ffn1_post_g"] = ffn_bwd(dy, "ffn1", l)
        small_grads[l] = gs
    grad_x = dy.reshape(x.shape)

    w_in_width = shard_shapes["w_in"][1]
    gcls[k_in] = jnp.stack([jnp.concatenate([g[:, s * w_in_width:(s + 1) * w_in_width] for g in dw_in], axis=0)
                            for s in range(N_CHIPS)])
    grs = gcls
    from_sibling = _sibling_exchange(grs, name="grads_sibling_exchange")
    zero = lambda a: 0
    own = lambda a: a
    core = lambda a: lax.axis_index("c")
    chip = lambda a: 2 * lax.axis_index("x") + lax.axis_index("y")
    chip_xn = lambda a: 2 * (1 - lax.axis_index("x")) + lax.axis_index("y")
    chip_yn = lambda a: 2 * lax.axis_index("x") + 1 - lax.axis_index("y")
    pairs = [_sum_terms([(g, own, core), (fs, own, zero)], fs.shape[1], N_CHIPS, (BF16,),
                        name=f"grads_pair_sum{k}")[0] for k, (g, fs) in enumerate(zip(grs, from_sibling))]
    hop1 = _scatter_hop1(pairs, name="grads_scatter_hop1")
    relay = [_sum_terms([(p, lambda a: chip_xn(a) + a * (chip_yn(a) - chip_xn(a)), own), (h, lambda a: 3 - 2 * a, zero)],
                        h.shape[1], 2, (BF16,), name=f"grads_relay_sum{k}")[0]
             for k, (p, h) in enumerate(zip(pairs, hop1))]
    hop2 = _scatter_hop2(relay, name="grads_scatter_hop2")
    quarter = lambda a: 2 * core(a) + a
    halves = [_sum_terms([(g, chip, quarter), (fs, chip, own), (h1, lambda a: 2 - 2 * a, zero), (h2, own, zero)],
                         h1.shape[1], 2, (F32,), name=f"grads_chip_sum{k}", out_lead=quarter, out_n_lead=4)[0]
              .reshape(g.shape[1:]) for k, (g, fs, h1, h2) in enumerate(zip(grs, from_sibling, hop1, hop2))]
    reduced_all = _sibling_fill(halves, name="grads_sibling_fill")

    grads = {n: [None] * L for n in SMALL}
    for names, reduced in zip(classes, reduced_all):
        for n in names:
            _, off, r = where[n]
            grads[n] = reduced[off:off + L * r].reshape((L,) + shard_shapes[n])

    def small_rows(gs):
        parts = []
        for n in SMALL:
            flat = gs[n].reshape(-1)
            pad = (-flat.shape[0]) % D
            parts.append(jnp.pad(flat, (0, pad)).reshape(-1, D))
        return jnp.concatenate(parts, axis=0)

    spack = jnp.concatenate([small_rows(small_grads[l]) for l in range(L)], axis=0)
    n_small = spack.shape[0]
    pad_rows = (-n_small) % SUBLANES_BF16
    spack = jnp.pad(spack, ((0, pad_rows), (0, 0)))
    everyone = _gather_all(spack, name="small_grads_gather")
    is_me = (jnp.arange(N_DEV) == 2 * my_chip + my_core)[:, None, None]
    everyone = jnp.where(is_me, spack[None], everyone)
    ssum = _sum_terms([(everyone, (lambda a, d=d: d), zero) for d in range(N_DEV)], spack.shape[0], 1, (F32,),
                      name="small_grads_sum")[0][0]
    per_layer = n_small // L
    for l in range(L):
        r0 = l * per_layer
        for n in SMALL:
            shp = W[n].shape[1:]
            size = math.prod(shp)
            nr = -(-size // D)
            grads[n][l] = ssum[r0:r0 + nr].reshape(-1)[:size].reshape(shp)
            r0 += nr
    grads.update({n: jnp.stack(grads[n]) for n in SMALL})

    delta, new_m, new_v = {}, {}, {}
    for n in WEIGHTS:
        delta[n], new_m[n], new_v[n] = _adamw(W[n], grads[n], M[n], V[n], name=f"adamw_{n}")

    return (loss, grad_x, *[grads[n] for n in WEIGHTS], *[delta[n] for n in WEIGHTS],
            *[new_m[n] for n in WEIGHTS], *[new_v[n] for n in WEIGHTS])
```

```python
import functools
import math

import jax
import jax.numpy as jnp
from jax import lax
from jax.experimental import pallas as pl
from jax.experimental.pallas import tpu as pltpu

F32, BF16 = jnp.float32, jnp.bfloat16
SDS = jax.ShapeDtypeStruct
MESH = pl.DeviceIdType.MESH
AXES = ("x", "y", "c")

HEAD_DIM = 64
N_Q_HEADS = 16
N_KV_HEADS = 2
Q_PER_KV = N_Q_HEADS // N_KV_HEADS
ATTN_WIDTH = N_Q_HEADS * HEAD_DIM
KV_WIDTH = N_KV_HEADS * HEAD_DIM
ATTN_BLOCK = 128
SGU_CHUNK = 128
SGU_GROUPS = 8
SGU_WIDTH = SGU_GROUPS * 128
QKV_WIDTH = ATTN_WIDTH + 2 * KV_WIDTH
RMS_EPS = 1e-6
LN_EPS = 1e-5
MASK_VALUE = -1e30
ATTN_SCALE = 1.0 / math.sqrt(HEAD_DIM)
assert math.frexp(ATTN_SCALE)[0] == 0.5

ADAM_LR, ADAM_B1, ADAM_B2, ADAM_EPS, ADAM_WD, ADAM_STEP = 0.001, 0.9, 0.999, 1e-08, 0.01, 10

N_CHIPS = 4
N_DEV = 8

VMEM_LIMIT_BYTES = 56 * 1024 * 1024
LANES = 128
SUBLANES_BF16 = 16

TM_NORM_MATMUL = 1024
TM_ROW = 512
TM_FFN_BWD = 512
TT_REDUCE = 1024
TQ_ATTN = 1024
TM_FEATURE_MAJOR = 1024
TS_SGU = 512


def _tile(n, pref, mult):
    t = (min(pref, n) // mult) * mult
    while t >= mult:
        if n % t == 0:
            return t
        t -= mult
    return n


def _params(*sem):
    return pltpu.CompilerParams(dimension_semantics=sem, vmem_limit_bytes=VMEM_LIMIT_BYTES)


def _dot(a, b):
    return jnp.dot(a, b, preferred_element_type=F32)


def _dot_nt(a, b):
    return lax.dot_general(a, b, (((1,), (1,)), ((), ())), preferred_element_type=F32)


def _dot_tn(a, b):
    return lax.dot_general(a, b, (((0,), (0,)), ((), ())), preferred_element_type=F32)


def _sigmoid(x):
    return 0.5 * (1.0 + jnp.tanh(0.5 * x))


def _rms_stats(xf):
    r = lax.rsqrt(jnp.mean(xf * xf, axis=-1, keepdims=True) + RMS_EPS)
    return r, xf * r


def _rms_bwd(xf, g, dy):
    r, xh = _rms_stats(xf)
    dyg = dy * g
    dx = r * (dyg - xh * jnp.mean(dyg * xh, axis=-1, keepdims=True))
    return dx, jnp.sum(dy * xh, axis=0, keepdims=True)


def _gelu_parts(x):
    cdf = 0.5 * (1.0 + lax.erf(x * (1.0 / math.sqrt(2.0))))
    return cdf


def _gelu(x):
    return x * _gelu_parts(x)


def _gelu_grad(x):
    return _gelu_parts(x) + x * jnp.exp(-0.5 * x * x) * (1.0 / math.sqrt(2.0 * math.pi))


def _resident(shape, index=None):
    index = (0,) * len(shape) if index is None else index
    return pl.BlockSpec(shape, lambda *_: index, pipeline_mode=pl.Buffered(1))


def _norm_matmul(x, g, w3, w_block, *, name, with_h, gather=()):
    T, D = x.shape
    nj, _, tn = w3.shape
    tm = _tile(T, TM_NORM_MATMUL, SUBLANES_BF16)
    ni, ng = T // tm, len(gather)
    n_out = 1 + with_h

    def body(x_ref, g_ref, w_ref, *rest):
        a_ref, h_sc = rest[ng], rest[ng + n_out + ng]
        i, j = pl.program_id(0), pl.program_id(1)
        if ng:
            stages = _gather_stages(rest[:ng], rest[ng + n_out:ng + n_out + ng], *rest[-2:])
            for stage, (si, sj) in zip(stages[:2], ((0, 0), (ni // 2, 0))):
                pl.when((i == si) & (j == sj))(stage)

        @pl.when(j == 0)
        def _():
            _, xh = _rms_stats(x_ref[...])
            h = (xh * g_ref[...]).astype(BF16)
            h_sc[...] = h
            if with_h:
                rest[ng + 1][...] = h

        a_ref[...] = _dot(h_sc[...], w_ref[j]).astype(BF16)
        if ng:
            pl.when((i == ni - 1) & (j == nj - 1))(stages[2])

    out_specs = [pl.BlockSpec((tm, tn), lambda i, j: (i, j))]
    out_shape = [SDS((T, nj * tn), BF16)]
    if with_h:
        out_specs.append(pl.BlockSpec((tm, D), lambda i, j: (i, 0)))
        out_shape.append(SDS((T, D), BF16))
    scratch = [pltpu.VMEM((tm, D), BF16)]
    if ng:
        scratch += [pltpu.SemaphoreType.DMA((GATHER_SEMS * ng,))] * 2
    return pl.pallas_call(
        body, name=name, grid=(ni, nj),
        in_specs=[pl.BlockSpec((tm, D), lambda i, j: (i, 0)),
                  pl.BlockSpec((1, D), lambda i, j: (0, 0)),
                  _resident((nj, D, tn), (0, w_block, 0))] + [ANY] * ng,
        out_specs=out_specs + [ANY] * ng, out_shape=out_shape + _gathered_shapes(gather),
        scratch_shapes=scratch,
        compiler_params=_params(*(("arbitrary", "arbitrary") if ng else ("parallel", "arbitrary"))),
    )(x, g, w3, *gather)


def _norm_matmul_t(x, g, wt, *, name):
    T, D = x.shape
    N = wt.shape[0]
    tm = _tile(T, TM_FEATURE_MAJOR, LANES)

    def body(x_ref, g_ref, w_ref, a_ref, h_ref):
        _, xh = _rms_stats(x_ref[...])
        h = (xh * g_ref[...]).astype(BF16)
        h_ref[...] = h
        a_ref[...] = _dot_nt(w_ref[...], h).astype(BF16)

    return pl.pallas_call(
        body, name=name, grid=(T // tm,),
        in_specs=[pl.BlockSpec((tm, D), lambda i: (i, 0)), pl.BlockSpec((1, D), lambda i: (0, 0)),
                  _resident((N, D))],
        out_specs=[pl.BlockSpec((N, tm), lambda i: (0, i)), pl.BlockSpec((tm, D), lambda i: (i, 0))],
        out_shape=[SDS((N, T), BF16), SDS((T, D), BF16)],
        compiler_params=_params("parallel"),
    )(x, g, wt)


def _matmul_tokens(at, b, *, name):
    K, T = at.shape
    N = b.shape[1]
    tt = _tile(T, TT_REDUCE, LANES)

    def body(a_ref, b_ref, o_ref):
        @pl.when(pl.program_id(0) == 0)
        def _():
            o_ref[...] = jnp.zeros_like(o_ref)

        o_ref[...] += _dot(a_ref[...], b_ref[...])

    return pl.pallas_call(
        body, name=name, grid=(T // tt,),
        in_specs=[pl.BlockSpec((K, tt), lambda t: (0, t)), pl.BlockSpec((tt, N), lambda t: (t, 0))],
        out_specs=pl.BlockSpec((K, N), lambda t: (0, 0)),
        out_shape=SDS((K, N), F32),
        compiler_params=_params("arbitrary"),
    )(at, b)


def _matmul_tn_rows(dat, wt, *, name):
    N, T = dat.shape
    D = wt.shape[1]
    tm = _tile(T, TM_FEATURE_MAJOR, LANES)

    def body(da_ref, w_ref, o_ref):
        o_ref[...] = _dot_tn(da_ref[...], w_ref[...])

    return pl.pallas_call(
        body, name=name, grid=(T // tm,),
        in_specs=[pl.BlockSpec((N, tm), lambda i: (0, i)), _resident((N, D))],
        out_specs=pl.BlockSpec((tm, D), lambda i: (i, 0)),
        out_shape=SDS((T, D), F32),
        compiler_params=_params("parallel"),
    )(dat, wt)


def _ff_chunk(F):
    return F if F <= 1408 else F // 2


def _swiglu_out(a, w2, x, g_post, *, name):
    T, F2 = a.shape
    F = F2 // 2
    D = x.shape[1]
    tm = _tile(T, TM_ROW, SUBLANES_BF16)
    fc = _ff_chunk(F)

    def body(a_ref, w_ref, x_ref, g_ref, xn_ref, o_ref):
        acc = None
        for c0 in range(0, F, fc):
            gt = a_ref[:, c0:c0 + fc].astype(F32)
            s = (gt * _sigmoid(gt)).astype(BF16) * a_ref[:, F + c0:F + c0 + fc]
            part = _dot(s, w_ref[c0:c0 + fc, :])
            acc = part if acc is None else acc + part
        o_ref[...] = acc.astype(BF16)
        _, oh = _rms_stats(acc)
        xn_ref[...] = x_ref[...] + 0.5 * (oh * g_ref[...])

    return pl.pallas_call(
        body, name=name, grid=(T // tm,),
        in_specs=[pl.BlockSpec((tm, F2), lambda i: (i, 0)),
                  _resident((F, D)),
                  pl.BlockSpec((tm, D), lambda i: (i, 0)),
                  pl.BlockSpec((1, D), lambda i: (0, 0))],
        out_specs=[pl.BlockSpec((tm, D), lambda i: (i, 0)), pl.BlockSpec((tm, D), lambda i: (i, 0))],
        out_shape=[SDS((T, D), F32), SDS((T, D), BF16)],
        compiler_params=_params("parallel"),
    )(a, w2, x, g_post)


def _ffn_bwd_hidden(dy, o, g_post, a, w2, *, name):
    T, F2 = a.shape
    F = F2 // 2
    D = dy.shape[1]
    tm = _tile(T, TM_FFN_BWD, SUBLANES_BF16)
    fc = _tile(F, 256, LANES)

    def body(dy_ref, o_ref, g_ref, a_ref, w_ref, da_ref, do_ref, dg_ref):
        @pl.when(pl.program_id(0) == 0)
        def _():
            dg_ref[...] = jnp.zeros_like(dg_ref)

        do, dg = _rms_bwd(o_ref[...].astype(F32), g_ref[...], 0.5 * dy_ref[...])
        dg_ref[...] += dg
        dob = do.astype(BF16)
        do_ref[...] = dob
        for c0 in range(0, F, fc):
            ds = _dot_nt(dob, w_ref[c0:c0 + fc, :]).astype(BF16)
            gt = a_ref[:, c0:c0 + fc].astype(F32)
            ub = a_ref[:, F + c0:F + c0 + fc]
            sg = _sigmoid(gt)
            sl = gt * sg
            dsl = (sg + sl * (1.0 - sg)).astype(BF16)
            da_ref[:, c0:c0 + fc] = ds * ub * dsl
            da_ref[:, F + c0:F + c0 + fc] = ds * sl.astype(BF16)

    row = lambda w: pl.BlockSpec((tm, w), lambda i: (i, 0))
    return pl.pallas_call(
        body, name=name, grid=(T // tm,),
        in_specs=[row(D), row(D), pl.BlockSpec((1, D), lambda i: (0, 0)), row(F2),
                  _resident((F, D))],
        out_specs=[row(F2), row(D), pl.BlockSpec((1, D), lambda i: (0, 0))],
        out_shape=[SDS((T, F2), BF16), SDS((T, D), BF16), SDS((1, D), F32)],
        compiler_params=_params("arbitrary"),
    )(dy, o, g_post, a, w2)


def _dw_call(body, name, grid, in_specs, args, block, pack, row_block, sem):
    if pack is None:
        out_spec = pl.BlockSpec(block, lambda *_: (0, 0, 0), pipeline_mode=pl.Buffered(1))
        return pl.pallas_call(body, name=name, grid=grid, in_specs=in_specs, out_specs=out_spec,
                              out_shape=SDS(block, F32), compiler_params=_params(*sem))(*args)
    assert pack.shape[0] == block[0] and pack.shape[2] == block[2]
    out_spec = pl.BlockSpec(block, lambda *_: (0, row_block, 0), pipeline_mode=pl.Buffered(1))
    return pl.pallas_call(body, name=name, grid=grid, in_specs=in_specs + [ANY], out_specs=out_spec,
                          out_shape=SDS(pack.shape, F32), input_output_aliases={len(args): 0},
                          compiler_params=_params(*sem))(*args, pack)


def _dw_cols(a, b, n_chunks, pack, row_block, *, name, ride=()):
    T, K = a.shape
    tn = b.shape[1] // n_chunks
    tt = _tile(T, TT_REDUCE, SUBLANES_BF16)
    per = 2 if n_chunks % 2 == 0 else 1
    grid = (n_chunks // per, T // tt)
    nr = len(ride)

    def body(a_ref, b_ref, *rest):
        o_ref = rest[1 + nr] if nr else rest[-1]
        j, t = pl.program_id(0), pl.program_id(1)
        if nr:
            x, y, c = (lax.axis_index(ax) for ax in AXES)
            rides = [_remote(g_ref.at[:, pl.ds((1 - c) * r_ref.shape[1], r_ref.shape[1])], r_ref, rest[-2], rest[-1], i,
                             (x, y, 1 - c)) for i, (g_ref, r_ref) in enumerate(zip(rest[1:1 + nr], rest[2 + nr:2 + 2 * nr]))]

            @pl.when((j == 0) & (t == 0))
            def _():
                for cp in rides:
                    cp.start()

        @pl.when(t == 0)
        def _():
            o_ref[...] = jnp.zeros_like(o_ref)

        for p in range(per):
            o_ref[p] += _dot_tn(a_ref[...], b_ref[:, p * tn:(p + 1) * tn])

        if nr:
            @pl.when((j == grid[0] - 1) & (t == grid[1] - 1))
            def _():
                for cp in rides:
                    cp.wait()

    in_specs = [pl.BlockSpec((tt, K), lambda j, t: (t, 0)), pl.BlockSpec((tt, per * tn), lambda j, t: (t, j))]
    sem = ("arbitrary", "arbitrary") if nr else ("parallel", "arbitrary")
    if pack is None:
        assert not nr
        return pl.pallas_call(body, name=name, grid=grid, in_specs=in_specs,
                              out_specs=pl.BlockSpec((per, K, tn), lambda j, t: (j, 0, 0)),
                              out_shape=SDS((n_chunks, K, tn), F32), compiler_params=_params(*sem))(a, b)
    assert pack.shape[0] == n_chunks and pack.shape[2] == tn
    out = pl.pallas_call(
        body, name=name, grid=grid, in_specs=in_specs + [ANY] * (1 + nr),
        out_specs=[pl.BlockSpec((per, K, tn), lambda j, t: (j, row_block, 0))] + [ANY] * nr,
        out_shape=[SDS(pack.shape, F32)] + [SDS((N_CHIPS, g.shape[1] // 2, g.shape[2]), g.dtype) for g in ride],
        input_output_aliases={2: 0},
        scratch_shapes=[pltpu.SemaphoreType.DMA((nr,)), pltpu.SemaphoreType.DMA((nr,))] if nr else [],
        compiler_params=_params(*sem))(a, b, pack, *ride)
    return out if nr else out[0]


def _dw_rows(a, b, pack, row_block, *, name, a_feature_major=False, swiglu=False):
    K, T = a.shape if a_feature_major else a.shape[::-1]
    K = K // 2 if swiglu else K
    N = b.shape[1]
    r = K // N_CHIPS
    cw = r if r % LANES == 0 else 2 * r
    assert cw % LANES == 0 and K % cw == 0 and r % 8 == 0
    tt = _tile(T, TT_REDUCE // 2 if swiglu else TT_REDUCE, LANES)

    def body(a_ref, b_ref, *rest):
        o_ref = rest[-1]

        @pl.when(pl.program_id(0) == 0)
        def _():
            o_ref[...] = jnp.zeros_like(o_ref)

        for c in range(K // cw):
            if a_feature_major:
                part = _dot(a_ref[c * cw:(c + 1) * cw, :], b_ref[...])
            elif swiglu:
                gt = a_ref[:, c * cw:(c + 1) * cw].astype(F32)
                part = _dot_tn((gt * _sigmoid(gt)).astype(BF16) * a_ref[:, K + c * cw:K + (c + 1) * cw], b_ref[...])
            else:
                part = _dot_tn(a_ref[:, c * cw:(c + 1) * cw], b_ref[...])
            for p in range(cw // r):
                o_ref[c * (cw // r) + p] += part[p * r:(p + 1) * r]

    a_spec = (pl.BlockSpec((K, tt), lambda t: (0, t)) if a_feature_major
              else pl.BlockSpec((tt, a.shape[1]), lambda t: (t, 0)))
    return _dw_call(body, name, (T // tt,), [a_spec, pl.BlockSpec((tt, N), lambda t: (t, 0))],
                    [a, b], (N_CHIPS, r, N), pack, row_block, ("arbitrary",))


def _matmul_nt_norm_bwd(da, w, w_block, x, g, dy, init, *, name):
    T, N = da.shape
    D = x.shape[1]
    nj, _, tn = w.shape
    tm = _tile(T, TM_ROW, SUBLANES_BF16)
    has_init = init is not None

    def body(da_ref, w_ref, x_ref, g_ref, dy_ref, *rest):
        dx_ref, dg_ref = rest[-2:]

        @pl.when(pl.program_id(0) == 0)
        def _():
            dg_ref[...] = jnp.zeros_like(dg_ref)

        dh = rest[0][...] if has_init else None
        for j in range(nj):
            part = _dot_nt(da_ref[:, j * tn:(j + 1) * tn], w_ref[j])
            dh = part if dh is None else dh + part
        dx, dg = _rms_bwd(x_ref[...], g_ref[...], dh)
        dx_ref[...] = dy_ref[...] + dx
        dg_ref[...] += dg

    row = pl.BlockSpec((tm, D), lambda i: (i, 0))
    vec = pl.BlockSpec((1, D), lambda i: (0, 0))
    in_specs = [pl.BlockSpec((tm, N), lambda i: (i, 0)), _resident((nj, D, tn), (0, w_block, 0)), row, vec, row]
    args = [da, w, x, g, dy]
    if has_init:
        in_specs.append(row)
        args.append(init)
    return pl.pallas_call(
        body, name=name, grid=(T // tm,), in_specs=in_specs,
        out_specs=[row, vec], out_shape=[SDS((T, D), F32), SDS((1, D), F32)],
        compiler_params=_params("arbitrary"),
    )(*args)


GROUP_LANES = Q_PER_KV * ATTN_BLOCK


def _attn_mask_t(first):
    kj = lax.broadcasted_iota(jnp.int32, (2 * ATTN_BLOCK, ATTN_BLOCK), 0)
    qi = lax.broadcasted_iota(jnp.int32, (2 * ATTN_BLOCK, ATTN_BLOCK), 1)
    rel = qi + ATTN_BLOCK - kj
    band = (rel >= 0) & (rel < ATTN_BLOCK)
    if first is False:
        return band
    return band & ((kj >= ATTN_BLOCK) | jnp.logical_not(first))


def _attn_probs_t(st, valid, sink):
    s = jnp.where(valid, st, MASK_VALUE)
    m = jnp.maximum(jnp.max(s, axis=0, keepdims=True), sink)
    p = jnp.exp(s - m)
    es = jnp.exp(sink - m)
    inv = 1.0 / (jnp.sum(p, axis=0, keepdims=True) + es)
    return p * inv, es * inv


def _attn_specs(tq, tile_of):
    nb = tq // ATTN_BLOCK
    krow, vrow = ATTN_WIDTH // KV_WIDTH, ATTN_WIDTH // KV_WIDTH + 1
    halo = lambda r: pl.BlockSpec((KV_WIDTH, ATTN_BLOCK), lambda t: (r, jnp.maximum(tile_of(t) * nb - 1, 0)))
    return [pl.BlockSpec((ATTN_WIDTH, tq), lambda t: (0, tile_of(t))),
            pl.BlockSpec((KV_WIDTH, tq), lambda t: (krow, tile_of(t))),
            pl.BlockSpec((KV_WIDTH, tq), lambda t: (vrow, tile_of(t))),
            halo(krow), halo(vrow)]


def _head_rows(g, r):
    h = g * Q_PER_KV + r
    return h, slice(h * HEAD_DIM, (h + 1) * HEAD_DIM)


def _group_stack(ref, g, cols):
    return jnp.concatenate([ref[_head_rows(g, r)[1], cols] for r in range(Q_PER_KV)], axis=1)


def _attn_fwd(zt, sinks, *, name):
    T = zt.shape[1]
    tq = _tile(T, TQ_ATTN, ATTN_BLOCK)
    nb = tq // ATTN_BLOCK

    def body(q_ref, k_ref, v_ref, kh_ref, vh_ref, s_ref, o_ref, kf, vf, pt):
        kf[:, 0:ATTN_BLOCK] = kh_ref[...]
        kf[:, ATTN_BLOCK:] = k_ref[...]
        vf[:, 0:ATTN_BLOCK] = vh_ref[...]
        vf[:, ATTN_BLOCK:] = v_ref[...]
        for b in range(nb):
            cols = slice(b * ATTN_BLOCK, (b + 1) * ATTN_BLOCK)
            win = slice(b * ATTN_BLOCK, (b + 2) * ATTN_BLOCK)
            valid = _attn_mask_t((pl.program_id(0) == 0) if b == 0 else False)
            for g in range(N_KV_HEADS):
                gr = slice(g * HEAD_DIM, (g + 1) * HEAD_DIM)
                st = _dot_tn(kf[gr, win], _group_stack(q_ref, g, cols) * ATTN_SCALE)
                for r in range(Q_PER_KV):
                    h, _ = _head_rows(g, r)
                    sl = slice(r * ATTN_BLOCK, (r + 1) * ATTN_BLOCK)
                    probs, _ = _attn_probs_t(st[:, sl], valid, s_ref[h])
                    pt[:, sl] = probs.astype(BF16)
                ot = _dot(vf[gr, win], pt[...])
                for r in range(Q_PER_KV):
                    o_ref[_head_rows(g, r)[1], cols] = ot[:, r * ATTN_BLOCK:(r + 1) * ATTN_BLOCK].astype(BF16)

    return pl.pallas_call(
        body, name=name, grid=(T // tq,),
        in_specs=_attn_specs(tq, lambda t: t) + [pl.BlockSpec(memory_space=pltpu.SMEM)],
        out_specs=pl.BlockSpec((ATTN_WIDTH, tq), lambda t: (0, t)),
        out_shape=SDS((ATTN_WIDTH, T), BF16),
        scratch_shapes=[pltpu.VMEM((KV_WIDTH, tq + ATTN_BLOCK), BF16)] * 2
        + [pltpu.VMEM((2 * ATTN_BLOCK, GROUP_LANES), BF16)],
        compiler_params=_params("parallel"),
    )(zt, zt, zt, zt, zt, sinks)


def _attn_bwd(zt, sinks, dot_, *, name):
    T = zt.shape[1]
    tq = _tile(T, TQ_ATTN, ATTN_BLOCK)
    nb = tq // ATTN_BLOCK
    nt = T // tq
    tile_of = lambda t: nt - 1 - t

    def body(q_ref, k_ref, v_ref, kh_ref, vh_ref, do_ref, s_ref, dz_ref, dsink_ref, kf, vf, dkf, dvf, carry, pt, dst):
        t = pl.program_id(0)

        @pl.when(t == 0)
        def _():
            carry[...] = jnp.zeros_like(carry)
            dsink_ref[...] = jnp.zeros_like(dsink_ref)

        kf[:, 0:ATTN_BLOCK] = kh_ref[...]
        kf[:, ATTN_BLOCK:] = k_ref[...]
        vf[:, 0:ATTN_BLOCK] = vh_ref[...]
        vf[:, ATTN_BLOCK:] = v_ref[...]
        dkf[...] = jnp.zeros_like(dkf)
        dvf[...] = jnp.zeros_like(dvf)
        dkf[:, tq:] = carry[0:KV_WIDTH, :]
        dvf[:, tq:] = carry[KV_WIDTH:, :]
        lane = lax.broadcasted_iota(jnp.int32, (1, LANES), 1)
        dsink = jnp.zeros((1, LANES), F32)
        for b in range(nb):
            cols = slice(b * ATTN_BLOCK, (b + 1) * ATTN_BLOCK)
            win = slice(b * ATTN_BLOCK, (b + 2) * ATTN_BLOCK)
            valid = _attn_mask_t((t == nt - 1) if b == 0 else False)
            for g in range(N_KV_HEADS):
                gr = slice(g * HEAD_DIM, (g + 1) * HEAD_DIM)
                kt2, vt2 = kf[gr, win], vf[gr, win]
                qst = _group_stack(q_ref, g, cols) * ATTN_SCALE
                dost = _group_stack(do_ref, g, cols)
                st = _dot_tn(kt2, qst)
                dpt = _dot_tn(vt2, dost)
                for r in range(Q_PER_KV):
                    h, _ = _head_rows(g, r)
                    sl = slice(r * ATTN_BLOCK, (r + 1) * ATTN_BLOCK)
                    probs, psink = _attn_probs_t(st[:, sl], valid, s_ref[h])
                    dp = dpt[:, sl]
                    delta = jnp.sum(probs * dp, axis=0, keepdims=True)
                    pt[:, sl] = probs.astype(BF16)
                    dst[:, sl] = (probs * (dp - delta)).astype(BF16)
                    dsink = dsink + jnp.where(lane == h, -jnp.sum(psink * delta), 0.0)
                dqt = _dot(kt2, dst[...]) * ATTN_SCALE
                for r in range(Q_PER_KV):
                    dz_ref[_head_rows(g, r)[1], cols] = dqt[:, r * ATTN_BLOCK:(r + 1) * ATTN_BLOCK].astype(BF16)
                dkf[gr, win] += _dot_nt(qst, dst[...])
                dvf[gr, win] += _dot_nt(dost, pt[...])
        dz_ref[ATTN_WIDTH:ATTN_WIDTH + KV_WIDTH, :] = dkf[:, ATTN_BLOCK:].astype(BF16)
        dz_ref[ATTN_WIDTH + KV_WIDTH:, :] = dvf[:, ATTN_BLOCK:].astype(BF16)
        carry[0:KV_WIDTH, :] = dkf[:, 0:ATTN_BLOCK]
        carry[KV_WIDTH:, :] = dvf[:, 0:ATTN_BLOCK]
        dsink_ref[...] += dsink

    return pl.pallas_call(
        body, name=name, grid=(nt,),
        in_specs=_attn_specs(tq, tile_of) + [pl.BlockSpec((ATTN_WIDTH, tq), lambda t: (0, tile_of(t))),
                                             pl.BlockSpec(memory_space=pltpu.SMEM)],
        out_specs=[pl.BlockSpec((QKV_WIDTH, tq), lambda t: (0, tile_of(t))),
                   pl.BlockSpec((8, LANES), lambda t: (0, 0))],
        out_shape=[SDS((QKV_WIDTH, T), BF16), SDS((8, LANES), F32)],
        scratch_shapes=[pltpu.VMEM((KV_WIDTH, tq + ATTN_BLOCK), BF16)] * 2
        + [pltpu.VMEM((KV_WIDTH, tq + ATTN_BLOCK), F32)] * 2 + [pltpu.VMEM((2 * KV_WIDTH, ATTN_BLOCK), F32)]
        + [pltpu.VMEM((2 * ATTN_BLOCK, GROUP_LANES), BF16)] * 2,
        compiler_params=_params("arbitrary"),
    )(zt, zt, zt, zt, zt, dot_, sinks)


def _layer_norm_stats(v):
    mu = jnp.mean(v, axis=-1, keepdims=True)
    xc = v - mu
    rstd = lax.rsqrt(jnp.mean(xc * xc, axis=-1, keepdims=True) + LN_EPS)
    return rstd, xc * rstd


def _sgu_fwd(zmain, ln_g, ln_b, wm, bias, *, name):
    T = zmain.shape[0]
    ts = _tile(T, TS_SGU, SGU_CHUNK)

    def body(u_ref, v_ref, g_ref, b_ref, w_ref, bias_ref, y_ref):
        u = _gelu(u_ref[...].astype(F32))
        _, vh = _layer_norm_stats(_gelu(v_ref[...].astype(F32)))
        vn = (vh * g_ref[...] + b_ref[...]).astype(BF16)
        for ch in range(ts // SGU_CHUNK):
            rows = slice(ch * SGU_CHUNK, (ch + 1) * SGU_CHUNK)
            for g in range(SGU_GROUPS):
                cols = slice(g * 128, (g + 1) * 128)
                s = _dot(w_ref[g], vn[rows, cols]) + bias_ref[g]
                y_ref[rows, cols] = (u[rows, cols] * s).astype(BF16)

    full = _resident
    return pl.pallas_call(
        body, name=name, grid=(T // ts,),
        in_specs=[pl.BlockSpec((ts, SGU_WIDTH), lambda i: (i, 0)), pl.BlockSpec((ts, SGU_WIDTH), lambda i: (i, 1)),
                  full((1, SGU_WIDTH)), full((1, SGU_WIDTH)), full(wm.shape), full(bias.shape)],
        out_specs=pl.BlockSpec((ts, SGU_WIDTH), lambda i: (i, 0)),
        out_shape=SDS((T, SGU_WIDTH), BF16),
        compiler_params=_params("parallel"),
    )(zmain, zmain, ln_g, ln_b, wm, bias)


def _sgu_bwd(zmain, dzmain, dy, ln_g, ln_b, wm, wmt, bias, *, name):
    T = zmain.shape[0]
    ts = _tile(T, TS_SGU, SGU_CHUNK)

    def body(u_ref, v_ref, dy_ref, g_ref, b_ref, w_ref, wt_ref, bias_ref, _, dz_ref, dw_ref, db_ref, dlg_ref, dlb_ref,
             dvn):
        @pl.when(pl.program_id(0) == 0)
        def _():
            dw_ref[...] = jnp.zeros_like(dw_ref)
            db_ref[...] = jnp.zeros_like(db_ref)
            dlg_ref[...] = jnp.zeros_like(dlg_ref)
            dlb_ref[...] = jnp.zeros_like(dlb_ref)

        us = u_ref[...].astype(F32)
        vs = v_ref[...].astype(F32)
        u = _gelu(us)
        rstd, vh = _layer_norm_stats(_gelu(vs))
        vn = (vh * g_ref[...] + b_ref[...]).astype(BF16)
        causal = (lax.broadcasted_iota(jnp.int32, (SGU_CHUNK, SGU_CHUNK), 0)
                  >= lax.broadcasted_iota(jnp.int32, (SGU_CHUNK, SGU_CHUNK), 1))
        lane = lax.broadcasted_iota(jnp.int32, (SGU_CHUNK, LANES), 1)
        db = jnp.zeros((SGU_CHUNK, LANES), F32)
        for ch in range(ts // SGU_CHUNK):
            rows = slice(ch * SGU_CHUNK, (ch + 1) * SGU_CHUNK)
            for g in range(SGU_GROUPS):
                cols = slice(g * 128, (g + 1) * 128)
                vng = vn[rows, cols]
                s = _dot(w_ref[g], vng) + bias_ref[g]
                dyf = dy_ref[rows, cols].astype(F32)
                dz_ref[rows, cols] = (dyf * s * _gelu_grad(us[rows, cols])).astype(BF16)
                dsf = dyf * u[rows, cols]
                dsb = dsf.astype(BF16)
                dvn[rows, cols] = _dot(wt_ref[g], dsb)
                dw_ref[g] += jnp.where(causal, _dot_nt(dsb, vng), 0.0)
                db = db + jnp.where(lane == g, jnp.sum(dsf, axis=1, keepdims=True), 0.0)
        db_ref[...] += db
        dvnf = dvn[...]
        dlg_ref[...] += jnp.sum(dvnf * vh, axis=0, keepdims=True)
        dlb_ref[...] += jnp.sum(dvnf, axis=0, keepdims=True)
        dvh = dvnf * g_ref[...]
        dv = rstd * (dvh - jnp.mean(dvh, axis=-1, keepdims=True) - vh * jnp.mean(dvh * vh, axis=-1, keepdims=True))
        dz_ref[:, SGU_WIDTH:] = (dv * _gelu_grad(vs)).astype(BF16)

    full = _resident
    vec = full((1, SGU_WIDTH))
    acc = lambda shape: pl.BlockSpec(shape, lambda i: (0,) * len(shape))
    return pl.pallas_call(
        body, name=name, grid=(T // ts,),
        in_specs=[pl.BlockSpec((ts, SGU_WIDTH), lambda i: (i, 0)), pl.BlockSpec((ts, SGU_WIDTH), lambda i: (i, 1)),
                  pl.BlockSpec((ts, SGU_WIDTH), lambda i: (i, 0)), vec, vec, full(wm.shape), full(wm.shape),
                  full(bias.shape), pl.BlockSpec(memory_space=pl.ANY)],
        out_specs=[pl.BlockSpec((ts, 2 * SGU_WIDTH), lambda i: (i, 0)), acc(wm.shape),
                   acc((SGU_CHUNK, LANES)), acc((1, SGU_WIDTH)), acc((1, SGU_WIDTH))],
        out_shape=[SDS(dzmain.shape, BF16), SDS(wm.shape, F32), SDS((SGU_CHUNK, LANES), F32),
                   SDS((1, SGU_WIDTH), F32), SDS((1, SGU_WIDTH), F32)],
        scratch_shapes=[pltpu.VMEM((ts, SGU_WIDTH), F32)],
        input_output_aliases={8: 0},
        compiler_params=_params("arbitrary"),
    )(zmain, zmain, dy, ln_g, ln_b, wm, wmt, bias, dzmain)


def _merge_fwd(y_attn_t, y_sgu, zmain, w_a, w_s, w_o, x, g_post, *, name):
    T, D = x.shape
    tm = _tile(T, TM_ROW, LANES)

    def body(ya_ref, ys_ref, ga_ref, gb_ref, wa_ref, ws_ref, wo_ref, x_ref, g_ref, xn_ref, pa_ref, ps_ref, o_ref):
        pa = _dot_tn(ya_ref[...], wa_ref[...])
        ps = _dot(ys_ref[...], ws_ref[...])
        pa_ref[...] = pa.astype(BF16)
        ps_ref[...] = ps.astype(BF16)
        merged = _sigmoid(ga_ref[...].astype(F32)) * pa + _sigmoid(gb_ref[...].astype(F32)) * ps
        out = _dot(merged.astype(BF16), wo_ref[...])
        o_ref[...] = out.astype(BF16)
        _, oh = _rms_stats(out)
        xn_ref[...] = x_ref[...] + oh * g_ref[...]

    row = lambda col: pl.BlockSpec((tm, D), lambda i: (i, col))
    wfull = _resident((D, D))
    return pl.pallas_call(
        body, name=name, grid=(T // tm,),
        in_specs=[pl.BlockSpec((D, tm), lambda i: (0, i)), row(0), row(2), row(3), wfull, wfull, wfull, row(0),
                  pl.BlockSpec((1, D), lambda i: (0, 0))],
        out_specs=[row(0)] * 4,
        out_shape=[SDS((T, D), F32), SDS((T, D), BF16), SDS((T, D), BF16), SDS((T, D), BF16)],
        compiler_params=_params("parallel"),
    )(y_attn_t, y_sgu, zmain, zmain, w_a, w_s, w_o, x, g_post)


def _merge_bwd(dy, out, g_post, pa, ps, zmain, w_a, w_s, w_o, *, name):
    T, D = dy.shape
    tm = _tile(T, TM_ROW, LANES)

    def body(dy_ref, o_ref, g_ref, pa_ref, ps_ref, ga_ref, gb_ref, wa_ref, ws_ref, wo_ref,
             dz_ref, dout_ref, mg_ref, dpa_ref, dps_ref, dya_ref, dys_ref, dg_ref):
        @pl.when(pl.program_id(0) == 0)
        def _():
            dg_ref[...] = jnp.zeros_like(dg_ref)

        dout, dg = _rms_bwd(o_ref[...].astype(F32), g_ref[...], dy_ref[...])
        dg_ref[...] += dg
        doutb = dout.astype(BF16)
        dout_ref[...] = doutb
        dm = _dot_nt(doutb, wo_ref[...]).astype(BF16)
        pa, ps = pa_ref[...], ps_ref[...]
        sa = _sigmoid(ga_ref[...].astype(F32))
        sb = _sigmoid(gb_ref[...].astype(F32))
        one_minus_sa, one_minus_sb = (1.0 - sa).astype(BF16), (1.0 - sb).astype(BF16)
        sa, sb = sa.astype(BF16), sb.astype(BF16)
        mg_ref[...] = sa * pa + sb * ps
        dpa = dm * sa
        dps = dm * sb
        dpa_ref[...] = dpa
        dps_ref[...] = dps
        dz_ref[:, 0:D] = dpa * pa * one_minus_sa
        dz_ref[:, D:] = dps * ps * one_minus_sb
        dya_ref[...] = _dot_nt(wa_ref[...], dpa).astype(BF16)
        dys_ref[...] = _dot_nt(dps, ws_ref[...]).astype(BF16)

    row = lambda col: pl.BlockSpec((tm, D), lambda i: (i, col))
    wfull = _resident((D, D))
    vec = pl.BlockSpec((1, D), lambda i: (0, 0))
    act = SDS((T, D), BF16)
    return pl.pallas_call(
        body, name=name, grid=(T // tm,),
        in_specs=[row(0), row(0), vec, row(0), row(0), row(2), row(3), wfull, wfull, wfull],
        out_specs=[pl.BlockSpec((tm, 2 * D), lambda i: (i, 1))] + [row(0)] * 4
        + [pl.BlockSpec((D, tm), lambda i: (0, i)), row(0), vec],
        out_shape=[SDS(zmain.shape, BF16)] + [act] * 4 + [SDS((D, T), BF16), act, SDS((1, D), F32)],
        compiler_params=_params("arbitrary"),
    )(dy, out, g_post, pa, ps, zmain, zmain, w_a, w_s, w_o)


def _loss_head(y, target, *, name):
    T, D = y.shape
    tm = _tile(T, TM_ROW, 8)

    def body(y_ref, t_ref, dy_ref, l_ref):
        @pl.when(pl.program_id(0) == 0)
        def _():
            l_ref[...] = jnp.zeros_like(l_ref)

        e = y_ref[...] - t_ref[...]
        dy_ref[...] = e * (1.0 / D)
        l_ref[...] += jnp.sum(jnp.mean(e * e, axis=-1, keepdims=True))

    row = pl.BlockSpec((tm, D), lambda i: (i, 0))
    return pl.pallas_call(
        body, name=name, grid=(T // tm,), in_specs=[row, row],
        out_specs=[row, pl.BlockSpec((8, LANES), lambda i: (0, 0))],
        out_shape=[SDS((T, D), F32), SDS((8, LANES), F32)],
        compiler_params=_params("arbitrary"),
    )(y, target)


def _adamw(w, g, m, v, *, name):
    shape = w.shape
    cols = shape[-1]
    rows = w.size // cols
    w2, g2, m2, v2 = (t.reshape(rows, cols) for t in (w, g, m, v))
    tr = _tile(rows, max(8, (256 * 1024) // cols // 8 * 8), 8)

    def body(w_ref, g_ref, m_ref, v_ref, d_ref, nm_ref, nv_ref):
        gg = g_ref[...]
        nm = ADAM_B1 * m_ref[...] + (1.0 - ADAM_B1) * gg
        nv = ADAM_B2 * v_ref[...] + (1.0 - ADAM_B2) * (gg * gg)
        m_hat = nm / (1.0 - ADAM_B1 ** ADAM_STEP)
        v_hat = nv / (1.0 - ADAM_B2 ** ADAM_STEP)
        d_ref[...] = -ADAM_LR * (m_hat / (jnp.sqrt(v_hat) + ADAM_EPS) + ADAM_WD * w_ref[...])
        nm_ref[...] = nm
        nv_ref[...] = nv

    blk = pl.BlockSpec((tr, cols), lambda i: (i, 0))
    outs = pl.pallas_call(
        body, name=name, grid=(rows // tr,), in_specs=[blk] * 4, out_specs=[blk] * 3,
        out_shape=[SDS((rows, cols), F32)] * 3, compiler_params=_params("parallel"),
    )(w2, g2, m2, v2)
    return tuple(o.reshape(shape) for o in outs)


def _sum_terms(terms, n_rows, n_lead, dtypes, *, name, out_lead=None, out_n_lead=None):
    cols = terms[0][0].shape[-1]
    tr = _tile(n_rows, 704 if len(terms) <= 4 else 256, SUBLANES_BF16)
    nblk = n_rows // tr
    n_out = len(dtypes)

    def body(*refs):
        acc = refs[0][...].astype(F32)
        for r in refs[1:-n_out]:
            acc = acc + r[...].astype(F32)
        for o_ref in refs[-n_out:]:
            o_ref[...] = acc.astype(o_ref.dtype)

    def spec(lead, first):
        return pl.BlockSpec((1, tr, cols), lambda a, i: (lead(a), first(a) * nblk + i, 0))

    out = pl.BlockSpec((1, tr, cols), lambda a, i: (a if out_lead is None else out_lead(a), i, 0))
    return pl.pallas_call(
        body, name=name, grid=(n_lead, nblk), in_specs=[spec(lead, first) for _, lead, first in terms],
        out_specs=[out] * n_out, out_shape=[SDS((out_n_lead or n_lead, n_rows, cols), d) for d in dtypes],
        compiler_params=_params("arbitrary", "arbitrary"),
    )(*[a for a, _, _ in terms])


def _position():
    x, y, c = (lax.axis_index(a) for a in AXES)
    chips = [(1 - x, y), (x, 1 - y), (1 - x, 1 - y)]
    return x, y, c, chips


ANY = pl.BlockSpec(memory_space=pl.ANY)


def _remote(src, dst, send_sems, recv_sems, k, to):
    return pltpu.make_async_remote_copy(src_ref=src, dst_ref=dst, send_sem=send_sems.at[k], recv_sem=recv_sems.at[k],
                                        device_id=to, device_id_type=MESH)


def _comm_call(body, arrays, out_shapes, n_sems, *, name):
    n = len(arrays)

    def wrapped(*refs):
        body(refs[:n], refs[n:n + len(out_shapes)], refs[-2], refs[-1])

    return pl.pallas_call(
        wrapped, name=name, in_specs=[ANY] * n, out_specs=[ANY] * len(out_shapes), out_shape=out_shapes,
        scratch_shapes=[pltpu.SemaphoreType.DMA((n_sems,)), pltpu.SemaphoreType.DMA((n_sems,))],
    )(*arrays)


def _gather_shards(packs, *, name):
    def body(p_refs, o_refs, send_sems, recv_sems):
        for stage in _gather_stages(p_refs, o_refs, send_sems, recv_sems):
            stage()

    for p in packs:
        assert p.shape[0] % (4 * SUBLANES_BF16) == 0
    return _comm_call(body, packs, _gathered_shapes(packs), GATHER_SEMS * len(packs), name=name)


GATHER_SEMS = 8


def _gathered_shapes(packs):
    return [SDS((N_CHIPS,) + p.shape, p.dtype) for p in packs]


def _gather_stages(p_refs, o_refs, send_sems, recv_sems):
    NS = GATHER_SEMS
    x, y, c, _ = _position()
    me, sib = (x, y, c), (x, y, 1 - c)
    xn, yn = (1 - x, y, c), (x, 1 - y, c)
    s_me, s_xn, s_yn, s_dg = 2 * x + y, 2 * (1 - x) + y, 2 * x + 1 - y, 2 * (1 - x) + 1 - y

    def copy(a, k):
        p_ref, o_ref = p_refs[a], o_refs[a]
        rh = p_ref.shape[0] // 2
        rq = rh // 2
        half, q0, q1 = pl.ds(c * rh, rh), pl.ds(c * rh, rq), pl.ds(c * rh + rq, rq)
        src, dst, to = [(p_ref.at[half], o_ref.at[s_me, half], xn), (p_ref.at[half], o_ref.at[s_me, half], yn),
                        (o_ref.at[s_xn, q0],) * 2 + (yn,), (o_ref.at[s_yn, q1],) * 2 + (xn,),
                        (o_ref.at[s_xn, half],) * 2 + (sib,), (o_ref.at[s_yn, half],) * 2 + (sib,),
                        (o_ref.at[s_dg, q0],) * 2 + (sib,), (o_ref.at[s_dg, q1],) * 2 + (sib,)][k]
        return _remote(src, dst, send_sems, recv_sems, NS * a + k, to)

    def landed(a, k):
        o_ref = o_refs[a]
        rh = o_ref.shape[1] // 2
        rq = rh // 2
        o = (1 - c) * rh
        dst = [o_ref.at[s_xn, pl.ds(c * rh, rh)], o_ref.at[s_yn, pl.ds(c * rh, rh)],
               o_ref.at[s_dg, pl.ds(c * rh, rq)], o_ref.at[s_dg, pl.ds(c * rh + rq, rq)],
               o_ref.at[s_xn, pl.ds(o, rh)], o_ref.at[s_yn, pl.ds(o, rh)],
               o_ref.at[s_dg, pl.ds(o, rq)], o_ref.at[s_dg, pl.ds(o + rq, rq)]][k]
        _remote(dst, dst, send_sems, recv_sems, NS * a + k, me).wait_recv()

    n = len(p_refs)

    def stage_a():
        for a in range(n):
            for k in (0, 1):
                copy(a, k).start()

    def stage_b():
        for a in range(n):
            landed(a, 0)
            copy(a, 2).start()
            copy(a, 4).start()
            landed(a, 1)
            copy(a, 3).start()
            copy(a, 5).start()

    def stage_c():
        for a in range(n):
            landed(a, 2)
            copy(a, 6).start()
            landed(a, 3)
            copy(a, 7).start()
        for a in range(n):
            for k in (4, 5, 6, 7):
                landed(a, k)
        for a in range(n):
            for k in range(NS):
                copy(a, k).wait_send()

    return stage_a, stage_b, stage_c


def _sibling_exchange(gs, *, name):
    def body(g_refs, o_refs, send_sems, recv_sems):
        x, y, c, _ = _position()
        sent = [_remote(g_ref.at[:, pl.ds((1 - c) * o_ref.shape[1], o_ref.shape[1])], o_ref, send_sems, recv_sems, a,
                        (x, y, 1 - c)) for a, (g_ref, o_ref) in enumerate(zip(g_refs, o_refs))]
        for cp in sent:
            cp.start()
        for cp in sent:
            cp.wait()

    return _comm_call(body, gs, [SDS((N_CHIPS, g.shape[1] // 2, g.shape[2]), g.dtype) for g in gs], len(gs), name=name)


def _scatter_hop1(ps, *, name):
    def body(p_refs, o_refs, send_sems, recv_sems):
        x, y, c, _ = _position()
        xn, yn = (1 - x, y, c), (x, 1 - y, c)
        s_xn, s_yn, s_dg = 2 * (1 - x) + y, 2 * x + 1 - y, 2 * (1 - x) + 1 - y
        sent = []
        for a, (p_ref, o_ref) in enumerate(zip(p_refs, o_refs)):
            rq = o_ref.shape[1]
            first, second = pl.ds(0, rq), pl.ds(rq, rq)
            sent += [_remote(p_ref.at[s_xn, second], o_ref.at[0], send_sems, recv_sems, 4 * a, xn),
                     _remote(p_ref.at[s_dg, second], o_ref.at[1], send_sems, recv_sems, 4 * a + 1, xn),
                     _remote(p_ref.at[s_yn, first], o_ref.at[2], send_sems, recv_sems, 4 * a + 2, yn),
                     _remote(p_ref.at[s_dg, first], o_ref.at[3], send_sems, recv_sems, 4 * a + 3, yn)]
        for cp in sent:
            cp.start()
        for cp in sent:
            cp.wait()

    return _comm_call(body, ps, [SDS((4, p.shape[1] // 2, p.shape[2]), p.dtype) for p in ps], 4 * len(ps), name=name)


def _scatter_hop2(fs, *, name):
    def body(f_refs, o_refs, send_sems, recv_sems):
        x, y, c, _ = _position()
        sent = []
        for a, (f_ref, o_ref) in enumerate(zip(f_refs, o_refs)):
            sent += [_remote(f_ref.at[0], o_ref.at[0], send_sems, recv_sems, 2 * a, (1 - x, y, c)),
                     _remote(f_ref.at[1], o_ref.at[1], send_sems, recv_sems, 2 * a + 1, (x, 1 - y, c))]
        for cp in sent:
            cp.start()
        for cp in sent:
            cp.wait()

    return _comm_call(body, fs, [SDS(f.shape, f.dtype) for f in fs], 2 * len(fs), name=name)


def _sibling_fill(rs, *, name):
    n = len(rs)

    def body(*refs):
        r_refs, send_sems, recv_sems = refs[n:2 * n], refs[-2], refs[-1]
        x, y, c, _ = _position()
        sent = []
        for a, r_ref in enumerate(r_refs):
            mine = r_ref.at[pl.ds(c * (r_ref.shape[0] // 2), r_ref.shape[0] // 2)]
            sent.append(_remote(mine, mine, send_sems, recv_sems, a, (x, y, 1 - c)))
            sent[-1].start()
        for a, r_ref in enumerate(r_refs):
            theirs = r_ref.at[pl.ds((1 - c) * (r_ref.shape[0] // 2), r_ref.shape[0] // 2)]
            _remote(theirs, theirs, send_sems, recv_sems, a, (x, y, c)).wait_recv()
        for cp in sent:
            cp.wait_send()

    return pl.pallas_call(
        body, name=name, in_specs=[ANY] * n, out_specs=[ANY] * n, out_shape=[SDS(r.shape, r.dtype) for r in rs],
        input_output_aliases={i: i for i in range(n)},
        scratch_shapes=[pltpu.SemaphoreType.DMA((n,)), pltpu.SemaphoreType.DMA((n,))],
    )(*rs)


def _gather_all(v, *, name):
    M, C = v.shape

    def body(v_ref, o_ref, send_sems, recv_sems):
        x, y, c, chips = _position()
        slot = lambda px, py, pc: o_ref.at[4 * px + 2 * py + pc]
        first = [_remote(v_ref, slot(x, y, c), send_sems, recv_sems, 0, (x, y, 1 - c))]
        first += [_remote(v_ref, slot(x, y, c), send_sems, recv_sems, 1 + j, (*chip, c)) for j, chip in enumerate(chips)]
        for cp in first:
            cp.start()
        passed = []
        for j, chip in enumerate(chips):
            landed = slot(*chip, c)
            _remote(landed, landed, send_sems, recv_sems, 1 + j, (x, y, c)).wait_recv()
            cp = _remote(landed, landed, send_sems, recv_sems, 4 + j, (x, y, 1 - c))
            cp.start()
            passed.append(cp)
        sib = slot(x, y, 1 - c)
        _remote(sib, sib, send_sems, recv_sems, 0, (x, y, c)).wait_recv()
        for j, chip in enumerate(chips):
            theirs = slot(*chip, 1 - c)
            _remote(theirs, theirs, send_sems, recv_sems, 4 + j, (x, y, c)).wait_recv()
        for cp in first + passed:
            cp.wait_send()

    return pl.pallas_call(
        body, name=name, in_specs=[ANY], out_specs=ANY, out_shape=SDS((N_DEV, M, C), v.dtype),
        scratch_shapes=[pltpu.SemaphoreType.DMA((7,)), pltpu.SemaphoreType.DMA((7,))],
    )(v)


BIG = ("ffn1_w1", "ffn2_w1", "w_in", "ffn1_w2", "ffn2_w2", "w_attn_branch", "w_sgu_branch", "w_out")
COL_SHARDED = ("ffn1_w1", "w_in", "ffn2_w1")
FFN_IN = ("ffn1_w1", "ffn2_w1")
SMALL = ("ffn1_pre_g", "ffn1_post_g", "mix_pre_g", "attn_sinks", "sgu_ln_g", "sgu_ln_b", "sgu_w", "sgu_b",
         "mix_post_g", "ffn2_pre_g", "ffn2_post_g")
WEIGHTS = ("ffn1_pre_g", "ffn1_w1", "ffn1_w2", "ffn1_post_g", "mix_pre_g", "w_in", "attn_sinks", "sgu_ln_g",
           "sgu_ln_b", "sgu_w", "sgu_b", "w_attn_branch", "w_sgu_branch", "w_out", "mix_post_g", "ffn2_pre_g",
           "ffn2_w1", "ffn2_w2", "ffn2_post_g")


def _column_chunks(w, tn):
    return jnp.swapaxes(w.reshape(w.shape[0], w.shape[1] // tn, tn), 0, 1)


def _width_classes(shard_shapes):
    widths = sorted({shard_shapes[n][-1] for n in BIG}, reverse=True)
    return [[n for n in BIG if shard_shapes[n][-1] == w] for w in widths]


def _class_rows(classes, shard_shapes, n_layers, aligned=BIG):
    where = {}
    for k, names in enumerate(classes):
        off = 0
        for n in names:
            r = shard_shapes[n][0]
            assert off % r == 0 or n not in aligned
            where[n] = (k, off, r)
            off += n_layers * r
    return where


def _ffn_fwd(x, pre_g, w1, w1_block, w2, post_g, tag, gather=()):
    a, h, *gathered = _norm_matmul(x, pre_g, w1, w1_block, name=f"{tag}_up", with_h=True, gather=gather)
    xn, o = _swiglu_out(a, w2, x, post_g, name=f"{tag}_down")
    return xn, (x, h, a, o), gathered


def _ffn_bwd(dy, saved, pre_g, w1, w1_block, w2, post_g, dw1_into, dw2_into, tag, ride=None):
    x, h, a, o = saved
    da, do, d_post = _ffn_bwd_hidden(dy, o, post_g, a, w2, name=f"{tag}_bwd_hidden")
    g2 = _dw_rows(a, do, *dw2_into, name=f"{tag}_dw2", swiglu=True)
    dx, d_pre = _matmul_nt_norm_bwd(da, w1, w1_block, x, pre_g, dy, None, name=f"{tag}_bwd_in")
    if ride is None:
        return dx, _dw_cols(h, da, N_CHIPS, *dw1_into, name=f"{tag}_dw1"), g2, d_pre, d_post, ()
    g1, *from_sibling = _dw_cols(h, da, N_CHIPS, *dw1_into, name=f"{tag}_dw1", ride=[g2] + list(ride))
    return dx, g1, g2, d_pre, d_post, from_sibling


def kernel(x, ffn1_pre_g, ffn1_w1, ffn1_w2, ffn1_post_g, mix_pre_g, w_in, attn_sinks, sgu_ln_g, sgu_ln_b, sgu_w, sgu_b, w_attn_branch, w_sgu_branch, w_out, mix_post_g, ffn2_pre_g, ffn2_w1, ffn2_w2, ffn2_post_g, loss_target, m_ffn1_pre_g, m_ffn1_w1, m_ffn1_w2, m_ffn1_post_g, m_mix_pre_g, m_w_in, m_attn_sinks, m_sgu_ln_g, m_sgu_ln_b, m_sgu_w, m_sgu_b, m_w_attn_branch, m_w_sgu_branch, m_w_out, m_mix_post_g, m_ffn2_pre_g, m_ffn2_w1, m_ffn2_w2, m_ffn2_post_g, v_ffn1_pre_g, v_ffn1_w1, v_ffn1_w2, v_ffn1_post_g, v_mix_pre_g, v_w_in, v_attn_sinks, v_sgu_ln_g, v_sgu_ln_b, v_sgu_w, v_sgu_b, v_w_attn_branch, v_w_sgu_branch, v_w_out, v_mix_post_g, v_ffn2_pre_g, v_ffn2_w1, v_ffn2_w2, v_ffn2_post_g):
    given = dict(locals())
    W = {n: given[n] for n in WEIGHTS}
    M = {n: given["m_" + n] for n in WEIGHTS}
    V = {n: given["v_" + n] for n in WEIGHTS}
    L = ffn1_w1.shape[0]
    T, D = x.shape[1], x.shape[2]
    xt = x.reshape(T, D)
    target = loss_target.reshape(T, D)
    assert L % 2 == 0 and D == ATTN_WIDTH == SGU_WIDTH and T % ATTN_BLOCK == 0

    shard_shapes = {n: W[n].shape[1:] for n in BIG}
    classes = _width_classes(shard_shapes)
    my_chip = 2 * lax.axis_index("x") + lax.axis_index("y")
    my_core = lax.axis_index("c")
    where = _class_rows(classes, shard_shapes, L)
    where_w = _class_rows(classes, shard_shapes, 1, aligned=FFN_IN)
    packs = [[jnp.concatenate([W[n][l].astype(BF16) for n in names], axis=0) for names in classes] for l in range(L)]

    def block_of(n, l):
        k, off, r = where[n]
        return k, off // r + l

    def layer_weights(l, got):
        wc = [lax.dynamic_update_slice(g, p[None], (my_chip, 0, 0)) for g, p in zip(got, packs[l])]

        def chip_shards(n):
            k, off, r = where_w[n]
            return wc[k][:, off:off + r, :]

        fw = {n: chip_shards(n).reshape(-1, D) for n in BIG if n not in COL_SHARDED}
        w_in_l = jnp.swapaxes(chip_shards("w_in"), 0, 1).reshape(D, -1)
        fw["w_qkv_t"] = w_in_l[:, :QKV_WIDTH].T
        fw["w_main"] = _column_chunks(w_in_l[:, QKV_WIDTH:], D)
        for n in FFN_IN:
            fw[n] = (wc[where_w[n][0]], where_w[n][1] // where_w[n][2])
        return fw

    full = [None] * L
    full[0] = layer_weights(0, _gather_shards(packs[0], name="gather_weights_l0"))

    row = lambda name, l: W[name][l].reshape(1, -1)
    causal = jnp.tril(jnp.ones((SGU_CHUNK, SGU_CHUNK), dtype=bool))
    saved = []
    h_cur = xt
    for l in range(L):
        fw = full[l]
        sv = {}
        h_cur, sv["ffn1"], got = _ffn_fwd(h_cur, row("ffn1_pre_g", l), *fw["ffn1_w1"], fw["ffn1_w2"],
                                          row("ffn1_post_g", l), f"l{l}_ffn1", gather=packs[l + 1] if l + 1 < L else ())
        if l + 1 < L:
            full[l + 1] = layer_weights(l + 1, got)
        zqkv, hm = _norm_matmul_t(h_cur, row("mix_pre_g", l), fw["w_qkv_t"], name=f"l{l}_mix_in_qkv")
        zmain, = _norm_matmul(h_cur, row("mix_pre_g", l), fw["w_main"], 0, name=f"l{l}_mix_in_main", with_h=False)
        wm = jnp.where(causal[None], sgu_w[l], 0.0).astype(BF16)
        wmt = jnp.swapaxes(wm, 1, 2)
        bias = jnp.broadcast_to(sgu_b[l][:, :, None], (SGU_GROUPS, SGU_CHUNK, 128)).astype(F32)
        y_attn = _attn_fwd(zqkv, attn_sinks[l], name=f"l{l}_attn")
        y_sgu = _sgu_fwd(zmain, row("sgu_ln_g", l), row("sgu_ln_b", l), wm, bias, name=f"l{l}_sgu")
        x_mix = h_cur
        h_cur, pa, ps, mo = _merge_fwd(y_attn, y_sgu, zmain, fw["w_attn_branch"], fw["w_sgu_branch"], fw["w_out"],
                                       x_mix, row("mix_post_g", l), name=f"l{l}_merge")
        sv["mix"] = (x_mix, hm, zqkv, zmain, y_attn, y_sgu, pa, ps, mo, wm, wmt, bias)
        h_cur, sv["ffn2"], _ = _ffn_fwd(h_cur, row("ffn2_pre_g", l), *fw["ffn2_w1"], fw["ffn2_w2"],
                                        row("ffn2_post_g", l), f"l{l}_ffn2")
        saved.append(sv)

    dy, lsum = _loss_head(h_cur, target, name="loss_head")
    loss = lax.psum(0.5 * lsum[0, 0], AXES)

    k_in = where["w_in"][0]
    assert classes[k_in] == ["w_in"]
    gcls = [None if k == k_in else lax.empty((N_CHIPS, L * p.shape[0], p.shape[1]), F32)
            for k, p in enumerate(packs[0])]
    dw_in = [None] * L
    small_grads = [None] * L

    def into(n, l):
        return gcls[block_of(n, l)[0]], block_of(n, l)[1]

    def ffn_bwd(dy, which, l, ride=None):
        n1, n2 = f"{which}_w1", f"{which}_w2"
        dy, g1, g2, d_pre, d_post, from_sibling = _ffn_bwd(
            dy, saved[l][which], row(f"{which}_pre_g", l), *full[l][n1], full[l][n2], row(f"{which}_post_g", l),
            into(n1, l), into(n2, l), f"l{l}_{which}", ride)
        gcls[where[n1][0]], gcls[where[n2][0]] = g1, g2
        return dy, d_pre, d_post, from_sibling

    for l in reversed(range(L)):
        fw, sv = full[l], saved[l]
        gs = {}
        dy, gs["ffn2_pre_g"], gs["ffn2_post_g"], _ = ffn_bwd(dy, "ffn2", l)

        x_mix, hm, zqkv, zmain, y_attn, y_sgu, pa, ps, mo, wm, wmt, bias = sv["mix"]
        dzmain, dout, merged, dpa, dps, dya, dys, gs["mix_post_g"] = _merge_bwd(
            dy, mo, row("mix_post_g", l), pa, ps, zmain, fw["w_attn_branch"], fw["w_sgu_branch"], fw["w_out"],
            name=f"l{l}_merge_bwd")
        k_sq = where["w_out"][0]
        gcls[k_sq] = _dw_rows(merged, dout, *into("w_out", l), name=f"l{l}_dw_out")
        gcls[k_sq] = _dw_rows(y_attn, dpa, *into("w_attn_branch", l), name=f"l{l}_dw_attn", a_feature_major=True)
        gcls[k_sq] = _dw_rows(y_sgu, dps, *into("w_sgu_branch", l), name=f"l{l}_dw_sgu")
        dzqkv, dsink = _attn_bwd(zqkv, attn_sinks[l], dya, name=f"l{l}_attn_bwd")
        gs["attn_sinks"] = dsink[0, :N_Q_HEADS]
        dzmain, dsw, dsb, gs["sgu_ln_g"], gs["sgu_ln_b"] = _sgu_bwd(
            zmain, dzmain, dys, row("sgu_ln_g", l), row("sgu_ln_b", l), wm, wmt, bias, name=f"l{l}_sgu_bwd")
        gs["sgu_w"] = dsw
        gs["sgu_b"] = dsb[:, :SGU_GROUPS].T
        dh_qkv = _matmul_tn_rows(dzqkv, fw["w_qkv_t"], name=f"l{l}_mix_bwd_qkv")
        dy, gs["mix_pre_g"] = _matmul_nt_norm_bwd(dzmain, fw["w_main"], 0, x_mix, row("mix_pre_g", l), dy, dh_qkv,
                                                   name=f"l{l}_mix_bwd_in")
        dw_main = _dw_cols(hm, dzmain, zmain.shape[1] // D, None, 0, name=f"l{l}_dw_in_main")
        dw_in[l] = jnp.concatenate([_matmul_tokens(dzqkv, hm, name=f"l{l}_dw_in_qkv").T,
                                    jnp.swapaxes(dw_main, 0, 1).reshape(D, -1)], axis=1)

        if l > 0:
            dy, gs["ffn1_pre_g"], gs["ffn1_post_g"], _ = ffn_bwd(dy, "ffn1", l)
        else:
            w_in_width = shard_shapes["w_in"][1]
            gcls[k_in] = jnp.stack([jnp.concatenate([g[:, s * w_in_width:(s + 1) * w_in_width] for g in dw_in], axis=0)
                                    for s in range(N_CHIPS)])
            k_up, k_down = where["ffn1_w1"][0], where["ffn1_w2"][0]
            assert sorted((k_up, k_down, k_in)) == list(range(len(classes)))
            dy, gs["ffn1_pre_g"], gs["ffn1_post_g"], rode = ffn_bwd(dy, "ffn1", l, ride=[gcls[k_in]])
        small_grads[l] = gs
    grad_x = dy.reshape(x.shape)

    grs = gcls
    from_sibling = [None] * len(classes)
    from_sibling[k_down], from_sibling[k_in] = rode
    from_sibling[k_up], = _sibling_exchange([grs[k_up]], name="grads_sibling_exchange")
    zero = lambda a: 0
    own = lambda a: a
    core = lambda a: lax.axis_index("c")
    chip = lambda a: 2 * lax.axis_index("x") + lax.axis_index("y")
    chip_xn = lambda a: 2 * (1 - lax.axis_index("x")) + lax.axis_index("y")
    chip_yn = lambda a: 2 * lax.axis_index("x") + 1 - lax.axis_index("y")
    pairs = [_sum_terms([(g, own, core), (fs, own, zero)], fs.shape[1], N_CHIPS, (BF16,),
                        name=f"grads_pair_sum{k}")[0] for k, (g, fs) in enumerate(zip(grs, from_sibling))]
    hop1 = _scatter_hop1(pairs, name="grads_scatter_hop1")
    relay = [_sum_terms([(p, lambda a: chip_xn(a) + a * (chip_yn(a) - chip_xn(a)), own), (h, lambda a: 3 - 2 * a, zero)],
                        h.shape[1], 2, (BF16,), name=f"grads_relay_sum{k}")[0]
             for k, (p, h) in enumerate(zip(pairs, hop1))]
    hop2 = _scatter_hop2(relay, name="grads_scatter_hop2")
    quarter = lambda a: 2 * core(a) + a
    halves = [_sum_terms([(g, chip, quarter), (fs, chip, own), (h1, lambda a: 2 - 2 * a, zero), (h2, own, zero)],
                         h1.shape[1], 2, (F32,), name=f"grads_chip_sum{k}", out_lead=quarter, out_n_lead=4)[0]
              .reshape(g.shape[1:]) for k, (g, fs, h1, h2) in enumerate(zip(grs, from_sibling, hop1, hop2))]
    reduced_all = _sibling_fill(halves, name="grads_sibling_fill")

    grads = {n: [None] * L for n in SMALL}
    for names, reduced in zip(classes, reduced_all):
        for n in names:
            _, off, r = where[n]
            grads[n] = reduced[off:off + L * r].reshape((L,) + shard_shapes[n])

    def small_rows(gs):
        parts = []
        for n in SMALL:
            flat = gs[n].reshape(-1)
            pad = (-flat.shape[0]) % D
            parts.append(jnp.pad(flat, (0, pad)).reshape(-1, D))
        return jnp.concatenate(parts, axis=0)

    spack = jnp.concatenate([small_rows(small_grads[l]) for l in range(L)], axis=0)
    n_small = spack.shape[0]
    pad_rows = (-n_small) % SUBLANES_BF16
    spack = jnp.pad(spack, ((0, pad_rows), (0, 0)))
    everyone = _gather_all(spack, name="small_grads_gather")
    is_me = (jnp.arange(N_DEV) == 2 * my_chip + my_core)[:, None, None]
    everyone = jnp.where(is_me, spack[None], everyone)
    ssum = _sum_terms([(everyone, (lambda a, d=d: d), zero) for d in range(N_DEV)], spack.shape[0], 1, (F32,),
                      name="small_grads_sum")[0][0]
    per_layer = n_small // L
    for l in range(L):
        r0 = l * per_layer
        for n in SMALL:
            shp = W[n].shape[1:]
            size = math.prod(shp)
            nr = -(-size // D)
            grads[n][l] = ssum[r0:r0 + nr].reshape(-1)[:size].reshape(shp)
            r0 += nr
    grads.update({n: jnp.stack(grads[n]) for n in SMALL})

    delta, new_m, new_v = {}, {}, {}
    for n in WEIGHTS:
        delta[n], new_m[n], new_v[n] = _adamw(W[n], grads[n], M[n], V[n], name=f"adamw_{n}")

    return (loss, grad_x, *[grads[n] for n in WEIGHTS], *[delta[n] for n in WEIGHTS],
            *[new_m[n] for n in WEIGHTS], *[new_v[n] for n in WEIGHTS])
```

```python
import functools
import math

import jax
import jax.numpy as jnp
from jax import lax
from jax.experimental import pallas as pl
from jax.experimental.pallas import tpu as pltpu

F32, BF16 = jnp.float32, jnp.bfloat16
SDS = jax.ShapeDtypeStruct
MESH = pl.DeviceIdType.MESH
AXES = ("x", "y", "c")

HEAD_DIM = 64
N_Q_HEADS = 16
N_KV_HEADS = 2
Q_PER_KV = N_Q_HEADS // N_KV_HEADS
ATTN_WIDTH = N_Q_HEADS * HEAD_DIM
KV_WIDTH = N_KV_HEADS * HEAD_DIM
ATTN_BLOCK = 128
SGU_CHUNK = 128
SGU_GROUPS = 8
SGU_WIDTH = SGU_GROUPS * 128
QKV_WIDTH = ATTN_WIDTH + 2 * KV_WIDTH
RMS_EPS = 1e-6
LN_EPS = 1e-5
MASK_VALUE = -1e30
ATTN_SCALE = 1.0 / math.sqrt(HEAD_DIM)
assert math.frexp(ATTN_SCALE)[0] == 0.5

ADAM_LR, ADAM_B1, ADAM_B2, ADAM_EPS, ADAM_WD, ADAM_STEP = 0.001, 0.9, 0.999, 1e-08, 0.01, 10

N_CHIPS = 4
N_DEV = 8

VMEM_LIMIT_BYTES = 56 * 1024 * 1024
LANES = 128
SUBLANES_BF16 = 16

TM_NORM_MATMUL = 1024
TM_ROW = 512
TM_FFN_BWD = 512
TT_REDUCE = 1024
TQ_ATTN = 1024
TM_FEATURE_MAJOR = 1024
TS_SGU = 512


def _tile(n, pref, mult):
    t = (min(pref, n) // mult) * mult
    while t >= mult:
        if n % t == 0:
            return t
        t -= mult
    return n


def _params(*sem):
    return pltpu.CompilerParams(dimension_semantics=sem, vmem_limit_bytes=VMEM_LIMIT_BYTES)


def _dot(a, b):
    return jnp.dot(a, b, preferred_element_type=F32)


def _dot_nt(a, b):
    return lax.dot_general(a, b, (((1,), (1,)), ((), ())), preferred_element_type=F32)


def _dot_tn(a, b):
    return lax.dot_general(a, b, (((0,), (0,)), ((), ())), preferred_element_type=F32)


def _sigmoid(x):
    return 0.5 * (1.0 + jnp.tanh(0.5 * x))


def _rms_stats(xf):
    r = lax.rsqrt(jnp.mean(xf * xf, axis=-1, keepdims=True) + RMS_EPS)
    return r, xf * r


def _rms_bwd(xf, g, dy):
    r, xh = _rms_stats(xf)
    dyg = dy * g
    dx = r * (dyg - xh * jnp.mean(dyg * xh, axis=-1, keepdims=True))
    return dx, jnp.sum(dy * xh, axis=0, keepdims=True)


def _gelu_parts(x):
    cdf = 0.5 * (1.0 + lax.erf(x * (1.0 / math.sqrt(2.0))))
    return cdf


def _gelu(x):
    return x * _gelu_parts(x)


def _gelu_grad(x):
    return _gelu_parts(x) + x * jnp.exp(-0.5 * x * x) * (1.0 / math.sqrt(2.0 * math.pi))


def _resident(shape, index=None):
    index = (0,) * len(shape) if index is None else index
    return pl.BlockSpec(shape, lambda *_: index, pipeline_mode=pl.Buffered(1))


def _norm_matmul(x, g, w3, w_block, *, name, with_h, gather=()):
    T, D = x.shape
    nj, _, tn = w3.shape
    tm = _tile(T, TM_NORM_MATMUL, SUBLANES_BF16)
    ni, ng = T // tm, len(gather)
    n_out = 1 + with_h

    def body(x_ref, g_ref, w_ref, *rest):
        a_ref, h_sc = rest[ng], rest[ng + n_out + ng]
        i, j = pl.program_id(0), pl.program_id(1)
        if ng:
            stages = _gather_stages(rest[:ng], rest[ng + n_out:ng + n_out + ng], *rest[-2:])
            for stage, (si, sj) in zip(stages[:2], ((0, 0), (ni // 2, 0))):
                pl.when((i == si) & (j == sj))(stage)

        @pl.when(j == 0)
        def _():
            _, xh = _rms_stats(x_ref[...])
            h = (xh * g_ref[...]).astype(BF16)
            h_sc[...] = h
            if with_h:
                rest[ng + 1][...] = h

        a_ref[...] = _dot(h_sc[...], w_ref[j]).astype(BF16)
        if ng:
            pl.when((i == ni - 1) & (j == nj - 1))(stages[2])

    out_specs = [pl.BlockSpec((tm, tn), lambda i, j: (i, j))]
    out_shape = [SDS((T, nj * tn), BF16)]
    if with_h:
        out_specs.append(pl.BlockSpec((tm, D), lambda i, j: (i, 0)))
        out_shape.append(SDS((T, D), BF16))
    scratch = [pltpu.VMEM((tm, D), BF16)]
    if ng:
        scratch += [pltpu.SemaphoreType.DMA((GATHER_SEMS * ng,))] * 2
    return pl.pallas_call(
        body, name=name, grid=(ni, nj),
        in_specs=[pl.BlockSpec((tm, D), lambda i, j: (i, 0)),
                  pl.BlockSpec((1, D), lambda i, j: (0, 0)),
                  _resident((nj, D, tn), (0, w_block, 0))] + [ANY] * ng,
        out_specs=out_specs + [ANY] * ng, out_shape=out_shape + _gathered_shapes(gather),
        scratch_shapes=scratch,
        compiler_params=_params(*(("arbitrary", "arbitrary") if ng else ("parallel", "arbitrary"))),
    )(x, g, w3, *gather)


def _norm_matmul_t(x, g, wt, *, name):
    T, D = x.shape
    N = wt.shape[0]
    tm = _tile(T, TM_FEATURE_MAJOR, LANES)

    def body(x_ref, g_ref, w_ref, a_ref, h_ref):
        _, xh = _rms_stats(x_ref[...])
        h = (xh * g_ref[...]).astype(BF16)
        h_ref[...] = h
        a_ref[...] = _dot_nt(w_ref[...], h).astype(BF16)

    return pl.pallas_call(
        body, name=name, grid=(T // tm,),
        in_specs=[pl.BlockSpec((tm, D), lambda i: (i, 0)), pl.BlockSpec((1, D), lambda i: (0, 0)),
                  _resident((N, D))],
        out_specs=[pl.BlockSpec((N, tm), lambda i: (0, i)), pl.BlockSpec((tm, D), lambda i: (i, 0))],
        out_shape=[SDS((N, T), BF16), SDS((T, D), BF16)],
        compiler_params=_params("parallel"),
    )(x, g, wt)


def _matmul_tokens(at, b, *, name):
    K, T = at.shape
    N = b.shape[1]
    tt = _tile(T, TT_REDUCE, LANES)

    def body(a_ref, b_ref, o_ref):
        @pl.when(pl.program_id(0) == 0)
        def _():
            o_ref[...] = jnp.zeros_like(o_ref)

        o_ref[...] += _dot(a_ref[...], b_ref[...])

    return pl.pallas_call(
        body, name=name, grid=(T // tt,),
        in_specs=[pl.BlockSpec((K, tt), lambda t: (0, t)), pl.BlockSpec((tt, N), lambda t: (t, 0))],
        out_specs=pl.BlockSpec((K, N), lambda t: (0, 0)),
        out_shape=SDS((K, N), F32),
        compiler_params=_params("arbitrary"),
    )(at, b)


def _matmul_tn_rows(dat, wt, *, name):
    N, T = dat.shape
    D = wt.shape[1]
    tm = _tile(T, TM_FEATURE_MAJOR, LANES)

    def body(da_ref, w_ref, o_ref):
        o_ref[...] = _dot_tn(da_ref[...], w_ref[...])

    return pl.pallas_call(
        body, name=name, grid=(T // tm,),
        in_specs=[pl.BlockSpec((N, tm), lambda i: (0, i)), _resident((N, D))],
        out_specs=pl.BlockSpec((tm, D), lambda i: (i, 0)),
        out_shape=SDS((T, D), F32),
        compiler_params=_params("parallel"),
    )(dat, wt)


def _ff_chunk(F):
    return F if F <= 1408 else F // 2


def _swiglu_out(a, w2, x, g_post, *, name):
    T, F2 = a.shape
    F = F2 // 2
    D = x.shape[1]
    tm = _tile(T, TM_ROW, SUBLANES_BF16)
    fc = _ff_chunk(F)

    def body(a_ref, w_ref, x_ref, g_ref, xn_ref, o_ref):
        acc = None
        for c0 in range(0, F, fc):
            gt = a_ref[:, c0:c0 + fc].astype(F32)
            s = (gt * _sigmoid(gt)).astype(BF16) * a_ref[:, F + c0:F + c0 + fc]
            part = _dot(s, w_ref[c0:c0 + fc, :])
            acc = part if acc is None else acc + part
        o_ref[...] = acc.astype(BF16)
        _, oh = _rms_stats(acc)
        xn_ref[...] = x_ref[...] + 0.5 * (oh * g_ref[...])

    return pl.pallas_call(
        body, name=name, grid=(T // tm,),
        in_specs=[pl.BlockSpec((tm, F2), lambda i: (i, 0)),
                  _resident((F, D)),
                  pl.BlockSpec((tm, D), lambda i: (i, 0)),
                  pl.BlockSpec((1, D), lambda i: (0, 0))],
        out_specs=[pl.BlockSpec((tm, D), lambda i: (i, 0)), pl.BlockSpec((tm, D), lambda i: (i, 0))],
        out_shape=[SDS((T, D), F32), SDS((T, D), BF16)],
        compiler_params=_params("parallel"),
    )(a, w2, x, g_post)


def _ffn_bwd_hidden(dy, o, g_post, a, w2, *, name):
    T, F2 = a.shape
    F = F2 // 2
    D = dy.shape[1]
    tm = _tile(T, TM_FFN_BWD, SUBLANES_BF16)
    fc = _tile(F, 256, LANES)

    def body(dy_ref, o_ref, g_ref, a_ref, w_ref, da_ref, do_ref, dg_ref):
        @pl.when(pl.program_id(0) == 0)
        def _():
            dg_ref[...] = jnp.zeros_like(dg_ref)

        do, dg = _rms_bwd(o_ref[...].astype(F32), g_ref[...], 0.5 * dy_ref[...])
        dg_ref[...] += dg
        dob = do.astype(BF16)
        do_ref[...] = dob
        for c0 in range(0, F, fc):
            ds = _dot_nt(dob, w_ref[c0:c0 + fc, :]).astype(BF16)
            gt = a_ref[:, c0:c0 + fc].astype(F32)
            ub = a_ref[:, F + c0:F + c0 + fc]
            sg = _sigmoid(gt)
            sl = gt * sg
            dsl = (sg + sl * (1.0 - sg)).astype(BF16)
            da_ref[:, c0:c0 + fc] = ds * ub * dsl
            da_ref[:, F + c0:F + c0 + fc] = ds * sl.astype(BF16)

    row = lambda w: pl.BlockSpec((tm, w), lambda i: (i, 0))
    return pl.pallas_call(
        body, name=name, grid=(T // tm,),
        in_specs=[row(D), row(D), pl.BlockSpec((1, D), lambda i: (0, 0)), row(F2),
                  _resident((F, D))],
        out_specs=[row(F2), row(D), pl.BlockSpec((1, D), lambda i: (0, 0))],
        out_shape=[SDS((T, F2), BF16), SDS((T, D), BF16), SDS((1, D), F32)],
        compiler_params=_params("arbitrary"),
    )(dy, o, g_post, a, w2)


def _dw_call(body, name, grid, in_specs, args, block, pack, row_block, sem):
    if pack is None:
        out_spec = pl.BlockSpec(block, lambda *_: (0, 0, 0), pipeline_mode=pl.Buffered(1))
        return pl.pallas_call(body, name=name, grid=grid, in_specs=in_specs, out_specs=out_spec,
                              out_shape=SDS(block, F32), compiler_params=_params(*sem))(*args)
    assert pack.shape[0] == block[0] and pack.shape[2] == block[2]
    out_spec = pl.BlockSpec(block, lambda *_: (0, row_block, 0), pipeline_mode=pl.Buffered(1))
    return pl.pallas_call(body, name=name, grid=grid, in_specs=in_specs + [ANY], out_specs=out_spec,
                          out_shape=SDS(pack.shape, F32), input_output_aliases={len(args): 0},
                          compiler_params=_params(*sem))(*args, pack)


def _dw_cols(a, b, n_chunks, pack, row_block, *, name, ride=()):
    T, K = a.shape
    tn = b.shape[1] // n_chunks
    tt = _tile(T, TT_REDUCE, SUBLANES_BF16)
    per = 2 if n_chunks % 2 == 0 else 1
    grid = (n_chunks // per, T // tt)
    kind, riders = ride if ride else (None, ())
    nr = len(riders)

    def body(a_ref, b_ref, *rest):
        o_ref = rest[1 + nr] if nr else rest[-1]
        j, t = pl.program_id(0), pl.program_id(1)
        if nr:
            rides = RIDES[kind][0](rest[1:1 + nr], rest[2 + nr:2 + 2 * nr], rest[-2], rest[-1])

            @pl.when((j == 0) & (t == 0))
            def _():
                for cp in rides:
                    cp.start()

        @pl.when(t == 0)
        def _():
            o_ref[...] = jnp.zeros_like(o_ref)

        for p in range(per):
            o_ref[p] += _dot_tn(a_ref[...], b_ref[:, p * tn:(p + 1) * tn])

        if nr:
            @pl.when((j == grid[0] - 1) & (t == grid[1] - 1))
            def _():
                for cp in rides:
                    cp.wait()

    in_specs = [pl.BlockSpec((tt, K), lambda j, t: (t, 0)), pl.BlockSpec((tt, per * tn), lambda j, t: (t, j))]
    sem = ("arbitrary", "arbitrary") if nr else ("parallel", "arbitrary")
    if pack is None:
        assert not nr
        return pl.pallas_call(body, name=name, grid=grid, in_specs=in_specs,
                              out_specs=pl.BlockSpec((per, K, tn), lambda j, t: (j, 0, 0)),
                              out_shape=SDS((n_chunks, K, tn), F32), compiler_params=_params(*sem))(a, b)
    assert pack.shape[0] == n_chunks and pack.shape[2] == tn
    out = pl.pallas_call(
        body, name=name, grid=grid, in_specs=in_specs + [ANY] * (1 + nr),
        out_specs=[pl.BlockSpec((per, K, tn), lambda j, t: (j, row_block, 0))] + [ANY] * nr,
        out_shape=[SDS(pack.shape, F32)] + (RIDES[kind][1](riders) if nr else []),
        input_output_aliases={2: 0},
        scratch_shapes=[pltpu.SemaphoreType.DMA((RIDES[kind][2] * nr,))] * 2 if nr else [],
        compiler_params=_params(*sem))(a, b, pack, *riders)
    return out if nr else out[0]


def _dw_rows(a, b, pack, row_block, *, name, a_feature_major=False, swiglu=False):
    K, T = a.shape if a_feature_major else a.shape[::-1]
    K = K // 2 if swiglu else K
    N = b.shape[1]
    r = K // N_CHIPS
    cw = r if r % LANES == 0 else 2 * r
    assert cw % LANES == 0 and K % cw == 0 and r % 8 == 0
    tt = _tile(T, TT_REDUCE // 2 if swiglu else TT_REDUCE, LANES)

    def body(a_ref, b_ref, *rest):
        o_ref = rest[-1]

        @pl.when(pl.program_id(0) == 0)
        def _():
            o_ref[...] = jnp.zeros_like(o_ref)

        for c in range(K // cw):
            if a_feature_major:
                part = _dot(a_ref[c * cw:(c + 1) * cw, :], b_ref[...])
            elif swiglu:
                gt = a_ref[:, c * cw:(c + 1) * cw].astype(F32)
                part = _dot_tn((gt * _sigmoid(gt)).astype(BF16) * a_ref[:, K + c * cw:K + (c + 1) * cw], b_ref[...])
            else:
                part = _dot_tn(a_ref[:, c * cw:(c + 1) * cw], b_ref[...])
            for p in range(cw // r):
                o_ref[c * (cw // r) + p] += part[p * r:(p + 1) * r]

    a_spec = (pl.BlockSpec((K, tt), lambda t: (0, t)) if a_feature_major
              else pl.BlockSpec((tt, a.shape[1]), lambda t: (t, 0)))
    return _dw_call(body, name, (T // tt,), [a_spec, pl.BlockSpec((tt, N), lambda t: (t, 0))],
                    [a, b], (N_CHIPS, r, N), pack, row_block, ("arbitrary",))


def _matmul_nt_norm_bwd(da, w, w_block, x, g, dy, init, *, name, ride=None):
    T, N = da.shape
    D = x.shape[1]
    nj, _, tn = w.shape
    tm = _tile(T, TM_ROW, SUBLANES_BF16)
    ni = T // tm
    has_init = init is not None
    kind, riders = ride if ride else (None, ())
    nr = len(riders)
    n_in = 5 + has_init

    def body(*refs):
        da_ref, w_ref, x_ref, g_ref, dy_ref = refs[:5]
        dx_ref, dg_ref = refs[n_in + nr:n_in + nr + 2]
        i = pl.program_id(0)
        if nr:
            rides = RIDES[kind][0](refs[n_in:n_in + nr], refs[n_in + nr + 2:n_in + 2 * nr + 2], refs[-2], refs[-1])

            @pl.when(i == 0)
            def _():
                for cp in rides:
                    cp.start()

        @pl.when(i == 0)
        def _():
            dg_ref[...] = jnp.zeros_like(dg_ref)

        dh = refs[5][...] if has_init else None
        for j in range(nj):
            part = _dot_nt(da_ref[:, j * tn:(j + 1) * tn], w_ref[j])
            dh = part if dh is None else dh + part
        dx, dg = _rms_bwd(x_ref[...], g_ref[...], dh)
        dx_ref[...] = dy_ref[...] + dx
        dg_ref[...] += dg

        if nr:
            @pl.when(i == ni - 1)
            def _():
                for cp in rides:
                    cp.wait()

    row = pl.BlockSpec((tm, D), lambda i: (i, 0))
    vec = pl.BlockSpec((1, D), lambda i: (0, 0))
    in_specs = [pl.BlockSpec((tm, N), lambda i: (i, 0)), _resident((nj, D, tn), (0, w_block, 0)), row, vec, row]
    args = [da, w, x, g, dy]
    if has_init:
        in_specs.append(row)
        args.append(init)
    return pl.pallas_call(
        body, name=name, grid=(ni,), in_specs=in_specs + [ANY] * nr,
        out_specs=[row, vec] + [ANY] * nr,
        out_shape=[SDS((T, D), F32), SDS((1, D), F32)] + (RIDES[kind][1](riders) if nr else []),
        scratch_shapes=[pltpu.SemaphoreType.DMA((RIDES[kind][2] * nr,))] * 2 if nr else [],
        compiler_params=_params("arbitrary"),
    )(*args, *riders)


GROUP_LANES = Q_PER_KV * ATTN_BLOCK


def _attn_mask_t(first):
    kj = lax.broadcasted_iota(jnp.int32, (2 * ATTN_BLOCK, ATTN_BLOCK), 0)
    qi = lax.broadcasted_iota(jnp.int32, (2 * ATTN_BLOCK, ATTN_BLOCK), 1)
    rel = qi + ATTN_BLOCK - kj
    band = (rel >= 0) & (rel < ATTN_BLOCK)
    if first is False:
        return band
    return band & ((kj >= ATTN_BLOCK) | jnp.logical_not(first))


def _attn_probs_t(st, valid, sink):
    s = jnp.where(valid, st, MASK_VALUE)
    m = jnp.maximum(jnp.max(s, axis=0, keepdims=True), sink)
    p = jnp.exp(s - m)
    es = jnp.exp(sink - m)
    inv = 1.0 / (jnp.sum(p, axis=0, keepdims=True) + es)
    return p * inv, es * inv


def _attn_specs(tq, tile_of):
    nb = tq // ATTN_BLOCK
    krow, vrow = ATTN_WIDTH // KV_WIDTH, ATTN_WIDTH // KV_WIDTH + 1
    halo = lambda r: pl.BlockSpec((KV_WIDTH, ATTN_BLOCK), lambda t: (r, jnp.maximum(tile_of(t) * nb - 1, 0)))
    return [pl.BlockSpec((ATTN_WIDTH, tq), lambda t: (0, tile_of(t))),
            pl.BlockSpec((KV_WIDTH, tq), lambda t: (krow, tile_of(t))),
            pl.BlockSpec((KV_WIDTH, tq), lambda t: (vrow, tile_of(t))),
            halo(krow), halo(vrow)]


def _head_rows(g, r):
    h = g * Q_PER_KV + r
    return h, slice(h * HEAD_DIM, (h + 1) * HEAD_DIM)


def _group_stack(ref, g, cols):
    return jnp.concatenate([ref[_head_rows(g, r)[1], cols] for r in range(Q_PER_KV)], axis=1)


def _attn_fwd(zt, sinks, *, name):
    T = zt.shape[1]
    tq = _tile(T, TQ_ATTN, ATTN_BLOCK)
    nb = tq // ATTN_BLOCK

    def body(q_ref, k_ref, v_ref, kh_ref, vh_ref, s_ref, o_ref, kf, vf, pt):
        kf[:, 0:ATTN_BLOCK] = kh_ref[...]
        kf[:, ATTN_BLOCK:] = k_ref[...]
        vf[:, 0:ATTN_BLOCK] = vh_ref[...]
        vf[:, ATTN_BLOCK:] = v_ref[...]
        for b in range(nb):
            cols = slice(b * ATTN_BLOCK, (b + 1) * ATTN_BLOCK)
            win = slice(b * ATTN_BLOCK, (b + 2) * ATTN_BLOCK)
            valid = _attn_mask_t((pl.program_id(0) == 0) if b == 0 else False)
            for g in range(N_KV_HEADS):
                gr = slice(g * HEAD_DIM, (g + 1) * HEAD_DIM)
                st = _dot_tn(kf[gr, win], _group_stack(q_ref, g, cols) * ATTN_SCALE)
                for r in range(Q_PER_KV):
                    h, _ = _head_rows(g, r)
                    sl = slice(r * ATTN_BLOCK, (r + 1) * ATTN_BLOCK)
                    probs, _ = _attn_probs_t(st[:, sl], valid, s_ref[h])
                    pt[:, sl] = probs.astype(BF16)
                ot = _dot(vf[gr, win], pt[...])
                for r in range(Q_PER_KV):
                    o_ref[_head_rows(g, r)[1], cols] = ot[:, r * ATTN_BLOCK:(r + 1) * ATTN_BLOCK].astype(BF16)

    return pl.pallas_call(
        body, name=name, grid=(T // tq,),
        in_specs=_attn_specs(tq, lambda t: t) + [pl.BlockSpec(memory_space=pltpu.SMEM)],
        out_specs=pl.BlockSpec((ATTN_WIDTH, tq), lambda t: (0, t)),
        out_shape=SDS((ATTN_WIDTH, T), BF16),
        scratch_shapes=[pltpu.VMEM((KV_WIDTH, tq + ATTN_BLOCK), BF16)] * 2
        + [pltpu.VMEM((2 * ATTN_BLOCK, GROUP_LANES), BF16)],
        compiler_params=_params("parallel"),
    )(zt, zt, zt, zt, zt, sinks)


def _attn_bwd(zt, sinks, dot_, *, name):
    T = zt.shape[1]
    tq = _tile(T, TQ_ATTN, ATTN_BLOCK)
    nb = tq // ATTN_BLOCK
    nt = T // tq
    tile_of = lambda t: nt - 1 - t

    def body(q_ref, k_ref, v_ref, kh_ref, vh_ref, do_ref, s_ref, dz_ref, dsink_ref, kf, vf, dkf, dvf, carry, pt, dst):
        t = pl.program_id(0)

        @pl.when(t == 0)
        def _():
            carry[...] = jnp.zeros_like(carry)
            dsink_ref[...] = jnp.zeros_like(dsink_ref)

        kf[:, 0:ATTN_BLOCK] = kh_ref[...]
        kf[:, ATTN_BLOCK:] = k_ref[...]
        vf[:, 0:ATTN_BLOCK] = vh_ref[...]
        vf[:, ATTN_BLOCK:] = v_ref[...]
        dkf[...] = jnp.zeros_like(dkf)
        dvf[...] = jnp.zeros_like(dvf)
        dkf[:, tq:] = carry[0:KV_WIDTH, :]
        dvf[:, tq:] = carry[KV_WIDTH:, :]
        lane = lax.broadcasted_iota(jnp.int32, (1, LANES), 1)
        dsink = jnp.zeros((1, LANES), F32)
        for b in range(nb):
            cols = slice(b * ATTN_BLOCK, (b + 1) * ATTN_BLOCK)
            win = slice(b * ATTN_BLOCK, (b + 2) * ATTN_BLOCK)
            valid = _attn_mask_t((t == nt - 1) if b == 0 else False)
            for g in range(N_KV_HEADS):
                gr = slice(g * HEAD_DIM, (g + 1) * HEAD_DIM)
                kt2, vt2 = kf[gr, win], vf[gr, win]
                qst = _group_stack(q_ref, g, cols) * ATTN_SCALE
                dost = _group_stack(do_ref, g, cols)
                st = _dot_tn(kt2, qst)
                dpt = _dot_tn(vt2, dost)
                for r in range(Q_PER_KV):
                    h, _ = _head_rows(g, r)
                    sl = slice(r * ATTN_BLOCK, (r + 1) * ATTN_BLOCK)
                    probs, psink = _attn_probs_t(st[:, sl], valid, s_ref[h])
                    dp = dpt[:, sl]
                    delta = jnp.sum(probs * dp, axis=0, keepdims=True)
                    pt[:, sl] = probs.astype(BF16)
                    dst[:, sl] = (probs * (dp - delta)).astype(BF16)
                    dsink = dsink + jnp.where(lane == h, -jnp.sum(psink * delta), 0.0)
                dqt = _dot(kt2, dst[...]) * ATTN_SCALE
                for r in range(Q_PER_KV):
                    dz_ref[_head_rows(g, r)[1], cols] = dqt[:, r * ATTN_BLOCK:(r + 1) * ATTN_BLOCK].astype(BF16)
                dkf[gr, win] += _dot_nt(qst, dst[...])
                dvf[gr, win] += _dot_nt(dost, pt[...])
        dz_ref[ATTN_WIDTH:ATTN_WIDTH + KV_WIDTH, :] = dkf[:, ATTN_BLOCK:].astype(BF16)
        dz_ref[ATTN_WIDTH + KV_WIDTH:, :] = dvf[:, ATTN_BLOCK:].astype(BF16)
        carry[0:KV_WIDTH, :] = dkf[:, 0:ATTN_BLOCK]
        carry[KV_WIDTH:, :] = dvf[:, 0:ATTN_BLOCK]
        dsink_ref[...] += dsink

    return pl.pallas_call(
        body, name=name, grid=(nt,),
        in_specs=_attn_specs(tq, tile_of) + [pl.BlockSpec((ATTN_WIDTH, tq), lambda t: (0, tile_of(t))),
                                             pl.BlockSpec(memory_space=pltpu.SMEM)],
        out_specs=[pl.BlockSpec((QKV_WIDTH, tq), lambda t: (0, tile_of(t))),
                   pl.BlockSpec((8, LANES), lambda t: (0, 0))],
        out_shape=[SDS((QKV_WIDTH, T), BF16), SDS((8, LANES), F32)],
        scratch_shapes=[pltpu.VMEM((KV_WIDTH, tq + ATTN_BLOCK), BF16)] * 2
        + [pltpu.VMEM((KV_WIDTH, tq + ATTN_BLOCK), F32)] * 2 + [pltpu.VMEM((2 * KV_WIDTH, ATTN_BLOCK), F32)]
        + [pltpu.VMEM((2 * ATTN_BLOCK, GROUP_LANES), BF16)] * 2,
        compiler_params=_params("arbitrary"),
    )(zt, zt, zt, zt, zt, dot_, sinks)


def _layer_norm_stats(v):
    mu = jnp.mean(v, axis=-1, keepdims=True)
    xc = v - mu
    rstd = lax.rsqrt(jnp.mean(xc * xc, axis=-1, keepdims=True) + LN_EPS)
    return rstd, xc * rstd


def _sgu_fwd(zmain, ln_g, ln_b, wm, bias, *, name):
    T = zmain.shape[0]
    ts = _tile(T, TS_SGU, SGU_CHUNK)

    def body(u_ref, v_ref, g_ref, b_ref, w_ref, bias_ref, y_ref):
        u = _gelu(u_ref[...].astype(F32))
        _, vh = _layer_norm_stats(_gelu(v_ref[...].astype(F32)))
        vn = (vh * g_ref[...] + b_ref[...]).astype(BF16)
        for ch in range(ts // SGU_CHUNK):
            rows = slice(ch * SGU_CHUNK, (ch + 1) * SGU_CHUNK)
            for g in range(SGU_GROUPS):
                cols = slice(g * 128, (g + 1) * 128)
                s = _dot(w_ref[g], vn[rows, cols]) + bias_ref[g]
                y_ref[rows, cols] = (u[rows, cols] * s).astype(BF16)

    full = _resident
    return pl.pallas_call(
        body, name=name, grid=(T // ts,),
        in_specs=[pl.BlockSpec((ts, SGU_WIDTH), lambda i: (i, 0)), pl.BlockSpec((ts, SGU_WIDTH), lambda i: (i, 1)),
                  full((1, SGU_WIDTH)), full((1, SGU_WIDTH)), full(wm.shape), full(bias.shape)],
        out_specs=pl.BlockSpec((ts, SGU_WIDTH), lambda i: (i, 0)),
        out_shape=SDS((T, SGU_WIDTH), BF16),
        compiler_params=_params("parallel"),
    )(zmain, zmain, ln_g, ln_b, wm, bias)


def _sgu_bwd(zmain, dzmain, dy, ln_g, ln_b, wm, wmt, bias, *, name):
    T = zmain.shape[0]
    ts = _tile(T, TS_SGU, SGU_CHUNK)

    def body(u_ref, v_ref, dy_ref, g_ref, b_ref, w_ref, wt_ref, bias_ref, _, dz_ref, dw_ref, db_ref, dlg_ref, dlb_ref,
             dvn):
        @pl.when(pl.program_id(0) == 0)
        def _():
            dw_ref[...] = jnp.zeros_like(dw_ref)
            db_ref[...] = jnp.zeros_like(db_ref)
            dlg_ref[...] = jnp.zeros_like(dlg_ref)
            dlb_ref[...] = jnp.zeros_like(dlb_ref)

        us = u_ref[...].astype(F32)
        vs = v_ref[...].astype(F32)
        u = _gelu(us)
        rstd, vh = _layer_norm_stats(_gelu(vs))
        vn = (vh * g_ref[...] + b_ref[...]).astype(BF16)
        causal = (lax.broadcasted_iota(jnp.int32, (SGU_CHUNK, SGU_CHUNK), 0)
                  >= lax.broadcasted_iota(jnp.int32, (SGU_CHUNK, SGU_CHUNK), 1))
        lane = lax.broadcasted_iota(jnp.int32, (SGU_CHUNK, LANES), 1)
        db = jnp.zeros((SGU_CHUNK, LANES), F32)
        for ch in range(ts // SGU_CHUNK):
            rows = slice(ch * SGU_CHUNK, (ch + 1) * SGU_CHUNK)
            for g in range(SGU_GROUPS):
                cols = slice(g * 128, (g + 1) * 128)
                vng = vn[rows, cols]
                s = _dot(w_ref[g], vng) + bias_ref[g]
                dyf = dy_ref[rows, cols].astype(F32)
                dz_ref[rows, cols] = (dyf * s * _gelu_grad(us[rows, cols])).astype(BF16)
                dsf = dyf * u[rows, cols]
                dsb = dsf.astype(BF16)
                dvn[rows, cols] = _dot(wt_ref[g], dsb)
                dw_ref[g] += jnp.where(causal, _dot_nt(dsb, vng), 0.0)
                db = db + jnp.where(lane == g, jnp.sum(dsf, axis=1, keepdims=True), 0.0)
        db_ref[...] += db
        dvnf = dvn[...]
        dlg_ref[...] += jnp.sum(dvnf * vh, axis=0, keepdims=True)
        dlb_ref[...] += jnp.sum(dvnf, axis=0, keepdims=True)
        dvh = dvnf * g_ref[...]
        dv = rstd * (dvh - jnp.mean(dvh, axis=-1, keepdims=True) - vh * jnp.mean(dvh * vh, axis=-1, keepdims=True))
        dz_ref[:, SGU_WIDTH:] = (dv * _gelu_grad(vs)).astype(BF16)

    full = _resident
    vec = full((1, SGU_WIDTH))
    acc = lambda shape: pl.BlockSpec(shape, lambda i: (0,) * len(shape))
    return pl.pallas_call(
        body, name=name, grid=(T // ts,),
        in_specs=[pl.BlockSpec((ts, SGU_WIDTH), lambda i: (i, 0)), pl.BlockSpec((ts, SGU_WIDTH), lambda i: (i, 1)),
                  pl.BlockSpec((ts, SGU_WIDTH), lambda i: (i, 0)), vec, vec, full(wm.shape), full(wm.shape),
                  full(bias.shape), pl.BlockSpec(memory_space=pl.ANY)],
        out_specs=[pl.BlockSpec((ts, 2 * SGU_WIDTH), lambda i: (i, 0)), acc(wm.shape),
                   acc((SGU_CHUNK, LANES)), acc((1, SGU_WIDTH)), acc((1, SGU_WIDTH))],
        out_shape=[SDS(dzmain.shape, BF16), SDS(wm.shape, F32), SDS((SGU_CHUNK, LANES), F32),
                   SDS((1, SGU_WIDTH), F32), SDS((1, SGU_WIDTH), F32)],
        scratch_shapes=[pltpu.VMEM((ts, SGU_WIDTH), F32)],
        input_output_aliases={8: 0},
        compiler_params=_params("arbitrary"),
    )(zmain, zmain, dy, ln_g, ln_b, wm, wmt, bias, dzmain)


def _merge_fwd(y_attn_t, y_sgu, zmain, w_a, w_s, w_o, x, g_post, *, name):
    T, D = x.shape
    tm = _tile(T, TM_ROW, LANES)

    def body(ya_ref, ys_ref, ga_ref, gb_ref, wa_ref, ws_ref, wo_ref, x_ref, g_ref, xn_ref, pa_ref, ps_ref, o_ref):
        pa = _dot_tn(ya_ref[...], wa_ref[...])
        ps = _dot(ys_ref[...], ws_ref[...])
        pa_ref[...] = pa.astype(BF16)
        ps_ref[...] = ps.astype(BF16)
        merged = _sigmoid(ga_ref[...].astype(F32)) * pa + _sigmoid(gb_ref[...].astype(F32)) * ps
        out = _dot(merged.astype(BF16), wo_ref[...])
        o_ref[...] = out.astype(BF16)
        _, oh = _rms_stats(out)
        xn_ref[...] = x_ref[...] + oh * g_ref[...]

    row = lambda col: pl.BlockSpec((tm, D), lambda i: (i, col))
    wfull = _resident((D, D))
    return pl.pallas_call(
        body, name=name, grid=(T // tm,),
        in_specs=[pl.BlockSpec((D, tm), lambda i: (0, i)), row(0), row(2), row(3), wfull, wfull, wfull, row(0),
                  pl.BlockSpec((1, D), lambda i: (0, 0))],
        out_specs=[row(0)] * 4,
        out_shape=[SDS((T, D), F32), SDS((T, D), BF16), SDS((T, D), BF16), SDS((T, D), BF16)],
        compiler_params=_params("parallel"),
    )(y_attn_t, y_sgu, zmain, zmain, w_a, w_s, w_o, x, g_post)


def _merge_bwd(dy, out, g_post, pa, ps, zmain, w_a, w_s, w_o, *, name):
    T, D = dy.shape
    tm = _tile(T, TM_ROW, LANES)

    def body(dy_ref, o_ref, g_ref, pa_ref, ps_ref, ga_ref, gb_ref, wa_ref, ws_ref, wo_ref,
             dz_ref, dout_ref, mg_ref, dpa_ref, dps_ref, dya_ref, dys_ref, dg_ref):
        @pl.when(pl.program_id(0) == 0)
        def _():
            dg_ref[...] = jnp.zeros_like(dg_ref)

        dout, dg = _rms_bwd(o_ref[...].astype(F32), g_ref[...], dy_ref[...])
        dg_ref[...] += dg
        doutb = dout.astype(BF16)
        dout_ref[...] = doutb
        dm = _dot_nt(doutb, wo_ref[...]).astype(BF16)
        pa, ps = pa_ref[...], ps_ref[...]
        sa = _sigmoid(ga_ref[...].astype(F32))
        sb = _sigmoid(gb_ref[...].astype(F32))
        one_minus_sa, one_minus_sb = (1.0 - sa).astype(BF16), (1.0 - sb).astype(BF16)
        sa, sb = sa.astype(BF16), sb.astype(BF16)
        mg_ref[...] = sa * pa + sb * ps
        dpa = dm * sa
        dps = dm * sb
        dpa_ref[...] = dpa
        dps_ref[...] = dps
        dz_ref[:, 0:D] = dpa * pa * one_minus_sa
        dz_ref[:, D:] = dps * ps * one_minus_sb
        dya_ref[...] = _dot_nt(wa_ref[...], dpa).astype(BF16)
        dys_ref[...] = _dot_nt(dps, ws_ref[...]).astype(BF16)

    row = lambda col: pl.BlockSpec((tm, D), lambda i: (i, col))
    wfull = _resident((D, D))
    vec = pl.BlockSpec((1, D), lambda i: (0, 0))
    act = SDS((T, D), BF16)
    return pl.pallas_call(
        body, name=name, grid=(T // tm,),
        in_specs=[row(0), row(0), vec, row(0), row(0), row(2), row(3), wfull, wfull, wfull],
        out_specs=[pl.BlockSpec((tm, 2 * D), lambda i: (i, 1))] + [row(0)] * 4
        + [pl.BlockSpec((D, tm), lambda i: (0, i)), row(0), vec],
        out_shape=[SDS(zmain.shape, BF16)] + [act] * 4 + [SDS((D, T), BF16), act, SDS((1, D), F32)],
        compiler_params=_params("arbitrary"),
    )(dy, out, g_post, pa, ps, zmain, zmain, w_a, w_s, w_o)


def _loss_head(y, target, *, name):
    T, D = y.shape
    tm = _tile(T, TM_ROW, 8)

    def body(y_ref, t_ref, dy_ref, l_ref):
        @pl.when(pl.program_id(0) == 0)
        def _():
            l_ref[...] = jnp.zeros_like(l_ref)

        e = y_ref[...] - t_ref[...]
        dy_ref[...] = e * (1.0 / D)
        l_ref[...] += jnp.sum(jnp.mean(e * e, axis=-1, keepdims=True))

    row = pl.BlockSpec((tm, D), lambda i: (i, 0))
    return pl.pallas_call(
        body, name=name, grid=(T // tm,), in_specs=[row, row],
        out_specs=[row, pl.BlockSpec((8, LANES), lambda i: (0, 0))],
        out_shape=[SDS((T, D), F32), SDS((8, LANES), F32)],
        compiler_params=_params("arbitrary"),
    )(y, target)


def _adamw(w, g, m, v, *, name):
    shape = w.shape
    cols = shape[-1]
    rows = w.size // cols
    w2, g2, m2, v2 = (t.reshape(rows, cols) for t in (w, g, m, v))
    tr = _tile(rows, max(8, (256 * 1024) // cols // 8 * 8), 8)

    def body(w_ref, g_ref, m_ref, v_ref, d_ref, nm_ref, nv_ref):
        gg = g_ref[...]
        nm = ADAM_B1 * m_ref[...] + (1.0 - ADAM_B1) * gg
        nv = ADAM_B2 * v_ref[...] + (1.0 - ADAM_B2) * (gg * gg)
        m_hat = nm / (1.0 - ADAM_B1 ** ADAM_STEP)
        v_hat = nv / (1.0 - ADAM_B2 ** ADAM_STEP)
        d_ref[...] = -ADAM_LR * (m_hat / (jnp.sqrt(v_hat) + ADAM_EPS) + ADAM_WD * w_ref[...])
        nm_ref[...] = nm
        nv_ref[...] = nv

    blk = pl.BlockSpec((tr, cols), lambda i: (i, 0))
    outs = pl.pallas_call(
        body, name=name, grid=(rows // tr,), in_specs=[blk] * 4, out_specs=[blk] * 3,
        out_shape=[SDS((rows, cols), F32)] * 3, compiler_params=_params("parallel"),
    )(w2, g2, m2, v2)
    return tuple(o.reshape(shape) for o in outs)


def _sum_terms(terms, n_rows, n_lead, dtypes, *, name, out_lead=None, out_n_lead=None):
    cols = terms[0][0].shape[-1]
    tr = _tile(n_rows, 704 if len(terms) <= 4 else 256, SUBLANES_BF16)
    nblk = n_rows // tr
    n_out = len(dtypes)

    def body(*refs):
        acc = refs[0][...].astype(F32)
        for r in refs[1:-n_out]:
            acc = acc + r[...].astype(F32)
        for o_ref in refs[-n_out:]:
            o_ref[...] = acc.astype(o_ref.dtype)

    def spec(lead, first):
        return pl.BlockSpec((1, tr, cols), lambda a, i: (lead(a), first(a) * nblk + i, 0))

    out = pl.BlockSpec((1, tr, cols), lambda a, i: (a if out_lead is None else out_lead(a), i, 0))
    return pl.pallas_call(
        body, name=name, grid=(n_lead, nblk), in_specs=[spec(lead, first) for _, lead, first in terms],
        out_specs=[out] * n_out, out_shape=[SDS((out_n_lead or n_lead, n_rows, cols), d) for d in dtypes],
        compiler_params=_params("arbitrary", "arbitrary"),
    )(*[a for a, _, _ in terms])


def _position():
    x, y, c = (lax.axis_index(a) for a in AXES)
    chips = [(1 - x, y), (x, 1 - y), (1 - x, 1 - y)]
    return x, y, c, chips


ANY = pl.BlockSpec(memory_space=pl.ANY)


def _remote(src, dst, send_sems, recv_sems, k, to):
    return pltpu.make_async_remote_copy(src_ref=src, dst_ref=dst, send_sem=send_sems.at[k], recv_sem=recv_sems.at[k],
                                        device_id=to, device_id_type=MESH)


def _comm_call(body, arrays, out_shapes, n_sems, *, name):
    n = len(arrays)

    def wrapped(*refs):
        body(refs[:n], refs[n:n + len(out_shapes)], refs[-2], refs[-1])

    return pl.pallas_call(
        wrapped, name=name, in_specs=[ANY] * n, out_specs=[ANY] * len(out_shapes), out_shape=out_shapes,
        scratch_shapes=[pltpu.SemaphoreType.DMA((n_sems,)), pltpu.SemaphoreType.DMA((n_sems,))],
    )(*arrays)


def _gather_shards(packs, *, name):
    def body(p_refs, o_refs, send_sems, recv_sems):
        for stage in _gather_stages(p_refs, o_refs, send_sems, recv_sems):
            stage()

    for p in packs:
        assert p.shape[0] % (4 * SUBLANES_BF16) == 0
    return _comm_call(body, packs, _gathered_shapes(packs), GATHER_SEMS * len(packs), name=name)


GATHER_SEMS = 8


def _gathered_shapes(packs):
    return [SDS((N_CHIPS,) + p.shape, p.dtype) for p in packs]


def _gather_stages(p_refs, o_refs, send_sems, recv_sems):
    NS = GATHER_SEMS
    x, y, c, _ = _position()
    me, sib = (x, y, c), (x, y, 1 - c)
    xn, yn = (1 - x, y, c), (x, 1 - y, c)
    s_me, s_xn, s_yn, s_dg = 2 * x + y, 2 * (1 - x) + y, 2 * x + 1 - y, 2 * (1 - x) + 1 - y

    def copy(a, k):
        p_ref, o_ref = p_refs[a], o_refs[a]
        rh = p_ref.shape[0] // 2
        rq = rh // 2
        half, q0, q1 = pl.ds(c * rh, rh), pl.ds(c * rh, rq), pl.ds(c * rh + rq, rq)
        src, dst, to = [(p_ref.at[half], o_ref.at[s_me, half], xn), (p_ref.at[half], o_ref.at[s_me, half], yn),
                        (o_ref.at[s_xn, q0],) * 2 + (yn,), (o_ref.at[s_yn, q1],) * 2 + (xn,),
                        (o_ref.at[s_xn, half],) * 2 + (sib,), (o_ref.at[s_yn, half],) * 2 + (sib,),
                        (o_ref.at[s_dg, q0],) * 2 + (sib,), (o_ref.at[s_dg, q1],) * 2 + (sib,)][k]
        return _remote(src, dst, send_sems, recv_sems, NS * a + k, to)

    def landed(a, k):
        o_ref = o_refs[a]
        rh = o_ref.shape[1] // 2
        rq = rh // 2
        o = (1 - c) * rh
        dst = [o_ref.at[s_xn, pl.ds(c * rh, rh)], o_ref.at[s_yn, pl.ds(c * rh, rh)],
               o_ref.at[s_dg, pl.ds(c * rh, rq)], o_ref.at[s_dg, pl.ds(c * rh + rq, rq)],
               o_ref.at[s_xn, pl.ds(o, rh)], o_ref.at[s_yn, pl.ds(o, rh)],
               o_ref.at[s_dg, pl.ds(o, rq)], o_ref.at[s_dg, pl.ds(o + rq, rq)]][k]
        _remote(dst, dst, send_sems, recv_sems, NS * a + k, me).wait_recv()

    n = len(p_refs)

    def stage_a():
        for a in range(n):
            for k in (0, 1):
                copy(a, k).start()

    def stage_b():
        for a in range(n):
            landed(a, 0)
            copy(a, 2).start()
            copy(a, 4).start()
            landed(a, 1)
            copy(a, 3).start()
            copy(a, 5).start()

    def stage_c():
        for a in range(n):
            landed(a, 2)
            copy(a, 6).start()
            landed(a, 3)
            copy(a, 7).start()
        for a in range(n):
            for k in (4, 5, 6, 7):
                landed(a, k)
        for a in range(n):
            for k in range(NS):
                copy(a, k).wait_send()

    return stage_a, stage_b, stage_c


def _sibling_exchange(gs, *, name):
    def body(g_refs, o_refs, send_sems, recv_sems):
        sent = _exchange_copies(g_refs, o_refs, send_sems, recv_sems)
        for cp in sent:
            cp.start()
        for cp in sent:
            cp.wait()

    return _comm_call(body, gs, _exchange_shapes(gs), len(gs), name=name)


def _exchange_shapes(gs):
    return [SDS((N_CHIPS, g.shape[1] // 2, g.shape[2]), g.dtype) for g in gs]


def _exchange_copies(g_refs, o_refs, send_sems, recv_sems):
    x, y, c, _ = _position()
    return [_remote(g_ref.at[:, pl.ds((1 - c) * o_ref.shape[1], o_ref.shape[1])], o_ref, send_sems, recv_sems, a,
                    (x, y, 1 - c)) for a, (g_ref, o_ref) in enumerate(zip(g_refs, o_refs))]


def _scatter_hop1(ps, *, name):
    def body(p_refs, o_refs, send_sems, recv_sems):
        sent = _hop1_copies(p_refs, o_refs, send_sems, recv_sems)
        for cp in sent:
            cp.start()
        for cp in sent:
            cp.wait()

    return _comm_call(body, ps, _hop1_shapes(ps), 4 * len(ps), name=name)


def _hop1_shapes(ps):
    return [SDS((4, p.shape[1] // 2, p.shape[2]), p.dtype) for p in ps]


def _hop1_copies(p_refs, o_refs, send_sems, recv_sems):
    x, y, c, _ = _position()
    xn, yn = (1 - x, y, c), (x, 1 - y, c)
    s_xn, s_yn, s_dg = 2 * (1 - x) + y, 2 * x + 1 - y, 2 * (1 - x) + 1 - y
    sent = []
    for a, (p_ref, o_ref) in enumerate(zip(p_refs, o_refs)):
        rq = o_ref.shape[1]
        first, second = pl.ds(0, rq), pl.ds(rq, rq)
        sent += [_remote(p_ref.at[s_xn, second], o_ref.at[0], send_sems, recv_sems, 4 * a, xn),
                 _remote(p_ref.at[s_dg, second], o_ref.at[1], send_sems, recv_sems, 4 * a + 1, xn),
                 _remote(p_ref.at[s_yn, first], o_ref.at[2], send_sems, recv_sems, 4 * a + 2, yn),
                 _remote(p_ref.at[s_dg, first], o_ref.at[3], send_sems, recv_sems, 4 * a + 3, yn)]
    return sent


RIDES = {"exchange": (_exchange_copies, _exchange_shapes, 1), "hop1": (_hop1_copies, _hop1_shapes, 4)}


def _scatter_hop2(fs, *, name):
    def body(f_refs, o_refs, send_sems, recv_sems):
        x, y, c, _ = _position()
        sent = []
        for a, (f_ref, o_ref) in enumerate(zip(f_refs, o_refs)):
            sent += [_remote(f_ref.at[0], o_ref.at[0], send_sems, recv_sems, 2 * a, (1 - x, y, c)),
                     _remote(f_ref.at[1], o_ref.at[1], send_sems, recv_sems, 2 * a + 1, (x, 1 - y, c))]
        for cp in sent:
            cp.start()
        for cp in sent:
            cp.wait()

    return _comm_call(body, fs, [SDS(f.shape, f.dtype) for f in fs], 2 * len(fs), name=name)


def _sibling_fill(rs, *, name):
    n = len(rs)

    def body(*refs):
        r_refs, send_sems, recv_sems = refs[n:2 * n], refs[-2], refs[-1]
        x, y, c, _ = _position()
        sent = []
        for a, r_ref in enumerate(r_refs):
            mine = r_ref.at[pl.ds(c * (r_ref.shape[0] // 2), r_ref.shape[0] // 2)]
            sent.append(_remote(mine, mine, send_sems, recv_sems, a, (x, y, 1 - c)))
            sent[-1].start()
        for a, r_ref in enumerate(r_refs):
            theirs = r_ref.at[pl.ds((1 - c) * (r_ref.shape[0] // 2), r_ref.shape[0] // 2)]
            _remote(theirs, theirs, send_sems, recv_sems, a, (x, y, c)).wait_recv()
        for cp in sent:
            cp.wait_send()

    return pl.pallas_call(
        body, name=name, in_specs=[ANY] * n, out_specs=[ANY] * n, out_shape=[SDS(r.shape, r.dtype) for r in rs],
        input_output_aliases={i: i for i in range(n)},
        scratch_shapes=[pltpu.SemaphoreType.DMA((n,)), pltpu.SemaphoreType.DMA((n,))],
    )(*rs)


def _gather_all(v, *, name):
    M, C = v.shape

    def body(v_ref, o_ref, send_sems, recv_sems):
        x, y, c, chips = _position()
        slot = lambda px, py, pc: o_ref.at[4 * px + 2 * py + pc]
        first = [_remote(v_ref, slot(x, y, c), send_sems, recv_sems, 0, (x, y, 1 - c))]
        first += [_remote(v_ref, slot(x, y, c), send_sems, recv_sems, 1 + j, (*chip, c)) for j, chip in enumerate(chips)]
        for cp in first:
            cp.start()
        passed = []
        for j, chip in enumerate(chips):
            landed = slot(*chip, c)
            _remote(landed, landed, send_sems, recv_sems, 1 + j, (x, y, c)).wait_recv()
            cp = _remote(landed, landed, send_sems, recv_sems, 4 + j, (x, y, 1 - c))
            cp.start()
            passed.append(cp)
        sib = slot(x, y, 1 - c)
        _remote(sib, sib, send_sems, recv_sems, 0, (x, y, c)).wait_recv()
        for j, chip in enumerate(chips):
            theirs = slot(*chip, 1 - c)
            _remote(theirs, theirs, send_sems, recv_sems, 4 + j, (x, y, c)).wait_recv()
        for cp in first + passed:
            cp.wait_send()

    return pl.pallas_call(
        body, name=name, in_specs=[ANY], out_specs=ANY, out_shape=SDS((N_DEV, M, C), v.dtype),
        scratch_shapes=[pltpu.SemaphoreType.DMA((7,)), pltpu.SemaphoreType.DMA((7,))],
    )(v)


BIG = ("ffn1_w1", "ffn2_w1", "w_in", "ffn1_w2", "ffn2_w2", "w_attn_branch", "w_sgu_branch", "w_out")
COL_SHARDED = ("ffn1_w1", "w_in", "ffn2_w1")
FFN_IN = ("ffn1_w1", "ffn2_w1")
SMALL = ("ffn1_pre_g", "ffn1_post_g", "mix_pre_g", "attn_sinks", "sgu_ln_g", "sgu_ln_b", "sgu_w", "sgu_b",
         "mix_post_g", "ffn2_pre_g", "ffn2_post_g")
WEIGHTS = ("ffn1_pre_g", "ffn1_w1", "ffn1_w2", "ffn1_post_g", "mix_pre_g", "w_in", "attn_sinks", "sgu_ln_g",
           "sgu_ln_b", "sgu_w", "sgu_b", "w_attn_branch", "w_sgu_branch", "w_out", "mix_post_g", "ffn2_pre_g",
           "ffn2_w1", "ffn2_w2", "ffn2_post_g")


def _column_chunks(w, tn):
    return jnp.swapaxes(w.reshape(w.shape[0], w.shape[1] // tn, tn), 0, 1)


def _width_classes(shard_shapes):
    widths = sorted({shard_shapes[n][-1] for n in BIG}, reverse=True)
    return [[n for n in BIG if shard_shapes[n][-1] == w] for w in widths]


def _class_rows(classes, shard_shapes, n_layers, aligned=BIG):
    where = {}
    for k, names in enumerate(classes):
        off = 0
        for n in names:
            r = shard_shapes[n][0]
            assert off % r == 0 or n not in aligned
            where[n] = (k, off, r)
            off += n_layers * r
    return where


def _ffn_fwd(x, pre_g, w1, w1_block, w2, post_g, tag, gather=()):
    a, h, *gathered = _norm_matmul(x, pre_g, w1, w1_block, name=f"{tag}_up", with_h=True, gather=gather)
    xn, o = _swiglu_out(a, w2, x, post_g, name=f"{tag}_down")
    return xn, (x, h, a, o), gathered


def _ffn_bwd(dy, saved, pre_g, w1, w1_block, w2, post_g, dw1_into, dw2_into, tag, after_dw2=None):
    x, h, a, o = saved
    da, do, d_post = _ffn_bwd_hidden(dy, o, post_g, a, w2, name=f"{tag}_bwd_hidden")
    g2 = _dw_rows(a, do, *dw2_into, name=f"{tag}_dw2", swiglu=True)
    ride_in, next_ride = after_dw2(g2) if after_dw2 else (None, None)
    dx, d_pre, *rode = _matmul_nt_norm_bwd(da, w1, w1_block, x, pre_g, dy, None, name=f"{tag}_bwd_in", ride=ride_in)
    if after_dw2 is None:
        return dx, _dw_cols(h, da, N_CHIPS, *dw1_into, name=f"{tag}_dw1"), g2, d_pre, d_post, ()
    g1, *rode = _dw_cols(h, da, N_CHIPS, *dw1_into, name=f"{tag}_dw1", ride=next_ride(rode))
    return dx, g1, g2, d_pre, d_post, rode


def kernel(x, ffn1_pre_g, ffn1_w1, ffn1_w2, ffn1_post_g, mix_pre_g, w_in, attn_sinks, sgu_ln_g, sgu_ln_b, sgu_w, sgu_b, w_attn_branch, w_sgu_branch, w_out, mix_post_g, ffn2_pre_g, ffn2_w1, ffn2_w2, ffn2_post_g, loss_target, m_ffn1_pre_g, m_ffn1_w1, m_ffn1_w2, m_ffn1_post_g, m_mix_pre_g, m_w_in, m_attn_sinks, m_sgu_ln_g, m_sgu_ln_b, m_sgu_w, m_sgu_b, m_w_attn_branch, m_w_sgu_branch, m_w_out, m_mix_post_g, m_ffn2_pre_g, m_ffn2_w1, m_ffn2_w2, m_ffn2_post_g, v_ffn1_pre_g, v_ffn1_w1, v_ffn1_w2, v_ffn1_post_g, v_mix_pre_g, v_w_in, v_attn_sinks, v_sgu_ln_g, v_sgu_ln_b, v_sgu_w, v_sgu_b, v_w_attn_branch, v_w_sgu_branch, v_w_out, v_mix_post_g, v_ffn2_pre_g, v_ffn2_w1, v_ffn2_w2, v_ffn2_post_g):
    given = dict(locals())
    W = {n: given[n] for n in WEIGHTS}
    M = {n: given["m_" + n] for n in WEIGHTS}
    V = {n: given["v_" + n] for n in WEIGHTS}
    L = ffn1_w1.shape[0]
    T, D = x.shape[1], x.shape[2]
    xt = x.reshape(T, D)
    target = loss_target.reshape(T, D)
    assert L % 2 == 0 and D == ATTN_WIDTH == SGU_WIDTH and T % ATTN_BLOCK == 0

    shard_shapes = {n: W[n].shape[1:] for n in BIG}
    classes = _width_classes(shard_shapes)
    my_chip = 2 * lax.axis_index("x") + lax.axis_index("y")
    my_core = lax.axis_index("c")
    where = _class_rows(classes, shard_shapes, L)
    where_w = _class_rows(classes, shard_shapes, 1, aligned=FFN_IN)
    packs = [[jnp.concatenate([W[n][l].astype(BF16) for n in names], axis=0) for names in classes] for l in range(L)]

    def block_of(n, l):
        k, off, r = where[n]
        return k, off // r + l

    def layer_weights(l, got):
        wc = [lax.dynamic_update_slice(g, p[None], (my_chip, 0, 0)) for g, p in zip(got, packs[l])]

        def chip_shards(n):
            k, off, r = where_w[n]
            return wc[k][:, off:off + r, :]

        fw = {n: chip_shards(n).reshape(-1, D) for n in BIG if n not in COL_SHARDED}
        w_in_l = jnp.swapaxes(chip_shards("w_in"), 0, 1).reshape(D, -1)
        fw["w_qkv_t"] = w_in_l[:, :QKV_WIDTH].T
        fw["w_main"] = _column_chunks(w_in_l[:, QKV_WIDTH:], D)
        for n in FFN_IN:
            fw[n] = (wc[where_w[n][0]], where_w[n][1] // where_w[n][2])
        return fw

    full = [None] * L
    full[0] = layer_weights(0, _gather_shards(packs[0], name="gather_weights_l0"))

    row = lambda name, l: W[name][l].reshape(1, -1)
    causal = jnp.tril(jnp.ones((SGU_CHUNK, SGU_CHUNK), dtype=bool))
    saved = []
    h_cur = xt
    for l in range(L):
        fw = full[l]
        sv = {}
        h_cur, sv["ffn1"], got = _ffn_fwd(h_cur, row("ffn1_pre_g", l), *fw["ffn1_w1"], fw["ffn1_w2"],
                                          row("ffn1_post_g", l), f"l{l}_ffn1", gather=packs[l + 1] if l + 1 < L else ())
        if l + 1 < L:
            full[l + 1] = layer_weights(l + 1, got)
        zqkv, hm = _norm_matmul_t(h_cur, row("mix_pre_g", l), fw["w_qkv_t"], name=f"l{l}_mix_in_qkv")
        zmain, = _norm_matmul(h_cur, row("mix_pre_g", l), fw["w_main"], 0, name=f"l{l}_mix_in_main", with_h=False)
        wm = jnp.where(causal[None], sgu_w[l], 0.0).astype(BF16)
        wmt = jnp.swapaxes(wm, 1, 2)
        bias = jnp.broadcast_to(sgu_b[l][:, :, None], (SGU_GROUPS, SGU_CHUNK, 128)).astype(F32)
        y_attn = _attn_fwd(zqkv, attn_sinks[l], name=f"l{l}_attn")
        y_sgu = _sgu_fwd(zmain, row("sgu_ln_g", l), row("sgu_ln_b", l), wm, bias, name=f"l{l}_sgu")
        x_mix = h_cur
        h_cur, pa, ps, mo = _merge_fwd(y_attn, y_sgu, zmain, fw["w_attn_branch"], fw["w_sgu_branch"], fw["w_out"],
                                       x_mix, row("mix_post_g", l), name=f"l{l}_merge")
        sv["mix"] = (x_mix, hm, zqkv, zmain, y_attn, y_sgu, pa, ps, mo, wm, wmt, bias)
        h_cur, sv["ffn2"], _ = _ffn_fwd(h_cur, row("ffn2_pre_g", l), *fw["ffn2_w1"], fw["ffn2_w2"],
                                        row("ffn2_post_g", l), f"l{l}_ffn2")
        saved.append(sv)

    dy, lsum = _loss_head(h_cur, target, name="loss_head")
    loss = lax.psum(0.5 * lsum[0, 0], AXES)

    k_in = where["w_in"][0]
    assert classes[k_in] == ["w_in"]
    gcls = [None if k == k_in else lax.empty((N_CHIPS, L * p.shape[0], p.shape[1]), F32)
            for k, p in enumerate(packs[0])]
    dw_in = [None] * L
    small_grads = [None] * L

    def into(n, l):
        return gcls[block_of(n, l)[0]], block_of(n, l)[1]

    def ffn_bwd(dy, which, l, after_dw2=None):
        n1, n2 = f"{which}_w1", f"{which}_w2"
        dy, g1, g2, d_pre, d_post, rode = _ffn_bwd(
            dy, saved[l][which], row(f"{which}_pre_g", l), *full[l][n1], full[l][n2], row(f"{which}_post_g", l),
            into(n1, l), into(n2, l), f"l{l}_{which}", after_dw2)
        gcls[where[n1][0]], gcls[where[n2][0]] = g1, g2
        return dy, d_pre, d_post, rode

    zero = lambda a: 0
    own = lambda a: a
    core = lambda a: lax.axis_index("c")
    chip = lambda a: 2 * lax.axis_index("x") + lax.axis_index("y")
    chip_xn = lambda a: 2 * (1 - lax.axis_index("x")) + lax.axis_index("y")
    chip_yn = lambda a: 2 * lax.axis_index("x") + 1 - lax.axis_index("y")

    def pair_sum(k, g, fs):
        return _sum_terms([(g, own, core), (fs, own, zero)], fs.shape[1], N_CHIPS, (BF16,), name=f"grads_pair_sum{k}")[0]

    for l in reversed(range(L)):
        fw, sv = full[l], saved[l]
        gs = {}
        dy, gs["ffn2_pre_g"], gs["ffn2_post_g"], _ = ffn_bwd(dy, "ffn2", l)

        x_mix, hm, zqkv, zmain, y_attn, y_sgu, pa, ps, mo, wm, wmt, bias = sv["mix"]
        dzmain, dout, merged, dpa, dps, dya, dys, gs["mix_post_g"] = _merge_bwd(
            dy, mo, row("mix_post_g", l), pa, ps, zmain, fw["w_attn_branch"], fw["w_sgu_branch"], fw["w_out"],
            name=f"l{l}_merge_bwd")
        k_sq = where["w_out"][0]
        gcls[k_sq] = _dw_rows(merged, dout, *into("w_out", l), name=f"l{l}_dw_out")
        gcls[k_sq] = _dw_rows(y_attn, dpa, *into("w_attn_branch", l), name=f"l{l}_dw_attn", a_feature_major=True)
        gcls[k_sq] = _dw_rows(y_sgu, dps, *into("w_sgu_branch", l), name=f"l{l}_dw_sgu")
        dzqkv, dsink = _attn_bwd(zqkv, attn_sinks[l], dya, name=f"l{l}_attn_bwd")
        gs["attn_sinks"] = dsink[0, :N_Q_HEADS]
        dzmain, dsw, dsb, gs["sgu_ln_g"], gs["sgu_ln_b"] = _sgu_bwd(
            zmain, dzmain, dys, row("sgu_ln_g", l), row("sgu_ln_b", l), wm, wmt, bias, name=f"l{l}_sgu_bwd")
        gs["sgu_w"] = dsw
        gs["sgu_b"] = dsb[:, :SGU_GROUPS].T
        dh_qkv = _matmul_tn_rows(dzqkv, fw["w_qkv_t"], name=f"l{l}_mix_bwd_qkv")
        dy, gs["mix_pre_g"] = _matmul_nt_norm_bwd(dzmain, fw["w_main"], 0, x_mix, row("mix_pre_g", l), dy, dh_qkv,
                                                   name=f"l{l}_mix_bwd_in")
        dw_main = _dw_cols(hm, dzmain, zmain.shape[1] // D, None, 0, name=f"l{l}_dw_in_main")
        dw_in[l] = jnp.concatenate([_matmul_tokens(dzqkv, hm, name=f"l{l}_dw_in_qkv").T,
                                    jnp.swapaxes(dw_main, 0, 1).reshape(D, -1)], axis=1)

        if l > 0:
            dy, gs["ffn1_pre_g"], gs["ffn1_post_g"], _ = ffn_bwd(dy, "ffn1", l)
        else:
            w_in_width = shard_shapes["w_in"][1]
            gcls[k_in] = jnp.stack([jnp.concatenate([g[:, s * w_in_width:(s + 1) * w_in_width] for g in dw_in], axis=0)
                                    for s in range(N_CHIPS)])
            k_up, k_down = where["ffn1_w1"][0], where["ffn1_w2"][0]
            assert sorted((k_up, k_down, k_in)) == list(range(len(classes)))
            from_sibling, pairs, hop1 = ([None] * len(classes) for _ in range(3))

            def after_dw2(g_down):
                def hop1_ride(rode):
                    from_sibling[k_down], from_sibling[k_in] = rode
                    pairs[k_down] = pair_sum(k_down, g_down, rode[0])
                    pairs[k_in] = pair_sum(k_in, gcls[k_in], rode[1])
                    return "hop1", [pairs[k_down], pairs[k_in]]
                return ("exchange", [g_down, gcls[k_in]]), hop1_ride

            dy, gs["ffn1_pre_g"], gs["ffn1_post_g"], (hop1[k_down], hop1[k_in]) = ffn_bwd(dy, "ffn1", l, after_dw2)
        small_grads[l] = gs
    grad_x = dy.reshape(x.shape)

    grs = gcls
    from_sibling[k_up], = _sibling_exchange([grs[k_up]], name="grads_sibling_exchange")
    pairs[k_up] = pair_sum(k_up, grs[k_up], from_sibling[k_up])
    hop1[k_up], = _scatter_hop1([pairs[k_up]], name="grads_scatter_hop1")
    relay = [_sum_terms([(p, lambda a: chip_xn(a) + a * (chip_yn(a) - chip_xn(a)), own), (h, lambda a: 3 - 2 * a, zero)],
                        h.shape[1], 2, (BF16,), name=f"grads_relay_sum{k}")[0]
             for k, (p, h) in enumerate(zip(pairs, hop1))]
    hop2 = _scatter_hop2(relay, name="grads_scatter_hop2")
    quarter = lambda a: 2 * core(a) + a
    halves = [_sum_terms([(g, chip, quarter), (fs, chip, own), (h1, lambda a: 2 - 2 * a, zero), (h2, own, zero)],
                         h1.shape[1], 2, (F32,), name=f"grads_chip_sum{k}", out_lead=quarter, out_n_lead=4)[0]
              .reshape(g.shape[1:]) for k, (g, fs, h1, h2) in enumerate(zip(grs, from_sibling, hop1, hop2))]
    reduced_all = _sibling_fill(halves, name="grads_sibling_fill")

    grads = {n: [None] * L for n in SMALL}
    for names, reduced in zip(classes, reduced_all):
        for n in names:
            _, off, r = where[n]
            grads[n] = reduced[off:off + L * r].reshape((L,) + shard_shapes[n])

    def small_rows(gs):
        parts = []
        for n in SMALL:
            flat = gs[n].reshape(-1)
            pad = (-flat.shape[0]) % D
            parts.append(jnp.pad(flat, (0, pad)).reshape(-1, D))
        return jnp.concatenate(parts, axis=0)

    spack = jnp.concatenate([small_rows(small_grads[l]) for l in range(L)], axis=0)
    n_small = spack.shape[0]
    pad_rows = (-n_small) % SUBLANES_BF16
    spack = jnp.pad(spack, ((0, pad_rows), (0, 0)))
    everyone = _gather_all(spack, name="small_grads_gather")
    is_me = (jnp.arange(N_DEV) == 2 * my_chip + my_core)[:, None, None]
    everyone = jnp.where(is_me, spack[None], everyone)
    ssum = _sum_terms([(everyone, (lambda a, d=d: d), zero) for d in range(N_DEV)], spack.shape[0], 1, (F32,),
                      name="small_grads_sum")[0][0]
    per_layer = n_small // L
    for l in range(L):
        r0 = l * per_layer
        for n in SMALL:
            shp = W[n].shape[1:]
            size = math.prod(shp)
            nr = -(-size // D)
            grads[n][l] = ssum[r0:r0 + nr].reshape(-1)[:size].reshape(shp)
            r0 += nr
    grads.update({n: jnp.stack(grads[n]) for n in SMALL})

    delta, new_m, new_v = {}, {}, {}
    for n in WEIGHTS:
        delta[n], new_m[n], new_v[n] = _adamw(W[n], grads[n], M[n], V[n], name=f"adamw_{n}")

    return (loss, grad_x, *[grads[n] for n in WEIGHTS], *[delta[n] for n in WEIGHTS],
            *[new_m[n] for n in WEIGHTS], *[new_v[n] for n in WEIGHTS])
```

```python
import functools
import math

import jax
import jax.numpy as jnp
from jax import lax
from jax.experimental import pallas as pl
from jax.experimental.pallas import tpu as pltpu

F32, BF16 = jnp.float32, jnp.bfloat16
SDS = jax.ShapeDtypeStruct
MESH = pl.DeviceIdType.MESH
AXES = ("x", "y", "c")

HEAD_DIM = 64
N_Q_HEADS = 16
N_KV_HEADS = 2
Q_PER_KV = N_Q_HEADS // N_KV_HEADS
ATTN_WIDTH = N_Q_HEADS * HEAD_DIM
KV_WIDTH = N_KV_HEADS * HEAD_DIM
ATTN_BLOCK = 128
SGU_CHUNK = 128
SGU_GROUPS = 8
SGU_WIDTH = SGU_GROUPS * 128
QKV_WIDTH = ATTN_WIDTH + 2 * KV_WIDTH
RMS_EPS = 1e-6
LN_EPS = 1e-5
MASK_VALUE = -1e30
ATTN_SCALE = 1.0 / math.sqrt(HEAD_DIM)
assert math.frexp(ATTN_SCALE)[0] == 0.5

ADAM_LR, ADAM_B1, ADAM_B2, ADAM_EPS, ADAM_WD, ADAM_STEP = 0.001, 0.9, 0.999, 1e-08, 0.01, 10

N_CHIPS = 4
N_DEV = 8

VMEM_LIMIT_BYTES = 56 * 1024 * 1024
LANES = 128
SUBLANES_BF16 = 16

TM_NORM_MATMUL = 1024
TM_ROW = 512
TM_FFN_BWD = 512
TT_REDUCE = 1024
TQ_ATTN = 1024
TM_FEATURE_MAJOR = 1024
TS_SGU = 512


def _tile(n, pref, mult):
    t = (min(pref, n) // mult) * mult
    while t >= mult:
        if n % t == 0:
            return t
        t -= mult
    return n


def _params(*sem):
    return pltpu.CompilerParams(dimension_semantics=sem, vmem_limit_bytes=VMEM_LIMIT_BYTES)


def _dot(a, b):
    return jnp.dot(a, b, preferred_element_type=F32)


def _dot_nt(a, b):
    return lax.dot_general(a, b, (((1,), (1,)), ((), ())), preferred_element_type=F32)


def _dot_tn(a, b):
    return lax.dot_general(a, b, (((0,), (0,)), ((), ())), preferred_element_type=F32)


def _sigmoid(x):
    return 0.5 * (1.0 + jnp.tanh(0.5 * x))


def _rms_stats(xf):
    r = lax.rsqrt(jnp.mean(xf * xf, axis=-1, keepdims=True) + RMS_EPS)
    return r, xf * r


def _rms_bwd(xf, g, dy):
    r, xh = _rms_stats(xf)
    dyg = dy * g
    dx = r * (dyg - xh * jnp.mean(dyg * xh, axis=-1, keepdims=True))
    return dx, jnp.sum(dy * xh, axis=0, keepdims=True)


def _gelu_parts(x):
    cdf = 0.5 * (1.0 + lax.erf(x * (1.0 / math.sqrt(2.0))))
    return cdf


def _gelu(x):
    return x * _gelu_parts(x)


def _gelu_grad(x):
    return _gelu_parts(x) + x * jnp.exp(-0.5 * x * x) * (1.0 / math.sqrt(2.0 * math.pi))


def _resident(shape, index=None):
    index = (0,) * len(shape) if index is None else index
    return pl.BlockSpec(shape, lambda *_: index, pipeline_mode=pl.Buffered(1))


def _norm_matmul(x, g, w3, w_block, *, name, with_h, gather=()):
    T, D = x.shape
    nj, _, tn = w3.shape
    tm = _tile(T, TM_NORM_MATMUL, SUBLANES_BF16)
    ni, ng = T // tm, len(gather)
    n_out = 1 + with_h

    def body(x_ref, g_ref, w_ref, *rest):
        a_ref, h_sc = rest[ng], rest[ng + n_out + ng]
        i, j = pl.program_id(0), pl.program_id(1)
        if ng:
            stages = _gather_stages(rest[:ng], rest[ng + n_out:ng + n_out + ng], *rest[-2:])
            for stage, (si, sj) in zip(stages[:2], ((0, 0), (ni // 2, 0))):
                pl.when((i == si) & (j == sj))(stage)

        @pl.when(j == 0)
        def _():
            _, xh = _rms_stats(x_ref[...])
            h = (xh * g_ref[...]).astype(BF16)
            h_sc[...] = h
            if with_h:
                rest[ng + 1][...] = h

        a_ref[...] = _dot(h_sc[...], w_ref[j]).astype(BF16)
        if ng:
            pl.when((i == ni - 1) & (j == nj - 1))(stages[2])

    out_specs = [pl.BlockSpec((tm, tn), lambda i, j: (i, j))]
    out_shape = [SDS((T, nj * tn), BF16)]
    if with_h:
        out_specs.append(pl.BlockSpec((tm, D), lambda i, j: (i, 0)))
        out_shape.append(SDS((T, D), BF16))
    scratch = [pltpu.VMEM((tm, D), BF16)]
    if ng:
        scratch += [pltpu.SemaphoreType.DMA((GATHER_SEMS * ng,))] * 2
    return pl.pallas_call(
        body, name=name, grid=(ni, nj),
        in_specs=[pl.BlockSpec((tm, D), lambda i, j: (i, 0)),
                  pl.BlockSpec((1, D), lambda i, j: (0, 0)),
                  _resident((nj, D, tn), (0, w_block, 0))] + [ANY] * ng,
        out_specs=out_specs + [ANY] * ng, out_shape=out_shape + _gathered_shapes(gather),
        scratch_shapes=scratch,
        compiler_params=_params(*(("arbitrary", "arbitrary") if ng else ("parallel", "arbitrary"))),
    )(x, g, w3, *gather)


def _norm_matmul_t(x, g, wt, *, name):
    T, D = x.shape
    N = wt.shape[0]
    tm = _tile(T, TM_FEATURE_MAJOR, LANES)

    def body(x_ref, g_ref, w_ref, a_ref, h_ref):
        _, xh = _rms_stats(x_ref[...])
        h = (xh * g_ref[...]).astype(BF16)
        h_ref[...] = h
        a_ref[...] = _dot_nt(w_ref[...], h).astype(BF16)

    return pl.pallas_call(
        body, name=name, grid=(T // tm,),
        in_specs=[pl.BlockSpec((tm, D), lambda i: (i, 0)), pl.BlockSpec((1, D), lambda i: (0, 0)),
                  _resident((N, D))],
        out_specs=[pl.BlockSpec((N, tm), lambda i: (0, i)), pl.BlockSpec((tm, D), lambda i: (i, 0))],
        out_shape=[SDS((N, T), BF16), SDS((T, D), BF16)],
        compiler_params=_params("parallel"),
    )(x, g, wt)


def _matmul_tokens(at, b, *, name):
    K, T = at.shape
    N = b.shape[1]
    tt = _tile(T, TT_REDUCE, LANES)

    def body(a_ref, b_ref, o_ref):
        @pl.when(pl.program_id(0) == 0)
        def _():
            o_ref[...] = jnp.zeros_like(o_ref)

        o_ref[...] += _dot(a_ref[...], b_ref[...])

    return pl.pallas_call(
        body, name=name, grid=(T // tt,),
        in_specs=[pl.BlockSpec((K, tt), lambda t: (0, t)), pl.BlockSpec((tt, N), lambda t: (t, 0))],
        out_specs=pl.BlockSpec((K, N), lambda t: (0, 0)),
        out_shape=SDS((K, N), F32),
        compiler_params=_params("arbitrary"),
    )(at, b)


def _matmul_tn_rows(dat, wt, *, name):
    N, T = dat.shape
    D = wt.shape[1]
    tm = _tile(T, TM_FEATURE_MAJOR, LANES)

    def body(da_ref, w_ref, o_ref):
        o_ref[...] = _dot_tn(da_ref[...], w_ref[...])

    return pl.pallas_call(
        body, name=name, grid=(T // tm,),
        in_specs=[pl.BlockSpec((N, tm), lambda i: (0, i)), _resident((N, D))],
        out_specs=pl.BlockSpec((tm, D), lambda i: (i, 0)),
        out_shape=SDS((T, D), F32),
        compiler_params=_params("parallel"),
    )(dat, wt)


def _ff_chunk(F):
    return F if F <= 1408 else F // 2


def _swiglu_out(a, w2, x, g_post, *, name):
    T, F2 = a.shape
    F = F2 // 2
    D = x.shape[1]
    tm = _tile(T, TM_ROW, SUBLANES_BF16)
    fc = _ff_chunk(F)

    def body(a_ref, w_ref, x_ref, g_ref, xn_ref, o_ref):
        acc = None
        for c0 in range(0, F, fc):
            gt = a_ref[:, c0:c0 + fc].astype(F32)
            s = (gt * _sigmoid(gt)).astype(BF16) * a_ref[:, F + c0:F + c0 + fc]
            part = _dot(s, w_ref[c0:c0 + fc, :])
            acc = part if acc is None else acc + part
        o_ref[...] = acc.astype(BF16)
        _, oh = _rms_stats(acc)
        xn_ref[...] = x_ref[...] + 0.5 * (oh * g_ref[...])

    return pl.pallas_call(
        body, name=name, grid=(T // tm,),
        in_specs=[pl.BlockSpec((tm, F2), lambda i: (i, 0)),
                  _resident((F, D)),
                  pl.BlockSpec((tm, D), lambda i: (i, 0)),
                  pl.BlockSpec((1, D), lambda i: (0, 0))],
        out_specs=[pl.BlockSpec((tm, D), lambda i: (i, 0)), pl.BlockSpec((tm, D), lambda i: (i, 0))],
        out_shape=[SDS((T, D), F32), SDS((T, D), BF16)],
        compiler_params=_params("parallel"),
    )(a, w2, x, g_post)


def _ffn_bwd_hidden(dy, o, g_post, a, w2, *, name):
    T, F2 = a.shape
    F = F2 // 2
    D = dy.shape[1]
    tm = _tile(T, TM_FFN_BWD, SUBLANES_BF16)
    fc = _tile(F, 256, LANES)

    def body(dy_ref, o_ref, g_ref, a_ref, w_ref, da_ref, do_ref, dg_ref):
        @pl.when(pl.program_id(0) == 0)
        def _():
            dg_ref[...] = jnp.zeros_like(dg_ref)

        do, dg = _rms_bwd(o_ref[...].astype(F32), g_ref[...], 0.5 * dy_ref[...])
        dg_ref[...] += dg
        dob = do.astype(BF16)
        do_ref[...] = dob
        for c0 in range(0, F, fc):
            ds = _dot_nt(dob, w_ref[c0:c0 + fc, :]).astype(BF16)
            gt = a_ref[:, c0:c0 + fc].astype(F32)
            ub = a_ref[:, F + c0:F + c0 + fc]
            sg = _sigmoid(gt)
            sl = gt * sg
            dsl = (sg + sl * (1.0 - sg)).astype(BF16)
            da_ref[:, c0:c0 + fc] = ds * ub * dsl
            da_ref[:, F + c0:F + c0 + fc] = ds * sl.astype(BF16)

    row = lambda w: pl.BlockSpec((tm, w), lambda i: (i, 0))
    return pl.pallas_call(
        body, name=name, grid=(T // tm,),
        in_specs=[row(D), row(D), pl.BlockSpec((1, D), lambda i: (0, 0)), row(F2),
                  _resident((F, D))],
        out_specs=[row(F2), row(D), pl.BlockSpec((1, D), lambda i: (0, 0))],
        out_shape=[SDS((T, F2), BF16), SDS((T, D), BF16), SDS((1, D), F32)],
        compiler_params=_params("arbitrary"),
    )(dy, o, g_post, a, w2)


def _dw_call(body, name, grid, in_specs, args, block, pack, row_block, sem):
    if pack is None:
        out_spec = pl.BlockSpec(block, lambda *_: (0, 0, 0), pipeline_mode=pl.Buffered(1))
        return pl.pallas_call(body, name=name, grid=grid, in_specs=in_specs, out_specs=out_spec,
                              out_shape=SDS(block, F32), compiler_params=_params(*sem))(*args)
    assert pack.shape[0] == block[0] and pack.shape[2] == block[2]
    out_spec = pl.BlockSpec(block, lambda *_: (0, row_block, 0), pipeline_mode=pl.Buffered(1))
    return pl.pallas_call(body, name=name, grid=grid, in_specs=in_specs + [ANY], out_specs=out_spec,
                          out_shape=SDS(pack.shape, F32), input_output_aliases={len(args): 0},
                          compiler_params=_params(*sem))(*args, pack)


def _dw_cols(a, b, n_chunks, pack, row_block, *, name, ride=()):
    T, K = a.shape
    tn = b.shape[1] // n_chunks
    tt = _tile(T, TT_REDUCE, SUBLANES_BF16)
    per = 2 if n_chunks % 2 == 0 else 1
    grid = (n_chunks // per, T // tt)
    kind, riders = ride if ride else (None, ())
    nr = len(riders)

    def body(a_ref, b_ref, *rest):
        o_ref = rest[1 + nr] if nr else rest[-1]
        j, t = pl.program_id(0), pl.program_id(1)
        if nr:
            rides = RIDES[kind][0](rest[1:1 + nr], rest[2 + nr:2 + 2 * nr], rest[-2], rest[-1])

            @pl.when((j == 0) & (t == 0))
            def _():
                for cp in rides:
                    cp.start()

        @pl.when(t == 0)
        def _():
            o_ref[...] = jnp.zeros_like(o_ref)

        for p in range(per):
            o_ref[p] += _dot_tn(a_ref[...], b_ref[:, p * tn:(p + 1) * tn])

        if nr:
            @pl.when((j == grid[0] - 1) & (t == grid[1] - 1))
            def _():
                for cp in rides:
                    cp.wait()

    in_specs = [pl.BlockSpec((tt, K), lambda j, t: (t, 0)), pl.BlockSpec((tt, per * tn), lambda j, t: (t, j))]
    sem = ("arbitrary", "arbitrary") if nr else ("parallel", "arbitrary")
    if pack is None:
        assert not nr
        return pl.pallas_call(body, name=name, grid=grid, in_specs=in_specs,
                              out_specs=pl.BlockSpec((per, K, tn), lambda j, t: (j, 0, 0)),
                              out_shape=SDS((n_chunks, K, tn), F32), compiler_params=_params(*sem))(a, b)
    assert pack.shape[0] == n_chunks and pack.shape[2] == tn
    out = pl.pallas_call(
        body, name=name, grid=grid, in_specs=in_specs + [ANY] * (1 + nr),
        out_specs=[pl.BlockSpec((per, K, tn), lambda j, t: (j, row_block, 0))] + [ANY] * nr,
        out_shape=[SDS(pack.shape, F32)] + (RIDES[kind][1](riders) if nr else []),
        input_output_aliases={2: 0},
        scratch_shapes=[pltpu.SemaphoreType.DMA((RIDES[kind][2] * nr,))] * 2 if nr else [],
        compiler_params=_params(*sem))(a, b, pack, *riders)
    return out if nr else out[0]


def _dw_rows(a, b, pack, row_block, *, name, a_feature_major=False, swiglu=False):
    K, T = a.shape if a_feature_major else a.shape[::-1]
    K = K // 2 if swiglu else K
    N = b.shape[1]
    r = K // N_CHIPS
    cw = r if r % LANES == 0 else 2 * r
    assert cw % LANES == 0 and K % cw == 0 and r % 8 == 0
    tt = _tile(T, TT_REDUCE // 2 if swiglu else TT_REDUCE, LANES)

    def body(a_ref, b_ref, *rest):
        o_ref = rest[-1]

        @pl.when(pl.program_id(0) == 0)
        def _():
            o_ref[...] = jnp.zeros_like(o_ref)

        for c in range(K // cw):
            if a_feature_major:
                part = _dot(a_ref[c * cw:(c + 1) * cw, :], b_ref[...])
            elif swiglu:
                gt = a_ref[:, c * cw:(c + 1) * cw].astype(F32)
                part = _dot_tn((gt * _sigmoid(gt)).astype(BF16) * a_ref[:, K + c * cw:K + (c + 1) * cw], b_ref[...])
            else:
                part = _dot_tn(a_ref[:, c * cw:(c + 1) * cw], b_ref[...])
            for p in range(cw // r):
                o_ref[c * (cw // r) + p] += part[p * r:(p + 1) * r]

    a_spec = (pl.BlockSpec((K, tt), lambda t: (0, t)) if a_feature_major
              else pl.BlockSpec((tt, a.shape[1]), lambda t: (t, 0)))
    return _dw_call(body, name, (T // tt,), [a_spec, pl.BlockSpec((tt, N), lambda t: (t, 0))],
                    [a, b], (N_CHIPS, r, N), pack, row_block, ("arbitrary",))


def _matmul_nt_norm_bwd(da, w, w_block, x, g, dy, init, *, name, ride=None):
    T, N = da.shape
    D = x.shape[1]
    nj, _, tn = w.shape
    tm = _tile(T, TM_ROW, SUBLANES_BF16)
    ni = T // tm
    has_init = init is not None
    kind, riders = ride if ride else (None, ())
    nr = len(riders)
    n_in = 5 + has_init

    def body(*refs):
        da_ref, w_ref, x_ref, g_ref, dy_ref = refs[:5]
        dx_ref, dg_ref = refs[n_in + nr:n_in + nr + 2]
        i = pl.program_id(0)
        if nr:
            rides = RIDES[kind][0](refs[n_in:n_in + nr], refs[n_in + nr + 2:n_in + 2 * nr + 2], refs[-2], refs[-1])

            @pl.when(i == 0)
            def _():
                for cp in rides:
                    cp.start()

        @pl.when(i == 0)
        def _():
            dg_ref[...] = jnp.zeros_like(dg_ref)

        dh = refs[5][...] if has_init else None
        for j in range(nj):
            part = _dot_nt(da_ref[:, j * tn:(j + 1) * tn], w_ref[j])
            dh = part if dh is None else dh + part
        dx, dg = _rms_bwd(x_ref[...], g_ref[...], dh)
        dx_ref[...] = dy_ref[...] + dx
        dg_ref[...] += dg

        if nr:
            @pl.when(i == ni - 1)
            def _():
                for cp in rides:
                    cp.wait()

    row = pl.BlockSpec((tm, D), lambda i: (i, 0))
    vec = pl.BlockSpec((1, D), lambda i: (0, 0))
    in_specs = [pl.BlockSpec((tm, N), lambda i: (i, 0)), _resident((nj, D, tn), (0, w_block, 0)), row, vec, row]
    args = [da, w, x, g, dy]
    if has_init:
        in_specs.append(row)
        args.append(init)
    return pl.pallas_call(
        body, name=name, grid=(ni,), in_specs=in_specs + [ANY] * nr,
        out_specs=[row, vec] + [ANY] * nr,
        out_shape=[SDS((T, D), F32), SDS((1, D), F32)] + (RIDES[kind][1](riders) if nr else []),
        scratch_shapes=[pltpu.SemaphoreType.DMA((RIDES[kind][2] * nr,))] * 2 if nr else [],
        compiler_params=_params("arbitrary"),
    )(*args, *riders)


GROUP_LANES = Q_PER_KV * ATTN_BLOCK


def _attn_mask_t(first):
    kj = lax.broadcasted_iota(jnp.int32, (2 * ATTN_BLOCK, ATTN_BLOCK), 0)
    qi = lax.broadcasted_iota(jnp.int32, (2 * ATTN_BLOCK, ATTN_BLOCK), 1)
    rel = qi + ATTN_BLOCK - kj
    band = (rel >= 0) & (rel < ATTN_BLOCK)
    if first is False:
        return band
    return band & ((kj >= ATTN_BLOCK) | jnp.logical_not(first))


def _attn_probs_t(st, valid, sink):
    s = jnp.where(valid, st, MASK_VALUE)
    m = jnp.maximum(jnp.max(s, axis=0, keepdims=True), sink)
    p = jnp.exp(s - m)
    es = jnp.exp(sink - m)
    inv = 1.0 / (jnp.sum(p, axis=0, keepdims=True) + es)
    return p * inv, es * inv


def _attn_specs(tq, tile_of):
    nb = tq // ATTN_BLOCK
    krow, vrow = ATTN_WIDTH // KV_WIDTH, ATTN_WIDTH // KV_WIDTH + 1
    halo = lambda r: pl.BlockSpec((KV_WIDTH, ATTN_BLOCK), lambda t: (r, jnp.maximum(tile_of(t) * nb - 1, 0)))
    return [pl.BlockSpec((ATTN_WIDTH, tq), lambda t: (0, tile_of(t))),
            pl.BlockSpec((KV_WIDTH, tq), lambda t: (krow, tile_of(t))),
            pl.BlockSpec((KV_WIDTH, tq), lambda t: (vrow, tile_of(t))),
            halo(krow), halo(vrow)]


def _head_rows(g, r):
    h = g * Q_PER_KV + r
    return h, slice(h * HEAD_DIM, (h + 1) * HEAD_DIM)


def _group_stack(ref, g, cols):
    return jnp.concatenate([ref[_head_rows(g, r)[1], cols] for r in range(Q_PER_KV)], axis=1)


def _attn_fwd(zt, sinks, *, name):
    T = zt.shape[1]
    tq = _tile(T, TQ_ATTN, ATTN_BLOCK)
    nb = tq // ATTN_BLOCK

    def body(q_ref, k_ref, v_ref, kh_ref, vh_ref, s_ref, o_ref, kf, vf, pt):
        kf[:, 0:ATTN_BLOCK] = kh_ref[...]
        kf[:, ATTN_BLOCK:] = k_ref[...]
        vf[:, 0:ATTN_BLOCK] = vh_ref[...]
        vf[:, ATTN_BLOCK:] = v_ref[...]
        for b in range(nb):
            cols = slice(b * ATTN_BLOCK, (b + 1) * ATTN_BLOCK)
            win = slice(b * ATTN_BLOCK, (b + 2) * ATTN_BLOCK)
            valid = _attn_mask_t((pl.program_id(0) == 0) if b == 0 else False)
            for g in range(N_KV_HEADS):
                gr = slice(g * HEAD_DIM, (g + 1) * HEAD_DIM)
                st = _dot_tn(kf[gr, win], _group_stack(q_ref, g, cols) * ATTN_SCALE)
                for r in range(Q_PER_KV):
                    h, _ = _head_rows(g, r)
                    sl = slice(r * ATTN_BLOCK, (r + 1) * ATTN_BLOCK)
                    probs, _ = _attn_probs_t(st[:, sl], valid, s_ref[h])
                    pt[:, sl] = probs.astype(BF16)
                ot = _dot(vf[gr, win], pt[...])
                for r in range(Q_PER_KV):
                    o_ref[_head_rows(g, r)[1], cols] = ot[:, r * ATTN_BLOCK:(r + 1) * ATTN_BLOCK].astype(BF16)

    return pl.pallas_call(
        body, name=name, grid=(T // tq,),
        in_specs=_attn_specs(tq, lambda t: t) + [pl.BlockSpec(memory_space=pltpu.SMEM)],
        out_specs=pl.BlockSpec((ATTN_WIDTH, tq), lambda t: (0, t)),
        out_shape=SDS((ATTN_WIDTH, T), BF16),
        scratch_shapes=[pltpu.VMEM((KV_WIDTH, tq + ATTN_BLOCK), BF16)] * 2
        + [pltpu.VMEM((2 * ATTN_BLOCK, GROUP_LANES), BF16)],
        compiler_params=_params("parallel"),
    )(zt, zt, zt, zt, zt, sinks)


def _attn_bwd(zt, sinks, dot_, *, name):
    T = zt.shape[1]
    tq = _tile(T, TQ_ATTN, ATTN_BLOCK)
    nb = tq // ATTN_BLOCK
    nt = T // tq
    tile_of = lambda t: nt - 1 - t

    def body(q_ref, k_ref, v_ref, kh_ref, vh_ref, do_ref, s_ref, dz_ref, dsink_ref, kf, vf, dkf, dvf, carry, pt, dst):
        t = pl.program_id(0)

        @pl.when(t == 0)
        def _():
            carry[...] = jnp.zeros_like(carry)
            dsink_ref[...] = jnp.zeros_like(dsink_ref)

        kf[:, 0:ATTN_BLOCK] = kh_ref[...]
        kf[:, ATTN_BLOCK:] = k_ref[...]
        vf[:, 0:ATTN_BLOCK] = vh_ref[...]
        vf[:, ATTN_BLOCK:] = v_ref[...]
        dkf[...] = jnp.zeros_like(dkf)
        dvf[...] = jnp.zeros_like(dvf)
        dkf[:, tq:] = carry[0:KV_WIDTH, :]
        dvf[:, tq:] = carry[KV_WIDTH:, :]
        lane = lax.broadcasted_iota(jnp.int32, (1, LANES), 1)
        dsink = jnp.zeros((1, LANES), F32)
        for b in range(nb):
            cols = slice(b * ATTN_BLOCK, (b + 1) * ATTN_BLOCK)
            win = slice(b * ATTN_BLOCK, (b + 2) * ATTN_BLOCK)
            valid = _attn_mask_t((t == nt - 1) if b == 0 else False)
            for g in range(N_KV_HEADS):
                gr = slice(g * HEAD_DIM, (g + 1) * HEAD_DIM)
                kt2, vt2 = kf[gr, win], vf[gr, win]
                qst = _group_stack(q_ref, g, cols) * ATTN_SCALE
                dost = _group_stack(do_ref, g, cols)
                st = _dot_tn(kt2, qst)
                dpt = _dot_tn(vt2, dost)
                for r in range(Q_PER_KV):
                    h, _ = _head_rows(g, r)
                    sl = slice(r * ATTN_BLOCK, (r + 1) * ATTN_BLOCK)
                    probs, psink = _attn_probs_t(st[:, sl], valid, s_ref[h])
                    dp = dpt[:, sl]
                    delta = jnp.sum(probs * dp, axis=0, keepdims=True)
                    pt[:, sl] = probs.astype(BF16)
                    dst[:, sl] = (probs * (dp - delta)).astype(BF16)
                    dsink = dsink + jnp.where(lane == h, -jnp.sum(psink * delta), 0.0)
                dqt = _dot(kt2, dst[...]) * ATTN_SCALE
                for r in range(Q_PER_KV):
                    dz_ref[_head_rows(g, r)[1], cols] = dqt[:, r * ATTN_BLOCK:(r + 1) * ATTN_BLOCK].astype(BF16)
                dkf[gr, win] += _dot_nt(qst, dst[...])
                dvf[gr, win] += _dot_nt(dost, pt[...])
        dz_ref[ATTN_WIDTH:ATTN_WIDTH + KV_WIDTH, :] = dkf[:, ATTN_BLOCK:].astype(BF16)
        dz_ref[ATTN_WIDTH + KV_WIDTH:, :] = dvf[:, ATTN_BLOCK:].astype(BF16)
        carry[0:KV_WIDTH, :] = dkf[:, 0:ATTN_BLOCK]
        carry[KV_WIDTH:, :] = dvf[:, 0:ATTN_BLOCK]
        dsink_ref[...] += dsink

    return pl.pallas_call(
        body, name=name, grid=(nt,),
        in_specs=_attn_specs(tq, tile_of) + [pl.BlockSpec((ATTN_WIDTH, tq), lambda t: (0, tile_of(t))),
                                             pl.BlockSpec(memory_space=pltpu.SMEM)],
        out_specs=[pl.BlockSpec((QKV_WIDTH, tq), lambda t: (0, tile_of(t))),
                   pl.BlockSpec((8, LANES), lambda t: (0, 0))],
        out_shape=[SDS((QKV_WIDTH, T), BF16), SDS((8, LANES), F32)],
        scratch_shapes=[pltpu.VMEM((KV_WIDTH, tq + ATTN_BLOCK), BF16)] * 2
        + [pltpu.VMEM((KV_WIDTH, tq + ATTN_BLOCK), F32)] * 2 + [pltpu.VMEM((2 * KV_WIDTH, ATTN_BLOCK), F32)]
        + [pltpu.VMEM((2 * ATTN_BLOCK, GROUP_LANES), BF16)] * 2,
        compiler_params=_params("arbitrary"),
    )(zt, zt, zt, zt, zt, dot_, sinks)


def _layer_norm_stats(v):
    mu = jnp.mean(v, axis=-1, keepdims=True)
    xc = v - mu
    rstd = lax.rsqrt(jnp.mean(xc * xc, axis=-1, keepdims=True) + LN_EPS)
    return rstd, xc * rstd


def _sgu_fwd(zmain, ln_g, ln_b, wm, bias, *, name):
    T = zmain.shape[0]
    ts = _tile(T, TS_SGU, SGU_CHUNK)

    def body(u_ref, v_ref, g_ref, b_ref, w_ref, bias_ref, y_ref):
        u = _gelu(u_ref[...].astype(F32))
        _, vh = _layer_norm_stats(_gelu(v_ref[...].astype(F32)))
        vn = (vh * g_ref[...] + b_ref[...]).astype(BF16)
        for ch in range(ts // SGU_CHUNK):
            rows = slice(ch * SGU_CHUNK, (ch + 1) * SGU_CHUNK)
            for g in range(SGU_GROUPS):
                cols = slice(g * 128, (g + 1) * 128)
                s = _dot(w_ref[g], vn[rows, cols]) + bias_ref[g]
                y_ref[rows, cols] = (u[rows, cols] * s).astype(BF16)

    full = _resident
    return pl.pallas_call(
        body, name=name, grid=(T // ts,),
        in_specs=[pl.BlockSpec((ts, SGU_WIDTH), lambda i: (i, 0)), pl.BlockSpec((ts, SGU_WIDTH), lambda i: (i, 1)),
                  full((1, SGU_WIDTH)), full((1, SGU_WIDTH)), full(wm.shape), full(bias.shape)],
        out_specs=pl.BlockSpec((ts, SGU_WIDTH), lambda i: (i, 0)),
        out_shape=SDS((T, SGU_WIDTH), BF16),
        compiler_params=_params("parallel"),
    )(zmain, zmain, ln_g, ln_b, wm, bias)


def _sgu_bwd(zmain, dzmain, dy, ln_g, ln_b, wm, wmt, bias, *, name):
    T = zmain.shape[0]
    ts = _tile(T, TS_SGU, SGU_CHUNK)

    def body(u_ref, v_ref, dy_ref, g_ref, b_ref, w_ref, wt_ref, bias_ref, _, dz_ref, dw_ref, db_ref, dlg_ref, dlb_ref,
             dvn):
        @pl.when(pl.program_id(0) == 0)
        def _():
            dw_ref[...] = jnp.zeros_like(dw_ref)
            db_ref[...] = jnp.zeros_like(db_ref)
            dlg_ref[...] = jnp.zeros_like(dlg_ref)
            dlb_ref[...] = jnp.zeros_like(dlb_ref)

        us = u_ref[...].astype(F32)
        vs = v_ref[...].astype(F32)
        u = _gelu(us)
        rstd, vh = _layer_norm_stats(_gelu(vs))
        vn = (vh * g_ref[...] + b_ref[...]).astype(BF16)
        causal = (lax.broadcasted_iota(jnp.int32, (SGU_CHUNK, SGU_CHUNK), 0)
                  >= lax.broadcasted_iota(jnp.int32, (SGU_CHUNK, SGU_CHUNK), 1))
        lane = lax.broadcasted_iota(jnp.int32, (SGU_CHUNK, LANES), 1)
        db = jnp.zeros((SGU_CHUNK, LANES), F32)
        for ch in range(ts // SGU_CHUNK):
            rows = slice(ch * SGU_CHUNK, (ch + 1) * SGU_CHUNK)
            for g in range(SGU_GROUPS):
                cols = slice(g * 128, (g + 1) * 128)
                vng = vn[rows, cols]
                s = _dot(w_ref[g], vng) + bias_ref[g]
                dyf = dy_ref[rows, cols].astype(F32)
                dz_ref[rows, cols] = (dyf * s * _gelu_grad(us[rows, cols])).astype(BF16)
                dsf = dyf * u[rows, cols]
                dsb = dsf.astype(BF16)
                dvn[rows, cols] = _dot(wt_ref[g], dsb)
                dw_ref[g] += jnp.where(causal, _dot_nt(dsb, vng), 0.0)
                db = db + jnp.where(lane == g, jnp.sum(dsf, axis=1, keepdims=True), 0.0)
        db_ref[...] += db
        dvnf = dvn[...]
        dlg_ref[...] += jnp.sum(dvnf * vh, axis=0, keepdims=True)
        dlb_ref[...] += jnp.sum(dvnf, axis=0, keepdims=True)
        dvh = dvnf * g_ref[...]
        dv = rstd * (dvh - jnp.mean(dvh, axis=-1, keepdims=True) - vh * jnp.mean(dvh * vh, axis=-1, keepdims=True))
        dz_ref[:, SGU_WIDTH:] = (dv * _gelu_grad(vs)).astype(BF16)

    full = _resident
    vec = full((1, SGU_WIDTH))
    acc = lambda shape: pl.BlockSpec(shape, lambda i: (0,) * len(shape))
    return pl.pallas_call(
        body, name=name, grid=(T // ts,),
        in_specs=[pl.BlockSpec((ts, SGU_WIDTH), lambda i: (i, 0)), pl.BlockSpec((ts, SGU_WIDTH), lambda i: (i, 1)),
                  pl.BlockSpec((ts, SGU_WIDTH), lambda i: (i, 0)), vec, vec, full(wm.shape), full(wm.shape),
                  full(bias.shape), pl.BlockSpec(memory_space=pl.ANY)],
        out_specs=[pl.BlockSpec((ts, 2 * SGU_WIDTH), lambda i: (i, 0)), acc(wm.shape),
                   acc((SGU_CHUNK, LANES)), acc((1, SGU_WIDTH)), acc((1, SGU_WIDTH))],
        out_shape=[SDS(dzmain.shape, BF16), SDS(wm.shape, F32), SDS((SGU_CHUNK, LANES), F32),
                   SDS((1, SGU_WIDTH), F32), SDS((1, SGU_WIDTH), F32)],
        scratch_shapes=[pltpu.VMEM((ts, SGU_WIDTH), F32)],
        input_output_aliases={8: 0},
        compiler_params=_params("arbitrary"),
    )(zmain, zmain, dy, ln_g, ln_b, wm, wmt, bias, dzmain)


def _merge_fwd(y_attn_t, y_sgu, zmain, w_a, w_s, w_o, x, g_post, *, name):
    T, D = x.shape
    tm = _tile(T, TM_ROW, LANES)

    def body(ya_ref, ys_ref, ga_ref, gb_ref, wa_ref, ws_ref, wo_ref, x_ref, g_ref, xn_ref, pa_ref, ps_ref, o_ref):
        pa = _dot_tn(ya_ref[...], wa_ref[...])
        ps = _dot(ys_ref[...], ws_ref[...])
        pa_ref[...] = pa.astype(BF16)
        ps_ref[...] = ps.astype(BF16)
        merged = _sigmoid(ga_ref[...].astype(F32)) * pa + _sigmoid(gb_ref[...].astype(F32)) * ps
        out = _dot(merged.astype(BF16), wo_ref[...])
        o_ref[...] = out.astype(BF16)
        _, oh = _rms_stats(out)
        xn_ref[...] = x_ref[...] + oh * g_ref[...]

    row = lambda col: pl.BlockSpec((tm, D), lambda i: (i, col))
    wfull = _resident((D, D))
    return pl.pallas_call(
        body, name=name, grid=(T // tm,),
        in_specs=[pl.BlockSpec((D, tm), lambda i: (0, i)), row(0), row(2), row(3), wfull, wfull, wfull, row(0),
                  pl.BlockSpec((1, D), lambda i: (0, 0))],
        out_specs=[row(0)] * 4,
        out_shape=[SDS((T, D), F32), SDS((T, D), BF16), SDS((T, D), BF16), SDS((T, D), BF16)],
        compiler_params=_params("parallel"),
    )(y_attn_t, y_sgu, zmain, zmain, w_a, w_s, w_o, x, g_post)


def _merge_bwd(dy, out, g_post, pa, ps, zmain, w_a, w_s, w_o, *, name):
    T, D = dy.shape
    tm = _tile(T, TM_ROW, LANES)

    def body(dy_ref, o_ref, g_ref, pa_ref, ps_ref, ga_ref, gb_ref, wa_ref, ws_ref, wo_ref,
             dz_ref, dout_ref, mg_ref, dpa_ref, dps_ref, dya_ref, dys_ref, dg_ref):
        @pl.when(pl.program_id(0) == 0)
        def _():
            dg_ref[...] = jnp.zeros_like(dg_ref)

        dout, dg = _rms_bwd(o_ref[...].astype(F32), g_ref[...], dy_ref[...])
        dg_ref[...] += dg
        doutb = dout.astype(BF16)
        dout_ref[...] = doutb
        dm = _dot_nt(doutb, wo_ref[...]).astype(BF16)
        pa, ps = pa_ref[...], ps_ref[...]
        sa = _sigmoid(ga_ref[...].astype(F32))
        sb = _sigmoid(gb_ref[...].astype(F32))
        one_minus_sa, one_minus_sb = (1.0 - sa).astype(BF16), (1.0 - sb).astype(BF16)
        sa, sb = sa.astype(BF16), sb.astype(BF16)
        mg_ref[...] = sa * pa + sb * ps
        dpa = dm * sa
        dps = dm * sb
        dpa_ref[...] = dpa
        dps_ref[...] = dps
        dz_ref[:, 0:D] = dpa * pa * one_minus_sa
        dz_ref[:, D:] = dps * ps * one_minus_sb
        dya_ref[...] = _dot_nt(wa_ref[...], dpa).astype(BF16)
        dys_ref[...] = _dot_nt(dps, ws_ref[...]).astype(BF16)

    row = lambda col: pl.BlockSpec((tm, D), lambda i: (i, col))
    wfull = _resident((D, D))
    vec = pl.BlockSpec((1, D), lambda i: (0, 0))
    act = SDS((T, D), BF16)
    return pl.pallas_call(
        body, name=name, grid=(T // tm,),
        in_specs=[row(0), row(0), vec, row(0), row(0), row(2), row(3), wfull, wfull, wfull],
        out_specs=[pl.BlockSpec((tm, 2 * D), lambda i: (i, 1))] + [row(0)] * 4
        + [pl.BlockSpec((D, tm), lambda i: (0, i)), row(0), vec],
        out_shape=[SDS(zmain.shape, BF16)] + [act] * 4 + [SDS((D, T), BF16), act, SDS((1, D), F32)],
        compiler_params=_params("arbitrary"),
    )(dy, out, g_post, pa, ps, zmain, zmain, w_a, w_s, w_o)


def _loss_head(y, target, *, name):
    T, D = y.shape
    tm = _tile(T, TM_ROW, 8)

    def body(y_ref, t_ref, dy_ref, l_ref):
        @pl.when(pl.program_id(0) == 0)
        def _():
            l_ref[...] = jnp.zeros_like(l_ref)

        e = y_ref[...] - t_ref[...]
        dy_ref[...] = e * (1.0 / D)
        l_ref[...] += jnp.sum(jnp.mean(e * e, axis=-1, keepdims=True))

    row = pl.BlockSpec((tm, D), lambda i: (i, 0))
    return pl.pallas_call(
        body, name=name, grid=(T // tm,), in_specs=[row, row],
        out_specs=[row, pl.BlockSpec((8, LANES), lambda i: (0, 0))],
        out_shape=[SDS((T, D), F32), SDS((8, LANES), F32)],
        compiler_params=_params("arbitrary"),
    )(y, target)


def _adamw(w, g, m, v, *, name):
    shape = w.shape
    cols = shape[-1]
    rows = w.size // cols
    w2, g2, m2, v2 = (t.reshape(rows, cols) for t in (w, g, m, v))
    tr = _tile(rows, max(8, (256 * 1024) // cols // 8 * 8), 8)

    def body(w_ref, g_ref, m_ref, v_ref, d_ref, nm_ref, nv_ref):
        gg = g_ref[...]
        nm = ADAM_B1 * m_ref[...] + (1.0 - ADAM_B1) * gg
        nv = ADAM_B2 * v_ref[...] + (1.0 - ADAM_B2) * (gg * gg)
        m_hat = nm / (1.0 - ADAM_B1 ** ADAM_STEP)
        v_hat = nv / (1.0 - ADAM_B2 ** ADAM_STEP)
        d_ref[...] = -ADAM_LR * (m_hat / (jnp.sqrt(v_hat) + ADAM_EPS) + ADAM_WD * w_ref[...])
        nm_ref[...] = nm
        nv_ref[...] = nv

    blk = pl.BlockSpec((tr, cols), lambda i: (i, 0))
    outs = pl.pallas_call(
        body, name=name, grid=(rows // tr,), in_specs=[blk] * 4, out_specs=[blk] * 3,
        out_shape=[SDS((rows, cols), F32)] * 3, compiler_params=_params("parallel"),
    )(w2, g2, m2, v2)
    return tuple(o.reshape(shape) for o in outs)


def _sum_terms(terms, n_rows, n_lead, dtypes, *, name, out_lead=None, out_n_lead=None):
    cols = terms[0][0].shape[-1]
    tr = _tile(n_rows, 704 if len(terms) <= 4 else 256, SUBLANES_BF16)
    nblk = n_rows // tr
    n_out = len(dtypes)

    def body(*refs):
        acc = refs[0][...].astype(F32)
        for r in refs[1:-n_out]:
            acc = acc + r[...].astype(F32)
        for o_ref in refs[-n_out:]:
            o_ref[...] = acc.astype(o_ref.dtype)

    def spec(lead, first):
        return pl.BlockSpec((1, tr, cols), lambda a, i: (lead(a), first(a) * nblk + i, 0))

    out = pl.BlockSpec((1, tr, cols), lambda a, i: (a if out_lead is None else out_lead(a), i, 0))
    return pl.pallas_call(
        body, name=name, grid=(n_lead, nblk), in_specs=[spec(lead, first) for _, lead, first in terms],
        out_specs=[out] * n_out, out_shape=[SDS((out_n_lead or n_lead, n_rows, cols), d) for d in dtypes],
        compiler_params=_params("arbitrary", "arbitrary"),
    )(*[a for a, _, _ in terms])


def _position():
    x, y, c = (lax.axis_index(a) for a in AXES)
    chips = [(1 - x, y), (x, 1 - y), (1 - x, 1 - y)]
    return x, y, c, chips


ANY = pl.BlockSpec(memory_space=pl.ANY)


def _remote(src, dst, send_sems, recv_sems, k, to):
    return pltpu.make_async_remote_copy(src_ref=src, dst_ref=dst, send_sem=send_sems.at[k], recv_sem=recv_sems.at[k],
                                        device_id=to, device_id_type=MESH)


def _comm_call(body, arrays, out_shapes, n_sems, *, name):
    n = len(arrays)

    def wrapped(*refs):
        body(refs[:n], refs[n:n + len(out_shapes)], refs[-2], refs[-1])

    return pl.pallas_call(
        wrapped, name=name, in_specs=[ANY] * n, out_specs=[ANY] * len(out_shapes), out_shape=out_shapes,
        scratch_shapes=[pltpu.SemaphoreType.DMA((n_sems,)), pltpu.SemaphoreType.DMA((n_sems,))],
    )(*arrays)


def _gather_shards(packs, *, name):
    def body(p_refs, o_refs, send_sems, recv_sems):
        for stage in _gather_stages(p_refs, o_refs, send_sems, recv_sems):
            stage()

    for p in packs:
        assert p.shape[0] % (4 * SUBLANES_BF16) == 0
    return _comm_call(body, packs, _gathered_shapes(packs), GATHER_SEMS * len(packs), name=name)


GATHER_SEMS = 8


def _gathered_shapes(packs):
    return [SDS((N_CHIPS,) + p.shape, p.dtype) for p in packs]


def _gather_stages(p_refs, o_refs, send_sems, recv_sems):
    NS = GATHER_SEMS
    x, y, c, _ = _position()
    me, sib = (x, y, c), (x, y, 1 - c)
    xn, yn = (1 - x, y, c), (x, 1 - y, c)
    s_me, s_xn, s_yn, s_dg = 2 * x + y, 2 * (1 - x) + y, 2 * x + 1 - y, 2 * (1 - x) + 1 - y

    def copy(a, k):
        p_ref, o_ref = p_refs[a], o_refs[a]
        rh = p_ref.shape[0] // 2
        rq = rh // 2
        half, q0, q1 = pl.ds(c * rh, rh), pl.ds(c * rh, rq), pl.ds(c * rh + rq, rq)
        src, dst, to = [(p_ref.at[half], o_ref.at[s_me, half], xn), (p_ref.at[half], o_ref.at[s_me, half], yn),
                        (o_ref.at[s_xn, q0],) * 2 + (yn,), (o_ref.at[s_yn, q1],) * 2 + (xn,),
                        (o_ref.at[s_xn, half],) * 2 + (sib,), (o_ref.at[s_yn, half],) * 2 + (sib,),
                        (o_ref.at[s_dg, q0],) * 2 + (sib,), (o_ref.at[s_dg, q1],) * 2 + (sib,)][k]
        return _remote(src, dst, send_sems, recv_sems, NS * a + k, to)

    def landed(a, k):
        o_ref = o_refs[a]
        rh = o_ref.shape[1] // 2
        rq = rh // 2
        o = (1 - c) * rh
        dst = [o_ref.at[s_xn, pl.ds(c * rh, rh)], o_ref.at[s_yn, pl.ds(c * rh, rh)],
               o_ref.at[s_dg, pl.ds(c * rh, rq)], o_ref.at[s_dg, pl.ds(c * rh + rq, rq)],
               o_ref.at[s_xn, pl.ds(o, rh)], o_ref.at[s_yn, pl.ds(o, rh)],
               o_ref.at[s_dg, pl.ds(o, rq)], o_ref.at[s_dg, pl.ds(o + rq, rq)]][k]
        _remote(dst, dst, send_sems, recv_sems, NS * a + k, me).wait_recv()

    n = len(p_refs)

    def stage_a():
        for a in range(n):
            for k in (0, 1):
                copy(a, k).start()

    def stage_b():
        for a in range(n):
            landed(a, 0)
            copy(a, 2).start()
            copy(a, 4).start()
            landed(a, 1)
            copy(a, 3).start()
            copy(a, 5).start()

    def stage_c():
        for a in range(n):
            landed(a, 2)
            copy(a, 6).start()
            landed(a, 3)
            copy(a, 7).start()
        for a in range(n):
            for k in (4, 5, 6, 7):
                landed(a, k)
        for a in range(n):
            for k in range(NS):
                copy(a, k).wait_send()

    return stage_a, stage_b, stage_c


def _sibling_exchange(gs, *, name):
    def body(g_refs, o_refs, send_sems, recv_sems):
        sent = _exchange_copies(g_refs, o_refs, send_sems, recv_sems)
        for cp in sent:
            cp.start()
        for cp in sent:
            cp.wait()

    return _comm_call(body, gs, _exchange_shapes(gs), len(gs), name=name)


def _exchange_shapes(gs):
    return [SDS((N_CHIPS, g.shape[1] // 2, g.shape[2]), g.dtype) for g in gs]


def _exchange_copies(g_refs, o_refs, send_sems, recv_sems):
    x, y, c, _ = _position()
    return [_remote(g_ref.at[:, pl.ds((1 - c) * o_ref.shape[1], o_ref.shape[1])], o_ref, send_sems, recv_sems, a,
                    (x, y, 1 - c)) for a, (g_ref, o_ref) in enumerate(zip(g_refs, o_refs))]


def _scatter_hop1(ps, *, name):
    def body(p_refs, o_refs, send_sems, recv_sems):
        sent = _hop1_copies(p_refs, o_refs, send_sems, recv_sems)
        for cp in sent:
            cp.start()
        for cp in sent:
            cp.wait()

    return _comm_call(body, ps, _hop1_shapes(ps), 4 * len(ps), name=name)


def _hop1_shapes(ps):
    return [SDS((4, p.shape[1] // 2, p.shape[2]), p.dtype) for p in ps]


def _hop1_copies(p_refs, o_refs, send_sems, recv_sems):
    x, y, c, _ = _position()
    xn, yn = (1 - x, y, c), (x, 1 - y, c)
    s_xn, s_yn, s_dg = 2 * (1 - x) + y, 2 * x + 1 - y, 2 * (1 - x) + 1 - y
    sent = []
    for a, (p_ref, o_ref) in enumerate(zip(p_refs, o_refs)):
        rq = o_ref.shape[1]
        first, second = pl.ds(0, rq), pl.ds(rq, rq)
        sent += [_remote(p_ref.at[s_xn, second], o_ref.at[0], send_sems, recv_sems, 4 * a, xn),
                 _remote(p_ref.at[s_dg, second], o_ref.at[1], send_sems, recv_sems, 4 * a + 1, xn),
                 _remote(p_ref.at[s_yn, first], o_ref.at[2], send_sems, recv_sems, 4 * a + 2, yn),
                 _remote(p_ref.at[s_dg, first], o_ref.at[3], send_sems, recv_sems, 4 * a + 3, yn)]
    return sent


RIDES = {"exchange": (_exchange_copies, _exchange_shapes, 1), "hop1": (_hop1_copies, _hop1_shapes, 4)}


def _scatter_hop2(fs, *, name):
    return _exchanges([("hop2", fs)], name=name)[0]


def _hop2_copies(f_refs, o_refs, send_sems, recv_sems):
    x, y, c, _ = _position()
    sent = []
    for a, (f_ref, o_ref) in enumerate(zip(f_refs, o_refs)):
        sent += [_remote(f_ref.at[0], o_ref.at[0], send_sems, recv_sems, 2 * a, (1 - x, y, c)),
                 _remote(f_ref.at[1], o_ref.at[1], send_sems, recv_sems, 2 * a + 1, (x, 1 - y, c))]
    return sent


RIDES["hop2"] = (_hop2_copies, lambda fs: [SDS(f.shape, f.dtype) for f in fs], 2)


class _SemaphoresFrom:
    def __init__(self, sems, first):
        self.sems, self.first, self.at = sems, first, self

    def __getitem__(self, k):
        return self.sems.at[self.first + k]


def _exchanges(parts, *, name):
    counts = [len(arrays) for _, arrays in parts]
    sems = [RIDES[kind][2] * n for (kind, _), n in zip(parts, counts)]
    n_in = sum(counts)

    def body(*refs):
        sent, a0, s0 = [], 0, 0
        for (kind, _), n, ns in zip(parts, counts, sems):
            sent += RIDES[kind][0](refs[a0:a0 + n], refs[n_in + a0:n_in + a0 + n],
                                   _SemaphoresFrom(refs[-2], s0), _SemaphoresFrom(refs[-1], s0))
            a0, s0 = a0 + n, s0 + ns
        for cp in sent:
            cp.start()
        for cp in sent:
            cp.wait()

    arrays = [a for _, arrs in parts for a in arrs]
    shapes = [s for kind, arrs in parts for s in RIDES[kind][1](arrs)]
    out = pl.pallas_call(
        body, name=name, in_specs=[ANY] * n_in, out_specs=[ANY] * n_in, out_shape=shapes,
        scratch_shapes=[pltpu.SemaphoreType.DMA((sum(sems),))] * 2,
    )(*arrays)
    return [out[sum(counts[:i]):sum(counts[:i + 1])] for i in range(len(parts))]


def _sibling_fill(rs, *, name):
    n = len(rs)

    def body(*refs):
        r_refs, send_sems, recv_sems = refs[n:2 * n], refs[-2], refs[-1]
        x, y, c, _ = _position()
        sent = []
        for a, r_ref in enumerate(r_refs):
            mine = r_ref.at[pl.ds(c * (r_ref.shape[0] // 2), r_ref.shape[0] // 2)]
            sent.append(_remote(mine, mine, send_sems, recv_sems, a, (x, y, 1 - c)))
            sent[-1].start()
        for a, r_ref in enumerate(r_refs):
            theirs = r_ref.at[pl.ds((1 - c) * (r_ref.shape[0] // 2), r_ref.shape[0] // 2)]
            _remote(theirs, theirs, send_sems, recv_sems, a, (x, y, c)).wait_recv()
        for cp in sent:
            cp.wait_send()

    return pl.pallas_call(
        body, name=name, in_specs=[ANY] * n, out_specs=[ANY] * n, out_shape=[SDS(r.shape, r.dtype) for r in rs],
        input_output_aliases={i: i for i in range(n)},
        scratch_shapes=[pltpu.SemaphoreType.DMA((n,)), pltpu.SemaphoreType.DMA((n,))],
    )(*rs)


def _gather_all(v, *, name):
    M, C = v.shape

    def body(v_ref, o_ref, send_sems, recv_sems):
        x, y, c, chips = _position()
        slot = lambda px, py, pc: o_ref.at[4 * px + 2 * py + pc]
        first = [_remote(v_ref, slot(x, y, c), send_sems, recv_sems, 0, (x, y, 1 - c))]
        first += [_remote(v_ref, slot(x, y, c), send_sems, recv_sems, 1 + j, (*chip, c)) for j, chip in enumerate(chips)]
        for cp in first:
            cp.start()
        passed = []
        for j, chip in enumerate(chips):
            landed = slot(*chip, c)
            _remote(landed, landed, send_sems, recv_sems, 1 + j, (x, y, c)).wait_recv()
            cp = _remote(landed, landed, send_sems, recv_sems, 4 + j, (x, y, 1 - c))
            cp.start()
            passed.append(cp)
        sib = slot(x, y, 1 - c)
        _remote(sib, sib, send_sems, recv_sems, 0, (x, y, c)).wait_recv()
        for j, chip in enumerate(chips):
            theirs = slot(*chip, 1 - c)
            _remote(theirs, theirs, send_sems, recv_sems, 4 + j, (x, y, c)).wait_recv()
        for cp in first + passed:
            cp.wait_send()

    return pl.pallas_call(
        body, name=name, in_specs=[ANY], out_specs=ANY, out_shape=SDS((N_DEV, M, C), v.dtype),
        scratch_shapes=[pltpu.SemaphoreType.DMA((7,)), pltpu.SemaphoreType.DMA((7,))],
    )(v)


BIG = ("ffn1_w1", "ffn2_w1", "w_in", "ffn1_w2", "ffn2_w2", "w_attn_branch", "w_sgu_branch", "w_out")
COL_SHARDED = ("ffn1_w1", "w_in", "ffn2_w1")
FFN_IN = ("ffn1_w1", "ffn2_w1")
SMALL = ("ffn1_pre_g", "ffn1_post_g", "mix_pre_g", "attn_sinks", "sgu_ln_g", "sgu_ln_b", "sgu_w", "sgu_b",
         "mix_post_g", "ffn2_pre_g", "ffn2_post_g")
WEIGHTS = ("ffn1_pre_g", "ffn1_w1", "ffn1_w2", "ffn1_post_g", "mix_pre_g", "w_in", "attn_sinks", "sgu_ln_g",
           "sgu_ln_b", "sgu_w", "sgu_b", "w_attn_branch", "w_sgu_branch", "w_out", "mix_post_g", "ffn2_pre_g",
           "ffn2_w1", "ffn2_w2", "ffn2_post_g")


def _column_chunks(w, tn):
    return jnp.swapaxes(w.reshape(w.shape[0], w.shape[1] // tn, tn), 0, 1)


def _width_classes(shard_shapes):
    widths = sorted({shard_shapes[n][-1] for n in BIG}, reverse=True)
    return [[n for n in BIG if shard_shapes[n][-1] == w] for w in widths]


def _class_rows(classes, shard_shapes, n_layers, aligned=BIG):
    where = {}
    for k, names in enumerate(classes):
        off = 0
        for n in names:
            r = shard_shapes[n][0]
            assert off % r == 0 or n not in aligned
            where[n] = (k, off, r)
            off += n_layers * r
    return where


def _ffn_fwd(x, pre_g, w1, w1_block, w2, post_g, tag, gather=()):
    a, h, *gathered = _norm_matmul(x, pre_g, w1, w1_block, name=f"{tag}_up", with_h=True, gather=gather)
    xn, o = _swiglu_out(a, w2, x, post_g, name=f"{tag}_down")
    return xn, (x, h, a, o), gathered


def _ffn_bwd(dy, saved, pre_g, w1, w1_block, w2, post_g, dw1_into, dw2_into, tag, after_dw2=None):
    x, h, a, o = saved
    da, do, d_post = _ffn_bwd_hidden(dy, o, post_g, a, w2, name=f"{tag}_bwd_hidden")
    g2 = _dw_rows(a, do, *dw2_into, name=f"{tag}_dw2", swiglu=True)
    ride_in, next_ride = after_dw2(g2) if after_dw2 else (None, None)
    dx, d_pre, *rode = _matmul_nt_norm_bwd(da, w1, w1_block, x, pre_g, dy, None, name=f"{tag}_bwd_in", ride=ride_in)
    if after_dw2 is None:
        return dx, _dw_cols(h, da, N_CHIPS, *dw1_into, name=f"{tag}_dw1"), g2, d_pre, d_post, ()
    g1, *rode = _dw_cols(h, da, N_CHIPS, *dw1_into, name=f"{tag}_dw1", ride=next_ride(rode))
    return dx, g1, g2, d_pre, d_post, rode


def kernel(x, ffn1_pre_g, ffn1_w1, ffn1_w2, ffn1_post_g, mix_pre_g, w_in, attn_sinks, sgu_ln_g, sgu_ln_b, sgu_w, sgu_b, w_attn_branch, w_sgu_branch, w_out, mix_post_g, ffn2_pre_g, ffn2_w1, ffn2_w2, ffn2_post_g, loss_target, m_ffn1_pre_g, m_ffn1_w1, m_ffn1_w2, m_ffn1_post_g, m_mix_pre_g, m_w_in, m_attn_sinks, m_sgu_ln_g, m_sgu_ln_b, m_sgu_w, m_sgu_b, m_w_attn_branch, m_w_sgu_branch, m_w_out, m_mix_post_g, m_ffn2_pre_g, m_ffn2_w1, m_ffn2_w2, m_ffn2_post_g, v_ffn1_pre_g, v_ffn1_w1, v_ffn1_w2, v_ffn1_post_g, v_mix_pre_g, v_w_in, v_attn_sinks, v_sgu_ln_g, v_sgu_ln_b, v_sgu_w, v_sgu_b, v_w_attn_branch, v_w_sgu_branch, v_w_out, v_mix_post_g, v_ffn2_pre_g, v_ffn2_w1, v_ffn2_w2, v_ffn2_post_g):
    given = dict(locals())
    W = {n: given[n] for n in WEIGHTS}
    M = {n: given["m_" + n] for n in WEIGHTS}
    V = {n: given["v_" + n] for n in WEIGHTS}
    L = ffn1_w1.shape[0]
    T, D = x.shape[1], x.shape[2]
    xt = x.reshape(T, D)
    target = loss_target.reshape(T, D)
    assert L % 2 == 0 and D == ATTN_WIDTH == SGU_WIDTH and T % ATTN_BLOCK == 0

    shard_shapes = {n: W[n].shape[1:] for n in BIG}
    classes = _width_classes(shard_shapes)
    my_chip = 2 * lax.axis_index("x") + lax.axis_index("y")
    my_core = lax.axis_index("c")
    where = _class_rows(classes, shard_shapes, L)
    where_w = _class_rows(classes, shard_shapes, 1, aligned=FFN_IN)
    packs = [[jnp.concatenate([W[n][l].astype(BF16) for n in names], axis=0) for names in classes] for l in range(L)]

    def block_of(n, l):
        k, off, r = where[n]
        return k, off // r + l

    def layer_weights(l, got):
        wc = [lax.dynamic_update_slice(g, p[None], (my_chip, 0, 0)) for g, p in zip(got, packs[l])]

        def chip_shards(n):
            k, off, r = where_w[n]
            return wc[k][:, off:off + r, :]

        fw = {n: chip_shards(n).reshape(-1, D) for n in BIG if n not in COL_SHARDED}
        w_in_l = jnp.swapaxes(chip_shards("w_in"), 0, 1).reshape(D, -1)
        fw["w_qkv_t"] = w_in_l[:, :QKV_WIDTH].T
        fw["w_main"] = _column_chunks(w_in_l[:, QKV_WIDTH:], D)
        for n in FFN_IN:
            fw[n] = (wc[where_w[n][0]], where_w[n][1] // where_w[n][2])
        return fw

    full = [None] * L
    full[0] = layer_weights(0, _gather_shards(packs[0], name="gather_weights_l0"))

    row = lambda name, l: W[name][l].reshape(1, -1)
    causal = jnp.tril(jnp.ones((SGU_CHUNK, SGU_CHUNK), dtype=bool))
    saved = []
    h_cur = xt
    for l in range(L):
        fw = full[l]
        sv = {}
        h_cur, sv["ffn1"], got = _ffn_fwd(h_cur, row("ffn1_pre_g", l), *fw["ffn1_w1"], fw["ffn1_w2"],
                                          row("ffn1_post_g", l), f"l{l}_ffn1", gather=packs[l + 1] if l + 1 < L else ())
        if l + 1 < L:
            full[l + 1] = layer_weights(l + 1, got)
        zqkv, hm = _norm_matmul_t(h_cur, row("mix_pre_g", l), fw["w_qkv_t"], name=f"l{l}_mix_in_qkv")
        zmain, = _norm_matmul(h_cur, row("mix_pre_g", l), fw["w_main"], 0, name=f"l{l}_mix_in_main", with_h=False)
        wm = jnp.where(causal[None], sgu_w[l], 0.0).astype(BF16)
        wmt = jnp.swapaxes(wm, 1, 2)
        bias = jnp.broadcast_to(sgu_b[l][:, :, None], (SGU_GROUPS, SGU_CHUNK, 128)).astype(F32)
        y_attn = _attn_fwd(zqkv, attn_sinks[l], name=f"l{l}_attn")
        y_sgu = _sgu_fwd(zmain, row("sgu_ln_g", l), row("sgu_ln_b", l), wm, bias, name=f"l{l}_sgu")
        x_mix = h_cur
        h_cur, pa, ps, mo = _merge_fwd(y_attn, y_sgu, zmain, fw["w_attn_branch"], fw["w_sgu_branch"], fw["w_out"],
                                       x_mix, row("mix_post_g", l), name=f"l{l}_merge")
        sv["mix"] = (x_mix, hm, zqkv, zmain, y_attn, y_sgu, pa, ps, mo, wm, wmt, bias)
        h_cur, sv["ffn2"], _ = _ffn_fwd(h_cur, row("ffn2_pre_g", l), *fw["ffn2_w1"], fw["ffn2_w2"],
                                        row("ffn2_post_g", l), f"l{l}_ffn2")
        saved.append(sv)

    dy, lsum = _loss_head(h_cur, target, name="loss_head")
    loss = lax.psum(0.5 * lsum[0, 0], AXES)

    k_in = where["w_in"][0]
    assert classes[k_in] == ["w_in"]
    gcls = [None if k == k_in else lax.empty((N_CHIPS, L * p.shape[0], p.shape[1]), F32)
            for k, p in enumerate(packs[0])]
    dw_in = [None] * L
    small_grads = [None] * L

    def into(n, l):
        return gcls[block_of(n, l)[0]], block_of(n, l)[1]

    def ffn_bwd(dy, which, l, after_dw2=None):
        n1, n2 = f"{which}_w1", f"{which}_w2"
        dy, g1, g2, d_pre, d_post, rode = _ffn_bwd(
            dy, saved[l][which], row(f"{which}_pre_g", l), *full[l][n1], full[l][n2], row(f"{which}_post_g", l),
            into(n1, l), into(n2, l), f"l{l}_{which}", after_dw2)
        gcls[where[n1][0]], gcls[where[n2][0]] = g1, g2
        return dy, d_pre, d_post, rode

    zero = lambda a: 0
    own = lambda a: a
    core = lambda a: lax.axis_index("c")
    chip = lambda a: 2 * lax.axis_index("x") + lax.axis_index("y")
    chip_xn = lambda a: 2 * (1 - lax.axis_index("x")) + lax.axis_index("y")
    chip_yn = lambda a: 2 * lax.axis_index("x") + 1 - lax.axis_index("y")

    def pair_sum(k, g, fs):
        return _sum_terms([(g, own, core), (fs, own, zero)], fs.shape[1], N_CHIPS, (BF16,), name=f"grads_pair_sum{k}")[0]

    for l in reversed(range(L)):
        fw, sv = full[l], saved[l]
        gs = {}
        dy, gs["ffn2_pre_g"], gs["ffn2_post_g"], _ = ffn_bwd(dy, "ffn2", l)

        x_mix, hm, zqkv, zmain, y_attn, y_sgu, pa, ps, mo, wm, wmt, bias = sv["mix"]
        dzmain, dout, merged, dpa, dps, dya, dys, gs["mix_post_g"] = _merge_bwd(
            dy, mo, row("mix_post_g", l), pa, ps, zmain, fw["w_attn_branch"], fw["w_sgu_branch"], fw["w_out"],
            name=f"l{l}_merge_bwd")
        k_sq = where["w_out"][0]
        gcls[k_sq] = _dw_rows(merged, dout, *into("w_out", l), name=f"l{l}_dw_out")
        gcls[k_sq] = _dw_rows(y_attn, dpa, *into("w_attn_branch", l), name=f"l{l}_dw_attn", a_feature_major=True)
        gcls[k_sq] = _dw_rows(y_sgu, dps, *into("w_sgu_branch", l), name=f"l{l}_dw_sgu")
        dzqkv, dsink = _attn_bwd(zqkv, attn_sinks[l], dya, name=f"l{l}_attn_bwd")
        gs["attn_sinks"] = dsink[0, :N_Q_HEADS]
        dzmain, dsw, dsb, gs["sgu_ln_g"], gs["sgu_ln_b"] = _sgu_bwd(
            zmain, dzmain, dys, row("sgu_ln_g", l), row("sgu_ln_b", l), wm, wmt, bias, name=f"l{l}_sgu_bwd")
        gs["sgu_w"] = dsw
        gs["sgu_b"] = dsb[:, :SGU_GROUPS].T
        dh_qkv = _matmul_tn_rows(dzqkv, fw["w_qkv_t"], name=f"l{l}_mix_bwd_qkv")
        dy, gs["mix_pre_g"] = _matmul_nt_norm_bwd(dzmain, fw["w_main"], 0, x_mix, row("mix_pre_g", l), dy, dh_qkv,
                                                   name=f"l{l}_mix_bwd_in")
        dw_main = _dw_cols(hm, dzmain, zmain.shape[1] // D, None, 0, name=f"l{l}_dw_in_main")
        dw_in[l] = jnp.concatenate([_matmul_tokens(dzqkv, hm, name=f"l{l}_dw_in_qkv").T,
                                    jnp.swapaxes(dw_main, 0, 1).reshape(D, -1)], axis=1)

        if l > 0:
            dy, gs["ffn1_pre_g"], gs["ffn1_post_g"], _ = ffn_bwd(dy, "ffn1", l)
        else:
            w_in_width = shard_shapes["w_in"][1]
            gcls[k_in] = jnp.stack([jnp.concatenate([g[:, s * w_in_width:(s + 1) * w_in_width] for g in dw_in], axis=0)
                                    for s in range(N_CHIPS)])
            k_up, k_down = where["ffn1_w1"][0], where["ffn1_w2"][0]
            assert sorted((k_up, k_down, k_in)) == list(range(len(classes)))
            from_sibling, pairs, hop1 = ([None] * len(classes) for _ in range(3))

            def after_dw2(g_down):
                def hop1_ride(rode):
                    from_sibling[k_down], from_sibling[k_in] = rode
                    pairs[k_down] = pair_sum(k_down, g_down, rode[0])
                    pairs[k_in] = pair_sum(k_in, gcls[k_in], rode[1])
                    return "hop1", [pairs[k_down], pairs[k_in]]
                return ("exchange", [g_down, gcls[k_in]]), hop1_ride

            dy, gs["ffn1_pre_g"], gs["ffn1_post_g"], (hop1[k_down], hop1[k_in]) = ffn_bwd(dy, "ffn1", l, after_dw2)
        small_grads[l] = gs
    grad_x = dy.reshape(x.shape)

    grs = gcls

    def relay_sum(k):
        return _sum_terms([(pairs[k], lambda a: chip_xn(a) + a * (chip_yn(a) - chip_xn(a)), own),
                           (hop1[k], lambda a: 3 - 2 * a, zero)], hop1[k].shape[1], 2, (BF16,),
                          name=f"grads_relay_sum{k}")[0]

    hop2 = [None] * len(classes)
    (hop2[k_down], hop2[k_in]), (from_sibling[k_up],) = _exchanges(
        [("hop2", [relay_sum(k_down), relay_sum(k_in)]), ("exchange", [grs[k_up]])], name="grads_hop2_and_exchange")
    pairs[k_up] = pair_sum(k_up, grs[k_up], from_sibling[k_up])
    hop1[k_up], = _scatter_hop1([pairs[k_up]], name="grads_scatter_hop1")
    hop2[k_up], = _scatter_hop2([relay_sum(k_up)], name="grads_scatter_hop2")
    quarter = lambda a: 2 * core(a) + a
    halves = [_sum_terms([(g, chip, quarter), (fs, chip, own), (h1, lambda a: 2 - 2 * a, zero), (h2, own, zero)],
                         h1.shape[1], 2, (F32,), name=f"grads_chip_sum{k}", out_lead=quarter, out_n_lead=4)[0]
              .reshape(g.shape[1:]) for k, (g, fs, h1, h2) in enumerate(zip(grs, from_sibling, hop1, hop2))]
    reduced_all = _sibling_fill(halves, name="grads_sibling_fill")

    grads = {n: [None] * L for n in SMALL}
    for names, reduced in zip(classes, reduced_all):
        for n in names:
            _, off, r = where[n]
            grads[n] = reduced[off:off + L * r].reshape((L,) + shard_shapes[n])

    def small_rows(gs):
        parts = []
        for n in SMALL:
            flat = gs[n].reshape(-1)
            pad = (-flat.shape[0]) % D
            parts.append(jnp.pad(flat, (0, pad)).reshape(-1, D))
        return jnp.concatenate(parts, axis=0)

    spack = jnp.concatenate([small_rows(small_grads[l]) for l in range(L)], axis=0)
    n_small = spack.shape[0]
    pad_rows = (-n_small) % SUBLANES_BF16
    spack = jnp.pad(spack, ((0, pad_rows), (0, 0)))
    everyone = _gather_all(spack, name="small_grads_gather")
    is_me = (jnp.arange(N_DEV) == 2 * my_chip + my_core)[:, None, None]
    everyone = jnp.where(is_me, spack[None], everyone)
    ssum = _sum_terms([(everyone, (lambda a, d=d: d), zero) for d in range(N_DEV)], spack.shape[0], 1, (F32,),
                      name="small_grads_sum")[0][0]
    per_layer = n_small // L
    for l in range(L):
        r0 = l * per_layer
        for n in SMALL:
            shp = W[n].shape[1:]
            size = math.prod(shp)
            nr = -(-size // D)
            grads[n][l] = ssum[r0:r0 + nr].reshape(-1)[:size].reshape(shp)
            r0 += nr
    grads.update({n: jnp.stack(grads[n]) for n in SMALL})

    delta, new_m, new_v = {}, {}, {}
    for n in WEIGHTS:
        delta[n], new_m[n], new_v[n] = _adamw(W[n], grads[n], M[n], V[n], name=f"adamw_{n}")

    return (loss, grad_x, *[grads[n] for n in WEIGHTS], *[delta[n] for n in WEIGHTS],
            *[new_m[n] for n in WEIGHTS], *[new_v[n] for n in WEIGHTS])
```

```python
import functools
import math

import jax
import jax.numpy as jnp
from jax import lax
from jax.experimental import pallas as pl
from jax.experimental.pallas import tpu as pltpu

F32, BF16 = jnp.float32, jnp.bfloat16
SDS = jax.ShapeDtypeStruct
MESH = pl.DeviceIdType.MESH
AXES = ("x", "y", "c")

HEAD_DIM = 64
N_Q_HEADS = 16
N_KV_HEADS = 2
Q_PER_KV = N_Q_HEADS // N_KV_HEADS
ATTN_WIDTH = N_Q_HEADS * HEAD_DIM
KV_WIDTH = N_KV_HEADS * HEAD_DIM
ATTN_BLOCK = 128
SGU_CHUNK = 128
SGU_GROUPS = 8
SGU_WIDTH = SGU_GROUPS * 128
QKV_WIDTH = ATTN_WIDTH + 2 * KV_WIDTH
RMS_EPS = 1e-6
LN_EPS = 1e-5
MASK_VALUE = -1e30
ATTN_SCALE = 1.0 / math.sqrt(HEAD_DIM)
assert math.frexp(ATTN_SCALE)[0] == 0.5

ADAM_LR, ADAM_B1, ADAM_B2, ADAM_EPS, ADAM_WD, ADAM_STEP = 0.001, 0.9, 0.999, 1e-08, 0.01, 10

N_CHIPS = 4
N_DEV = 8

VMEM_LIMIT_BYTES = 56 * 1024 * 1024
LANES = 128
SUBLANES_BF16 = 16

TM_NORM_MATMUL = 1024
TM_ROW = 512
TM_FFN_BWD = 512
TT_REDUCE = 1024
TQ_ATTN = 1024
TM_FEATURE_MAJOR = 1024
TS_SGU = 512


def _tile(n, pref, mult):
    t = (min(pref, n) // mult) * mult
    while t >= mult:
        if n % t == 0:
            return t
        t -= mult
    return n


def _params(*sem):
    return pltpu.CompilerParams(dimension_semantics=sem, vmem_limit_bytes=VMEM_LIMIT_BYTES)


def _dot(a, b):
    return jnp.dot(a, b, preferred_element_type=F32)


def _dot_nt(a, b):
    return lax.dot_general(a, b, (((1,), (1,)), ((), ())), preferred_element_type=F32)


def _dot_tn(a, b):
    return lax.dot_general(a, b, (((0,), (0,)), ((), ())), preferred_element_type=F32)


def _sigmoid(x):
    return 0.5 * (1.0 + jnp.tanh(0.5 * x))


def _rms_stats(xf):
    r = lax.rsqrt(jnp.mean(xf * xf, axis=-1, keepdims=True) + RMS_EPS)
    return r, xf * r


def _rms_bwd(xf, g, dy):
    r, xh = _rms_stats(xf)
    dyg = dy * g
    dx = r * (dyg - xh * jnp.mean(dyg * xh, axis=-1, keepdims=True))
    return dx, jnp.sum(dy * xh, axis=0, keepdims=True)


def _gelu_parts(x):
    cdf = 0.5 * (1.0 + lax.erf(x * (1.0 / math.sqrt(2.0))))
    return cdf


def _gelu(x):
    return x * _gelu_parts(x)


def _gelu_grad(x):
    return _gelu_parts(x) + x * jnp.exp(-0.5 * x * x) * (1.0 / math.sqrt(2.0 * math.pi))


def _resident(shape, index=None):
    index = (0,) * len(shape) if index is None else index
    return pl.BlockSpec(shape, lambda *_: index, pipeline_mode=pl.Buffered(1))


def _norm_matmul(x, g, w3, w_block, *, name, with_h, gather=()):
    T, D = x.shape
    nj, _, tn = w3.shape
    tm = _tile(T, TM_NORM_MATMUL, SUBLANES_BF16)
    ni, ng = T // tm, len(gather)
    n_out = 1 + with_h

    def body(x_ref, g_ref, w_ref, *rest):
        a_ref, h_sc = rest[ng], rest[ng + n_out + ng]
        i, j = pl.program_id(0), pl.program_id(1)
        if ng:
            stages = _gather_stages(rest[:ng], rest[ng + n_out:ng + n_out + ng], *rest[-2:])
            for stage, (si, sj) in zip(stages[:2], ((0, 0), (ni // 2, 0))):
                pl.when((i == si) & (j == sj))(stage)

        @pl.when(j == 0)
        def _():
            _, xh = _rms_stats(x_ref[...])
            h = (xh * g_ref[...]).astype(BF16)
            h_sc[...] = h
            if with_h:
                rest[ng + 1][...] = h

        a_ref[...] = _dot(h_sc[...], w_ref[j]).astype(BF16)
        if ng:
            pl.when((i == ni - 1) & (j == nj - 1))(stages[2])

    out_specs = [pl.BlockSpec((tm, tn), lambda i, j: (i, j))]
    out_shape = [SDS((T, nj * tn), BF16)]
    if with_h:
        out_specs.append(pl.BlockSpec((tm, D), lambda i, j: (i, 0)))
        out_shape.append(SDS((T, D), BF16))
    scratch = [pltpu.VMEM((tm, D), BF16)]
    if ng:
        scratch += [pltpu.SemaphoreType.DMA((GATHER_SEMS * ng,))] * 2
    return pl.pallas_call(
        body, name=name, grid=(ni, nj),
        in_specs=[pl.BlockSpec((tm, D), lambda i, j: (i, 0)),
                  pl.BlockSpec((1, D), lambda i, j: (0, 0)),
                  _resident((nj, D, tn), (0, w_block, 0))] + [ANY] * ng,
        out_specs=out_specs + [ANY] * ng, out_shape=out_shape + _gathered_shapes(gather),
        scratch_shapes=scratch,
        compiler_params=_params(*(("arbitrary", "arbitrary") if ng else ("parallel", "arbitrary"))),
    )(x, g, w3, *gather)


def _norm_matmul_t(x, g, wt, *, name):
    T, D = x.shape
    N = wt.shape[0]
    tm = _tile(T, TM_FEATURE_MAJOR, LANES)

    def body(x_ref, g_ref, w_ref, a_ref, h_ref):
        _, xh = _rms_stats(x_ref[...])
        h = (xh * g_ref[...]).astype(BF16)
        h_ref[...] = h
        a_ref[...] = _dot_nt(w_ref[...], h).astype(BF16)

    return pl.pallas_call(
        body, name=name, grid=(T // tm,),
        in_specs=[pl.BlockSpec((tm, D), lambda i: (i, 0)), pl.BlockSpec((1, D), lambda i: (0, 0)),
                  _resident((N, D))],
        out_specs=[pl.BlockSpec((N, tm), lambda i: (0, i)), pl.BlockSpec((tm, D), lambda i: (i, 0))],
        out_shape=[SDS((N, T), BF16), SDS((T, D), BF16)],
        compiler_params=_params("parallel"),
    )(x, g, wt)


def _matmul_tokens(at, b, *, name):
    K, T = at.shape
    N = b.shape[1]
    tt = _tile(T, TT_REDUCE, LANES)

    def body(a_ref, b_ref, o_ref):
        @pl.when(pl.program_id(0) == 0)
        def _():
            o_ref[...] = jnp.zeros_like(o_ref)

        o_ref[...] += _dot(a_ref[...], b_ref[...])

    return pl.pallas_call(
        body, name=name, grid=(T // tt,),
        in_specs=[pl.BlockSpec((K, tt), lambda t: (0, t)), pl.BlockSpec((tt, N), lambda t: (t, 0))],
        out_specs=pl.BlockSpec((K, N), lambda t: (0, 0)),
        out_shape=SDS((K, N), F32),
        compiler_params=_params("arbitrary"),
    )(at, b)


def _matmul_tn_rows(dat, wt, *, name):
    N, T = dat.shape
    D = wt.shape[1]
    tm = _tile(T, TM_FEATURE_MAJOR, LANES)

    def body(da_ref, w_ref, o_ref):
        o_ref[...] = _dot_tn(da_ref[...], w_ref[...])

    return pl.pallas_call(
        body, name=name, grid=(T // tm,),
        in_specs=[pl.BlockSpec((N, tm), lambda i: (0, i)), _resident((N, D))],
        out_specs=pl.BlockSpec((tm, D), lambda i: (i, 0)),
        out_shape=SDS((T, D), F32),
        compiler_params=_params("parallel"),
    )(dat, wt)


def _ff_chunk(F):
    return F if F <= 1408 else F // 2


def _swiglu_out(a, w2, x, g_post, *, name):
    T, F2 = a.shape
    F = F2 // 2
    D = x.shape[1]
    tm = _tile(T, TM_ROW, SUBLANES_BF16)
    fc = _ff_chunk(F)

    def body(a_ref, w_ref, x_ref, g_ref, xn_ref, o_ref):
        acc = None
        for c0 in range(0, F, fc):
            gt = a_ref[:, c0:c0 + fc].astype(F32)
            s = (gt * _sigmoid(gt)).astype(BF16) * a_ref[:, F + c0:F + c0 + fc]
            part = _dot(s, w_ref[c0:c0 + fc, :])
            acc = part if acc is None else acc + part
        o_ref[...] = acc.astype(BF16)
        _, oh = _rms_stats(acc)
        xn_ref[...] = x_ref[...] + 0.5 * (oh * g_ref[...])

    return pl.pallas_call(
        body, name=name, grid=(T // tm,),
        in_specs=[pl.BlockSpec((tm, F2), lambda i: (i, 0)),
                  _resident((F, D)),
                  pl.BlockSpec((tm, D), lambda i: (i, 0)),
                  pl.BlockSpec((1, D), lambda i: (0, 0))],
        out_specs=[pl.BlockSpec((tm, D), lambda i: (i, 0)), pl.BlockSpec((tm, D), lambda i: (i, 0))],
        out_shape=[SDS((T, D), F32), SDS((T, D), BF16)],
        compiler_params=_params("parallel"),
    )(a, w2, x, g_post)


def _ffn_bwd_hidden(dy, o, g_post, a, w2, *, name):
    T, F2 = a.shape
    F = F2 // 2
    D = dy.shape[1]
    tm = _tile(T, TM_FFN_BWD, SUBLANES_BF16)
    fc = _tile(F, 256, LANES)

    def body(dy_ref, o_ref, g_ref, a_ref, w_ref, da_ref, do_ref, dg_ref):
        @pl.when(pl.program_id(0) == 0)
        def _():
            dg_ref[...] = jnp.zeros_like(dg_ref)

        do, dg = _rms_bwd(o_ref[...].astype(F32), g_ref[...], 0.5 * dy_ref[...])
        dg_ref[...] += dg
        dob = do.astype(BF16)
        do_ref[...] = dob
        for c0 in range(0, F, fc):
            ds = _dot_nt(dob, w_ref[c0:c0 + fc, :]).astype(BF16)
            gt = a_ref[:, c0:c0 + fc].astype(F32)
            ub = a_ref[:, F + c0:F + c0 + fc]
            sg = _sigmoid(gt)
            sl = gt * sg
            dsl = (sg + sl * (1.0 - sg)).astype(BF16)
            da_ref[:, c0:c0 + fc] = ds * ub * dsl
            da_ref[:, F + c0:F + c0 + fc] = ds * sl.astype(BF16)

    row = lambda w: pl.BlockSpec((tm, w), lambda i: (i, 0))
    return pl.pallas_call(
        body, name=name, grid=(T // tm,),
        in_specs=[row(D), row(D), pl.BlockSpec((1, D), lambda i: (0, 0)), row(F2),
                  _resident((F, D))],
        out_specs=[row(F2), row(D), pl.BlockSpec((1, D), lambda i: (0, 0))],
        out_shape=[SDS((T, F2), BF16), SDS((T, D), BF16), SDS((1, D), F32)],
        compiler_params=_params("arbitrary"),
    )(dy, o, g_post, a, w2)


def _dw_call(body, name, grid, in_specs, args, block, pack, row_block, sem):
    if pack is None:
        out_spec = pl.BlockSpec(block, lambda *_: (0, 0, 0), pipeline_mode=pl.Buffered(1))
        return pl.pallas_call(body, name=name, grid=grid, in_specs=in_specs, out_specs=out_spec,
                              out_shape=SDS(block, F32), compiler_params=_params(*sem))(*args)
    assert pack.shape[0] == block[0] and pack.shape[2] == block[2]
    out_spec = pl.BlockSpec(block, lambda *_: (0, row_block, 0), pipeline_mode=pl.Buffered(1))
    return pl.pallas_call(body, name=name, grid=grid, in_specs=in_specs + [ANY], out_specs=out_spec,
                          out_shape=SDS(pack.shape, F32), input_output_aliases={len(args): 0},
                          compiler_params=_params(*sem))(*args, pack)


def _dw_cols(a, b, n_chunks, pack, row_block, *, name, ride=()):
    T, K = a.shape
    tn = b.shape[1] // n_chunks
    tt = _tile(T, TT_REDUCE, SUBLANES_BF16)
    per = 2 if n_chunks % 2 == 0 else 1
    grid = (n_chunks // per, T // tt)
    kind, riders = ride if ride else (None, ())
    nr = len(riders)

    def body(a_ref, b_ref, *rest):
        o_ref = rest[1 + nr] if nr else rest[-1]
        j, t = pl.program_id(0), pl.program_id(1)
        if nr:
            rides = RIDES[kind][0](rest[1:1 + nr], rest[2 + nr:2 + 2 * nr], rest[-2], rest[-1])

            @pl.when((j == 0) & (t == 0))
            def _():
                for cp in rides:
                    cp.start()

        @pl.when(t == 0)
        def _():
            o_ref[...] = jnp.zeros_like(o_ref)

        for p in range(per):
            o_ref[p] += _dot_tn(a_ref[...], b_ref[:, p * tn:(p + 1) * tn])

        if nr:
            @pl.when((j == grid[0] - 1) & (t == grid[1] - 1))
            def _():
                for cp in rides:
                    cp.wait()

    in_specs = [pl.BlockSpec((tt, K), lambda j, t: (t, 0)), pl.BlockSpec((tt, per * tn), lambda j, t: (t, j))]
    sem = ("arbitrary", "arbitrary") if nr else ("parallel", "arbitrary")
    if pack is None:
        assert not nr
        return pl.pallas_call(body, name=name, grid=grid, in_specs=in_specs,
                              out_specs=pl.BlockSpec((per, K, tn), lambda j, t: (j, 0, 0)),
                              out_shape=SDS((n_chunks, K, tn), F32), compiler_params=_params(*sem))(a, b)
    assert pack.shape[0] == n_chunks and pack.shape[2] == tn
    out = pl.pallas_call(
        body, name=name, grid=grid, in_specs=in_specs + [ANY] * (1 + nr),
        out_specs=[pl.BlockSpec((per, K, tn), lambda j, t: (j, row_block, 0))] + [ANY] * nr,
        out_shape=[SDS(pack.shape, F32)] + (RIDES[kind][1](riders) if nr else []),
        input_output_aliases={2: 0},
        scratch_shapes=[pltpu.SemaphoreType.DMA((RIDES[kind][2] * nr,))] * 2 if nr else [],
        compiler_params=_params(*sem))(a, b, pack, *riders)
    return out if nr else out[0]


def _dw_rows(a, b, pack, row_block, *, name, a_feature_major=False, swiglu=False):
    K, T = a.shape if a_feature_major else a.shape[::-1]
    K = K // 2 if swiglu else K
    N = b.shape[1]
    r = K // N_CHIPS
    cw = r if r % LANES == 0 else 2 * r
    assert cw % LANES == 0 and K % cw == 0 and r % 8 == 0
    tt = _tile(T, TT_REDUCE // 2 if swiglu else TT_REDUCE, LANES)

    def body(a_ref, b_ref, *rest):
        o_ref = rest[-1]

        @pl.when(pl.program_id(0) == 0)
        def _():
            o_ref[...] = jnp.zeros_like(o_ref)

        for c in range(K // cw):
            if a_feature_major:
                part = _dot(a_ref[c * cw:(c + 1) * cw, :], b_ref[...])
            elif swiglu:
                gt = a_ref[:, c * cw:(c + 1) * cw].astype(F32)
                part = _dot_tn((gt * _sigmoid(gt)).astype(BF16) * a_ref[:, K + c * cw:K + (c + 1) * cw], b_ref[...])
            else:
                part = _dot_tn(a_ref[:, c * cw:(c + 1) * cw], b_ref[...])
            for p in range(cw // r):
                o_ref[c * (cw // r) + p] += part[p * r:(p + 1) * r]

    a_spec = (pl.BlockSpec((K, tt), lambda t: (0, t)) if a_feature_major
              else pl.BlockSpec((tt, a.shape[1]), lambda t: (t, 0)))
    return _dw_call(body, name, (T // tt,), [a_spec, pl.BlockSpec((tt, N), lambda t: (t, 0))],
                    [a, b], (N_CHIPS, r, N), pack, row_block, ("arbitrary",))


def _matmul_nt_norm_bwd(da, w, w_block, x, g, dy, init, *, name, ride=None):
    T, N = da.shape
    D = x.shape[1]
    nj, _, tn = w.shape
    tm = _tile(T, TM_ROW, SUBLANES_BF16)
    ni = T // tm
    has_init = init is not None
    kind, riders = ride if ride else (None, ())
    nr = len(riders)
    n_in = 5 + has_init

    def body(*refs):
        da_ref, w_ref, x_ref, g_ref, dy_ref = refs[:5]
        dx_ref, dg_ref = refs[n_in + nr:n_in + nr + 2]
        i = pl.program_id(0)
        if nr:
            rides = RIDES[kind][0](refs[n_in:n_in + nr], refs[n_in + nr + 2:n_in + 2 * nr + 2], refs[-2], refs[-1])

            @pl.when(i == 0)
            def _():
                for cp in rides:
                    cp.start()

        @pl.when(i == 0)
        def _():
            dg_ref[...] = jnp.zeros_like(dg_ref)

        dh = refs[5][...] if has_init else None
        for j in range(nj):
            part = _dot_nt(da_ref[:, j * tn:(j + 1) * tn], w_ref[j])
            dh = part if dh is None else dh + part
        dx, dg = _rms_bwd(x_ref[...], g_ref[...], dh)
        dx_ref[...] = dy_ref[...] + dx
        dg_ref[...] += dg

        if nr:
            @pl.when(i == ni - 1)
            def _():
                for cp in rides:
                    cp.wait()

    row = pl.BlockSpec((tm, D), lambda i: (i, 0))
    vec = pl.BlockSpec((1, D), lambda i: (0, 0))
    in_specs = [pl.BlockSpec((tm, N), lambda i: (i, 0)), _resident((nj, D, tn), (0, w_block, 0)), row, vec, row]
    args = [da, w, x, g, dy]
    if has_init:
        in_specs.append(row)
        args.append(init)
    return pl.pallas_call(
        body, name=name, grid=(ni,), in_specs=in_specs + [ANY] * nr,
        out_specs=[row, vec] + [ANY] * nr,
        out_shape=[SDS((T, D), F32), SDS((1, D), F32)] + (RIDES[kind][1](riders) if nr else []),
        scratch_shapes=[pltpu.SemaphoreType.DMA((RIDES[kind][2] * nr,))] * 2 if nr else [],
        compiler_params=_params("arbitrary"),
    )(*args, *riders)


GROUP_LANES = Q_PER_KV * ATTN_BLOCK


def _attn_mask_t(first):
    kj = lax.broadcasted_iota(jnp.int32, (2 * ATTN_BLOCK, ATTN_BLOCK), 0)
    qi = lax.broadcasted_iota(jnp.int32, (2 * ATTN_BLOCK, ATTN_BLOCK), 1)
    rel = qi + ATTN_BLOCK - kj
    band = (rel >= 0) & (rel < ATTN_BLOCK)
    if first is False:
        return band
    return band & ((kj >= ATTN_BLOCK) | jnp.logical_not(first))


def _attn_probs_t(st, valid, sink):
    s = jnp.where(valid, st, MASK_VALUE)
    m = jnp.maximum(jnp.max(s, axis=0, keepdims=True), sink)
    p = jnp.exp(s - m)
    es = jnp.exp(sink - m)
    inv = 1.0 / (jnp.sum(p, axis=0, keepdims=True) + es)
    return p * inv, es * inv


def _attn_specs(tq, tile_of):
    nb = tq // ATTN_BLOCK
    krow, vrow = ATTN_WIDTH // KV_WIDTH, ATTN_WIDTH // KV_WIDTH + 1
    halo = lambda r: pl.BlockSpec((KV_WIDTH, ATTN_BLOCK), lambda t: (r, jnp.maximum(tile_of(t) * nb - 1, 0)))
    return [pl.BlockSpec((ATTN_WIDTH, tq), lambda t: (0, tile_of(t))),
            pl.BlockSpec((KV_WIDTH, tq), lambda t: (krow, tile_of(t))),
            pl.BlockSpec((KV_WIDTH, tq), lambda t: (vrow, tile_of(t))),
            halo(krow), halo(vrow)]


def _head_rows(g, r):
    h = g * Q_PER_KV + r
    return h, slice(h * HEAD_DIM, (h + 1) * HEAD_DIM)


def _group_stack(ref, g, cols):
    return jnp.concatenate([ref[_head_rows(g, r)[1], cols] for r in range(Q_PER_KV)], axis=1)


def _attn_fwd(zt, sinks, *, name):
    T = zt.shape[1]
    tq = _tile(T, TQ_ATTN, ATTN_BLOCK)
    nb = tq // ATTN_BLOCK

    def body(q_ref, k_ref, v_ref, kh_ref, vh_ref, s_ref, o_ref, kf, vf, pt):
        kf[:, 0:ATTN_BLOCK] = kh_ref[...]
        kf[:, ATTN_BLOCK:] = k_ref[...]
        vf[:, 0:ATTN_BLOCK] = vh_ref[...]
        vf[:, ATTN_BLOCK:] = v_ref[...]
        for b in range(nb):
            cols = slice(b * ATTN_BLOCK, (b + 1) * ATTN_BLOCK)
            win = slice(b * ATTN_BLOCK, (b + 2) * ATTN_BLOCK)
            valid = _attn_mask_t((pl.program_id(0) == 0) if b == 0 else False)
            for g in range(N_KV_HEADS):
                gr = slice(g * HEAD_DIM, (g + 1) * HEAD_DIM)
                st = _dot_tn(kf[gr, win], _group_stack(q_ref, g, cols) * ATTN_SCALE)
                for r in range(Q_PER_KV):
                    h, _ = _head_rows(g, r)
                    sl = slice(r * ATTN_BLOCK, (r + 1) * ATTN_BLOCK)
                    probs, _ = _attn_probs_t(st[:, sl], valid, s_ref[h])
                    pt[:, sl] = probs.astype(BF16)
                ot = _dot(vf[gr, win], pt[...])
                for r in range(Q_PER_KV):
                    o_ref[_head_rows(g, r)[1], cols] = ot[:, r * ATTN_BLOCK:(r + 1) * ATTN_BLOCK].astype(BF16)

    return pl.pallas_call(
        body, name=name, grid=(T // tq,),
        in_specs=_attn_specs(tq, lambda t: t) + [pl.BlockSpec(memory_space=pltpu.SMEM)],
        out_specs=pl.BlockSpec((ATTN_WIDTH, tq), lambda t: (0, t)),
        out_shape=SDS((ATTN_WIDTH, T), BF16),
        scratch_shapes=[pltpu.VMEM((KV_WIDTH, tq + ATTN_BLOCK), BF16)] * 2
        + [pltpu.VMEM((2 * ATTN_BLOCK, GROUP_LANES), BF16)],
        compiler_params=_params("parallel"),
    )(zt, zt, zt, zt, zt, sinks)


def _attn_bwd(zt, sinks, dot_, *, name):
    T = zt.shape[1]
    tq = _tile(T, TQ_ATTN, ATTN_BLOCK)
    nb = tq // ATTN_BLOCK
    nt = T // tq
    tile_of = lambda t: nt - 1 - t

    def body(q_ref, k_ref, v_ref, kh_ref, vh_ref, do_ref, s_ref, dz_ref, dsink_ref, kf, vf, dkf, dvf, carry, pt, dst):
        t = pl.program_id(0)

        @pl.when(t == 0)
        def _():
            carry[...] = jnp.zeros_like(carry)
            dsink_ref[...] = jnp.zeros_like(dsink_ref)

        kf[:, 0:ATTN_BLOCK] = kh_ref[...]
        kf[:, ATTN_BLOCK:] = k_ref[...]
        vf[:, 0:ATTN_BLOCK] = vh_ref[...]
        vf[:, ATTN_BLOCK:] = v_ref[...]
        dkf[...] = jnp.zeros_like(dkf)
        dvf[...] = jnp.zeros_like(dvf)
        dkf[:, tq:] = carry[0:KV_WIDTH, :]
        dvf[:, tq:] = carry[KV_WIDTH:, :]
        lane = lax.broadcasted_iota(jnp.int32, (1, LANES), 1)
        dsink = jnp.zeros((1, LANES), F32)
        for b in range(nb):
            cols = slice(b * ATTN_BLOCK, (b + 1) * ATTN_BLOCK)
            win = slice(b * ATTN_BLOCK, (b + 2) * ATTN_BLOCK)
            valid = _attn_mask_t((t == nt - 1) if b == 0 else False)
            for g in range(N_KV_HEADS):
                gr = slice(g * HEAD_DIM, (g + 1) * HEAD_DIM)
                kt2, vt2 = kf[gr, win], vf[gr, win]
                qst = _group_stack(q_ref, g, cols) * ATTN_SCALE
                dost = _group_stack(do_ref, g, cols)
                st = _dot_tn(kt2, qst)
                dpt = _dot_tn(vt2, dost)
                for r in range(Q_PER_KV):
                    h, _ = _head_rows(g, r)
                    sl = slice(r * ATTN_BLOCK, (r + 1) * ATTN_BLOCK)
                    probs, psink = _attn_probs_t(st[:, sl], valid, s_ref[h])
                    dp = dpt[:, sl]
                    delta = jnp.sum(probs * dp, axis=0, keepdims=True)
                    pt[:, sl] = probs.astype(BF16)
                    dst[:, sl] = (probs * (dp - delta)).astype(BF16)
                    dsink = dsink + jnp.where(lane == h, -jnp.sum(psink * delta), 0.0)
                dqt = _dot(kt2, dst[...]) * ATTN_SCALE
                for r in range(Q_PER_KV):
                    dz_ref[_head_rows(g, r)[1], cols] = dqt[:, r * ATTN_BLOCK:(r + 1) * ATTN_BLOCK].astype(BF16)
                dkf[gr, win] += _dot_nt(qst, dst[...])
                dvf[gr, win] += _dot_nt(dost, pt[...])
        dz_ref[ATTN_WIDTH:ATTN_WIDTH + KV_WIDTH, :] = dkf[:, ATTN_BLOCK:].astype(BF16)
        dz_ref[ATTN_WIDTH + KV_WIDTH:, :] = dvf[:, ATTN_BLOCK:].astype(BF16)
        carry[0:KV_WIDTH, :] = dkf[:, 0:ATTN_BLOCK]
        carry[KV_WIDTH:, :] = dvf[:, 0:ATTN_BLOCK]
        dsink_ref[...] += dsink

    return pl.pallas_call(
        body, name=name, grid=(nt,),
        in_specs=_attn_specs(tq, tile_of) + [pl.BlockSpec((ATTN_WIDTH, tq), lambda t: (0, tile_of(t))),
                                             pl.BlockSpec(memory_space=pltpu.SMEM)],
        out_specs=[pl.BlockSpec((QKV_WIDTH, tq), lambda t: (0, tile_of(t))),
                   pl.BlockSpec((8, LANES), lambda t: (0, 0))],
        out_shape=[SDS((QKV_WIDTH, T), BF16), SDS((8, LANES), F32)],
        scratch_shapes=[pltpu.VMEM((KV_WIDTH, tq + ATTN_BLOCK), BF16)] * 2
        + [pltpu.VMEM((KV_WIDTH, tq + ATTN_BLOCK), F32)] * 2 + [pltpu.VMEM((2 * KV_WIDTH, ATTN_BLOCK), F32)]
        + [pltpu.VMEM((2 * ATTN_BLOCK, GROUP_LANES), BF16)] * 2,
        compiler_params=_params("arbitrary"),
    )(zt, zt, zt, zt, zt, dot_, sinks)


def _layer_norm_stats(v):
    mu = jnp.mean(v, axis=-1, keepdims=True)
    xc = v - mu
    rstd = lax.rsqrt(jnp.mean(xc * xc, axis=-1, keepdims=True) + LN_EPS)
    return rstd, xc * rstd


def _sgu_fwd(zmain, ln_g, ln_b, wm, bias, *, name):
    T = zmain.shape[0]
    ts = _tile(T, TS_SGU, SGU_CHUNK)

    def body(u_ref, v_ref, g_ref, b_ref, w_ref, bias_ref, y_ref):
        u = _gelu(u_ref[...].astype(F32))
        _, vh = _layer_norm_stats(_gelu(v_ref[...].astype(F32)))
        vn = (vh * g_ref[...] + b_ref[...]).astype(BF16)
        for ch in range(ts // SGU_CHUNK):
            rows = slice(ch * SGU_CHUNK, (ch + 1) * SGU_CHUNK)
            for g in range(SGU_GROUPS):
                cols = slice(g * 128, (g + 1) * 128)
                s = _dot(w_ref[g], vn[rows, cols]) + bias_ref[g]
                y_ref[rows, cols] = (u[rows, cols] * s).astype(BF16)

    full = _resident
    return pl.pallas_call(
        body, name=name, grid=(T // ts,),
        in_specs=[pl.BlockSpec((ts, SGU_WIDTH), lambda i: (i, 0)), pl.BlockSpec((ts, SGU_WIDTH), lambda i: (i, 1)),
                  full((1, SGU_WIDTH)), full((1, SGU_WIDTH)), full(wm.shape), full(bias.shape)],
        out_specs=pl.BlockSpec((ts, SGU_WIDTH), lambda i: (i, 0)),
        out_shape=SDS((T, SGU_WIDTH), BF16),
        compiler_params=_params("parallel"),
    )(zmain, zmain, ln_g, ln_b, wm, bias)


def _sgu_bwd(zmain, dzmain, dy, ln_g, ln_b, wm, wmt, bias, *, name):
    T = zmain.shape[0]
    ts = _tile(T, TS_SGU, SGU_CHUNK)

    def body(u_ref, v_ref, dy_ref, g_ref, b_ref, w_ref, wt_ref, bias_ref, _, dz_ref, dw_ref, db_ref, dlg_ref, dlb_ref,
             dvn):
        @pl.when(pl.program_id(0) == 0)
        def _():
            dw_ref[...] = jnp.zeros_like(dw_ref)
            db_ref[...] = jnp.zeros_like(db_ref)
            dlg_ref[...] = jnp.zeros_like(dlg_ref)
            dlb_ref[...] = jnp.zeros_like(dlb_ref)

        us = u_ref[...].astype(F32)
        vs = v_ref[...].astype(F32)
        u = _gelu(us)
        rstd, vh = _layer_norm_stats(_gelu(vs))
        vn = (vh * g_ref[...] + b_ref[...]).astype(BF16)
        causal = (lax.broadcasted_iota(jnp.int32, (SGU_CHUNK, SGU_CHUNK), 0)
                  >= lax.broadcasted_iota(jnp.int32, (SGU_CHUNK, SGU_CHUNK), 1))
        lane = lax.broadcasted_iota(jnp.int32, (SGU_CHUNK, LANES), 1)
        db = jnp.zeros((SGU_CHUNK, LANES), F32)
        for ch in range(ts // SGU_CHUNK):
            rows = slice(ch * SGU_CHUNK, (ch + 1) * SGU_CHUNK)
            for g in range(SGU_GROUPS):
                cols = slice(g * 128, (g + 1) * 128)
                vng = vn[rows, cols]
                s = _dot(w_ref[g], vng) + bias_ref[g]
                dyf = dy_ref[rows, cols].astype(F32)
                dz_ref[rows, cols] = (dyf * s * _gelu_grad(us[rows, cols])).astype(BF16)
                dsf = dyf * u[rows, cols]
                dsb = dsf.astype(BF16)
                dvn[rows, cols] = _dot(wt_ref[g], dsb)
                dw_ref[g] += jnp.where(causal, _dot_nt(dsb, vng), 0.0)
                db = db + jnp.where(lane == g, jnp.sum(dsf, axis=1, keepdims=True), 0.0)
        db_ref[...] += db
        dvnf = dvn[...]
        dlg_ref[...] += jnp.sum(dvnf * vh, axis=0, keepdims=True)
        dlb_ref[...] += jnp.sum(dvnf, axis=0, keepdims=True)
        dvh = dvnf * g_ref[...]
        dv = rstd * (dvh - jnp.mean(dvh, axis=-1, keepdims=True) - vh * jnp.mean(dvh * vh, axis=-1, keepdims=True))
        dz_ref[:, SGU_WIDTH:] = (dv * _gelu_grad(vs)).astype(BF16)

    full = _resident
    vec = full((1, SGU_WIDTH))
    acc = lambda shape: pl.BlockSpec(shape, lambda i: (0,) * len(shape))
    return pl.pallas_call(
        body, name=name, grid=(T // ts,),
        in_specs=[pl.BlockSpec((ts, SGU_WIDTH), lambda i: (i, 0)), pl.BlockSpec((ts, SGU_WIDTH), lambda i: (i, 1)),
                  pl.BlockSpec((ts, SGU_WIDTH), lambda i: (i, 0)), vec, vec, full(wm.shape), full(wm.shape),
                  full(bias.shape), pl.BlockSpec(memory_space=pl.ANY)],
        out_specs=[pl.BlockSpec((ts, 2 * SGU_WIDTH), lambda i: (i, 0)), acc(wm.shape),
                   acc((SGU_CHUNK, LANES)), acc((1, SGU_WIDTH)), acc((1, SGU_WIDTH))],
        out_shape=[SDS(dzmain.shape, BF16), SDS(wm.shape, F32), SDS((SGU_CHUNK, LANES), F32),
                   SDS((1, SGU_WIDTH), F32), SDS((1, SGU_WIDTH), F32)],
        scratch_shapes=[pltpu.VMEM((ts, SGU_WIDTH), F32)],
        input_output_aliases={8: 0},
        compiler_params=_params("arbitrary"),
    )(zmain, zmain, dy, ln_g, ln_b, wm, wmt, bias, dzmain)


def _merge_fwd(y_attn_t, y_sgu, zmain, w_a, w_s, w_o, x, g_post, *, name):
    T, D = x.shape
    tm = _tile(T, TM_ROW, LANES)

    def body(ya_ref, ys_ref, ga_ref, gb_ref, wa_ref, ws_ref, wo_ref, x_ref, g_ref, xn_ref, o_ref):
        pa = _dot_tn(ya_ref[...], wa_ref[...])
        ps = _dot(ys_ref[...], ws_ref[...])
        merged = _sigmoid(ga_ref[...].astype(F32)) * pa + _sigmoid(gb_ref[...].astype(F32)) * ps
        out = _dot(merged.astype(BF16), wo_ref[...])
        o_ref[...] = out.astype(BF16)
        _, oh = _rms_stats(out)
        xn_ref[...] = x_ref[...] + oh * g_ref[...]

    row = lambda col: pl.BlockSpec((tm, D), lambda i: (i, col))
    wfull = _resident((D, D))
    return pl.pallas_call(
        body, name=name, grid=(T // tm,),
        in_specs=[pl.BlockSpec((D, tm), lambda i: (0, i)), row(0), row(2), row(3), wfull, wfull, wfull, row(0),
                  pl.BlockSpec((1, D), lambda i: (0, 0))],
        out_specs=[row(0)] * 2,
        out_shape=[SDS((T, D), F32), SDS((T, D), BF16)],
        compiler_params=_params("parallel"),
    )(y_attn_t, y_sgu, zmain, zmain, w_a, w_s, w_o, x, g_post)


def _merge_bwd(dy, out, g_post, pa, ps, zmain, w_a, w_s, w_o, *, name):
    T, D = dy.shape
    tm = _tile(T, TM_ROW, LANES)

    def body(dy_ref, o_ref, g_ref, pa_ref, ps_ref, ga_ref, gb_ref, wa_ref, ws_ref, wo_ref,
             dz_ref, dout_ref, mg_ref, dpa_ref, dps_ref, dya_ref, dys_ref, dg_ref):
        @pl.when(pl.program_id(0) == 0)
        def _():
            dg_ref[...] = jnp.zeros_like(dg_ref)

        dout, dg = _rms_bwd(o_ref[...].astype(F32), g_ref[...], dy_ref[...])
        dg_ref[...] += dg
        doutb = dout.astype(BF16)
        dout_ref[...] = doutb
        dm = _dot_nt(doutb, wo_ref[...]).astype(BF16)
        pa = _dot_tn(pa_ref[...], wa_ref[...]).astype(BF16)
        ps = _dot(ps_ref[...], ws_ref[...]).astype(BF16)
        sa = _sigmoid(ga_ref[...].astype(F32))
        sb = _sigmoid(gb_ref[...].astype(F32))
        one_minus_sa, one_minus_sb = (1.0 - sa).astype(BF16), (1.0 - sb).astype(BF16)
        sa, sb = sa.astype(BF16), sb.astype(BF16)
        mg_ref[...] = sa * pa + sb * ps
        dpa = dm * sa
        dps = dm * sb
        dpa_ref[...] = dpa
        dps_ref[...] = dps
        dz_ref[:, 0:D] = dpa * pa * one_minus_sa
        dz_ref[:, D:] = dps * ps * one_minus_sb
        dya_ref[...] = _dot_nt(wa_ref[...], dpa).astype(BF16)
        dys_ref[...] = _dot_nt(dps, ws_ref[...]).astype(BF16)

    row = lambda col: pl.BlockSpec((tm, D), lambda i: (i, col))
    wfull = _resident((D, D))
    vec = pl.BlockSpec((1, D), lambda i: (0, 0))
    act = SDS((T, D), BF16)
    return pl.pallas_call(
        body, name=name, grid=(T // tm,),
        in_specs=[row(0), row(0), vec, pl.BlockSpec((D, tm), lambda i: (0, i)), row(0), row(2), row(3), wfull, wfull, wfull],
        out_specs=[pl.BlockSpec((tm, 2 * D), lambda i: (i, 1))] + [row(0)] * 4
        + [pl.BlockSpec((D, tm), lambda i: (0, i)), row(0), vec],
        out_shape=[SDS(zmain.shape, BF16)] + [act] * 4 + [SDS((D, T), BF16), act, SDS((1, D), F32)],
        compiler_params=_params("arbitrary"),
    )(dy, out, g_post, pa, ps, zmain, zmain, w_a, w_s, w_o)


def _loss_head(y, target, *, name):
    T, D = y.shape
    tm = _tile(T, TM_ROW, 8)

    def body(y_ref, t_ref, dy_ref, l_ref):
        @pl.when(pl.program_id(0) == 0)
        def _():
            l_ref[...] = jnp.zeros_like(l_ref)

        e = y_ref[...] - t_ref[...]
        dy_ref[...] = e * (1.0 / D)
        l_ref[...] += jnp.sum(jnp.mean(e * e, axis=-1, keepdims=True))

    row = pl.BlockSpec((tm, D), lambda i: (i, 0))
    return pl.pallas_call(
        body, name=name, grid=(T // tm,), in_specs=[row, row],
        out_specs=[row, pl.BlockSpec((8, LANES), lambda i: (0, 0))],
        out_shape=[SDS((T, D), F32), SDS((8, LANES), F32)],
        compiler_params=_params("arbitrary"),
    )(y, target)


def _adamw(w, g, m, v, *, name):
    shape = w.shape
    cols = shape[-1]
    rows = w.size // cols
    w2, g2, m2, v2 = (t.reshape(rows, cols) for t in (w, g, m, v))
    tr = _tile(rows, max(8, (256 * 1024) // cols // 8 * 8), 8)

    def body(w_ref, g_ref, m_ref, v_ref, d_ref, nm_ref, nv_ref):
        gg = g_ref[...]
        nm = ADAM_B1 * m_ref[...] + (1.0 - ADAM_B1) * gg
        nv = ADAM_B2 * v_ref[...] + (1.0 - ADAM_B2) * (gg * gg)
        m_hat = nm / (1.0 - ADAM_B1 ** ADAM_STEP)
        v_hat = nv / (1.0 - ADAM_B2 ** ADAM_STEP)
        d_ref[...] = -ADAM_LR * (m_hat / (jnp.sqrt(v_hat) + ADAM_EPS) + ADAM_WD * w_ref[...])
        nm_ref[...] = nm
        nv_ref[...] = nv

    blk = pl.BlockSpec((tr, cols), lambda i: (i, 0))
    outs = pl.pallas_call(
        body, name=name, grid=(rows // tr,), in_specs=[blk] * 4, out_specs=[blk] * 3,
        out_shape=[SDS((rows, cols), F32)] * 3, compiler_params=_params("parallel"),
    )(w2, g2, m2, v2)
    return tuple(o.reshape(shape) for o in outs)


def _sum_terms(terms, n_rows, n_lead, dtypes, *, name, out_lead=None, out_n_lead=None):
    cols = terms[0][0].shape[-1]
    tr = _tile(n_rows, 704 if len(terms) <= 4 else 256, SUBLANES_BF16)
    nblk = n_rows // tr
    n_out = len(dtypes)

    def body(*refs):
        acc = refs[0][...].astype(F32)
        for r in refs[1:-n_out]:
            acc = acc + r[...].astype(F32)
        for o_ref in refs[-n_out:]:
            o_ref[...] = acc.astype(o_ref.dtype)

    def spec(lead, first):
        return pl.BlockSpec((1, tr, cols), lambda a, i: (lead(a), first(a) * nblk + i, 0))

    out = pl.BlockSpec((1, tr, cols), lambda a, i: (a if out_lead is None else out_lead(a), i, 0))
    return pl.pallas_call(
        body, name=name, grid=(n_lead, nblk), in_specs=[spec(lead, first) for _, lead, first in terms],
        out_specs=[out] * n_out, out_shape=[SDS((out_n_lead or n_lead, n_rows, cols), d) for d in dtypes],
        compiler_params=_params("arbitrary", "arbitrary"),
    )(*[a for a, _, _ in terms])


def _position():
    x, y, c = (lax.axis_index(a) for a in AXES)
    chips = [(1 - x, y), (x, 1 - y), (1 - x, 1 - y)]
    return x, y, c, chips


ANY = pl.BlockSpec(memory_space=pl.ANY)


def _remote(src, dst, send_sems, recv_sems, k, to):
    return pltpu.make_async_remote_copy(src_ref=src, dst_ref=dst, send_sem=send_sems.at[k], recv_sem=recv_sems.at[k],
                                        device_id=to, device_id_type=MESH)


def _comm_call(body, arrays, out_shapes, n_sems, *, name):
    n = len(arrays)

    def wrapped(*refs):
        body(refs[:n], refs[n:n + len(out_shapes)], refs[-2], refs[-1])

    return pl.pallas_call(
        wrapped, name=name, in_specs=[ANY] * n, out_specs=[ANY] * len(out_shapes), out_shape=out_shapes,
        scratch_shapes=[pltpu.SemaphoreType.DMA((n_sems,)), pltpu.SemaphoreType.DMA((n_sems,))],
    )(*arrays)


def _gather_shards(packs, *, name):
    def body(p_refs, o_refs, send_sems, recv_sems):
        for stage in _gather_stages(p_refs, o_refs, send_sems, recv_sems):
            stage()

    for p in packs:
        assert p.shape[0] % (4 * SUBLANES_BF16) == 0
    return _comm_call(body, packs, _gathered_shapes(packs), GATHER_SEMS * len(packs), name=name)


GATHER_SEMS = 8


def _gathered_shapes(packs):
    return [SDS((N_CHIPS,) + p.shape, p.dtype) for p in packs]


def _gather_stages(p_refs, o_refs, send_sems, recv_sems):
    NS = GATHER_SEMS
    x, y, c, _ = _position()
    me, sib = (x, y, c), (x, y, 1 - c)
    xn, yn = (1 - x, y, c), (x, 1 - y, c)
    s_me, s_xn, s_yn, s_dg = 2 * x + y, 2 * (1 - x) + y, 2 * x + 1 - y, 2 * (1 - x) + 1 - y

    def copy(a, k):
        p_ref, o_ref = p_refs[a], o_refs[a]
        rh = p_ref.shape[0] // 2
        rq = rh // 2
        half, q0, q1 = pl.ds(c * rh, rh), pl.ds(c * rh, rq), pl.ds(c * rh + rq, rq)
        src, dst, to = [(p_ref.at[half], o_ref.at[s_me, half], xn), (p_ref.at[half], o_ref.at[s_me, half], yn),
                        (o_ref.at[s_xn, q0],) * 2 + (yn,), (o_ref.at[s_yn, q1],) * 2 + (xn,),
                        (o_ref.at[s_xn, half],) * 2 + (sib,), (o_ref.at[s_yn, half],) * 2 + (sib,),
                        (o_ref.at[s_dg, q0],) * 2 + (sib,), (o_ref.at[s_dg, q1],) * 2 + (sib,)][k]
        return _remote(src, dst, send_sems, recv_sems, NS * a + k, to)

    def landed(a, k):
        o_ref = o_refs[a]
        rh = o_ref.shape[1] // 2
        rq = rh // 2
        o = (1 - c) * rh
        dst = [o_ref.at[s_xn, pl.ds(c * rh, rh)], o_ref.at[s_yn, pl.ds(c * rh, rh)],
               o_ref.at[s_dg, pl.ds(c * rh, rq)], o_ref.at[s_dg, pl.ds(c * rh + rq, rq)],
               o_ref.at[s_xn, pl.ds(o, rh)], o_ref.at[s_yn, pl.ds(o, rh)],
               o_ref.at[s_dg, pl.ds(o, rq)], o_ref.at[s_dg, pl.ds(o + rq, rq)]][k]
        _remote(dst, dst, send_sems, recv_sems, NS * a + k, me).wait_recv()

    n = len(p_refs)

    def stage_a():
        for a in range(n):
            for k in (0, 1):
                copy(a, k).start()

    def stage_b():
        for a in range(n):
            landed(a, 0)
            copy(a, 2).start()
            copy(a, 4).start()
            landed(a, 1)
            copy(a, 3).start()
            copy(a, 5).start()

    def stage_c():
        for a in range(n):
            landed(a, 2)
            copy(a, 6).start()
            landed(a, 3)
            copy(a, 7).start()
        for a in range(n):
            for k in (4, 5, 6, 7):
                landed(a, k)
        for a in range(n):
            for k in range(NS):
                copy(a, k).wait_send()

    return stage_a, stage_b, stage_c


def _sibling_exchange(gs, *, name):
    def body(g_refs, o_refs, send_sems, recv_sems):
        sent = _exchange_copies(g_refs, o_refs, send_sems, recv_sems)
        for cp in sent:
            cp.start()
        for cp in sent:
            cp.wait()

    return _comm_call(body, gs, _exchange_shapes(gs), len(gs), name=name)


def _exchange_shapes(gs):
    return [SDS((N_CHIPS, g.shape[1] // 2, g.shape[2]), g.dtype) for g in gs]


def _exchange_copies(g_refs, o_refs, send_sems, recv_sems):
    x, y, c, _ = _position()
    return [_remote(g_ref.at[:, pl.ds((1 - c) * o_ref.shape[1], o_ref.shape[1])], o_ref, send_sems, recv_sems, a,
                    (x, y, 1 - c)) for a, (g_ref, o_ref) in enumerate(zip(g_refs, o_refs))]


def _scatter_hop1(ps, *, name):
    def body(p_refs, o_refs, send_sems, recv_sems):
        sent = _hop1_copies(p_refs, o_refs, send_sems, recv_sems)
        for cp in sent:
            cp.start()
        for cp in sent:
            cp.wait()

    return _comm_call(body, ps, _hop1_shapes(ps), 4 * len(ps), name=name)


def _hop1_shapes(ps):
    return [SDS((4, p.shape[1] // 2, p.shape[2]), p.dtype) for p in ps]


def _hop1_copies(p_refs, o_refs, send_sems, recv_sems):
    x, y, c, _ = _position()
    xn, yn = (1 - x, y, c), (x, 1 - y, c)
    s_xn, s_yn, s_dg = 2 * (1 - x) + y, 2 * x + 1 - y, 2 * (1 - x) + 1 - y
    sent = []
    for a, (p_ref, o_ref) in enumerate(zip(p_refs, o_refs)):
        rq = o_ref.shape[1]
        first, second = pl.ds(0, rq), pl.ds(rq, rq)
        sent += [_remote(p_ref.at[s_xn, second], o_ref.at[0], send_sems, recv_sems, 4 * a, xn),
                 _remote(p_ref.at[s_dg, second], o_ref.at[1], send_sems, recv_sems, 4 * a + 1, xn),
                 _remote(p_ref.at[s_yn, first], o_ref.at[2], send_sems, recv_sems, 4 * a + 2, yn),
                 _remote(p_ref.at[s_dg, first], o_ref.at[3], send_sems, recv_sems, 4 * a + 3, yn)]
    return sent


RIDES = {"exchange": (_exchange_copies, _exchange_shapes, 1), "hop1": (_hop1_copies, _hop1_shapes, 4)}


def _scatter_hop2(fs, *, name):
    return _exchanges([("hop2", fs)], name=name)[0]


def _hop2_copies(f_refs, o_refs, send_sems, recv_sems):
    x, y, c, _ = _position()
    sent = []
    for a, (f_ref, o_ref) in enumerate(zip(f_refs, o_refs)):
        sent += [_remote(f_ref.at[0], o_ref.at[0], send_sems, recv_sems, 2 * a, (1 - x, y, c)),
                 _remote(f_ref.at[1], o_ref.at[1], send_sems, recv_sems, 2 * a + 1, (x, 1 - y, c))]
    return sent


RIDES["hop2"] = (_hop2_copies, lambda fs: [SDS(f.shape, f.dtype) for f in fs], 2)


class _SemaphoresFrom:
    def __init__(self, sems, first):
        self.sems, self.first, self.at = sems, first, self

    def __getitem__(self, k):
        return self.sems.at[self.first + k]


def _exchanges(parts, *, name):
    counts = [len(arrays) for _, arrays in parts]
    sems = [RIDES[kind][2] * n for (kind, _), n in zip(parts, counts)]
    n_in = sum(counts)

    def body(*refs):
        sent, a0, s0 = [], 0, 0
        for (kind, _), n, ns in zip(parts, counts, sems):
            sent += RIDES[kind][0](refs[a0:a0 + n], refs[n_in + a0:n_in + a0 + n],
                                   _SemaphoresFrom(refs[-2], s0), _SemaphoresFrom(refs[-1], s0))
            a0, s0 = a0 + n, s0 + ns
        for cp in sent:
            cp.start()
        for cp in sent:
            cp.wait()

    arrays = [a for _, arrs in parts for a in arrs]
    shapes = [s for kind, arrs in parts for s in RIDES[kind][1](arrs)]
    out = pl.pallas_call(
        body, name=name, in_specs=[ANY] * n_in, out_specs=[ANY] * n_in, out_shape=shapes,
        scratch_shapes=[pltpu.SemaphoreType.DMA((sum(sems),))] * 2,
    )(*arrays)
    return [out[sum(counts[:i]):sum(counts[:i + 1])] for i in range(len(parts))]


def _sibling_fill(rs, *, name):
    n = len(rs)

    def body(*refs):
        r_refs, send_sems, recv_sems = refs[n:2 * n], refs[-2], refs[-1]
        x, y, c, _ = _position()
        sent = []
        for a, r_ref in enumerate(r_refs):
            mine = r_ref.at[pl.ds(c * (r_ref.shape[0] // 2), r_ref.shape[0] // 2)]
            sent.append(_remote(mine, mine, send_sems, recv_sems, a, (x, y, 1 - c)))
            sent[-1].start()
        for a, r_ref in enumerate(r_refs):
            theirs = r_ref.at[pl.ds((1 - c) * (r_ref.shape[0] // 2), r_ref.shape[0] // 2)]
            _remote(theirs, theirs, send_sems, recv_sems, a, (x, y, c)).wait_recv()
        for cp in sent:
            cp.wait_send()

    return pl.pallas_call(
        body, name=name, in_specs=[ANY] * n, out_specs=[ANY] * n, out_shape=[SDS(r.shape, r.dtype) for r in rs],
        input_output_aliases={i: i for i in range(n)},
        scratch_shapes=[pltpu.SemaphoreType.DMA((n,)), pltpu.SemaphoreType.DMA((n,))],
    )(*rs)


def _gather_all(v, *, name):
    M, C = v.shape

    def body(v_ref, o_ref, send_sems, recv_sems):
        x, y, c, chips = _position()
        slot = lambda px, py, pc: o_ref.at[4 * px + 2 * py + pc]
        first = [_remote(v_ref, slot(x, y, c), send_sems, recv_sems, 0, (x, y, 1 - c))]
        first += [_remote(v_ref, slot(x, y, c), send_sems, recv_sems, 1 + j, (*chip, c)) for j, chip in enumerate(chips)]
        for cp in first:
            cp.start()
        passed = []
        for j, chip in enumerate(chips):
            landed = slot(*chip, c)
            _remote(landed, landed, send_sems, recv_sems, 1 + j, (x, y, c)).wait_recv()
            cp = _remote(landed, landed, send_sems, recv_sems, 4 + j, (x, y, 1 - c))
            cp.start()
            passed.append(cp)
        sib = slot(x, y, 1 - c)
        _remote(sib, sib, send_sems, recv_sems, 0, (x, y, c)).wait_recv()
        for j, chip in enumerate(chips):
            theirs = slot(*chip, 1 - c)
            _remote(theirs, theirs, send_sems, recv_sems, 4 + j, (x, y, c)).wait_recv()
        for cp in first + passed:
            cp.wait_send()

    return pl.pallas_call(
        body, name=name, in_specs=[ANY], out_specs=ANY, out_shape=SDS((N_DEV, M, C), v.dtype),
        scratch_shapes=[pltpu.SemaphoreType.DMA((7,)), pltpu.SemaphoreType.DMA((7,))],
    )(v)


BIG = ("ffn1_w1", "ffn2_w1", "w_in", "ffn1_w2", "ffn2_w2", "w_attn_branch", "w_sgu_branch", "w_out")
COL_SHARDED = ("ffn1_w1", "w_in", "ffn2_w1")
FFN_IN = ("ffn1_w1", "ffn2_w1")
SMALL = ("ffn1_pre_g", "ffn1_post_g", "mix_pre_g", "attn_sinks", "sgu_ln_g", "sgu_ln_b", "sgu_w", "sgu_b",
         "mix_post_g", "ffn2_pre_g", "ffn2_post_g")
WEIGHTS = ("ffn1_pre_g", "ffn1_w1", "ffn1_w2", "ffn1_post_g", "mix_pre_g", "w_in", "attn_sinks", "sgu_ln_g",
           "sgu_ln_b", "sgu_w", "sgu_b", "w_attn_branch", "w_sgu_branch", "w_out", "mix_post_g", "ffn2_pre_g",
           "ffn2_w1", "ffn2_w2", "ffn2_post_g")


def _column_chunks(w, tn):
    return jnp.swapaxes(w.reshape(w.shape[0], w.shape[1] // tn, tn), 0, 1)


def _width_classes(shard_shapes):
    widths = sorted({shard_shapes[n][-1] for n in BIG}, reverse=True)
    return [[n for n in BIG if shard_shapes[n][-1] == w] for w in widths]


def _class_rows(classes, shard_shapes, n_layers, aligned=BIG):
    where = {}
    for k, names in enumerate(classes):
        off = 0
        for n in names:
            r = shard_shapes[n][0]
            assert off % r == 0 or n not in aligned
            where[n] = (k, off, r)
            off += n_layers * r
    return where


def _ffn_fwd(x, pre_g, w1, w1_block, w2, post_g, tag, gather=()):
    a, h, *gathered = _norm_matmul(x, pre_g, w1, w1_block, name=f"{tag}_up", with_h=True, gather=gather)
    xn, o = _swiglu_out(a, w2, x, post_g, name=f"{tag}_down")
    return xn, (x, h, a, o), gathered


def _ffn_bwd(dy, saved, pre_g, w1, w1_block, w2, post_g, dw1_into, dw2_into, tag, after_dw2=None):
    x, h, a, o = saved
    da, do, d_post = _ffn_bwd_hidden(dy, o, post_g, a, w2, name=f"{tag}_bwd_hidden")
    g2 = _dw_rows(a, do, *dw2_into, name=f"{tag}_dw2", swiglu=True)
    ride_in, next_ride = after_dw2(g2) if after_dw2 else (None, None)
    dx, d_pre, *rode = _matmul_nt_norm_bwd(da, w1, w1_block, x, pre_g, dy, None, name=f"{tag}_bwd_in", ride=ride_in)
    if after_dw2 is None:
        return dx, _dw_cols(h, da, N_CHIPS, *dw1_into, name=f"{tag}_dw1"), g2, d_pre, d_post, ()
    g1, *rode = _dw_cols(h, da, N_CHIPS, *dw1_into, name=f"{tag}_dw1", ride=next_ride(rode))
    return dx, g1, g2, d_pre, d_post, rode


def kernel(x, ffn1_pre_g, ffn1_w1, ffn1_w2, ffn1_post_g, mix_pre_g, w_in, attn_sinks, sgu_ln_g, sgu_ln_b, sgu_w, sgu_b, w_attn_branch, w_sgu_branch, w_out, mix_post_g, ffn2_pre_g, ffn2_w1, ffn2_w2, ffn2_post_g, loss_target, m_ffn1_pre_g, m_ffn1_w1, m_ffn1_w2, m_ffn1_post_g, m_mix_pre_g, m_w_in, m_attn_sinks, m_sgu_ln_g, m_sgu_ln_b, m_sgu_w, m_sgu_b, m_w_attn_branch, m_w_sgu_branch, m_w_out, m_mix_post_g, m_ffn2_pre_g, m_ffn2_w1, m_ffn2_w2, m_ffn2_post_g, v_ffn1_pre_g, v_ffn1_w1, v_ffn1_w2, v_ffn1_post_g, v_mix_pre_g, v_w_in, v_attn_sinks, v_sgu_ln_g, v_sgu_ln_b, v_sgu_w, v_sgu_b, v_w_attn_branch, v_w_sgu_branch, v_w_out, v_mix_post_g, v_ffn2_pre_g, v_ffn2_w1, v_ffn2_w2, v_ffn2_post_g):
    given = dict(locals())
    W = {n: given[n] for n in WEIGHTS}
    M = {n: given["m_" + n] for n in WEIGHTS}
    V = {n: given["v_" + n] for n in WEIGHTS}
    L = ffn1_w1.shape[0]
    T, D = x.shape[1], x.shape[2]
    xt = x.reshape(T, D)
    target = loss_target.reshape(T, D)
    assert L % 2 == 0 and D == ATTN_WIDTH == SGU_WIDTH and T % ATTN_BLOCK == 0

    shard_shapes = {n: W[n].shape[1:] for n in BIG}
    classes = _width_classes(shard_shapes)
    my_chip = 2 * lax.axis_index("x") + lax.axis_index("y")
    my_core = lax.axis_index("c")
    where = _class_rows(classes, shard_shapes, L)
    where_w = _class_rows(classes, shard_shapes, 1, aligned=FFN_IN)
    packs = [[jnp.concatenate([W[n][l].astype(BF16) for n in names], axis=0) for names in classes] for l in range(L)]

    def block_of(n, l):
        k, off, r = where[n]
        return k, off // r + l

    def layer_weights(l, got):
        wc = [lax.dynamic_update_slice(g, p[None], (my_chip, 0, 0)) for g, p in zip(got, packs[l])]

        def chip_shards(n):
            k, off, r = where_w[n]
            return wc[k][:, off:off + r, :]

        fw = {n: chip_shards(n).reshape(-1, D) for n in BIG if n not in COL_SHARDED}
        w_in_l = jnp.swapaxes(chip_shards("w_in"), 0, 1).reshape(D, -1)
        fw["w_qkv_t"] = w_in_l[:, :QKV_WIDTH].T
        fw["w_main"] = _column_chunks(w_in_l[:, QKV_WIDTH:], D)
        for n in FFN_IN:
            fw[n] = (wc[where_w[n][0]], where_w[n][1] // where_w[n][2])
        return fw

    full = [None] * L
    full[0] = layer_weights(0, _gather_shards(packs[0], name="gather_weights_l0"))

    row = lambda name, l: W[name][l].reshape(1, -1)
    causal = jnp.tril(jnp.ones((SGU_CHUNK, SGU_CHUNK), dtype=bool))
    saved = []
    h_cur = xt
    for l in range(L):
        fw = full[l]
        sv = {}
        h_cur, sv["ffn1"], got = _ffn_fwd(h_cur, row("ffn1_pre_g", l), *fw["ffn1_w1"], fw["ffn1_w2"],
                                          row("ffn1_post_g", l), f"l{l}_ffn1", gather=packs[l + 1] if l + 1 < L else ())
        if l + 1 < L:
            full[l + 1] = layer_weights(l + 1, got)
        zqkv, hm = _norm_matmul_t(h_cur, row("mix_pre_g", l), fw["w_qkv_t"], name=f"l{l}_mix_in_qkv")
        zmain, = _norm_matmul(h_cur, row("mix_pre_g", l), fw["w_main"], 0, name=f"l{l}_mix_in_main", with_h=False)
        wm = jnp.where(causal[None], sgu_w[l], 0.0).astype(BF16)
        wmt = jnp.swapaxes(wm, 1, 2)
        bias = jnp.broadcast_to(sgu_b[l][:, :, None], (SGU_GROUPS, SGU_CHUNK, 128)).astype(F32)
        y_attn = _attn_fwd(zqkv, attn_sinks[l], name=f"l{l}_attn")
        y_sgu = _sgu_fwd(zmain, row("sgu_ln_g", l), row("sgu_ln_b", l), wm, bias, name=f"l{l}_sgu")
        x_mix = h_cur
        pa, ps = y_attn, y_sgu
        h_cur, mo = _merge_fwd(y_attn, y_sgu, zmain, fw["w_attn_branch"], fw["w_sgu_branch"], fw["w_out"],
                                       x_mix, row("mix_post_g", l), name=f"l{l}_merge")
        sv["mix"] = (x_mix, hm, zqkv, zmain, y_attn, y_sgu, pa, ps, mo, wm, wmt, bias)
        h_cur, sv["ffn2"], _ = _ffn_fwd(h_cur, row("ffn2_pre_g", l), *fw["ffn2_w1"], fw["ffn2_w2"],
                                        row("ffn2_post_g", l), f"l{l}_ffn2")
        saved.append(sv)

    dy, lsum = _loss_head(h_cur, target, name="loss_head")
    loss = lax.psum(0.5 * lsum[0, 0], AXES)

    k_in = where["w_in"][0]
    assert classes[k_in] == ["w_in"]
    gcls = [None if k == k_in else lax.empty((N_CHIPS, L * p.shape[0], p.shape[1]), F32)
            for k, p in enumerate(packs[0])]
    dw_in = [None] * L
    small_grads = [None] * L

    def into(n, l):
        return gcls[block_of(n, l)[0]], block_of(n, l)[1]

    def ffn_bwd(dy, which, l, after_dw2=None):
        n1, n2 = f"{which}_w1", f"{which}_w2"
        dy, g1, g2, d_pre, d_post, rode = _ffn_bwd(
            dy, saved[l][which], row(f"{which}_pre_g", l), *full[l][n1], full[l][n2], row(f"{which}_post_g", l),
            into(n1, l), into(n2, l), f"l{l}_{which}", after_dw2)
        gcls[where[n1][0]], gcls[where[n2][0]] = g1, g2
        return dy, d_pre, d_post, rode

    zero = lambda a: 0
    own = lambda a: a
    core = lambda a: lax.axis_index("c")
    chip = lambda a: 2 * lax.axis_index("x") + lax.axis_index("y")
    chip_xn = lambda a: 2 * (1 - lax.axis_index("x")) + lax.axis_index("y")
    chip_yn = lambda a: 2 * lax.axis_index("x") + 1 - lax.axis_index("y")

    def pair_sum(k, g, fs):
        return _sum_terms([(g, own, core), (fs, own, zero)], fs.shape[1], N_CHIPS, (BF16,), name=f"grads_pair_sum{k}")[0]

    for l in reversed(range(L)):
        fw, sv = full[l], saved[l]
        gs = {}
        dy, gs["ffn2_pre_g"], gs["ffn2_post_g"], _ = ffn_bwd(dy, "ffn2", l)

        x_mix, hm, zqkv, zmain, y_attn, y_sgu, pa, ps, mo, wm, wmt, bias = sv["mix"]
        dzmain, dout, merged, dpa, dps, dya, dys, gs["mix_post_g"] = _merge_bwd(
            dy, mo, row("mix_post_g", l), pa, ps, zmain, fw["w_attn_branch"], fw["w_sgu_branch"], fw["w_out"],
            name=f"l{l}_merge_bwd")
        k_sq = where["w_out"][0]
        gcls[k_sq] = _dw_rows(merged, dout, *into("w_out", l), name=f"l{l}_dw_out")
        gcls[k_sq] = _dw_rows(y_attn, dpa, *into("w_attn_branch", l), name=f"l{l}_dw_attn", a_feature_major=True)
        gcls[k_sq] = _dw_rows(y_sgu, dps, *into("w_sgu_branch", l), name=f"l{l}_dw_sgu")
        dzqkv, dsink = _attn_bwd(zqkv, attn_sinks[l], dya, name=f"l{l}_attn_bwd")
        gs["attn_sinks"] = dsink[0, :N_Q_HEADS]
        dzmain, dsw, dsb, gs["sgu_ln_g"], gs["sgu_ln_b"] = _sgu_bwd(
            zmain, dzmain, dys, row("sgu_ln_g", l), row("sgu_ln_b", l), wm, wmt, bias, name=f"l{l}_sgu_bwd")
        gs["sgu_w"] = dsw
        gs["sgu_b"] = dsb[:, :SGU_GROUPS].T
        dh_qkv = _matmul_tn_rows(dzqkv, fw["w_qkv_t"], name=f"l{l}_mix_bwd_qkv")
        dy, gs["mix_pre_g"] = _matmul_nt_norm_bwd(dzmain, fw["w_main"], 0, x_mix, row("mix_pre_g", l), dy, dh_qkv,
                                                   name=f"l{l}_mix_bwd_in")
        dw_main = _dw_cols(hm, dzmain, zmain.shape[1] // D, None, 0, name=f"l{l}_dw_in_main")
        dw_in[l] = jnp.concatenate([_matmul_tokens(dzqkv, hm, name=f"l{l}_dw_in_qkv").T,
                                    jnp.swapaxes(dw_main, 0, 1).reshape(D, -1)], axis=1)

        if l > 0:
            dy, gs["ffn1_pre_g"], gs["ffn1_post_g"], _ = ffn_bwd(dy, "ffn1", l)
        else:
            w_in_width = shard_shapes["w_in"][1]
            gcls[k_in] = jnp.stack([jnp.concatenate([g[:, s * w_in_width:(s + 1) * w_in_width] for g in dw_in], axis=0)
                                    for s in range(N_CHIPS)])
            k_up, k_down = where["ffn1_w1"][0], where["ffn1_w2"][0]
            assert sorted((k_up, k_down, k_in)) == list(range(len(classes)))
            from_sibling, pairs, hop1 = ([None] * len(classes) for _ in range(3))

            def after_dw2(g_down):
                def hop1_ride(rode):
                    from_sibling[k_down], from_sibling[k_in] = rode
                    pairs[k_down] = pair_sum(k_down, g_down, rode[0])
                    pairs[k_in] = pair_sum(k_in, gcls[k_in], rode[1])
                    return "hop1", [pairs[k_down], pairs[k_in]]
                return ("exchange", [g_down, gcls[k_in]]), hop1_ride

            dy, gs["ffn1_pre_g"], gs["ffn1_post_g"], (hop1[k_down], hop1[k_in]) = ffn_bwd(dy, "ffn1", l, after_dw2)
        small_grads[l] = gs
    grad_x = dy.reshape(x.shape)

    grs = gcls

    def relay_sum(k):
        return _sum_terms([(pairs[k], lambda a: chip_xn(a) + a * (chip_yn(a) - chip_xn(a)), own),
                           (hop1[k], lambda a: 3 - 2 * a, zero)], hop1[k].shape[1], 2, (BF16,),
                          name=f"grads_relay_sum{k}")[0]

    hop2 = [None] * len(classes)
    (hop2[k_down], hop2[k_in]), (from_sibling[k_up],) = _exchanges(
        [("hop2", [relay_sum(k_down), relay_sum(k_in)]), ("exchange", [grs[k_up]])], name="grads_hop2_and_exchange")
    pairs[k_up] = pair_sum(k_up, grs[k_up], from_sibling[k_up])
    hop1[k_up], = _scatter_hop1([pairs[k_up]], name="grads_scatter_hop1")
    hop2[k_up], = _scatter_hop2([relay_sum(k_up)], name="grads_scatter_hop2")
    quarter = lambda a: 2 * core(a) + a
    halves = [_sum_terms([(g, chip, quarter), (fs, chip, own), (h1, lambda a: 2 - 2 * a, zero), (h2, own, zero)],
                         h1.shape[1], 2, (F32,), name=f"grads_chip_sum{k}", out_lead=quarter, out_n_lead=4)[0]
              .reshape(g.shape[1:]) for k, (g, fs, h1, h2) in enumerate(zip(grs, from_sibling, hop1, hop2))]
    reduced_all = _sibling_fill(halves, name="grads_sibling_fill")

    grads = {n: [None] * L for n in SMALL}
    for names, reduced in zip(classes, reduced_all):
        for n in names:
            _, off, r = where[n]
            grads[n] = reduced[off:off + L * r].reshape((L,) + shard_shapes[n])

    def small_rows(gs):
        parts = []
        for n in SMALL:
            flat = gs[n].reshape(-1)
            pad = (-flat.shape[0]) % D
            parts.append(jnp.pad(flat, (0, pad)).reshape(-1, D))
        return jnp.concatenate(parts, axis=0)

    spack = jnp.concatenate([small_rows(small_grads[l]) for l in range(L)], axis=0)
    n_small = spack.shape[0]
    pad_rows = (-n_small) % SUBLANES_BF16
    spack = jnp.pad(spack, ((0, pad_rows), (0, 0)))
    everyone = _gather_all(spack, name="small_grads_gather")
    is_me = (jnp.arange(N_DEV) == 2 * my_chip + my_core)[:, None, None]
    everyone = jnp.where(is_me, spack[None], everyone)
    ssum = _sum_terms([(everyone, (lambda a, d=d: d), zero) for d in range(N_DEV)], spack.shape[0], 1, (F32,),
                      name="small_grads_sum")[0][0]
    per_layer = n_small // L
    for l in range(L):
        r0 = l * per_layer
        for n in SMALL:
            shp = W[n].shape[1:]
            size = math.prod(shp)
            nr = -(-size // D)
            grads[n][l] = ssum[r0:r0 + nr].reshape(-1)[:size].reshape(shp)
            r0 += nr
    grads.update({n: jnp.stack(grads[n]) for n in SMALL})

    delta, new_m, new_v = {}, {}, {}
    for n in WEIGHTS:
        delta[n], new_m[n], new_v[n] = _adamw(W[n], grads[n], M[n], V[n], name=f"adamw_{n}")

    return (loss, grad_x, *[grads[n] for n in WEIGHTS], *[delta[n] for n in WEIGHTS],
            *[new_m[n] for n in WEIGHTS], *[new_v[n] for n in WEIGHTS])
```
